```python
import math
import jax, jax.numpy as jnp
from jax import lax
import numpy as np

D_MODEL = 2048
BATCH = 8
SEQ = 4096
DEPTH = 4

CHUNK = 64

D_MIX = 2 * D_MODEL
POOL_W = D_MIX // 2
SSD_W = D_MIX - POOL_W

POOL_WINDOWS = (2, 4, 8, 16)
POOL_GROUPS = len(POOL_WINDOWS)
POOL_GROUP_W = POOL_W // POOL_GROUPS
MAX_WINDOW = max(POOL_WINDOWS)

SSD_HEAD_DIM = 64
SSD_HEADS = SSD_W // SSD_HEAD_DIM
SSD_STATE = 128
SSD_GROUPS = 4
SSD_HEADS_PER_GROUP = SSD_HEADS // SSD_GROUPS
CONV_WIDTH = 4
CONV_DIM = SSD_W + 2 * SSD_GROUPS * SSD_STATE

D_IN_PROJ = 2 * POOL_W + 2 * SSD_W + 2 * SSD_GROUPS * SSD_STATE + SSD_HEADS

NORM_EPS = 1e-6

kernel_name = "hybrid_pool_ssd_sandwich_trunk"


def rms_norm(x, w):
    xf = x.astype(jnp.float32)
    var = jnp.mean(xf * xf, axis=-1, keepdims=True)
    return (xf * lax.rsqrt(var + NORM_EPS) * w.astype(jnp.float32)).astype(x.dtype)


def pool_mixer(u, mix_w, scale):
    b, l, _ = u.shape
    uf = u.astype(jnp.float32)
    cs = jnp.cumsum(uf, axis=1)
    csp = jnp.pad(cs, ((0, 0), (MAX_WINDOW, 0), (0, 0)))
    t = jnp.arange(l)
    outs = []
    for g, w in enumerate(POOL_WINDOWS):
        sl = slice(g * POOL_GROUP_W, (g + 1) * POOL_GROUP_W)
        win_sum = csp[:, MAX_WINDOW:, sl] - csp[:, MAX_WINDOW - w:MAX_WINDOW - w + l, sl]
        cnt = jnp.minimum(t + 1, w).astype(jnp.float32)[None, :, None]
        outs.append(win_sum / cnt - uf[:, :, sl])
    pooled = jnp.stack(outs, axis=2).astype(u.dtype)
    mixed = jnp.einsum('blgc,gcd->blgd', pooled, mix_w)
    return mixed.reshape(b, l, POOL_W) * scale


def causal_dwconv(u, w, bias):
    y = lax.conv_general_dilated(
        u, w[:, None, :], window_strides=(1,), padding=[(CONV_WIDTH - 1, 0)],
        dimension_numbers=('NWC', 'WIO', 'NWC'), feature_group_count=u.shape[-1])
    return y + bias


def ssd_scan(xh, dt, a, bm, cm):
    b, l, h, p = xh.shape
    c = l // CHUNK
    G, R, N, Q = SSD_GROUPS, SSD_HEADS_PER_GROUP, SSD_STATE, CHUNK
    xdt = (xh * dt[..., None]).reshape(b, c, Q, G, R, p)
    adt = (dt * a).reshape(b, c, Q, G, R).transpose(0, 3, 4, 1, 2)
    acs = jnp.cumsum(adt, axis=-1)
    bm = bm.reshape(b, c, Q, G, N)
    cm = cm.reshape(b, c, Q, G, N)
    mask = jnp.tril(jnp.ones((Q, Q), dtype=bool))
    seg = acs[..., :, None] - acs[..., None, :]
    decay = jnp.exp(jnp.where(mask, seg, -jnp.inf))
    scores = jnp.einsum('bclgn,bcsgn->bgcls', cm, bm)
    y_diag = jnp.einsum('bgcls,bgrcls,bcsgrp->bclgrp', scores, decay, xdt)
    decay_states = jnp.exp(acs[..., -1:] - acs)
    states = jnp.einsum('bcsgn,bgrcs,bcsgrp->bcgrpn', bm, decay_states, xdt)
    chunk_decay = jnp.exp(acs[..., -1])

    def step(carry, inp):
        st, dec = inp
        return carry * dec[..., None, None] + st, carry

    init = jnp.zeros((b, G, R, p, N), dtype=jnp.float32)
    _, prev = lax.scan(step, init, (jnp.moveaxis(states, 1, 0), jnp.moveaxis(chunk_decay, 3, 0)))
    prev = jnp.moveaxis(prev, 0, 1)
    y_off = jnp.einsum('bclgn,bcgrpn,bgrcl->bclgrp', cm, prev, jnp.exp(acs))
    return (y_diag + y_off).reshape(b, l, h, p)


def ssd_branch(z, xbc_raw, dt_raw, conv_w, conv_b, dt_bias, a_log, d_skip, norm_w):
    b, l, _ = z.shape
    xbc = jax.nn.silu(causal_dwconv(xbc_raw, conv_w, conv_b))
    xs = xbc[..., :SSD_W]
    bm = xbc[..., SSD_W:SSD_W + SSD_GROUPS * SSD_STATE].reshape(b, l, SSD_GROUPS, SSD_STATE)
    cm = xbc[..., SSD_W + SSD_GROUPS * SSD_STATE:].reshape(b, l, SSD_GROUPS, SSD_STATE)
    dt = jax.nn.softplus(dt_raw.astype(jnp.float32) + dt_bias.astype(jnp.float32))
    a = -jnp.exp(a_log.astype(jnp.float32))
    xh = xs.reshape(b, l, SSD_HEADS, SSD_HEAD_DIM).astype(jnp.float32)
    y = ssd_scan(xh, dt, a, bm.astype(jnp.float32), cm.astype(jnp.float32))
    y = y + d_skip.astype(jnp.float32)[:, None] * xh
    y = y.reshape(b, l, SSD_W) * jax.nn.silu(z.astype(jnp.float32))
    yg = y.reshape(b, l, SSD_GROUPS, SSD_W // SSD_GROUPS)
    yg = yg * lax.rsqrt(jnp.mean(yg * yg, axis=-1, keepdims=True) + NORM_EPS)
    y = yg.reshape(b, l, SSD_W) * norm_w.astype(jnp.float32)
    return y.astype(z.dtype)


def _fwd_setup_inputs(seed: int = 0) -> dict:
    key = jax.random.key(seed)
    ks = jax.random.split(key, 16)
    f32 = jnp.float32
    x = jax.random.normal(ks[0], (BATCH, SEQ, D_MODEL), f32)
    pre_norm_w = 1.0 + 0.02 * jax.random.normal(ks[1], (DEPTH, D_MODEL), f32)
    w_in = jax.random.normal(ks[2], (DEPTH, D_MODEL, D_IN_PROJ), f32) * D_MODEL ** -0.5
    pool_mix_w = jax.random.normal(ks[3], (DEPTH, POOL_GROUPS, POOL_GROUP_W, POOL_GROUP_W), f32) * POOL_GROUP_W ** -0.5
    pool_scale = 1.0 + 0.02 * jax.random.normal(ks[4], (DEPTH, POOL_W), f32)
    conv_w = jax.random.normal(ks[5], (DEPTH, CONV_WIDTH, CONV_DIM), f32) * CONV_WIDTH ** -0.5
    conv_b = 0.02 * jax.random.normal(ks[6], (DEPTH, CONV_DIM), f32)
    u = jax.random.uniform(ks[7], (DEPTH, SSD_HEADS), f32)
    dt0 = jnp.exp(u * (math.log(0.1) - math.log(0.001)) + math.log(0.001))
    dt_bias = dt0 + jnp.log(-jnp.expm1(-dt0))
    a_log = jnp.log(jax.random.uniform(ks[8], (DEPTH, SSD_HEADS), f32, minval=1.0, maxval=16.0))
    d_skip = 1.0 + 0.02 * jax.random.normal(ks[9], (DEPTH, SSD_HEADS), f32)
    ssd_norm_w = 1.0 + 0.02 * jax.random.normal(ks[10], (DEPTH, SSD_W), f32)
    w_out = jax.random.normal(ks[11], (DEPTH, D_MIX, D_MODEL), f32) * D_MIX ** -0.5
    post_norm_w = 1.0 + 0.02 * jax.random.normal(ks[12], (DEPTH, D_MODEL), f32)
    return {"x": x, "pre_norm_w": pre_norm_w, "w_in": w_in, "pool_mix_w": pool_mix_w,
            "pool_scale": pool_scale, "conv_w": conv_w, "conv_b": conv_b,
            "dt_bias": dt_bias, "a_log": a_log, "d_skip": d_skip,
            "ssd_norm_w": ssd_norm_w, "w_out": w_out, "post_norm_w": post_norm_w}


def _fwd_reference(x, pre_norm_w, w_in, pool_mix_w, pool_scale, conv_w, conv_b,
              dt_bias, a_log, d_skip, ssd_norm_w, w_out, post_norm_w):
    o1 = POOL_W
    o2 = o1 + POOL_W
    o3 = o2 + SSD_W
    o4 = o3 + CONV_DIM
    for layer in range(DEPTH):
        h = rms_norm(x, pre_norm_w[layer])
        proj = jnp.einsum('bld,de->ble', h, w_in[layer])
        pool_u = proj[..., :o1]
        pool_gate = proj[..., o1:o2]
        ssd_z = proj[..., o2:o3]
        ssd_xbc = proj[..., o3:o4]
        ssd_dt = proj[..., o4:]
        y_pool = pool_mixer(pool_u, pool_mix_w[layer], pool_scale[layer]) * jax.nn.silu(pool_gate)
        y_ssd = ssd_branch(ssd_z, ssd_xbc, ssd_dt, conv_w[layer], conv_b[layer],
                           dt_bias[layer], a_log[layer], d_skip[layer], ssd_norm_w[layer])
        mixed = jnp.concatenate([y_pool.astype(x.dtype), y_ssd.astype(x.dtype)], axis=-1)
        out = jnp.einsum('ble,ed->bld', mixed, w_out[layer])
        x = x + rms_norm(out, post_norm_w[layer])
    return x


import jax as _jax
import jax.numpy as _jnp

TWIN_FORMAT = 'train_step'
FWD_PARAMS = ['x', 'pre_norm_w', 'w_in', 'pool_mix_w', 'pool_scale', 'conv_w', 'conv_b', 'dt_bias', 'a_log', 'd_skip', 'ssd_norm_w', 'w_out', 'post_norm_w']
TWIN_WEIGHTS = ['pre_norm_w', 'w_in', 'pool_mix_w', 'pool_scale', 'conv_w', 'conv_b', 'dt_bias', 'a_log', 'd_skip', 'ssd_norm_w', 'w_out', 'post_norm_w']
TWIN_DIFF_INPUT = 'x'
TWIN_INPUTS = ['x', 'pre_norm_w', 'w_in', 'pool_mix_w', 'pool_scale', 'conv_w', 'conv_b', 'dt_bias', 'a_log', 'd_skip', 'ssd_norm_w', 'w_out', 'post_norm_w', 'loss_target', 'm_pre_norm_w', 'm_w_in', 'm_pool_mix_w', 'm_pool_scale', 'm_conv_w', 'm_conv_b', 'm_dt_bias', 'm_a_log', 'm_d_skip', 'm_ssd_norm_w', 'm_w_out', 'm_post_norm_w', 'v_pre_norm_w', 'v_w_in', 'v_pool_mix_w', 'v_pool_scale', 'v_conv_w', 'v_conv_b', 'v_dt_bias', 'v_a_log', 'v_d_skip', 'v_ssd_norm_w', 'v_w_out', 'v_post_norm_w']
TWIN_OUTPUTS = ['loss', 'grad_x', 'grad_pre_norm_w', 'grad_w_in', 'grad_pool_mix_w', 'grad_pool_scale', 'grad_conv_w', 'grad_conv_b', 'grad_dt_bias', 'grad_a_log', 'grad_d_skip', 'grad_ssd_norm_w', 'grad_w_out', 'grad_post_norm_w', 'delta_pre_norm_w', 'delta_w_in', 'delta_pool_mix_w', 'delta_pool_scale', 'delta_conv_w', 'delta_conv_b', 'delta_dt_bias', 'delta_a_log', 'delta_d_skip', 'delta_ssd_norm_w', 'delta_w_out', 'delta_post_norm_w', 'new_m_pre_norm_w', 'new_m_w_in', 'new_m_pool_mix_w', 'new_m_pool_scale', 'new_m_conv_w', 'new_m_conv_b', 'new_m_dt_bias', 'new_m_a_log', 'new_m_d_skip', 'new_m_ssd_norm_w', 'new_m_w_out', 'new_m_post_norm_w', 'new_v_pre_norm_w', 'new_v_w_in', 'new_v_pool_mix_w', 'new_v_pool_scale', 'new_v_conv_w', 'new_v_conv_b', 'new_v_dt_bias', 'new_v_a_log', 'new_v_d_skip', 'new_v_ssd_norm_w', 'new_v_w_out', 'new_v_post_norm_w']
TWIN_LEAF_KINDS = {'loss': 'loss', 'grad_x': 'grad_x', 'grad_pre_norm_w': 'grad_w', 'grad_w_in': 'grad_w', 'grad_pool_mix_w': 'grad_w', 'grad_pool_scale': 'grad_w', 'grad_conv_w': 'grad_w', 'grad_conv_b': 'grad_w', 'grad_dt_bias': 'grad_w', 'grad_a_log': 'grad_w', 'grad_d_skip': 'grad_w', 'grad_ssd_norm_w': 'grad_w', 'grad_w_out': 'grad_w', 'grad_post_norm_w': 'grad_w', 'delta_pre_norm_w': 'delta_w', 'delta_w_in': 'delta_w', 'delta_pool_mix_w': 'delta_w', 'delta_pool_scale': 'delta_w', 'delta_conv_w': 'delta_w', 'delta_conv_b': 'delta_w', 'delta_dt_bias': 'delta_w', 'delta_a_log': 'delta_w', 'delta_d_skip': 'delta_w', 'delta_ssd_norm_w': 'delta_w', 'delta_w_out': 'delta_w', 'delta_post_norm_w': 'delta_w', 'new_m_pre_norm_w': 'new_m', 'new_m_w_in': 'new_m', 'new_m_pool_mix_w': 'new_m', 'new_m_pool_scale': 'new_m', 'new_m_conv_w': 'new_m', 'new_m_conv_b': 'new_m', 'new_m_dt_bias': 'new_m', 'new_m_a_log': 'new_m', 'new_m_d_skip': 'new_m', 'new_m_ssd_norm_w': 'new_m', 'new_m_w_out': 'new_m', 'new_m_post_norm_w': 'new_m', 'new_v_pre_norm_w': 'new_v', 'new_v_w_in': 'new_v', 'new_v_pool_mix_w': 'new_v', 'new_v_pool_scale': 'new_v', 'new_v_conv_w': 'new_v', 'new_v_conv_b': 'new_v', 'new_v_dt_bias': 'new_v', 'new_v_a_log': 'new_v', 'new_v_d_skip': 'new_v', 'new_v_ssd_norm_w': 'new_v', 'new_v_w_out': 'new_v', 'new_v_post_norm_w': 'new_v'}


def _forward(args):
    return _fwd_reference(*[args[k] for k in FWD_PARAMS])


def _output_shape():
    def fwd():
        inp = _fwd_setup_inputs(0)
        return _fwd_reference(*[inp[k] for k in FWD_PARAMS])
    out = _jax.eval_shape(fwd)
    return out.shape, out.dtype

N_MICROBATCH = 1
ADAM_LR = 0.001
ADAM_B1 = 0.9
ADAM_B2 = 0.999
ADAM_EPS = 1e-08
ADAM_WD = 0.01
ADAM_STEP = 10
PER_EXAMPLE_BATCH_AXIS = {'x': 0, 'loss_target': 0}
SHARED_INPUTS = []
_WEIGHT_DTYPES = {'pre_norm_w': _jnp.float32, 'w_in': _jnp.float32, 'pool_mix_w': _jnp.float32, 'pool_scale': _jnp.float32, 'conv_w': _jnp.float32, 'conv_b': _jnp.float32, 'dt_bias': _jnp.float32, 'a_log': _jnp.float32, 'd_skip': _jnp.float32, 'ssd_norm_w': _jnp.float32, 'w_out': _jnp.float32, 'post_norm_w': _jnp.float32}
MOMENT_SCALE = {'pre_norm_w': 5.943750e-01, 'w_in': 2.791689e-01, 'pool_mix_w': 1.680996e-01, 'pool_scale': 1.705115e-01, 'conv_w': 5.528040e-01, 'conv_b': 1.788364e+00, 'dt_bias': 6.014368e-01, 'a_log': 3.971271e+00, 'd_skip': 3.712024e+00, 'ssd_norm_w': 9.772134e-01, 'w_out': 9.504071e-01, 'post_norm_w': 1.601480e+01}


def _to_microbatches(a, axis):
    t = _jnp.moveaxis(a, axis, 0)
    t = t.reshape((N_MICROBATCH, t.shape[0] // N_MICROBATCH) + t.shape[1:])
    return _jnp.moveaxis(t, 1, axis + 1)


def setup_inputs(seed: int = 0) -> dict:
    inp = _fwd_setup_inputs(seed)
    key = _jax.random.fold_in(_jax.random.key(seed), 7919)
    shape, _ = _output_shape()
    out = dict(inp)
    out["loss_target"] = _jax.random.normal(_jax.random.fold_in(key, 0), shape, _jnp.float32)
    for i, name in enumerate(TWIN_WEIGHTS):
        w = inp[name].astype(_jnp.float32)
        if MOMENT_SCALE is None:
            s = _jnp.sqrt(_jnp.mean(_jnp.square(w)) + 1e-30)
        else:
            s = MOMENT_SCALE[name]
        km, kv = _jax.random.split(_jax.random.fold_in(key, i + 1))
        out[name] = w
        out["m_" + name] = s * _jax.random.normal(km, w.shape, _jnp.float32)
        out["v_" + name] = (s * s) * _jax.random.uniform(kv, w.shape, _jnp.float32, 0.5, 1.5)
    if N_MICROBATCH > 1:
        for name, axis in PER_EXAMPLE_BATCH_AXIS.items():
            out[name] = _to_microbatches(out[name], axis)
    return {'x': out['x'], 'pre_norm_w': out['pre_norm_w'], 'w_in': out['w_in'], 'pool_mix_w': out['pool_mix_w'], 'pool_scale': out['pool_scale'], 'conv_w': out['conv_w'], 'conv_b': out['conv_b'], 'dt_bias': out['dt_bias'], 'a_log': out['a_log'], 'd_skip': out['d_skip'], 'ssd_norm_w': out['ssd_norm_w'], 'w_out': out['w_out'], 'post_norm_w': out['post_norm_w'], 'loss_target': out['loss_target'], 'm_pre_norm_w': out['m_pre_norm_w'], 'm_w_in': out['m_w_in'], 'm_pool_mix_w': out['m_pool_mix_w'], 'm_pool_scale': out['m_pool_scale'], 'm_conv_w': out['m_conv_w'], 'm_conv_b': out['m_conv_b'], 'm_dt_bias': out['m_dt_bias'], 'm_a_log': out['m_a_log'], 'm_d_skip': out['m_d_skip'], 'm_ssd_norm_w': out['m_ssd_norm_w'], 'm_w_out': out['m_w_out'], 'm_post_norm_w': out['m_post_norm_w'], 'v_pre_norm_w': out['v_pre_norm_w'], 'v_w_in': out['v_w_in'], 'v_pool_mix_w': out['v_pool_mix_w'], 'v_pool_scale': out['v_pool_scale'], 'v_conv_w': out['v_conv_w'], 'v_conv_b': out['v_conv_b'], 'v_dt_bias': out['v_dt_bias'], 'v_a_log': out['v_a_log'], 'v_d_skip': out['v_d_skip'], 'v_ssd_norm_w': out['v_ssd_norm_w'], 'v_w_out': out['v_w_out'], 'v_post_norm_w': out['v_post_norm_w']}


def _loss(weights, diff, rest, loss_target):
    with _jax.named_scope("forward"):
        args = {**rest, TWIN_DIFF_INPUT: diff, **{k: w.astype(_WEIGHT_DTYPES[k]) for k, w in weights.items()}}
        y = _forward(args)
    with _jax.named_scope("loss_head"):
        err = _jnp.square(y.astype(_jnp.float32) - loss_target)
        return 0.5 * _jnp.sum(_jnp.mean(err, axis=-1)) if err.ndim else 0.5 * err


def _adamw(w, g, m, v):
    m = ADAM_B1 * m + (1.0 - ADAM_B1) * g
    v = ADAM_B2 * v + (1.0 - ADAM_B2) * _jnp.square(g)
    m_hat = m / (1.0 - ADAM_B1 ** ADAM_STEP)
    v_hat = v / (1.0 - ADAM_B2 ** ADAM_STEP)
    delta = -ADAM_LR * (m_hat / (_jnp.sqrt(v_hat) + ADAM_EPS) + ADAM_WD * w)
    return delta, m, v


def reference(x, pre_norm_w, w_in, pool_mix_w, pool_scale, conv_w, conv_b, dt_bias, a_log, d_skip, ssd_norm_w, w_out, post_norm_w, loss_target, m_pre_norm_w, m_w_in, m_pool_mix_w, m_pool_scale, m_conv_w, m_conv_b, m_dt_bias, m_a_log, m_d_skip, m_ssd_norm_w, m_w_out, m_post_norm_w, v_pre_norm_w, v_w_in, v_pool_mix_w, v_pool_scale, v_conv_w, v_conv_b, v_dt_bias, v_a_log, v_d_skip, v_ssd_norm_w, v_w_out, v_post_norm_w):
    given = dict(x=x, pre_norm_w=pre_norm_w, w_in=w_in, pool_mix_w=pool_mix_w, pool_scale=pool_scale, conv_w=conv_w, conv_b=conv_b, dt_bias=dt_bias, a_log=a_log, d_skip=d_skip, ssd_norm_w=ssd_norm_w, w_out=w_out, post_norm_w=post_norm_w, loss_target=loss_target, m_pre_norm_w=m_pre_norm_w, m_w_in=m_w_in, m_pool_mix_w=m_pool_mix_w, m_pool_scale=m_pool_scale, m_conv_w=m_conv_w, m_conv_b=m_conv_b, m_dt_bias=m_dt_bias, m_a_log=m_a_log, m_d_skip=m_d_skip, m_ssd_norm_w=m_ssd_norm_w, m_w_out=m_w_out, m_post_norm_w=m_post_norm_w, v_pre_norm_w=v_pre_norm_w, v_w_in=v_w_in, v_pool_mix_w=v_pool_mix_w, v_pool_scale=v_pool_scale, v_conv_w=v_conv_w, v_conv_b=v_conv_b, v_dt_bias=v_dt_bias, v_a_log=v_a_log, v_d_skip=v_d_skip, v_ssd_norm_w=v_ssd_norm_w, v_w_out=v_w_out, v_post_norm_w=v_post_norm_w)
    weights = {n: given[n] for n in TWIN_WEIGHTS}
    shared = {n: given[n] for n in SHARED_INPUTS}
    per_example = {n: given[n] for n in ['x']}
    grad_fn = _jax.value_and_grad(_loss, argnums=(0, 1))

    def one_microbatch(ex, loss_target):
        ex = dict(ex)
        diff = ex.pop(TWIN_DIFF_INPUT)
        return grad_fn(weights, diff, {**shared, **ex}, loss_target)

    if N_MICROBATCH == 1:
        loss, (grad_w, grad_x) = one_microbatch(per_example, given["loss_target"])
    else:
        def body(carry, xs):
            loss_sum, grad_sum = carry
            l_k, (gw_k, gx_k) = one_microbatch(xs[0], xs[1])
            with _jax.named_scope("update"):
                return (loss_sum + l_k, _jax.tree.map(_jnp.add, grad_sum, gw_k)), gx_k

        init = (_jnp.zeros((), _jnp.float32), _jax.tree.map(_jnp.zeros_like, weights))
        (loss, grad_w), grad_x = _jax.lax.scan(body, init, (per_example, given["loss_target"]))
    with _jax.named_scope("update"):
        delta_w, new_m, new_v = {}, {}, {}
        for n in TWIN_WEIGHTS:
            delta_w[n], new_m[n], new_v[n] = _adamw(weights[n], grad_w[n], given["m_" + n], given["v_" + n])
    return (loss, grad_x, *[grad_w[n] for n in TWIN_WEIGHTS], *[delta_w[n] for n in TWIN_WEIGHTS],
            *[new_m[n] for n in TWIN_WEIGHTS], *[new_v[n] for n in TWIN_WEIGHTS])
```

```python
import jax
import jax.numpy as jnp
from jax import lax
from jax.experimental import pallas as pl
from jax.experimental.pallas import tpu as pltpu

F32 = jnp.float32
BF16 = jnp.bfloat16
SDS = jax.ShapeDtypeStruct
MESH = pl.DeviceIdType.MESH
HIGHEST = lax.Precision.HIGHEST

NORM_EPS = 1e-6
POOL_WINDOWS = (2, 4, 8, 16)
POOL_HALO = 16
CONV_WIDTH = 4
CONV_HALO = 8
SSD_CHUNK = 128
SSD_HEAD_DIM = 64
SSD_STATE = 128
SSD_GROUPS = 4
LANES = 128
N_DEV = 8

ADAM_LR = 0.001
ADAM_B1 = 0.9
ADAM_B2 = 0.999
ADAM_EPS = 1e-08
ADAM_WD = 0.01
ADAM_STEP = 10

VMEM_LIMIT = 56 * 1024 * 1024

NT = (((1,), (1,)), ((), ()))
TN = (((0,), (0,)), ((), ()))


def _params(sem=None):
    kw = dict(vmem_limit_bytes=VMEM_LIMIT)
    if sem is not None:
        kw["dimension_semantics"] = sem
    return pltpu.CompilerParams(**kw)


def _silu(v):
    return v * jax.nn.sigmoid(v)


def _dsilu(v):
    s = jax.nn.sigmoid(v)
    return s * (1.0 + v * (1.0 - s))


def _split_dot(v, sel):
    hi = v.astype(BF16)
    lo = (v - hi.astype(F32)).astype(BF16)
    return (jnp.dot(hi, sel, preferred_element_type=F32) + jnp.dot(lo, sel, preferred_element_type=F32))


def _head_selector(width, per):
    ch = lax.broadcasted_iota(jnp.int32, (width, LANES), 0)
    hd = lax.broadcasted_iota(jnp.int32, (width, LANES), 1)
    return jnp.where((ch >= hd * per) & (ch < (hd + 1) * per), 1.0, 0.0).astype(BF16)


def rms_fwd(x, w, tm):
    t, d = x.shape

    def body(x_ref, w_ref, h_ref, r_ref):
        xv = x_ref[...]
        r = lax.rsqrt(jnp.mean(xv * xv, axis=-1, keepdims=True) + NORM_EPS)
        h_ref[...] = (xv * r * w_ref[...]).astype(BF16)
        r_ref[...] = r

    return pl.pallas_call(
        body, name="rms_fwd", grid=(t // tm,),
        in_specs=[pl.BlockSpec((tm, d), lambda i: (i, 0)), pl.BlockSpec((1, d), lambda i: (0, 0))],
        out_specs=[pl.BlockSpec((tm, d), lambda i: (i, 0)), pl.BlockSpec((tm, 1), lambda i: (i, 0))],
        out_shape=[SDS((t, d), BF16), SDS((t, 1), F32)],
        compiler_params=_params(("arbitrary",)),
    )(x, w)


def post_fwd(out, x, w, tm):
    t, d = x.shape

    def body(o_ref, x_ref, w_ref, y_ref, r_ref):
        ov = o_ref[...]
        r = lax.rsqrt(jnp.mean(ov * ov, axis=-1, keepdims=True) + NORM_EPS)
        y_ref[...] = x_ref[...] + ov * r * w_ref[...]
        r_ref[...] = r

    return pl.pallas_call(
        body, name="post_fwd", grid=(t // tm,),
        in_specs=[pl.BlockSpec((tm, d), lambda i: (i, 0)), pl.BlockSpec((tm, d), lambda i: (i, 0)),
                  pl.BlockSpec((1, d), lambda i: (0, 0))],
        out_specs=[pl.BlockSpec((tm, d), lambda i: (i, 0)), pl.BlockSpec((tm, 1), lambda i: (i, 0))],
        out_shape=[SDS((t, d), F32), SDS((t, 1), F32)],
        compiler_params=_params(("arbitrary",)),
    )(out, x, w)


def _norm_bwd(g_n, n, r):
    return r * (g_n - n * jnp.mean(g_n * n, axis=-1, keepdims=True))


def post_bwd(g, out, r, w, tm):
    t, d = g.shape

    def body(g_ref, o_ref, r_ref, w_ref, do_ref, dw_ref):
        i = pl.program_id(0)
        gv = g_ref[...]
        rv = r_ref[...]
        n = o_ref[...] * rv
        part = jnp.sum(gv * n, axis=0, keepdims=True)

        @pl.when(i == 0)
        def _():
            dw_ref[...] = part

        @pl.when(i > 0)
        def _():
            dw_ref[...] += part

        do_ref[...] = _norm_bwd(gv * w_ref[...], n, rv).astype(BF16)

    return pl.pallas_call(
        body, name="post_bwd", grid=(t // tm,),
        in_specs=[pl.BlockSpec((tm, d), lambda i: (i, 0)), pl.BlockSpec((tm, d), lambda i: (i, 0)),
                  pl.BlockSpec((tm, 1), lambda i: (i, 0)), pl.BlockSpec((1, d), lambda i: (0, 0))],
        out_specs=[pl.BlockSpec((tm, d), lambda i: (i, 0)), pl.BlockSpec((1, d), lambda i: (0, 0))],
        out_shape=[SDS((t, d), BF16), SDS((1, d), F32)],
        compiler_params=_params(("arbitrary",)),
    )(g, out, r, w)


def rms_bwd(dh_a, dh_b, x, r, w, g, tm):
    t, d = x.shape

    def body(a_ref, b_ref, x_ref, r_ref, w_ref, g_ref, gx_ref, dw_ref):
        i = pl.program_id(0)
        dh = a_ref[...] + b_ref[...]
        rv = r_ref[...]
        n = x_ref[...] * rv
        part = jnp.sum(dh * n, axis=0, keepdims=True)

        @pl.when(i == 0)
        def _():
            dw_ref[...] = part

        @pl.when(i > 0)
        def _():
            dw_ref[...] += part

        gx_ref[...] = g_ref[...] + _norm_bwd(dh * w_ref[...], n, rv)

    row = pl.BlockSpec((tm, d), lambda i: (i, 0))
    return pl.pallas_call(
        body, name="rms_bwd", grid=(t // tm,),
        in_specs=[row, row, row, pl.BlockSpec((tm, 1), lambda i: (i, 0)), pl.BlockSpec((1, d), lambda i: (0, 0)), row],
        out_specs=[row, pl.BlockSpec((1, d), lambda i: (0, 0))],
        out_shape=[SDS((t, d), F32), SDS((1, d), F32)],
        compiler_params=_params(("arbitrary",)),
    )(dh_a, dh_b, x, r, w, g)


def loss_grad(y, target, tm):
    t, d = y.shape

    def body(y_ref, t_ref, l_ref, g_ref):
        i = pl.program_id(0)
        err = y_ref[...] - t_ref[...]
        g_ref[...] = err / d
        part = 0.5 * jnp.sum(jnp.mean(err * err, axis=-1, keepdims=True), axis=0, keepdims=True)

        @pl.when(i == 0)
        def _():
            l_ref[...] = part

        @pl.when(i > 0)
        def _():
            l_ref[...] += part

    row = pl.BlockSpec((tm, d), lambda i: (i, 0))
    return pl.pallas_call(
        body, name="loss_grad", grid=(t // tm,), in_specs=[row, row],
        out_specs=[pl.BlockSpec((1, 1), lambda i: (0, 0)), row],
        out_shape=[SDS((1, 1), F32), SDS((t, d), F32)],
        compiler_params=_params(("arbitrary",)),
    )(y, target)


def mm_nn(a, b, out_dtype, tm, tn, name):
    m, k = a.shape
    n = b.shape[1]

    def body(a_ref, b_ref, o_ref):
        o_ref[...] = jnp.dot(a_ref[...], b_ref[...], preferred_element_type=F32).astype(out_dtype)

    return pl.pallas_call(
        body, name=name, grid=(n // tn, m // tm),
        in_specs=[pl.BlockSpec((tm, k), lambda j, i: (i, 0)), pl.BlockSpec((k, tn), lambda j, i: (0, j))],
        out_specs=pl.BlockSpec((tm, tn), lambda j, i: (i, j)),
        out_shape=SDS((m, n), out_dtype),
        compiler_params=_params(("arbitrary", "arbitrary")),
    )(a, b)


def mm_nt(a, b, out_dtype, tm, tn, tk, name):
    m, k = a.shape
    n = b.shape[0]
    nk = k // tk

    def body(a_ref, b_ref, o_ref, acc_ref):
        kk = pl.program_id(2)
        part = lax.dot_general(a_ref[...], b_ref[...], NT, preferred_element_type=F32)
        if nk == 1:
            o_ref[...] = part.astype(out_dtype)
        else:
            @pl.when(kk == 0)
            def _():
                acc_ref[...] = part

            @pl.when(kk > 0)
            def _():
                acc_ref[...] += part

            @pl.when(kk == nk - 1)
            def _():
                o_ref[...] = acc_ref[...].astype(out_dtype)

    return pl.pallas_call(
        body, name=name, grid=(m // tm, n // tn, nk),
        in_specs=[pl.BlockSpec((tm, tk), lambda i, j, kk: (i, kk)), pl.BlockSpec((tn, tk), lambda i, j, kk: (j, kk))],
        out_specs=pl.BlockSpec((tm, tn), lambda i, j, kk: (i, j)),
        out_shape=SDS((m, n), out_dtype),
        scratch_shapes=[pltpu.VMEM((tm, tn) if nk > 1 else (8, LANES), F32)],
        compiler_params=_params(("arbitrary", "arbitrary", "arbitrary")),
    )(a, b)


def mm_tn(a, b, tm, tn, name):
    t, m = a.shape
    n = b.shape[1]

    def body(a_ref, b_ref, o_ref):
        o_ref[...] = lax.dot_general(a_ref[...], b_ref[...], TN, preferred_element_type=F32)

    return pl.pallas_call(
        body, name=name, grid=(m // tm, n // tn),
        in_specs=[pl.BlockSpec((t, tm), lambda i, j: (0, i)), pl.BlockSpec((t, tn), lambda i, j: (0, j))],
        out_specs=pl.BlockSpec((tm, tn), lambda i, j: (i, j)),
        out_shape=SDS((m, n), F32),
        compiler_params=_params(("arbitrary", "arbitrary")),
    )(a, b)


def _window_sums(ext, n_rows, lookahead):
    def sh(v, k):
        return pltpu.roll(v, (n_rows - k) if lookahead else k, 0)
    s2 = ext + sh(ext, 1)
    s4 = s2 + sh(s2, 2)
    s8 = s4 + sh(s4, 4)
    s16 = s8 + sh(s8, 8)
    return (s2, s4, s8, s16)


def _pool_counts(i, tm, w):
    tpos = i * tm + lax.broadcasted_iota(jnp.int32, (tm, 1), 0)
    return jnp.minimum(tpos + 1, w).astype(F32)


def _pooled(uc_ref, up_ref, i, tm):
    cur = uc_ref[...]
    prev = jnp.where(i > 0, up_ref[...], 0.0)
    ext = jnp.concatenate([prev, cur], axis=0)
    return cur, _window_sums(ext, tm + POOL_HALO, False)


def pool_fwd(proj, mixw, scale, tm):
    t = proj.shape[0]
    pw = scale.shape[1]
    gw = pw // len(POOL_WINDOWS)
    nh = tm // POOL_HALO

    def body(uc_ref, up_ref, g_ref, w_ref, s_ref, o_ref):
        i = pl.program_id(0)
        cur, sums = _pooled(uc_ref, up_ref, i, tm)
        for g, w in enumerate(POOL_WINDOWS):
            cols = slice(g * gw, (g + 1) * gw)
            pooled = sums[g][POOL_HALO:, cols] / _pool_counts(i, tm, w) - cur[:, cols]
            mixed = jnp.dot(pooled.astype(BF16), w_ref[g], preferred_element_type=F32)
            o_ref[:, cols] = (mixed * s_ref[:, cols] * _silu(g_ref[:, cols])).astype(BF16)

    return pl.pallas_call(
        body, name="pool_fwd", grid=(t // tm,),
        in_specs=[pl.BlockSpec((tm, pw), lambda i: (i, 0)),
                  pl.BlockSpec((POOL_HALO, pw), lambda i: (jnp.maximum(i * nh - 1, 0), 0)),
                  pl.BlockSpec((tm, pw), lambda i: (i, 1)),
                  pl.BlockSpec(mixw.shape, lambda i: (0, 0, 0)),
                  pl.BlockSpec((1, pw), lambda i: (0, 0))],
        out_specs=pl.BlockSpec((tm, pw), lambda i: (i, 0)),
        out_shape=SDS((t, 2 * pw), BF16),
        compiler_params=_params(("arbitrary",)),
    )(proj, proj, proj, mixw, scale)


def pool_bwd_a(dmixed, proj, mixw, scale, tm):
    t, e = proj.shape
    pw = scale.shape[1]
    ng = len(POOL_WINDOWS)
    gw = pw // ng
    nh = tm // POOL_HALO

    def body(dy_ref, uc_ref, up_ref, g_ref, w_ref, s_ref, dg_ref, dq_ref, ds_ref, dw_ref):
        i = pl.program_id(0)

        @pl.when(i == 0)
        def _():
            ds_ref[...] = jnp.zeros_like(ds_ref)
            dw_ref[...] = jnp.zeros_like(dw_ref)

        cur, sums = _pooled(uc_ref, up_ref, i, tm)
        for g, w in enumerate(POOL_WINDOWS):
            cols = slice(g * gw, (g + 1) * gw)
            cnt = _pool_counts(i, tm, w)
            pooled = (sums[g][POOL_HALO:, cols] / cnt - cur[:, cols]).astype(BF16)
            mixed = jnp.dot(pooled, w_ref[g], preferred_element_type=F32)
            gate = g_ref[:, cols]
            dy = dy_ref[:, cols]
            sc = s_ref[:, cols]
            dg_ref[:, cols] = (dy * mixed * sc * _dsilu(gate)).astype(BF16)
            ds = dy * _silu(gate)
            ds_ref[:, cols] += jnp.sum(ds * mixed, axis=0, keepdims=True)
            dmix = (ds * sc).astype(BF16)
            dw_ref[g] += lax.dot_general(pooled, dmix, TN, preferred_element_type=F32)
            dq_ref[:, cols] = lax.dot_general(dmix, w_ref[g], NT, preferred_element_type=F32) / cnt

    return pl.pallas_call(
        body, name="pool_bwd_a", grid=(t // tm,),
        in_specs=[pl.BlockSpec((tm, pw), lambda i: (i, 0)),
                  pl.BlockSpec((tm, pw), lambda i: (i, 0)),
                  pl.BlockSpec((POOL_HALO, pw), lambda i: (jnp.maximum(i * nh - 1, 0), 0)),
                  pl.BlockSpec((tm, pw), lambda i: (i, 1)),
                  pl.BlockSpec(mixw.shape, lambda i: (0, 0, 0)),
                  pl.BlockSpec((1, pw), lambda i: (0, 0))],
        out_specs=[pl.BlockSpec((tm, pw), lambda i: (i, 1)),
                   pl.BlockSpec((tm, pw), lambda i: (i, 0)),
                   pl.BlockSpec((1, pw), lambda i: (0, 0)),
                   pl.BlockSpec((ng, gw, gw), lambda i: (0, 0, 0))],
        out_shape=[SDS((t, e), BF16), SDS((t, pw), F32), SDS((1, pw), F32), SDS((ng, gw, gw), F32)],
        compiler_params=_params(("arbitrary",)),
    )(dmixed, proj, proj, proj, mixw, scale)


def pool_bwd_b(dq, dproj, tm):
    t, pw = dq.shape
    gw = pw // len(POOL_WINDOWS)
    nh = tm // POOL_HALO
    nt = t // tm

    def body(c_ref, n_ref, alias_ref, o_ref):
        i = pl.program_id(0)
        cur = c_ref[...]
        nxt = jnp.where(i < nt - 1, n_ref[...], 0.0)
        sums = _window_sums(jnp.concatenate([cur, nxt], axis=0), tm + POOL_HALO, True)
        for g, w in enumerate(POOL_WINDOWS):
            cols = slice(g * gw, (g + 1) * gw)
            o_ref[:, cols] = (sums[g][:tm, cols] - cur[:, cols] * _pool_counts(i, tm, w)).astype(BF16)

    return pl.pallas_call(
        body, name="pool_bwd_b", grid=(nt,),
        in_specs=[pl.BlockSpec((tm, pw), lambda i: (i, 0)),
                  pl.BlockSpec((POOL_HALO, pw), lambda i: (jnp.minimum((i + 1) * nh, t // POOL_HALO - 1), 0)),
                  pl.BlockSpec(memory_space=pl.ANY)],
        out_specs=pl.BlockSpec((tm, pw), lambda i: (i, 0)),
        out_shape=SDS(dproj.shape, dproj.dtype),
        input_output_aliases={2: 0},
        compiler_params=_params(("arbitrary",)),
    )(dq, dq, dproj)


def _conv_pre(xc_ref, xp_ref, w_ref, b_ref, i, tm):
    cur = xc_ref[...]
    prev = jnp.where(i > 0, xp_ref[...], 0.0)
    ext = jnp.concatenate([prev, cur], axis=0)
    taps = [pltpu.roll(ext, CONV_WIDTH - 1 - k, 0)[CONV_HALO:] for k in range(CONV_WIDTH - 1)] + [cur]
    pre = b_ref[...]
    for k in range(CONV_WIDTH):
        pre = pre + w_ref[k:k + 1, :] * taps[k]
    return pre, taps


def conv_fwd(proj, conv_w, conv_b, col_block, tm):
    t = proj.shape[0]
    cd = conv_b.shape[1]
    nh = tm // CONV_HALO

    def body(xc_ref, xp_ref, w_ref, b_ref, o_ref):
        i = pl.program_id(0)
        pre, _ = _conv_pre(xc_ref, xp_ref, w_ref, b_ref, i, tm)
        o_ref[...] = _silu(pre)

    return pl.pallas_call(
        body, name="conv_fwd", grid=(t // tm,),
        in_specs=[pl.BlockSpec((tm, cd), lambda i: (i, col_block)),
                  pl.BlockSpec((CONV_HALO, cd), lambda i: (jnp.maximum(i * nh - 1, 0), col_block)),
                  pl.BlockSpec((CONV_WIDTH, cd), lambda i: (0, 0)),
                  pl.BlockSpec((1, cd), lambda i: (0, 0))],
        out_specs=pl.BlockSpec((tm, cd), lambda i: (i, 0)),
        out_shape=SDS((t, cd), F32),
        compiler_params=_params(("arbitrary",)),
    )(proj, proj, conv_w, conv_b)


def conv_bwd_a(dxs, db, dc, proj, conv_w, conv_b, col_block, tm):
    t = proj.shape[0]
    cd = conv_b.shape[1]
    sw = dxs.shape[1]
    gn = db.shape[1]
    nh = tm // CONV_HALO

    def body(dx_ref, db_ref, dc_ref, xc_ref, xp_ref, w_ref, b_ref, dp_ref, dw_ref, dbias_ref):
        i = pl.program_id(0)

        @pl.when(i == 0)
        def _():
            dw_ref[...] = jnp.zeros_like(dw_ref)
            dbias_ref[...] = jnp.zeros_like(dbias_ref)

        pre, taps = _conv_pre(xc_ref, xp_ref, w_ref, b_ref, i, tm)
        dact = jnp.concatenate([dx_ref[...], db_ref[...], dc_ref[...]], axis=1)
        dpre = dact * _dsilu(pre)
        dp_ref[...] = dpre
        dbias_ref[...] += jnp.sum(dpre, axis=0, keepdims=True)
        for k in range(CONV_WIDTH):
            dw_ref[k:k + 1, :] += jnp.sum(dpre * taps[k], axis=0, keepdims=True)

    return pl.pallas_call(
        body, name="conv_bwd_a", grid=(t // tm,),
        in_specs=[pl.BlockSpec((tm, sw), lambda i: (i, 0)), pl.BlockSpec((tm, gn), lambda i: (i, 0)),
                  pl.BlockSpec((tm, gn), lambda i: (i, 0)),
                  pl.BlockSpec((tm, cd), lambda i: (i, col_block)),
                  pl.BlockSpec((CONV_HALO, cd), lambda i: (jnp.maximum(i * nh - 1, 0), col_block)),
                  pl.BlockSpec((CONV_WIDTH, cd), lambda i: (0, 0)),
                  pl.BlockSpec((1, cd), lambda i: (0, 0))],
        out_specs=[pl.BlockSpec((tm, cd), lambda i: (i, 0)),
                   pl.BlockSpec((CONV_WIDTH, cd), lambda i: (0, 0)),
                   pl.BlockSpec((1, cd), lambda i: (0, 0))],
        out_shape=[SDS((t, cd), F32), SDS((CONV_WIDTH, cd), F32), SDS((1, cd), F32)],
        compiler_params=_params(("arbitrary",)),
    )(dxs, db, dc, proj, proj, conv_w, conv_b)


def conv_bwd_b(dpre, conv_w, dproj, col_block, tm):
    t, cd = dpre.shape
    nh = tm // CONV_HALO
    nt = t // tm

    def body(c_ref, n_ref, w_ref, alias_ref, o_ref):
        i = pl.program_id(0)
        cur = c_ref[...]
        nxt = jnp.where(i < nt - 1, n_ref[...], 0.0)
        ext = jnp.concatenate([cur, nxt], axis=0)
        n = tm + CONV_HALO
        acc = w_ref[CONV_WIDTH - 1:CONV_WIDTH, :] * cur
        for k in range(CONV_WIDTH - 1):
            acc = acc + w_ref[k:k + 1, :] * pltpu.roll(ext, n - (CONV_WIDTH - 1 - k), 0)[:tm]
        o_ref[...] = acc.astype(BF16)

    return pl.pallas_call(
        body, name="conv_bwd_b", grid=(nt,),
        in_specs=[pl.BlockSpec((tm, cd), lambda i: (i, 0)),
                  pl.BlockSpec((CONV_HALO, cd), lambda i: (jnp.minimum((i + 1) * nh, t // CONV_HALO - 1), 0)),
                  pl.BlockSpec((CONV_WIDTH, cd), lambda i: (0, 0)),
                  pl.BlockSpec(memory_space=pl.ANY)],
        out_specs=pl.BlockSpec((tm, cd), lambda i: (i, col_block)),
        out_shape=SDS(dproj.shape, dproj.dtype),
        input_output_aliases={3: 0},
        compiler_params=_params(("arbitrary",)),
    )(dpre, dpre, conv_w, dproj)


def _softplus(v):
    return jnp.maximum(v, 0.0) + jnp.log(1.0 + jnp.exp(-jnp.abs(v)))


def _ssd_chunk_terms(dtr_ref, bias_ref, a_ref, n_heads):
    q = SSD_CHUNK
    lane = lax.broadcasted_iota(jnp.int32, (1, LANES), 1)
    pre = dtr_ref[...] + bias_ref[...]
    dt = jnp.where(lane < n_heads, _softplus(pre), 0.0)
    a = jnp.where(lane < n_heads, -jnp.exp(a_ref[...]), 0.0)
    row = lax.broadcasted_iota(jnp.int32, (q, q), 0)
    col = lax.broadcasted_iota(jnp.int32, (q, q), 1)
    causal = row >= col
    acs = jnp.dot(causal.astype(F32), dt * a, precision=HIGHEST, preferred_element_type=F32)
    last = acs[q - 1:q, :]
    return dict(pre=pre, dt=dt, a=a, acs=acs, acs_t=acs.T, eacs=jnp.exp(acs), dstate=jnp.exp(last - acs),
                cdec=jnp.exp(last), causal=causal, lane=lane)


def _pair_cols(lo, v, h):
    return jnp.where(lo, v[:, h:h + 1], v[:, h + 1:h + 2])


def _pair_decay(tm_, cb, h):
    l0 = jnp.exp(jnp.where(tm_["causal"], tm_["acs"][:, h:h + 1] - tm_["acs_t"][h:h + 1, :], -jnp.inf))
    l1 = jnp.exp(jnp.where(tm_["causal"], tm_["acs"][:, h + 1:h + 2] - tm_["acs_t"][h + 1:h + 2, :], -jnp.inf))
    return l0, l1, jnp.concatenate([cb * l0, cb * l1], axis=1)


def _column_sums(v):
    ones = jnp.ones((v.shape[0], LANES), BF16)
    hi = v.astype(BF16)
    r1 = v - hi.astype(F32)
    mid = r1.astype(BF16)
    lo = (r1 - mid.astype(F32)).astype(BF16)
    return sum(lax.dot_general(part, ones, TN, preferred_element_type=F32) for part in (hi, mid, lo))


def _block_diag(lo, xdt):
    return jnp.concatenate([jnp.where(lo, xdt, 0.0), jnp.where(lo, 0.0, xdt)], axis=0).astype(BF16)


def ssd_fwd(xbc, dt_raw, dt_bias, a_log, n_heads):
    t = xbc.shape[0]
    q = SSD_CHUNK
    gn = SSD_GROUPS * SSD_STATE
    sw = n_heads * SSD_HEAD_DIM
    n_pairs = n_heads // 2
    pairs_per_group = n_pairs // SSD_GROUPS
    nc = t // q
    bblk = sw // gn

    def body(xs_ref, b_ref, c_ref, dtr_ref, bias_ref, a_ref, y_ref, sin_ref, state):
        @pl.when(pl.program_id(0) == 0)
        def _():
            state[...] = jnp.zeros_like(state)

        tm_ = _ssd_chunk_terms(dtr_ref, bias_ref, a_ref, n_heads)
        lo = tm_["lane"] < SSD_HEAD_DIM
        for g in range(SSD_GROUPS):
            gcols = slice(g * SSD_STATE, (g + 1) * SSD_STATE)
            bg = b_ref[:, gcols].astype(BF16)
            cg = c_ref[:, gcols].astype(BF16)
            cb = lax.dot_general(cg, bg, NT, preferred_element_type=F32)
            for j in range(pairs_per_group):
                p = g * pairs_per_group + j
                h = 2 * p
                pcols = slice(p * LANES, (p + 1) * LANES)
                _, _, mcat = _pair_decay(tm_, cb, h)
                xdt = xs_ref[:, pcols] * _pair_cols(lo, tm_["dt"], h)
                ydiag = jnp.dot(mcat.astype(BF16), _block_diag(lo, xdt), preferred_element_type=F32)
                st = state[p]
                sin_ref[0, p] = st
                yoff = jnp.dot(cg, st.astype(BF16), preferred_element_type=F32) * _pair_cols(lo, tm_["eacs"], h)
                y_ref[:, pcols] = ydiag + yoff
                xw = (xdt * _pair_cols(lo, tm_["dstate"], h)).astype(BF16)
                state[p] = st * _pair_cols(lo, tm_["cdec"], h) + lax.dot_general(bg, xw, TN, preferred_element_type=F32)

    vec = pl.BlockSpec((1, LANES), lambda c: (0, 0))
    return pl.pallas_call(
        body, name="ssd_fwd", grid=(nc,),
        in_specs=[pl.BlockSpec((q, sw), lambda c: (c, 0)),
                  pl.BlockSpec((q, gn), lambda c: (c, bblk)),
                  pl.BlockSpec((q, gn), lambda c: (c, bblk + 1)),
                  pl.BlockSpec((q, LANES), lambda c: (c, 0)), vec, vec],
        out_specs=[pl.BlockSpec((q, sw), lambda c: (c, 0)),
                   pl.BlockSpec((1, n_pairs, SSD_STATE, LANES), lambda c: (c, 0, 0, 0))],
        out_shape=[SDS((t, sw), F32), SDS((nc, n_pairs, SSD_STATE, LANES), F32)],
        scratch_shapes=[pltpu.VMEM((n_pairs, SSD_STATE, LANES), F32)],
        compiler_params=_params(("arbitrary",)),
    )(xbc, xbc, xbc, dt_raw, dt_bias, a_log)


def ssd_bwd(dy, xbc, dt_raw, dt_bias, a_log, d_full, s_in, n_heads):
    t = xbc.shape[0]
    q = SSD_CHUNK
    gn = SSD_GROUPS * SSD_STATE
    sw = n_heads * SSD_HEAD_DIM
    n_pairs = n_heads // 2
    pairs_per_group = n_pairs // SSD_GROUPS
    nc = t // q
    bblk = sw // gn

    def body(dy_ref, xs_ref, b_ref, c_ref, dtr_ref, bias_ref, a_ref, dsk_ref, sin_ref,
             dxs_ref, db_ref, dc_ref, ddtr_ref, dbias_ref, dalog_ref,
             dstate, tbuf, xbuf, rbuf, acc_a, acc_b):
        i = pl.program_id(0)

        @pl.when(i == 0)
        def _():
            dstate[...] = jnp.zeros_like(dstate)
            rbuf[...] = jnp.zeros_like(rbuf)
            acc_a[...] = jnp.zeros_like(acc_a)
            acc_b[...] = jnp.zeros_like(acc_b)

        tm_ = _ssd_chunk_terms(dtr_ref, bias_ref, a_ref, n_heads)
        lane = tm_["lane"]
        lo = lane < SSD_HEAD_DIM
        intra = jnp.zeros((q, LANES), F32)
        for g in range(SSD_GROUPS):
            gcols = slice(g * SSD_STATE, (g + 1) * SSD_STATE)
            bg = b_ref[:, gcols].astype(BF16)
            cg = c_ref[:, gcols].astype(BF16)
            cb = lax.dot_general(cg, bg, NT, preferred_element_type=F32)
            dcb = jnp.zeros((q, q), F32)
            db_acc = jnp.zeros((q, SSD_STATE), F32)
            dc_acc = jnp.zeros((q, SSD_STATE), F32)
            for j in range(pairs_per_group):
                p = g * pairs_per_group + j
                h = 2 * p
                pcols = slice(p * LANES, (p + 1) * LANES)
                l0, l1, mcat = _pair_decay(tm_, cb, h)
                xp = xs_ref[:, pcols]
                dtp = _pair_cols(lo, tm_["dt"], h)
                xdt = xp * dtp
                xbd = _block_diag(lo, xdt)
                dyp = dy_ref[:, pcols]
                dyb = dyp.astype(BF16)
                dsb = _pair_cols(lo, tm_["dstate"], h)
                cdr = _pair_cols(lo, tm_["cdec"], h)
                eb = _pair_cols(lo, tm_["eacs"], h)
                st = sin_ref[0, p]
                stb = st.astype(BF16)
                dst = dstate[p]
                dstb = dst.astype(BF16)
                dye = (dyp * eb).astype(BF16)
                both = lax.dot_general(mcat.astype(BF16), dyb, TN, preferred_element_type=F32)
                dx_state = jnp.dot(bg, dstb, preferred_element_type=F32) * dsb
                dxdt = jnp.where(lo, both[:q], both[q:]) + dx_state
                dmcat = lax.dot_general(dyb, xbd, NT, preferred_element_type=F32)
                dcb = dcb + dmcat[:, :q] * l0 + dmcat[:, q:] * l1
                dseg = dmcat * mcat
                csum = _column_sums(dseg)
                intra = (intra
                         + jnp.where(lane == h, jnp.sum(dseg[:, :q], axis=1, keepdims=True) - csum[:q], 0.0)
                         + jnp.where(lane == h + 1, jnp.sum(dseg[:, q:], axis=1, keepdims=True) - csum[q:], 0.0))
                dc_acc = dc_acc + lax.dot_general(dye, stb, NT, preferred_element_type=F32)
                db_acc = db_acc + lax.dot_general((xdt * dsb).astype(BF16), dstb, NT, preferred_element_type=F32)
                yoff = jnp.dot(cg, stb, preferred_element_type=F32) * eb
                tbuf[:, pcols] = dyp * yoff - xdt * dx_state
                xbuf[:, pcols] = dxdt * xp
                rbuf[0:1, pcols] = (jnp.sum(xdt * dx_state, axis=0, keepdims=True)
                                    + cdr * jnp.sum(dst * st, axis=0, keepdims=True))
                dxs_ref[:, pcols] = dxdt * dtp + dyp * dsk_ref[:, pcols]
                dstate[p] = dst * cdr + lax.dot_general(cg, dye, TN, preferred_element_type=F32)
            dcbb = dcb.astype(BF16)
            dc_ref[:, gcols] = dc_acc + jnp.dot(dcbb, bg, preferred_element_type=F32)
            db_ref[:, gcols] = db_acc + lax.dot_general(dcbb, cg, TN, preferred_element_type=F32)

        sel = _head_selector(sw, SSD_HEAD_DIM)
        dacs = intra + _split_dot(tbuf[...], sel)
        carry = _split_dot(rbuf[...], sel)[0:1]
        anti = jnp.logical_not(tm_["causal"]) | (lax.broadcasted_iota(jnp.int32, (q, q), 0)
                                                 == lax.broadcasted_iota(jnp.int32, (q, q), 1))
        da = jnp.dot(anti.astype(F32), dacs, precision=HIGHEST, preferred_element_type=F32) + carry
        ddt = da * tm_["a"] + _split_dot(xbuf[...], sel)
        ddtr = jnp.where(tm_["lane"] < n_heads, ddt * jax.nn.sigmoid(tm_["pre"]), 0.0)
        ddtr_ref[...] = ddtr.astype(BF16)
        acc_b[...] += jnp.sum(ddtr, axis=0, keepdims=True)
        acc_a[...] += jnp.sum(da * tm_["dt"], axis=0, keepdims=True)

        @pl.when(i == nc - 1)
        def _():
            dbias_ref[...] = acc_b[...]
            dalog_ref[...] = acc_a[...] * tm_["a"]

    vec = pl.BlockSpec((1, LANES), lambda i: (0, 0))
    wide = pl.BlockSpec((q, sw), lambda i: (nc - 1 - i, 0))
    return pl.pallas_call(
        body, name="ssd_bwd", grid=(nc,),
        in_specs=[wide, wide,
                  pl.BlockSpec((q, gn), lambda i: (nc - 1 - i, bblk)),
                  pl.BlockSpec((q, gn), lambda i: (nc - 1 - i, bblk + 1)),
                  pl.BlockSpec((q, LANES), lambda i: (nc - 1 - i, 0)), vec, vec,
                  pl.BlockSpec((1, sw), lambda i: (0, 0)),
                  pl.BlockSpec((1, n_pairs, SSD_STATE, LANES), lambda i: (nc - 1 - i, 0, 0, 0))],
        out_specs=[wide, pl.BlockSpec((q, gn), lambda i: (nc - 1 - i, 0)), pl.BlockSpec((q, gn), lambda i: (nc - 1 - i, 0)),
                   pl.BlockSpec((q, LANES), lambda i: (nc - 1 - i, 0)), vec, vec],
        out_shape=[SDS((t, sw), F32), SDS((t, gn), F32), SDS((t, gn), F32), SDS((t, LANES), BF16),
                   SDS((1, LANES), F32), SDS((1, LANES), F32)],
        scratch_shapes=[pltpu.VMEM((n_pairs, SSD_STATE, LANES), F32), pltpu.VMEM((q, sw), F32), pltpu.VMEM((q, sw), F32),
                        pltpu.VMEM((8, sw), F32), pltpu.VMEM((1, LANES), F32), pltpu.VMEM((1, LANES), F32)],
        compiler_params=_params(("arbitrary",)),
    )(dy, xbc, xbc, xbc, dt_raw, dt_bias, a_log, d_full, s_in)


def _gated(y_ref, xs_ref, z_ref, dsk_ref):
    y1 = y_ref[...] + dsk_ref[...] * xs_ref[...]
    return y1, y1 * _silu(z_ref[...])


def gate_norm_fwd(y, xbc, proj, d_full, norm_w, mixed, z_block, tm):
    t, sw = y.shape
    gw = sw // SSD_GROUPS

    def body(y_ref, xs_ref, z_ref, dsk_ref, nw_ref, alias_ref, o_ref):
        _, y2 = _gated(y_ref, xs_ref, z_ref, dsk_ref)
        for g in range(SSD_GROUPS):
            cols = slice(g * gw, (g + 1) * gw)
            blk = y2[:, cols]
            r = lax.rsqrt(jnp.mean(blk * blk, axis=-1, keepdims=True) + NORM_EPS)
            o_ref[:, cols] = (blk * r * nw_ref[:, cols]).astype(BF16)

    row = pl.BlockSpec((tm, sw), lambda i: (i, 0))
    vec = pl.BlockSpec((1, sw), lambda i: (0, 0))
    return pl.pallas_call(
        body, name="gate_norm_fwd", grid=(t // tm,),
        in_specs=[row, row, pl.BlockSpec((tm, sw), lambda i: (i, z_block)), vec, vec, pl.BlockSpec(memory_space=pl.ANY)],
        out_specs=pl.BlockSpec((tm, sw), lambda i: (i, 1)),
        out_shape=SDS(mixed.shape, mixed.dtype),
        input_output_aliases={5: 0},
        compiler_params=_params(("arbitrary",)),
    )(y, xbc, proj, d_full, norm_w, mixed)


def gate_norm_bwd(dmixed, y, xbc, proj, d_full, norm_w, dproj, z_block, tm):
    t, sw = y.shape
    gw = sw // SSD_GROUPS
    nt = t // tm

    def body(d_ref, y_ref, xs_ref, z_ref, dsk_ref, nw_ref, alias_ref, dy_ref, dz_ref, dnw_ref, dd_ref, acc_d):
        i = pl.program_id(0)

        @pl.when(i == 0)
        def _():
            dnw_ref[...] = jnp.zeros_like(dnw_ref)
            acc_d[...] = jnp.zeros_like(acc_d)

        y1, y2 = _gated(y_ref, xs_ref, z_ref, dsk_ref)
        d3 = d_ref[...]
        parts = []
        for g in range(SSD_GROUPS):
            cols = slice(g * gw, (g + 1) * gw)
            blk = y2[:, cols]
            r = lax.rsqrt(jnp.mean(blk * blk, axis=-1, keepdims=True) + NORM_EPS)
            n = blk * r
            dg = d3[:, cols]
            dnw_ref[:, cols] += jnp.sum(dg * n, axis=0, keepdims=True)
            parts.append(_norm_bwd(dg * nw_ref[:, cols], n, r))
        dy2 = jnp.concatenate(parts, axis=1)
        zv = z_ref[...]
        dz_ref[...] = (dy2 * y1 * _dsilu(zv)).astype(BF16)
        dy1 = dy2 * _silu(zv)
        dy_ref[...] = dy1
        acc_d[0:1, :] += jnp.sum(dy1 * xs_ref[...], axis=0, keepdims=True)

        @pl.when(i == nt - 1)
        def _():
            dd_ref[...] = _split_dot(acc_d[...], _head_selector(sw, SSD_HEAD_DIM))[0:1]

    row = pl.BlockSpec((tm, sw), lambda i: (i, 0))
    vec = pl.BlockSpec((1, sw), lambda i: (0, 0))
    return pl.pallas_call(
        body, name="gate_norm_bwd", grid=(nt,),
        in_specs=[pl.BlockSpec((tm, sw), lambda i: (i, 1)), row, row, pl.BlockSpec((tm, sw), lambda i: (i, z_block)),
                  vec, vec, pl.BlockSpec(memory_space=pl.ANY)],
        out_specs=[row, pl.BlockSpec((tm, sw), lambda i: (i, z_block)), vec, pl.BlockSpec((1, LANES), lambda i: (0, 0))],
        out_shape=[SDS((t, sw), F32), SDS(dproj.shape, dproj.dtype), SDS((1, sw), F32), SDS((1, LANES), F32)],
        scratch_shapes=[pltpu.VMEM((8, sw), F32)],
        input_output_aliases={6: 1},
        compiler_params=_params(("arbitrary",)),
    )(dmixed, y, xbc, proj, d_full, norm_w, dproj)


GATE_BLOCK = 1
Z_BLOCK = 2
CONV_BLOCK = 2


def _tiles(t):
    return min(256, t), min(512, t)


def layer_fwd(x, p):
    t = x.shape[0]
    tm, tmm = _tiles(t)
    n_heads = p["d_full"].shape[1] // SSD_HEAD_DIM
    h, r_pre = rms_fwd(x, p["pre_w"], tm)
    proj = mm_nn(h, p["w_main"], F32, tmm, 1024, "in_proj")
    dt_raw = mm_nn(h, p["w_dt"], F32, tmm, LANES, "dt_proj")
    mixed = pool_fwd(proj, p["mixw"], p["pscale"], tm)
    xbc = conv_fwd(proj, p["conv_w"], p["conv_b"], CONV_BLOCK, tm)
    y, s_in = ssd_fwd(xbc, dt_raw, p["dt_bias"], p["a_log"], n_heads)
    mixed = gate_norm_fwd(y, xbc, proj, p["d_full"], p["norm_w"], mixed, Z_BLOCK, tm)
    out = mm_nn(mixed, p["w_out"], F32, tmm, 512, "out_proj")
    x_next, r_post = post_fwd(out, x, p["post_w"], tm)
    return x_next, dict(x=x, h=h, r_pre=r_pre, proj=proj, dt_raw=dt_raw, xbc=xbc, y=y, s_in=s_in, mixed=mixed,
                        out=out, r_post=r_post)


def layer_bwd(g, s, p):
    t = g.shape[0]
    tm, tmm = _tiles(t)
    d = g.shape[1]
    n_heads = p["d_full"].shape[1] // SSD_HEAD_DIM
    d_out, d_post = post_bwd(g, s["out"], s["r_post"], p["post_w"], tm)
    dmixed = mm_nt(d_out, p["w_out"], F32, tmm, 1024, d, "d_mixed")
    dw_out = mm_tn(s["mixed"], d_out, 512, 512, "dw_out")
    dproj, dq, d_pscale, d_mixw = pool_bwd_a(dmixed, s["proj"], p["mixw"], p["pscale"], tm)
    dproj = pool_bwd_b(dq, dproj, tm)
    dy, dproj, d_norm, d_dskip = gate_norm_bwd(dmixed, s["y"], s["xbc"], s["proj"], p["d_full"], p["norm_w"], dproj,
                                               Z_BLOCK, tm)
    dxs, db, dc, ddtr, d_dtb, d_alog = ssd_bwd(dy, s["xbc"], s["dt_raw"], p["dt_bias"], p["a_log"], p["d_full"],
                                               s["s_in"], n_heads)
    dpre, d_convw, d_convb = conv_bwd_a(dxs, db, dc, s["proj"], p["conv_w"], p["conv_b"], CONV_BLOCK, tm)
    dproj = conv_bwd_b(dpre, p["conv_w"], dproj, CONV_BLOCK, tm)
    dh_main = mm_nt(dproj, p["w_main"], F32, tmm, d, 1024, "dh_main")
    dh_dt = mm_nt(ddtr, p["w_dt"], F32, tmm, d, LANES, "dh_dt")
    dw_main = mm_tn(s["h"], dproj, 512, 1024, "dw_main")
    dw_dt = mm_tn(s["h"], ddtr, 512, LANES, "dw_dt")
    gx, d_pre = rms_bwd(dh_main, dh_dt, s["x"], s["r_pre"], p["pre_w"], g, tm)
    grads = dict(pre_w=d_pre, w_main=dw_main, w_dt=dw_dt, mixw=d_mixw, pscale=d_pscale, conv_w=d_convw, conv_b=d_convb,
                 dt_bias=d_dtb, a_log=d_alog, d_skip=d_dskip, norm_w=d_norm, w_out=dw_out, post_w=d_post)
    return gx, grads


def _place():
    x, y, c = lax.axis_index("x"), lax.axis_index("y"), lax.axis_index("c")
    return x, y, c, [(1 - x, y), (x, 1 - y), (1 - x, 1 - y)]


def _two_level_gather(x_refs, out_slots, send_sems, recv_sems, local_sems):
    x, y, c, chips = _place()
    me, sibling = (x, y, c), (x, y, 1 - c)
    n = len(x_refs)

    def copy(a, k, block, to, src=None):
        return pltpu.make_async_remote_copy(
            src_ref=out_slots[a](*block) if src is None else src, dst_ref=out_slots[a](*block),
            send_sem=send_sems.at[7 * a + k], recv_sem=recv_sems.at[7 * a + k], device_id=to, device_id_type=MESH)

    mine = [pltpu.make_async_copy(x_refs[a], out_slots[a](*me), local_sems.at[a]) for a in range(n)]
    for cp in mine:
        cp.start()
    first = []
    for a in range(n):
        first.append(copy(a, 0, me, sibling, src=x_refs[a]))
        first += [copy(a, 1 + j, me, (*chip, c), src=x_refs[a]) for j, chip in enumerate(chips)]
    for cp in first:
        cp.start()
    passed = []
    for j, chip in enumerate(chips):
        for a in range(n):
            copy(a, 1 + j, (*chip, c), me).wait_recv()
            fwd = copy(a, 4 + j, (*chip, c), sibling)
            fwd.start()
            passed.append(fwd)
    for a in range(n):
        copy(a, 0, sibling, me).wait_recv()
        for j, chip in enumerate(chips):
            copy(a, 4 + j, (*chip, 1 - c), me).wait_recv()
    for cp in first + passed:
        cp.wait_send()
    for cp in mine:
        cp.wait()


def all_gather_hbm(shards, name):
    n = len(shards)

    def body(*refs):
        x_refs, out_refs = refs[:n], refs[n:2 * n]
        send_sems, recv_sems, local_sems = refs[2 * n:]
        slots = [lambda px, py, pc, o=o: o.at[:, 4 * px + 2 * py + pc] for o in out_refs]
        _two_level_gather(x_refs, slots, send_sems, recv_sems, local_sems)

    hbm = pl.BlockSpec(memory_space=pl.ANY)
    return pl.pallas_call(
        body, name=name,
        out_shape=[SDS((s.shape[0], N_DEV) + s.shape[1:], s.dtype) for s in shards],
        in_specs=[hbm] * n, out_specs=[hbm] * n,
        scratch_shapes=[pltpu.SemaphoreType.DMA((7 * n,)), pltpu.SemaphoreType.DMA((7 * n,)), pltpu.SemaphoreType.DMA((n,))],
    )(*shards)


def all_gather_vmem(block, name):
    r, c_ = block.shape

    def body(x_ref, out_ref, send_sems, recv_sems, local_sems):
        _two_level_gather([x_ref], [lambda px, py, pc: out_ref.at[4 * px + 2 * py + pc]], send_sems, recv_sems, local_sems)

    return pl.pallas_call(
        body, name=name, out_shape=SDS((N_DEV, r, c_), block.dtype),
        in_specs=[pl.BlockSpec(memory_space=pltpu.VMEM)], out_specs=pl.BlockSpec(memory_space=pltpu.VMEM),
        scratch_shapes=[pltpu.SemaphoreType.DMA((7,)), pltpu.SemaphoreType.DMA((7,)), pltpu.SemaphoreType.DMA((1,))],
        compiler_params=_params(),
    )(block)


def exchange_sibling(sends, name):
    n = len(sends)

    def body(*refs):
        src, dst = refs[:n], refs[n:2 * n]
        send_sems, recv_sems = refs[2 * n:]
        x, y, c, _ = _place()
        cps = [pltpu.make_async_remote_copy(src_ref=src[a], dst_ref=dst[a], send_sem=send_sems.at[a], recv_sem=recv_sems.at[a],
                                            device_id=(x, y, 1 - c), device_id_type=MESH) for a in range(n)]
        for cp in cps:
            cp.start()
        for cp in cps:
            cp.wait()

    hbm = pl.BlockSpec(memory_space=pl.ANY)
    return pl.pallas_call(
        body, name=name, out_shape=[SDS(s.shape, s.dtype) for s in sends],
        in_specs=[hbm] * n, out_specs=[hbm] * n,
        scratch_shapes=[pltpu.SemaphoreType.DMA((n,)), pltpu.SemaphoreType.DMA((n,))],
    )(*sends)


def scatter_chips(slabs, name):
    n = len(slabs)

    def body(*refs):
        src, dst = refs[:n], refs[n:2 * n]
        send_sems, recv_sems = refs[2 * n:]
        x, y, c, chips = _place()
        mychip = 2 * x + y
        cps = []
        for a in range(n):
            for j, (px, py) in enumerate(chips):
                k = 2 * px + py
                slot = lax.rem(mychip - k + 4, 4) - 1
                cps.append(pltpu.make_async_remote_copy(
                    src_ref=src[a].at[k], dst_ref=dst[a].at[slot], send_sem=send_sems.at[3 * a + j],
                    recv_sem=recv_sems.at[3 * a + j], device_id=(px, py, c), device_id_type=MESH))
        for cp in cps:
            cp.start()
        for cp in cps:
            cp.wait()

    hbm = pl.BlockSpec(memory_space=pl.ANY)
    return pl.pallas_call(
        body, name=name, out_shape=[SDS((3,) + s.shape[1:], s.dtype) for s in slabs],
        in_specs=[hbm] * n, out_specs=[hbm] * n,
        scratch_shapes=[pltpu.SemaphoreType.DMA((3 * n,)), pltpu.SemaphoreType.DMA((3 * n,))],
    )(*slabs)


def _adamw(w, g, m, v):
    m = ADAM_B1 * m + (1.0 - ADAM_B1) * g
    v = ADAM_B2 * v + (1.0 - ADAM_B2) * jnp.square(g)
    m_hat = m / (1.0 - ADAM_B1 ** ADAM_STEP)
    v_hat = v / (1.0 - ADAM_B2 ** ADAM_STEP)
    delta = -ADAM_LR * (m_hat / (jnp.sqrt(v_hat) + ADAM_EPS) + ADAM_WD * w)
    return delta, m, v


def pair_sum(own, got, tr, name):
    k, r, c_ = own.shape

    def body(a_ref, b_ref, o_ref):
        o_ref[...] = (a_ref[...] + b_ref[...].astype(F32)).astype(BF16)

    blk = pl.BlockSpec((pl.Squeezed(), tr, c_), lambda kk, i: (kk, i, 0))
    return pl.pallas_call(
        body, name=name, grid=(k, r // tr), in_specs=[blk, blk], out_specs=blk, out_shape=SDS(own.shape, BF16),
        compiler_params=_params(("arbitrary", "arbitrary")),
    )(own, got)


def reduce_adam(own, got_sibling, got_chips, w, m, v, prev, layer, tr, name):
    nl, r, c_ = w.shape

    def body(own_ref, sib_ref, c0_ref, c1_ref, c2_ref, w_ref, m_ref, v_ref, *rest):
        g_ref, d_ref, nm_ref, nv_ref = rest[-4:]
        g = (own_ref[...] + sib_ref[...].astype(F32) + c0_ref[...].astype(F32) + c1_ref[...].astype(F32)
             + c2_ref[...].astype(F32))
        delta, nm, nv = _adamw(w_ref[...], g, m_ref[...], v_ref[...])
        g_ref[...] = g
        d_ref[...] = delta
        nm_ref[...] = nm
        nv_ref[...] = nv

    row = pl.BlockSpec((tr, c_), lambda i: (i, 0))
    lay = pl.BlockSpec((pl.Squeezed(), tr, c_), lambda i: (layer, i, 0))
    chips = [pl.BlockSpec((pl.Squeezed(), tr, c_), lambda i, s=s: (s, i, 0)) for s in range(3)]
    in_specs = [row, row] + chips + [lay, lay, lay]
    args = [own, got_sibling, got_chips, got_chips, got_chips, w, m, v]
    aliases = {}
    if prev is not None:
        in_specs += [pl.BlockSpec(memory_space=pl.ANY)] * 4
        aliases = {len(args) + k: k for k in range(4)}
        args += list(prev)
    return pl.pallas_call(
        body, name=name, grid=(r // tr,), in_specs=in_specs, out_specs=[lay] * 4,
        out_shape=[SDS((nl, r, c_), F32)] * 4, input_output_aliases=aliases,
        compiler_params=_params(("arbitrary",)),
    )(*args)


def sum_devices(packs):
    n, r, c_ = packs.shape

    def body(p_ref, o_ref):
        acc = p_ref[0]
        for k in range(1, n):
            acc = acc + p_ref[k]
        o_ref[...] = acc

    return pl.pallas_call(body, name="sum_devices", out_shape=SDS((r, c_), F32), compiler_params=_params())(packs)


def adam_small(w, g, m, v):
    def body(w_ref, g_ref, m_ref, v_ref, d_ref, nm_ref, nv_ref):
        delta, nm, nv = _adamw(w_ref[...], g_ref[...], m_ref[...], v_ref[...])
        d_ref[...] = delta
        nm_ref[...] = nm
        nv_ref[...] = nv

    return pl.pallas_call(body, name="adam_small", out_shape=[SDS(w.shape, F32)] * 3, compiler_params=_params())(w, g, m, v)


SMALL = ("pre_norm_w", "pool_scale", "conv_b", "dt_bias", "a_log", "d_skip", "_pad", "ssd_norm_w", "post_norm_w", "conv_w")


def _pack(parts):
    flat = jnp.concatenate([parts[k] for k in SMALL], axis=1).reshape(-1, LANES)
    return jnp.pad(flat, ((0, (-flat.shape[0]) % 8), (0, 0)))


def _unpack(pack, sizes, nl):
    total = sum(sizes[k] for k in SMALL)
    flat = pack[: nl * total // LANES].reshape(nl, total)
    out, o = {}, 0
    for k in SMALL:
        out[k] = flat[:, o:o + sizes[k]]
        o += sizes[k]
    return out


def kernel(x, pre_norm_w, w_in, pool_mix_w, pool_scale, conv_w, conv_b, dt_bias, a_log, d_skip, ssd_norm_w, w_out, post_norm_w, loss_target, m_pre_norm_w, m_w_in, m_pool_mix_w, m_pool_scale, m_conv_w, m_conv_b, m_dt_bias, m_a_log, m_d_skip, m_ssd_norm_w, m_w_out, m_post_norm_w, v_pre_norm_w, v_w_in, v_pool_mix_w, v_pool_scale, v_conv_w, v_conv_b, v_dt_bias, v_a_log, v_d_skip, v_ssd_norm_w, v_w_out, v_post_norm_w):
    cx, cy, cc = lax.axis_index("x"), lax.axis_index("y"), lax.axis_index("c")
    me = 4 * cx + 2 * cy + cc
    mychip = 2 * cx + cy
    nl, d, cols = w_in.shape
    t = x.shape[1]
    n_heads = a_log.shape[1]
    sw = n_heads * SSD_HEAD_DIM
    pw = pool_scale.shape[1]
    cd = conv_b.shape[1]
    ng, gsh, gw = pool_mix_w.shape[1:]
    e_main = N_DEV * cols - n_heads
    assert x.shape[0] == 1 and pw == sw and cd == sw + 2 * SSD_GROUPS * SSD_STATE and e_main == 2 * pw + sw + cd
    assert 2 * pw + sw == CONV_BLOCK * cd and n_heads <= LANES and t % SSD_CHUNK == 0 and gsh * N_DEV == gw
    tm, _ = _tiles(t)

    g_in, g_out, g_mix, g_conv = all_gather_hbm(
        [w_in.astype(BF16), w_out.astype(BF16), pool_mix_w.astype(BF16), conv_w], "gather_weights")
    wcat = g_in.transpose(0, 2, 1, 3).reshape(nl, d, N_DEV * cols)
    w_main = wcat[:, :, :e_main]
    w_dt = jnp.pad(wcat[:, :, e_main:], ((0, 0), (0, 0), (0, LANES - n_heads)))
    w_out_full = g_out.reshape(nl, N_DEV * w_out.shape[1], d)
    mixw = g_mix.transpose(0, 2, 1, 3, 4).reshape(nl, ng, gw, gw)
    convw = g_conv.transpose(0, 2, 1, 3).reshape(nl, CONV_WIDTH, cd)
    pad_h = ((0, 0), (0, LANES - n_heads))
    params = [dict(pre_w=pre_norm_w[l:l + 1], w_main=w_main[l], w_dt=w_dt[l], mixw=mixw[l], pscale=pool_scale[l:l + 1],
                   conv_w=convw[l], conv_b=conv_b[l:l + 1], dt_bias=jnp.pad(dt_bias[l:l + 1], pad_h),
                   a_log=jnp.pad(a_log[l:l + 1], pad_h), d_full=jnp.repeat(d_skip[l:l + 1], SSD_HEAD_DIM, axis=1),
                   norm_w=ssd_norm_w[l:l + 1], w_out=w_out_full[l], post_w=post_norm_w[l:l + 1]) for l in range(nl)]

    xs = x[0]
    saved = []
    for l in range(nl):
        xs, s = layer_fwd(xs, params[l])
        saved.append(s)
    loss_part, g = loss_grad(xs, loss_target[0], tm)
    loss = lax.psum(loss_part[0, 0], ("x", "y", "c"))

    big = {"w_in": (w_in, m_w_in, v_w_in), "w_out": (w_out, m_w_out, v_w_out),
           "pool_mix_w": tuple(a.reshape(nl, ng * gsh, gw) for a in (pool_mix_w, m_pool_mix_w, v_pool_mix_w))}
    big_out = {k: None for k in big}
    small_g = [None] * nl
    for l in reversed(range(nl)):
        g, gr = layer_bwd(g, saved[l], params[l])
        dwc = jnp.concatenate([gr["w_main"], gr["w_dt"][:, :n_heads]], axis=1).reshape(d, 4, 2, cols)
        halves = {"w_in": lambda ci: lax.dynamic_index_in_dim(dwc, ci, 2, keepdims=False).transpose(1, 0, 2),
                  "w_out": lambda ci: lax.dynamic_index_in_dim(gr["w_out"].reshape(4, 2, -1, d), ci, 1, keepdims=False),
                  "pool_mix_w": lambda ci: lax.dynamic_index_in_dim(
                      gr["mixw"].reshape(ng, 4, 2, gsh, gw), ci, 2, keepdims=False).transpose(1, 0, 2, 3).reshape(4, ng * gsh, gw)}
        names = list(big)
        own = {k: halves[k](cc) for k in names}
        got_sib = dict(zip(names, exchange_sibling([halves[k](1 - cc).astype(BF16) for k in names], "grads_to_sibling")))
        chip_sums = [pair_sum(own[k], got_sib[k], min(256, own[k].shape[1]), "pair_sum_" + k) for k in names]
        got_chips = dict(zip(names, scatter_chips(chip_sums, "grads_to_chips")))
        for k in names:
            wk, mk, vk = big[k]
            big_out[k] = reduce_adam(lax.dynamic_index_in_dim(own[k], mychip, 0, keepdims=False),
                                     lax.dynamic_index_in_dim(got_sib[k], mychip, 0, keepdims=False),
                                     got_chips[k], wk, mk, vk, big_out[k], l, min(256, wk.shape[1]), "reduce_adam_" + k)
        small_g[l] = dict(pre_norm_w=gr["pre_w"], pool_scale=gr["pscale"], conv_b=gr["conv_b"], dt_bias=gr["dt_bias"][:, :n_heads],
                          a_log=gr["a_log"][:, :n_heads], d_skip=gr["d_skip"][:, :n_heads], _pad=jnp.zeros((1, LANES - 3 * n_heads), F32),
                          ssd_norm_w=gr["norm_w"], post_norm_w=gr["post_w"], conv_w=gr["conv_w"].reshape(1, CONV_WIDTH * cd))

    sizes = {k: small_g[0][k].shape[1] for k in SMALL}
    gsum = sum_devices(all_gather_vmem(_pack({k: jnp.concatenate([sg[k] for sg in small_g], axis=0) for k in SMALL}),
                                       "gather_small_grads"))
    gs = _unpack(gsum, sizes, nl)
    csh = conv_w.shape[2]
    gs["conv_w"] = lax.dynamic_slice_in_dim(gs["conv_w"].reshape(nl, CONV_WIDTH, cd), me * csh, csh, axis=2).reshape(nl, -1)
    lsizes = dict(sizes, conv_w=CONV_WIDTH * csh)
    zpad = jnp.zeros((nl, sizes["_pad"]), F32)

    def local(pre, scale, cb, dtb, al, dsk, nw, post, cw):
        return _pack(dict(pre_norm_w=pre, pool_scale=scale, conv_b=cb, dt_bias=dtb, a_log=al, d_skip=dsk, _pad=zpad,
                          ssd_norm_w=nw, post_norm_w=post, conv_w=cw.reshape(nl, -1)))

    wp = local(pre_norm_w, pool_scale, conv_b, dt_bias, a_log, d_skip, ssd_norm_w, post_norm_w, conv_w)
    mp = local(m_pre_norm_w, m_pool_scale, m_conv_b, m_dt_bias, m_a_log, m_d_skip, m_ssd_norm_w, m_post_norm_w, m_conv_w)
    vp = local(v_pre_norm_w, v_pool_scale, v_conv_b, v_dt_bias, v_a_log, v_d_skip, v_ssd_norm_w, v_post_norm_w, v_conv_w)
    small_out = [gs] + [_unpack(o, lsizes, nl) for o in adam_small(wp, _pack(gs), mp, vp)]

    def leaf(kind, name):
        if name in big:
            return big_out[name][kind].reshape(big[name][0].shape if name != "pool_mix_w" else pool_mix_w.shape)
        val = small_out[kind][name]
        return val.reshape(conv_w.shape) if name == "conv_w" else val

    order = ("pre_norm_w", "w_in", "pool_mix_w", "pool_scale", "conv_w", "conv_b", "dt_bias", "a_log", "d_skip",
             "ssd_norm_w", "w_out", "post_norm_w")
    return (loss, g[None]) + tuple(leaf(kind, name) for kind in range(4) for name in order)
```

```python
import jax
import jax.numpy as jnp
from jax import lax
from jax.experimental import pallas as pl
from jax.experimental.pallas import tpu as pltpu

F32 = jnp.float32
BF16 = jnp.bfloat16
SDS = jax.ShapeDtypeStruct
MESH = pl.DeviceIdType.MESH
HIGHEST = lax.Precision.HIGHEST

NORM_EPS = 1e-6
POOL_WINDOWS = (2, 4, 8, 16)
POOL_HALO = 16
CONV_WIDTH = 4
CONV_HALO = 8
SSD_CHUNK = 128
SSD_HEAD_DIM = 64
SSD_STATE = 128
SSD_GROUPS = 4
LANES = 128
N_DEV = 8

ADAM_LR = 0.001
ADAM_B1 = 0.9
ADAM_B2 = 0.999
ADAM_EPS = 1e-08
ADAM_WD = 0.01
ADAM_STEP = 10

VMEM_LIMIT = 56 * 1024 * 1024

NT = (((1,), (1,)), ((), ()))
TN = (((0,), (0,)), ((), ()))


def _params(sem=None):
    kw = dict(vmem_limit_bytes=VMEM_LIMIT)
    if sem is not None:
        kw["dimension_semantics"] = sem
    return pltpu.CompilerParams(**kw)


def _silu(v):
    return v * jax.nn.sigmoid(v)


def _dsilu(v):
    s = jax.nn.sigmoid(v)
    return s * (1.0 + v * (1.0 - s))


def _split_dot(v, sel):
    hi = v.astype(BF16)
    lo = (v - hi.astype(F32)).astype(BF16)
    return (jnp.dot(hi, sel, preferred_element_type=F32) + jnp.dot(lo, sel, preferred_element_type=F32))


def _head_selector(width, per):
    ch = lax.broadcasted_iota(jnp.int32, (width, LANES), 0)
    hd = lax.broadcasted_iota(jnp.int32, (width, LANES), 1)
    return jnp.where((ch >= hd * per) & (ch < (hd + 1) * per), 1.0, 0.0).astype(BF16)


class Comm:
    def __init__(self, inputs, out_shapes, aliases, n_sems, make):
        self.inputs, self.out_shapes, self.aliases, self.n_sems, self.make = list(inputs), list(out_shapes), dict(aliases), n_sems, make


def _remote(src, dst, send_sems, recv_sems, k, peer):
    return pltpu.make_async_remote_copy(src_ref=src, dst_ref=dst, send_sem=send_sems.at[k], recv_sem=recv_sems.at[k],
                                        device_id=peer, device_id_type=MESH)


def _call(body, args, *, name, grid, in_specs, out_specs, out_shape, scratch_shapes=(), sem=None, comm=None):
    in_specs, out_specs, out_shape = list(in_specs), list(out_specs), list(out_shape)
    if comm is None:
        outs = pl.pallas_call(body, name=name, grid=grid, in_specs=in_specs, out_specs=out_specs, out_shape=out_shape,
                              scratch_shapes=list(scratch_shapes), compiler_params=_params(sem))(*args)
        return list(outs), []
    ni, no, nci, nco, ns = len(in_specs), len(out_specs), len(comm.inputs), len(comm.out_shapes), len(scratch_shapes)
    hbm = pl.BlockSpec(memory_space=pl.ANY)

    def hosted(*refs):
        ins, cins = refs[:ni], refs[ni:ni + nci]
        outs, couts = refs[ni + nci:ni + nci + no], refs[ni + nci + no:ni + nci + no + nco]
        scratch = refs[ni + nci + no + nco:]
        sends, locals_, arrivals = comm.make(cins, couts, scratch[ns], scratch[ns + 1])
        first = last = None if grid else True
        for axis, extent in enumerate(grid):
            pid = pl.program_id(axis)
            first = (pid == 0) if first is None else first & (pid == 0)
            last = (pid == extent - 1) if last is None else last & (pid == extent - 1)

        @pl.when(first)
        def _():
            for cp in locals_ + sends:
                cp.start()

        body(*ins, *outs, *scratch[:ns])

        @pl.when(last)
        def _():
            for cp in arrivals:
                cp.wait_recv()
            for cp in sends:
                cp.wait_send()
            for cp in locals_:
                cp.wait()

    outs = pl.pallas_call(
        hosted, name=name, grid=grid, in_specs=in_specs + [hbm] * nci, out_specs=out_specs + [hbm] * nco,
        out_shape=out_shape + comm.out_shapes,
        scratch_shapes=list(scratch_shapes) + [pltpu.SemaphoreType.DMA((comm.n_sems,)), pltpu.SemaphoreType.DMA((comm.n_sems,))],
        input_output_aliases={ni + k: no + v for k, v in comm.aliases.items()},
        compiler_params=_params(sem),
    )(*args, *comm.inputs)
    return list(outs[:no]), list(outs[no:])


def rms_fwd(x, w, tm):
    t, d = x.shape

    def body(x_ref, w_ref, h_ref, r_ref):
        xv = x_ref[...]
        r = lax.rsqrt(jnp.mean(xv * xv, axis=-1, keepdims=True) + NORM_EPS)
        h_ref[...] = (xv * r * w_ref[...]).astype(BF16)
        r_ref[...] = r

    return pl.pallas_call(
        body, name="rms_fwd", grid=(t // tm,),
        in_specs=[pl.BlockSpec((tm, d), lambda i: (i, 0)), pl.BlockSpec((1, d), lambda i: (0, 0))],
        out_specs=[pl.BlockSpec((tm, d), lambda i: (i, 0)), pl.BlockSpec((tm, 1), lambda i: (i, 0))],
        out_shape=[SDS((t, d), BF16), SDS((t, 1), F32)],
        compiler_params=_params(("arbitrary",)),
    )(x, w)


def post_fwd(out, x, w, tm):
    t, d = x.shape

    def body(o_ref, x_ref, w_ref, y_ref, r_ref):
        ov = o_ref[...]
        r = lax.rsqrt(jnp.mean(ov * ov, axis=-1, keepdims=True) + NORM_EPS)
        y_ref[...] = x_ref[...] + ov * r * w_ref[...]
        r_ref[...] = r

    return pl.pallas_call(
        body, name="post_fwd", grid=(t // tm,),
        in_specs=[pl.BlockSpec((tm, d), lambda i: (i, 0)), pl.BlockSpec((tm, d), lambda i: (i, 0)),
                  pl.BlockSpec((1, d), lambda i: (0, 0))],
        out_specs=[pl.BlockSpec((tm, d), lambda i: (i, 0)), pl.BlockSpec((tm, 1), lambda i: (i, 0))],
        out_shape=[SDS((t, d), F32), SDS((t, 1), F32)],
        compiler_params=_params(("arbitrary",)),
    )(out, x, w)


def _norm_bwd(g_n, n, r):
    return r * (g_n - n * jnp.mean(g_n * n, axis=-1, keepdims=True))


def post_bwd(g, out, r, w, tm):
    t, d = g.shape

    def body(g_ref, o_ref, r_ref, w_ref, do_ref, dw_ref):
        i = pl.program_id(0)
        gv = g_ref[...]
        rv = r_ref[...]
        n = o_ref[...] * rv
        part = jnp.sum(gv * n, axis=0, keepdims=True)

        @pl.when(i == 0)
        def _():
            dw_ref[...] = part

        @pl.when(i > 0)
        def _():
            dw_ref[...] += part

        do_ref[...] = _norm_bwd(gv * w_ref[...], n, rv).astype(BF16)

    return pl.pallas_call(
        body, name="post_bwd", grid=(t // tm,),
        in_specs=[pl.BlockSpec((tm, d), lambda i: (i, 0)), pl.BlockSpec((tm, d), lambda i: (i, 0)),
                  pl.BlockSpec((tm, 1), lambda i: (i, 0)), pl.BlockSpec((1, d), lambda i: (0, 0))],
        out_specs=[pl.BlockSpec((tm, d), lambda i: (i, 0)), pl.BlockSpec((1, d), lambda i: (0, 0))],
        out_shape=[SDS((t, d), BF16), SDS((1, d), F32)],
        compiler_params=_params(("arbitrary",)),
    )(g, out, r, w)


def rms_bwd(dh_a, dh_b, x, r, w, g, tm):
    t, d = x.shape

    def body(a_ref, b_ref, x_ref, r_ref, w_ref, g_ref, gx_ref, dw_ref):
        i = pl.program_id(0)
        dh = a_ref[...] + b_ref[...]
        rv = r_ref[...]
        n = x_ref[...] * rv
        part = jnp.sum(dh * n, axis=0, keepdims=True)

        @pl.when(i == 0)
        def _():
            dw_ref[...] = part

        @pl.when(i > 0)
        def _():
            dw_ref[...] += part

        gx_ref[...] = g_ref[...] + _norm_bwd(dh * w_ref[...], n, rv)

    row = pl.BlockSpec((tm, d), lambda i: (i, 0))
    return pl.pallas_call(
        body, name="rms_bwd", grid=(t // tm,),
        in_specs=[row, row, row, pl.BlockSpec((tm, 1), lambda i: (i, 0)), pl.BlockSpec((1, d), lambda i: (0, 0)), row],
        out_specs=[row, pl.BlockSpec((1, d), lambda i: (0, 0))],
        out_shape=[SDS((t, d), F32), SDS((1, d), F32)],
        compiler_params=_params(("arbitrary",)),
    )(dh_a, dh_b, x, r, w, g)


def loss_grad(y, target, tm):
    t, d = y.shape

    def body(y_ref, t_ref, l_ref, g_ref):
        i = pl.program_id(0)
        err = y_ref[...] - t_ref[...]
        g_ref[...] = err / d
        part = 0.5 * jnp.sum(jnp.mean(err * err, axis=-1, keepdims=True), axis=0, keepdims=True)

        @pl.when(i == 0)
        def _():
            l_ref[...] = part

        @pl.when(i > 0)
        def _():
            l_ref[...] += part

    row = pl.BlockSpec((tm, d), lambda i: (i, 0))
    return pl.pallas_call(
        body, name="loss_grad", grid=(t // tm,), in_specs=[row, row],
        out_specs=[pl.BlockSpec((1, 1), lambda i: (0, 0)), row],
        out_shape=[SDS((1, 1), F32), SDS((t, d), F32)],
        compiler_params=_params(("arbitrary",)),
    )(y, target)


def mm_nn(a, b, out_dtype, tm, tn, name, comm=None):
    m, k = a.shape
    n = b.shape[1]

    def body(a_ref, b_ref, o_ref):
        o_ref[...] = jnp.dot(a_ref[...], b_ref[...], preferred_element_type=F32).astype(out_dtype)

    outs, couts = _call(
        body, (a, b), name=name, grid=(n // tn, m // tm),
        in_specs=[pl.BlockSpec((tm, k), lambda j, i: (i, 0)), pl.BlockSpec((k, tn), lambda j, i: (0, j))],
        out_specs=[pl.BlockSpec((tm, tn), lambda j, i: (i, j))],
        out_shape=[SDS((m, n), out_dtype)], sem=("arbitrary", "arbitrary"), comm=comm)
    return outs[0], couts


def mm_nt(a, b, out_dtype, tm, tn, tk, name, comm=None):
    m, k = a.shape
    n = b.shape[0]
    nk = k // tk

    def body(a_ref, b_ref, o_ref, acc_ref):
        kk = pl.program_id(2)
        part = lax.dot_general(a_ref[...], b_ref[...], NT, preferred_element_type=F32)
        if nk == 1:
            o_ref[...] = part.astype(out_dtype)
        else:
            @pl.when(kk == 0)
            def _():
                acc_ref[...] = part

            @pl.when(kk > 0)
            def _():
                acc_ref[...] += part

            @pl.when(kk == nk - 1)
            def _():
                o_ref[...] = acc_ref[...].astype(out_dtype)

    outs, couts = _call(
        body, (a, b), name=name, grid=(m // tm, n // tn, nk),
        in_specs=[pl.BlockSpec((tm, tk), lambda i, j, kk: (i, kk)), pl.BlockSpec((tn, tk), lambda i, j, kk: (j, kk))],
        out_specs=[pl.BlockSpec((tm, tn), lambda i, j, kk: (i, j))],
        out_shape=[SDS((m, n), out_dtype)],
        scratch_shapes=[pltpu.VMEM((tm, tn) if nk > 1 else (8, LANES), F32)],
        sem=("arbitrary", "arbitrary", "arbitrary"), comm=comm)
    return outs[0], couts


def mm_tn(a, b, tm, tn, name):
    t, m = a.shape
    n = b.shape[1]

    def body(a_ref, b_ref, o_ref):
        o_ref[...] = lax.dot_general(a_ref[...], b_ref[...], TN, preferred_element_type=F32)

    return pl.pallas_call(
        body, name=name, grid=(m // tm, n // tn),
        in_specs=[pl.BlockSpec((t, tm), lambda i, j: (0, i)), pl.BlockSpec((t, tn), lambda i, j: (0, j))],
        out_specs=pl.BlockSpec((tm, tn), lambda i, j: (i, j)),
        out_shape=SDS((m, n), F32),
        compiler_params=_params(("arbitrary", "arbitrary")),
    )(a, b)


def _window_sums(ext, n_rows, lookahead):
    def sh(v, k):
        return pltpu.roll(v, (n_rows - k) if lookahead else k, 0)
    s2 = ext + sh(ext, 1)
    s4 = s2 + sh(s2, 2)
    s8 = s4 + sh(s4, 4)
    s16 = s8 + sh(s8, 8)
    return (s2, s4, s8, s16)


def _pool_counts(i, tm, w):
    tpos = i * tm + lax.broadcasted_iota(jnp.int32, (tm, 1), 0)
    return jnp.minimum(tpos + 1, w).astype(F32)


def _pooled(uc_ref, up_ref, i, tm):
    cur = uc_ref[...]
    prev = jnp.where(i > 0, up_ref[...], 0.0)
    ext = jnp.concatenate([prev, cur], axis=0)
    return cur, _window_sums(ext, tm + POOL_HALO, False)


def pool_fwd(proj, mixw, scale, tm):
    t = proj.shape[0]
    pw = scale.shape[1]
    gw = pw // len(POOL_WINDOWS)
    nh = tm // POOL_HALO

    def body(uc_ref, up_ref, g_ref, w_ref, s_ref, o_ref):
        i = pl.program_id(0)
        cur, sums = _pooled(uc_ref, up_ref, i, tm)
        for g, w in enumerate(POOL_WINDOWS):
            cols = slice(g * gw, (g + 1) * gw)
            pooled = sums[g][POOL_HALO:, cols] / _pool_counts(i, tm, w) - cur[:, cols]
            mixed = jnp.dot(pooled.astype(BF16), w_ref[g], preferred_element_type=F32)
            o_ref[:, cols] = (mixed * s_ref[:, cols] * _silu(g_ref[:, cols])).astype(BF16)

    return pl.pallas_call(
        body, name="pool_fwd", grid=(t // tm,),
        in_specs=[pl.BlockSpec((tm, pw), lambda i: (i, 0)),
                  pl.BlockSpec((POOL_HALO, pw), lambda i: (jnp.maximum(i * nh - 1, 0), 0)),
                  pl.BlockSpec((tm, pw), lambda i: (i, 1)),
                  pl.BlockSpec(mixw.shape, lambda i: (0, 0, 0)),
                  pl.BlockSpec((1, pw), lambda i: (0, 0))],
        out_specs=pl.BlockSpec((tm, pw), lambda i: (i, 0)),
        out_shape=SDS((t, 2 * pw), BF16),
        compiler_params=_params(("arbitrary",)),
    )(proj, proj, proj, mixw, scale)


def pool_bwd_a(dmixed, proj, mixw, scale, tm):
    t, e = proj.shape
    pw = scale.shape[1]
    ng = len(POOL_WINDOWS)
    gw = pw // ng
    nh = tm // POOL_HALO

    def body(dy_ref, uc_ref, up_ref, g_ref, w_ref, s_ref, dg_ref, dq_ref, ds_ref, dw_ref):
        i = pl.program_id(0)

        @pl.when(i == 0)
        def _():
            ds_ref[...] = jnp.zeros_like(ds_ref)
            dw_ref[...] = jnp.zeros_like(dw_ref)

        cur, sums = _pooled(uc_ref, up_ref, i, tm)
        for g, w in enumerate(POOL_WINDOWS):
            cols = slice(g * gw, (g + 1) * gw)
            cnt = _pool_counts(i, tm, w)
            pooled = (sums[g][POOL_HALO:, cols] / cnt - cur[:, cols]).astype(BF16)
            mixed = jnp.dot(pooled, w_ref[g], preferred_element_type=F32)
            gate = g_ref[:, cols]
            dy = dy_ref[:, cols]
            sc = s_ref[:, cols]
            dg_ref[:, cols] = (dy * mixed * sc * _dsilu(gate)).astype(BF16)
            ds = dy * _silu(gate)
            ds_ref[:, cols] += jnp.sum(ds * mixed, axis=0, keepdims=True)
            dmix = (ds * sc).astype(BF16)
            dw_ref[g] += lax.dot_general(pooled, dmix, TN, preferred_element_type=F32)
            dq_ref[:, cols] = lax.dot_general(dmix, w_ref[g], NT, preferred_element_type=F32) / cnt

    return pl.pallas_call(
        body, name="pool_bwd_a", grid=(t // tm,),
        in_specs=[pl.BlockSpec((tm, pw), lambda i: (i, 0)),
                  pl.BlockSpec((tm, pw), lambda i: (i, 0)),
                  pl.BlockSpec((POOL_HALO, pw), lambda i: (jnp.maximum(i * nh - 1, 0), 0)),
                  pl.BlockSpec((tm, pw), lambda i: (i, 1)),
                  pl.BlockSpec(mixw.shape, lambda i: (0, 0, 0)),
                  pl.BlockSpec((1, pw), lambda i: (0, 0))],
        out_specs=[pl.BlockSpec((tm, pw), lambda i: (i, 1)),
                   pl.BlockSpec((tm, pw), lambda i: (i, 0)),
                   pl.BlockSpec((1, pw), lambda i: (0, 0)),
                   pl.BlockSpec((ng, gw, gw), lambda i: (0, 0, 0))],
        out_shape=[SDS((t, e), BF16), SDS((t, pw), F32), SDS((1, pw), F32), SDS((ng, gw, gw), F32)],
        compiler_params=_params(("arbitrary",)),
    )(dmixed, proj, proj, proj, mixw, scale)


def pool_bwd_b(dq, dproj, tm):
    t, pw = dq.shape
    gw = pw // len(POOL_WINDOWS)
    nh = tm // POOL_HALO
    nt = t // tm

    def body(c_ref, n_ref, alias_ref, o_ref):
        i = pl.program_id(0)
        cur = c_ref[...]
        nxt = jnp.where(i < nt - 1, n_ref[...], 0.0)
        sums = _window_sums(jnp.concatenate([cur, nxt], axis=0), tm + POOL_HALO, True)
        for g, w in enumerate(POOL_WINDOWS):
            cols = slice(g * gw, (g + 1) * gw)
            o_ref[:, cols] = (sums[g][:tm, cols] - cur[:, cols] * _pool_counts(i, tm, w)).astype(BF16)

    return pl.pallas_call(
        body, name="pool_bwd_b", grid=(nt,),
        in_specs=[pl.BlockSpec((tm, pw), lambda i: (i, 0)),
                  pl.BlockSpec((POOL_HALO, pw), lambda i: (jnp.minimum((i + 1) * nh, t // POOL_HALO - 1), 0)),
                  pl.BlockSpec(memory_space=pl.ANY)],
        out_specs=pl.BlockSpec((tm, pw), lambda i: (i, 0)),
        out_shape=SDS(dproj.shape, dproj.dtype),
        input_output_aliases={2: 0},
        compiler_params=_params(("arbitrary",)),
    )(dq, dq, dproj)


def _conv_pre(xc_ref, xp_ref, w_ref, b_ref, i, tm):
    cur = xc_ref[...]
    prev = jnp.where(i > 0, xp_ref[...], 0.0)
    ext = jnp.concatenate([prev, cur], axis=0)
    taps = [pltpu.roll(ext, CONV_WIDTH - 1 - k, 0)[CONV_HALO:] for k in range(CONV_WIDTH - 1)] + [cur]
    pre = b_ref[...]
    for k in range(CONV_WIDTH):
        pre = pre + w_ref[k:k + 1, :] * taps[k]
    return pre, taps


def conv_fwd(proj, conv_w, conv_b, col_block, tm):
    t = proj.shape[0]
    cd = conv_b.shape[1]
    nh = tm // CONV_HALO

    def body(xc_ref, xp_ref, w_ref, b_ref, o_ref):
        i = pl.program_id(0)
        pre, _ = _conv_pre(xc_ref, xp_ref, w_ref, b_ref, i, tm)
        o_ref[...] = _silu(pre)

    return pl.pallas_call(
        body, name="conv_fwd", grid=(t // tm,),
        in_specs=[pl.BlockSpec((tm, cd), lambda i: (i, col_block)),
                  pl.BlockSpec((CONV_HALO, cd), lambda i: (jnp.maximum(i * nh - 1, 0), col_block)),
                  pl.BlockSpec((CONV_WIDTH, cd), lambda i: (0, 0)),
                  pl.BlockSpec((1, cd), lambda i: (0, 0))],
        out_specs=pl.BlockSpec((tm, cd), lambda i: (i, 0)),
        out_shape=SDS((t, cd), F32),
        compiler_params=_params(("arbitrary",)),
    )(proj, proj, conv_w, conv_b)


def conv_bwd_a(dxs, db, dc, proj, conv_w, conv_b, col_block, tm):
    t = proj.shape[0]
    cd = conv_b.shape[1]
    sw = dxs.shape[1]
    gn = db.shape[1]
    nh = tm // CONV_HALO

    def body(dx_ref, db_ref, dc_ref, xc_ref, xp_ref, w_ref, b_ref, dp_ref, dw_ref, dbias_ref):
        i = pl.program_id(0)

        @pl.when(i == 0)
        def _():
            dw_ref[...] = jnp.zeros_like(dw_ref)
            dbias_ref[...] = jnp.zeros_like(dbias_ref)

        pre, taps = _conv_pre(xc_ref, xp_ref, w_ref, b_ref, i, tm)
        dact = jnp.concatenate([dx_ref[...], db_ref[...], dc_ref[...]], axis=1)
        dpre = dact * _dsilu(pre)
        dp_ref[...] = dpre
        dbias_ref[...] += jnp.sum(dpre, axis=0, keepdims=True)
        for k in range(CONV_WIDTH):
            dw_ref[k:k + 1, :] += jnp.sum(dpre * taps[k], axis=0, keepdims=True)

    return pl.pallas_call(
        body, name="conv_bwd_a", grid=(t // tm,),
        in_specs=[pl.BlockSpec((tm, sw), lambda i: (i, 0)), pl.BlockSpec((tm, gn), lambda i: (i, 0)),
                  pl.BlockSpec((tm, gn), lambda i: (i, 0)),
                  pl.BlockSpec((tm, cd), lambda i: (i, col_block)),
                  pl.BlockSpec((CONV_HALO, cd), lambda i: (jnp.maximum(i * nh - 1, 0), col_block)),
                  pl.BlockSpec((CONV_WIDTH, cd), lambda i: (0, 0)),
                  pl.BlockSpec((1, cd), lambda i: (0, 0))],
        out_specs=[pl.BlockSpec((tm, cd), lambda i: (i, 0)),
                   pl.BlockSpec((CONV_WIDTH, cd), lambda i: (0, 0)),
                   pl.BlockSpec((1, cd), lambda i: (0, 0))],
        out_shape=[SDS((t, cd), F32), SDS((CONV_WIDTH, cd), F32), SDS((1, cd), F32)],
        compiler_params=_params(("arbitrary",)),
    )(dxs, db, dc, proj, proj, conv_w, conv_b)


def conv_bwd_b(dpre, conv_w, dproj, col_block, tm):
    t, cd = dpre.shape
    nh = tm // CONV_HALO
    nt = t // tm

    def body(c_ref, n_ref, w_ref, alias_ref, o_ref):
        i = pl.program_id(0)
        cur = c_ref[...]
        nxt = jnp.where(i < nt - 1, n_ref[...], 0.0)
        ext = jnp.concatenate([cur, nxt], axis=0)
        n = tm + CONV_HALO
        acc = w_ref[CONV_WIDTH - 1:CONV_WIDTH, :] * cur
        for k in range(CONV_WIDTH - 1):
            acc = acc + w_ref[k:k + 1, :] * pltpu.roll(ext, n - (CONV_WIDTH - 1 - k), 0)[:tm]
        o_ref[...] = acc.astype(BF16)

    return pl.pallas_call(
        body, name="conv_bwd_b", grid=(nt,),
        in_specs=[pl.BlockSpec((tm, cd), lambda i: (i, 0)),
                  pl.BlockSpec((CONV_HALO, cd), lambda i: (jnp.minimum((i + 1) * nh, t // CONV_HALO - 1), 0)),
                  pl.BlockSpec((CONV_WIDTH, cd), lambda i: (0, 0)),
                  pl.BlockSpec(memory_space=pl.ANY)],
        out_specs=pl.BlockSpec((tm, cd), lambda i: (i, col_block)),
        out_shape=SDS(dproj.shape, dproj.dtype),
        input_output_aliases={3: 0},
        compiler_params=_params(("arbitrary",)),
    )(dpre, dpre, conv_w, dproj)


def _softplus(v):
    return jnp.maximum(v, 0.0) + jnp.log(1.0 + jnp.exp(-jnp.abs(v)))


def _ssd_chunk_terms(dtr_ref, bias_ref, a_ref, n_heads):
    q = SSD_CHUNK
    lane = lax.broadcasted_iota(jnp.int32, (1, LANES), 1)
    pre = dtr_ref[...] + bias_ref[...]
    dt = jnp.where(lane < n_heads, _softplus(pre), 0.0)
    a = jnp.where(lane < n_heads, -jnp.exp(a_ref[...]), 0.0)
    row = lax.broadcasted_iota(jnp.int32, (q, q), 0)
    col = lax.broadcasted_iota(jnp.int32, (q, q), 1)
    causal = row >= col
    acs = jnp.dot(causal.astype(F32), dt * a, precision=HIGHEST, preferred_element_type=F32)
    last = acs[q - 1:q, :]
    return dict(pre=pre, dt=dt, a=a, acs=acs, acs_t=acs.T, eacs=jnp.exp(acs), dstate=jnp.exp(last - acs),
                cdec=jnp.exp(last), causal=causal, lane=lane)


def _pair_cols(lo, v, h):
    return jnp.where(lo, v[:, h:h + 1], v[:, h + 1:h + 2])


def _pair_decay(tm_, cb, h):
    l0 = jnp.exp(jnp.where(tm_["causal"], tm_["acs"][:, h:h + 1] - tm_["acs_t"][h:h + 1, :], -jnp.inf))
    l1 = jnp.exp(jnp.where(tm_["causal"], tm_["acs"][:, h + 1:h + 2] - tm_["acs_t"][h + 1:h + 2, :], -jnp.inf))
    return l0, l1, jnp.concatenate([cb * l0, cb * l1], axis=1)


def _column_sums(v):
    ones = jnp.ones((v.shape[0], LANES), BF16)
    hi = v.astype(BF16)
    r1 = v - hi.astype(F32)
    mid = r1.astype(BF16)
    lo = (r1 - mid.astype(F32)).astype(BF16)
    return sum(lax.dot_general(part, ones, TN, preferred_element_type=F32) for part in (hi, mid, lo))


def _block_diag(lo, xdt):
    return jnp.concatenate([jnp.where(lo, xdt, 0.0), jnp.where(lo, 0.0, xdt)], axis=0).astype(BF16)


def ssd_fwd(xbc, dt_raw, dt_bias, a_log, n_heads, comm=None):
    t = xbc.shape[0]
    q = SSD_CHUNK
    gn = SSD_GROUPS * SSD_STATE
    sw = n_heads * SSD_HEAD_DIM
    n_pairs = n_heads // 2
    pairs_per_group = n_pairs // SSD_GROUPS
    nc = t // q
    bblk = sw // gn

    def body(xs_ref, b_ref, c_ref, dtr_ref, bias_ref, a_ref, y_ref, sin_ref, state):
        @pl.when(pl.program_id(0) == 0)
        def _():
            state[...] = jnp.zeros_like(state)

        tm_ = _ssd_chunk_terms(dtr_ref, bias_ref, a_ref, n_heads)
        lo = tm_["lane"] < SSD_HEAD_DIM
        for g in range(SSD_GROUPS):
            gcols = slice(g * SSD_STATE, (g + 1) * SSD_STATE)
            bg = b_ref[:, gcols].astype(BF16)
            cg = c_ref[:, gcols].astype(BF16)
            cb = lax.dot_general(cg, bg, NT, preferred_element_type=F32)
            for j in range(pairs_per_group):
                p = g * pairs_per_group + j
                h = 2 * p
                pcols = slice(p * LANES, (p + 1) * LANES)
                _, _, mcat = _pair_decay(tm_, cb, h)
                xdt = xs_ref[:, pcols] * _pair_cols(lo, tm_["dt"], h)
                ydiag = jnp.dot(mcat.astype(BF16), _block_diag(lo, xdt), preferred_element_type=F32)
                st = state[p]
                sin_ref[0, p] = st
                yoff = jnp.dot(cg, st.astype(BF16), preferred_element_type=F32) * _pair_cols(lo, tm_["eacs"], h)
                y_ref[:, pcols] = ydiag + yoff
                xw = (xdt * _pair_cols(lo, tm_["dstate"], h)).astype(BF16)
                state[p] = st * _pair_cols(lo, tm_["cdec"], h) + lax.dot_general(bg, xw, TN, preferred_element_type=F32)

    vec = pl.BlockSpec((1, LANES), lambda c: (0, 0))
    return _call(
        body, (xbc, xbc, xbc, dt_raw, dt_bias, a_log), name="ssd_fwd", grid=(nc,),
        in_specs=[pl.BlockSpec((q, sw), lambda c: (c, 0)),
                  pl.BlockSpec((q, gn), lambda c: (c, bblk)),
                  pl.BlockSpec((q, gn), lambda c: (c, bblk + 1)),
                  pl.BlockSpec((q, LANES), lambda c: (c, 0)), vec, vec],
        out_specs=[pl.BlockSpec((q, sw), lambda c: (c, 0)),
                   pl.BlockSpec((1, n_pairs, SSD_STATE, LANES), lambda c: (c, 0, 0, 0))],
        out_shape=[SDS((t, sw), F32), SDS((nc, n_pairs, SSD_STATE, LANES), F32)],
        scratch_shapes=[pltpu.VMEM((n_pairs, SSD_STATE, LANES), F32)],
        sem=("arbitrary",), comm=comm)


def ssd_bwd(dy, xbc, dt_raw, dt_bias, a_log, d_full, s_in, n_heads, comm=None):
    t = xbc.shape[0]
    q = SSD_CHUNK
    gn = SSD_GROUPS * SSD_STATE
    sw = n_heads * SSD_HEAD_DIM
    n_pairs = n_heads // 2
    pairs_per_group = n_pairs // SSD_GROUPS
    nc = t // q
    bblk = sw // gn

    def body(dy_ref, xs_ref, b_ref, c_ref, dtr_ref, bias_ref, a_ref, dsk_ref, sin_ref,
             dxs_ref, db_ref, dc_ref, ddtr_ref, dbias_ref, dalog_ref,
             dstate, tbuf, xbuf, rbuf, acc_a, acc_b):
        i = pl.program_id(0)

        @pl.when(i == 0)
        def _():
            dstate[...] = jnp.zeros_like(dstate)
            rbuf[...] = jnp.zeros_like(rbuf)
            acc_a[...] = jnp.zeros_like(acc_a)
            acc_b[...] = jnp.zeros_like(acc_b)

        tm_ = _ssd_chunk_terms(dtr_ref, bias_ref, a_ref, n_heads)
        lane = tm_["lane"]
        lo = lane < SSD_HEAD_DIM
        intra = jnp.zeros((q, LANES), F32)
        for g in range(SSD_GROUPS):
            gcols = slice(g * SSD_STATE, (g + 1) * SSD_STATE)
            bg = b_ref[:, gcols].astype(BF16)
            cg = c_ref[:, gcols].astype(BF16)
            cb = lax.dot_general(cg, bg, NT, preferred_element_type=F32)
            dcb = jnp.zeros((q, q), F32)
            db_acc = jnp.zeros((q, SSD_STATE), F32)
            dc_acc = jnp.zeros((q, SSD_STATE), F32)
            for j in range(pairs_per_group):
                p = g * pairs_per_group + j
                h = 2 * p
                pcols = slice(p * LANES, (p + 1) * LANES)
                l0, l1, mcat = _pair_decay(tm_, cb, h)
                xp = xs_ref[:, pcols]
                dtp = _pair_cols(lo, tm_["dt"], h)
                xdt = xp * dtp
                xbd = _block_diag(lo, xdt)
                dyp = dy_ref[:, pcols]
                dyb = dyp.astype(BF16)
                dsb = _pair_cols(lo, tm_["dstate"], h)
                cdr = _pair_cols(lo, tm_["cdec"], h)
                eb = _pair_cols(lo, tm_["eacs"], h)
                st = sin_ref[0, p]
                stb = st.astype(BF16)
                dst = dstate[p]
                dstb = dst.astype(BF16)
                dye = (dyp * eb).astype(BF16)
                both = lax.dot_general(mcat.astype(BF16), dyb, TN, preferred_element_type=F32)
                dx_state = jnp.dot(bg, dstb, preferred_element_type=F32) * dsb
                dxdt = jnp.where(lo, both[:q], both[q:]) + dx_state
                dmcat = lax.dot_general(dyb, xbd, NT, preferred_element_type=F32)
                dcb = dcb + dmcat[:, :q] * l0 + dmcat[:, q:] * l1
                dseg = dmcat * mcat
                csum = _column_sums(dseg)
                intra = (intra
                         + jnp.where(lane == h, jnp.sum(dseg[:, :q], axis=1, keepdims=True) - csum[:q], 0.0)
                         + jnp.where(lane == h + 1, jnp.sum(dseg[:, q:], axis=1, keepdims=True) - csum[q:], 0.0))
                dc_acc = dc_acc + lax.dot_general(dye, stb, NT, preferred_element_type=F32)
                db_acc = db_acc + lax.dot_general((xdt * dsb).astype(BF16), dstb, NT, preferred_element_type=F32)
                yoff = jnp.dot(cg, stb, preferred_element_type=F32) * eb
                tbuf[:, pcols] = dyp * yoff - xdt * dx_state
                xbuf[:, pcols] = dxdt * xp
                rbuf[0:1, pcols] = (jnp.sum(xdt * dx_state, axis=0, keepdims=True)
                                    + cdr * jnp.sum(dst * st, axis=0, keepdims=True))
                dxs_ref[:, pcols] = dxdt * dtp + dyp * dsk_ref[:, pcols]
                dstate[p] = dst * cdr + lax.dot_general(cg, dye, TN, preferred_element_type=F32)
            dcbb = dcb.astype(BF16)
            dc_ref[:, gcols] = dc_acc + jnp.dot(dcbb, bg, preferred_element_type=F32)
            db_ref[:, gcols] = db_acc + lax.dot_general(dcbb, cg, TN, preferred_element_type=F32)

        sel = _head_selector(sw, SSD_HEAD_DIM)
        dacs = intra + _split_dot(tbuf[...], sel)
        carry = _split_dot(rbuf[...], sel)[0:1]
        anti = jnp.logical_not(tm_["causal"]) | (lax.broadcasted_iota(jnp.int32, (q, q), 0)
                                                 == lax.broadcasted_iota(jnp.int32, (q, q), 1))
        da = jnp.dot(anti.astype(F32), dacs, precision=HIGHEST, preferred_element_type=F32) + carry
        ddt = da * tm_["a"] + _split_dot(xbuf[...], sel)
        ddtr = jnp.where(tm_["lane"] < n_heads, ddt * jax.nn.sigmoid(tm_["pre"]), 0.0)
        ddtr_ref[...] = ddtr.astype(BF16)
        acc_b[...] += jnp.sum(ddtr, axis=0, keepdims=True)
        acc_a[...] += jnp.sum(da * tm_["dt"], axis=0, keepdims=True)

        @pl.when(i == nc - 1)
        def _():
            dbias_ref[...] = acc_b[...]
            dalog_ref[...] = acc_a[...] * tm_["a"]

    vec = pl.BlockSpec((1, LANES), lambda i: (0, 0))
    wide = pl.BlockSpec((q, sw), lambda i: (nc - 1 - i, 0))
    return _call(
        body, (dy, xbc, xbc, xbc, dt_raw, dt_bias, a_log, d_full, s_in), name="ssd_bwd", grid=(nc,),
        in_specs=[wide, wide,
                  pl.BlockSpec((q, gn), lambda i: (nc - 1 - i, bblk)),
                  pl.BlockSpec((q, gn), lambda i: (nc - 1 - i, bblk + 1)),
                  pl.BlockSpec((q, LANES), lambda i: (nc - 1 - i, 0)), vec, vec,
                  pl.BlockSpec((1, sw), lambda i: (0, 0)),
                  pl.BlockSpec((1, n_pairs, SSD_STATE, LANES), lambda i: (nc - 1 - i, 0, 0, 0))],
        out_specs=[wide, pl.BlockSpec((q, gn), lambda i: (nc - 1 - i, 0)), pl.BlockSpec((q, gn), lambda i: (nc - 1 - i, 0)),
                   pl.BlockSpec((q, LANES), lambda i: (nc - 1 - i, 0)), vec, vec],
        out_shape=[SDS((t, sw), F32), SDS((t, gn), F32), SDS((t, gn), F32), SDS((t, LANES), BF16),
                   SDS((1, LANES), F32), SDS((1, LANES), F32)],
        scratch_shapes=[pltpu.VMEM((n_pairs, SSD_STATE, LANES), F32), pltpu.VMEM((q, sw), F32), pltpu.VMEM((q, sw), F32),
                        pltpu.VMEM((8, sw), F32), pltpu.VMEM((1, LANES), F32), pltpu.VMEM((1, LANES), F32)],
        sem=("arbitrary",), comm=comm)


def _gated(y_ref, xs_ref, z_ref, dsk_ref):
    y1 = y_ref[...] + dsk_ref[...] * xs_ref[...]
    return y1, y1 * _silu(z_ref[...])


def gate_norm_fwd(y, xbc, proj, d_full, norm_w, mixed, z_block, tm):
    t, sw = y.shape
    gw = sw // SSD_GROUPS

    def body(y_ref, xs_ref, z_ref, dsk_ref, nw_ref, alias_ref, o_ref):
        _, y2 = _gated(y_ref, xs_ref, z_ref, dsk_ref)
        for g in range(SSD_GROUPS):
            cols = slice(g * gw, (g + 1) * gw)
            blk = y2[:, cols]
            r = lax.rsqrt(jnp.mean(blk * blk, axis=-1, keepdims=True) + NORM_EPS)
            o_ref[:, cols] = (blk * r * nw_ref[:, cols]).astype(BF16)

    row = pl.BlockSpec((tm, sw), lambda i: (i, 0))
    vec = pl.BlockSpec((1, sw), lambda i: (0, 0))
    return pl.pallas_call(
        body, name="gate_norm_fwd", grid=(t // tm,),
        in_specs=[row, row, pl.BlockSpec((tm, sw), lambda i: (i, z_block)), vec, vec, pl.BlockSpec(memory_space=pl.ANY)],
        out_specs=pl.BlockSpec((tm, sw), lambda i: (i, 1)),
        out_shape=SDS(mixed.shape, mixed.dtype),
        input_output_aliases={5: 0},
        compiler_params=_params(("arbitrary",)),
    )(y, xbc, proj, d_full, norm_w, mixed)


def gate_norm_bwd(dmixed, y, xbc, proj, d_full, norm_w, dproj, z_block, tm):
    t, sw = y.shape
    gw = sw // SSD_GROUPS
    nt = t // tm

    def body(d_ref, y_ref, xs_ref, z_ref, dsk_ref, nw_ref, alias_ref, dy_ref, dz_ref, dnw_ref, dd_ref, acc_d):
        i = pl.program_id(0)

        @pl.when(i == 0)
        def _():
            dnw_ref[...] = jnp.zeros_like(dnw_ref)
            acc_d[...] = jnp.zeros_like(acc_d)

        y1, y2 = _gated(y_ref, xs_ref, z_ref, dsk_ref)
        d3 = d_ref[...]
        parts = []
        for g in range(SSD_GROUPS):
            cols = slice(g * gw, (g + 1) * gw)
            blk = y2[:, cols]
            r = lax.rsqrt(jnp.mean(blk * blk, axis=-1, keepdims=True) + NORM_EPS)
            n = blk * r
            dg = d3[:, cols]
            dnw_ref[:, cols] += jnp.sum(dg * n, axis=0, keepdims=True)
            parts.append(_norm_bwd(dg * nw_ref[:, cols], n, r))
        dy2 = jnp.concatenate(parts, axis=1)
        zv = z_ref[...]
        dz_ref[...] = (dy2 * y1 * _dsilu(zv)).astype(BF16)
        dy1 = dy2 * _silu(zv)
        dy_ref[...] = dy1
        acc_d[0:1, :] += jnp.sum(dy1 * xs_ref[...], axis=0, keepdims=True)

        @pl.when(i == nt - 1)
        def _():
            dd_ref[...] = _split_dot(acc_d[...], _head_selector(sw, SSD_HEAD_DIM))[0:1]

    row = pl.BlockSpec((tm, sw), lambda i: (i, 0))
    vec = pl.BlockSpec((1, sw), lambda i: (0, 0))
    return pl.pallas_call(
        body, name="gate_norm_bwd", grid=(nt,),
        in_specs=[pl.BlockSpec((tm, sw), lambda i: (i, 1)), row, row, pl.BlockSpec((tm, sw), lambda i: (i, z_block)),
                  vec, vec, pl.BlockSpec(memory_space=pl.ANY)],
        out_specs=[row, pl.BlockSpec((tm, sw), lambda i: (i, z_block)), vec, pl.BlockSpec((1, LANES), lambda i: (0, 0))],
        out_shape=[SDS((t, sw), F32), SDS(dproj.shape, dproj.dtype), SDS((1, sw), F32), SDS((1, LANES), F32)],
        scratch_shapes=[pltpu.VMEM((8, sw), F32)],
        input_output_aliases={6: 1},
        compiler_params=_params(("arbitrary",)),
    )(dmixed, y, xbc, proj, d_full, norm_w, dproj)


GATE_BLOCK = 1
Z_BLOCK = 2
CONV_BLOCK = 2


def _tiles(t):
    return min(256, t), min(512, t)


def _place():
    x, y, c = lax.axis_index("x"), lax.axis_index("y"), lax.axis_index("c")
    return x, y, c, [(1 - x, y), (x, 1 - y), (1 - x, 1 - y)]


def gather_spread(shards, layer):
    def make(ins, outs, ss, rs):
        x, y, c, chips = _place()
        mine = 4 * x + 2 * y + c
        peers = [(x, y, 1 - c)] + [(px, py, c) for px, py in chips]
        sends, locals_, arrivals = [], [], []
        for a in range(len(ins)):
            src = ins[a].at[layer]
            locals_.append(pltpu.make_async_copy(src, outs[a].at[mine], ss.at[5 * a + 4]))
            for j, (px, py, pc) in enumerate(peers):
                sends.append(_remote(src, outs[a].at[mine], ss, rs, 5 * a + j, (px, py, pc)))
                arrivals.append(_remote(src, outs[a].at[4 * px + 2 * py + pc], ss, rs, 5 * a + j, (px, py, pc)))
        return sends, locals_, arrivals

    return Comm(shards, [SDS((N_DEV,) + s.shape[1:], s.dtype) for s in shards], {}, 5 * len(shards), make)


def gather_pass_on(gathered):
    def make(ins, outs, ss, rs):
        x, y, c, chips = _place()
        sends, arrivals = [], []
        for a in range(len(outs)):
            for j, (px, py) in enumerate(chips):
                blk, other = 4 * px + 2 * py + c, 4 * px + 2 * py + (1 - c)
                sends.append(_remote(outs[a].at[blk], outs[a].at[blk], ss, rs, 3 * a + j, (x, y, 1 - c)))
                arrivals.append(_remote(outs[a].at[other], outs[a].at[other], ss, rs, 3 * a + j, (x, y, 1 - c)))
        return sends, [], arrivals

    return Comm(gathered, [SDS(g.shape, g.dtype) for g in gathered], {a: a for a in range(len(gathered))},
                3 * len(gathered), make)


def sibling_swap(sends_):
    def make(ins, outs, ss, rs):
        x, y, c, _ = _place()
        cps = [_remote(ins[a], outs[a], ss, rs, a, (x, y, 1 - c)) for a in range(len(ins))]
        return cps, [], cps

    return Comm(sends_, [SDS(s.shape, s.dtype) for s in sends_], {}, len(sends_), make)


def chips_scatter(slabs):
    def make(ins, outs, ss, rs):
        x, y, c, chips = _place()
        mychip = 2 * x + y
        sends, arrivals = [], []
        for a in range(len(ins)):
            for j, (px, py) in enumerate(chips):
                k = 2 * px + py
                sends.append(_remote(ins[a].at[k], outs[a].at[lax.rem(mychip - k + 4, 4) - 1], ss, rs, 3 * a + j, (px, py, c)))
                arrivals.append(_remote(ins[a].at[k], outs[a].at[lax.rem(k - mychip + 4, 4) - 1], ss, rs, 3 * a + j, (px, py, c)))
        return sends, [], arrivals

    return Comm(slabs, [SDS((3,) + s.shape[1:], s.dtype) for s in slabs], {}, 3 * len(slabs), make)


def comm_only(comm, name):
    def body():
        pass

    return _call(body, (), name=name, grid=(), in_specs=[], out_specs=[], out_shape=[], comm=comm)[1]


def layer_fwd(x, p, nxt=None):
    t = x.shape[0]
    tm, tmm = _tiles(t)
    n_heads = p["d_full"].shape[1] // SSD_HEAD_DIM
    h, r_pre = rms_fwd(x, p["pre_w"], tm)
    proj, got_a = mm_nn(h, p["w_main"], F32, tmm, 1024, "in_proj", gather_spread(nxt[0], nxt[2]) if nxt else None)
    dt_raw, _ = mm_nn(h, p["w_dt"], F32, tmm, LANES, "dt_proj")
    mixed = pool_fwd(proj, p["mixw"], p["pscale"], tm)
    xbc = conv_fwd(proj, p["conv_w"], p["conv_b"], CONV_BLOCK, tm)
    (y, s_in), got_b = ssd_fwd(xbc, dt_raw, p["dt_bias"], p["a_log"], n_heads, gather_spread(nxt[1], nxt[2]) if nxt else None)
    mixed = gate_norm_fwd(y, xbc, proj, p["d_full"], p["norm_w"], mixed, Z_BLOCK, tm)
    out, gathered = mm_nn(mixed, p["w_out"], F32, tmm, 512, "out_proj", gather_pass_on(got_a + got_b) if nxt else None)
    x_next, r_post = post_fwd(out, x, p["post_w"], tm)
    return x_next, dict(x=x, h=h, r_pre=r_pre, proj=proj, dt_raw=dt_raw, xbc=xbc, y=y, s_in=s_in, mixed=mixed,
                        out=out, r_post=r_post), gathered


def layer_bwd(g, s, p, pending=None):
    t = g.shape[0]
    tm, tmm = _tiles(t)
    d = g.shape[1]
    n_heads = p["d_full"].shape[1] // SSD_HEAD_DIM
    d_out, d_post = post_bwd(g, s["out"], s["r_post"], p["post_w"], tm)
    dmixed, got_sib = mm_nt(d_out, p["w_out"], F32, tmm, 1024, d, "d_mixed", sibling_swap(pending[1]) if pending else None)
    chip_sums = [pair_sum(o, r, min(256, o.shape[1]), "pair_sum") for o, r in zip(pending[0], got_sib)] if pending else []
    dw_out = mm_tn(s["mixed"], d_out, 512, 512, "dw_out")
    dproj, dq, d_pscale, d_mixw = pool_bwd_a(dmixed, s["proj"], p["mixw"], p["pscale"], tm)
    dproj = pool_bwd_b(dq, dproj, tm)
    dy, dproj, d_norm, d_dskip = gate_norm_bwd(dmixed, s["y"], s["xbc"], s["proj"], p["d_full"], p["norm_w"], dproj,
                                               Z_BLOCK, tm)
    (dxs, db, dc, ddtr, d_dtb, d_alog), got_first = ssd_bwd(
        dy, s["xbc"], s["dt_raw"], p["dt_bias"], p["a_log"], p["d_full"], s["s_in"], n_heads,
        chips_scatter(chip_sums[:1]) if pending else None)
    dpre, d_convw, d_convb = conv_bwd_a(dxs, db, dc, s["proj"], p["conv_w"], p["conv_b"], CONV_BLOCK, tm)
    dproj = conv_bwd_b(dpre, p["conv_w"], dproj, CONV_BLOCK, tm)
    dh_main, got_rest = mm_nt(dproj, p["w_main"], F32, tmm, d, 1024, "dh_main", chips_scatter(chip_sums[1:]) if pending else None)
    dh_dt, _ = mm_nt(ddtr, p["w_dt"], F32, tmm, d, LANES, "dh_dt")
    dw_main = mm_tn(s["h"], dproj, 512, 1024, "dw_main")
    dw_dt = mm_tn(s["h"], ddtr, 512, LANES, "dw_dt")
    gx, d_pre = rms_bwd(dh_main, dh_dt, s["x"], s["r_pre"], p["pre_w"], g, tm)
    grads = dict(pre_w=d_pre, w_main=dw_main, w_dt=dw_dt, mixw=d_mixw, pscale=d_pscale, conv_w=d_convw, conv_b=d_convb,
                 dt_bias=d_dtb, a_log=d_alog, d_skip=d_dskip, norm_w=d_norm, w_out=dw_out, post_w=d_post)
    return gx, grads, (got_sib, got_first + got_rest)


def _two_level_gather(x_refs, out_slots, send_sems, recv_sems, local_sems):
    x, y, c, chips = _place()
    me, sibling = (x, y, c), (x, y, 1 - c)
    n = len(x_refs)

    def copy(a, k, block, to, src=None):
        return pltpu.make_async_remote_copy(
            src_ref=out_slots[a](*block) if src is None else src, dst_ref=out_slots[a](*block),
            send_sem=send_sems.at[7 * a + k], recv_sem=recv_sems.at[7 * a + k], device_id=to, device_id_type=MESH)

    mine = [pltpu.make_async_copy(x_refs[a], out_slots[a](*me), local_sems.at[a]) for a in range(n)]
    for cp in mine:
        cp.start()
    first = []
    for a in range(n):
        first.append(copy(a, 0, me, sibling, src=x_refs[a]))
        first += [copy(a, 1 + j, me, (*chip, c), src=x_refs[a]) for j, chip in enumerate(chips)]
    for cp in first:
        cp.start()
    passed = []
    for j, chip in enumerate(chips):
        for a in range(n):
            copy(a, 1 + j, (*chip, c), me).wait_recv()
            fwd = copy(a, 4 + j, (*chip, c), sibling)
            fwd.start()
            passed.append(fwd)
    for a in range(n):
        copy(a, 0, sibling, me).wait_recv()
        for j, chip in enumerate(chips):
            copy(a, 4 + j, (*chip, 1 - c), me).wait_recv()
    for cp in first + passed:
        cp.wait_send()
    for cp in mine:
        cp.wait()


def all_gather_hbm(shards, name):
    n = len(shards)

    def body(*refs):
        x_refs, out_refs = refs[:n], refs[n:2 * n]
        send_sems, recv_sems, local_sems = refs[2 * n:]
        slots = [lambda px, py, pc, o=o: o.at[:, 4 * px + 2 * py + pc] for o in out_refs]
        _two_level_gather(x_refs, slots, send_sems, recv_sems, local_sems)

    hbm = pl.BlockSpec(memory_space=pl.ANY)
    return pl.pallas_call(
        body, name=name,
        out_shape=[SDS((s.shape[0], N_DEV) + s.shape[1:], s.dtype) for s in shards],
        in_specs=[hbm] * n, out_specs=[hbm] * n,
        scratch_shapes=[pltpu.SemaphoreType.DMA((7 * n,)), pltpu.SemaphoreType.DMA((7 * n,)), pltpu.SemaphoreType.DMA((n,))],
    )(*shards)


def all_gather_vmem(block, name):
    r, c_ = block.shape

    def body(x_ref, out_ref, send_sems, recv_sems, local_sems):
        _two_level_gather([x_ref], [lambda px, py, pc: out_ref.at[4 * px + 2 * py + pc]], send_sems, recv_sems, local_sems)

    return pl.pallas_call(
        body, name=name, out_shape=SDS((N_DEV, r, c_), block.dtype),
        in_specs=[pl.BlockSpec(memory_space=pltpu.VMEM)], out_specs=pl.BlockSpec(memory_space=pltpu.VMEM),
        scratch_shapes=[pltpu.SemaphoreType.DMA((7,)), pltpu.SemaphoreType.DMA((7,)), pltpu.SemaphoreType.DMA((1,))],
        compiler_params=_params(),
    )(block)


def _adamw(w, g, m, v):
    m = ADAM_B1 * m + (1.0 - ADAM_B1) * g
    v = ADAM_B2 * v + (1.0 - ADAM_B2) * jnp.square(g)
    m_hat = m / (1.0 - ADAM_B1 ** ADAM_STEP)
    v_hat = v / (1.0 - ADAM_B2 ** ADAM_STEP)
    delta = -ADAM_LR * (m_hat / (jnp.sqrt(v_hat) + ADAM_EPS) + ADAM_WD * w)
    return delta, m, v


def pair_sum(own, got, tr, name):
    k, r, c_ = own.shape

    def body(a_ref, b_ref, o_ref):
        o_ref[...] = (a_ref[...] + b_ref[...].astype(F32)).astype(BF16)

    blk = pl.BlockSpec((pl.Squeezed(), tr, c_), lambda kk, i: (kk, i, 0))
    return pl.pallas_call(
        body, name=name, grid=(k, r // tr), in_specs=[blk, blk], out_specs=blk, out_shape=SDS(own.shape, BF16),
        compiler_params=_params(("arbitrary", "arbitrary")),
    )(own, got)


def reduce_adam(own, got_sibling, got_chips, w, m, v, prev, layer, tr, name):
    nl, r, c_ = w.shape

    def body(own_ref, sib_ref, c0_ref, c1_ref, c2_ref, w_ref, m_ref, v_ref, *rest):
        g_ref, d_ref, nm_ref, nv_ref = rest[-4:]
        g = (own_ref[...] + sib_ref[...].astype(F32) + c0_ref[...].astype(F32) + c1_ref[...].astype(F32)
             + c2_ref[...].astype(F32))
        delta, nm, nv = _adamw(w_ref[...], g, m_ref[...], v_ref[...])
        g_ref[...] = g
        d_ref[...] = delta
        nm_ref[...] = nm
        nv_ref[...] = nv

    row = pl.BlockSpec((tr, c_), lambda i: (i, 0))
    lay = pl.BlockSpec((pl.Squeezed(), tr, c_), lambda i: (layer, i, 0))
    chips = [pl.BlockSpec((pl.Squeezed(), tr, c_), lambda i, s=s: (s, i, 0)) for s in range(3)]
    in_specs = [row, row] + chips + [lay, lay, lay]
    args = [own, got_sibling, got_chips, got_chips, got_chips, w, m, v]
    aliases = {}
    if prev is not None:
        in_specs += [pl.BlockSpec(memory_space=pl.ANY)] * 4
        aliases = {len(args) + k: k for k in range(4)}
        args += list(prev)
    return pl.pallas_call(
        body, name=name, grid=(r // tr,), in_specs=in_specs, out_specs=[lay] * 4,
        out_shape=[SDS((nl, r, c_), F32)] * 4, input_output_aliases=aliases,
        compiler_params=_params(("arbitrary",)),
    )(*args)


def sum_devices(packs):
    n, r, c_ = packs.shape

    def body(p_ref, o_ref):
        acc = p_ref[0]
        for k in range(1, n):
            acc = acc + p_ref[k]
        o_ref[...] = acc

    return pl.pallas_call(body, name="sum_devices", out_shape=SDS((r, c_), F32), compiler_params=_params())(packs)


def adam_small(w, g, m, v):
    def body(w_ref, g_ref, m_ref, v_ref, d_ref, nm_ref, nv_ref):
        delta, nm, nv = _adamw(w_ref[...], g_ref[...], m_ref[...], v_ref[...])
        d_ref[...] = delta
        nm_ref[...] = nm
        nv_ref[...] = nv

    return pl.pallas_call(body, name="adam_small", out_shape=[SDS(w.shape, F32)] * 3, compiler_params=_params())(w, g, m, v)


SMALL = ("pre_norm_w", "pool_scale", "conv_b", "dt_bias", "a_log", "d_skip", "_pad", "ssd_norm_w", "post_norm_w", "conv_w")


def _pack(parts):
    flat = jnp.concatenate([parts[k] for k in SMALL], axis=1).reshape(-1, LANES)
    return jnp.pad(flat, ((0, (-flat.shape[0]) % 8), (0, 0)))


def _unpack(pack, sizes, nl):
    total = sum(sizes[k] for k in SMALL)
    flat = pack[: nl * total // LANES].reshape(nl, total)
    out, o = {}, 0
    for k in SMALL:
        out[k] = flat[:, o:o + sizes[k]]
        o += sizes[k]
    return out


def kernel(x, pre_norm_w, w_in, pool_mix_w, pool_scale, conv_w, conv_b, dt_bias, a_log, d_skip, ssd_norm_w, w_out, post_norm_w, loss_target, m_pre_norm_w, m_w_in, m_pool_mix_w, m_pool_scale, m_conv_w, m_conv_b, m_dt_bias, m_a_log, m_d_skip, m_ssd_norm_w, m_w_out, m_post_norm_w, v_pre_norm_w, v_w_in, v_pool_mix_w, v_pool_scale, v_conv_w, v_conv_b, v_dt_bias, v_a_log, v_d_skip, v_ssd_norm_w, v_w_out, v_post_norm_w):
    cx, cy, cc = lax.axis_index("x"), lax.axis_index("y"), lax.axis_index("c")
    me = 4 * cx + 2 * cy + cc
    mychip = 2 * cx + cy
    nl, d, cols = w_in.shape
    t = x.shape[1]
    n_heads = a_log.shape[1]
    sw = n_heads * SSD_HEAD_DIM
    pw = pool_scale.shape[1]
    cd = conv_b.shape[1]
    ng, gsh, gw = pool_mix_w.shape[1:]
    e_main = N_DEV * cols - n_heads
    assert x.shape[0] == 1 and pw == sw and cd == sw + 2 * SSD_GROUPS * SSD_STATE and e_main == 2 * pw + sw + cd
    assert 2 * pw + sw == CONV_BLOCK * cd and n_heads <= LANES and t % SSD_CHUNK == 0 and gsh * N_DEV == gw
    tm, _ = _tiles(t)

    shards_a, shards_b = [w_in.astype(BF16)], [w_out.astype(BF16), pool_mix_w.astype(BF16), conv_w]
    gathered = [a[0] for a in all_gather_hbm([s[:1] for s in shards_a + shards_b], "gather_weights")]
    pad_h = ((0, 0), (0, LANES - n_heads))

    def layer_params(l, g_in, g_out, g_mix, g_conv):
        wcat = g_in.transpose(1, 0, 2).reshape(d, N_DEV * cols)
        return dict(pre_w=pre_norm_w[l:l + 1], w_main=wcat[:, :e_main],
                    w_dt=jnp.pad(wcat[:, e_main:], ((0, 0), (0, LANES - n_heads))),
                    mixw=g_mix.transpose(1, 0, 2, 3).reshape(ng, gw, gw), pscale=pool_scale[l:l + 1],
                    conv_w=g_conv.transpose(1, 0, 2).reshape(CONV_WIDTH, cd), conv_b=conv_b[l:l + 1],
                    dt_bias=jnp.pad(dt_bias[l:l + 1], pad_h), a_log=jnp.pad(a_log[l:l + 1], pad_h),
                    d_full=jnp.repeat(d_skip[l:l + 1], SSD_HEAD_DIM, axis=1), norm_w=ssd_norm_w[l:l + 1],
                    w_out=g_out.reshape(N_DEV * w_out.shape[1], d), post_w=post_norm_w[l:l + 1])

    xs = x[0]
    saved, params = [], []
    for l in range(nl):
        params.append(layer_params(l, *gathered))
        xs, s, gathered = layer_fwd(xs, params[l], (shards_a, shards_b, l + 1) if l + 1 < nl else None)
        saved.append(s)
    loss_part, g = loss_grad(xs, loss_target[0], tm)
    loss = lax.psum(loss_part[0, 0], ("x", "y", "c"))

    big = {"w_in": (w_in, m_w_in, v_w_in), "w_out": (w_out, m_w_out, v_w_out),
           "pool_mix_w": tuple(a.reshape(nl, ng * gsh, gw) for a in (pool_mix_w, m_pool_mix_w, v_pool_mix_w))}
    names = list(big)
    big_out = {k: None for k in big}
    small_g = [None] * nl

    def apply(layer, own, got_sib, got_chips):
        for k, o, gs_, gc in zip(names, own, got_sib, got_chips):
            wk, mk, vk = big[k]
            big_out[k] = reduce_adam(lax.dynamic_index_in_dim(o, mychip, 0, keepdims=False),
                                     lax.dynamic_index_in_dim(gs_, mychip, 0, keepdims=False),
                                     gc, wk, mk, vk, big_out[k], layer, min(256, wk.shape[1]), "reduce_adam_" + k)

    pending = None
    for l in reversed(range(nl)):
        g, gr, got = layer_bwd(g, saved[l], params[l], pending)
        if pending is not None:
            apply(l + 1, pending[0], *got)
        dwc = jnp.concatenate([gr["w_main"], gr["w_dt"][:, :n_heads]], axis=1).reshape(d, 4, 2, cols)
        halves = [lambda ci: lax.dynamic_index_in_dim(dwc, ci, 2, keepdims=False).transpose(1, 0, 2),
                  lambda ci: lax.dynamic_index_in_dim(gr["w_out"].reshape(4, 2, -1, d), ci, 1, keepdims=False),
                  lambda ci: lax.dynamic_index_in_dim(
                      gr["mixw"].reshape(ng, 4, 2, gsh, gw), ci, 2, keepdims=False).transpose(1, 0, 2, 3).reshape(4, ng * gsh, gw)]
        pending = ([h(cc) for h in halves], [h(1 - cc).astype(BF16) for h in halves])
        small_g[l] = dict(pre_norm_w=gr["pre_w"], pool_scale=gr["pscale"], conv_b=gr["conv_b"], dt_bias=gr["dt_bias"][:, :n_heads],
                          a_log=gr["a_log"][:, :n_heads], d_skip=gr["d_skip"][:, :n_heads], _pad=jnp.zeros((1, LANES - 3 * n_heads), F32),
                          ssd_norm_w=gr["norm_w"], post_norm_w=gr["post_w"], conv_w=gr["conv_w"].reshape(1, CONV_WIDTH * cd))
    got_sib = comm_only(sibling_swap(pending[1]), "grads_to_sibling")
    chip_sums = [pair_sum(o, r, min(256, o.shape[1]), "pair_sum") for o, r in zip(pending[0], got_sib)]
    apply(0, pending[0], got_sib, comm_only(chips_scatter(chip_sums), "grads_to_chips"))

    sizes = {k: small_g[0][k].shape[1] for k in SMALL}
    gsum = sum_devices(all_gather_vmem(_pack({k: jnp.concatenate([sg[k] for sg in small_g], axis=0) for k in SMALL}),
                                       "gather_small_grads"))
    gs = _unpack(gsum, sizes, nl)
    csh = conv_w.shape[2]
    gs["conv_w"] = lax.dynamic_slice_in_dim(gs["conv_w"].reshape(nl, CONV_WIDTH, cd), me * csh, csh, axis=2).reshape(nl, -1)
    lsizes = dict(sizes, conv_w=CONV_WIDTH * csh)
    zpad = jnp.zeros((nl, sizes["_pad"]), F32)

    def local(pre, scale, cb, dtb, al, dsk, nw, post, cw):
        return _pack(dict(pre_norm_w=pre, pool_scale=scale, conv_b=cb, dt_bias=dtb, a_log=al, d_skip=dsk, _pad=zpad,
                          ssd_norm_w=nw, post_norm_w=post, conv_w=cw.reshape(nl, -1)))

    wp = local(pre_norm_w, pool_scale, conv_b, dt_bias, a_log, d_skip, ssd_norm_w, post_norm_w, conv_w)
    mp = local(m_pre_norm_w, m_pool_scale, m_conv_b, m_dt_bias, m_a_log, m_d_skip, m_ssd_norm_w, m_post_norm_w, m_conv_w)
    vp = local(v_pre_norm_w, v_pool_scale, v_conv_b, v_dt_bias, v_a_log, v_d_skip, v_ssd_norm_w, v_post_norm_w, v_conv_w)
    small_out = [gs] + [_unpack(o, lsizes, nl) for o in adam_small(wp, _pack(gs), mp, vp)]

    def leaf(kind, name):
        if name in big:
            return big_out[name][kind].reshape(big[name][0].shape if name != "pool_mix_w" else pool_mix_w.shape)
        val = small_out[kind][name]
        return val.reshape(conv_w.shape) if name == "conv_w" else val

    order = ("pre_norm_w", "w_in", "pool_mix_w", "pool_scale", "conv_w", "conv_b", "dt_bias", "a_log", "d_skip",
             "ssd_norm_w", "w_out", "post_norm_w")
    return (loss, g[None]) + tuple(leaf(kind, name) for kind in range(4) for name in order)
```

```python
import jax
import jax.numpy as jnp
from jax import lax
from jax.experimental import pallas as pl
from jax.experimental.pallas import tpu as pltpu

F32 = jnp.float32
BF16 = jnp.bfloat16
SDS = jax.ShapeDtypeStruct
MESH = pl.DeviceIdType.MESH
HIGHEST = lax.Precision.HIGHEST

NORM_EPS = 1e-6
POOL_WINDOWS = (2, 4, 8, 16)
POOL_HALO = 16
CONV_WIDTH = 4
CONV_HALO = 8
SSD_CHUNK = 128
SSD_HEAD_DIM = 64
SSD_STATE = 128
SSD_GROUPS = 4
LANES = 128
N_DEV = 8

ADAM_LR = 0.001
ADAM_B1 = 0.9
ADAM_B2 = 0.999
ADAM_EPS = 1e-08
ADAM_WD = 0.01
ADAM_STEP = 10

VMEM_LIMIT = 56 * 1024 * 1024

NT = (((1,), (1,)), ((), ()))
TN = (((0,), (0,)), ((), ()))


def _params(sem=None):
    kw = dict(vmem_limit_bytes=VMEM_LIMIT)
    if sem is not None:
        kw["dimension_semantics"] = sem
    return pltpu.CompilerParams(**kw)


def _silu(v):
    return v * jax.nn.sigmoid(v)


def _dsilu(v):
    s = jax.nn.sigmoid(v)
    return s * (1.0 + v * (1.0 - s))


def _split_dot(v, sel):
    hi = v.astype(BF16)
    lo = (v - hi.astype(F32)).astype(BF16)
    return (jnp.dot(hi, sel, preferred_element_type=F32) + jnp.dot(lo, sel, preferred_element_type=F32))


def _head_selector(width, per):
    ch = lax.broadcasted_iota(jnp.int32, (width, LANES), 0)
    hd = lax.broadcasted_iota(jnp.int32, (width, LANES), 1)
    return jnp.where((ch >= hd * per) & (ch < (hd + 1) * per), 1.0, 0.0).astype(BF16)


class Comm:
    def __init__(self, inputs, out_shapes, aliases, n_sems, make):
        self.inputs, self.out_shapes, self.aliases, self.n_sems, self.make = list(inputs), list(out_shapes), dict(aliases), n_sems, make


def _remote(src, dst, send_sems, recv_sems, k, peer):
    return pltpu.make_async_remote_copy(src_ref=src, dst_ref=dst, send_sem=send_sems.at[k], recv_sem=recv_sems.at[k],
                                        device_id=peer, device_id_type=MESH)


def _call(body, args, *, name, grid, in_specs, out_specs, out_shape, scratch_shapes=(), sem=None, comm=None):
    in_specs, out_specs, out_shape = list(in_specs), list(out_specs), list(out_shape)
    if comm is None:
        outs = pl.pallas_call(body, name=name, grid=grid, in_specs=in_specs, out_specs=out_specs, out_shape=out_shape,
                              scratch_shapes=list(scratch_shapes), compiler_params=_params(sem))(*args)
        return list(outs), []
    ni, no, nci, nco, ns = len(in_specs), len(out_specs), len(comm.inputs), len(comm.out_shapes), len(scratch_shapes)
    hbm = pl.BlockSpec(memory_space=pl.ANY)

    def hosted(*refs):
        ins, cins = refs[:ni], refs[ni:ni + nci]
        outs, couts = refs[ni + nci:ni + nci + no], refs[ni + nci + no:ni + nci + no + nco]
        scratch = refs[ni + nci + no + nco:]
        sends, locals_, arrivals = comm.make(cins, couts, scratch[ns], scratch[ns + 1])
        first = last = None if grid else True
        for axis, extent in enumerate(grid):
            pid = pl.program_id(axis)
            first = (pid == 0) if first is None else first & (pid == 0)
            last = (pid == extent - 1) if last is None else last & (pid == extent - 1)

        @pl.when(first)
        def _():
            for cp in locals_ + sends:
                cp.start()

        body(*ins, *outs, *scratch[:ns])

        @pl.when(last)
        def _():
            for cp in arrivals:
                cp.wait_recv()
            for cp in sends:
                cp.wait_send()
            for cp in locals_:
                cp.wait()

    outs = pl.pallas_call(
        hosted, name=name, grid=grid, in_specs=in_specs + [hbm] * nci, out_specs=out_specs + [hbm] * nco,
        out_shape=out_shape + comm.out_shapes,
        scratch_shapes=list(scratch_shapes) + [pltpu.SemaphoreType.DMA((comm.n_sems,)), pltpu.SemaphoreType.DMA((comm.n_sems,))],
        input_output_aliases={ni + k: no + v for k, v in comm.aliases.items()},
        compiler_params=_params(sem),
    )(*args, *comm.inputs)
    return list(outs[:no]), list(outs[no:])


def rms_fwd(x, w, tm):
    t, d = x.shape

    def body(x_ref, w_ref, h_ref, r_ref):
        xv = x_ref[...]
        r = lax.rsqrt(jnp.mean(xv * xv, axis=-1, keepdims=True) + NORM_EPS)
        h_ref[...] = (xv * r * w_ref[...]).astype(BF16)
        r_ref[...] = r

    return pl.pallas_call(
        body, name="rms_fwd", grid=(t // tm,),
        in_specs=[pl.BlockSpec((tm, d), lambda i: (i, 0)), pl.BlockSpec((1, d), lambda i: (0, 0))],
        out_specs=[pl.BlockSpec((tm, d), lambda i: (i, 0)), pl.BlockSpec((tm, 1), lambda i: (i, 0))],
        out_shape=[SDS((t, d), BF16), SDS((t, 1), F32)],
        compiler_params=_params(("arbitrary",)),
    )(x, w)


def post_fwd(out, x, w, tm):
    t, d = x.shape

    def body(o_ref, x_ref, w_ref, y_ref, r_ref):
        ov = o_ref[...]
        r = lax.rsqrt(jnp.mean(ov * ov, axis=-1, keepdims=True) + NORM_EPS)
        y_ref[...] = x_ref[...] + ov * r * w_ref[...]
        r_ref[...] = r

    return pl.pallas_call(
        body, name="post_fwd", grid=(t // tm,),
        in_specs=[pl.BlockSpec((tm, d), lambda i: (i, 0)), pl.BlockSpec((tm, d), lambda i: (i, 0)),
                  pl.BlockSpec((1, d), lambda i: (0, 0))],
        out_specs=[pl.BlockSpec((tm, d), lambda i: (i, 0)), pl.BlockSpec((tm, 1), lambda i: (i, 0))],
        out_shape=[SDS((t, d), F32), SDS((t, 1), F32)],
        compiler_params=_params(("arbitrary",)),
    )(out, x, w)


def _norm_bwd(g_n, n, r):
    return r * (g_n - n * jnp.mean(g_n * n, axis=-1, keepdims=True))


def post_bwd(g, out, r, w, tm):
    t, d = g.shape

    def body(g_ref, o_ref, r_ref, w_ref, do_ref, dw_ref):
        i = pl.program_id(0)
        gv = g_ref[...]
        rv = r_ref[...]
        n = o_ref[...] * rv
        part = jnp.sum(gv * n, axis=0, keepdims=True)

        @pl.when(i == 0)
        def _():
            dw_ref[...] = part

        @pl.when(i > 0)
        def _():
            dw_ref[...] += part

        do_ref[...] = _norm_bwd(gv * w_ref[...], n, rv).astype(BF16)

    return pl.pallas_call(
        body, name="post_bwd", grid=(t // tm,),
        in_specs=[pl.BlockSpec((tm, d), lambda i: (i, 0)), pl.BlockSpec((tm, d), lambda i: (i, 0)),
                  pl.BlockSpec((tm, 1), lambda i: (i, 0)), pl.BlockSpec((1, d), lambda i: (0, 0))],
        out_specs=[pl.BlockSpec((tm, d), lambda i: (i, 0)), pl.BlockSpec((1, d), lambda i: (0, 0))],
        out_shape=[SDS((t, d), BF16), SDS((1, d), F32)],
        compiler_params=_params(("arbitrary",)),
    )(g, out, r, w)


def rms_bwd(dh_a, dh_b, x, r, w, g, tm):
    t, d = x.shape

    def body(a_ref, b_ref, x_ref, r_ref, w_ref, g_ref, gx_ref, dw_ref):
        i = pl.program_id(0)
        dh = a_ref[...] + b_ref[...]
        rv = r_ref[...]
        n = x_ref[...] * rv
        part = jnp.sum(dh * n, axis=0, keepdims=True)

        @pl.when(i == 0)
        def _():
            dw_ref[...] = part

        @pl.when(i > 0)
        def _():
            dw_ref[...] += part

        gx_ref[...] = g_ref[...] + _norm_bwd(dh * w_ref[...], n, rv)

    row = pl.BlockSpec((tm, d), lambda i: (i, 0))
    return pl.pallas_call(
        body, name="rms_bwd", grid=(t // tm,),
        in_specs=[row, row, row, pl.BlockSpec((tm, 1), lambda i: (i, 0)), pl.BlockSpec((1, d), lambda i: (0, 0)), row],
        out_specs=[row, pl.BlockSpec((1, d), lambda i: (0, 0))],
        out_shape=[SDS((t, d), F32), SDS((1, d), F32)],
        compiler_params=_params(("arbitrary",)),
    )(dh_a, dh_b, x, r, w, g)


def loss_grad(y, target, tm):
    t, d = y.shape

    def body(y_ref, t_ref, l_ref, g_ref):
        i = pl.program_id(0)
        err = y_ref[...] - t_ref[...]
        g_ref[...] = err / d
        part = 0.5 * jnp.sum(jnp.mean(err * err, axis=-1, keepdims=True), axis=0, keepdims=True)

        @pl.when(i == 0)
        def _():
            l_ref[...] = part

        @pl.when(i > 0)
        def _():
            l_ref[...] += part

    row = pl.BlockSpec((tm, d), lambda i: (i, 0))
    return pl.pallas_call(
        body, name="loss_grad", grid=(t // tm,), in_specs=[row, row],
        out_specs=[pl.BlockSpec((1, 1), lambda i: (0, 0)), row],
        out_shape=[SDS((1, 1), F32), SDS((t, d), F32)],
        compiler_params=_params(("arbitrary",)),
    )(y, target)


def mm_nn(a, b, out_dtype, tm, tn, name, comm=None):
    m, k = a.shape
    n = b.shape[1]

    def body(a_ref, b_ref, o_ref):
        o_ref[...] = jnp.dot(a_ref[...], b_ref[...], preferred_element_type=F32).astype(out_dtype)

    outs, couts = _call(
        body, (a, b), name=name, grid=(n // tn, m // tm),
        in_specs=[pl.BlockSpec((tm, k), lambda j, i: (i, 0)), pl.BlockSpec((k, tn), lambda j, i: (0, j))],
        out_specs=[pl.BlockSpec((tm, tn), lambda j, i: (i, j))],
        out_shape=[SDS((m, n), out_dtype)], sem=("arbitrary", "arbitrary"), comm=comm)
    return outs[0], couts


def mm_nt(a, b, out_dtype, tm, tn, tk, name, comm=None):
    m, k = a.shape
    n = b.shape[0]
    nk = k // tk

    def body(a_ref, b_ref, o_ref, acc_ref):
        kk = pl.program_id(2)
        part = lax.dot_general(a_ref[...], b_ref[...], NT, preferred_element_type=F32)
        if nk == 1:
            o_ref[...] = part.astype(out_dtype)
        else:
            @pl.when(kk == 0)
            def _():
                acc_ref[...] = part

            @pl.when(kk > 0)
            def _():
                acc_ref[...] += part

            @pl.when(kk == nk - 1)
            def _():
                o_ref[...] = acc_ref[...].astype(out_dtype)

    outs, couts = _call(
        body, (a, b), name=name, grid=(m // tm, n // tn, nk),
        in_specs=[pl.BlockSpec((tm, tk), lambda i, j, kk: (i, kk)), pl.BlockSpec((tn, tk), lambda i, j, kk: (j, kk))],
        out_specs=[pl.BlockSpec((tm, tn), lambda i, j, kk: (i, j))],
        out_shape=[SDS((m, n), out_dtype)],
        scratch_shapes=[pltpu.VMEM((tm, tn) if nk > 1 else (8, LANES), F32)],
        sem=("arbitrary", "arbitrary", "arbitrary"), comm=comm)
    return outs[0], couts


def mm_tn(a, b, tm, tn, name):
    t, m = a.shape
    n = b.shape[1]

    def body(a_ref, b_ref, o_ref):
        o_ref[...] = lax.dot_general(a_ref[...], b_ref[...], TN, preferred_element_type=F32)

    return pl.pallas_call(
        body, name=name, grid=(m // tm, n // tn),
        in_specs=[pl.BlockSpec((t, tm), lambda i, j: (0, i)), pl.BlockSpec((t, tn), lambda i, j: (0, j))],
        out_specs=pl.BlockSpec((tm, tn), lambda i, j: (i, j)),
        out_shape=SDS((m, n), F32),
        compiler_params=_params(("arbitrary", "arbitrary")),
    )(a, b)


def _window_sums(ext, n_rows, lookahead):
    def sh(v, k):
        return pltpu.roll(v, (n_rows - k) if lookahead else k, 0)
    s2 = ext + sh(ext, 1)
    s4 = s2 + sh(s2, 2)
    s8 = s4 + sh(s4, 4)
    s16 = s8 + sh(s8, 8)
    return (s2, s4, s8, s16)


def _pool_counts(i, tm, w):
    tpos = i * tm + lax.broadcasted_iota(jnp.int32, (tm, 1), 0)
    return jnp.minimum(tpos + 1, w).astype(F32)


def _pooled(uc_ref, up_ref, i, tm):
    cur = uc_ref[...]
    prev = jnp.where(i > 0, up_ref[...], 0.0)
    ext = jnp.concatenate([prev, cur], axis=0)
    return cur, _window_sums(ext, tm + POOL_HALO, False)


def pool_fwd(proj, mixw, scale, tm):
    t = proj.shape[0]
    pw = scale.shape[1]
    gw = pw // len(POOL_WINDOWS)
    nh = tm // POOL_HALO

    def body(uc_ref, up_ref, g_ref, w_ref, s_ref, o_ref):
        i = pl.program_id(0)
        cur, sums = _pooled(uc_ref, up_ref, i, tm)
        for g, w in enumerate(POOL_WINDOWS):
            cols = slice(g * gw, (g + 1) * gw)
            pooled = sums[g][POOL_HALO:, cols] / _pool_counts(i, tm, w) - cur[:, cols]
            mixed = jnp.dot(pooled.astype(BF16), w_ref[g], preferred_element_type=F32)
            o_ref[:, cols] = (mixed * s_ref[:, cols] * _silu(g_ref[:, cols])).astype(BF16)

    return pl.pallas_call(
        body, name="pool_fwd", grid=(t // tm,),
        in_specs=[pl.BlockSpec((tm, pw), lambda i: (i, 0)),
                  pl.BlockSpec((POOL_HALO, pw), lambda i: (jnp.maximum(i * nh - 1, 0), 0)),
                  pl.BlockSpec((tm, pw), lambda i: (i, 1)),
                  pl.BlockSpec(mixw.shape, lambda i: (0, 0, 0)),
                  pl.BlockSpec((1, pw), lambda i: (0, 0))],
        out_specs=pl.BlockSpec((tm, pw), lambda i: (i, 0)),
        out_shape=SDS((t, 2 * pw), BF16),
        compiler_params=_params(("arbitrary",)),
    )(proj, proj, proj, mixw, scale)


def pool_bwd_a(dmixed, proj, mixw, scale, tm):
    t, e = proj.shape
    pw = scale.shape[1]
    ng = len(POOL_WINDOWS)
    gw = pw // ng
    nh = tm // POOL_HALO

    def body(dy_ref, uc_ref, up_ref, g_ref, w_ref, s_ref, dg_ref, dq_ref, ds_ref, dw_ref):
        i = pl.program_id(0)

        @pl.when(i == 0)
        def _():
            ds_ref[...] = jnp.zeros_like(ds_ref)
            dw_ref[...] = jnp.zeros_like(dw_ref)

        cur, sums = _pooled(uc_ref, up_ref, i, tm)
        for g, w in enumerate(POOL_WINDOWS):
            cols = slice(g * gw, (g + 1) * gw)
            cnt = _pool_counts(i, tm, w)
            pooled = (sums[g][POOL_HALO:, cols] / cnt - cur[:, cols]).astype(BF16)
            mixed = jnp.dot(pooled, w_ref[g], preferred_element_type=F32)
            gate = g_ref[:, cols]
            dy = dy_ref[:, cols]
            sc = s_ref[:, cols]
            dg_ref[:, cols] = (dy * mixed * sc * _dsilu(gate)).astype(BF16)
            ds = dy * _silu(gate)
            ds_ref[:, cols] += jnp.sum(ds * mixed, axis=0, keepdims=True)
            dmix = (ds * sc).astype(BF16)
            dw_ref[g] += lax.dot_general(pooled, dmix, TN, preferred_element_type=F32)
            dq_ref[:, cols] = lax.dot_general(dmix, w_ref[g], NT, preferred_element_type=F32) / cnt

    return pl.pallas_call(
        body, name="pool_bwd_a", grid=(t // tm,),
        in_specs=[pl.BlockSpec((tm, pw), lambda i: (i, 0)),
                  pl.BlockSpec((tm, pw), lambda i: (i, 0)),
                  pl.BlockSpec((POOL_HALO, pw), lambda i: (jnp.maximum(i * nh - 1, 0), 0)),
                  pl.BlockSpec((tm, pw), lambda i: (i, 1)),
                  pl.BlockSpec(mixw.shape, lambda i: (0, 0, 0)),
                  pl.BlockSpec((1, pw), lambda i: (0, 0))],
        out_specs=[pl.BlockSpec((tm, pw), lambda i: (i, 1)),
                   pl.BlockSpec((tm, pw), lambda i: (i, 0)),
                   pl.BlockSpec((1, pw), lambda i: (0, 0)),
                   pl.BlockSpec((ng, gw, gw), lambda i: (0, 0, 0))],
        out_shape=[SDS((t, e), BF16), SDS((t, pw), F32), SDS((1, pw), F32), SDS((ng, gw, gw), F32)],
        compiler_params=_params(("arbitrary",)),
    )(dmixed, proj, proj, proj, mixw, scale)


def pool_bwd_b(dq, dproj, tm):
    t, pw = dq.shape
    gw = pw // len(POOL_WINDOWS)
    nh = tm // POOL_HALO
    nt = t // tm

    def body(c_ref, n_ref, alias_ref, o_ref):
        i = pl.program_id(0)
        cur = c_ref[...]
        nxt = jnp.where(i < nt - 1, n_ref[...], 0.0)
        sums = _window_sums(jnp.concatenate([cur, nxt], axis=0), tm + POOL_HALO, True)
        for g, w in enumerate(POOL_WINDOWS):
            cols = slice(g * gw, (g + 1) * gw)
            o_ref[:, cols] = (sums[g][:tm, cols] - cur[:, cols] * _pool_counts(i, tm, w)).astype(BF16)

    return pl.pallas_call(
        body, name="pool_bwd_b", grid=(nt,),
        in_specs=[pl.BlockSpec((tm, pw), lambda i: (i, 0)),
                  pl.BlockSpec((POOL_HALO, pw), lambda i: (jnp.minimum((i + 1) * nh, t // POOL_HALO - 1), 0)),
                  pl.BlockSpec(memory_space=pl.ANY)],
        out_specs=pl.BlockSpec((tm, pw), lambda i: (i, 0)),
        out_shape=SDS(dproj.shape, dproj.dtype),
        input_output_aliases={2: 0},
        compiler_params=_params(("arbitrary",)),
    )(dq, dq, dproj)


def _conv_pre(xc_ref, xp_ref, w_ref, b_ref, i, tm):
    cur = xc_ref[...]
    prev = jnp.where(i > 0, xp_ref[...], 0.0)
    ext = jnp.concatenate([prev, cur], axis=0)
    taps = [pltpu.roll(ext, CONV_WIDTH - 1 - k, 0)[CONV_HALO:] for k in range(CONV_WIDTH - 1)] + [cur]
    pre = b_ref[...]
    for k in range(CONV_WIDTH):
        pre = pre + w_ref[k:k + 1, :] * taps[k]
    return pre, taps


def conv_fwd(proj, conv_w, conv_b, col_block, tm):
    t = proj.shape[0]
    cd = conv_b.shape[1]
    nh = tm // CONV_HALO

    def body(xc_ref, xp_ref, w_ref, b_ref, o_ref):
        i = pl.program_id(0)
        pre, _ = _conv_pre(xc_ref, xp_ref, w_ref, b_ref, i, tm)
        o_ref[...] = _silu(pre)

    return pl.pallas_call(
        body, name="conv_fwd", grid=(t // tm,),
        in_specs=[pl.BlockSpec((tm, cd), lambda i: (i, col_block)),
                  pl.BlockSpec((CONV_HALO, cd), lambda i: (jnp.maximum(i * nh - 1, 0), col_block)),
                  pl.BlockSpec((CONV_WIDTH, cd), lambda i: (0, 0)),
                  pl.BlockSpec((1, cd), lambda i: (0, 0))],
        out_specs=pl.BlockSpec((tm, cd), lambda i: (i, 0)),
        out_shape=SDS((t, cd), F32),
        compiler_params=_params(("arbitrary",)),
    )(proj, proj, conv_w, conv_b)


def conv_bwd_a(dxs, db, dc, proj, conv_w, conv_b, col_block, tm):
    t = proj.shape[0]
    cd = conv_b.shape[1]
    sw = dxs.shape[1]
    gn = db.shape[1]
    nh = tm // CONV_HALO

    def body(dx_ref, db_ref, dc_ref, xc_ref, xp_ref, w_ref, b_ref, dp_ref, dw_ref, dbias_ref):
        i = pl.program_id(0)

        @pl.when(i == 0)
        def _():
            dw_ref[...] = jnp.zeros_like(dw_ref)
            dbias_ref[...] = jnp.zeros_like(dbias_ref)

        pre, taps = _conv_pre(xc_ref, xp_ref, w_ref, b_ref, i, tm)
        dact = jnp.concatenate([dx_ref[...], db_ref[...], dc_ref[...]], axis=1)
        dpre = dact * _dsilu(pre)
        dp_ref[...] = dpre
        dbias_ref[...] += jnp.sum(dpre, axis=0, keepdims=True)
        for k in range(CONV_WIDTH):
            dw_ref[k:k + 1, :] += jnp.sum(dpre * taps[k], axis=0, keepdims=True)

    return pl.pallas_call(
        body, name="conv_bwd_a", grid=(t // tm,),
        in_specs=[pl.BlockSpec((tm, sw), lambda i: (i, 0)), pl.BlockSpec((tm, gn), lambda i: (i, 0)),
                  pl.BlockSpec((tm, gn), lambda i: (i, 0)),
                  pl.BlockSpec((tm, cd), lambda i: (i, col_block)),
                  pl.BlockSpec((CONV_HALO, cd), lambda i: (jnp.maximum(i * nh - 1, 0), col_block)),
                  pl.BlockSpec((CONV_WIDTH, cd), lambda i: (0, 0)),
                  pl.BlockSpec((1, cd), lambda i: (0, 0))],
        out_specs=[pl.BlockSpec((tm, cd), lambda i: (i, 0)),
                   pl.BlockSpec((CONV_WIDTH, cd), lambda i: (0, 0)),
                   pl.BlockSpec((1, cd), lambda i: (0, 0))],
        out_shape=[SDS((t, cd), F32), SDS((CONV_WIDTH, cd), F32), SDS((1, cd), F32)],
        compiler_params=_params(("arbitrary",)),
    )(dxs, db, dc, proj, proj, conv_w, conv_b)


def conv_bwd_b(dpre, conv_w, dproj, col_block, tm):
    t, cd = dpre.shape
    nh = tm // CONV_HALO
    nt = t // tm

    def body(c_ref, n_ref, w_ref, alias_ref, o_ref):
        i = pl.program_id(0)
        cur = c_ref[...]
        nxt = jnp.where(i < nt - 1, n_ref[...], 0.0)
        ext = jnp.concatenate([cur, nxt], axis=0)
        n = tm + CONV_HALO
        acc = w_ref[CONV_WIDTH - 1:CONV_WIDTH, :] * cur
        for k in range(CONV_WIDTH - 1):
            acc = acc + w_ref[k:k + 1, :] * pltpu.roll(ext, n - (CONV_WIDTH - 1 - k), 0)[:tm]
        o_ref[...] = acc.astype(BF16)

    return pl.pallas_call(
        body, name="conv_bwd_b", grid=(nt,),
        in_specs=[pl.BlockSpec((tm, cd), lambda i: (i, 0)),
                  pl.BlockSpec((CONV_HALO, cd), lambda i: (jnp.minimum((i + 1) * nh, t // CONV_HALO - 1), 0)),
                  pl.BlockSpec((CONV_WIDTH, cd), lambda i: (0, 0)),
                  pl.BlockSpec(memory_space=pl.ANY)],
        out_specs=pl.BlockSpec((tm, cd), lambda i: (i, col_block)),
        out_shape=SDS(dproj.shape, dproj.dtype),
        input_output_aliases={3: 0},
        compiler_params=_params(("arbitrary",)),
    )(dpre, dpre, conv_w, dproj)


def _softplus(v):
    return jnp.maximum(v, 0.0) + jnp.log(1.0 + jnp.exp(-jnp.abs(v)))


def _ssd_chunk_terms(dtr_ref, bias_ref, a_ref, n_heads):
    q = SSD_CHUNK
    lane = lax.broadcasted_iota(jnp.int32, (1, LANES), 1)
    pre = dtr_ref[...] + bias_ref[...]
    dt = jnp.where(lane < n_heads, _softplus(pre), 0.0)
    a = jnp.where(lane < n_heads, -jnp.exp(a_ref[...]), 0.0)
    row = lax.broadcasted_iota(jnp.int32, (q, q), 0)
    col = lax.broadcasted_iota(jnp.int32, (q, q), 1)
    causal = row >= col
    acs = jnp.dot(causal.astype(F32), dt * a, precision=HIGHEST, preferred_element_type=F32)
    last = acs[q - 1:q, :]
    return dict(pre=pre, dt=dt, a=a, acs=acs, acs_t=acs.T, eacs=jnp.exp(acs), dstate=jnp.exp(last - acs),
                cdec=jnp.exp(last), causal=causal, lane=lane)


def _pair_cols(lo, v, h):
    return jnp.where(lo, v[:, h:h + 1], v[:, h + 1:h + 2])


def _pair_decay(tm_, cb, h):
    l0 = jnp.exp(jnp.where(tm_["causal"], tm_["acs"][:, h:h + 1] - tm_["acs_t"][h:h + 1, :], -jnp.inf))
    l1 = jnp.exp(jnp.where(tm_["causal"], tm_["acs"][:, h + 1:h + 2] - tm_["acs_t"][h + 1:h + 2, :], -jnp.inf))
    return l0, l1, jnp.concatenate([cb * l0, cb * l1], axis=1)


def _column_sums(v):
    ones = jnp.ones((v.shape[0], LANES), BF16)
    hi = v.astype(BF16)
    r1 = v - hi.astype(F32)
    mid = r1.astype(BF16)
    lo = (r1 - mid.astype(F32)).astype(BF16)
    return sum(lax.dot_general(part, ones, TN, preferred_element_type=F32) for part in (hi, mid, lo))


def _block_diag(lo, xdt):
    return jnp.concatenate([jnp.where(lo, xdt, 0.0), jnp.where(lo, 0.0, xdt)], axis=0).astype(BF16)


def ssd_fwd(xbc, dt_raw, dt_bias, a_log, n_heads, comm=None):
    t = xbc.shape[0]
    q = SSD_CHUNK
    gn = SSD_GROUPS * SSD_STATE
    sw = n_heads * SSD_HEAD_DIM
    n_pairs = n_heads // 2
    pairs_per_group = n_pairs // SSD_GROUPS
    nc = t // q
    bblk = sw // gn

    def body(xs_ref, b_ref, c_ref, dtr_ref, bias_ref, a_ref, y_ref, sin_ref, state):
        @pl.when(pl.program_id(0) == 0)
        def _():
            state[...] = jnp.zeros_like(state)

        tm_ = _ssd_chunk_terms(dtr_ref, bias_ref, a_ref, n_heads)
        lo = tm_["lane"] < SSD_HEAD_DIM
        for g in range(SSD_GROUPS):
            gcols = slice(g * SSD_STATE, (g + 1) * SSD_STATE)
            bg = b_ref[:, gcols].astype(BF16)
            cg = c_ref[:, gcols].astype(BF16)
            cb = lax.dot_general(cg, bg, NT, preferred_element_type=F32)
            for j in range(pairs_per_group):
                p = g * pairs_per_group + j
                h = 2 * p
                pcols = slice(p * LANES, (p + 1) * LANES)
                _, _, mcat = _pair_decay(tm_, cb, h)
                xdt = xs_ref[:, pcols] * _pair_cols(lo, tm_["dt"], h)
                ydiag = jnp.dot(mcat.astype(BF16), _block_diag(lo, xdt), preferred_element_type=F32)
                st = state[p]
                sin_ref[0, p] = st
                yoff = jnp.dot(cg, st.astype(BF16), preferred_element_type=F32) * _pair_cols(lo, tm_["eacs"], h)
                y_ref[:, pcols] = ydiag + yoff
                xw = (xdt * _pair_cols(lo, tm_["dstate"], h)).astype(BF16)
                state[p] = st * _pair_cols(lo, tm_["cdec"], h) + lax.dot_general(bg, xw, TN, preferred_element_type=F32)

    vec = pl.BlockSpec((1, LANES), lambda c: (0, 0))
    return _call(
        body, (xbc, xbc, xbc, dt_raw, dt_bias, a_log), name="ssd_fwd", grid=(nc,),
        in_specs=[pl.BlockSpec((q, sw), lambda c: (c, 0)),
                  pl.BlockSpec((q, gn), lambda c: (c, bblk)),
                  pl.BlockSpec((q, gn), lambda c: (c, bblk + 1)),
                  pl.BlockSpec((q, LANES), lambda c: (c, 0)), vec, vec],
        out_specs=[pl.BlockSpec((q, sw), lambda c: (c, 0)),
                   pl.BlockSpec((1, n_pairs, SSD_STATE, LANES), lambda c: (c, 0, 0, 0))],
        out_shape=[SDS((t, sw), F32), SDS((nc, n_pairs, SSD_STATE, LANES), F32)],
        scratch_shapes=[pltpu.VMEM((n_pairs, SSD_STATE, LANES), F32)],
        sem=("arbitrary",), comm=comm)


def ssd_bwd(dy, xbc, dt_raw, dt_bias, a_log, d_full, s_in, n_heads, comm=None):
    t = xbc.shape[0]
    q = SSD_CHUNK
    gn = SSD_GROUPS * SSD_STATE
    sw = n_heads * SSD_HEAD_DIM
    n_pairs = n_heads // 2
    pairs_per_group = n_pairs // SSD_GROUPS
    nc = t // q
    bblk = sw // gn

    def body(dy_ref, xs_ref, b_ref, c_ref, dtr_ref, bias_ref, a_ref, dsk_ref, sin_ref,
             dxs_ref, db_ref, dc_ref, ddtr_ref, dbias_ref, dalog_ref,
             dstate, tbuf, xbuf, rbuf, acc_a, acc_b):
        i = pl.program_id(0)

        @pl.when(i == 0)
        def _():
            dstate[...] = jnp.zeros_like(dstate)
            rbuf[...] = jnp.zeros_like(rbuf)
            acc_a[...] = jnp.zeros_like(acc_a)
            acc_b[...] = jnp.zeros_like(acc_b)

        tm_ = _ssd_chunk_terms(dtr_ref, bias_ref, a_ref, n_heads)
        lane = tm_["lane"]
        lo = lane < SSD_HEAD_DIM
        intra = jnp.zeros((q, LANES), F32)
        for g in range(SSD_GROUPS):
            gcols = slice(g * SSD_STATE, (g + 1) * SSD_STATE)
            bg = b_ref[:, gcols].astype(BF16)
            cg = c_ref[:, gcols].astype(BF16)
            cb = lax.dot_general(cg, bg, NT, preferred_element_type=F32)
            dcb = jnp.zeros((q, q), F32)
            db_acc = jnp.zeros((q, SSD_STATE), F32)
            dc_acc = jnp.zeros((q, SSD_STATE), F32)
            for j in range(pairs_per_group):
                p = g * pairs_per_group + j
                h = 2 * p
                pcols = slice(p * LANES, (p + 1) * LANES)
                l0, l1, mcat = _pair_decay(tm_, cb, h)
                xp = xs_ref[:, pcols]
                dtp = _pair_cols(lo, tm_["dt"], h)
                xdt = xp * dtp
                xbd = _block_diag(lo, xdt)
                dyp = dy_ref[:, pcols]
                dyb = dyp.astype(BF16)
                dsb = _pair_cols(lo, tm_["dstate"], h)
                cdr = _pair_cols(lo, tm_["cdec"], h)
                eb = _pair_cols(lo, tm_["eacs"], h)
                st = sin_ref[0, p]
                stb = st.astype(BF16)
                dst = dstate[p]
                dstb = dst.astype(BF16)
                dye = (dyp * eb).astype(BF16)
                both = lax.dot_general(mcat.astype(BF16), dyb, TN, preferred_element_type=F32)
                dx_state = jnp.dot(bg, dstb, preferred_element_type=F32) * dsb
                dxdt = jnp.where(lo, both[:q], both[q:]) + dx_state
                dmcat = lax.dot_general(dyb, xbd, NT, preferred_element_type=F32)
                dcb = dcb + dmcat[:, :q] * l0 + dmcat[:, q:] * l1
                dseg = dmcat * mcat
                csum = _column_sums(dseg)
                intra = (intra
                         + jnp.where(lane == h, jnp.sum(dseg[:, :q], axis=1, keepdims=True) - csum[:q], 0.0)
                         + jnp.where(lane == h + 1, jnp.sum(dseg[:, q:], axis=1, keepdims=True) - csum[q:], 0.0))
                dc_acc = dc_acc + lax.dot_general(dye, stb, NT, preferred_element_type=F32)
                db_acc = db_acc + lax.dot_general((xdt * dsb).astype(BF16), dstb, NT, preferred_element_type=F32)
                yoff = jnp.dot(cg, stb, preferred_element_type=F32) * eb
                tbuf[:, pcols] = dyp * yoff - xdt * dx_state
                xbuf[:, pcols] = dxdt * xp
                rbuf[0:1, pcols] = (jnp.sum(xdt * dx_state, axis=0, keepdims=True)
                                    + cdr * jnp.sum(dst * st, axis=0, keepdims=True))
                dxs_ref[:, pcols] = dxdt * dtp + dyp * dsk_ref[:, pcols]
                dstate[p] = dst * cdr + lax.dot_general(cg, dye, TN, preferred_element_type=F32)
            dcbb = dcb.astype(BF16)
            dc_ref[:, gcols] = dc_acc + jnp.dot(dcbb, bg, preferred_element_type=F32)
            db_ref[:, gcols] = db_acc + lax.dot_general(dcbb, cg, TN, preferred_element_type=F32)

        sel = _head_selector(sw, SSD_HEAD_DIM)
        dacs = intra + _split_dot(tbuf[...], sel)
        carry = _split_dot(rbuf[...], sel)[0:1]
        anti = jnp.logical_not(tm_["causal"]) | (lax.broadcasted_iota(jnp.int32, (q, q), 0)
                                                 == lax.broadcasted_iota(jnp.int32, (q, q), 1))
        da = jnp.dot(anti.astype(F32), dacs, precision=HIGHEST, preferred_element_type=F32) + carry
        ddt = da * tm_["a"] + _split_dot(xbuf[...], sel)
        ddtr = jnp.where(tm_["lane"] < n_heads, ddt * jax.nn.sigmoid(tm_["pre"]), 0.0)
        ddtr_ref[...] = ddtr.astype(BF16)
        acc_b[...] += jnp.sum(ddtr, axis=0, keepdims=True)
        acc_a[...] += jnp.sum(da * tm_["dt"], axis=0, keepdims=True)

        @pl.when(i == nc - 1)
        def _():
            dbias_ref[...] = acc_b[...]
            dalog_ref[...] = acc_a[...] * tm_["a"]

    vec = pl.BlockSpec((1, LANES), lambda i: (0, 0))
    wide = pl.BlockSpec((q, sw), lambda i: (nc - 1 - i, 0))
    return _call(
        body, (dy, xbc, xbc, xbc, dt_raw, dt_bias, a_log, d_full, s_in), name="ssd_bwd", grid=(nc,),
        in_specs=[wide, wide,
                  pl.BlockSpec((q, gn), lambda i: (nc - 1 - i, bblk)),
                  pl.BlockSpec((q, gn), lambda i: (nc - 1 - i, bblk + 1)),
                  pl.BlockSpec((q, LANES), lambda i: (nc - 1 - i, 0)), vec, vec,
                  pl.BlockSpec((1, sw), lambda i: (0, 0)),
                  pl.BlockSpec((1, n_pairs, SSD_STATE, LANES), lambda i: (nc - 1 - i, 0, 0, 0))],
        out_specs=[wide, pl.BlockSpec((q, gn), lambda i: (nc - 1 - i, 0)), pl.BlockSpec((q, gn), lambda i: (nc - 1 - i, 0)),
                   pl.BlockSpec((q, LANES), lambda i: (nc - 1 - i, 0)), vec, vec],
        out_shape=[SDS((t, sw), F32), SDS((t, gn), F32), SDS((t, gn), F32), SDS((t, LANES), BF16),
                   SDS((1, LANES), F32), SDS((1, LANES), F32)],
        scratch_shapes=[pltpu.VMEM((n_pairs, SSD_STATE, LANES), F32), pltpu.VMEM((q, sw), F32), pltpu.VMEM((q, sw), F32),
                        pltpu.VMEM((8, sw), F32), pltpu.VMEM((1, LANES), F32), pltpu.VMEM((1, LANES), F32)],
        sem=("arbitrary",), comm=comm)


def _gated(y_ref, xs_ref, z_ref, dsk_ref):
    y1 = y_ref[...] + dsk_ref[...] * xs_ref[...]
    return y1, y1 * _silu(z_ref[...])


def gate_norm_fwd(y, xbc, proj, d_full, norm_w, mixed, z_block, tm):
    t, sw = y.shape
    gw = sw // SSD_GROUPS

    def body(y_ref, xs_ref, z_ref, dsk_ref, nw_ref, alias_ref, o_ref):
        _, y2 = _gated(y_ref, xs_ref, z_ref, dsk_ref)
        for g in range(SSD_GROUPS):
            cols = slice(g * gw, (g + 1) * gw)
            blk = y2[:, cols]
            r = lax.rsqrt(jnp.mean(blk * blk, axis=-1, keepdims=True) + NORM_EPS)
            o_ref[:, cols] = (blk * r * nw_ref[:, cols]).astype(BF16)

    row = pl.BlockSpec((tm, sw), lambda i: (i, 0))
    vec = pl.BlockSpec((1, sw), lambda i: (0, 0))
    return pl.pallas_call(
        body, name="gate_norm_fwd", grid=(t // tm,),
        in_specs=[row, row, pl.BlockSpec((tm, sw), lambda i: (i, z_block)), vec, vec, pl.BlockSpec(memory_space=pl.ANY)],
        out_specs=pl.BlockSpec((tm, sw), lambda i: (i, 1)),
        out_shape=SDS(mixed.shape, mixed.dtype),
        input_output_aliases={5: 0},
        compiler_params=_params(("arbitrary",)),
    )(y, xbc, proj, d_full, norm_w, mixed)


def gate_norm_bwd(dmixed, y, xbc, proj, d_full, norm_w, dproj, z_block, tm):
    t, sw = y.shape
    gw = sw // SSD_GROUPS
    nt = t // tm

    def body(d_ref, y_ref, xs_ref, z_ref, dsk_ref, nw_ref, alias_ref, dy_ref, dz_ref, dnw_ref, dd_ref, acc_d):
        i = pl.program_id(0)

        @pl.when(i == 0)
        def _():
            dnw_ref[...] = jnp.zeros_like(dnw_ref)
            acc_d[...] = jnp.zeros_like(acc_d)

        y1, y2 = _gated(y_ref, xs_ref, z_ref, dsk_ref)
        d3 = d_ref[...]
        parts = []
        for g in range(SSD_GROUPS):
            cols = slice(g * gw, (g + 1) * gw)
            blk = y2[:, cols]
            r = lax.rsqrt(jnp.mean(blk * blk, axis=-1, keepdims=True) + NORM_EPS)
            n = blk * r
            dg = d3[:, cols]
            dnw_ref[:, cols] += jnp.sum(dg * n, axis=0, keepdims=True)
            parts.append(_norm_bwd(dg * nw_ref[:, cols], n, r))
        dy2 = jnp.concatenate(parts, axis=1)
        zv = z_ref[...]
        dz_ref[...] = (dy2 * y1 * _dsilu(zv)).astype(BF16)
        dy1 = dy2 * _silu(zv)
        dy_ref[...] = dy1
        acc_d[0:1, :] += jnp.sum(dy1 * xs_ref[...], axis=0, keepdims=True)

        @pl.when(i == nt - 1)
        def _():
            dd_ref[...] = _split_dot(acc_d[...], _head_selector(sw, SSD_HEAD_DIM))[0:1]

    row = pl.BlockSpec((tm, sw), lambda i: (i, 0))
    vec = pl.BlockSpec((1, sw), lambda i: (0, 0))
    return pl.pallas_call(
        body, name="gate_norm_bwd", grid=(nt,),
        in_specs=[pl.BlockSpec((tm, sw), lambda i: (i, 1)), row, row, pl.BlockSpec((tm, sw), lambda i: (i, z_block)),
                  vec, vec, pl.BlockSpec(memory_space=pl.ANY)],
        out_specs=[row, pl.BlockSpec((tm, sw), lambda i: (i, z_block)), vec, pl.BlockSpec((1, LANES), lambda i: (0, 0))],
        out_shape=[SDS((t, sw), F32), SDS(dproj.shape, dproj.dtype), SDS((1, sw), F32), SDS((1, LANES), F32)],
        scratch_shapes=[pltpu.VMEM((8, sw), F32)],
        input_output_aliases={6: 1},
        compiler_params=_params(("arbitrary",)),
    )(dmixed, y, xbc, proj, d_full, norm_w, dproj)


GATE_BLOCK = 1
Z_BLOCK = 2
CONV_BLOCK = 2


def _tiles(t):
    return min(256, t), min(512, t)


def _place():
    x, y, c = lax.axis_index("x"), lax.axis_index("y"), lax.axis_index("c")
    return x, y, c, [(1 - x, y), (x, 1 - y), (1 - x, 1 - y)]


def gather_spread(shards, layer):
    def make(ins, outs, ss, rs):
        x, y, c, chips = _place()
        mine = 4 * x + 2 * y + c
        peers = [(x, y, 1 - c)] + [(px, py, c) for px, py in chips]
        sends, locals_, arrivals = [], [], []
        for a in range(len(ins)):
            src = ins[a].at[layer]
            locals_.append(pltpu.make_async_copy(src, outs[a].at[mine], ss.at[5 * a + 4]))
            for j, (px, py, pc) in enumerate(peers):
                sends.append(_remote(src, outs[a].at[mine], ss, rs, 5 * a + j, (px, py, pc)))
                arrivals.append(_remote(src, outs[a].at[4 * px + 2 * py + pc], ss, rs, 5 * a + j, (px, py, pc)))
        return sends, locals_, arrivals

    return Comm(shards, [SDS((N_DEV,) + s.shape[1:], s.dtype) for s in shards], {}, 5 * len(shards), make)


def gather_pass_on(gathered):
    def make(ins, outs, ss, rs):
        x, y, c, chips = _place()
        sends, arrivals = [], []
        for a in range(len(outs)):
            for j, (px, py) in enumerate(chips):
                blk, other = 4 * px + 2 * py + c, 4 * px + 2 * py + (1 - c)
                sends.append(_remote(outs[a].at[blk], outs[a].at[blk], ss, rs, 3 * a + j, (x, y, 1 - c)))
                arrivals.append(_remote(outs[a].at[other], outs[a].at[other], ss, rs, 3 * a + j, (x, y, 1 - c)))
        return sends, [], arrivals

    return Comm(gathered, [SDS(g.shape, g.dtype) for g in gathered], {a: a for a in range(len(gathered))},
                3 * len(gathered), make)


def sibling_swap(sends_):
    def make(ins, outs, ss, rs):
        x, y, c, _ = _place()
        cps = [_remote(ins[a], outs[a], ss, rs, a, (x, y, 1 - c)) for a in range(len(ins))]
        return cps, [], cps

    return Comm(sends_, [SDS(s.shape, s.dtype) for s in sends_], {}, len(sends_), make)


def chips_scatter(slabs):
    def make(ins, outs, ss, rs):
        x, y, c, chips = _place()
        mychip = 2 * x + y
        sends, arrivals = [], []
        for a in range(len(ins)):
            for j, (px, py) in enumerate(chips):
                k = 2 * px + py
                sends.append(_remote(ins[a].at[k], outs[a].at[lax.rem(mychip - k + 4, 4) - 1], ss, rs, 3 * a + j, (px, py, c)))
                arrivals.append(_remote(ins[a].at[k], outs[a].at[lax.rem(k - mychip + 4, 4) - 1], ss, rs, 3 * a + j, (px, py, c)))
        return sends, [], arrivals

    return Comm(slabs, [SDS((3,) + s.shape[1:], s.dtype) for s in slabs], {}, 3 * len(slabs), make)


def comm_only(comm, name):
    def body():
        pass

    return _call(body, (), name=name, grid=(), in_specs=[], out_specs=[], out_shape=[], comm=comm)[1]


def layer_fwd(x, p, nxt=None):
    t = x.shape[0]
    tm, tmm = _tiles(t)
    n_heads = p["d_full"].shape[1] // SSD_HEAD_DIM
    h, r_pre = rms_fwd(x, p["pre_w"], tm)
    proj, got_a = mm_nn(h, p["w_main"], F32, tmm, 1024, "in_proj", gather_spread(nxt[0], nxt[2]) if nxt else None)
    dt_raw, _ = mm_nn(h, p["w_dt"], F32, tmm, LANES, "dt_proj")
    mixed = pool_fwd(proj, p["mixw"], p["pscale"], tm)
    xbc = conv_fwd(proj, p["conv_w"], p["conv_b"], CONV_BLOCK, tm)
    (y, s_in), got_b = ssd_fwd(xbc, dt_raw, p["dt_bias"], p["a_log"], n_heads, gather_spread(nxt[1], nxt[2]) if nxt else None)
    mixed = gate_norm_fwd(y, xbc, proj, p["d_full"], p["norm_w"], mixed, Z_BLOCK, tm)
    out, gathered = mm_nn(mixed, p["w_out"], F32, tmm, 512, "out_proj", gather_pass_on(got_a + got_b) if nxt else None)
    x_next, r_post = post_fwd(out, x, p["post_w"], tm)
    return x_next, dict(x=x, h=h, r_pre=r_pre, proj=proj, dt_raw=dt_raw, xbc=xbc, y=y, s_in=s_in, mixed=mixed,
                        out=out, r_post=r_post), gathered


def layer_bwd(g, s, p, pending=None):
    t = g.shape[0]
    tm, tmm = _tiles(t)
    d = g.shape[1]
    n_heads = p["d_full"].shape[1] // SSD_HEAD_DIM
    d_out, d_post = post_bwd(g, s["out"], s["r_post"], p["post_w"], tm)
    dmixed, got_sib = mm_nt(d_out, p["w_out"], F32, tmm, 1024, d, "d_mixed", sibling_swap(pending[1]) if pending else None)
    chip_sums = [pair_sum(o, r, min(256, o.shape[1]), "pair_sum") for o, r in zip(pending[0], got_sib)] if pending else []
    dw_out = mm_tn(s["mixed"], d_out, 512, 512, "dw_out")
    dproj, dq, d_pscale, d_mixw = pool_bwd_a(dmixed, s["proj"], p["mixw"], p["pscale"], tm)
    dproj = pool_bwd_b(dq, dproj, tm)
    dy, dproj, d_norm, d_dskip = gate_norm_bwd(dmixed, s["y"], s["xbc"], s["proj"], p["d_full"], p["norm_w"], dproj,
                                               Z_BLOCK, tm)
    (dxs, db, dc, ddtr, d_dtb, d_alog), got_first = ssd_bwd(
        dy, s["xbc"], s["dt_raw"], p["dt_bias"], p["a_log"], p["d_full"], s["s_in"], n_heads,
        chips_scatter(chip_sums[:1]) if pending else None)
    dpre, d_convw, d_convb = conv_bwd_a(dxs, db, dc, s["proj"], p["conv_w"], p["conv_b"], CONV_BLOCK, tm)
    dproj = conv_bwd_b(dpre, p["conv_w"], dproj, CONV_BLOCK, tm)
    dh_main, got_rest = mm_nt(dproj, p["w_main"], F32, tmm, d, 1024, "dh_main", chips_scatter(chip_sums[1:]) if pending else None)
    dh_dt, _ = mm_nt(ddtr, p["w_dt"], F32, tmm, d, LANES, "dh_dt")
    dw_main = mm_tn(s["h"], dproj, 512, 1024, "dw_main")
    dw_dt = mm_tn(s["h"], ddtr, 512, LANES, "dw_dt")
    gx, d_pre = rms_bwd(dh_main, dh_dt, s["x"], s["r_pre"], p["pre_w"], g, tm)
    grads = dict(pre_w=d_pre, w_main=dw_main, w_dt=dw_dt, mixw=d_mixw, pscale=d_pscale, conv_w=d_convw, conv_b=d_convb,
                 dt_bias=d_dtb, a_log=d_alog, d_skip=d_dskip, norm_w=d_norm, w_out=dw_out, post_w=d_post)
    return gx, grads, (got_sib, got_first + got_rest)


def _two_level_gather(x_refs, out_slots, send_sems, recv_sems, local_sems):
    x, y, c, chips = _place()
    me, sibling = (x, y, c), (x, y, 1 - c)
    n = len(x_refs)

    def copy(a, k, block, to, src=None):
        return pltpu.make_async_remote_copy(
            src_ref=out_slots[a](*block) if src is None else src, dst_ref=out_slots[a](*block),
            send_sem=send_sems.at[7 * a + k], recv_sem=recv_sems.at[7 * a + k], device_id=to, device_id_type=MESH)

    mine = [pltpu.make_async_copy(x_refs[a], out_slots[a](*me), local_sems.at[a]) for a in range(n)]
    for cp in mine:
        cp.start()
    first = []
    for a in range(n):
        first.append(copy(a, 0, me, sibling, src=x_refs[a]))
        first += [copy(a, 1 + j, me, (*chip, c), src=x_refs[a]) for j, chip in enumerate(chips)]
    for cp in first:
        cp.start()
    passed = []
    for j, chip in enumerate(chips):
        for a in range(n):
            copy(a, 1 + j, (*chip, c), me).wait_recv()
            fwd = copy(a, 4 + j, (*chip, c), sibling)
            fwd.start()
            passed.append(fwd)
    for a in range(n):
        copy(a, 0, sibling, me).wait_recv()
        for j, chip in enumerate(chips):
            copy(a, 4 + j, (*chip, 1 - c), me).wait_recv()
    for cp in first + passed:
        cp.wait_send()
    for cp in mine:
        cp.wait()


def all_gather_hbm(shards, name):
    n = len(shards)

    def body(*refs):
        x_refs, out_refs = refs[:n], refs[n:2 * n]
        send_sems, recv_sems, local_sems = refs[2 * n:]
        slots = [lambda px, py, pc, o=o: o.at[:, 4 * px + 2 * py + pc] for o in out_refs]
        _two_level_gather(x_refs, slots, send_sems, recv_sems, local_sems)

    hbm = pl.BlockSpec(memory_space=pl.ANY)
    return pl.pallas_call(
        body, name=name,
        out_shape=[SDS((s.shape[0], N_DEV) + s.shape[1:], s.dtype) for s in shards],
        in_specs=[hbm] * n, out_specs=[hbm] * n,
        scratch_shapes=[pltpu.SemaphoreType.DMA((7 * n,)), pltpu.SemaphoreType.DMA((7 * n,)), pltpu.SemaphoreType.DMA((n,))],
    )(*shards)


def all_gather_vmem(block, name):
    r, c_ = block.shape

    def body(x_ref, out_ref, send_sems, recv_sems, local_sems):
        _two_level_gather([x_ref], [lambda px, py, pc: out_ref.at[4 * px + 2 * py + pc]], send_sems, recv_sems, local_sems)

    return pl.pallas_call(
        body, name=name, out_shape=SDS((N_DEV, r, c_), block.dtype),
        in_specs=[pl.BlockSpec(memory_space=pltpu.VMEM)], out_specs=pl.BlockSpec(memory_space=pltpu.VMEM),
        scratch_shapes=[pltpu.SemaphoreType.DMA((7,)), pltpu.SemaphoreType.DMA((7,)), pltpu.SemaphoreType.DMA((1,))],
        compiler_params=_params(),
    )(block)


def _block_tiles(cols):
    base = [(cols * i) // LANES for i in range(N_DEV)]
    ends = [-((-cols * (i + 1)) // LANES) for i in range(N_DEV)]
    return base, ends, max(e - b for b, e in zip(base, ends))


def _my_lane_offset(cols):
    me = 4 * lax.axis_index("x") + 2 * lax.axis_index("y") + lax.axis_index("c")
    return lax.rem(cols * me, LANES)


def shift_cast(w, tr):
    nl, r, cols = w.shape
    width = _block_tiles(cols)[2] * LANES

    def body(x_ref, o_ref, pad):
        pad[:, width - LANES:] = jnp.zeros((tr, LANES), F32)
        pad[:, :cols] = x_ref[...]
        o_ref[...] = pltpu.roll(pad[...], _my_lane_offset(cols), 1).astype(BF16)

    assert width - LANES <= cols
    return pl.pallas_call(
        body, name="shift_cast", grid=(nl, r // tr),
        in_specs=[pl.BlockSpec((pl.Squeezed(), tr, cols), lambda l, i: (l, i, 0))],
        out_specs=pl.BlockSpec((pl.Squeezed(), tr, width), lambda l, i: (l, i, 0)),
        out_shape=SDS((nl, r, width), BF16), scratch_shapes=[pltpu.VMEM((tr, width), F32)],
        compiler_params=_params(("arbitrary", "arbitrary")))(w)


def assemble_w_in(blocks, cols, n_tail, tr):
    _, r, width = blocks.shape
    base, ends, _ = _block_tiles(cols)
    total = ends[-1]
    main_tiles = (N_DEV * cols - n_tail) // LANES
    assert main_tiles == total - 1 and (N_DEV * cols - n_tail) % LANES == 0

    def body(b_ref, main_ref, tail_ref):
        for tile in range(total):
            parts = [b_ref[i, :, (tile - base[i]) * LANES:(tile - base[i] + 1) * LANES]
                     for i in range(N_DEV) if base[i] <= tile < ends[i]]
            val = parts[0] if len(parts) == 1 else parts[0] + parts[1]
            if tile < main_tiles:
                main_ref[:, tile * LANES:(tile + 1) * LANES] = val
            else:
                tail_ref[...] = val

    return pl.pallas_call(
        body, name="assemble_w_in", grid=(r // tr,),
        in_specs=[pl.BlockSpec((N_DEV, tr, width), lambda i: (0, i, 0))],
        out_specs=[pl.BlockSpec((tr, main_tiles * LANES), lambda i: (i, 0)), pl.BlockSpec((tr, LANES), lambda i: (i, 0))],
        out_shape=[SDS((r, main_tiles * LANES), blocks.dtype), SDS((r, LANES), blocks.dtype)],
        compiler_params=_params(("arbitrary",)),
    )(blocks)


def grad_blocks(dw_main, dw_tail, cols, tr):
    r = dw_main.shape[0]
    base, _, tpb = _block_tiles(cols)
    width = tpb * LANES

    def body(m_ref, t_ref, own_ref, send_ref):
        cat = jnp.concatenate([m_ref[...], t_ref[...]], axis=1)
        south = lax.axis_index("c") == 0
        for k in range(N_DEV // 2):
            a = cat[:, base[2 * k] * LANES:base[2 * k] * LANES + width]
            b = cat[:, base[2 * k + 1] * LANES:base[2 * k + 1] * LANES + width]
            own_ref[k] = jnp.where(south, a, b)
            send_ref[k] = jnp.where(south, b, a).astype(BF16)

    return pl.pallas_call(
        body, name="grad_blocks", grid=(r // tr,),
        in_specs=[pl.BlockSpec((tr, dw_main.shape[1]), lambda i: (i, 0)), pl.BlockSpec((tr, LANES), lambda i: (i, 0))],
        out_specs=[pl.BlockSpec((N_DEV // 2, tr, width), lambda i: (0, i, 0))] * 2,
        out_shape=[SDS((N_DEV // 2, r, width), F32), SDS((N_DEV // 2, r, width), BF16)],
        compiler_params=_params(("arbitrary",)),
    )(dw_main, dw_tail)


def _adamw(w, g, m, v):
    m = ADAM_B1 * m + (1.0 - ADAM_B1) * g
    v = ADAM_B2 * v + (1.0 - ADAM_B2) * jnp.square(g)
    m_hat = m / (1.0 - ADAM_B1 ** ADAM_STEP)
    v_hat = v / (1.0 - ADAM_B2 ** ADAM_STEP)
    delta = -ADAM_LR * (m_hat / (jnp.sqrt(v_hat) + ADAM_EPS) + ADAM_WD * w)
    return delta, m, v


def pair_sum(own, got, tr, name):
    k, r, c_ = own.shape

    def body(a_ref, b_ref, o_ref):
        o_ref[...] = (a_ref[...] + b_ref[...].astype(F32)).astype(BF16)

    blk = pl.BlockSpec((pl.Squeezed(), tr, c_), lambda kk, i: (kk, i, 0))
    return pl.pallas_call(
        body, name=name, grid=(k, r // tr), in_specs=[blk, blk], out_specs=blk, out_shape=SDS(own.shape, BF16),
        compiler_params=_params(("arbitrary", "arbitrary")),
    )(own, got)


def reduce_adam(own, got_sibling, got_chips, w, m, v, prev, layer, tr, name, shifted=False):
    nl, r, cols = w.shape
    c_ = own.shape[-1]
    n_scratch = 1 if shifted else 0

    def body(own_ref, sib_ref, c0_ref, c1_ref, c2_ref, w_ref, m_ref, v_ref, *rest):
        g_ref, d_ref, nm_ref, nv_ref = rest[len(rest) - n_scratch - 4:len(rest) - n_scratch]
        g = (own_ref[...] + sib_ref[...].astype(F32) + c0_ref[...].astype(F32) + c1_ref[...].astype(F32)
             + c2_ref[...].astype(F32))
        if shifted:
            rest[-1][...] = pltpu.roll(g, c_ - _my_lane_offset(cols), 1)
            g = rest[-1][:, :cols]
        delta, nm, nv = _adamw(w_ref[...], g, m_ref[...], v_ref[...])
        g_ref[...] = g
        d_ref[...] = delta
        nm_ref[...] = nm
        nv_ref[...] = nv

    row = pl.BlockSpec((tr, c_), lambda i: (i, 0))
    lay = pl.BlockSpec((pl.Squeezed(), tr, cols), lambda i: (layer, i, 0))
    chips = [pl.BlockSpec((pl.Squeezed(), tr, c_), lambda i, s=s: (s, i, 0)) for s in range(3)]
    in_specs = [row, row] + chips + [lay, lay, lay]
    args = [own, got_sibling, got_chips, got_chips, got_chips, w, m, v]
    aliases = {}
    if prev is not None:
        in_specs += [pl.BlockSpec(memory_space=pl.ANY)] * 4
        aliases = {len(args) + k: k for k in range(4)}
        args += list(prev)
    return pl.pallas_call(
        body, name=name, grid=(r // tr,), in_specs=in_specs, out_specs=[lay] * 4,
        out_shape=[SDS((nl, r, cols), F32)] * 4, input_output_aliases=aliases,
        scratch_shapes=[pltpu.VMEM((tr, c_), F32)] * n_scratch,
        compiler_params=_params(("arbitrary",)),
    )(*args)


def sum_devices(packs):
    n, r, c_ = packs.shape

    def body(p_ref, o_ref):
        acc = p_ref[0]
        for k in range(1, n):
            acc = acc + p_ref[k]
        o_ref[...] = acc

    return pl.pallas_call(body, name="sum_devices", out_shape=SDS((r, c_), F32), compiler_params=_params())(packs)


def adam_small(w, g, m, v):
    def body(w_ref, g_ref, m_ref, v_ref, d_ref, nm_ref, nv_ref):
        delta, nm, nv = _adamw(w_ref[...], g_ref[...], m_ref[...], v_ref[...])
        d_ref[...] = delta
        nm_ref[...] = nm
        nv_ref[...] = nv

    return pl.pallas_call(body, name="adam_small", out_shape=[SDS(w.shape, F32)] * 3, compiler_params=_params())(w, g, m, v)


SMALL = ("pre_norm_w", "pool_scale", "conv_b", "dt_bias", "a_log", "d_skip", "_pad", "ssd_norm_w", "post_norm_w", "conv_w")


def _pack(parts):
    flat = jnp.concatenate([parts[k] for k in SMALL], axis=1).reshape(-1, LANES)
    return jnp.pad(flat, ((0, (-flat.shape[0]) % 8), (0, 0)))


def _unpack(pack, sizes, nl):
    total = sum(sizes[k] for k in SMALL)
    flat = pack[: nl * total // LANES].reshape(nl, total)
    out, o = {}, 0
    for k in SMALL:
        out[k] = flat[:, o:o + sizes[k]]
        o += sizes[k]
    return out


def kernel(x, pre_norm_w, w_in, pool_mix_w, pool_scale, conv_w, conv_b, dt_bias, a_log, d_skip, ssd_norm_w, w_out, post_norm_w, loss_target, m_pre_norm_w, m_w_in, m_pool_mix_w, m_pool_scale, m_conv_w, m_conv_b, m_dt_bias, m_a_log, m_d_skip, m_ssd_norm_w, m_w_out, m_post_norm_w, v_pre_norm_w, v_w_in, v_pool_mix_w, v_pool_scale, v_conv_w, v_conv_b, v_dt_bias, v_a_log, v_d_skip, v_ssd_norm_w, v_w_out, v_post_norm_w):
    cx, cy, cc = lax.axis_index("x"), lax.axis_index("y"), lax.axis_index("c")
    me = 4 * cx + 2 * cy + cc
    mychip = 2 * cx + cy
    nl, d, cols = w_in.shape
    t = x.shape[1]
    n_heads = a_log.shape[1]
    sw = n_heads * SSD_HEAD_DIM
    pw = pool_scale.shape[1]
    cd = conv_b.shape[1]
    ng, gsh, gw = pool_mix_w.shape[1:]
    e_main = N_DEV * cols - n_heads
    assert x.shape[0] == 1 and pw == sw and cd == sw + 2 * SSD_GROUPS * SSD_STATE and e_main == 2 * pw + sw + cd
    assert 2 * pw + sw == CONV_BLOCK * cd and n_heads <= LANES and t % SSD_CHUNK == 0 and gsh * N_DEV == gw
    tm, _ = _tiles(t)

    shards_a, shards_b = [shift_cast(w_in, tm)], [w_out.astype(BF16), pool_mix_w.astype(BF16), conv_w]
    gathered = [a[0] for a in all_gather_hbm([s[:1] for s in shards_a + shards_b], "gather_weights")]
    pad_h = ((0, 0), (0, LANES - n_heads))

    def layer_params(l, g_in, g_out, g_mix, g_conv):
        w_main, w_dt = assemble_w_in(g_in, cols, n_heads, tm)
        return dict(pre_w=pre_norm_w[l:l + 1], w_main=w_main, w_dt=w_dt,
                    mixw=g_mix.transpose(1, 0, 2, 3).reshape(ng, gw, gw), pscale=pool_scale[l:l + 1],
                    conv_w=g_conv.transpose(1, 0, 2).reshape(CONV_WIDTH, cd), conv_b=conv_b[l:l + 1],
                    dt_bias=jnp.pad(dt_bias[l:l + 1], pad_h), a_log=jnp.pad(a_log[l:l + 1], pad_h),
                    d_full=jnp.repeat(d_skip[l:l + 1], SSD_HEAD_DIM, axis=1), norm_w=ssd_norm_w[l:l + 1],
                    w_out=g_out.reshape(N_DEV * w_out.shape[1], d), post_w=post_norm_w[l:l + 1])

    xs = x[0]
    saved, params = [], []
    for l in range(nl):
        params.append(layer_params(l, *gathered))
        xs, s, gathered = layer_fwd(xs, params[l], (shards_a, shards_b, l + 1) if l + 1 < nl else None)
        saved.append(s)
    loss_part, g = loss_grad(xs, loss_target[0], tm)
    loss = lax.psum(loss_part[0, 0], ("x", "y", "c"))

    big = {"w_in": (w_in, m_w_in, v_w_in), "w_out": (w_out, m_w_out, v_w_out),
           "pool_mix_w": tuple(a.reshape(nl, ng * gsh, gw) for a in (pool_mix_w, m_pool_mix_w, v_pool_mix_w))}
    names = list(big)
    big_out = {k: None for k in big}
    small_g = [None] * nl

    def apply(layer, own, got_sib, got_chips):
        for k, o, gs_, gc in zip(names, own, got_sib, got_chips):
            wk, mk, vk = big[k]
            big_out[k] = reduce_adam(lax.dynamic_index_in_dim(o, mychip, 0, keepdims=False),
                                     lax.dynamic_index_in_dim(gs_, mychip, 0, keepdims=False),
                                     gc, wk, mk, vk, big_out[k], layer, min(256, wk.shape[1]), "reduce_adam_" + k,
                                     shifted=(k == "w_in"))

    pending = None
    for l in reversed(range(nl)):
        g, gr, got = layer_bwd(g, saved[l], params[l], pending)
        if pending is not None:
            apply(l + 1, pending[0], *got)
        own_in, send_in = grad_blocks(gr["w_main"], gr["w_dt"], cols, min(128, d))
        halves = [lambda ci: lax.dynamic_index_in_dim(gr["w_out"].reshape(4, 2, -1, d), ci, 1, keepdims=False),
                  lambda ci: lax.dynamic_index_in_dim(
                      gr["mixw"].reshape(ng, 4, 2, gsh, gw), ci, 2, keepdims=False).transpose(1, 0, 2, 3).reshape(4, ng * gsh, gw)]
        pending = ([own_in] + [h(cc) for h in halves], [send_in] + [h(1 - cc).astype(BF16) for h in halves])
        small_g[l] = dict(pre_norm_w=gr["pre_w"], pool_scale=gr["pscale"], conv_b=gr["conv_b"], dt_bias=gr["dt_bias"][:, :n_heads],
                          a_log=gr["a_log"][:, :n_heads], d_skip=gr["d_skip"][:, :n_heads], _pad=jnp.zeros((1, LANES - 3 * n_heads), F32),
                          ssd_norm_w=gr["norm_w"], post_norm_w=gr["post_w"], conv_w=gr["conv_w"].reshape(1, CONV_WIDTH * cd))
    got_sib = comm_only(sibling_swap(pending[1]), "grads_to_sibling")
    chip_sums = [pair_sum(o, r, min(256, o.shape[1]), "pair_sum") for o, r in zip(pending[0], got_sib)]
    apply(0, pending[0], got_sib, comm_only(chips_scatter(chip_sums), "grads_to_chips"))

    sizes = {k: small_g[0][k].shape[1] for k in SMALL}
    gsum = sum_devices(all_gather_vmem(_pack({k: jnp.concatenate([sg[k] for sg in small_g], axis=0) for k in SMALL}),
                                       "gather_small_grads"))
    gs = _unpack(gsum, sizes, nl)
    csh = conv_w.shape[2]
    gs["conv_w"] = lax.dynamic_slice_in_dim(gs["conv_w"].reshape(nl, CONV_WIDTH, cd), me * csh, csh, axis=2).reshape(nl, -1)
    lsizes = dict(sizes, conv_w=CONV_WIDTH * csh)
    zpad = jnp.zeros((nl, sizes["_pad"]), F32)

    def local(pre, scale, cb, dtb, al, dsk, nw, post, cw):
        return _pack(dict(pre_norm_w=pre, pool_scale=scale, conv_b=cb, dt_bias=dtb, a_log=al, d_skip=dsk, _pad=zpad,
                          ssd_norm_w=nw, post_norm_w=post, conv_w=cw.reshape(nl, -1)))

    wp = local(pre_norm_w, pool_scale, conv_b, dt_bias, a_log, d_skip, ssd_norm_w, post_norm_w, conv_w)
    mp = local(m_pre_norm_w, m_pool_scale, m_conv_b, m_dt_bias, m_a_log, m_d_skip, m_ssd_norm_w, m_post_norm_w, m_conv_w)
    vp = local(v_pre_norm_w, v_pool_scale, v_conv_b, v_dt_bias, v_a_log, v_d_skip, v_ssd_norm_w, v_post_norm_w, v_conv_w)
    small_out = [gs] + [_unpack(o, lsizes, nl) for o in adam_small(wp, _pack(gs), mp, vp)]

    def leaf(kind, name):
        if name in big:
            return big_out[name][kind].reshape(big[name][0].shape if name != "pool_mix_w" else pool_mix_w.shape)
        val = small_out[kind][name]
        return val.reshape(conv_w.shape) if name == "conv_w" else val

    order = ("pre_norm_w", "w_in", "pool_mix_w", "pool_scale", "conv_w", "conv_b", "dt_bias", "a_log", "d_skip",
             "ssd_norm_w", "w_out", "post_norm_w")
    return (loss, g[None]) + tuple(leaf(kind, name) for kind in range(4) for name in order)
```

```python
import jax
import jax.numpy as jnp
from jax import lax
from jax.experimental import pallas as pl
from jax.experimental.pallas import tpu as pltpu

F32 = jnp.float32
BF16 = jnp.bfloat16
SDS = jax.ShapeDtypeStruct
MESH = pl.DeviceIdType.MESH
HIGHEST = lax.Precision.HIGHEST

NORM_EPS = 1e-6
POOL_WINDOWS = (2, 4, 8, 16)
POOL_HALO = 16
CONV_WIDTH = 4
CONV_HALO = 8
SSD_CHUNK = 128
SSD_HEAD_DIM = 64
SSD_STATE = 128
SSD_GROUPS = 4
LANES = 128
N_DEV = 8

ADAM_LR = 0.001
ADAM_B1 = 0.9
ADAM_B2 = 0.999
ADAM_EPS = 1e-08
ADAM_WD = 0.01
ADAM_STEP = 10

VMEM_LIMIT = 56 * 1024 * 1024

NT = (((1,), (1,)), ((), ()))
TN = (((0,), (0,)), ((), ()))


def _params(sem=None):
    kw = dict(vmem_limit_bytes=VMEM_LIMIT)
    if sem is not None:
        kw["dimension_semantics"] = sem
    return pltpu.CompilerParams(**kw)


def _silu(v):
    return v * jax.nn.sigmoid(v)


def _dsilu(v):
    s = jax.nn.sigmoid(v)
    return s * (1.0 + v * (1.0 - s))


def _split_dot(v, sel):
    hi = v.astype(BF16)
    lo = (v - hi.astype(F32)).astype(BF16)
    return (jnp.dot(hi, sel, preferred_element_type=F32) + jnp.dot(lo, sel, preferred_element_type=F32))


def _head_selector(width, per):
    ch = lax.broadcasted_iota(jnp.int32, (width, LANES), 0)
    hd = lax.broadcasted_iota(jnp.int32, (width, LANES), 1)
    return jnp.where((ch >= hd * per) & (ch < (hd + 1) * per), 1.0, 0.0).astype(BF16)


class Comm:
    def __init__(self, inputs, out_shapes, aliases, n_sems, make):
        self.inputs, self.out_shapes, self.aliases, self.n_sems, self.make = list(inputs), list(out_shapes), dict(aliases), n_sems, make


def _remote(src, dst, send_sems, recv_sems, k, peer):
    return pltpu.make_async_remote_copy(src_ref=src, dst_ref=dst, send_sem=send_sems.at[k], recv_sem=recv_sems.at[k],
                                        device_id=peer, device_id_type=MESH)


class _SemRange:
    def __init__(self, sems, start):
        self.sems, self.start = sems, start

    @property
    def at(self):
        return self

    def __getitem__(self, k):
        return self.sems.at[self.start + k]


def merge_comms(comms):
    comms = [c for c in comms if c is not None]
    if len(comms) <= 1:
        return comms[0] if comms else None
    aliases, i_off, o_off = {}, 0, 0
    for c in comms:
        aliases.update({i_off + k: o_off + v for k, v in c.aliases.items()})
        i_off, o_off = i_off + len(c.inputs), o_off + len(c.out_shapes)

    def make(ins, outs, ss, rs):
        sends, locals_, arrivals, i0, o0, s0 = [], [], [], 0, 0, 0
        for c in comms:
            s, l, a = c.make(ins[i0:i0 + len(c.inputs)], outs[o0:o0 + len(c.out_shapes)], _SemRange(ss, s0), _SemRange(rs, s0))
            sends, locals_, arrivals = sends + s, locals_ + l, arrivals + a
            i0, o0, s0 = i0 + len(c.inputs), o0 + len(c.out_shapes), s0 + c.n_sems
        return sends, locals_, arrivals

    return Comm(sum((c.inputs for c in comms), []), sum((c.out_shapes for c in comms), []), aliases,
                sum(c.n_sems for c in comms), make)


def _call(body, args, *, name, grid, in_specs, out_specs, out_shape, scratch_shapes=(), sem=None, comm=None):
    in_specs, out_specs, out_shape = list(in_specs), list(out_specs), list(out_shape)
    if comm is None:
        outs = pl.pallas_call(body, name=name, grid=grid, in_specs=in_specs, out_specs=out_specs, out_shape=out_shape,
                              scratch_shapes=list(scratch_shapes), compiler_params=_params(sem))(*args)
        return list(outs), []
    ni, no, nci, nco, ns = len(in_specs), len(out_specs), len(comm.inputs), len(comm.out_shapes), len(scratch_shapes)
    hbm = pl.BlockSpec(memory_space=pl.ANY)

    def hosted(*refs):
        ins, cins = refs[:ni], refs[ni:ni + nci]
        outs, couts = refs[ni + nci:ni + nci + no], refs[ni + nci + no:ni + nci + no + nco]
        scratch = refs[ni + nci + no + nco:]
        sends, locals_, arrivals = comm.make(cins, couts, scratch[ns], scratch[ns + 1])
        first = last = None if grid else True
        for axis, extent in enumerate(grid):
            pid = pl.program_id(axis)
            first = (pid == 0) if first is None else first & (pid == 0)
            last = (pid == extent - 1) if last is None else last & (pid == extent - 1)

        @pl.when(first)
        def _():
            for cp in locals_ + sends:
                cp.start()

        body(*ins, *outs, *scratch[:ns])

        @pl.when(last)
        def _():
            for cp in arrivals:
                cp.wait_recv()
            for cp in sends:
                cp.wait_send()
            for cp in locals_:
                cp.wait()

    outs = pl.pallas_call(
        hosted, name=name, grid=grid, in_specs=in_specs + [hbm] * nci, out_specs=out_specs + [hbm] * nco,
        out_shape=out_shape + comm.out_shapes,
        scratch_shapes=list(scratch_shapes) + [pltpu.SemaphoreType.DMA((comm.n_sems,)), pltpu.SemaphoreType.DMA((comm.n_sems,))],
        input_output_aliases={ni + k: no + v for k, v in comm.aliases.items()},
        compiler_params=_params(sem),
    )(*args, *comm.inputs)
    return list(outs[:no]), list(outs[no:])


def rms_fwd(x, w, tm):
    t, d = x.shape

    def body(x_ref, w_ref, h_ref, r_ref):
        xv = x_ref[...]
        r = lax.rsqrt(jnp.mean(xv * xv, axis=-1, keepdims=True) + NORM_EPS)
        h_ref[...] = (xv * r * w_ref[...]).astype(BF16)
        r_ref[...] = r

    return pl.pallas_call(
        body, name="rms_fwd", grid=(t // tm,),
        in_specs=[pl.BlockSpec((tm, d), lambda i: (i, 0)), pl.BlockSpec((1, d), lambda i: (0, 0))],
        out_specs=[pl.BlockSpec((tm, d), lambda i: (i, 0)), pl.BlockSpec((tm, 1), lambda i: (i, 0))],
        out_shape=[SDS((t, d), BF16), SDS((t, 1), F32)],
        compiler_params=_params(("arbitrary",)),
    )(x, w)


def post_fwd(out, x, w, tm):
    t, d = x.shape

    def body(o_ref, x_ref, w_ref, y_ref, r_ref):
        ov = o_ref[...]
        r = lax.rsqrt(jnp.mean(ov * ov, axis=-1, keepdims=True) + NORM_EPS)
        y_ref[...] = x_ref[...] + ov * r * w_ref[...]
        r_ref[...] = r

    return pl.pallas_call(
        body, name="post_fwd", grid=(t // tm,),
        in_specs=[pl.BlockSpec((tm, d), lambda i: (i, 0)), pl.BlockSpec((tm, d), lambda i: (i, 0)),
                  pl.BlockSpec((1, d), lambda i: (0, 0))],
        out_specs=[pl.BlockSpec((tm, d), lambda i: (i, 0)), pl.BlockSpec((tm, 1), lambda i: (i, 0))],
        out_shape=[SDS((t, d), F32), SDS((t, 1), F32)],
        compiler_params=_params(("arbitrary",)),
    )(out, x, w)


def _norm_bwd(g_n, n, r):
    return r * (g_n - n * jnp.mean(g_n * n, axis=-1, keepdims=True))


def post_bwd(g, out, r, w, tm):
    t, d = g.shape

    def body(g_ref, o_ref, r_ref, w_ref, do_ref, dw_ref):
        i = pl.program_id(0)
        gv = g_ref[...]
        rv = r_ref[...]
        n = o_ref[...] * rv
        part = jnp.sum(gv * n, axis=0, keepdims=True)

        @pl.when(i == 0)
        def _():
            dw_ref[...] = part

        @pl.when(i > 0)
        def _():
            dw_ref[...] += part

        do_ref[...] = _norm_bwd(gv * w_ref[...], n, rv).astype(BF16)

    return pl.pallas_call(
        body, name="post_bwd", grid=(t // tm,),
        in_specs=[pl.BlockSpec((tm, d), lambda i: (i, 0)), pl.BlockSpec((tm, d), lambda i: (i, 0)),
                  pl.BlockSpec((tm, 1), lambda i: (i, 0)), pl.BlockSpec((1, d), lambda i: (0, 0))],
        out_specs=[pl.BlockSpec((tm, d), lambda i: (i, 0)), pl.BlockSpec((1, d), lambda i: (0, 0))],
        out_shape=[SDS((t, d), BF16), SDS((1, d), F32)],
        compiler_params=_params(("arbitrary",)),
    )(g, out, r, w)


def rms_bwd(dh_a, dh_b, x, r, w, g, tm):
    t, d = x.shape

    def body(a_ref, b_ref, x_ref, r_ref, w_ref, g_ref, gx_ref, dw_ref):
        i = pl.program_id(0)
        dh = a_ref[...] + b_ref[...]
        rv = r_ref[...]
        n = x_ref[...] * rv
        part = jnp.sum(dh * n, axis=0, keepdims=True)

        @pl.when(i == 0)
        def _():
            dw_ref[...] = part

        @pl.when(i > 0)
        def _():
            dw_ref[...] += part

        gx_ref[...] = g_ref[...] + _norm_bwd(dh * w_ref[...], n, rv)

    row = pl.BlockSpec((tm, d), lambda i: (i, 0))
    return pl.pallas_call(
        body, name="rms_bwd", grid=(t // tm,),
        in_specs=[row, row, row, pl.BlockSpec((tm, 1), lambda i: (i, 0)), pl.BlockSpec((1, d), lambda i: (0, 0)), row],
        out_specs=[row, pl.BlockSpec((1, d), lambda i: (0, 0))],
        out_shape=[SDS((t, d), F32), SDS((1, d), F32)],
        compiler_params=_params(("arbitrary",)),
    )(dh_a, dh_b, x, r, w, g)


def loss_grad(y, target, tm):
    t, d = y.shape

    def body(y_ref, t_ref, l_ref, g_ref):
        i = pl.program_id(0)
        err = y_ref[...] - t_ref[...]
        g_ref[...] = err / d
        part = 0.5 * jnp.sum(jnp.mean(err * err, axis=-1, keepdims=True), axis=0, keepdims=True)

        @pl.when(i == 0)
        def _():
            l_ref[...] = part

        @pl.when(i > 0)
        def _():
            l_ref[...] += part

    row = pl.BlockSpec((tm, d), lambda i: (i, 0))
    return pl.pallas_call(
        body, name="loss_grad", grid=(t // tm,), in_specs=[row, row],
        out_specs=[pl.BlockSpec((1, 1), lambda i: (0, 0)), row],
        out_shape=[SDS((1, 1), F32), SDS((t, d), F32)],
        compiler_params=_params(("arbitrary",)),
    )(y, target)


def mm_nn(a, b, out_dtype, tm, tn, name, comm=None):
    m, k = a.shape
    n = b.shape[1]

    def body(a_ref, b_ref, o_ref):
        o_ref[...] = jnp.dot(a_ref[...], b_ref[...], preferred_element_type=F32).astype(out_dtype)

    outs, couts = _call(
        body, (a, b), name=name, grid=(n // tn, m // tm),
        in_specs=[pl.BlockSpec((tm, k), lambda j, i: (i, 0)), pl.BlockSpec((k, tn), lambda j, i: (0, j))],
        out_specs=[pl.BlockSpec((tm, tn), lambda j, i: (i, j))],
        out_shape=[SDS((m, n), out_dtype)], sem=("arbitrary", "arbitrary"), comm=comm)
    return outs[0], couts


def mm_nt(a, b, out_dtype, tm, tn, tk, name, comm=None):
    m, k = a.shape
    n = b.shape[0]
    nk = k // tk

    def body(a_ref, b_ref, o_ref, acc_ref):
        kk = pl.program_id(2)
        part = lax.dot_general(a_ref[...], b_ref[...], NT, preferred_element_type=F32)
        if nk == 1:
            o_ref[...] = part.astype(out_dtype)
        else:
            @pl.when(kk == 0)
            def _():
                acc_ref[...] = part

            @pl.when(kk > 0)
            def _():
                acc_ref[...] += part

            @pl.when(kk == nk - 1)
            def _():
                o_ref[...] = acc_ref[...].astype(out_dtype)

    outs, couts = _call(
        body, (a, b), name=name, grid=(m // tm, n // tn, nk),
        in_specs=[pl.BlockSpec((tm, tk), lambda i, j, kk: (i, kk)), pl.BlockSpec((tn, tk), lambda i, j, kk: (j, kk))],
        out_specs=[pl.BlockSpec((tm, tn), lambda i, j, kk: (i, j))],
        out_shape=[SDS((m, n), out_dtype)],
        scratch_shapes=[pltpu.VMEM((tm, tn) if nk > 1 else (8, LANES), F32)],
        sem=("arbitrary", "arbitrary", "arbitrary"), comm=comm)
    return outs[0], couts


def mm_tn(a, b, tm, tn, name, comm=None):
    t, m = a.shape
    n = b.shape[1]

    def body(a_ref, b_ref, o_ref):
        o_ref[...] = lax.dot_general(a_ref[...], b_ref[...], TN, preferred_element_type=F32)

    outs, couts = _call(
        body, (a, b), name=name, grid=(m // tm, n // tn),
        in_specs=[pl.BlockSpec((t, tm), lambda i, j: (0, i)), pl.BlockSpec((t, tn), lambda i, j: (0, j))],
        out_specs=[pl.BlockSpec((tm, tn), lambda i, j: (i, j))],
        out_shape=[SDS((m, n), F32)], sem=("arbitrary", "arbitrary"), comm=comm)
    return outs[0], couts


def _window_sums(ext, n_rows, lookahead):
    def sh(v, k):
        return pltpu.roll(v, (n_rows - k) if lookahead else k, 0)
    s2 = ext + sh(ext, 1)
    s4 = s2 + sh(s2, 2)
    s8 = s4 + sh(s4, 4)
    s16 = s8 + sh(s8, 8)
    return (s2, s4, s8, s16)


def _pool_counts(i, tm, w):
    tpos = i * tm + lax.broadcasted_iota(jnp.int32, (tm, 1), 0)
    return jnp.minimum(tpos + 1, w).astype(F32)


def _pooled(uc_ref, up_ref, i, tm):
    cur = uc_ref[...]
    prev = jnp.where(i > 0, up_ref[...], 0.0)
    ext = jnp.concatenate([prev, cur], axis=0)
    return cur, _window_sums(ext, tm + POOL_HALO, False)


def pool_fwd(proj, mixw, scale, tm):
    t = proj.shape[0]
    pw = scale.shape[1]
    gw = pw // len(POOL_WINDOWS)
    nh = tm // POOL_HALO

    def body(uc_ref, up_ref, g_ref, w_ref, s_ref, o_ref):
        i = pl.program_id(0)
        cur, sums = _pooled(uc_ref, up_ref, i, tm)
        for g, w in enumerate(POOL_WINDOWS):
            cols = slice(g * gw, (g + 1) * gw)
            pooled = sums[g][POOL_HALO:, cols] / _pool_counts(i, tm, w) - cur[:, cols]
            mixed = jnp.dot(pooled.astype(BF16), w_ref[g], preferred_element_type=F32)
            o_ref[:, cols] = (mixed * s_ref[:, cols] * _silu(g_ref[:, cols])).astype(BF16)

    return pl.pallas_call(
        body, name="pool_fwd", grid=(t // tm,),
        in_specs=[pl.BlockSpec((tm, pw), lambda i: (i, 0)),
                  pl.BlockSpec((POOL_HALO, pw), lambda i: (jnp.maximum(i * nh - 1, 0), 0)),
                  pl.BlockSpec((tm, pw), lambda i: (i, 1)),
                  pl.BlockSpec(mixw.shape, lambda i: (0, 0, 0)),
                  pl.BlockSpec((1, pw), lambda i: (0, 0))],
        out_specs=pl.BlockSpec((tm, pw), lambda i: (i, 0)),
        out_shape=SDS((t, 2 * pw), BF16),
        compiler_params=_params(("arbitrary",)),
    )(proj, proj, proj, mixw, scale)


def pool_bwd_a(dmixed, proj, mixw, scale, tm):
    t, e = proj.shape
    pw = scale.shape[1]
    ng = len(POOL_WINDOWS)
    gw = pw // ng
    nh = tm // POOL_HALO

    def body(dy_ref, uc_ref, up_ref, g_ref, w_ref, s_ref, dg_ref, dq_ref, ds_ref, dw_ref):
        i = pl.program_id(0)

        @pl.when(i == 0)
        def _():
            ds_ref[...] = jnp.zeros_like(ds_ref)
            dw_ref[...] = jnp.zeros_like(dw_ref)

        cur, sums = _pooled(uc_ref, up_ref, i, tm)
        for g, w in enumerate(POOL_WINDOWS):
            cols = slice(g * gw, (g + 1) * gw)
            cnt = _pool_counts(i, tm, w)
            pooled = (sums[g][POOL_HALO:, cols] / cnt - cur[:, cols]).astype(BF16)
            mixed = jnp.dot(pooled, w_ref[g], preferred_element_type=F32)
            gate = g_ref[:, cols]
            dy = dy_ref[:, cols]
            sc = s_ref[:, cols]
            dg_ref[:, cols] = (dy * mixed * sc * _dsilu(gate)).astype(BF16)
            ds = dy * _silu(gate)
            ds_ref[:, cols] += jnp.sum(ds * mixed, axis=0, keepdims=True)
            dmix = (ds * sc).astype(BF16)
            dw_ref[g] += lax.dot_general(pooled, dmix, TN, preferred_element_type=F32)
            dq_ref[:, cols] = lax.dot_general(dmix, w_ref[g], NT, preferred_element_type=F32) / cnt

    return pl.pallas_call(
        body, name="pool_bwd_a", grid=(t // tm,),
        in_specs=[pl.BlockSpec((tm, pw), lambda i: (i, 0)),
                  pl.BlockSpec((tm, pw), lambda i: (i, 0)),
                  pl.BlockSpec((POOL_HALO, pw), lambda i: (jnp.maximum(i * nh - 1, 0), 0)),
                  pl.BlockSpec((tm, pw), lambda i: (i, 1)),
                  pl.BlockSpec(mixw.shape, lambda i: (0, 0, 0)),
                  pl.BlockSpec((1, pw), lambda i: (0, 0))],
        out_specs=[pl.BlockSpec((tm, pw), lambda i: (i, 1)),
                   pl.BlockSpec((tm, pw), lambda i: (i, 0)),
                   pl.BlockSpec((1, pw), lambda i: (0, 0)),
                   pl.BlockSpec((ng, gw, gw), lambda i: (0, 0, 0))],
        out_shape=[SDS((t, e), BF16), SDS((t, pw), F32), SDS((1, pw), F32), SDS((ng, gw, gw), F32)],
        compiler_params=_params(("arbitrary",)),
    )(dmixed, proj, proj, proj, mixw, scale)


def pool_bwd_b(dq, dproj, tm):
    t, pw = dq.shape
    gw = pw // len(POOL_WINDOWS)
    nh = tm // POOL_HALO
    nt = t // tm

    def body(c_ref, n_ref, alias_ref, o_ref):
        i = pl.program_id(0)
        cur = c_ref[...]
        nxt = jnp.where(i < nt - 1, n_ref[...], 0.0)
        sums = _window_sums(jnp.concatenate([cur, nxt], axis=0), tm + POOL_HALO, True)
        for g, w in enumerate(POOL_WINDOWS):
            cols = slice(g * gw, (g + 1) * gw)
            o_ref[:, cols] = (sums[g][:tm, cols] - cur[:, cols] * _pool_counts(i, tm, w)).astype(BF16)

    return pl.pallas_call(
        body, name="pool_bwd_b", grid=(nt,),
        in_specs=[pl.BlockSpec((tm, pw), lambda i: (i, 0)),
                  pl.BlockSpec((POOL_HALO, pw), lambda i: (jnp.minimum((i + 1) * nh, t // POOL_HALO - 1), 0)),
                  pl.BlockSpec(memory_space=pl.ANY)],
        out_specs=pl.BlockSpec((tm, pw), lambda i: (i, 0)),
        out_shape=SDS(dproj.shape, dproj.dtype),
        input_output_aliases={2: 0},
        compiler_params=_params(("arbitrary",)),
    )(dq, dq, dproj)


def _conv_pre(xc_ref, xp_ref, w_ref, b_ref, i, tm):
    cur = xc_ref[...]
    prev = jnp.where(i > 0, xp_ref[...], 0.0)
    ext = jnp.concatenate([prev, cur], axis=0)
    taps = [pltpu.roll(ext, CONV_WIDTH - 1 - k, 0)[CONV_HALO:] for k in range(CONV_WIDTH - 1)] + [cur]
    pre = b_ref[...]
    for k in range(CONV_WIDTH):
        pre = pre + w_ref[k:k + 1, :] * taps[k]
    return pre, taps


def conv_fwd(proj, conv_w, conv_b, col_block, tm):
    t = proj.shape[0]
    cd = conv_b.shape[1]
    nh = tm // CONV_HALO

    def body(xc_ref, xp_ref, w_ref, b_ref, o_ref):
        i = pl.program_id(0)
        pre, _ = _conv_pre(xc_ref, xp_ref, w_ref, b_ref, i, tm)
        o_ref[...] = _silu(pre)

    return pl.pallas_call(
        body, name="conv_fwd", grid=(t // tm,),
        in_specs=[pl.BlockSpec((tm, cd), lambda i: (i, col_block)),
                  pl.BlockSpec((CONV_HALO, cd), lambda i: (jnp.maximum(i * nh - 1, 0), col_block)),
                  pl.BlockSpec((CONV_WIDTH, cd), lambda i: (0, 0)),
                  pl.BlockSpec((1, cd), lambda i: (0, 0))],
        out_specs=pl.BlockSpec((tm, cd), lambda i: (i, 0)),
        out_shape=SDS((t, cd), F32),
        compiler_params=_params(("arbitrary",)),
    )(proj, proj, conv_w, conv_b)


def conv_bwd_a(dxs, db, dc, proj, conv_w, conv_b, col_block, tm, comm=None):
    t = proj.shape[0]
    cd = conv_b.shape[1]
    sw = dxs.shape[1]
    gn = db.shape[1]
    nh = tm // CONV_HALO

    def body(dx_ref, db_ref, dc_ref, xc_ref, xp_ref, w_ref, b_ref, dp_ref, dw_ref, dbias_ref):
        i = pl.program_id(0)

        @pl.when(i == 0)
        def _():
            dw_ref[...] = jnp.zeros_like(dw_ref)
            dbias_ref[...] = jnp.zeros_like(dbias_ref)

        pre, taps = _conv_pre(xc_ref, xp_ref, w_ref, b_ref, i, tm)
        dact = jnp.concatenate([dx_ref[...], db_ref[...], dc_ref[...]], axis=1)
        dpre = dact * _dsilu(pre)
        dp_ref[...] = dpre
        dbias_ref[...] += jnp.sum(dpre, axis=0, keepdims=True)
        for k in range(CONV_WIDTH):
            dw_ref[k:k + 1, :] += jnp.sum(dpre * taps[k], axis=0, keepdims=True)

    return _call(
        body, (dxs, db, dc, proj, proj, conv_w, conv_b), name="conv_bwd_a", grid=(t // tm,),
        in_specs=[pl.BlockSpec((tm, sw), lambda i: (i, 0)), pl.BlockSpec((tm, gn), lambda i: (i, 0)),
                  pl.BlockSpec((tm, gn), lambda i: (i, 0)),
                  pl.BlockSpec((tm, cd), lambda i: (i, col_block)),
                  pl.BlockSpec((CONV_HALO, cd), lambda i: (jnp.maximum(i * nh - 1, 0), col_block)),
                  pl.BlockSpec((CONV_WIDTH, cd), lambda i: (0, 0)),
                  pl.BlockSpec((1, cd), lambda i: (0, 0))],
        out_specs=[pl.BlockSpec((tm, cd), lambda i: (i, 0)),
                   pl.BlockSpec((CONV_WIDTH, cd), lambda i: (0, 0)),
                   pl.BlockSpec((1, cd), lambda i: (0, 0))],
        out_shape=[SDS((t, cd), F32), SDS((CONV_WIDTH, cd), F32), SDS((1, cd), F32)],
        sem=("arbitrary",), comm=comm)


def conv_bwd_b(dpre, conv_w, dproj, col_block, tm):
    t, cd = dpre.shape
    nh = tm // CONV_HALO
    nt = t // tm

    def body(c_ref, n_ref, w_ref, alias_ref, o_ref):
        i = pl.program_id(0)
        cur = c_ref[...]
        nxt = jnp.where(i < nt - 1, n_ref[...], 0.0)
        ext = jnp.concatenate([cur, nxt], axis=0)
        n = tm + CONV_HALO
        acc = w_ref[CONV_WIDTH - 1:CONV_WIDTH, :] * cur
        for k in range(CONV_WIDTH - 1):
            acc = acc + w_ref[k:k + 1, :] * pltpu.roll(ext, n - (CONV_WIDTH - 1 - k), 0)[:tm]
        o_ref[...] = acc.astype(BF16)

    return pl.pallas_call(
        body, name="conv_bwd_b", grid=(nt,),
        in_specs=[pl.BlockSpec((tm, cd), lambda i: (i, 0)),
                  pl.BlockSpec((CONV_HALO, cd), lambda i: (jnp.minimum((i + 1) * nh, t // CONV_HALO - 1), 0)),
                  pl.BlockSpec((CONV_WIDTH, cd), lambda i: (0, 0)),
                  pl.BlockSpec(memory_space=pl.ANY)],
        out_specs=pl.BlockSpec((tm, cd), lambda i: (i, col_block)),
        out_shape=SDS(dproj.shape, dproj.dtype),
        input_output_aliases={3: 0},
        compiler_params=_params(("arbitrary",)),
    )(dpre, dpre, conv_w, dproj)


def _softplus(v):
    return jnp.maximum(v, 0.0) + jnp.log(1.0 + jnp.exp(-jnp.abs(v)))


def _ssd_chunk_terms(dtr_ref, bias_ref, a_ref, n_heads):
    q = SSD_CHUNK
    lane = lax.broadcasted_iota(jnp.int32, (1, LANES), 1)
    pre = dtr_ref[...] + bias_ref[...]
    dt = jnp.where(lane < n_heads, _softplus(pre), 0.0)
    a = jnp.where(lane < n_heads, -jnp.exp(a_ref[...]), 0.0)
    row = lax.broadcasted_iota(jnp.int32, (q, q), 0)
    col = lax.broadcasted_iota(jnp.int32, (q, q), 1)
    causal = row >= col
    acs = jnp.dot(causal.astype(F32), dt * a, precision=HIGHEST, preferred_element_type=F32)
    last = acs[q - 1:q, :]
    return dict(pre=pre, dt=dt, a=a, acs=acs, acs_t=acs.T, eacs=jnp.exp(acs), dstate=jnp.exp(last - acs),
                cdec=jnp.exp(last), causal=causal, diag=row == col, lane=lane)


def _pair_cols(lo, v, h):
    return jnp.where(lo, v[:, h:h + 1], v[:, h + 1:h + 2])


def _pair_decay(tm_, cb, h):
    l0 = jnp.exp(jnp.where(tm_["causal"], tm_["acs"][:, h:h + 1] - tm_["acs_t"][h:h + 1, :], -jnp.inf))
    l1 = jnp.exp(jnp.where(tm_["causal"], tm_["acs"][:, h + 1:h + 2] - tm_["acs_t"][h + 1:h + 2, :], -jnp.inf))
    return l0, l1, jnp.concatenate([cb * l0, cb * l1], axis=1)


def _pair_decay_t(tm_, cbt, h):
    upper = jnp.logical_not(tm_["causal"]) | tm_["diag"]
    t0 = jnp.exp(jnp.where(upper, tm_["acs_t"][h:h + 1, :] - tm_["acs"][:, h:h + 1], -jnp.inf))
    t1 = jnp.exp(jnp.where(upper, tm_["acs_t"][h + 1:h + 2, :] - tm_["acs"][:, h + 1:h + 2], -jnp.inf))
    return jnp.concatenate([cbt * t0, cbt * t1], axis=0).astype(BF16)


def _block_diag(lo, xdt):
    return jnp.concatenate([jnp.where(lo, xdt, 0.0), jnp.where(lo, 0.0, xdt)], axis=0).astype(BF16)


def ssd_fwd(xbc, dt_raw, dt_bias, a_log, n_heads, comm=None):
    t = xbc.shape[0]
    q = SSD_CHUNK
    gn = SSD_GROUPS * SSD_STATE
    sw = n_heads * SSD_HEAD_DIM
    n_pairs = n_heads // 2
    pairs_per_group = n_pairs // SSD_GROUPS
    nc = t // q
    bblk = sw // gn

    def body(xs_ref, b_ref, c_ref, dtr_ref, bias_ref, a_ref, y_ref, sin_ref, state):
        @pl.when(pl.program_id(0) == 0)
        def _():
            state[...] = jnp.zeros_like(state)

        tm_ = _ssd_chunk_terms(dtr_ref, bias_ref, a_ref, n_heads)
        lo = tm_["lane"] < SSD_HEAD_DIM
        for g in range(SSD_GROUPS):
            gcols = slice(g * SSD_STATE, (g + 1) * SSD_STATE)
            bg = b_ref[:, gcols].astype(BF16)
            bg_t = b_ref[:, gcols].T.astype(BF16)
            cg = c_ref[:, gcols].astype(BF16)
            cb = lax.dot_general(cg, bg, NT, preferred_element_type=F32)
            for j in range(pairs_per_group):
                p = g * pairs_per_group + j
                h = 2 * p
                pcols = slice(p * LANES, (p + 1) * LANES)
                _, _, mcat = _pair_decay(tm_, cb, h)
                xdt = xs_ref[:, pcols] * _pair_cols(lo, tm_["dt"], h)
                ydiag = jnp.dot(mcat.astype(BF16), _block_diag(lo, xdt), preferred_element_type=F32)
                st = state[p]
                sin_ref[0, p] = st
                yoff = jnp.dot(cg, st.astype(BF16), preferred_element_type=F32) * _pair_cols(lo, tm_["eacs"], h)
                y_ref[:, pcols] = ydiag + yoff
                xw = (xdt * _pair_cols(lo, tm_["dstate"], h)).astype(BF16)
                state[p] = st * _pair_cols(lo, tm_["cdec"], h) + jnp.dot(bg_t, xw, preferred_element_type=F32)

    vec = pl.BlockSpec((1, LANES), lambda c: (0, 0))
    return _call(
        body, (xbc, xbc, xbc, dt_raw, dt_bias, a_log), name="ssd_fwd", grid=(nc,),
        in_specs=[pl.BlockSpec((q, sw), lambda c: (c, 0)),
                  pl.BlockSpec((q, gn), lambda c: (c, bblk)),
                  pl.BlockSpec((q, gn), lambda c: (c, bblk + 1)),
                  pl.BlockSpec((q, LANES), lambda c: (c, 0)), vec, vec],
        out_specs=[pl.BlockSpec((q, sw), lambda c: (c, 0)),
                   pl.BlockSpec((1, n_pairs, SSD_STATE, LANES), lambda c: (c, 0, 0, 0))],
        out_shape=[SDS((t, sw), F32), SDS((nc, n_pairs, SSD_STATE, LANES), F32)],
        scratch_shapes=[pltpu.VMEM((n_pairs, SSD_STATE, LANES), F32)],
        sem=("arbitrary",), comm=comm)


def ssd_bwd(dy, xbc, dt_raw, dt_bias, a_log, d_full, s_in, n_heads, comm=None):
    t = xbc.shape[0]
    q = SSD_CHUNK
    gn = SSD_GROUPS * SSD_STATE
    sw = n_heads * SSD_HEAD_DIM
    n_pairs = n_heads // 2
    pairs_per_group = n_pairs // SSD_GROUPS
    nc = t // q
    bblk = sw // gn

    def body(dy_ref, xs_ref, b_ref, c_ref, dtr_ref, bias_ref, a_ref, dsk_ref, sin_ref,
             dxs_ref, db_ref, dc_ref, ddtr_ref, dbias_ref, dalog_ref,
             dstate, tbuf, xbuf, rbuf, acc_a, acc_b, sel_ref):
        i = pl.program_id(0)

        @pl.when(i == 0)
        def _():
            dstate[...] = jnp.zeros_like(dstate)
            rbuf[...] = jnp.zeros_like(rbuf)
            acc_a[...] = jnp.zeros_like(acc_a)
            acc_b[...] = jnp.zeros_like(acc_b)
            sel_ref[...] = _head_selector(sw, SSD_HEAD_DIM)

        tm_ = _ssd_chunk_terms(dtr_ref, bias_ref, a_ref, n_heads)
        lane = tm_["lane"]
        lo = lane < SSD_HEAD_DIM
        head_row = lax.broadcasted_iota(jnp.int32, (LANES, 1), 0)
        rows = jnp.zeros((q, LANES), F32)
        cols_t = jnp.zeros((LANES, q), F32)
        for g in range(SSD_GROUPS):
            gcols = slice(g * SSD_STATE, (g + 1) * SSD_STATE)
            bg = b_ref[:, gcols].astype(BF16)
            cg = c_ref[:, gcols].astype(BF16)
            cg_t = c_ref[:, gcols].T.astype(BF16)
            cb = lax.dot_general(cg, bg, NT, preferred_element_type=F32)
            cbt = lax.dot_general(bg, cg, NT, preferred_element_type=F32)
            dcb = jnp.zeros((q, q), F32)
            db_acc = jnp.zeros((q, SSD_STATE), F32)
            dc_acc = jnp.zeros((q, SSD_STATE), F32)
            for j in range(pairs_per_group):
                p = g * pairs_per_group + j
                h = 2 * p
                pcols = slice(p * LANES, (p + 1) * LANES)
                l0, l1, mcat = _pair_decay(tm_, cb, h)
                xp = xs_ref[:, pcols]
                dtp = _pair_cols(lo, tm_["dt"], h)
                xdt = xp * dtp
                xbd = _block_diag(lo, xdt)
                dyp = dy_ref[:, pcols]
                dyb = dyp.astype(BF16)
                dsb = _pair_cols(lo, tm_["dstate"], h)
                cdr = _pair_cols(lo, tm_["cdec"], h)
                eb = _pair_cols(lo, tm_["eacs"], h)
                st = sin_ref[0, p]
                stb = st.astype(BF16)
                dst = dstate[p]
                dstb = dst.astype(BF16)
                dye = (dyp * eb).astype(BF16)
                both = jnp.dot(_pair_decay_t(tm_, cbt, h), dyb, preferred_element_type=F32)
                dx_state = jnp.dot(bg, dstb, preferred_element_type=F32) * dsb
                dxdt = jnp.where(lo, both[:q], both[q:]) + dx_state
                dmcat = lax.dot_general(dyb, xbd, NT, preferred_element_type=F32)
                dcb = dcb + dmcat[:, :q] * l0 + dmcat[:, q:] * l1
                dseg = dmcat * mcat
                csum = jnp.sum(dseg, axis=0, keepdims=True)
                rows = (rows + jnp.where(lane == h, jnp.sum(dseg[:, :q], axis=1, keepdims=True), 0.0)
                        + jnp.where(lane == h + 1, jnp.sum(dseg[:, q:], axis=1, keepdims=True), 0.0))
                cols_t = (cols_t + jnp.where(head_row == h, csum[:, :q], 0.0)
                          + jnp.where(head_row == h + 1, csum[:, q:], 0.0))
                dc_acc = dc_acc + lax.dot_general(dye, stb, NT, preferred_element_type=F32)
                db_acc = db_acc + lax.dot_general((xdt * dsb).astype(BF16), dstb, NT, preferred_element_type=F32)
                yoff = jnp.dot(cg, stb, preferred_element_type=F32) * eb
                tbuf[:, pcols] = dyp * yoff - xdt * dx_state
                xbuf[:, pcols] = dxdt * xp
                rbuf[0:1, pcols] = (jnp.sum(xdt * dx_state, axis=0, keepdims=True)
                                    + cdr * jnp.sum(dst * st, axis=0, keepdims=True))
                dxs_ref[:, pcols] = dxdt * dtp + dyp * dsk_ref[:, pcols]
                dstate[p] = dst * cdr + jnp.dot(cg_t, dye, preferred_element_type=F32)
            dcbb = dcb.astype(BF16)
            dc_ref[:, gcols] = dc_acc + jnp.dot(dcbb, bg, preferred_element_type=F32)
            db_ref[:, gcols] = db_acc + lax.dot_general(dcbb, cg, TN, preferred_element_type=F32)

        sel = sel_ref[...]
        dacs = rows - cols_t.T + _split_dot(tbuf[...], sel)
        carry = _split_dot(rbuf[...], sel)[0:1]
        anti = jnp.logical_not(tm_["causal"]) | tm_["diag"]
        da = jnp.dot(anti.astype(F32), dacs, precision=HIGHEST, preferred_element_type=F32) + carry
        ddt = da * tm_["a"] + _split_dot(xbuf[...], sel)
        ddtr = jnp.where(tm_["lane"] < n_heads, ddt * jax.nn.sigmoid(tm_["pre"]), 0.0)
        ddtr_ref[...] = ddtr.astype(BF16)
        acc_b[...] += jnp.sum(ddtr, axis=0, keepdims=True)
        acc_a[...] += jnp.sum(da * tm_["dt"], axis=0, keepdims=True)

        @pl.when(i == nc - 1)
        def _():
            dbias_ref[...] = acc_b[...]
            dalog_ref[...] = acc_a[...] * tm_["a"]

    vec = pl.BlockSpec((1, LANES), lambda i: (0, 0))
    wide = pl.BlockSpec((q, sw), lambda i: (nc - 1 - i, 0))
    return _call(
        body, (dy, xbc, xbc, xbc, dt_raw, dt_bias, a_log, d_full, s_in), name="ssd_bwd", grid=(nc,),
        in_specs=[wide, wide,
                  pl.BlockSpec((q, gn), lambda i: (nc - 1 - i, bblk)),
                  pl.BlockSpec((q, gn), lambda i: (nc - 1 - i, bblk + 1)),
                  pl.BlockSpec((q, LANES), lambda i: (nc - 1 - i, 0)), vec, vec,
                  pl.BlockSpec((1, sw), lambda i: (0, 0)),
                  pl.BlockSpec((1, n_pairs, SSD_STATE, LANES), lambda i: (nc - 1 - i, 0, 0, 0))],
        out_specs=[wide, pl.BlockSpec((q, gn), lambda i: (nc - 1 - i, 0)), pl.BlockSpec((q, gn), lambda i: (nc - 1 - i, 0)),
                   pl.BlockSpec((q, LANES), lambda i: (nc - 1 - i, 0)), vec, vec],
        out_shape=[SDS((t, sw), F32), SDS((t, gn), F32), SDS((t, gn), F32), SDS((t, LANES), BF16),
                   SDS((1, LANES), F32), SDS((1, LANES), F32)],
        scratch_shapes=[pltpu.VMEM((n_pairs, SSD_STATE, LANES), F32), pltpu.VMEM((q, sw), F32), pltpu.VMEM((q, sw), F32),
                        pltpu.VMEM((8, sw), F32), pltpu.VMEM((1, LANES), F32), pltpu.VMEM((1, LANES), F32),
                        pltpu.VMEM((sw, LANES), BF16)],
        sem=("arbitrary",), comm=comm)


def _gated(y_ref, xs_ref, z_ref, dsk_ref):
    y1 = y_ref[...] + dsk_ref[...] * xs_ref[...]
    return y1, y1 * _silu(z_ref[...])


def gate_norm_fwd(y, xbc, proj, d_full, norm_w, mixed, z_block, tm):
    t, sw = y.shape
    gw = sw // SSD_GROUPS

    def body(y_ref, xs_ref, z_ref, dsk_ref, nw_ref, alias_ref, o_ref):
        _, y2 = _gated(y_ref, xs_ref, z_ref, dsk_ref)
        for g in range(SSD_GROUPS):
            cols = slice(g * gw, (g + 1) * gw)
            blk = y2[:, cols]
            r = lax.rsqrt(jnp.mean(blk * blk, axis=-1, keepdims=True) + NORM_EPS)
            o_ref[:, cols] = (blk * r * nw_ref[:, cols]).astype(BF16)

    row = pl.BlockSpec((tm, sw), lambda i: (i, 0))
    vec = pl.BlockSpec((1, sw), lambda i: (0, 0))
    return pl.pallas_call(
        body, name="gate_norm_fwd", grid=(t // tm,),
        in_specs=[row, row, pl.BlockSpec((tm, sw), lambda i: (i, z_block)), vec, vec, pl.BlockSpec(memory_space=pl.ANY)],
        out_specs=pl.BlockSpec((tm, sw), lambda i: (i, 1)),
        out_shape=SDS(mixed.shape, mixed.dtype),
        input_output_aliases={5: 0},
        compiler_params=_params(("arbitrary",)),
    )(y, xbc, proj, d_full, norm_w, mixed)


def gate_norm_bwd(dmixed, y, xbc, proj, d_full, norm_w, dproj, z_block, tm):
    t, sw = y.shape
    gw = sw // SSD_GROUPS
    nt = t // tm

    def body(d_ref, y_ref, xs_ref, z_ref, dsk_ref, nw_ref, alias_ref, dy_ref, dz_ref, dnw_ref, dd_ref, acc_d):
        i = pl.program_id(0)

        @pl.when(i == 0)
        def _():
            dnw_ref[...] = jnp.zeros_like(dnw_ref)
            acc_d[...] = jnp.zeros_like(acc_d)

        y1, y2 = _gated(y_ref, xs_ref, z_ref, dsk_ref)
        d3 = d_ref[...]
        parts = []
        for g in range(SSD_GROUPS):
            cols = slice(g * gw, (g + 1) * gw)
            blk = y2[:, cols]
            r = lax.rsqrt(jnp.mean(blk * blk, axis=-1, keepdims=True) + NORM_EPS)
            n = blk * r
            dg = d3[:, cols]
            dnw_ref[:, cols] += jnp.sum(dg * n, axis=0, keepdims=True)
            parts.append(_norm_bwd(dg * nw_ref[:, cols], n, r))
        dy2 = jnp.concatenate(parts, axis=1)
        zv = z_ref[...]
        dz_ref[...] = (dy2 * y1 * _dsilu(zv)).astype(BF16)
        dy1 = dy2 * _silu(zv)
        dy_ref[...] = dy1
        acc_d[0:1, :] += jnp.sum(dy1 * xs_ref[...], axis=0, keepdims=True)

        @pl.when(i == nt - 1)
        def _():
            dd_ref[...] = _split_dot(acc_d[...], _head_selector(sw, SSD_HEAD_DIM))[0:1]

    row = pl.BlockSpec((tm, sw), lambda i: (i, 0))
    vec = pl.BlockSpec((1, sw), lambda i: (0, 0))
    return pl.pallas_call(
        body, name="gate_norm_bwd", grid=(nt,),
        in_specs=[pl.BlockSpec((tm, sw), lambda i: (i, 1)), row, row, pl.BlockSpec((tm, sw), lambda i: (i, z_block)),
                  vec, vec, pl.BlockSpec(memory_space=pl.ANY)],
        out_specs=[row, pl.BlockSpec((tm, sw), lambda i: (i, z_block)), vec, pl.BlockSpec((1, LANES), lambda i: (0, 0))],
        out_shape=[SDS((t, sw), F32), SDS(dproj.shape, dproj.dtype), SDS((1, sw), F32), SDS((1, LANES), F32)],
        scratch_shapes=[pltpu.VMEM((8, sw), F32)],
        input_output_aliases={6: 1},
        compiler_params=_params(("arbitrary",)),
    )(dmixed, y, xbc, proj, d_full, norm_w, dproj)


GATE_BLOCK = 1
Z_BLOCK = 2
CONV_BLOCK = 2


def _tiles(t):
    return min(256, t), min(512, t)


def _place():
    x, y, c = lax.axis_index("x"), lax.axis_index("y"), lax.axis_index("c")
    return x, y, c, [(1 - x, y), (x, 1 - y), (1 - x, 1 - y)]


def gather_spread(shards, layer):
    def make(ins, outs, ss, rs):
        x, y, c, chips = _place()
        mine = 4 * x + 2 * y + c
        peers = [(x, y, 1 - c)] + [(px, py, c) for px, py in chips]
        sends, locals_, arrivals = [], [], []
        for a in range(len(ins)):
            src = ins[a].at[layer]
            locals_.append(pltpu.make_async_copy(src, outs[a].at[mine], ss.at[5 * a + 4]))
            for j, (px, py, pc) in enumerate(peers):
                sends.append(_remote(src, outs[a].at[mine], ss, rs, 5 * a + j, (px, py, pc)))
                arrivals.append(_remote(src, outs[a].at[4 * px + 2 * py + pc], ss, rs, 5 * a + j, (px, py, pc)))
        return sends, locals_, arrivals

    return Comm(shards, [SDS((N_DEV,) + s.shape[1:], s.dtype) for s in shards], {}, 5 * len(shards), make)


def gather_pass_on(gathered):
    def make(ins, outs, ss, rs):
        x, y, c, chips = _place()
        sends, arrivals = [], []
        for a in range(len(outs)):
            for j, (px, py) in enumerate(chips):
                blk, other = 4 * px + 2 * py + c, 4 * px + 2 * py + (1 - c)
                sends.append(_remote(outs[a].at[blk], outs[a].at[blk], ss, rs, 3 * a + j, (x, y, 1 - c)))
                arrivals.append(_remote(outs[a].at[other], outs[a].at[other], ss, rs, 3 * a + j, (x, y, 1 - c)))
        return sends, [], arrivals

    return Comm(gathered, [SDS(g.shape, g.dtype) for g in gathered], {a: a for a in range(len(gathered))},
                3 * len(gathered), make)


def sibling_swap(sends_):
    def make(ins, outs, ss, rs):
        x, y, c, _ = _place()
        cps = [_remote(ins[a], outs[a], ss, rs, a, (x, y, 1 - c)) for a in range(len(ins))]
        return cps, [], cps

    return Comm(sends_, [SDS(s.shape, s.dtype) for s in sends_], {}, len(sends_), make)


def chips_scatter(slabs):
    def make(ins, outs, ss, rs):
        x, y, c, chips = _place()
        mychip = 2 * x + y
        sends, arrivals = [], []
        for a in range(len(ins)):
            for j, (px, py) in enumerate(chips):
                k = 2 * px + py
                sends.append(_remote(ins[a].at[k], outs[a].at[lax.rem(mychip - k + 4, 4) - 1], ss, rs, 3 * a + j, (px, py, c)))
                arrivals.append(_remote(ins[a].at[k], outs[a].at[lax.rem(k - mychip + 4, 4) - 1], ss, rs, 3 * a + j, (px, py, c)))
        return sends, [], arrivals

    return Comm(slabs, [SDS((3,) + s.shape[1:], s.dtype) for s in slabs], {}, 3 * len(slabs), make)


def comm_only(comm, name):
    def body():
        pass

    return _call(body, (), name=name, grid=(), in_specs=[], out_specs=[], out_shape=[], comm=comm)[1]


def layer_fwd(x, p, nxt=None):
    t = x.shape[0]
    tm, tmm = _tiles(t)
    n_heads = p["d_full"].shape[1] // SSD_HEAD_DIM
    h, r_pre = rms_fwd(x, p["pre_w"], tm)
    proj, got_a = mm_nn(h, p["w_main"], F32, tmm, 1024, "in_proj", gather_spread(nxt[0], nxt[2]) if nxt else None)
    dt_raw, _ = mm_nn(h, p["w_dt"], F32, tmm, LANES, "dt_proj")
    mixed = pool_fwd(proj, p["mixw"], p["pscale"], tm)
    xbc = conv_fwd(proj, p["conv_w"], p["conv_b"], CONV_BLOCK, tm)
    (y, s_in), got_b = ssd_fwd(xbc, dt_raw, p["dt_bias"], p["a_log"], n_heads, gather_spread(nxt[1], nxt[2]) if nxt else None)
    mixed = gate_norm_fwd(y, xbc, proj, p["d_full"], p["norm_w"], mixed, Z_BLOCK, tm)
    out, gathered = mm_nn(mixed, p["w_out"], F32, tmm, 512, "out_proj", gather_pass_on(got_a + got_b) if nxt else None)
    x_next, r_post = post_fwd(out, x, p["post_w"], tm)
    return x_next, dict(x=x, h=h, r_pre=r_pre, proj=proj, dt_raw=dt_raw, xbc=xbc, y=y, s_in=s_in, mixed=mixed,
                        out=out, r_post=r_post), gathered


def _pair_sums(own, got):
    return [pair_sum(o, r, min(256, o.shape[1]), "pair_sum") for o, r in zip(own, got)]


def layer_bwd(g, s, p, split_in, split_rest, pending=None, last=False):
    t = g.shape[0]
    tm, tmm = _tiles(t)
    d = g.shape[1]
    n_heads = p["d_full"].shape[1] // SSD_HEAD_DIM
    d_out, d_post = post_bwd(g, s["out"], s["r_post"], p["post_w"], tm)
    dmixed, got_sib = mm_nt(d_out, p["w_out"], F32, tmm, 1024, d, "d_mixed", sibling_swap(pending[1]) if pending else None)
    chip_sums = _pair_sums(pending[0], got_sib) if pending else []
    dw_out, _ = mm_tn(s["mixed"], d_out, 512, 512, "dw_out")
    dproj, dq, d_pscale, d_mixw = pool_bwd_a(dmixed, s["proj"], p["mixw"], p["pscale"], tm)
    dproj = pool_bwd_b(dq, dproj, tm)
    own_rest, send_rest = split_rest(dw_out, d_mixw)
    dy, dproj, d_norm, d_dskip = gate_norm_bwd(dmixed, s["y"], s["xbc"], s["proj"], p["d_full"], p["norm_w"], dproj,
                                               Z_BLOCK, tm)
    (dxs, db, dc, ddtr, d_dtb, d_alog), got = ssd_bwd(
        dy, s["xbc"], s["dt_raw"], p["dt_bias"], p["a_log"], p["d_full"], s["s_in"], n_heads,
        merge_comms([chips_scatter(chip_sums[:1]) if pending else None, sibling_swap(send_rest) if last else None]))
    got_first, my_sib_rest = (got[:1], got[1:]) if pending else ([], got)
    (dpre, d_convw, d_convb), got_rest = conv_bwd_a(dxs, db, dc, s["proj"], p["conv_w"], p["conv_b"], CONV_BLOCK, tm,
                                                    chips_scatter(chip_sums[1:]) if pending else None)
    dproj = conv_bwd_b(dpre, p["conv_w"], dproj, CONV_BLOCK, tm)
    dw_main, my_chips_rest = mm_tn(s["h"], dproj, 512, 1024, "dw_main",
                                   chips_scatter(_pair_sums(own_rest, my_sib_rest)) if last else None)
    dw_dt, _ = mm_tn(s["h"], ddtr, 512, LANES, "dw_dt")
    own_in, send_in = split_in(dw_main, dw_dt)
    my_sib_in = comm_only(sibling_swap(send_in), "grads_to_sibling") if last else []
    dh_main, my_chips_in = mm_nt(dproj, p["w_main"], F32, tmm, d, 1024, "dh_main",
                                 chips_scatter(_pair_sums(own_in, my_sib_in)) if last else None)
    dh_dt, _ = mm_nt(ddtr, p["w_dt"], F32, tmm, d, LANES, "dh_dt")
    gx, d_pre = rms_bwd(dh_main, dh_dt, s["x"], s["r_pre"], p["pre_w"], g, tm)
    small = dict(pre_w=d_pre, pscale=d_pscale, conv_w=d_convw, conv_b=d_convb, dt_bias=d_dtb, a_log=d_alog,
                 d_skip=d_dskip, norm_w=d_norm, post_w=d_post)
    done = [(got_sib, got_first + got_rest)] if pending else [None]
    if last:
        done.append((my_sib_in + my_sib_rest, my_chips_in + my_chips_rest))
    return gx, small, (own_in + own_rest, send_in + send_rest), done


def _two_level_gather(x_refs, out_slots, send_sems, recv_sems, local_sems):
    x, y, c, chips = _place()
    me, sibling = (x, y, c), (x, y, 1 - c)
    n = len(x_refs)

    def copy(a, k, block, to, src=None):
        return pltpu.make_async_remote_copy(
            src_ref=out_slots[a](*block) if src is None else src, dst_ref=out_slots[a](*block),
            send_sem=send_sems.at[7 * a + k], recv_sem=recv_sems.at[7 * a + k], device_id=to, device_id_type=MESH)

    mine = [pltpu.make_async_copy(x_refs[a], out_slots[a](*me), local_sems.at[a]) for a in range(n)]
    for cp in mine:
        cp.start()
    first = []
    for a in range(n):
        first.append(copy(a, 0, me, sibling, src=x_refs[a]))
        first += [copy(a, 1 + j, me, (*chip, c), src=x_refs[a]) for j, chip in enumerate(chips)]
    for cp in first:
        cp.start()
    passed = []
    for j, chip in enumerate(chips):
        for a in range(n):
            copy(a, 1 + j, (*chip, c), me).wait_recv()
            fwd = copy(a, 4 + j, (*chip, c), sibling)
            fwd.start()
            passed.append(fwd)
    for a in range(n):
        copy(a, 0, sibling, me).wait_recv()
        for j, chip in enumerate(chips):
            copy(a, 4 + j, (*chip, 1 - c), me).wait_recv()
    for cp in first + passed:
        cp.wait_send()
    for cp in mine:
        cp.wait()


def all_gather_hbm(shards, name):
    n = len(shards)

    def body(*refs):
        x_refs, out_refs = refs[:n], refs[n:2 * n]
        send_sems, recv_sems, local_sems = refs[2 * n:]
        slots = [lambda px, py, pc, o=o: o.at[:, 4 * px + 2 * py + pc] for o in out_refs]
        _two_level_gather(x_refs, slots, send_sems, recv_sems, local_sems)

    hbm = pl.BlockSpec(memory_space=pl.ANY)
    return pl.pallas_call(
        body, name=name,
        out_shape=[SDS((s.shape[0], N_DEV) + s.shape[1:], s.dtype) for s in shards],
        in_specs=[hbm] * n, out_specs=[hbm] * n,
        scratch_shapes=[pltpu.SemaphoreType.DMA((7 * n,)), pltpu.SemaphoreType.DMA((7 * n,)), pltpu.SemaphoreType.DMA((n,))],
    )(*shards)


def all_gather_vmem(block, name):
    r, c_ = block.shape

    def body(x_ref, out_ref, send_sems, recv_sems, local_sems):
        _two_level_gather([x_ref], [lambda px, py, pc: out_ref.at[4 * px + 2 * py + pc]], send_sems, recv_sems, local_sems)

    return pl.pallas_call(
        body, name=name, out_shape=SDS((N_DEV, r, c_), block.dtype),
        in_specs=[pl.BlockSpec(memory_space=pltpu.VMEM)], out_specs=pl.BlockSpec(memory_space=pltpu.VMEM),
        scratch_shapes=[pltpu.SemaphoreType.DMA((7,)), pltpu.SemaphoreType.DMA((7,)), pltpu.SemaphoreType.DMA((1,))],
        compiler_params=_params(),
    )(block)


def _block_tiles(cols):
    base = [(cols * i) // LANES for i in range(N_DEV)]
    ends = [-((-cols * (i + 1)) // LANES) for i in range(N_DEV)]
    return base, ends, max(e - b for b, e in zip(base, ends))


def _my_lane_offset(cols):
    me = 4 * lax.axis_index("x") + 2 * lax.axis_index("y") + lax.axis_index("c")
    return lax.rem(cols * me, LANES)


def shift_cast(w, tr):
    nl, r, cols = w.shape
    width = _block_tiles(cols)[2] * LANES

    def body(x_ref, o_ref, pad):
        pad[:, width - LANES:] = jnp.zeros((tr, LANES), F32)
        pad[:, :cols] = x_ref[...]
        o_ref[...] = pltpu.roll(pad[...], _my_lane_offset(cols), 1).astype(BF16)

    assert width - LANES <= cols
    return pl.pallas_call(
        body, name="shift_cast", grid=(nl, r // tr),
        in_specs=[pl.BlockSpec((pl.Squeezed(), tr, cols), lambda l, i: (l, i, 0))],
        out_specs=pl.BlockSpec((pl.Squeezed(), tr, width), lambda l, i: (l, i, 0)),
        out_shape=SDS((nl, r, width), BF16), scratch_shapes=[pltpu.VMEM((tr, width), F32)],
        compiler_params=_params(("arbitrary", "arbitrary")))(w)


def assemble_w_in(blocks, cols, n_tail, tr):
    _, r, width = blocks.shape
    base, ends, _ = _block_tiles(cols)
    total = ends[-1]
    main_tiles = (N_DEV * cols - n_tail) // LANES
    assert main_tiles == total - 1 and (N_DEV * cols - n_tail) % LANES == 0

    def body(b_ref, main_ref, tail_ref):
        for tile in range(total):
            parts = [b_ref[i, :, (tile - base[i]) * LANES:(tile - base[i] + 1) * LANES]
                     for i in range(N_DEV) if base[i] <= tile < ends[i]]
            val = parts[0] if len(parts) == 1 else parts[0] + parts[1]
            if tile < main_tiles:
                main_ref[:, tile * LANES:(tile + 1) * LANES] = val
            else:
                tail_ref[...] = val

    return pl.pallas_call(
        body, name="assemble_w_in", grid=(r // tr,),
        in_specs=[pl.BlockSpec((N_DEV, tr, width), lambda i: (0, i, 0))],
        out_specs=[pl.BlockSpec((tr, main_tiles * LANES), lambda i: (i, 0)), pl.BlockSpec((tr, LANES), lambda i: (i, 0))],
        out_shape=[SDS((r, main_tiles * LANES), blocks.dtype), SDS((r, LANES), blocks.dtype)],
        compiler_params=_params(("arbitrary",)),
    )(blocks)


def grad_blocks(dw_main, dw_tail, cols, tr):
    r = dw_main.shape[0]
    base, _, tpb = _block_tiles(cols)
    width = tpb * LANES

    def body(m_ref, t_ref, own_ref, send_ref):
        cat = jnp.concatenate([m_ref[...], t_ref[...]], axis=1)
        south = lax.axis_index("c") == 0
        for k in range(N_DEV // 2):
            a = cat[:, base[2 * k] * LANES:base[2 * k] * LANES + width]
            b = cat[:, base[2 * k + 1] * LANES:base[2 * k + 1] * LANES + width]
            own_ref[k] = jnp.where(south, a, b)
            send_ref[k] = jnp.where(south, b, a).astype(BF16)

    return pl.pallas_call(
        body, name="grad_blocks", grid=(r // tr,),
        in_specs=[pl.BlockSpec((tr, dw_main.shape[1]), lambda i: (i, 0)), pl.BlockSpec((tr, LANES), lambda i: (i, 0))],
        out_specs=[pl.BlockSpec((N_DEV // 2, tr, width), lambda i: (0, i, 0))] * 2,
        out_shape=[SDS((N_DEV // 2, r, width), F32), SDS((N_DEV // 2, r, width), BF16)],
        compiler_params=_params(("arbitrary",)),
    )(dw_main, dw_tail)


def _adamw(w, g, m, v):
    m = ADAM_B1 * m + (1.0 - ADAM_B1) * g
    v = ADAM_B2 * v + (1.0 - ADAM_B2) * jnp.square(g)
    m_hat = m / (1.0 - ADAM_B1 ** ADAM_STEP)
    v_hat = v / (1.0 - ADAM_B2 ** ADAM_STEP)
    delta = -ADAM_LR * (m_hat / (jnp.sqrt(v_hat) + ADAM_EPS) + ADAM_WD * w)
    return delta, m, v


def pair_sum(own, got, tr, name):
    k, r, c_ = own.shape

    def body(a_ref, b_ref, o_ref):
        o_ref[...] = (a_ref[...] + b_ref[...].astype(F32)).astype(BF16)

    blk = pl.BlockSpec((pl.Squeezed(), tr, c_), lambda kk, i: (kk, i, 0))
    return pl.pallas_call(
        body, name=name, grid=(k, r // tr), in_specs=[blk, blk], out_specs=blk, out_shape=SDS(own.shape, BF16),
        compiler_params=_params(("arbitrary", "arbitrary")),
    )(own, got)


def reduce_adam(own, got_sibling, got_chips, w, m, v, prev, layer, tr, name, shifted=False):
    nl, r, cols = w.shape
    c_ = own.shape[-1]
    n_scratch = 1 if shifted else 0

    def body(own_ref, sib_ref, c0_ref, c1_ref, c2_ref, w_ref, m_ref, v_ref, *rest):
        g_ref, d_ref, nm_ref, nv_ref = rest[len(rest) - n_scratch - 4:len(rest) - n_scratch]
        g = (own_ref[...] + sib_ref[...].astype(F32) + c0_ref[...].astype(F32) + c1_ref[...].astype(F32)
             + c2_ref[...].astype(F32))
        if shifted:
            rest[-1][...] = pltpu.roll(g, c_ - _my_lane_offset(cols), 1)
            g = rest[-1][:, :cols]
        delta, nm, nv = _adamw(w_ref[...], g, m_ref[...], v_ref[...])
        g_ref[...] = g
        d_ref[...] = delta
        nm_ref[...] = nm
        nv_ref[...] = nv

    row = pl.BlockSpec((tr, c_), lambda i: (i, 0))
    lay = pl.BlockSpec((pl.Squeezed(), tr, cols), lambda i: (layer, i, 0))
    chips = [pl.BlockSpec((pl.Squeezed(), tr, c_), lambda i, s=s: (s, i, 0)) for s in range(3)]
    in_specs = [row, row] + chips + [lay, lay, lay]
    args = [own, got_sibling, got_chips, got_chips, got_chips, w, m, v]
    aliases = {}
    if prev is not None:
        in_specs += [pl.BlockSpec(memory_space=pl.ANY)] * 4
        aliases = {len(args) + k: k for k in range(4)}
        args += list(prev)
    return pl.pallas_call(
        body, name=name, grid=(r // tr,), in_specs=in_specs, out_specs=[lay] * 4,
        out_shape=[SDS((nl, r, cols), F32)] * 4, input_output_aliases=aliases,
        scratch_shapes=[pltpu.VMEM((tr, c_), F32)] * n_scratch,
        compiler_params=_params(("arbitrary",)),
    )(*args)


def sum_devices(packs):
    n, r, c_ = packs.shape

    def body(p_ref, o_ref):
        acc = p_ref[0]
        for k in range(1, n):
            acc = acc + p_ref[k]
        o_ref[...] = acc

    return pl.pallas_call(body, name="sum_devices", out_shape=SDS((r, c_), F32), compiler_params=_params())(packs)


def adam_small(w, g, m, v):
    def body(w_ref, g_ref, m_ref, v_ref, d_ref, nm_ref, nv_ref):
        delta, nm, nv = _adamw(w_ref[...], g_ref[...], m_ref[...], v_ref[...])
        d_ref[...] = delta
        nm_ref[...] = nm
        nv_ref[...] = nv

    return pl.pallas_call(body, name="adam_small", out_shape=[SDS(w.shape, F32)] * 3, compiler_params=_params())(w, g, m, v)


SMALL = ("pre_norm_w", "pool_scale", "conv_b", "dt_bias", "a_log", "d_skip", "_pad", "ssd_norm_w", "post_norm_w", "conv_w")


def _pack(parts):
    flat = jnp.concatenate([parts[k] for k in SMALL], axis=1).reshape(-1, LANES)
    return jnp.pad(flat, ((0, (-flat.shape[0]) % 8), (0, 0)))


def _unpack(pack, sizes, nl):
    total = sum(sizes[k] for k in SMALL)
    flat = pack[: nl * total // LANES].reshape(nl, total)
    out, o = {}, 0
    for k in SMALL:
        out[k] = flat[:, o:o + sizes[k]]
        o += sizes[k]
    return out


def kernel(x, pre_norm_w, w_in, pool_mix_w, pool_scale, conv_w, conv_b, dt_bias, a_log, d_skip, ssd_norm_w, w_out, post_norm_w, loss_target, m_pre_norm_w, m_w_in, m_pool_mix_w, m_pool_scale, m_conv_w, m_conv_b, m_dt_bias, m_a_log, m_d_skip, m_ssd_norm_w, m_w_out, m_post_norm_w, v_pre_norm_w, v_w_in, v_pool_mix_w, v_pool_scale, v_conv_w, v_conv_b, v_dt_bias, v_a_log, v_d_skip, v_ssd_norm_w, v_w_out, v_post_norm_w):
    cx, cy, cc = lax.axis_index("x"), lax.axis_index("y"), lax.axis_index("c")
    me = 4 * cx + 2 * cy + cc
    mychip = 2 * cx + cy
    nl, d, cols = w_in.shape
    t = x.shape[1]
    n_heads = a_log.shape[1]
    sw = n_heads * SSD_HEAD_DIM
    pw = pool_scale.shape[1]
    cd = conv_b.shape[1]
    ng, gsh, gw = pool_mix_w.shape[1:]
    e_main = N_DEV * cols - n_heads
    assert x.shape[0] == 1 and pw == sw and cd == sw + 2 * SSD_GROUPS * SSD_STATE and e_main == 2 * pw + sw + cd
    assert 2 * pw + sw == CONV_BLOCK * cd and n_heads <= LANES and t % SSD_CHUNK == 0 and gsh * N_DEV == gw
    tm, _ = _tiles(t)

    shards_a, shards_b = [shift_cast(w_in, tm)], [w_out.astype(BF16), pool_mix_w.astype(BF16), conv_w]
    gathered = [a[0] for a in all_gather_hbm([s[:1] for s in shards_a + shards_b], "gather_weights")]
    pad_h = ((0, 0), (0, LANES - n_heads))

    def layer_params(l, g_in, g_out, g_mix, g_conv):
        w_main, w_dt = assemble_w_in(g_in, cols, n_heads, tm)
        return dict(pre_w=pre_norm_w[l:l + 1], w_main=w_main, w_dt=w_dt,
                    mixw=g_mix.transpose(1, 0, 2, 3).reshape(ng, gw, gw), pscale=pool_scale[l:l + 1],
                    conv_w=g_conv.transpose(1, 0, 2).reshape(CONV_WIDTH, cd), conv_b=conv_b[l:l + 1],
                    dt_bias=jnp.pad(dt_bias[l:l + 1], pad_h), a_log=jnp.pad(a_log[l:l + 1], pad_h),
                    d_full=jnp.repeat(d_skip[l:l + 1], SSD_HEAD_DIM, axis=1), norm_w=ssd_norm_w[l:l + 1],
                    w_out=g_out.reshape(N_DEV * w_out.shape[1], d), post_w=post_norm_w[l:l + 1])

    xs = x[0]
    saved, params = [], []
    for l in range(nl):
        params.append(layer_params(l, *gathered))
        xs, s, gathered = layer_fwd(xs, params[l], (shards_a, shards_b, l + 1) if l + 1 < nl else None)
        saved.append(s)
    loss_part, g = loss_grad(xs, loss_target[0], tm)
    loss = lax.psum(loss_part[0, 0], ("x", "y", "c"))

    big = {"w_in": (w_in, m_w_in, v_w_in), "w_out": (w_out, m_w_out, v_w_out),
           "pool_mix_w": tuple(a.reshape(nl, ng * gsh, gw) for a in (pool_mix_w, m_pool_mix_w, v_pool_mix_w))}
    names = list(big)
    big_out = {k: None for k in big}
    small_g = [None] * nl

    def apply(layer, own, got_sib, got_chips):
        for k, o, gs_, gc in zip(names, own, got_sib, got_chips):
            wk, mk, vk = big[k]
            big_out[k] = reduce_adam(lax.dynamic_index_in_dim(o, mychip, 0, keepdims=False),
                                     lax.dynamic_index_in_dim(gs_, mychip, 0, keepdims=False),
                                     gc, wk, mk, vk, big_out[k], layer, min(256, wk.shape[1]), "reduce_adam_" + k,
                                     shifted=(k == "w_in"))

    def split_in(dw_main, dw_dt):
        own, send = grad_blocks(dw_main, dw_dt, cols, min(128, d))
        return [own], [send]

    def split_rest(dw_out, d_mixw):
        halves = [lambda ci: lax.dynamic_index_in_dim(dw_out.reshape(4, 2, -1, d), ci, 1, keepdims=False),
                  lambda ci: lax.dynamic_index_in_dim(
                      d_mixw.reshape(ng, 4, 2, gsh, gw), ci, 2, keepdims=False).transpose(1, 0, 2, 3).reshape(4, ng * gsh, gw)]
        return [h(cc) for h in halves], [h(1 - cc).astype(BF16) for h in halves]

    pending = None
    for l in reversed(range(nl)):
        g, gr, mine, done = layer_bwd(g, saved[l], params[l], split_in, split_rest, pending, last=(l == 0))
        if pending is not None:
            apply(l + 1, pending[0], *done[0])
        if l == 0:
            apply(0, mine[0], *done[1])
        pending = mine
        small_g[l] = dict(pre_norm_w=gr["pre_w"], pool_scale=gr["pscale"], conv_b=gr["conv_b"], dt_bias=gr["dt_bias"][:, :n_heads],
                          a_log=gr["a_log"][:, :n_heads], d_skip=gr["d_skip"][:, :n_heads], _pad=jnp.zeros((1, LANES - 3 * n_heads), F32),
                          ssd_norm_w=gr["norm_w"], post_norm_w=gr["post_w"], conv_w=gr["conv_w"].reshape(1, CONV_WIDTH * cd))

    sizes = {k: small_g[0][k].shape[1] for k in SMALL}
    gsum = sum_devices(all_gather_vmem(_pack({k: jnp.concatenate([sg[k] for sg in small_g], axis=0) for k in SMALL}),
                                       "gather_small_grads"))
    gs = _unpack(gsum, sizes, nl)
    csh = conv_w.shape[2]
    gs["conv_w"] = lax.dynamic_slice_in_dim(gs["conv_w"].reshape(nl, CONV_WIDTH, cd), me * csh, csh, axis=2).reshape(nl, -1)
    lsizes = dict(sizes, conv_w=CONV_WIDTH * csh)
    zpad = jnp.zeros((nl, sizes["_pad"]), F32)

    def local(pre, scale, cb, dtb, al, dsk, nw, post, cw):
        return _pack(dict(pre_norm_w=pre, pool_scale=scale, conv_b=cb, dt_bias=dtb, a_log=al, d_skip=dsk, _pad=zpad,
                          ssd_norm_w=nw, post_norm_w=post, conv_w=cw.reshape(nl, -1)))

    wp = local(pre_norm_w, pool_scale, conv_b, dt_bias, a_log, d_skip, ssd_norm_w, post_norm_w, conv_w)
    mp = local(m_pre_norm_w, m_pool_scale, m_conv_b, m_dt_bias, m_a_log, m_d_skip, m_ssd_norm_w, m_post_norm_w, m_conv_w)
    vp = local(v_pre_norm_w, v_pool_scale, v_conv_b, v_dt_bias, v_a_log, v_d_skip, v_ssd_norm_w, v_post_norm_w, v_conv_w)
    small_out = [gs] + [_unpack(o, lsizes, nl) for o in adam_small(wp, _pack(gs), mp, vp)]

    def leaf(kind, name):
        if name in big:
            return big_out[name][kind].reshape(big[name][0].shape if name != "pool_mix_w" else pool_mix_w.shape)
        val = small_out[kind][name]
        return val.reshape(conv_w.shape) if name == "conv_w" else val

    order = ("pre_norm_w", "w_in", "pool_mix_w", "pool_scale", "conv_w", "conv_b", "dt_bias", "a_log", "d_skip",
             "ssd_norm_w", "w_out", "post_norm_w")
    return (loss, g[None]) + tuple(leaf(kind, name) for kind in range(4) for name in order)
```

```python
import jax
import jax.numpy as jnp
from jax import lax
from jax.experimental import pallas as pl
from jax.experimental.pallas import tpu as pltpu

F32 = jnp.float32
BF16 = jnp.bfloat16
SDS = jax.ShapeDtypeStruct
MESH = pl.DeviceIdType.MESH
HIGHEST = lax.Precision.HIGHEST

NORM_EPS = 1e-6
POOL_WINDOWS = (2, 4, 8, 16)
POOL_HALO = 16
CONV_WIDTH = 4
CONV_HALO = 8
SSD_CHUNK = 128
SSD_HEAD_DIM = 64
SSD_STATE = 128
SSD_GROUPS = 4
LANES = 128
N_DEV = 8

ADAM_LR = 0.001
ADAM_B1 = 0.9
ADAM_B2 = 0.999
ADAM_EPS = 1e-08
ADAM_WD = 0.01
ADAM_STEP = 10

VMEM_LIMIT = 56 * 1024 * 1024

NT = (((1,), (1,)), ((), ()))
TN = (((0,), (0,)), ((), ()))


def _params(sem=None):
    kw = dict(vmem_limit_bytes=VMEM_LIMIT)
    if sem is not None:
        kw["dimension_semantics"] = sem
    return pltpu.CompilerParams(**kw)


def _silu(v):
    return v * jax.nn.sigmoid(v)


def _dsilu(v):
    s = jax.nn.sigmoid(v)
    return s * (1.0 + v * (1.0 - s))


def _split_dot(v, sel):
    hi = v.astype(BF16)
    lo = (v - hi.astype(F32)).astype(BF16)
    return (jnp.dot(hi, sel, preferred_element_type=F32) + jnp.dot(lo, sel, preferred_element_type=F32))


def _head_selector(width, per):
    ch = lax.broadcasted_iota(jnp.int32, (width, LANES), 0)
    hd = lax.broadcasted_iota(jnp.int32, (width, LANES), 1)
    return jnp.where((ch >= hd * per) & (ch < (hd + 1) * per), 1.0, 0.0).astype(BF16)


class Comm:
    def __init__(self, inputs, out_shapes, aliases, n_sems, make):
        self.inputs, self.out_shapes, self.aliases, self.n_sems, self.make = list(inputs), list(out_shapes), dict(aliases), n_sems, make


def _remote(src, dst, send_sems, recv_sems, k, peer):
    return pltpu.make_async_remote_copy(src_ref=src, dst_ref=dst, send_sem=send_sems.at[k], recv_sem=recv_sems.at[k],
                                        device_id=peer, device_id_type=MESH)


class _SemRange:
    def __init__(self, sems, start):
        self.sems, self.start = sems, start

    @property
    def at(self):
        return self

    def __getitem__(self, k):
        return self.sems.at[self.start + k]


def merge_comms(comms):
    comms = [c for c in comms if c is not None]
    if len(comms) <= 1:
        return comms[0] if comms else None
    aliases, i_off, o_off = {}, 0, 0
    for c in comms:
        aliases.update({i_off + k: o_off + v for k, v in c.aliases.items()})
        i_off, o_off = i_off + len(c.inputs), o_off + len(c.out_shapes)

    def make(ins, outs, ss, rs):
        sends, locals_, arrivals, i0, o0, s0 = [], [], [], 0, 0, 0
        for c in comms:
            s, l, a = c.make(ins[i0:i0 + len(c.inputs)], outs[o0:o0 + len(c.out_shapes)], _SemRange(ss, s0), _SemRange(rs, s0))
            sends, locals_, arrivals = sends + s, locals_ + l, arrivals + a
            i0, o0, s0 = i0 + len(c.inputs), o0 + len(c.out_shapes), s0 + c.n_sems
        return sends, locals_, arrivals

    return Comm(sum((c.inputs for c in comms), []), sum((c.out_shapes for c in comms), []), aliases,
                sum(c.n_sems for c in comms), make)


def _call(body, args, *, name, grid, in_specs, out_specs, out_shape, scratch_shapes=(), sem=None, comm=None):
    in_specs, out_specs, out_shape = list(in_specs), list(out_specs), list(out_shape)
    if comm is None:
        outs = pl.pallas_call(body, name=name, grid=grid, in_specs=in_specs, out_specs=out_specs, out_shape=out_shape,
                              scratch_shapes=list(scratch_shapes), compiler_params=_params(sem))(*args)
        return list(outs), []
    ni, no, nci, nco, ns = len(in_specs), len(out_specs), len(comm.inputs), len(comm.out_shapes), len(scratch_shapes)
    hbm = pl.BlockSpec(memory_space=pl.ANY)

    def hosted(*refs):
        ins, cins = refs[:ni], refs[ni:ni + nci]
        outs, couts = refs[ni + nci:ni + nci + no], refs[ni + nci + no:ni + nci + no + nco]
        scratch = refs[ni + nci + no + nco:]
        sends, locals_, arrivals = comm.make(cins, couts, scratch[ns], scratch[ns + 1])
        first = last = None if grid else True
        for axis, extent in enumerate(grid):
            pid = pl.program_id(axis)
            first = (pid == 0) if first is None else first & (pid == 0)
            last = (pid == extent - 1) if last is None else last & (pid == extent - 1)

        @pl.when(first)
        def _():
            for cp in locals_ + sends:
                cp.start()

        body(*ins, *outs, *scratch[:ns])

        @pl.when(last)
        def _():
            for cp in arrivals:
                cp.wait_recv()
            for cp in sends:
                cp.wait_send()
            for cp in locals_:
                cp.wait()

    outs = pl.pallas_call(
        hosted, name=name, grid=grid, in_specs=in_specs + [hbm] * nci, out_specs=out_specs + [hbm] * nco,
        out_shape=out_shape + comm.out_shapes,
        scratch_shapes=list(scratch_shapes) + [pltpu.SemaphoreType.DMA((comm.n_sems,)), pltpu.SemaphoreType.DMA((comm.n_sems,))],
        input_output_aliases={ni + k: no + v for k, v in comm.aliases.items()},
        compiler_params=_params(sem),
    )(*args, *comm.inputs)
    return list(outs[:no]), list(outs[no:])


def rms_fwd(x, w, tm):
    t, d = x.shape

    def body(x_ref, w_ref, h_ref, r_ref):
        xv = x_ref[...]
        r = lax.rsqrt(jnp.mean(xv * xv, axis=-1, keepdims=True) + NORM_EPS)
        h_ref[...] = (xv * r * w_ref[...]).astype(BF16)
        r_ref[...] = r

    return pl.pallas_call(
        body, name="rms_fwd", grid=(t // tm,),
        in_specs=[pl.BlockSpec((tm, d), lambda i: (i, 0)), pl.BlockSpec((1, d), lambda i: (0, 0))],
        out_specs=[pl.BlockSpec((tm, d), lambda i: (i, 0)), pl.BlockSpec((tm, 1), lambda i: (i, 0))],
        out_shape=[SDS((t, d), BF16), SDS((t, 1), F32)],
        compiler_params=_params(("arbitrary",)),
    )(x, w)


def post_fwd(out, x, w, tm, comm=None):
    t, d = x.shape

    def body(o_ref, x_ref, w_ref, y_ref, r_ref):
        ov = o_ref[...]
        r = lax.rsqrt(jnp.mean(ov * ov, axis=-1, keepdims=True) + NORM_EPS)
        y_ref[...] = x_ref[...] + ov * r * w_ref[...]
        r_ref[...] = r

    return _call(
        body, (out, x, w), name="post_fwd", grid=(t // tm,),
        in_specs=[pl.BlockSpec((tm, d), lambda i: (i, 0)), pl.BlockSpec((tm, d), lambda i: (i, 0)),
                  pl.BlockSpec((1, d), lambda i: (0, 0))],
        out_specs=[pl.BlockSpec((tm, d), lambda i: (i, 0)), pl.BlockSpec((tm, 1), lambda i: (i, 0))],
        out_shape=[SDS((t, d), F32), SDS((t, 1), F32)], sem=("arbitrary",), comm=comm)


def _norm_bwd(g_n, n, r):
    return r * (g_n - n * jnp.mean(g_n * n, axis=-1, keepdims=True))


def post_bwd(g, out, r, w, tm):
    t, d = g.shape

    def body(g_ref, o_ref, r_ref, w_ref, do_ref, dw_ref):
        i = pl.program_id(0)
        gv = g_ref[...]
        rv = r_ref[...]
        n = o_ref[...] * rv
        part = jnp.sum(gv * n, axis=0, keepdims=True)

        @pl.when(i == 0)
        def _():
            dw_ref[...] = part

        @pl.when(i > 0)
        def _():
            dw_ref[...] += part

        do_ref[...] = _norm_bwd(gv * w_ref[...], n, rv).astype(BF16)

    return pl.pallas_call(
        body, name="post_bwd", grid=(t // tm,),
        in_specs=[pl.BlockSpec((tm, d), lambda i: (i, 0)), pl.BlockSpec((tm, d), lambda i: (i, 0)),
                  pl.BlockSpec((tm, 1), lambda i: (i, 0)), pl.BlockSpec((1, d), lambda i: (0, 0))],
        out_specs=[pl.BlockSpec((tm, d), lambda i: (i, 0)), pl.BlockSpec((1, d), lambda i: (0, 0))],
        out_shape=[SDS((t, d), BF16), SDS((1, d), F32)],
        compiler_params=_params(("arbitrary",)),
    )(g, out, r, w)


def rms_bwd(dh, x, r, w, g, tm):
    t, d = x.shape

    def body(a_ref, x_ref, r_ref, w_ref, g_ref, gx_ref, dw_ref):
        i = pl.program_id(0)
        dh = a_ref[...]
        rv = r_ref[...]
        n = x_ref[...] * rv
        part = jnp.sum(dh * n, axis=0, keepdims=True)

        @pl.when(i == 0)
        def _():
            dw_ref[...] = part

        @pl.when(i > 0)
        def _():
            dw_ref[...] += part

        gx_ref[...] = g_ref[...] + _norm_bwd(dh * w_ref[...], n, rv)

    row = pl.BlockSpec((tm, d), lambda i: (i, 0))
    return pl.pallas_call(
        body, name="rms_bwd", grid=(t // tm,),
        in_specs=[row, row, pl.BlockSpec((tm, 1), lambda i: (i, 0)), pl.BlockSpec((1, d), lambda i: (0, 0)), row],
        out_specs=[row, pl.BlockSpec((1, d), lambda i: (0, 0))],
        out_shape=[SDS((t, d), F32), SDS((1, d), F32)],
        compiler_params=_params(("arbitrary",)),
    )(dh, x, r, w, g)


def loss_grad(y, target, tm):
    t, d = y.shape

    def body(y_ref, t_ref, l_ref, g_ref):
        i = pl.program_id(0)
        err = y_ref[...] - t_ref[...]
        g_ref[...] = err / d
        part = 0.5 * jnp.sum(jnp.mean(err * err, axis=-1, keepdims=True), axis=0, keepdims=True)

        @pl.when(i == 0)
        def _():
            l_ref[...] = part

        @pl.when(i > 0)
        def _():
            l_ref[...] += part

    row = pl.BlockSpec((tm, d), lambda i: (i, 0))
    return pl.pallas_call(
        body, name="loss_grad", grid=(t // tm,), in_specs=[row, row],
        out_specs=[pl.BlockSpec((1, 1), lambda i: (0, 0)), row],
        out_shape=[SDS((1, 1), F32), SDS((t, d), F32)],
        compiler_params=_params(("arbitrary",)),
    )(y, target)


def mm_nn(a, b, out_dtype, tm, tn, name, comm=None):
    m, k = a.shape
    n = b.shape[1]

    def body(a_ref, b_ref, o_ref):
        o_ref[...] = jnp.dot(a_ref[...], b_ref[...], preferred_element_type=F32).astype(out_dtype)

    outs, couts = _call(
        body, (a, b), name=name, grid=(n // tn, m // tm),
        in_specs=[pl.BlockSpec((tm, k), lambda j, i: (i, 0)), pl.BlockSpec((k, tn), lambda j, i: (0, j))],
        out_specs=[pl.BlockSpec((tm, tn), lambda j, i: (i, j))],
        out_shape=[SDS((m, n), out_dtype)], sem=("arbitrary", "arbitrary"), comm=comm)
    return outs[0], couts


def mm_nt(a, b, out_dtype, tm, tn, tk, name, comm=None, extra=None):
    m, k = a.shape
    n = b.shape[0]
    nk = k // tk

    def body(a_ref, b_ref, *rest):
        o_ref, acc_ref = rest[-2:]
        kk = pl.program_id(2)
        part = lax.dot_general(a_ref[...], b_ref[...], NT, preferred_element_type=F32)
        if nk == 1:
            if extra is not None:
                part = part + lax.dot_general(rest[0][...], rest[1][...], NT, preferred_element_type=F32)
            o_ref[...] = part.astype(out_dtype)
        else:
            @pl.when(kk == 0)
            def _():
                if extra is None:
                    acc_ref[...] = part
                else:
                    acc_ref[...] = part + lax.dot_general(rest[0][...], rest[1][...], NT, preferred_element_type=F32)

            @pl.when(kk > 0)
            def _():
                acc_ref[...] += part

            @pl.when(kk == nk - 1)
            def _():
                o_ref[...] = acc_ref[...].astype(out_dtype)

    more_specs = [] if extra is None else [pl.BlockSpec((tm, extra[0].shape[1]), lambda i, j, kk: (i, 0)),
                                           pl.BlockSpec((tn, extra[1].shape[1]), lambda i, j, kk: (j, 0))]
    outs, couts = _call(
        body, (a, b) + tuple(extra or ()), name=name, grid=(m // tm, n // tn, nk),
        in_specs=[pl.BlockSpec((tm, tk), lambda i, j, kk: (i, kk)), pl.BlockSpec((tn, tk), lambda i, j, kk: (j, kk))] + more_specs,
        out_specs=[pl.BlockSpec((tm, tn), lambda i, j, kk: (i, j))],
        out_shape=[SDS((m, n), out_dtype)],
        scratch_shapes=[pltpu.VMEM((tm, tn) if nk > 1 else (8, LANES), F32)],
        sem=("arbitrary", "arbitrary", "arbitrary"), comm=comm)
    return outs[0], couts


def mm_tn(a, b, tm, tn, name, comm=None):
    t, m = a.shape
    n = b.shape[1]

    def body(a_ref, b_ref, o_ref):
        o_ref[...] = lax.dot_general(a_ref[...], b_ref[...], TN, preferred_element_type=F32)

    outs, couts = _call(
        body, (a, b), name=name, grid=(m // tm, n // tn),
        in_specs=[pl.BlockSpec((t, tm), lambda i, j: (0, i)), pl.BlockSpec((t, tn), lambda i, j: (0, j))],
        out_specs=[pl.BlockSpec((tm, tn), lambda i, j: (i, j))],
        out_shape=[SDS((m, n), F32)], sem=("arbitrary", "arbitrary"), comm=comm)
    return outs[0], couts


def _window_sums(ext, n_rows, lookahead):
    def sh(v, k):
        return pltpu.roll(v, (n_rows - k) if lookahead else k, 0)
    s2 = ext + sh(ext, 1)
    s4 = s2 + sh(s2, 2)
    s8 = s4 + sh(s4, 4)
    s16 = s8 + sh(s8, 8)
    return (s2, s4, s8, s16)


def _pool_counts(i, tm, w):
    tpos = i * tm + lax.broadcasted_iota(jnp.int32, (tm, 1), 0)
    return jnp.minimum(tpos + 1, w).astype(F32)


def _pooled(uc_ref, up_ref, i, tm):
    cur = uc_ref[...]
    prev = jnp.where(i > 0, up_ref[...], 0.0)
    ext = jnp.concatenate([prev, cur], axis=0)
    return cur, _window_sums(ext, tm + POOL_HALO, False)


def pool_fwd(proj, mixw, scale, tm):
    t = proj.shape[0]
    pw = scale.shape[1]
    gw = pw // len(POOL_WINDOWS)
    nh = tm // POOL_HALO

    def body(uc_ref, up_ref, g_ref, w_ref, s_ref, o_ref):
        i = pl.program_id(0)
        cur, sums = _pooled(uc_ref, up_ref, i, tm)
        for g, w in enumerate(POOL_WINDOWS):
            cols = slice(g * gw, (g + 1) * gw)
            pooled = sums[g][POOL_HALO:, cols] / _pool_counts(i, tm, w) - cur[:, cols]
            mixed = jnp.dot(pooled.astype(BF16), w_ref[g], preferred_element_type=F32)
            o_ref[:, cols] = (mixed * s_ref[:, cols] * _silu(g_ref[:, cols])).astype(BF16)

    return pl.pallas_call(
        body, name="pool_fwd", grid=(t // tm,),
        in_specs=[pl.BlockSpec((tm, pw), lambda i: (i, 0)),
                  pl.BlockSpec((POOL_HALO, pw), lambda i: (jnp.maximum(i * nh - 1, 0), 0)),
                  pl.BlockSpec((tm, pw), lambda i: (i, 1)),
                  pl.BlockSpec(mixw.shape, lambda i: (0, 0, 0)),
                  pl.BlockSpec((1, pw), lambda i: (0, 0))],
        out_specs=pl.BlockSpec((tm, pw), lambda i: (i, 0)),
        out_shape=SDS((t, 2 * pw), BF16),
        compiler_params=_params(("arbitrary",)),
    )(proj, proj, proj, mixw, scale)


def pool_bwd_a(dmixed, proj, mixw, scale, tm):
    t, e = proj.shape
    pw = scale.shape[1]
    ng = len(POOL_WINDOWS)
    gw = pw // ng
    nh = tm // POOL_HALO

    def body(dy_ref, uc_ref, up_ref, g_ref, w_ref, s_ref, dg_ref, dq_ref, ds_ref, dw_ref):
        i = pl.program_id(0)

        @pl.when(i == 0)
        def _():
            ds_ref[...] = jnp.zeros_like(ds_ref)
            dw_ref[...] = jnp.zeros_like(dw_ref)

        cur, sums = _pooled(uc_ref, up_ref, i, tm)
        for g, w in enumerate(POOL_WINDOWS):
            cols = slice(g * gw, (g + 1) * gw)
            cnt = _pool_counts(i, tm, w)
            pooled = (sums[g][POOL_HALO:, cols] / cnt - cur[:, cols]).astype(BF16)
            mixed = jnp.dot(pooled, w_ref[g], preferred_element_type=F32)
            gate = g_ref[:, cols]
            dy = dy_ref[:, cols]
            sc = s_ref[:, cols]
            dg_ref[:, cols] = (dy * mixed * sc * _dsilu(gate)).astype(BF16)
            ds = dy * _silu(gate)
            ds_ref[:, cols] += jnp.sum(ds * mixed, axis=0, keepdims=True)
            dmix = (ds * sc).astype(BF16)
            dw_ref[g] += lax.dot_general(pooled, dmix, TN, preferred_element_type=F32)
            dq_ref[:, cols] = lax.dot_general(dmix, w_ref[g], NT, preferred_element_type=F32) / cnt

    return pl.pallas_call(
        body, name="pool_bwd_a", grid=(t // tm,),
        in_specs=[pl.BlockSpec((tm, pw), lambda i: (i, 0)),
                  pl.BlockSpec((tm, pw), lambda i: (i, 0)),
                  pl.BlockSpec((POOL_HALO, pw), lambda i: (jnp.maximum(i * nh - 1, 0), 0)),
                  pl.BlockSpec((tm, pw), lambda i: (i, 1)),
                  pl.BlockSpec(mixw.shape, lambda i: (0, 0, 0)),
                  pl.BlockSpec((1, pw), lambda i: (0, 0))],
        out_specs=[pl.BlockSpec((tm, pw), lambda i: (i, 1)),
                   pl.BlockSpec((tm, pw), lambda i: (i, 0)),
                   pl.BlockSpec((1, pw), lambda i: (0, 0)),
                   pl.BlockSpec((ng, gw, gw), lambda i: (0, 0, 0))],
        out_shape=[SDS((t, e), BF16), SDS((t, pw), F32), SDS((1, pw), F32), SDS((ng, gw, gw), F32)],
        compiler_params=_params(("arbitrary",)),
    )(dmixed, proj, proj, proj, mixw, scale)


def pool_bwd_b(dq, dproj, tm):
    t, pw = dq.shape
    gw = pw // len(POOL_WINDOWS)
    nh = tm // POOL_HALO
    nt = t // tm

    def body(c_ref, n_ref, alias_ref, o_ref):
        i = pl.program_id(0)
        cur = c_ref[...]
        nxt = jnp.where(i < nt - 1, n_ref[...], 0.0)
        sums = _window_sums(jnp.concatenate([cur, nxt], axis=0), tm + POOL_HALO, True)
        for g, w in enumerate(POOL_WINDOWS):
            cols = slice(g * gw, (g + 1) * gw)
            o_ref[:, cols] = (sums[g][:tm, cols] - cur[:, cols] * _pool_counts(i, tm, w)).astype(BF16)

    return pl.pallas_call(
        body, name="pool_bwd_b", grid=(nt,),
        in_specs=[pl.BlockSpec((tm, pw), lambda i: (i, 0)),
                  pl.BlockSpec((POOL_HALO, pw), lambda i: (jnp.minimum((i + 1) * nh, t // POOL_HALO - 1), 0)),
                  pl.BlockSpec(memory_space=pl.ANY)],
        out_specs=pl.BlockSpec((tm, pw), lambda i: (i, 0)),
        out_shape=SDS(dproj.shape, dproj.dtype),
        input_output_aliases={2: 0},
        compiler_params=_params(("arbitrary",)),
    )(dq, dq, dproj)


def _conv_pre(xc_ref, xp_ref, w_ref, b_ref, i, tm):
    cur = xc_ref[...]
    prev = jnp.where(i > 0, xp_ref[...], 0.0)
    ext = jnp.concatenate([prev, cur], axis=0)
    taps = [pltpu.roll(ext, CONV_WIDTH - 1 - k, 0)[CONV_HALO:] for k in range(CONV_WIDTH - 1)] + [cur]
    pre = b_ref[...]
    for k in range(CONV_WIDTH):
        pre = pre + w_ref[k:k + 1, :] * taps[k]
    return pre, taps


def conv_fwd(proj, conv_w, conv_b, col_block, tm):
    t = proj.shape[0]
    cd = conv_b.shape[1]
    nh = tm // CONV_HALO

    def body(xc_ref, xp_ref, w_ref, b_ref, o_ref):
        i = pl.program_id(0)
        pre, _ = _conv_pre(xc_ref, xp_ref, w_ref, b_ref, i, tm)
        o_ref[...] = _silu(pre)

    return pl.pallas_call(
        body, name="conv_fwd", grid=(t // tm,),
        in_specs=[pl.BlockSpec((tm, cd), lambda i: (i, col_block)),
                  pl.BlockSpec((CONV_HALO, cd), lambda i: (jnp.maximum(i * nh - 1, 0), col_block)),
                  pl.BlockSpec((CONV_WIDTH, cd), lambda i: (0, 0)),
                  pl.BlockSpec((1, cd), lambda i: (0, 0))],
        out_specs=pl.BlockSpec((tm, cd), lambda i: (i, 0)),
        out_shape=SDS((t, cd), F32),
        compiler_params=_params(("arbitrary",)),
    )(proj, proj, conv_w, conv_b)


def conv_bwd_a(dxs, db, dc, proj, conv_w, conv_b, col_block, tm, comm=None):
    t = proj.shape[0]
    cd = conv_b.shape[1]
    sw = dxs.shape[1]
    gn = db.shape[1]
    nh = tm // CONV_HALO

    def body(dx_ref, db_ref, dc_ref, xc_ref, xp_ref, w_ref, b_ref, dp_ref, dw_ref, dbias_ref):
        i = pl.program_id(0)

        @pl.when(i == 0)
        def _():
            dw_ref[...] = jnp.zeros_like(dw_ref)
            dbias_ref[...] = jnp.zeros_like(dbias_ref)

        pre, taps = _conv_pre(xc_ref, xp_ref, w_ref, b_ref, i, tm)
        dact = jnp.concatenate([dx_ref[...], db_ref[...], dc_ref[...]], axis=1)
        dpre = dact * _dsilu(pre)
        dp_ref[...] = dpre
        dbias_ref[...] += jnp.sum(dpre, axis=0, keepdims=True)
        for k in range(CONV_WIDTH):
            dw_ref[k:k + 1, :] += jnp.sum(dpre * taps[k], axis=0, keepdims=True)

    return _call(
        body, (dxs, db, dc, proj, proj, conv_w, conv_b), name="conv_bwd_a", grid=(t // tm,),
        in_specs=[pl.BlockSpec((tm, sw), lambda i: (i, 0)), pl.BlockSpec((tm, gn), lambda i: (i, 0)),
                  pl.BlockSpec((tm, gn), lambda i: (i, 0)),
                  pl.BlockSpec((tm, cd), lambda i: (i, col_block)),
                  pl.BlockSpec((CONV_HALO, cd), lambda i: (jnp.maximum(i * nh - 1, 0), col_block)),
                  pl.BlockSpec((CONV_WIDTH, cd), lambda i: (0, 0)),
                  pl.BlockSpec((1, cd), lambda i: (0, 0))],
        out_specs=[pl.BlockSpec((tm, cd), lambda i: (i, 0)),
                   pl.BlockSpec((CONV_WIDTH, cd), lambda i: (0, 0)),
                   pl.BlockSpec((1, cd), lambda i: (0, 0))],
        out_shape=[SDS((t, cd), F32), SDS((CONV_WIDTH, cd), F32), SDS((1, cd), F32)],
        sem=("arbitrary",), comm=comm)


def conv_bwd_b(dpre, conv_w, dproj, col_block, tm):
    t, cd = dpre.shape
    nh = tm // CONV_HALO
    nt = t // tm

    def body(c_ref, n_ref, w_ref, alias_ref, o_ref):
        i = pl.program_id(0)
        cur = c_ref[...]
        nxt = jnp.where(i < nt - 1, n_ref[...], 0.0)
        ext = jnp.concatenate([cur, nxt], axis=0)
        n = tm + CONV_HALO
        acc = w_ref[CONV_WIDTH - 1:CONV_WIDTH, :] * cur
        for k in range(CONV_WIDTH - 1):
            acc = acc + w_ref[k:k + 1, :] * pltpu.roll(ext, n - (CONV_WIDTH - 1 - k), 0)[:tm]
        o_ref[...] = acc.astype(BF16)

    return pl.pallas_call(
        body, name="conv_bwd_b", grid=(nt,),
        in_specs=[pl.BlockSpec((tm, cd), lambda i: (i, 0)),
                  pl.BlockSpec((CONV_HALO, cd), lambda i: (jnp.minimum((i + 1) * nh, t // CONV_HALO - 1), 0)),
                  pl.BlockSpec((CONV_WIDTH, cd), lambda i: (0, 0)),
                  pl.BlockSpec(memory_space=pl.ANY)],
        out_specs=pl.BlockSpec((tm, cd), lambda i: (i, col_block)),
        out_shape=SDS(dproj.shape, dproj.dtype),
        input_output_aliases={3: 0},
        compiler_params=_params(("arbitrary",)),
    )(dpre, dpre, conv_w, dproj)


def _softplus(v):
    return jnp.maximum(v, 0.0) + jnp.log(1.0 + jnp.exp(-jnp.abs(v)))


def _ssd_chunk_terms(dtr_ref, bias_ref, a_ref, n_heads):
    q = SSD_CHUNK
    lane = lax.broadcasted_iota(jnp.int32, (1, LANES), 1)
    pre = dtr_ref[...] + bias_ref[...]
    dt = jnp.where(lane < n_heads, _softplus(pre), 0.0)
    a = jnp.where(lane < n_heads, -jnp.exp(a_ref[...]), 0.0)
    row = lax.broadcasted_iota(jnp.int32, (q, q), 0)
    col = lax.broadcasted_iota(jnp.int32, (q, q), 1)
    causal = row >= col
    acs = jnp.dot(causal.astype(F32), dt * a, precision=HIGHEST, preferred_element_type=F32)
    last = acs[q - 1:q, :]
    return dict(pre=pre, dt=dt, a=a, acs=acs, acs_t=acs.T, eacs=jnp.exp(acs), dstate=jnp.exp(last - acs),
                cdec=jnp.exp(last), causal=causal, diag=row == col, lane=lane)


def _pair_cols(lo, v, h):
    return jnp.where(lo, v[:, h:h + 1], v[:, h + 1:h + 2])


def _pair_decay(tm_, cb, h):
    l0 = jnp.exp(jnp.where(tm_["causal"], tm_["acs"][:, h:h + 1] - tm_["acs_t"][h:h + 1, :], -jnp.inf))
    l1 = jnp.exp(jnp.where(tm_["causal"], tm_["acs"][:, h + 1:h + 2] - tm_["acs_t"][h + 1:h + 2, :], -jnp.inf))
    return l0, l1, jnp.concatenate([cb * l0, cb * l1], axis=1)


def _pair_decay_t(tm_, cbt, h):
    upper = jnp.logical_not(tm_["causal"]) | tm_["diag"]
    t0 = jnp.exp(jnp.where(upper, tm_["acs_t"][h:h + 1, :] - tm_["acs"][:, h:h + 1], -jnp.inf))
    t1 = jnp.exp(jnp.where(upper, tm_["acs_t"][h + 1:h + 2, :] - tm_["acs"][:, h + 1:h + 2], -jnp.inf))
    return jnp.concatenate([cbt * t0, cbt * t1], axis=0).astype(BF16)


def _block_diag(lo, xdt):
    return jnp.concatenate([jnp.where(lo, xdt, 0.0), jnp.where(lo, 0.0, xdt)], axis=0).astype(BF16)


def ssd_fwd(xbc, dt_raw, dt_bias, a_log, n_heads, comm=None):
    t = xbc.shape[0]
    q = SSD_CHUNK
    gn = SSD_GROUPS * SSD_STATE
    sw = n_heads * SSD_HEAD_DIM
    n_pairs = n_heads // 2
    pairs_per_group = n_pairs // SSD_GROUPS
    nc = t // q
    bblk = sw // gn

    def body(xs_ref, b_ref, c_ref, dtr_ref, bias_ref, a_ref, y_ref, sin_ref, state):
        @pl.when(pl.program_id(0) == 0)
        def _():
            state[...] = jnp.zeros_like(state)

        tm_ = _ssd_chunk_terms(dtr_ref, bias_ref, a_ref, n_heads)
        lo = tm_["lane"] < SSD_HEAD_DIM
        for g in range(SSD_GROUPS):
            gcols = slice(g * SSD_STATE, (g + 1) * SSD_STATE)
            bg = b_ref[:, gcols].astype(BF16)
            bg_t = b_ref[:, gcols].T.astype(BF16)
            cg = c_ref[:, gcols].astype(BF16)
            cb = lax.dot_general(cg, bg, NT, preferred_element_type=F32)
            for j in range(pairs_per_group):
                p = g * pairs_per_group + j
                h = 2 * p
                pcols = slice(p * LANES, (p + 1) * LANES)
                _, _, mcat = _pair_decay(tm_, cb, h)
                xdt = xs_ref[:, pcols] * _pair_cols(lo, tm_["dt"], h)
                ydiag = jnp.dot(mcat.astype(BF16), _block_diag(lo, xdt), preferred_element_type=F32)
                st = state[p]
                sin_ref[0, p] = st
                yoff = jnp.dot(cg, st.astype(BF16), preferred_element_type=F32) * _pair_cols(lo, tm_["eacs"], h)
                y_ref[:, pcols] = ydiag + yoff
                xw = (xdt * _pair_cols(lo, tm_["dstate"], h)).astype(BF16)
                state[p] = st * _pair_cols(lo, tm_["cdec"], h) + jnp.dot(bg_t, xw, preferred_element_type=F32)

    vec = pl.BlockSpec((1, LANES), lambda c: (0, 0))
    return _call(
        body, (xbc, xbc, xbc, dt_raw, dt_bias, a_log), name="ssd_fwd", grid=(nc,),
        in_specs=[pl.BlockSpec((q, sw), lambda c: (c, 0)),
                  pl.BlockSpec((q, gn), lambda c: (c, bblk)),
                  pl.BlockSpec((q, gn), lambda c: (c, bblk + 1)),
                  pl.BlockSpec((q, LANES), lambda c: (c, 0)), vec, vec],
        out_specs=[pl.BlockSpec((q, sw), lambda c: (c, 0)),
                   pl.BlockSpec((1, n_pairs, SSD_STATE, LANES), lambda c: (c, 0, 0, 0))],
        out_shape=[SDS((t, sw), F32), SDS((nc, n_pairs, SSD_STATE, LANES), F32)],
        scratch_shapes=[pltpu.VMEM((n_pairs, SSD_STATE, LANES), F32)],
        sem=("arbitrary",), comm=comm)


def ssd_bwd(dy, xbc, dt_raw, dt_bias, a_log, d_full, s_in, n_heads, comm=None):
    t = xbc.shape[0]
    q = SSD_CHUNK
    gn = SSD_GROUPS * SSD_STATE
    sw = n_heads * SSD_HEAD_DIM
    n_pairs = n_heads // 2
    pairs_per_group = n_pairs // SSD_GROUPS
    nc = t // q
    bblk = sw // gn

    def body(dy_ref, xs_ref, b_ref, c_ref, dtr_ref, bias_ref, a_ref, dsk_ref, sin_ref,
             dxs_ref, db_ref, dc_ref, ddtr_ref, dbias_ref, dalog_ref,
             dstate, tbuf, xbuf, rbuf, acc_a, acc_b, sel_ref):
        i = pl.program_id(0)

        @pl.when(i == 0)
        def _():
            dstate[...] = jnp.zeros_like(dstate)
            rbuf[...] = jnp.zeros_like(rbuf)
            acc_a[...] = jnp.zeros_like(acc_a)
            acc_b[...] = jnp.zeros_like(acc_b)
            sel_ref[...] = _head_selector(sw, SSD_HEAD_DIM)

        tm_ = _ssd_chunk_terms(dtr_ref, bias_ref, a_ref, n_heads)
        lane = tm_["lane"]
        lo = lane < SSD_HEAD_DIM
        head_row = lax.broadcasted_iota(jnp.int32, (LANES, 1), 0)
        rows = jnp.zeros((q, LANES), F32)
        cols_t = jnp.zeros((LANES, q), F32)
        for g in range(SSD_GROUPS):
            gcols = slice(g * SSD_STATE, (g + 1) * SSD_STATE)
            bg = b_ref[:, gcols].astype(BF16)
            cg = c_ref[:, gcols].astype(BF16)
            cg_t = c_ref[:, gcols].T.astype(BF16)
            cb = lax.dot_general(cg, bg, NT, preferred_element_type=F32)
            cbt = lax.dot_general(bg, cg, NT, preferred_element_type=F32)
            dcb = jnp.zeros((q, q), F32)
            db_acc = jnp.zeros((q, SSD_STATE), F32)
            dc_acc = jnp.zeros((q, SSD_STATE), F32)
            for j in range(pairs_per_group):
                p = g * pairs_per_group + j
                h = 2 * p
                pcols = slice(p * LANES, (p + 1) * LANES)
                l0, l1, mcat = _pair_decay(tm_, cb, h)
                xp = xs_ref[:, pcols]
                dtp = _pair_cols(lo, tm_["dt"], h)
                xdt = xp * dtp
                xbd = _block_diag(lo, xdt)
                dyp = dy_ref[:, pcols]
                dyb = dyp.astype(BF16)
                dsb = _pair_cols(lo, tm_["dstate"], h)
                cdr = _pair_cols(lo, tm_["cdec"], h)
                eb = _pair_cols(lo, tm_["eacs"], h)
                st = sin_ref[0, p]
                stb = st.astype(BF16)
                dst = dstate[p]
                dstb = dst.astype(BF16)
                dye = (dyp * eb).astype(BF16)
                both = jnp.dot(_pair_decay_t(tm_, cbt, h), dyb, preferred_element_type=F32)
                dx_state = jnp.dot(bg, dstb, preferred_element_type=F32) * dsb
                dxdt = jnp.where(lo, both[:q], both[q:]) + dx_state
                dmcat = lax.dot_general(dyb, xbd, NT, preferred_element_type=F32)
                dcb = dcb + dmcat[:, :q] * l0 + dmcat[:, q:] * l1
                dseg = dmcat * mcat
                csum = jnp.sum(dseg, axis=0, keepdims=True)
                rows = (rows + jnp.where(lane == h, jnp.sum(dseg[:, :q], axis=1, keepdims=True), 0.0)
                        + jnp.where(lane == h + 1, jnp.sum(dseg[:, q:], axis=1, keepdims=True), 0.0))
                cols_t = (cols_t + jnp.where(head_row == h, csum[:, :q], 0.0)
                          + jnp.where(head_row == h + 1, csum[:, q:], 0.0))
                dc_acc = dc_acc + lax.dot_general(dye, stb, NT, preferred_element_type=F32)
                db_acc = db_acc + lax.dot_general((xdt * dsb).astype(BF16), dstb, NT, preferred_element_type=F32)
                yoff = jnp.dot(cg, stb, preferred_element_type=F32) * eb
                tbuf[:, pcols] = dyp * yoff - xdt * dx_state
                xbuf[:, pcols] = dxdt * xp
                rbuf[0:1, pcols] = (jnp.sum(xdt * dx_state, axis=0, keepdims=True)
                                    + cdr * jnp.sum(dst * st, axis=0, keepdims=True))
                dxs_ref[:, pcols] = dxdt * dtp + dyp * dsk_ref[:, pcols]
                dstate[p] = dst * cdr + jnp.dot(cg_t, dye, preferred_element_type=F32)
            dcbb = dcb.astype(BF16)
            dc_ref[:, gcols] = dc_acc + jnp.dot(dcbb, bg, preferred_element_type=F32)
            db_ref[:, gcols] = db_acc + lax.dot_general(dcbb, cg, TN, preferred_element_type=F32)

        sel = sel_ref[...]
        dacs = rows - cols_t.T + _split_dot(tbuf[...], sel)
        carry = _split_dot(rbuf[...], sel)[0:1]
        anti = jnp.logical_not(tm_["causal"]) | tm_["diag"]
        da = jnp.dot(anti.astype(F32), dacs, precision=HIGHEST, preferred_element_type=F32) + carry
        ddt = da * tm_["a"] + _split_dot(xbuf[...], sel)
        ddtr = jnp.where(tm_["lane"] < n_heads, ddt * jax.nn.sigmoid(tm_["pre"]), 0.0)
        ddtr_ref[...] = ddtr.astype(BF16)
        acc_b[...] += jnp.sum(ddtr, axis=0, keepdims=True)
        acc_a[...] += jnp.sum(da * tm_["dt"], axis=0, keepdims=True)

        @pl.when(i == nc - 1)
        def _():
            dbias_ref[...] = acc_b[...]
            dalog_ref[...] = acc_a[...] * tm_["a"]

    vec = pl.BlockSpec((1, LANES), lambda i: (0, 0))
    wide = pl.BlockSpec((q, sw), lambda i: (nc - 1 - i, 0))
    return _call(
        body, (dy, xbc, xbc, xbc, dt_raw, dt_bias, a_log, d_full, s_in), name="ssd_bwd", grid=(nc,),
        in_specs=[wide, wide,
                  pl.BlockSpec((q, gn), lambda i: (nc - 1 - i, bblk)),
                  pl.BlockSpec((q, gn), lambda i: (nc - 1 - i, bblk + 1)),
                  pl.BlockSpec((q, LANES), lambda i: (nc - 1 - i, 0)), vec, vec,
                  pl.BlockSpec((1, sw), lambda i: (0, 0)),
                  pl.BlockSpec((1, n_pairs, SSD_STATE, LANES), lambda i: (nc - 1 - i, 0, 0, 0))],
        out_specs=[wide, pl.BlockSpec((q, gn), lambda i: (nc - 1 - i, 0)), pl.BlockSpec((q, gn), lambda i: (nc - 1 - i, 0)),
                   pl.BlockSpec((q, LANES), lambda i: (nc - 1 - i, 0)), vec, vec],
        out_shape=[SDS((t, sw), F32), SDS((t, gn), F32), SDS((t, gn), F32), SDS((t, LANES), BF16),
                   SDS((1, LANES), F32), SDS((1, LANES), F32)],
        scratch_shapes=[pltpu.VMEM((n_pairs, SSD_STATE, LANES), F32), pltpu.VMEM((q, sw), F32), pltpu.VMEM((q, sw), F32),
                        pltpu.VMEM((8, sw), F32), pltpu.VMEM((1, LANES), F32), pltpu.VMEM((1, LANES), F32),
                        pltpu.VMEM((sw, LANES), BF16)],
        sem=("arbitrary",), comm=comm)


def _gated(y_ref, xs_ref, z_ref, dsk_ref):
    y1 = y_ref[...] + dsk_ref[...] * xs_ref[...]
    return y1, y1 * _silu(z_ref[...])


def gate_norm_fwd(y, xbc, proj, d_full, norm_w, mixed, z_block, tm):
    t, sw = y.shape
    gw = sw // SSD_GROUPS

    def body(y_ref, xs_ref, z_ref, dsk_ref, nw_ref, alias_ref, o_ref):
        _, y2 = _gated(y_ref, xs_ref, z_ref, dsk_ref)
        for g in range(SSD_GROUPS):
            cols = slice(g * gw, (g + 1) * gw)
            blk = y2[:, cols]
            r = lax.rsqrt(jnp.mean(blk * blk, axis=-1, keepdims=True) + NORM_EPS)
            o_ref[:, cols] = (blk * r * nw_ref[:, cols]).astype(BF16)

    row = pl.BlockSpec((tm, sw), lambda i: (i, 0))
    vec = pl.BlockSpec((1, sw), lambda i: (0, 0))
    return pl.pallas_call(
        body, name="gate_norm_fwd", grid=(t // tm,),
        in_specs=[row, row, pl.BlockSpec((tm, sw), lambda i: (i, z_block)), vec, vec, pl.BlockSpec(memory_space=pl.ANY)],
        out_specs=pl.BlockSpec((tm, sw), lambda i: (i, 1)),
        out_shape=SDS(mixed.shape, mixed.dtype),
        input_output_aliases={5: 0},
        compiler_params=_params(("arbitrary",)),
    )(y, xbc, proj, d_full, norm_w, mixed)


def gate_norm_bwd(dmixed, y, xbc, proj, d_full, norm_w, dproj, z_block, tm):
    t, sw = y.shape
    gw = sw // SSD_GROUPS
    nt = t // tm

    def body(d_ref, y_ref, xs_ref, z_ref, dsk_ref, nw_ref, alias_ref, dy_ref, dz_ref, dnw_ref, dd_ref, acc_d):
        i = pl.program_id(0)

        @pl.when(i == 0)
        def _():
            dnw_ref[...] = jnp.zeros_like(dnw_ref)
            acc_d[...] = jnp.zeros_like(acc_d)

        y1, y2 = _gated(y_ref, xs_ref, z_ref, dsk_ref)
        d3 = d_ref[...]
        parts = []
        for g in range(SSD_GROUPS):
            cols = slice(g * gw, (g + 1) * gw)
            blk = y2[:, cols]
            r = lax.rsqrt(jnp.mean(blk * blk, axis=-1, keepdims=True) + NORM_EPS)
            n = blk * r
            dg = d3[:, cols]
            dnw_ref[:, cols] += jnp.sum(dg * n, axis=0, keepdims=True)
            parts.append(_norm_bwd(dg * nw_ref[:, cols], n, r))
        dy2 = jnp.concatenate(parts, axis=1)
        zv = z_ref[...]
        dz_ref[...] = (dy2 * y1 * _dsilu(zv)).astype(BF16)
        dy1 = dy2 * _silu(zv)
        dy_ref[...] = dy1
        acc_d[0:1, :] += jnp.sum(dy1 * xs_ref[...], axis=0, keepdims=True)

        @pl.when(i == nt - 1)
        def _():
            dd_ref[...] = _split_dot(acc_d[...], _head_selector(sw, SSD_HEAD_DIM))[0:1]

    row = pl.BlockSpec((tm, sw), lambda i: (i, 0))
    vec = pl.BlockSpec((1, sw), lambda i: (0, 0))
    return pl.pallas_call(
        body, name="gate_norm_bwd", grid=(nt,),
        in_specs=[pl.BlockSpec((tm, sw), lambda i: (i, 1)), row, row, pl.BlockSpec((tm, sw), lambda i: (i, z_block)),
                  vec, vec, pl.BlockSpec(memory_space=pl.ANY)],
        out_specs=[row, pl.BlockSpec((tm, sw), lambda i: (i, z_block)), vec, pl.BlockSpec((1, LANES), lambda i: (0, 0))],
        out_shape=[SDS((t, sw), F32), SDS(dproj.shape, dproj.dtype), SDS((1, sw), F32), SDS((1, LANES), F32)],
        scratch_shapes=[pltpu.VMEM((8, sw), F32)],
        input_output_aliases={6: 1},
        compiler_params=_params(("arbitrary",)),
    )(dmixed, y, xbc, proj, d_full, norm_w, dproj)


GATE_BLOCK = 1
Z_BLOCK = 2
CONV_BLOCK = 2


def _tiles(t):
    mm = dict(in_proj=(min(1024, t), 1024), dt_proj=(min(512, t), LANES), out_proj=(min(512, t), 1024),
              d_mixed=(min(512, t), 2048), dh=(min(512, t), 3072), dw_out=(512, 1024), dw_main=(1024, 1024),
              dw_dt=(512, LANES))
    return min(256, t), mm


def _place():
    x, y, c = lax.axis_index("x"), lax.axis_index("y"), lax.axis_index("c")
    return x, y, c, [(1 - x, y), (x, 1 - y), (1 - x, 1 - y)]


def gather_spread(shards, layer, rows=None, carry=None):
    n = len(shards)

    def make(ins, outs, ss, rs):
        x, y, c, chips = _place()
        mine = 4 * x + 2 * y + c
        peers = [(x, y, 1 - c)] + [(px, py, c) for px, py in chips]
        sends, locals_, arrivals = [], [], []
        for a in range(n):
            def place(ref, idx):
                return ref.at[idx] if rows is None else ref.at[idx, pl.ds(rows[0], rows[1])]

            src = place(ins[a], layer)
            locals_.append(pltpu.make_async_copy(src, place(outs[a], mine), ss.at[5 * a + 4]))
            for j, (px, py, pc) in enumerate(peers):
                sends.append(_remote(src, place(outs[a], mine), ss, rs, 5 * a + j, (px, py, pc)))
                arrivals.append(_remote(src, place(outs[a], 4 * px + 2 * py + pc), ss, rs, 5 * a + j, (px, py, pc)))
        return sends, locals_, arrivals

    return Comm(list(shards) + list(carry or []), [SDS((N_DEV,) + s.shape[1:], s.dtype) for s in shards],
                {n + a: a for a in range(n)} if carry else {}, 5 * n, make)


def gather_pass_on(gathered):
    def make(ins, outs, ss, rs):
        x, y, c, chips = _place()
        sends, arrivals = [], []
        for a in range(len(outs)):
            for j, (px, py) in enumerate(chips):
                blk, other = 4 * px + 2 * py + c, 4 * px + 2 * py + (1 - c)
                sends.append(_remote(outs[a].at[blk], outs[a].at[blk], ss, rs, 3 * a + j, (x, y, 1 - c)))
                arrivals.append(_remote(outs[a].at[other], outs[a].at[other], ss, rs, 3 * a + j, (x, y, 1 - c)))
        return sends, [], arrivals

    return Comm(gathered, [SDS(g.shape, g.dtype) for g in gathered], {a: a for a in range(len(gathered))},
                3 * len(gathered), make)


def sibling_swap(sends_):
    def make(ins, outs, ss, rs):
        x, y, c, _ = _place()
        cps = [_remote(ins[a], outs[a], ss, rs, a, (x, y, 1 - c)) for a in range(len(ins))]
        return cps, [], cps

    return Comm(sends_, [SDS(s.shape, s.dtype) for s in sends_], {}, len(sends_), make)


def chips_scatter(slabs):
    def make(ins, outs, ss, rs):
        x, y, c, chips = _place()
        mychip = 2 * x + y
        sends, arrivals = [], []
        for a in range(len(ins)):
            for j, (px, py) in enumerate(chips):
                k = 2 * px + py
                sends.append(_remote(ins[a].at[k], outs[a].at[lax.rem(mychip - k + 4, 4) - 1], ss, rs, 3 * a + j, (px, py, c)))
                arrivals.append(_remote(ins[a].at[k], outs[a].at[lax.rem(k - mychip + 4, 4) - 1], ss, rs, 3 * a + j, (px, py, c)))
        return sends, [], arrivals

    return Comm(slabs, [SDS((3,) + s.shape[1:], s.dtype) for s in slabs], {}, 3 * len(slabs), make)


def comm_only(comm, name):
    def body():
        pass

    return _call(body, (), name=name, grid=(), in_specs=[], out_specs=[], out_shape=[], comm=comm)[1]


def layer_fwd(x, p, nxt=None, late=None):
    t = x.shape[0]
    tm, mm = _tiles(t)
    n_heads = p["d_full"].shape[1] // SSD_HEAD_DIM
    half = nxt[0][0].shape[1] // 2 if nxt else 0
    h, r_pre = rms_fwd(x, p["pre_w"], tm)
    proj, got = mm_nn(h, p["w_main"], F32, *mm["in_proj"], "in_proj", merge_comms([
        gather_spread(late[0], late[1]) if late else None,
        gather_spread(nxt[0], nxt[2], rows=(0, half)) if nxt else None]))
    n_late = len(late[0]) if late else 0
    dt_raw, got_late = mm_nn(h, p["w_dt"], F32, *mm["dt_proj"], "dt_proj", gather_pass_on(got[:n_late]) if late else None)
    if late:
        p = dict(p, **late[2](got_late))
    mixed = pool_fwd(proj, p["mixw"], p["pscale"], tm)
    xbc = conv_fwd(proj, p["conv_w"], p["conv_b"], CONV_BLOCK, tm)
    (y, s_in), got_b = ssd_fwd(xbc, dt_raw, p["dt_bias"], p["a_log"], n_heads, gather_spread(nxt[1], nxt[2]) if nxt else None)
    mixed = gate_norm_fwd(y, xbc, proj, p["d_full"], p["norm_w"], mixed, Z_BLOCK, tm)
    out, got_a = mm_nn(mixed, p["w_out"], F32, *mm["out_proj"], "out_proj",
                       gather_spread(nxt[0], nxt[2], rows=(half, half), carry=got[n_late:]) if nxt else None)
    (x_next, r_post), gathered = post_fwd(out, x, p["post_w"], tm, gather_pass_on(got_a + got_b) if nxt else None)
    return x_next, dict(x=x, h=h, r_pre=r_pre, proj=proj, dt_raw=dt_raw, xbc=xbc, y=y, s_in=s_in, mixed=mixed,
                        out=out, r_post=r_post), gathered, p


def _pair_sums(own, got):
    return [pair_sum(o, r, min(256, o.shape[1]), "pair_sum") for o, r in zip(own, got)]


def layer_bwd(g, s, p, split_in, split_rest, pending=None, last=False):
    t = g.shape[0]
    tm, mm = _tiles(t)
    d = g.shape[1]
    n_heads = p["d_full"].shape[1] // SSD_HEAD_DIM
    d_out, d_post = post_bwd(g, s["out"], s["r_post"], p["post_w"], tm)
    dmixed, got_sib = mm_nt(d_out, p["w_out"], F32, *mm["d_mixed"], d, "d_mixed", sibling_swap(pending[1]) if pending else None)
    chip_sums = _pair_sums(pending[0], got_sib) if pending else []
    dw_out, _ = mm_tn(s["mixed"], d_out, *mm["dw_out"], "dw_out")
    dproj, dq, d_pscale, d_mixw = pool_bwd_a(dmixed, s["proj"], p["mixw"], p["pscale"], tm)
    dproj = pool_bwd_b(dq, dproj, tm)
    own_rest, send_rest = split_rest(dw_out, d_mixw)
    dy, dproj, d_norm, d_dskip = gate_norm_bwd(dmixed, s["y"], s["xbc"], s["proj"], p["d_full"], p["norm_w"], dproj,
                                               Z_BLOCK, tm)
    (dxs, db, dc, ddtr, d_dtb, d_alog), got = ssd_bwd(
        dy, s["xbc"], s["dt_raw"], p["dt_bias"], p["a_log"], p["d_full"], s["s_in"], n_heads,
        merge_comms([chips_scatter(chip_sums[:1]) if pending else None, sibling_swap(send_rest) if last else None]))
    got_first, my_sib_rest = (got[:1], got[1:]) if pending else ([], got)
    (dpre, d_convw, d_convb), got_rest = conv_bwd_a(dxs, db, dc, s["proj"], p["conv_w"], p["conv_b"], CONV_BLOCK, tm,
                                                    chips_scatter(chip_sums[1:]) if pending else None)
    dproj = conv_bwd_b(dpre, p["conv_w"], dproj, CONV_BLOCK, tm)
    dw_main, my_chips_rest = mm_tn(s["h"], dproj, *mm["dw_main"], "dw_main",
                                   chips_scatter(_pair_sums(own_rest, my_sib_rest)) if last else None)
    dw_dt, _ = mm_tn(s["h"], ddtr, *mm["dw_dt"], "dw_dt")
    own_in, send_in = split_in(dw_main, dw_dt)
    my_sib_in = comm_only(sibling_swap(send_in), "grads_to_sibling") if last else []
    dh, my_chips_in = mm_nt(dproj, p["w_main"], F32, mm["dh"][0], d, mm["dh"][1], "dh_main",
                            chips_scatter(_pair_sums(own_in, my_sib_in)) if last else None, extra=(ddtr, p["w_dt"]))
    gx, d_pre = rms_bwd(dh, s["x"], s["r_pre"], p["pre_w"], g, tm)
    small = dict(pre_w=d_pre, pscale=d_pscale, conv_w=d_convw, conv_b=d_convb, dt_bias=d_dtb, a_log=d_alog,
                 d_skip=d_dskip, norm_w=d_norm, post_w=d_post)
    done = [(got_sib, got_first + got_rest)] if pending else [None]
    if last:
        done.append((my_sib_in + my_sib_rest, my_chips_in + my_chips_rest))
    return gx, small, (own_in + own_rest, send_in + send_rest), done


def _two_level_gather(x_refs, out_slots, send_sems, recv_sems, local_sems):
    x, y, c, chips = _place()
    me, sibling = (x, y, c), (x, y, 1 - c)
    n = len(x_refs)

    def copy(a, k, block, to, src=None):
        return pltpu.make_async_remote_copy(
            src_ref=out_slots[a](*block) if src is None else src, dst_ref=out_slots[a](*block),
            send_sem=send_sems.at[7 * a + k], recv_sem=recv_sems.at[7 * a + k], device_id=to, device_id_type=MESH)

    mine = [pltpu.make_async_copy(x_refs[a], out_slots[a](*me), local_sems.at[a]) for a in range(n)]
    for cp in mine:
        cp.start()
    first = []
    for a in range(n):
        first.append(copy(a, 0, me, sibling, src=x_refs[a]))
        first += [copy(a, 1 + j, me, (*chip, c), src=x_refs[a]) for j, chip in enumerate(chips)]
    for cp in first:
        cp.start()
    passed = []
    for j, chip in enumerate(chips):
        for a in range(n):
            copy(a, 1 + j, (*chip, c), me).wait_recv()
            fwd = copy(a, 4 + j, (*chip, c), sibling)
            fwd.start()
            passed.append(fwd)
    for a in range(n):
        copy(a, 0, sibling, me).wait_recv()
        for j, chip in enumerate(chips):
            copy(a, 4 + j, (*chip, 1 - c), me).wait_recv()
    for cp in first + passed:
        cp.wait_send()
    for cp in mine:
        cp.wait()


def all_gather_hbm(shards, name):
    n = len(shards)

    def body(*refs):
        x_refs, out_refs = refs[:n], refs[n:2 * n]
        send_sems, recv_sems, local_sems = refs[2 * n:]
        slots = [lambda px, py, pc, o=o: o.at[:, 4 * px + 2 * py + pc] for o in out_refs]
        _two_level_gather(x_refs, slots, send_sems, recv_sems, local_sems)

    hbm = pl.BlockSpec(memory_space=pl.ANY)
    return pl.pallas_call(
        body, name=name,
        out_shape=[SDS((s.shape[0], N_DEV) + s.shape[1:], s.dtype) for s in shards],
        in_specs=[hbm] * n, out_specs=[hbm] * n,
        scratch_shapes=[pltpu.SemaphoreType.DMA((7 * n,)), pltpu.SemaphoreType.DMA((7 * n,)), pltpu.SemaphoreType.DMA((n,))],
    )(*shards)


def all_gather_vmem(block, name):
    r, c_ = block.shape

    def body(x_ref, out_ref, send_sems, recv_sems, local_sems):
        _two_level_gather([x_ref], [lambda px, py, pc: out_ref.at[4 * px + 2 * py + pc]], send_sems, recv_sems, local_sems)

    return pl.pallas_call(
        body, name=name, out_shape=SDS((N_DEV, r, c_), block.dtype),
        in_specs=[pl.BlockSpec(memory_space=pltpu.VMEM)], out_specs=pl.BlockSpec(memory_space=pltpu.VMEM),
        scratch_shapes=[pltpu.SemaphoreType.DMA((7,)), pltpu.SemaphoreType.DMA((7,)), pltpu.SemaphoreType.DMA((1,))],
        compiler_params=_params(),
    )(block)


def _block_tiles(cols):
    base = [(cols * i) // LANES for i in range(N_DEV)]
    ends = [-((-cols * (i + 1)) // LANES) for i in range(N_DEV)]
    return base, ends, max(e - b for b, e in zip(base, ends))


def _my_lane_offset(cols):
    me = 4 * lax.axis_index("x") + 2 * lax.axis_index("y") + lax.axis_index("c")
    return lax.rem(cols * me, LANES)


def shift_cast(w, tr):
    nl, r, cols = w.shape
    width = _block_tiles(cols)[2] * LANES

    def body(x_ref, o_ref, pad):
        pad[:, width - LANES:] = jnp.zeros((tr, LANES), F32)
        pad[:, :cols] = x_ref[...]
        o_ref[...] = pltpu.roll(pad[...], _my_lane_offset(cols), 1).astype(BF16)

    assert width - LANES <= cols
    return pl.pallas_call(
        body, name="shift_cast", grid=(nl, r // tr),
        in_specs=[pl.BlockSpec((pl.Squeezed(), tr, cols), lambda l, i: (l, i, 0))],
        out_specs=pl.BlockSpec((pl.Squeezed(), tr, width), lambda l, i: (l, i, 0)),
        out_shape=SDS((nl, r, width), BF16), scratch_shapes=[pltpu.VMEM((tr, width), F32)],
        compiler_params=_params(("arbitrary", "arbitrary")))(w)


def assemble_w_in(blocks, cols, n_tail, tr):
    _, r, width = blocks.shape
    base, ends, _ = _block_tiles(cols)
    total = ends[-1]
    main_tiles = (N_DEV * cols - n_tail) // LANES
    assert main_tiles == total - 1 and (N_DEV * cols - n_tail) % LANES == 0

    def body(b_ref, main_ref, tail_ref):
        for tile in range(total):
            parts = [b_ref[i, :, (tile - base[i]) * LANES:(tile - base[i] + 1) * LANES]
                     for i in range(N_DEV) if base[i] <= tile < ends[i]]
            val = parts[0] if len(parts) == 1 else parts[0] + parts[1]
            if tile < main_tiles:
                main_ref[:, tile * LANES:(tile + 1) * LANES] = val
            else:
                tail_ref[...] = val

    return pl.pallas_call(
        body, name="assemble_w_in", grid=(r // tr,),
        in_specs=[pl.BlockSpec((N_DEV, tr, width), lambda i: (0, i, 0))],
        out_specs=[pl.BlockSpec((tr, main_tiles * LANES), lambda i: (i, 0)), pl.BlockSpec((tr, LANES), lambda i: (i, 0))],
        out_shape=[SDS((r, main_tiles * LANES), blocks.dtype), SDS((r, LANES), blocks.dtype)],
        compiler_params=_params(("arbitrary",)),
    )(blocks)


def grad_blocks(dw_main, dw_tail, cols, tr):
    r = dw_main.shape[0]
    base, _, tpb = _block_tiles(cols)
    width = tpb * LANES

    def body(m_ref, t_ref, own_ref, send_ref):
        cat = jnp.concatenate([m_ref[...], t_ref[...]], axis=1)
        south = lax.axis_index("c") == 0
        for k in range(N_DEV // 2):
            a = cat[:, base[2 * k] * LANES:base[2 * k] * LANES + width]
            b = cat[:, base[2 * k + 1] * LANES:base[2 * k + 1] * LANES + width]
            own_ref[k] = jnp.where(south, a, b)
            send_ref[k] = jnp.where(south, b, a).astype(BF16)

    return pl.pallas_call(
        body, name="grad_blocks", grid=(r // tr,),
        in_specs=[pl.BlockSpec((tr, dw_main.shape[1]), lambda i: (i, 0)), pl.BlockSpec((tr, LANES), lambda i: (i, 0))],
        out_specs=[pl.BlockSpec((N_DEV // 2, tr, width), lambda i: (0, i, 0))] * 2,
        out_shape=[SDS((N_DEV // 2, r, width), F32), SDS((N_DEV // 2, r, width), BF16)],
        compiler_params=_params(("arbitrary",)),
    )(dw_main, dw_tail)


def _adamw(w, g, m, v):
    m = ADAM_B1 * m + (1.0 - ADAM_B1) * g
    v = ADAM_B2 * v + (1.0 - ADAM_B2) * jnp.square(g)
    m_hat = m / (1.0 - ADAM_B1 ** ADAM_STEP)
    v_hat = v / (1.0 - ADAM_B2 ** ADAM_STEP)
    delta = -ADAM_LR * (m_hat / (jnp.sqrt(v_hat) + ADAM_EPS) + ADAM_WD * w)
    return delta, m, v


def pair_sum(own, got, tr, name):
    k, r, c_ = own.shape

    def body(a_ref, b_ref, o_ref):
        o_ref[...] = (a_ref[...] + b_ref[...].astype(F32)).astype(BF16)

    blk = pl.BlockSpec((pl.Squeezed(), tr, c_), lambda kk, i: (kk, i, 0))
    return pl.pallas_call(
        body, name=name, grid=(k, r // tr), in_specs=[blk, blk], out_specs=blk, out_shape=SDS(own.shape, BF16),
        compiler_params=_params(("arbitrary", "arbitrary")),
    )(own, got)


def reduce_adam(own, got_sibling, got_chips, w, m, v, prev, layer, tr, name, shifted=False):
    nl, r, cols = w.shape
    c_ = own.shape[-1]
    n_scratch = 1 if shifted else 0

    def body(own_ref, sib_ref, c0_ref, c1_ref, c2_ref, w_ref, m_ref, v_ref, *rest):
        g_ref, d_ref, nm_ref, nv_ref = rest[len(rest) - n_scratch - 4:len(rest) - n_scratch]
        g = (own_ref[...] + sib_ref[...].astype(F32) + c0_ref[...].astype(F32) + c1_ref[...].astype(F32)
             + c2_ref[...].astype(F32))
        if shifted:
            rest[-1][...] = pltpu.roll(g, c_ - _my_lane_offset(cols), 1)
            g = rest[-1][:, :cols]
        delta, nm, nv = _adamw(w_ref[...], g, m_ref[...], v_ref[...])
        g_ref[...] = g
        d_ref[...] = delta
        nm_ref[...] = nm
        nv_ref[...] = nv

    row = pl.BlockSpec((tr, c_), lambda i: (i, 0))
    lay = pl.BlockSpec((pl.Squeezed(), tr, cols), lambda i: (layer, i, 0))
    chips = [pl.BlockSpec((pl.Squeezed(), tr, c_), lambda i, s=s: (s, i, 0)) for s in range(3)]
    in_specs = [row, row] + chips + [lay, lay, lay]
    args = [own, got_sibling, got_chips, got_chips, got_chips, w, m, v]
    aliases = {}
    if prev is not None:
        in_specs += [pl.BlockSpec(memory_space=pl.ANY)] * 4
        aliases = {len(args) + k: k for k in range(4)}
        args += list(prev)
    return pl.pallas_call(
        body, name=name, grid=(r // tr,), in_specs=in_specs, out_specs=[lay] * 4,
        out_shape=[SDS((nl, r, cols), F32)] * 4, input_output_aliases=aliases,
        scratch_shapes=[pltpu.VMEM((tr, c_), F32)] * n_scratch,
        compiler_params=_params(("arbitrary",)),
    )(*args)


def sum_devices(packs):
    n, r, c_ = packs.shape

    def body(p_ref, o_ref):
        acc = p_ref[0]
        for k in range(1, n):
            acc = acc + p_ref[k]
        o_ref[...] = acc

    return pl.pallas_call(body, name="sum_devices", out_shape=SDS((r, c_), F32), compiler_params=_params())(packs)


def adam_small(w, g, m, v):
    def body(w_ref, g_ref, m_ref, v_ref, d_ref, nm_ref, nv_ref):
        delta, nm, nv = _adamw(w_ref[...], g_ref[...], m_ref[...], v_ref[...])
        d_ref[...] = delta
        nm_ref[...] = nm
        nv_ref[...] = nv

    return pl.pallas_call(body, name="adam_small", out_shape=[SDS(w.shape, F32)] * 3, compiler_params=_params())(w, g, m, v)


SMALL = ("pre_norm_w", "pool_scale", "conv_b", "dt_bias", "a_log", "d_skip", "_pad", "ssd_norm_w", "post_norm_w", "conv_w")


def _pack(parts):
    flat = jnp.concatenate([parts[k] for k in SMALL], axis=1).reshape(-1, LANES)
    return jnp.pad(flat, ((0, (-flat.shape[0]) % 8), (0, 0)))


def _unpack(pack, sizes, nl):
    total = sum(sizes[k] for k in SMALL)
    flat = pack[: nl * total // LANES].reshape(nl, total)
    out, o = {}, 0
    for k in SMALL:
        out[k] = flat[:, o:o + sizes[k]]
        o += sizes[k]
    return out


def kernel(x, pre_norm_w, w_in, pool_mix_w, pool_scale, conv_w, conv_b, dt_bias, a_log, d_skip, ssd_norm_w, w_out, post_norm_w, loss_target, m_pre_norm_w, m_w_in, m_pool_mix_w, m_pool_scale, m_conv_w, m_conv_b, m_dt_bias, m_a_log, m_d_skip, m_ssd_norm_w, m_w_out, m_post_norm_w, v_pre_norm_w, v_w_in, v_pool_mix_w, v_pool_scale, v_conv_w, v_conv_b, v_dt_bias, v_a_log, v_d_skip, v_ssd_norm_w, v_w_out, v_post_norm_w):
    cx, cy, cc = lax.axis_index("x"), lax.axis_index("y"), lax.axis_index("c")
    me = 4 * cx + 2 * cy + cc
    mychip = 2 * cx + cy
    nl, d, cols = w_in.shape
    t = x.shape[1]
    n_heads = a_log.shape[1]
    sw = n_heads * SSD_HEAD_DIM
    pw = pool_scale.shape[1]
    cd = conv_b.shape[1]
    ng, gsh, gw = pool_mix_w.shape[1:]
    e_main = N_DEV * cols - n_heads
    assert x.shape[0] == 1 and pw == sw and cd == sw + 2 * SSD_GROUPS * SSD_STATE and e_main == 2 * pw + sw + cd
    assert 2 * pw + sw == CONV_BLOCK * cd and n_heads <= LANES and t % SSD_CHUNK == 0 and gsh * N_DEV == gw
    tm, _ = _tiles(t)

    shards_a, shards_b = [shift_cast(w_in, tm)], [w_out.astype(BF16), pool_mix_w.astype(BF16), conv_w]
    pad_h = ((0, 0), (0, LANES - n_heads))

    def params_a(l, g_in):
        w_main, w_dt = assemble_w_in(g_in, cols, n_heads, tm)
        return dict(pre_w=pre_norm_w[l:l + 1], w_main=w_main, w_dt=w_dt, pscale=pool_scale[l:l + 1], conv_b=conv_b[l:l + 1],
                    dt_bias=jnp.pad(dt_bias[l:l + 1], pad_h), a_log=jnp.pad(a_log[l:l + 1], pad_h),
                    d_full=jnp.repeat(d_skip[l:l + 1], SSD_HEAD_DIM, axis=1), norm_w=ssd_norm_w[l:l + 1],
                    post_w=post_norm_w[l:l + 1])

    def params_b(g_out, g_mix, g_conv):
        return dict(mixw=g_mix.transpose(1, 0, 2, 3).reshape(ng, gw, gw), conv_w=g_conv.transpose(1, 0, 2).reshape(CONV_WIDTH, cd),
                    w_out=g_out.reshape(N_DEV * w_out.shape[1], d))

    xs = x[0]
    saved, params = [], []
    p = params_a(0, all_gather_hbm([shards_a[0][:1]], "gather_w_in")[0][0])
    for l in range(nl):
        xs, s, gathered, p = layer_fwd(xs, p, (shards_a, shards_b, l + 1) if l + 1 < nl else None,
                                       (shards_b, 0, lambda got: params_b(*got)) if l == 0 else None)
        saved.append(s)
        params.append(p)
        if l + 1 < nl:
            p = dict(params_a(l + 1, gathered[0]), **params_b(*gathered[1:]))
    loss_part, g = loss_grad(xs, loss_target[0], tm)
    loss = lax.psum(loss_part[0, 0], ("x", "y", "c"))

    big = {"w_in": (w_in, m_w_in, v_w_in), "w_out": (w_out, m_w_out, v_w_out),
           "pool_mix_w": tuple(a.reshape(nl, ng * gsh, gw) for a in (pool_mix_w, m_pool_mix_w, v_pool_mix_w))}
    names = list(big)
    big_out = {k: None for k in big}
    small_g = [None] * nl

    def apply(layer, own, got_sib, got_chips):
        for k, o, gs_, gc in zip(names, own, got_sib, got_chips):
            wk, mk, vk = big[k]
            big_out[k] = reduce_adam(lax.dynamic_index_in_dim(o, mychip, 0, keepdims=False),
                                     lax.dynamic_index_in_dim(gs_, mychip, 0, keepdims=False),
                                     gc, wk, mk, vk, big_out[k], layer, min(256, wk.shape[1]), "reduce_adam_" + k,
                                     shifted=(k == "w_in"))

    def split_in(dw_main, dw_dt):
        own, send = grad_blocks(dw_main, dw_dt, cols, min(128, d))
        return [own], [send]

    def split_rest(dw_out, d_mixw):
        halves = [lambda ci: lax.dynamic_index_in_dim(dw_out.reshape(4, 2, -1, d), ci, 1, keepdims=False),
                  lambda ci: lax.dynamic_index_in_dim(
                      d_mixw.reshape(ng, 4, 2, gsh, gw), ci, 2, keepdims=False).transpose(1, 0, 2, 3).reshape(4, ng * gsh, gw)]
        return [h(cc) for h in halves], [h(1 - cc).astype(BF16) for h in halves]

    pending = None
    for l in reversed(range(nl)):
        g, gr, mine, done = layer_bwd(g, saved[l], params[l], split_in, split_rest, pending, last=(l == 0))
        if pending is not None:
            apply(l + 1, pending[0], *done[0])
        if l == 0:
            apply(0, mine[0], *done[1])
        pending = mine
        small_g[l] = dict(pre_norm_w=gr["pre_w"], pool_scale=gr["pscale"], conv_b=gr["conv_b"], dt_bias=gr["dt_bias"][:, :n_heads],
                          a_log=gr["a_log"][:, :n_heads], d_skip=gr["d_skip"][:, :n_heads], _pad=jnp.zeros((1, LANES - 3 * n_heads), F32),
                          ssd_norm_w=gr["norm_w"], post_norm_w=gr["post_w"], conv_w=gr["conv_w"].reshape(1, CONV_WIDTH * cd))

    sizes = {k: small_g[0][k].shape[1] for k in SMALL}
    gsum = sum_devices(all_gather_vmem(_pack({k: jnp.concatenate([sg[k] for sg in small_g], axis=0) for k in SMALL}),
                                       "gather_small_grads"))
    gs = _unpack(gsum, sizes, nl)
    csh = conv_w.shape[2]
    gs["conv_w"] = lax.dynamic_slice_in_dim(gs["conv_w"].reshape(nl, CONV_WIDTH, cd), me * csh, csh, axis=2).reshape(nl, -1)
    lsizes = dict(sizes, conv_w=CONV_WIDTH * csh)
    zpad = jnp.zeros((nl, sizes["_pad"]), F32)

    def local(pre, scale, cb, dtb, al, dsk, nw, post, cw):
        return _pack(dict(pre_norm_w=pre, pool_scale=scale, conv_b=cb, dt_bias=dtb, a_log=al, d_skip=dsk, _pad=zpad,
                          ssd_norm_w=nw, post_norm_w=post, conv_w=cw.reshape(nl, -1)))

    wp = local(pre_norm_w, pool_scale, conv_b, dt_bias, a_log, d_skip, ssd_norm_w, post_norm_w, conv_w)
    mp = local(m_pre_norm_w, m_pool_scale, m_conv_b, m_dt_bias, m_a_log, m_d_skip, m_ssd_norm_w, m_post_norm_w, m_conv_w)
    vp = local(v_pre_norm_w, v_pool_scale, v_conv_b, v_dt_bias, v_a_log, v_d_skip, v_ssd_norm_w, v_post_norm_w, v_conv_w)
    small_out = [gs] + [_unpack(o, lsizes, nl) for o in adam_small(wp, _pack(gs), mp, vp)]

    def leaf(kind, name):
        if name in big:
            return big_out[name][kind].reshape(big[name][0].shape if name != "pool_mix_w" else pool_mix_w.shape)
        val = small_out[kind][name]
        return val.reshape(conv_w.shape) if name == "conv_w" else val

    order = ("pre_norm_w", "w_in", "pool_mix_w", "pool_scale", "conv_w", "conv_b", "dt_bias", "a_log", "d_skip",
             "ssd_norm_w", "w_out", "post_norm_w")
    return (loss, g[None]) + tuple(leaf(kind, name) for kind in range(4) for name in order)
```

```python
import jax
import jax.numpy as jnp
from jax import lax
from jax.experimental import pallas as pl
from jax.experimental.pallas import tpu as pltpu

F32 = jnp.float32
BF16 = jnp.bfloat16
SDS = jax.ShapeDtypeStruct
MESH = pl.DeviceIdType.MESH
HIGHEST = lax.Precision.HIGHEST

NORM_EPS = 1e-6
POOL_WINDOWS = (2, 4, 8, 16)
POOL_HALO = 16
CONV_WIDTH = 4
CONV_HALO = 8
SSD_CHUNK = 128
SSD_HEAD_DIM = 64
SSD_STATE = 128
SSD_GROUPS = 4
LANES = 128
N_DEV = 8

ADAM_LR = 0.001
ADAM_B1 = 0.9
ADAM_B2 = 0.999
ADAM_EPS = 1e-08
ADAM_WD = 0.01
ADAM_STEP = 10

VMEM_LIMIT = 56 * 1024 * 1024

NT = (((1,), (1,)), ((), ()))
TN = (((0,), (0,)), ((), ()))


def _params(sem=None):
    kw = dict(vmem_limit_bytes=VMEM_LIMIT)
    if sem is not None:
        kw["dimension_semantics"] = sem
    return pltpu.CompilerParams(**kw)


def _silu(v):
    return v * jax.nn.sigmoid(v)


def _dsilu(v):
    s = jax.nn.sigmoid(v)
    return s * (1.0 + v * (1.0 - s))


def _split_dot(v, sel):
    hi = v.astype(BF16)
    lo = (v - hi.astype(F32)).astype(BF16)
    return (jnp.dot(hi, sel, preferred_element_type=F32) + jnp.dot(lo, sel, preferred_element_type=F32))


def _head_selector(width, per):
    ch = lax.broadcasted_iota(jnp.int32, (width, LANES), 0)
    hd = lax.broadcasted_iota(jnp.int32, (width, LANES), 1)
    return jnp.where((ch >= hd * per) & (ch < (hd + 1) * per), 1.0, 0.0).astype(BF16)


class Comm:
    def __init__(self, inputs, out_shapes, aliases, n_sems, make):
        self.inputs, self.out_shapes, self.aliases, self.n_sems, self.make = list(inputs), list(out_shapes), dict(aliases), n_sems, make


def _remote(src, dst, send_sems, recv_sems, k, peer):
    return pltpu.make_async_remote_copy(src_ref=src, dst_ref=dst, send_sem=send_sems.at[k], recv_sem=recv_sems.at[k],
                                        device_id=peer, device_id_type=MESH)


class _SemRange:
    def __init__(self, sems, start):
        self.sems, self.start = sems, start

    @property
    def at(self):
        return self

    def __getitem__(self, k):
        return self.sems.at[self.start + k]


def merge_comms(comms):
    comms = [c for c in comms if c is not None]
    if len(comms) <= 1:
        return comms[0] if comms else None
    aliases, i_off, o_off = {}, 0, 0
    for c in comms:
        aliases.update({i_off + k: o_off + v for k, v in c.aliases.items()})
        i_off, o_off = i_off + len(c.inputs), o_off + len(c.out_shapes)

    def make(ins, outs, ss, rs):
        sends, locals_, arrivals, i0, o0, s0 = [], [], [], 0, 0, 0
        for c in comms:
            s, l, a = c.make(ins[i0:i0 + len(c.inputs)], outs[o0:o0 + len(c.out_shapes)], _SemRange(ss, s0), _SemRange(rs, s0))
            sends, locals_, arrivals = sends + s, locals_ + l, arrivals + a
            i0, o0, s0 = i0 + len(c.inputs), o0 + len(c.out_shapes), s0 + c.n_sems
        return sends, locals_, arrivals

    return Comm(sum((c.inputs for c in comms), []), sum((c.out_shapes for c in comms), []), aliases,
                sum(c.n_sems for c in comms), make)


def _call(body, args, *, name, grid, in_specs, out_specs, out_shape, scratch_shapes=(), sem=None, comm=None):
    in_specs, out_specs, out_shape = list(in_specs), list(out_specs), list(out_shape)
    if comm is None:
        outs = pl.pallas_call(body, name=name, grid=grid, in_specs=in_specs, out_specs=out_specs, out_shape=out_shape,
                              scratch_shapes=list(scratch_shapes), compiler_params=_params(sem))(*args)
        return list(outs), []
    ni, no, nci, nco, ns = len(in_specs), len(out_specs), len(comm.inputs), len(comm.out_shapes), len(scratch_shapes)
    hbm = pl.BlockSpec(memory_space=pl.ANY)

    def hosted(*refs):
        ins, cins = refs[:ni], refs[ni:ni + nci]
        outs, couts = refs[ni + nci:ni + nci + no], refs[ni + nci + no:ni + nci + no + nco]
        scratch = refs[ni + nci + no + nco:]
        sends, locals_, arrivals = comm.make(cins, couts, scratch[ns], scratch[ns + 1])
        first = last = None if grid else True
        for axis, extent in enumerate(grid):
            pid = pl.program_id(axis)
            first = (pid == 0) if first is None else first & (pid == 0)
            last = (pid == extent - 1) if last is None else last & (pid == extent - 1)

        @pl.when(first)
        def _():
            for cp in locals_ + sends:
                cp.start()

        body(*ins, *outs, *scratch[:ns])

        @pl.when(last)
        def _():
            for cp in arrivals:
                cp.wait_recv()
            for cp in sends:
                cp.wait_send()
            for cp in locals_:
                cp.wait()

    outs = pl.pallas_call(
        hosted, name=name, grid=grid, in_specs=in_specs + [hbm] * nci, out_specs=out_specs + [hbm] * nco,
        out_shape=out_shape + comm.out_shapes,
        scratch_shapes=list(scratch_shapes) + [pltpu.SemaphoreType.DMA((comm.n_sems,)), pltpu.SemaphoreType.DMA((comm.n_sems,))],
        input_output_aliases={ni + k: no + v for k, v in comm.aliases.items()},
        compiler_params=_params(sem),
    )(*args, *comm.inputs)
    return list(outs[:no]), list(outs[no:])


def rms_fwd(x, w, tm):
    t, d = x.shape

    def body(x_ref, w_ref, h_ref, r_ref):
        xv = x_ref[...]
        r = lax.rsqrt(jnp.mean(xv * xv, axis=-1, keepdims=True) + NORM_EPS)
        h_ref[...] = (xv * r * w_ref[...]).astype(BF16)
        r_ref[...] = r

    return pl.pallas_call(
        body, name="rms_fwd", grid=(t // tm,),
        in_specs=[pl.BlockSpec((tm, d), lambda i: (i, 0)), pl.BlockSpec((1, d), lambda i: (0, 0))],
        out_specs=[pl.BlockSpec((tm, d), lambda i: (i, 0)), pl.BlockSpec((tm, 1), lambda i: (i, 0))],
        out_shape=[SDS((t, d), BF16), SDS((t, 1), F32)],
        compiler_params=_params(("arbitrary",)),
    )(x, w)


def post_fwd(out, x, w, tm, comm=None):
    t, d = x.shape

    def body(o_ref, x_ref, w_ref, y_ref, r_ref):
        ov = o_ref[...]
        r = lax.rsqrt(jnp.mean(ov * ov, axis=-1, keepdims=True) + NORM_EPS)
        y_ref[...] = x_ref[...] + ov * r * w_ref[...]
        r_ref[...] = r

    return _call(
        body, (out, x, w), name="post_fwd", grid=(t // tm,),
        in_specs=[pl.BlockSpec((tm, d), lambda i: (i, 0)), pl.BlockSpec((tm, d), lambda i: (i, 0)),
                  pl.BlockSpec((1, d), lambda i: (0, 0))],
        out_specs=[pl.BlockSpec((tm, d), lambda i: (i, 0)), pl.BlockSpec((tm, 1), lambda i: (i, 0))],
        out_shape=[SDS((t, d), F32), SDS((t, 1), F32)], sem=("arbitrary",), comm=comm)


def _norm_bwd(g_n, n, r):
    return r * (g_n - n * jnp.mean(g_n * n, axis=-1, keepdims=True))


def post_bwd(g, out, r, w, tm):
    t, d = g.shape

    def body(g_ref, o_ref, r_ref, w_ref, do_ref, dw_ref):
        i = pl.program_id(0)
        gv = g_ref[...]
        rv = r_ref[...]
        n = o_ref[...] * rv
        part = jnp.sum(gv * n, axis=0, keepdims=True)

        @pl.when(i == 0)
        def _():
            dw_ref[...] = part

        @pl.when(i > 0)
        def _():
            dw_ref[...] += part

        do_ref[...] = _norm_bwd(gv * w_ref[...], n, rv).astype(BF16)

    return pl.pallas_call(
        body, name="post_bwd", grid=(t // tm,),
        in_specs=[pl.BlockSpec((tm, d), lambda i: (i, 0)), pl.BlockSpec((tm, d), lambda i: (i, 0)),
                  pl.BlockSpec((tm, 1), lambda i: (i, 0)), pl.BlockSpec((1, d), lambda i: (0, 0))],
        out_specs=[pl.BlockSpec((tm, d), lambda i: (i, 0)), pl.BlockSpec((1, d), lambda i: (0, 0))],
        out_shape=[SDS((t, d), BF16), SDS((1, d), F32)],
        compiler_params=_params(("arbitrary",)),
    )(g, out, r, w)


def rms_bwd(dh, x, r, w, g, tm):
    t, d = x.shape

    def body(a_ref, x_ref, r_ref, w_ref, g_ref, gx_ref, dw_ref):
        i = pl.program_id(0)
        dh = a_ref[...]
        rv = r_ref[...]
        n = x_ref[...] * rv
        part = jnp.sum(dh * n, axis=0, keepdims=True)

        @pl.when(i == 0)
        def _():
            dw_ref[...] = part

        @pl.when(i > 0)
        def _():
            dw_ref[...] += part

        gx_ref[...] = g_ref[...] + _norm_bwd(dh * w_ref[...], n, rv)

    row = pl.BlockSpec((tm, d), lambda i: (i, 0))
    return pl.pallas_call(
        body, name="rms_bwd", grid=(t // tm,),
        in_specs=[row, row, pl.BlockSpec((tm, 1), lambda i: (i, 0)), pl.BlockSpec((1, d), lambda i: (0, 0)), row],
        out_specs=[row, pl.BlockSpec((1, d), lambda i: (0, 0))],
        out_shape=[SDS((t, d), F32), SDS((1, d), F32)],
        compiler_params=_params(("arbitrary",)),
    )(dh, x, r, w, g)


def loss_grad(y, target, tm):
    t, d = y.shape

    def body(y_ref, t_ref, l_ref, g_ref):
        i = pl.program_id(0)
        err = y_ref[...] - t_ref[...]
        g_ref[...] = err / d
        part = 0.5 * jnp.sum(jnp.mean(err * err, axis=-1, keepdims=True), axis=0, keepdims=True)

        @pl.when(i == 0)
        def _():
            l_ref[...] = part

        @pl.when(i > 0)
        def _():
            l_ref[...] += part

    row = pl.BlockSpec((tm, d), lambda i: (i, 0))
    return pl.pallas_call(
        body, name="loss_grad", grid=(t // tm,), in_specs=[row, row],
        out_specs=[pl.BlockSpec((1, 1), lambda i: (0, 0)), row],
        out_shape=[SDS((1, 1), F32), SDS((t, d), F32)],
        compiler_params=_params(("arbitrary",)),
    )(y, target)


def mm_nn(a, b, out_dtype, tm, tn, name, comm=None):
    m, k = a.shape
    n = b.shape[1]

    def body(a_ref, b_ref, o_ref):
        o_ref[...] = jnp.dot(a_ref[...], b_ref[...], preferred_element_type=F32).astype(out_dtype)

    outs, couts = _call(
        body, (a, b), name=name, grid=(n // tn, m // tm),
        in_specs=[pl.BlockSpec((tm, k), lambda j, i: (i, 0)), pl.BlockSpec((k, tn), lambda j, i: (0, j))],
        out_specs=[pl.BlockSpec((tm, tn), lambda j, i: (i, j))],
        out_shape=[SDS((m, n), out_dtype)], sem=("arbitrary", "arbitrary"), comm=comm)
    return outs[0], couts


def mm_nt(a, b, out_dtype, tm, tn, tk, name, comm=None, extra=None):
    m, k = a.shape
    n = b.shape[0]
    nk = k // tk

    def body(a_ref, b_ref, *rest):
        o_ref, acc_ref = rest[-2:]
        kk = pl.program_id(2)
        part = lax.dot_general(a_ref[...], b_ref[...], NT, preferred_element_type=F32)
        if nk == 1:
            if extra is not None:
                part = part + lax.dot_general(rest[0][...], rest[1][...], NT, preferred_element_type=F32)
            o_ref[...] = part.astype(out_dtype)
        else:
            @pl.when(kk == 0)
            def _():
                if extra is None:
                    acc_ref[...] = part
                else:
                    acc_ref[...] = part + lax.dot_general(rest[0][...], rest[1][...], NT, preferred_element_type=F32)

            @pl.when(kk > 0)
            def _():
                acc_ref[...] += part

            @pl.when(kk == nk - 1)
            def _():
                o_ref[...] = acc_ref[...].astype(out_dtype)

    more_specs = [] if extra is None else [pl.BlockSpec((tm, extra[0].shape[1]), lambda i, j, kk: (i, 0)),
                                           pl.BlockSpec((tn, extra[1].shape[1]), lambda i, j, kk: (j, 0))]
    outs, couts = _call(
        body, (a, b) + tuple(extra or ()), name=name, grid=(m // tm, n // tn, nk),
        in_specs=[pl.BlockSpec((tm, tk), lambda i, j, kk: (i, kk)), pl.BlockSpec((tn, tk), lambda i, j, kk: (j, kk))] + more_specs,
        out_specs=[pl.BlockSpec((tm, tn), lambda i, j, kk: (i, j))],
        out_shape=[SDS((m, n), out_dtype)],
        scratch_shapes=[pltpu.VMEM((tm, tn) if nk > 1 else (8, LANES), F32)],
        sem=("arbitrary", "arbitrary", "arbitrary"), comm=comm)
    return outs[0], couts


def mm_tn(a, b, tm, tn, name, comm=None):
    t, m = a.shape
    n = b.shape[1]

    def body(a_ref, b_ref, o_ref):
        o_ref[...] = lax.dot_general(a_ref[...], b_ref[...], TN, preferred_element_type=F32)

    outs, couts = _call(
        body, (a, b), name=name, grid=(m // tm, n // tn),
        in_specs=[pl.BlockSpec((t, tm), lambda i, j: (0, i)), pl.BlockSpec((t, tn), lambda i, j: (0, j))],
        out_specs=[pl.BlockSpec((tm, tn), lambda i, j: (i, j))],
        out_shape=[SDS((m, n), F32)], sem=("arbitrary", "arbitrary"), comm=comm)
    return outs[0], couts


def _window_sums(ext, n_rows, lookahead):
    def sh(v, k):
        return pltpu.roll(v, (n_rows - k) if lookahead else k, 0)
    s2 = ext + sh(ext, 1)
    s4 = s2 + sh(s2, 2)
    s8 = s4 + sh(s4, 4)
    s16 = s8 + sh(s8, 8)
    return (s2, s4, s8, s16)


def _pool_counts(i, tm, w):
    tpos = i * tm + lax.broadcasted_iota(jnp.int32, (tm, 1), 0)
    return jnp.minimum(tpos + 1, w).astype(F32)


def _pooled(uc_ref, up_ref, i, tm):
    cur = uc_ref[...]
    prev = jnp.where(i > 0, up_ref[...], 0.0)
    ext = jnp.concatenate([prev, cur], axis=0)
    return cur, _window_sums(ext, tm + POOL_HALO, False)


def pool_fwd(proj, mixw, scale, tm):
    t = proj.shape[0]
    pw = scale.shape[1]
    gw = pw // len(POOL_WINDOWS)
    nh = tm // POOL_HALO

    def body(uc_ref, up_ref, g_ref, w_ref, s_ref, o_ref):
        i = pl.program_id(0)
        cur, sums = _pooled(uc_ref, up_ref, i, tm)
        for g, w in enumerate(POOL_WINDOWS):
            cols = slice(g * gw, (g + 1) * gw)
            pooled = sums[g][POOL_HALO:, cols] / _pool_counts(i, tm, w) - cur[:, cols]
            mixed = jnp.dot(pooled.astype(BF16), w_ref[g], preferred_element_type=F32)
            o_ref[:, cols] = (mixed * s_ref[:, cols] * _silu(g_ref[:, cols])).astype(BF16)

    return pl.pallas_call(
        body, name="pool_fwd", grid=(t // tm,),
        in_specs=[pl.BlockSpec((tm, pw), lambda i: (i, 0)),
                  pl.BlockSpec((POOL_HALO, pw), lambda i: (jnp.maximum(i * nh - 1, 0), 0)),
                  pl.BlockSpec((tm, pw), lambda i: (i, 1)),
                  pl.BlockSpec(mixw.shape, lambda i: (0, 0, 0)),
                  pl.BlockSpec((1, pw), lambda i: (0, 0))],
        out_specs=pl.BlockSpec((tm, pw), lambda i: (i, 0)),
        out_shape=SDS((t, 2 * pw), BF16),
        compiler_params=_params(("arbitrary",)),
    )(proj, proj, proj, mixw, scale)


def pool_bwd_a(dmixed, proj, mixw, scale, tm):
    t, e = proj.shape
    pw = scale.shape[1]
    ng = len(POOL_WINDOWS)
    gw = pw // ng
    nh = tm // POOL_HALO

    def body(dy_ref, uc_ref, up_ref, g_ref, w_ref, s_ref, dg_ref, dq_ref, ds_ref, dw_ref):
        i = pl.program_id(0)

        @pl.when(i == 0)
        def _():
            ds_ref[...] = jnp.zeros_like(ds_ref)
            dw_ref[...] = jnp.zeros_like(dw_ref)

        cur, sums = _pooled(uc_ref, up_ref, i, tm)
        for g, w in enumerate(POOL_WINDOWS):
            cols = slice(g * gw, (g + 1) * gw)
            cnt = _pool_counts(i, tm, w)
            pooled = (sums[g][POOL_HALO:, cols] / cnt - cur[:, cols]).astype(BF16)
            mixed = jnp.dot(pooled, w_ref[g], preferred_element_type=F32)
            gate = g_ref[:, cols]
            dy = dy_ref[:, cols]
            sc = s_ref[:, cols]
            dg_ref[:, cols] = (dy * mixed * sc * _dsilu(gate)).astype(BF16)
            ds = dy * _silu(gate)
            ds_ref[:, cols] += jnp.sum(ds * mixed, axis=0, keepdims=True)
            dmix = (ds * sc).astype(BF16)
            dw_ref[g] += lax.dot_general(pooled, dmix, TN, preferred_element_type=F32)
            dq_ref[:, cols] = lax.dot_general(dmix, w_ref[g], NT, preferred_element_type=F32) / cnt

    return pl.pallas_call(
        body, name="pool_bwd_a", grid=(t // tm,),
        in_specs=[pl.BlockSpec((tm, pw), lambda i: (i, 0)),
                  pl.BlockSpec((tm, pw), lambda i: (i, 0)),
                  pl.BlockSpec((POOL_HALO, pw), lambda i: (jnp.maximum(i * nh - 1, 0), 0)),
                  pl.BlockSpec((tm, pw), lambda i: (i, 1)),
                  pl.BlockSpec(mixw.shape, lambda i: (0, 0, 0)),
                  pl.BlockSpec((1, pw), lambda i: (0, 0))],
        out_specs=[pl.BlockSpec((tm, pw), lambda i: (i, 1)),
                   pl.BlockSpec((tm, pw), lambda i: (i, 0)),
                   pl.BlockSpec((1, pw), lambda i: (0, 0)),
                   pl.BlockSpec((ng, gw, gw), lambda i: (0, 0, 0))],
        out_shape=[SDS((t, e), BF16), SDS((t, pw), F32), SDS((1, pw), F32), SDS((ng, gw, gw), F32)],
        compiler_params=_params(("arbitrary",)),
    )(dmixed, proj, proj, proj, mixw, scale)


def pool_bwd_b(dq, dproj, tm):
    t, pw = dq.shape
    gw = pw // len(POOL_WINDOWS)
    nh = tm // POOL_HALO
    nt = t // tm

    def body(c_ref, n_ref, alias_ref, o_ref):
        i = pl.program_id(0)
        cur = c_ref[...]
        nxt = jnp.where(i < nt - 1, n_ref[...], 0.0)
        sums = _window_sums(jnp.concatenate([cur, nxt], axis=0), tm + POOL_HALO, True)
        for g, w in enumerate(POOL_WINDOWS):
            cols = slice(g * gw, (g + 1) * gw)
            o_ref[:, cols] = (sums[g][:tm, cols] - cur[:, cols] * _pool_counts(i, tm, w)).astype(BF16)

    return pl.pallas_call(
        body, name="pool_bwd_b", grid=(nt,),
        in_specs=[pl.BlockSpec((tm, pw), lambda i: (i, 0)),
                  pl.BlockSpec((POOL_HALO, pw), lambda i: (jnp.minimum((i + 1) * nh, t // POOL_HALO - 1), 0)),
                  pl.BlockSpec(memory_space=pl.ANY)],
        out_specs=pl.BlockSpec((tm, pw), lambda i: (i, 0)),
        out_shape=SDS(dproj.shape, dproj.dtype),
        input_output_aliases={2: 0},
        compiler_params=_params(("arbitrary",)),
    )(dq, dq, dproj)


ELEMENTWISE_LANE_CHUNK = 256


def _lane_chunks(width):
    return [slice(c, c + ELEMENTWISE_LANE_CHUNK) for c in range(0, width, ELEMENTWISE_LANE_CHUNK)]


def _conv_pre(xc_ref, xp_ref, w_ref, b_ref, i, cols):
    cur = xc_ref[:, cols]
    prev = jnp.where(i > 0, xp_ref[:, cols], 0.0)
    ext = jnp.concatenate([prev, cur], axis=0)
    taps = [pltpu.roll(ext, CONV_WIDTH - 1 - k, 0)[CONV_HALO:] for k in range(CONV_WIDTH - 1)] + [cur]
    pre = b_ref[:, cols]
    for k in range(CONV_WIDTH):
        pre = pre + w_ref[k:k + 1, cols] * taps[k]
    return pre, taps


def conv_fwd(proj, conv_w, conv_b, col_block, tm):
    t = proj.shape[0]
    cd = conv_b.shape[1]
    nh = tm // CONV_HALO

    def body(xc_ref, xp_ref, w_ref, b_ref, o_ref):
        i = pl.program_id(0)
        for cols in _lane_chunks(cd):
            pre, _ = _conv_pre(xc_ref, xp_ref, w_ref, b_ref, i, cols)
            o_ref[:, cols] = _silu(pre)

    return pl.pallas_call(
        body, name="conv_fwd", grid=(t // tm,),
        in_specs=[pl.BlockSpec((tm, cd), lambda i: (i, col_block)),
                  pl.BlockSpec((CONV_HALO, cd), lambda i: (jnp.maximum(i * nh - 1, 0), col_block)),
                  pl.BlockSpec((CONV_WIDTH, cd), lambda i: (0, 0)),
                  pl.BlockSpec((1, cd), lambda i: (0, 0))],
        out_specs=pl.BlockSpec((tm, cd), lambda i: (i, 0)),
        out_shape=SDS((t, cd), F32),
        compiler_params=_params(("arbitrary",)),
    )(proj, proj, conv_w, conv_b)


def conv_bwd_a(dxs, db, dc, proj, conv_w, conv_b, col_block, tm, comm=None):
    t = proj.shape[0]
    cd = conv_b.shape[1]
    sw = dxs.shape[1]
    gn = db.shape[1]
    nh = tm // CONV_HALO

    def body(dx_ref, db_ref, dc_ref, xc_ref, xp_ref, w_ref, b_ref, dp_ref, dw_ref, dbias_ref):
        i = pl.program_id(0)

        @pl.when(i == 0)
        def _():
            dw_ref[...] = jnp.zeros_like(dw_ref)
            dbias_ref[...] = jnp.zeros_like(dbias_ref)

        for cols in _lane_chunks(cd):
            pre, taps = _conv_pre(xc_ref, xp_ref, w_ref, b_ref, i, cols)
            if cols.start < sw:
                dact = dx_ref[:, cols]
            elif cols.start < sw + gn:
                dact = db_ref[:, cols.start - sw:cols.stop - sw]
            else:
                dact = dc_ref[:, cols.start - sw - gn:cols.stop - sw - gn]
            dpre = dact * _dsilu(pre)
            dp_ref[:, cols] = dpre
            dbias_ref[:, cols] += jnp.sum(dpre, axis=0, keepdims=True)
            for k in range(CONV_WIDTH):
                dw_ref[k:k + 1, cols] += jnp.sum(dpre * taps[k], axis=0, keepdims=True)

    return _call(
        body, (dxs, db, dc, proj, proj, conv_w, conv_b), name="conv_bwd_a", grid=(t // tm,),
        in_specs=[pl.BlockSpec((tm, sw), lambda i: (i, 0)), pl.BlockSpec((tm, gn), lambda i: (i, 0)),
                  pl.BlockSpec((tm, gn), lambda i: (i, 0)),
                  pl.BlockSpec((tm, cd), lambda i: (i, col_block)),
                  pl.BlockSpec((CONV_HALO, cd), lambda i: (jnp.maximum(i * nh - 1, 0), col_block)),
                  pl.BlockSpec((CONV_WIDTH, cd), lambda i: (0, 0)),
                  pl.BlockSpec((1, cd), lambda i: (0, 0))],
        out_specs=[pl.BlockSpec((tm, cd), lambda i: (i, 0)),
                   pl.BlockSpec((CONV_WIDTH, cd), lambda i: (0, 0)),
                   pl.BlockSpec((1, cd), lambda i: (0, 0))],
        out_shape=[SDS((t, cd), F32), SDS((CONV_WIDTH, cd), F32), SDS((1, cd), F32)],
        sem=("arbitrary",), comm=comm)


def conv_bwd_b(dpre, conv_w, dproj, col_block, tm):
    t, cd = dpre.shape
    nh = tm // CONV_HALO
    nt = t // tm

    def body(c_ref, n_ref, w_ref, alias_ref, o_ref):
        i = pl.program_id(0)
        n = tm + CONV_HALO
        for cols in _lane_chunks(cd):
            cur = c_ref[:, cols]
            nxt = jnp.where(i < nt - 1, n_ref[:, cols], 0.0)
            ext = jnp.concatenate([cur, nxt], axis=0)
            acc = w_ref[CONV_WIDTH - 1:CONV_WIDTH, cols] * cur
            for k in range(CONV_WIDTH - 1):
                acc = acc + w_ref[k:k + 1, cols] * pltpu.roll(ext, n - (CONV_WIDTH - 1 - k), 0)[:tm]
            o_ref[:, cols] = acc.astype(BF16)

    return pl.pallas_call(
        body, name="conv_bwd_b", grid=(nt,),
        in_specs=[pl.BlockSpec((tm, cd), lambda i: (i, 0)),
                  pl.BlockSpec((CONV_HALO, cd), lambda i: (jnp.minimum((i + 1) * nh, t // CONV_HALO - 1), 0)),
                  pl.BlockSpec((CONV_WIDTH, cd), lambda i: (0, 0)),
                  pl.BlockSpec(memory_space=pl.ANY)],
        out_specs=pl.BlockSpec((tm, cd), lambda i: (i, col_block)),
        out_shape=SDS(dproj.shape, dproj.dtype),
        input_output_aliases={3: 0},
        compiler_params=_params(("arbitrary",)),
    )(dpre, dpre, conv_w, dproj)


def _softplus(v):
    return jnp.maximum(v, 0.0) + jnp.log(1.0 + jnp.exp(-jnp.abs(v)))


def _ssd_chunk_terms(dtr_ref, bias_ref, a_ref, n_heads):
    q = SSD_CHUNK
    lane = lax.broadcasted_iota(jnp.int32, (1, LANES), 1)
    pre = dtr_ref[...] + bias_ref[...]
    dt = jnp.where(lane < n_heads, _softplus(pre), 0.0)
    a = jnp.where(lane < n_heads, -jnp.exp(a_ref[...]), 0.0)
    row = lax.broadcasted_iota(jnp.int32, (q, q), 0)
    col = lax.broadcasted_iota(jnp.int32, (q, q), 1)
    causal = row >= col
    acs = jnp.dot(causal.astype(F32), dt * a, precision=HIGHEST, preferred_element_type=F32)
    last = acs[q - 1:q, :]
    return dict(pre=pre, dt=dt, a=a, acs=acs, acs_t=acs.T, eacs=jnp.exp(acs), dstate=jnp.exp(last - acs),
                cdec=jnp.exp(last), causal=causal, diag=row == col, lane=lane)


def _pair_cols(lo, v, h):
    return jnp.where(lo, v[:, h:h + 1], v[:, h + 1:h + 2])


def _pair_decay(tm_, cb, h):
    l0 = jnp.exp(jnp.where(tm_["causal"], tm_["acs"][:, h:h + 1] - tm_["acs_t"][h:h + 1, :], -jnp.inf))
    l1 = jnp.exp(jnp.where(tm_["causal"], tm_["acs"][:, h + 1:h + 2] - tm_["acs_t"][h + 1:h + 2, :], -jnp.inf))
    return l0, l1, jnp.concatenate([cb * l0, cb * l1], axis=1)


def _pair_decay_t(tm_, cbt, h):
    upper = jnp.logical_not(tm_["causal"]) | tm_["diag"]
    t0 = jnp.exp(jnp.where(upper, tm_["acs_t"][h:h + 1, :] - tm_["acs"][:, h:h + 1], -jnp.inf))
    t1 = jnp.exp(jnp.where(upper, tm_["acs_t"][h + 1:h + 2, :] - tm_["acs"][:, h + 1:h + 2], -jnp.inf))
    return jnp.concatenate([cbt * t0, cbt * t1], axis=0).astype(BF16)


def _block_diag(lo, xdt):
    return jnp.concatenate([jnp.where(lo, xdt, 0.0), jnp.where(lo, 0.0, xdt)], axis=0).astype(BF16)


def ssd_fwd(xbc, dt_raw, dt_bias, a_log, n_heads, comm=None):
    t = xbc.shape[0]
    q = SSD_CHUNK
    gn = SSD_GROUPS * SSD_STATE
    sw = n_heads * SSD_HEAD_DIM
    n_pairs = n_heads // 2
    pairs_per_group = n_pairs // SSD_GROUPS
    nc = t // q
    bblk = sw // gn

    def body(xs_ref, b_ref, c_ref, dtr_ref, bias_ref, a_ref, y_ref, sin_ref, state):
        @pl.when(pl.program_id(0) == 0)
        def _():
            state[...] = jnp.zeros_like(state)

        tm_ = _ssd_chunk_terms(dtr_ref, bias_ref, a_ref, n_heads)
        lo = tm_["lane"] < SSD_HEAD_DIM
        for g in range(SSD_GROUPS):
            gcols = slice(g * SSD_STATE, (g + 1) * SSD_STATE)
            bg = b_ref[:, gcols].astype(BF16)
            bg_t = b_ref[:, gcols].T.astype(BF16)
            cg = c_ref[:, gcols].astype(BF16)
            cb = lax.dot_general(cg, bg, NT, preferred_element_type=F32)
            for j in range(pairs_per_group):
                p = g * pairs_per_group + j
                h = 2 * p
                pcols = slice(p * LANES, (p + 1) * LANES)
                _, _, mcat = _pair_decay(tm_, cb, h)
                xdt = xs_ref[:, pcols] * _pair_cols(lo, tm_["dt"], h)
                ydiag = jnp.dot(mcat.astype(BF16), _block_diag(lo, xdt), preferred_element_type=F32)
                st = state[p]
                sin_ref[0, p] = st
                yoff = jnp.dot(cg, st.astype(BF16), preferred_element_type=F32) * _pair_cols(lo, tm_["eacs"], h)
                y_ref[:, pcols] = ydiag + yoff
                xw = (xdt * _pair_cols(lo, tm_["dstate"], h)).astype(BF16)
                state[p] = st * _pair_cols(lo, tm_["cdec"], h) + jnp.dot(bg_t, xw, preferred_element_type=F32)

    vec = pl.BlockSpec((1, LANES), lambda c: (0, 0))
    return _call(
        body, (xbc, xbc, xbc, dt_raw, dt_bias, a_log), name="ssd_fwd", grid=(nc,),
        in_specs=[pl.BlockSpec((q, sw), lambda c: (c, 0)),
                  pl.BlockSpec((q, gn), lambda c: (c, bblk)),
                  pl.BlockSpec((q, gn), lambda c: (c, bblk + 1)),
                  pl.BlockSpec((q, LANES), lambda c: (c, 0)), vec, vec],
        out_specs=[pl.BlockSpec((q, sw), lambda c: (c, 0)),
                   pl.BlockSpec((1, n_pairs, SSD_STATE, LANES), lambda c: (c, 0, 0, 0))],
        out_shape=[SDS((t, sw), F32), SDS((nc, n_pairs, SSD_STATE, LANES), F32)],
        scratch_shapes=[pltpu.VMEM((n_pairs, SSD_STATE, LANES), F32)],
        sem=("arbitrary",), comm=comm)


def ssd_bwd(dy, xbc, dt_raw, dt_bias, a_log, d_full, s_in, n_heads, comm=None):
    t = xbc.shape[0]
    q = SSD_CHUNK
    gn = SSD_GROUPS * SSD_STATE
    sw = n_heads * SSD_HEAD_DIM
    n_pairs = n_heads // 2
    pairs_per_group = n_pairs // SSD_GROUPS
    nc = t // q
    bblk = sw // gn

    def body(dy_ref, xs_ref, b_ref, c_ref, dtr_ref, bias_ref, a_ref, dsk_ref, sin_ref,
             dxs_ref, db_ref, dc_ref, ddtr_ref, dbias_ref, dalog_ref,
             dstate, tbuf, xbuf, rbuf, acc_a, acc_b, sel_ref):
        i = pl.program_id(0)

        @pl.when(i == 0)
        def _():
            dstate[...] = jnp.zeros_like(dstate)
            rbuf[...] = jnp.zeros_like(rbuf)
            acc_a[...] = jnp.zeros_like(acc_a)
            acc_b[...] = jnp.zeros_like(acc_b)
            sel_ref[...] = _head_selector(sw, SSD_HEAD_DIM)

        tm_ = _ssd_chunk_terms(dtr_ref, bias_ref, a_ref, n_heads)
        lane = tm_["lane"]
        lo = lane < SSD_HEAD_DIM
        head_row = lax.broadcasted_iota(jnp.int32, (LANES, 1), 0)
        rows = jnp.zeros((q, LANES), F32)
        cols_t = jnp.zeros((LANES, q), F32)
        for g in range(SSD_GROUPS):
            gcols = slice(g * SSD_STATE, (g + 1) * SSD_STATE)
            bg = b_ref[:, gcols].astype(BF16)
            cg = c_ref[:, gcols].astype(BF16)
            cg_t = c_ref[:, gcols].T.astype(BF16)
            cb = lax.dot_general(cg, bg, NT, preferred_element_type=F32)
            cbt = lax.dot_general(bg, cg, NT, preferred_element_type=F32)
            dcb = jnp.zeros((q, q), F32)
            db_acc = jnp.zeros((q, SSD_STATE), F32)
            dc_acc = jnp.zeros((q, SSD_STATE), F32)
            for j in range(pairs_per_group):
                p = g * pairs_per_group + j
                h = 2 * p
                pcols = slice(p * LANES, (p + 1) * LANES)
                l0, l1, mcat = _pair_decay(tm_, cb, h)
                xp = xs_ref[:, pcols]
                dtp = _pair_cols(lo, tm_["dt"], h)
                xdt = xp * dtp
                xbd = _block_diag(lo, xdt)
                dyp = dy_ref[:, pcols]
                dyb = dyp.astype(BF16)
                dsb = _pair_cols(lo, tm_["dstate"], h)
                cdr = _pair_cols(lo, tm_["cdec"], h)
                eb = _pair_cols(lo, tm_["eacs"], h)
                st = sin_ref[0, p]
                stb = st.astype(BF16)
                dst = dstate[p]
                dstb = dst.astype(BF16)
                dye = (dyp * eb).astype(BF16)
                both = jnp.dot(_pair_decay_t(tm_, cbt, h), dyb, preferred_element_type=F32)
                dx_state = jnp.dot(bg, dstb, preferred_element_type=F32) * dsb
                dxdt = jnp.where(lo, both[:q], both[q:]) + dx_state
                dmcat = lax.dot_general(dyb, xbd, NT, preferred_element_type=F32)
                dcb = dcb + dmcat[:, :q] * l0 + dmcat[:, q:] * l1
                dseg = dmcat * mcat
                csum = jnp.sum(dseg, axis=0, keepdims=True)
                rows = (rows + jnp.where(lane == h, jnp.sum(dseg[:, :q], axis=1, keepdims=True), 0.0)
                        + jnp.where(lane == h + 1, jnp.sum(dseg[:, q:], axis=1, keepdims=True), 0.0))
                cols_t = (cols_t + jnp.where(head_row == h, csum[:, :q], 0.0)
                          + jnp.where(head_row == h + 1, csum[:, q:], 0.0))
                dc_acc = dc_acc + lax.dot_general(dye, stb, NT, preferred_element_type=F32)
                db_acc = db_acc + lax.dot_general((xdt * dsb).astype(BF16), dstb, NT, preferred_element_type=F32)
                yoff = jnp.dot(cg, stb, preferred_element_type=F32) * eb
                tbuf[:, pcols] = dyp * yoff - xdt * dx_state
                xbuf[:, pcols] = dxdt * xp
                rbuf[0:1, pcols] = (jnp.sum(xdt * dx_state, axis=0, keepdims=True)
                                    + cdr * jnp.sum(dst * st, axis=0, keepdims=True))
                dxs_ref[:, pcols] = dxdt * dtp + dyp * dsk_ref[:, pcols]
                dstate[p] = dst * cdr + jnp.dot(cg_t, dye, preferred_element_type=F32)
            dcbb = dcb.astype(BF16)
            dc_ref[:, gcols] = dc_acc + jnp.dot(dcbb, bg, preferred_element_type=F32)
            db_ref[:, gcols] = db_acc + lax.dot_general(dcbb, cg, TN, preferred_element_type=F32)

        sel = sel_ref[...]
        dacs = rows - cols_t.T + _split_dot(tbuf[...], sel)
        carry = _split_dot(rbuf[...], sel)[0:1]
        anti = jnp.logical_not(tm_["causal"]) | tm_["diag"]
        da = jnp.dot(anti.astype(F32), dacs, precision=HIGHEST, preferred_element_type=F32) + carry
        ddt = da * tm_["a"] + _split_dot(xbuf[...], sel)
        ddtr = jnp.where(tm_["lane"] < n_heads, ddt * jax.nn.sigmoid(tm_["pre"]), 0.0)
        ddtr_ref[...] = ddtr.astype(BF16)
        acc_b[...] += jnp.sum(ddtr, axis=0, keepdims=True)
        acc_a[...] += jnp.sum(da * tm_["dt"], axis=0, keepdims=True)

        @pl.when(i == nc - 1)
        def _():
            dbias_ref[...] = acc_b[...]
            dalog_ref[...] = acc_a[...] * tm_["a"]

    vec = pl.BlockSpec((1, LANES), lambda i: (0, 0))
    wide = pl.BlockSpec((q, sw), lambda i: (nc - 1 - i, 0))
    return _call(
        body, (dy, xbc, xbc, xbc, dt_raw, dt_bias, a_log, d_full, s_in), name="ssd_bwd", grid=(nc,),
        in_specs=[wide, wide,
                  pl.BlockSpec((q, gn), lambda i: (nc - 1 - i, bblk)),
                  pl.BlockSpec((q, gn), lambda i: (nc - 1 - i, bblk + 1)),
                  pl.BlockSpec((q, LANES), lambda i: (nc - 1 - i, 0)), vec, vec,
                  pl.BlockSpec((1, sw), lambda i: (0, 0)),
                  pl.BlockSpec((1, n_pairs, SSD_STATE, LANES), lambda i: (nc - 1 - i, 0, 0, 0))],
        out_specs=[wide, pl.BlockSpec((q, gn), lambda i: (nc - 1 - i, 0)), pl.BlockSpec((q, gn), lambda i: (nc - 1 - i, 0)),
                   pl.BlockSpec((q, LANES), lambda i: (nc - 1 - i, 0)), vec, vec],
        out_shape=[SDS((t, sw), F32), SDS((t, gn), F32), SDS((t, gn), F32), SDS((t, LANES), BF16),
                   SDS((1, LANES), F32), SDS((1, LANES), F32)],
        scratch_shapes=[pltpu.VMEM((n_pairs, SSD_STATE, LANES), F32), pltpu.VMEM((q, sw), F32), pltpu.VMEM((q, sw), F32),
                        pltpu.VMEM((8, sw), F32), pltpu.VMEM((1, LANES), F32), pltpu.VMEM((1, LANES), F32),
                        pltpu.VMEM((sw, LANES), BF16)],
        sem=("arbitrary",), comm=comm)


def _gated(y_ref, xs_ref, z_ref, dsk_ref):
    y1 = y_ref[...] + dsk_ref[...] * xs_ref[...]
    return y1, y1 * _silu(z_ref[...])


def gate_norm_fwd(y, xbc, proj, d_full, norm_w, mixed, z_block, tm):
    t, sw = y.shape
    gw = sw // SSD_GROUPS

    def body(y_ref, xs_ref, z_ref, dsk_ref, nw_ref, alias_ref, o_ref):
        _, y2 = _gated(y_ref, xs_ref, z_ref, dsk_ref)
        for g in range(SSD_GROUPS):
            cols = slice(g * gw, (g + 1) * gw)
            blk = y2[:, cols]
            r = lax.rsqrt(jnp.mean(blk * blk, axis=-1, keepdims=True) + NORM_EPS)
            o_ref[:, cols] = (blk * r * nw_ref[:, cols]).astype(BF16)

    row = pl.BlockSpec((tm, sw), lambda i: (i, 0))
    vec = pl.BlockSpec((1, sw), lambda i: (0, 0))
    return pl.pallas_call(
        body, name="gate_norm_fwd", grid=(t // tm,),
        in_specs=[row, row, pl.BlockSpec((tm, sw), lambda i: (i, z_block)), vec, vec, pl.BlockSpec(memory_space=pl.ANY)],
        out_specs=pl.BlockSpec((tm, sw), lambda i: (i, 1)),
        out_shape=SDS(mixed.shape, mixed.dtype),
        input_output_aliases={5: 0},
        compiler_params=_params(("arbitrary",)),
    )(y, xbc, proj, d_full, norm_w, mixed)


def gate_norm_bwd(dmixed, y, xbc, proj, d_full, norm_w, dproj, z_block, tm):
    t, sw = y.shape
    gw = sw // SSD_GROUPS
    nt = t // tm

    def body(d_ref, y_ref, xs_ref, z_ref, dsk_ref, nw_ref, alias_ref, dy_ref, dz_ref, dnw_ref, dd_ref, acc_d):
        i = pl.program_id(0)

        @pl.when(i == 0)
        def _():
            dnw_ref[...] = jnp.zeros_like(dnw_ref)
            acc_d[...] = jnp.zeros_like(acc_d)

        y1, y2 = _gated(y_ref, xs_ref, z_ref, dsk_ref)
        d3 = d_ref[...]
        parts = []
        for g in range(SSD_GROUPS):
            cols = slice(g * gw, (g + 1) * gw)
            blk = y2[:, cols]
            r = lax.rsqrt(jnp.mean(blk * blk, axis=-1, keepdims=True) + NORM_EPS)
            n = blk * r
            dg = d3[:, cols]
            dnw_ref[:, cols] += jnp.sum(dg * n, axis=0, keepdims=True)
            parts.append(_norm_bwd(dg * nw_ref[:, cols], n, r))
        dy2 = jnp.concatenate(parts, axis=1)
        zv = z_ref[...]
        dz_ref[...] = (dy2 * y1 * _dsilu(zv)).astype(BF16)
        dy1 = dy2 * _silu(zv)
        dy_ref[...] = dy1
        acc_d[0:1, :] += jnp.sum(dy1 * xs_ref[...], axis=0, keepdims=True)

        @pl.when(i == nt - 1)
        def _():
            dd_ref[...] = _split_dot(acc_d[...], _head_selector(sw, SSD_HEAD_DIM))[0:1]

    row = pl.BlockSpec((tm, sw), lambda i: (i, 0))
    vec = pl.BlockSpec((1, sw), lambda i: (0, 0))
    return pl.pallas_call(
        body, name="gate_norm_bwd", grid=(nt,),
        in_specs=[pl.BlockSpec((tm, sw), lambda i: (i, 1)), row, row, pl.BlockSpec((tm, sw), lambda i: (i, z_block)),
                  vec, vec, pl.BlockSpec(memory_space=pl.ANY)],
        out_specs=[row, pl.BlockSpec((tm, sw), lambda i: (i, z_block)), vec, pl.BlockSpec((1, LANES), lambda i: (0, 0))],
        out_shape=[SDS((t, sw), F32), SDS(dproj.shape, dproj.dtype), SDS((1, sw), F32), SDS((1, LANES), F32)],
        scratch_shapes=[pltpu.VMEM((8, sw), F32)],
        input_output_aliases={6: 1},
        compiler_params=_params(("arbitrary",)),
    )(dmixed, y, xbc, proj, d_full, norm_w, dproj)


GATE_BLOCK = 1
Z_BLOCK = 2
CONV_BLOCK = 2


def _tiles(t):
    mm = dict(in_proj=(min(1024, t), 1024), dt_proj=(min(512, t), LANES), out_proj=(min(512, t), 1024),
              d_mixed=(min(512, t), 2048), dh=(min(512, t), 3072), dw_out=(512, 1024), dw_main=(1024, 1024),
              dw_dt=(512, LANES))
    return min(256, t), mm


def _place():
    x, y, c = lax.axis_index("x"), lax.axis_index("y"), lax.axis_index("c")
    return x, y, c, [(1 - x, y), (x, 1 - y), (1 - x, 1 - y)]


def gather_spread(shards, layer, rows=None, carry=None):
    n = len(shards)

    def make(ins, outs, ss, rs):
        x, y, c, chips = _place()
        mine = 4 * x + 2 * y + c
        peers = [(x, y, 1 - c)] + [(px, py, c) for px, py in chips]
        sends, locals_, arrivals = [], [], []
        for a in range(n):
            def place(ref, idx):
                return ref.at[idx] if rows is None else ref.at[idx, pl.ds(rows[0], rows[1])]

            src = place(ins[a], layer)
            locals_.append(pltpu.make_async_copy(src, place(outs[a], mine), ss.at[5 * a + 4]))
            for j, (px, py, pc) in enumerate(peers):
                sends.append(_remote(src, place(outs[a], mine), ss, rs, 5 * a + j, (px, py, pc)))
                arrivals.append(_remote(src, place(outs[a], 4 * px + 2 * py + pc), ss, rs, 5 * a + j, (px, py, pc)))
        return sends, locals_, arrivals

    return Comm(list(shards) + list(carry or []), [SDS((N_DEV,) + s.shape[1:], s.dtype) for s in shards],
                {n + a: a for a in range(n)} if carry else {}, 5 * n, make)


def gather_pass_on(gathered):
    def make(ins, outs, ss, rs):
        x, y, c, chips = _place()
        sends, arrivals = [], []
        for a in range(len(outs)):
            for j, (px, py) in enumerate(chips):
                blk, other = 4 * px + 2 * py + c, 4 * px + 2 * py + (1 - c)
                sends.append(_remote(outs[a].at[blk], outs[a].at[blk], ss, rs, 3 * a + j, (x, y, 1 - c)))
                arrivals.append(_remote(outs[a].at[other], outs[a].at[other], ss, rs, 3 * a + j, (x, y, 1 - c)))
        return sends, [], arrivals

    return Comm(gathered, [SDS(g.shape, g.dtype) for g in gathered], {a: a for a in range(len(gathered))},
                3 * len(gathered), make)


def sibling_swap(sends_):
    def make(ins, outs, ss, rs):
        x, y, c, _ = _place()
        cps = [_remote(ins[a], outs[a], ss, rs, a, (x, y, 1 - c)) for a in range(len(ins))]
        return cps, [], cps

    return Comm(sends_, [SDS(s.shape, s.dtype) for s in sends_], {}, len(sends_), make)


def chips_scatter(slabs):
    def make(ins, outs, ss, rs):
        x, y, c, chips = _place()
        mychip = 2 * x + y
        sends, arrivals = [], []
        for a in range(len(ins)):
            for j, (px, py) in enumerate(chips):
                to_there = lax.rem(2 * px + py - mychip + 4, 4) - 1
                from_here = lax.rem(mychip - 2 * px - py + 4, 4) - 1
                sends.append(_remote(ins[a].at[to_there], outs[a].at[from_here], ss, rs, 3 * a + j, (px, py, c)))
                arrivals.append(_remote(ins[a].at[to_there], outs[a].at[to_there], ss, rs, 3 * a + j, (px, py, c)))
        return sends, [], arrivals

    return Comm(slabs, [SDS(s.shape, s.dtype) for s in slabs], {}, 3 * len(slabs), make)


def comm_only(comm, name):
    def body():
        pass

    return _call(body, (), name=name, grid=(), in_specs=[], out_specs=[], out_shape=[], comm=comm)[1]


def layer_fwd(x, p, nxt=None, late=None):
    t = x.shape[0]
    tm, mm = _tiles(t)
    n_heads = p["d_full"].shape[1] // SSD_HEAD_DIM
    half = nxt[0][0].shape[1] // 2 if nxt else 0
    h, r_pre = rms_fwd(x, p["pre_w"], tm)
    proj, got = mm_nn(h, p["w_main"], F32, *mm["in_proj"], "in_proj", merge_comms([
        gather_spread(late[0], late[1]) if late else None,
        gather_spread(nxt[0], nxt[2], rows=(0, half)) if nxt else None]))
    n_late = len(late[0]) if late else 0
    dt_raw, got_late = mm_nn(h, p["w_dt"], F32, *mm["dt_proj"], "dt_proj", gather_pass_on(got[:n_late]) if late else None)
    if late:
        p = dict(p, **late[2](got_late))
    mixed = pool_fwd(proj, p["mixw"], p["pscale"], tm)
    xbc = conv_fwd(proj, p["conv_w"], p["conv_b"], CONV_BLOCK, tm)
    (y, s_in), got_b = ssd_fwd(xbc, dt_raw, p["dt_bias"], p["a_log"], n_heads, gather_spread(nxt[1], nxt[2]) if nxt else None)
    mixed = gate_norm_fwd(y, xbc, proj, p["d_full"], p["norm_w"], mixed, Z_BLOCK, tm)
    out, got_a = mm_nn(mixed, p["w_out"], F32, *mm["out_proj"], "out_proj",
                       gather_spread(nxt[0], nxt[2], rows=(half, half), carry=got[n_late:]) if nxt else None)
    (x_next, r_post), gathered = post_fwd(out, x, p["post_w"], tm, gather_pass_on(got_a + got_b) if nxt else None)
    return x_next, dict(x=x, h=h, r_pre=r_pre, proj=proj, dt_raw=dt_raw, xbc=xbc, y=y, s_in=s_in, mixed=mixed,
                        out=out, r_post=r_post), gathered, p


def _pair_sums(own, got):
    return [pair_sum(o, r, min(256, o.shape[1]), "pair_sum") for o, r in zip(own, got)]


def layer_bwd(g, s, p, split_in, split_rest, pending=None, last=False):
    t = g.shape[0]
    tm, mm = _tiles(t)
    d = g.shape[1]
    n_heads = p["d_full"].shape[1] // SSD_HEAD_DIM
    d_out, d_post = post_bwd(g, s["out"], s["r_post"], p["post_w"], tm)
    dmixed, got_sib = mm_nt(d_out, p["w_out"], F32, *mm["d_mixed"], d, "d_mixed", sibling_swap(pending[1]) if pending else None)
    chip_sums = _pair_sums(pending[0], got_sib) if pending else []
    dw_out, _ = mm_tn(s["mixed"], d_out, *mm["dw_out"], "dw_out")
    dproj, dq, d_pscale, d_mixw = pool_bwd_a(dmixed, s["proj"], p["mixw"], p["pscale"], tm)
    dproj = pool_bwd_b(dq, dproj, tm)
    own_rest, send_rest = split_rest(dw_out, d_mixw)
    dy, dproj, d_norm, d_dskip = gate_norm_bwd(dmixed, s["y"], s["xbc"], s["proj"], p["d_full"], p["norm_w"], dproj,
                                               Z_BLOCK, tm)
    (dxs, db, dc, ddtr, d_dtb, d_alog), got = ssd_bwd(
        dy, s["xbc"], s["dt_raw"], p["dt_bias"], p["a_log"], p["d_full"], s["s_in"], n_heads,
        merge_comms([chips_scatter(chip_sums[:1]) if pending else None, sibling_swap(send_rest) if last else None]))
    got_first, my_sib_rest = (got[:1], got[1:]) if pending else ([], got)
    (dpre, d_convw, d_convb), got_rest = conv_bwd_a(dxs, db, dc, s["proj"], p["conv_w"], p["conv_b"], CONV_BLOCK, tm,
                                                    chips_scatter(chip_sums[1:]) if pending else None)
    dproj = conv_bwd_b(dpre, p["conv_w"], dproj, CONV_BLOCK, tm)
    dw_main, my_chips_rest = mm_tn(s["h"], dproj, *mm["dw_main"], "dw_main",
                                   chips_scatter(_pair_sums(own_rest, my_sib_rest)) if last else None)
    dw_dt, _ = mm_tn(s["h"], ddtr, *mm["dw_dt"], "dw_dt")
    own_in, send_in = split_in(dw_main, dw_dt)
    my_sib_in = comm_only(sibling_swap(send_in), "grads_to_sibling") if last else []
    dh, my_chips_in = mm_nt(dproj, p["w_main"], F32, mm["dh"][0], d, mm["dh"][1], "dh_main",
                            chips_scatter(_pair_sums(own_in, my_sib_in)) if last else None, extra=(ddtr, p["w_dt"]))
    gx, d_pre = rms_bwd(dh, s["x"], s["r_pre"], p["pre_w"], g, tm)
    small = dict(pre_w=d_pre, pscale=d_pscale, conv_w=d_convw, conv_b=d_convb, dt_bias=d_dtb, a_log=d_alog,
                 d_skip=d_dskip, norm_w=d_norm, post_w=d_post)
    done = [(got_sib, got_first + got_rest)] if pending else [None]
    if last:
        done.append((my_sib_in + my_sib_rest, my_chips_in + my_chips_rest))
    return gx, small, (own_in + own_rest, send_in + send_rest), done


def _two_level_gather(x_refs, out_slots, send_sems, recv_sems, local_sems):
    x, y, c, chips = _place()
    me, sibling = (x, y, c), (x, y, 1 - c)
    n = len(x_refs)

    def copy(a, k, block, to, src=None):
        return pltpu.make_async_remote_copy(
            src_ref=out_slots[a](*block) if src is None else src, dst_ref=out_slots[a](*block),
            send_sem=send_sems.at[7 * a + k], recv_sem=recv_sems.at[7 * a + k], device_id=to, device_id_type=MESH)

    mine = [pltpu.make_async_copy(x_refs[a], out_slots[a](*me), local_sems.at[a]) for a in range(n)]
    for cp in mine:
        cp.start()
    first = []
    for a in range(n):
        first.append(copy(a, 0, me, sibling, src=x_refs[a]))
        first += [copy(a, 1 + j, me, (*chip, c), src=x_refs[a]) for j, chip in enumerate(chips)]
    for cp in first:
        cp.start()
    passed = []
    for j, chip in enumerate(chips):
        for a in range(n):
            copy(a, 1 + j, (*chip, c), me).wait_recv()
            fwd = copy(a, 4 + j, (*chip, c), sibling)
            fwd.start()
            passed.append(fwd)
    for a in range(n):
        copy(a, 0, sibling, me).wait_recv()
        for j, chip in enumerate(chips):
            copy(a, 4 + j, (*chip, 1 - c), me).wait_recv()
    for cp in first + passed:
        cp.wait_send()
    for cp in mine:
        cp.wait()


def all_gather_hbm(shards, name):
    n = len(shards)

    def body(*refs):
        x_refs, out_refs = refs[:n], refs[n:2 * n]
        send_sems, recv_sems, local_sems = refs[2 * n:]
        slots = [lambda px, py, pc, o=o: o.at[:, 4 * px + 2 * py + pc] for o in out_refs]
        _two_level_gather(x_refs, slots, send_sems, recv_sems, local_sems)

    hbm = pl.BlockSpec(memory_space=pl.ANY)
    return pl.pallas_call(
        body, name=name,
        out_shape=[SDS((s.shape[0], N_DEV) + s.shape[1:], s.dtype) for s in shards],
        in_specs=[hbm] * n, out_specs=[hbm] * n,
        scratch_shapes=[pltpu.SemaphoreType.DMA((7 * n,)), pltpu.SemaphoreType.DMA((7 * n,)), pltpu.SemaphoreType.DMA((n,))],
    )(*shards)


def all_gather_vmem(block, name):
    r, c_ = block.shape

    def body(x_ref, out_ref, send_sems, recv_sems, local_sems):
        _two_level_gather([x_ref], [lambda px, py, pc: out_ref.at[4 * px + 2 * py + pc]], send_sems, recv_sems, local_sems)

    return pl.pallas_call(
        body, name=name, out_shape=SDS((N_DEV, r, c_), block.dtype),
        in_specs=[pl.BlockSpec(memory_space=pltpu.VMEM)], out_specs=pl.BlockSpec(memory_space=pltpu.VMEM),
        scratch_shapes=[pltpu.SemaphoreType.DMA((7,)), pltpu.SemaphoreType.DMA((7,)), pltpu.SemaphoreType.DMA((1,))],
        compiler_params=_params(),
    )(block)


def _block_tiles(cols):
    base = [(cols * i) // LANES for i in range(N_DEV)]
    ends = [-((-cols * (i + 1)) // LANES) for i in range(N_DEV)]
    return base, ends, max(e - b for b, e in zip(base, ends))


def _my_lane_offset(cols):
    me = 4 * lax.axis_index("x") + 2 * lax.axis_index("y") + lax.axis_index("c")
    return lax.rem(cols * me, LANES)


def shift_cast(w, tr):
    nl, r, cols = w.shape
    width = _block_tiles(cols)[2] * LANES

    def body(x_ref, o_ref, pad):
        pad[:, width - LANES:] = jnp.zeros((tr, LANES), F32)
        pad[:, :cols] = x_ref[...]
        o_ref[...] = pltpu.roll(pad[...], _my_lane_offset(cols), 1).astype(BF16)

    assert width - LANES <= cols
    return pl.pallas_call(
        body, name="shift_cast", grid=(nl, r // tr),
        in_specs=[pl.BlockSpec((pl.Squeezed(), tr, cols), lambda l, i: (l, i, 0))],
        out_specs=pl.BlockSpec((pl.Squeezed(), tr, width), lambda l, i: (l, i, 0)),
        out_shape=SDS((nl, r, width), BF16), scratch_shapes=[pltpu.VMEM((tr, width), F32)],
        compiler_params=_params(("arbitrary", "arbitrary")))(w)


def assemble_w_in(blocks, cols, n_tail, tr):
    _, r, width = blocks.shape
    base, ends, _ = _block_tiles(cols)
    total = ends[-1]
    main_tiles = (N_DEV * cols - n_tail) // LANES
    assert main_tiles == total - 1 and (N_DEV * cols - n_tail) % LANES == 0

    def body(b_ref, main_ref, tail_ref):
        for tile in range(total):
            parts = [b_ref[i, :, (tile - base[i]) * LANES:(tile - base[i] + 1) * LANES]
                     for i in range(N_DEV) if base[i] <= tile < ends[i]]
            val = parts[0] if len(parts) == 1 else parts[0] + parts[1]
            if tile < main_tiles:
                main_ref[:, tile * LANES:(tile + 1) * LANES] = val
            else:
                tail_ref[...] = val

    return pl.pallas_call(
        body, name="assemble_w_in", grid=(r // tr,),
        in_specs=[pl.BlockSpec((N_DEV, tr, width), lambda i: (0, i, 0))],
        out_specs=[pl.BlockSpec((tr, main_tiles * LANES), lambda i: (i, 0)), pl.BlockSpec((tr, LANES), lambda i: (i, 0))],
        out_shape=[SDS((r, main_tiles * LANES), blocks.dtype), SDS((r, LANES), blocks.dtype)],
        compiler_params=_params(("arbitrary",)),
    )(blocks)


def grad_blocks(dw_main, dw_tail, cols, tr):
    r = dw_main.shape[0]
    base, _, tpb = _block_tiles(cols)
    width = tpb * LANES

    def body(m_ref, t_ref, own_ref, send_ref):
        cat = jnp.concatenate([m_ref[...], t_ref[...]], axis=1)
        south = lax.axis_index("c") == 0
        for k in range(N_DEV // 2):
            a = cat[:, base[2 * k] * LANES:base[2 * k] * LANES + width]
            b = cat[:, base[2 * k + 1] * LANES:base[2 * k + 1] * LANES + width]
            own_ref[k] = jnp.where(south, a, b)
            send_ref[k] = jnp.where(south, b, a).astype(BF16)

    return pl.pallas_call(
        body, name="grad_blocks", grid=(r // tr,),
        in_specs=[pl.BlockSpec((tr, dw_main.shape[1]), lambda i: (i, 0)), pl.BlockSpec((tr, LANES), lambda i: (i, 0))],
        out_specs=[pl.BlockSpec((N_DEV // 2, tr, width), lambda i: (0, i, 0))] * 2,
        out_shape=[SDS((N_DEV // 2, r, width), F32), SDS((N_DEV // 2, r, width), BF16)],
        compiler_params=_params(("arbitrary",)),
    )(dw_main, dw_tail)


def _adamw(w, g, m, v):
    m = ADAM_B1 * m + (1.0 - ADAM_B1) * g
    v = ADAM_B2 * v + (1.0 - ADAM_B2) * jnp.square(g)
    m_hat = m / (1.0 - ADAM_B1 ** ADAM_STEP)
    v_hat = v / (1.0 - ADAM_B2 ** ADAM_STEP)
    delta = -ADAM_LR * (m_hat / (jnp.sqrt(v_hat) + ADAM_EPS) + ADAM_WD * w)
    return delta, m, v


def _my_chip():
    return 2 * lax.axis_index("x") + lax.axis_index("y")


def pair_sum(own, got, tr, name):
    k, r, c_ = own.shape
    others = lax.rem(_my_chip() + 1 + jnp.arange(k - 1, dtype=jnp.int32), k)

    def body(others_ref, a_ref, b_ref, o_ref):
        o_ref[...] = (a_ref[...] + b_ref[...].astype(F32)).astype(BF16)

    src = pl.BlockSpec((pl.Squeezed(), tr, c_), lambda s, i, oth: (oth[s], i, 0))
    return pl.pallas_call(
        body, name=name, out_shape=SDS((k - 1, r, c_), BF16),
        grid_spec=pltpu.PrefetchScalarGridSpec(
            num_scalar_prefetch=1, grid=(k - 1, r // tr), in_specs=[src, src],
            out_specs=pl.BlockSpec((pl.Squeezed(), tr, c_), lambda s, i, oth: (s, i, 0))),
        compiler_params=_params(("arbitrary", "arbitrary")),
    )(others, own, got)


def reduce_adam(own, got_sibling, got_chips, w, m, v, prev, layer, tr, name, shifted=False):
    nl, r, cols = w.shape
    c_ = own.shape[-1]
    n_scratch = 1 if shifted else 0
    chip = jnp.reshape(_my_chip(), (1,)).astype(jnp.int32)

    def body(chip_ref, own_ref, sib_ref, c0_ref, c1_ref, c2_ref, w_ref, m_ref, v_ref, *rest):
        g_ref, d_ref, nm_ref, nv_ref = rest[len(rest) - n_scratch - 4:len(rest) - n_scratch]
        g = (own_ref[...] + sib_ref[...].astype(F32) + c0_ref[...].astype(F32) + c1_ref[...].astype(F32)
             + c2_ref[...].astype(F32))
        if shifted:
            rest[-1][...] = pltpu.roll(g, c_ - _my_lane_offset(cols), 1)
            g = rest[-1][:, :cols]
        delta, nm, nv = _adamw(w_ref[...], g, m_ref[...], v_ref[...])
        g_ref[...] = g
        d_ref[...] = delta
        nm_ref[...] = nm
        nv_ref[...] = nv

    mine = pl.BlockSpec((pl.Squeezed(), tr, c_), lambda i, ch: (ch[0], i, 0))
    lay = pl.BlockSpec((pl.Squeezed(), tr, cols), lambda i, ch: (layer, i, 0))
    chips = [pl.BlockSpec((pl.Squeezed(), tr, c_), lambda i, ch, s=s: (s, i, 0)) for s in range(3)]
    in_specs = [mine, mine] + chips + [lay, lay, lay]
    args = [chip, own, got_sibling, got_chips, got_chips, got_chips, w, m, v]
    aliases = {}
    if prev is not None:
        in_specs += [pl.BlockSpec(memory_space=pl.ANY)] * 4
        aliases = {len(args) + k: k for k in range(4)}
        args += list(prev)
    return pl.pallas_call(
        body, name=name, out_shape=[SDS((nl, r, cols), F32)] * 4, input_output_aliases=aliases,
        grid_spec=pltpu.PrefetchScalarGridSpec(
            num_scalar_prefetch=1, grid=(r // tr,), in_specs=in_specs, out_specs=[lay] * 4,
            scratch_shapes=[pltpu.VMEM((tr, c_), F32)] * n_scratch),
        compiler_params=_params(("arbitrary",)),
    )(*args)


def sum_devices(packs):
    n, r, c_ = packs.shape

    def body(p_ref, o_ref):
        acc = p_ref[0]
        for k in range(1, n):
            acc = acc + p_ref[k]
        o_ref[...] = acc

    return pl.pallas_call(body, name="sum_devices", out_shape=SDS((r, c_), F32), compiler_params=_params())(packs)


def adam_small(w, g, m, v):
    def body(w_ref, g_ref, m_ref, v_ref, d_ref, nm_ref, nv_ref):
        delta, nm, nv = _adamw(w_ref[...], g_ref[...], m_ref[...], v_ref[...])
        d_ref[...] = delta
        nm_ref[...] = nm
        nv_ref[...] = nv

    return pl.pallas_call(body, name="adam_small", out_shape=[SDS(w.shape, F32)] * 3, compiler_params=_params())(w, g, m, v)


SMALL = ("pre_norm_w", "pool_scale", "conv_b", "dt_bias", "a_log", "d_skip", "_pad", "ssd_norm_w", "post_norm_w", "conv_w")


def _pack(parts):
    flat = jnp.concatenate([parts[k] for k in SMALL], axis=1).reshape(-1, LANES)
    return jnp.pad(flat, ((0, (-flat.shape[0]) % 8), (0, 0)))


def _unpack(pack, sizes, nl):
    total = sum(sizes[k] for k in SMALL)
    flat = pack[: nl * total // LANES].reshape(nl, total)
    out, o = {}, 0
    for k in SMALL:
        out[k] = flat[:, o:o + sizes[k]]
        o += sizes[k]
    return out


def kernel(x, pre_norm_w, w_in, pool_mix_w, pool_scale, conv_w, conv_b, dt_bias, a_log, d_skip, ssd_norm_w, w_out, post_norm_w, loss_target, m_pre_norm_w, m_w_in, m_pool_mix_w, m_pool_scale, m_conv_w, m_conv_b, m_dt_bias, m_a_log, m_d_skip, m_ssd_norm_w, m_w_out, m_post_norm_w, v_pre_norm_w, v_w_in, v_pool_mix_w, v_pool_scale, v_conv_w, v_conv_b, v_dt_bias, v_a_log, v_d_skip, v_ssd_norm_w, v_w_out, v_post_norm_w):
    cx, cy, cc = lax.axis_index("x"), lax.axis_index("y"), lax.axis_index("c")
    me = 4 * cx + 2 * cy + cc
    mychip = 2 * cx + cy
    nl, d, cols = w_in.shape
    t = x.shape[1]
    n_heads = a_log.shape[1]
    sw = n_heads * SSD_HEAD_DIM
    pw = pool_scale.shape[1]
    cd = conv_b.shape[1]
    ng, gsh, gw = pool_mix_w.shape[1:]
    e_main = N_DEV * cols - n_heads
    assert x.shape[0] == 1 and pw == sw and cd == sw + 2 * SSD_GROUPS * SSD_STATE and e_main == 2 * pw + sw + cd
    assert 2 * pw + sw == CONV_BLOCK * cd and n_heads <= LANES and t % SSD_CHUNK == 0 and gsh * N_DEV == gw
    tm, _ = _tiles(t)

    shards_a, shards_b = [shift_cast(w_in, tm)], [w_out.astype(BF16), pool_mix_w.astype(BF16), conv_w]
    pad_h = ((0, 0), (0, LANES - n_heads))

    def params_a(l, g_in):
        w_main, w_dt = assemble_w_in(g_in, cols, n_heads, tm)
        return dict(pre_w=pre_norm_w[l:l + 1], w_main=w_main, w_dt=w_dt, pscale=pool_scale[l:l + 1], conv_b=conv_b[l:l + 1],
                    dt_bias=jnp.pad(dt_bias[l:l + 1], pad_h), a_log=jnp.pad(a_log[l:l + 1], pad_h),
                    d_full=jnp.repeat(d_skip[l:l + 1], SSD_HEAD_DIM, axis=1), norm_w=ssd_norm_w[l:l + 1],
                    post_w=post_norm_w[l:l + 1])

    def params_b(g_out, g_mix, g_conv):
        return dict(mixw=g_mix.transpose(1, 0, 2, 3).reshape(ng, gw, gw), conv_w=g_conv.transpose(1, 0, 2).reshape(CONV_WIDTH, cd),
                    w_out=g_out.reshape(N_DEV * w_out.shape[1], d))

    xs = x[0]
    saved, params = [], []
    p = params_a(0, all_gather_hbm([shards_a[0][:1]], "gather_w_in")[0][0])
    for l in range(nl):
        xs, s, gathered, p = layer_fwd(xs, p, (shards_a, shards_b, l + 1) if l + 1 < nl else None,
                                       (shards_b, 0, lambda got: params_b(*got)) if l == 0 else None)
        saved.append(s)
        params.append(p)
        if l + 1 < nl:
            p = dict(params_a(l + 1, gathered[0]), **params_b(*gathered[1:]))
    loss_part, g = loss_grad(xs, loss_target[0], tm)
    loss = lax.psum(loss_part[0, 0], ("x", "y", "c"))

    big = {"w_in": (w_in, m_w_in, v_w_in), "w_out": (w_out, m_w_out, v_w_out),
           "pool_mix_w": tuple(a.reshape(nl, ng * gsh, gw) for a in (pool_mix_w, m_pool_mix_w, v_pool_mix_w))}
    names = list(big)
    big_out = {k: None for k in big}
    small_g = [None] * nl

    def apply(layer, own, got_sib, got_chips):
        for k, o, gs_, gc in zip(names, own, got_sib, got_chips):
            wk, mk, vk = big[k]
            big_out[k] = reduce_adam(o, gs_, gc, wk, mk, vk, big_out[k], layer, min(256, wk.shape[1]), "reduce_adam_" + k,
                                     shifted=(k == "w_in"))

    def split_in(dw_main, dw_dt):
        own, send = grad_blocks(dw_main, dw_dt, cols, min(128, d))
        return [own], [send]

    def split_rest(dw_out, d_mixw):
        halves = [lambda ci: lax.dynamic_index_in_dim(dw_out.reshape(4, 2, -1, d), ci, 1, keepdims=False),
                  lambda ci: lax.dynamic_index_in_dim(
                      d_mixw.reshape(ng, 4, 2, gsh, gw), ci, 2, keepdims=False).transpose(1, 0, 2, 3).reshape(4, ng * gsh, gw)]
        return [h(cc) for h in halves], [h(1 - cc).astype(BF16) for h in halves]

    pending = None
    for l in reversed(range(nl)):
        g, gr, mine, done = layer_bwd(g, saved[l], params[l], split_in, split_rest, pending, last=(l == 0))
        if pending is not None:
            apply(l + 1, pending[0], *done[0])
        if l == 0:
            apply(0, mine[0], *done[1])
        pending = mine
        small_g[l] = dict(pre_norm_w=gr["pre_w"], pool_scale=gr["pscale"], conv_b=gr["conv_b"], dt_bias=gr["dt_bias"][:, :n_heads],
                          a_log=gr["a_log"][:, :n_heads], d_skip=gr["d_skip"][:, :n_heads], _pad=jnp.zeros((1, LANES - 3 * n_heads), F32),
                          ssd_norm_w=gr["norm_w"], post_norm_w=gr["post_w"], conv_w=gr["conv_w"].reshape(1, CONV_WIDTH * cd))

    sizes = {k: small_g[0][k].shape[1] for k in SMALL}
    gsum = sum_devices(all_gather_vmem(_pack({k: jnp.concatenate([sg[k] for sg in small_g], axis=0) for k in SMALL}),
                                       "gather_small_grads"))
    gs = _unpack(gsum, sizes, nl)
    csh = conv_w.shape[2]
    gs["conv_w"] = lax.dynamic_slice_in_dim(gs["conv_w"].reshape(nl, CONV_WIDTH, cd), me * csh, csh, axis=2).reshape(nl, -1)
    lsizes = dict(sizes, conv_w=CONV_WIDTH * csh)
    zpad = jnp.zeros((nl, sizes["_pad"]), F32)

    def local(pre, scale, cb, dtb, al, dsk, nw, post, cw):
        return _pack(dict(pre_norm_w=pre, pool_scale=scale, conv_b=cb, dt_bias=dtb, a_log=al, d_skip=dsk, _pad=zpad,
                          ssd_norm_w=nw, post_norm_w=post, conv_w=cw.reshape(nl, -1)))

    wp = local(pre_norm_w, pool_scale, conv_b, dt_bias, a_log, d_skip, ssd_norm_w, post_norm_w, conv_w)
    mp = local(m_pre_norm_w, m_pool_scale, m_conv_b, m_dt_bias, m_a_log, m_d_skip, m_ssd_norm_w, m_post_norm_w, m_conv_w)
    vp = local(v_pre_norm_w, v_pool_scale, v_conv_b, v_dt_bias, v_a_log, v_d_skip, v_ssd_norm_w, v_post_norm_w, v_conv_w)
    small_out = [gs] + [_unpack(o, lsizes, nl) for o in adam_small(wp, _pack(gs), mp, vp)]

    def leaf(kind, name):
        if name in big:
            return big_out[name][kind].reshape(big[name][0].shape if name != "pool_mix_w" else pool_mix_w.shape)
        val = small_out[kind][name]
        return val.reshape(conv_w.shape) if name == "conv_w" else val

    order = ("pre_norm_w", "w_in", "pool_mix_w", "pool_scale", "conv_w", "conv_b", "dt_bias", "a_log", "d_skip",
             "ssd_norm_w", "w_out", "post_norm_w")
    return (loss, g[None]) + tuple(leaf(kind, name) for kind in range(4) for name in order)
```

```python
import jax
import jax.numpy as jnp
from jax import lax
from jax.experimental import pallas as pl
from jax.experimental.pallas import tpu as pltpu

F32 = jnp.float32
BF16 = jnp.bfloat16
SDS = jax.ShapeDtypeStruct
MESH = pl.DeviceIdType.MESH
HIGHEST = lax.Precision.HIGHEST

NORM_EPS = 1e-6
POOL_WINDOWS = (2, 4, 8, 16)
POOL_HALO = 16
CONV_WIDTH = 4
CONV_HALO = 8
SSD_CHUNK = 128
SSD_HEAD_DIM = 64
SSD_STATE = 128
SSD_GROUPS = 4
LANES = 128
N_DEV = 8

ADAM_LR = 0.001
ADAM_B1 = 0.9
ADAM_B2 = 0.999
ADAM_EPS = 1e-08
ADAM_WD = 0.01
ADAM_STEP = 10

VMEM_LIMIT = 56 * 1024 * 1024

NT = (((1,), (1,)), ((), ()))
TN = (((0,), (0,)), ((), ()))


def _params(sem=None):
    kw = dict(vmem_limit_bytes=VMEM_LIMIT)
    if sem is not None:
        kw["dimension_semantics"] = sem
    return pltpu.CompilerParams(**kw)


def _silu(v):
    return v * jax.nn.sigmoid(v)


def _dsilu(v):
    s = jax.nn.sigmoid(v)
    return s * (1.0 + v * (1.0 - s))


def _split_dot(v, sel):
    hi = v.astype(BF16)
    lo = (v - hi.astype(F32)).astype(BF16)
    return (jnp.dot(hi, sel, preferred_element_type=F32) + jnp.dot(lo, sel, preferred_element_type=F32))


def _head_selector(width, per):
    ch = lax.broadcasted_iota(jnp.int32, (width, LANES), 0)
    hd = lax.broadcasted_iota(jnp.int32, (width, LANES), 1)
    return jnp.where((ch >= hd * per) & (ch < (hd + 1) * per), 1.0, 0.0).astype(BF16)


class Comm:
    def __init__(self, inputs, out_shapes, aliases, n_sems, make):
        self.inputs, self.out_shapes, self.aliases, self.n_sems, self.make = list(inputs), list(out_shapes), dict(aliases), n_sems, make


def _remote(src, dst, send_sems, recv_sems, k, peer):
    return pltpu.make_async_remote_copy(src_ref=src, dst_ref=dst, send_sem=send_sems.at[k], recv_sem=recv_sems.at[k],
                                        device_id=peer, device_id_type=MESH)


class _SemRange:
    def __init__(self, sems, start):
        self.sems, self.start = sems, start

    @property
    def at(self):
        return self

    def __getitem__(self, k):
        return self.sems.at[self.start + k]


def merge_comms(comms):
    comms = [c for c in comms if c is not None]
    if len(comms) <= 1:
        return comms[0] if comms else None
    aliases, i_off, o_off = {}, 0, 0
    for c in comms:
        aliases.update({i_off + k: o_off + v for k, v in c.aliases.items()})
        i_off, o_off = i_off + len(c.inputs), o_off + len(c.out_shapes)

    def make(ins, outs, ss, rs):
        sends, locals_, arrivals, i0, o0, s0 = [], [], [], 0, 0, 0
        for c in comms:
            s, l, a = c.make(ins[i0:i0 + len(c.inputs)], outs[o0:o0 + len(c.out_shapes)], _SemRange(ss, s0), _SemRange(rs, s0))
            sends, locals_, arrivals = sends + s, locals_ + l, arrivals + a
            i0, o0, s0 = i0 + len(c.inputs), o0 + len(c.out_shapes), s0 + c.n_sems
        return sends, locals_, arrivals

    return Comm(sum((c.inputs for c in comms), []), sum((c.out_shapes for c in comms), []), aliases,
                sum(c.n_sems for c in comms), make)


def _call(body, args, *, name, grid, in_specs, out_specs, out_shape, scratch_shapes=(), sem=None, comm=None):
    in_specs, out_specs, out_shape = list(in_specs), list(out_specs), list(out_shape)
    if comm is None:
        outs = pl.pallas_call(body, name=name, grid=grid, in_specs=in_specs, out_specs=out_specs, out_shape=out_shape,
                              scratch_shapes=list(scratch_shapes), compiler_params=_params(sem))(*args)
        return list(outs), []
    ni, no, nci, nco, ns = len(in_specs), len(out_specs), len(comm.inputs), len(comm.out_shapes), len(scratch_shapes)
    hbm = pl.BlockSpec(memory_space=pl.ANY)

    def hosted(*refs):
        ins, cins = refs[:ni], refs[ni:ni + nci]
        outs, couts = refs[ni + nci:ni + nci + no], refs[ni + nci + no:ni + nci + no + nco]
        scratch = refs[ni + nci + no + nco:]
        sends, locals_, arrivals = comm.make(cins, couts, scratch[ns], scratch[ns + 1])
        first = last = None if grid else True
        for axis, extent in enumerate(grid):
            pid = pl.program_id(axis)
            first = (pid == 0) if first is None else first & (pid == 0)
            last = (pid == extent - 1) if last is None else last & (pid == extent - 1)

        @pl.when(first)
        def _():
            for cp in locals_ + sends:
                cp.start()

        body(*ins, *outs, *scratch[:ns])

        @pl.when(last)
        def _():
            for cp in arrivals:
                cp.wait_recv()
            for cp in sends:
                cp.wait_send()
            for cp in locals_:
                cp.wait()

    outs = pl.pallas_call(
        hosted, name=name, grid=grid, in_specs=in_specs + [hbm] * nci, out_specs=out_specs + [hbm] * nco,
        out_shape=out_shape + comm.out_shapes,
        scratch_shapes=list(scratch_shapes) + [pltpu.SemaphoreType.DMA((comm.n_sems,)), pltpu.SemaphoreType.DMA((comm.n_sems,))],
        input_output_aliases={ni + k: no + v for k, v in comm.aliases.items()},
        compiler_params=_params(sem),
    )(*args, *comm.inputs)
    return list(outs[:no]), list(outs[no:])


def rms_fwd(x, w, tm):
    t, d = x.shape

    def body(x_ref, w_ref, h_ref, r_ref):
        xv = x_ref[...]
        r = lax.rsqrt(jnp.mean(xv * xv, axis=-1, keepdims=True) + NORM_EPS)
        h_ref[...] = (xv * r * w_ref[...]).astype(BF16)
        r_ref[...] = r

    return pl.pallas_call(
        body, name="rms_fwd", grid=(t // tm,),
        in_specs=[pl.BlockSpec((tm, d), lambda i: (i, 0)), pl.BlockSpec((1, d), lambda i: (0, 0))],
        out_specs=[pl.BlockSpec((tm, d), lambda i: (i, 0)), pl.BlockSpec((tm, 1), lambda i: (i, 0))],
        out_shape=[SDS((t, d), BF16), SDS((t, 1), F32)],
        compiler_params=_params(("arbitrary",)),
    )(x, w)


def post_fwd(out, x, w, tm, comm=None):
    t, d = x.shape

    def body(o_ref, x_ref, w_ref, y_ref, r_ref):
        ov = o_ref[...]
        r = lax.rsqrt(jnp.mean(ov * ov, axis=-1, keepdims=True) + NORM_EPS)
        y_ref[...] = x_ref[...] + ov * r * w_ref[...]
        r_ref[...] = r

    return _call(
        body, (out, x, w), name="post_fwd", grid=(t // tm,),
        in_specs=[pl.BlockSpec((tm, d), lambda i: (i, 0)), pl.BlockSpec((tm, d), lambda i: (i, 0)),
                  pl.BlockSpec((1, d), lambda i: (0, 0))],
        out_specs=[pl.BlockSpec((tm, d), lambda i: (i, 0)), pl.BlockSpec((tm, 1), lambda i: (i, 0))],
        out_shape=[SDS((t, d), F32), SDS((t, 1), F32)], sem=("arbitrary",), comm=comm)


def _norm_bwd(g_n, n, r):
    return r * (g_n - n * jnp.mean(g_n * n, axis=-1, keepdims=True))


def post_bwd(g, out, r, w, tm):
    t, d = g.shape

    def body(g_ref, o_ref, r_ref, w_ref, do_ref, dw_ref):
        i = pl.program_id(0)
        gv = g_ref[...]
        rv = r_ref[...]
        n = o_ref[...] * rv
        part = jnp.sum(gv * n, axis=0, keepdims=True)

        @pl.when(i == 0)
        def _():
            dw_ref[...] = part

        @pl.when(i > 0)
        def _():
            dw_ref[...] += part

        do_ref[...] = _norm_bwd(gv * w_ref[...], n, rv).astype(BF16)

    return pl.pallas_call(
        body, name="post_bwd", grid=(t // tm,),
        in_specs=[pl.BlockSpec((tm, d), lambda i: (i, 0)), pl.BlockSpec((tm, d), lambda i: (i, 0)),
                  pl.BlockSpec((tm, 1), lambda i: (i, 0)), pl.BlockSpec((1, d), lambda i: (0, 0))],
        out_specs=[pl.BlockSpec((tm, d), lambda i: (i, 0)), pl.BlockSpec((1, d), lambda i: (0, 0))],
        out_shape=[SDS((t, d), BF16), SDS((1, d), F32)],
        compiler_params=_params(("arbitrary",)),
    )(g, out, r, w)


def rms_bwd(dh, x, r, w, g, tm):
    t, d = x.shape

    def body(a_ref, x_ref, r_ref, w_ref, g_ref, gx_ref, dw_ref):
        i = pl.program_id(0)
        dh = a_ref[...]
        rv = r_ref[...]
        n = x_ref[...] * rv
        part = jnp.sum(dh * n, axis=0, keepdims=True)

        @pl.when(i == 0)
        def _():
            dw_ref[...] = part

        @pl.when(i > 0)
        def _():
            dw_ref[...] += part

        gx_ref[...] = g_ref[...] + _norm_bwd(dh * w_ref[...], n, rv)

    row = pl.BlockSpec((tm, d), lambda i: (i, 0))
    return pl.pallas_call(
        body, name="rms_bwd", grid=(t // tm,),
        in_specs=[row, row, pl.BlockSpec((tm, 1), lambda i: (i, 0)), pl.BlockSpec((1, d), lambda i: (0, 0)), row],
        out_specs=[row, pl.BlockSpec((1, d), lambda i: (0, 0))],
        out_shape=[SDS((t, d), F32), SDS((1, d), F32)],
        compiler_params=_params(("arbitrary",)),
    )(dh, x, r, w, g)


def loss_grad(y, target, tm):
    t, d = y.shape

    def body(y_ref, t_ref, l_ref, g_ref):
        i = pl.program_id(0)
        err = y_ref[...] - t_ref[...]
        g_ref[...] = err / d
        part = 0.5 * jnp.sum(jnp.mean(err * err, axis=-1, keepdims=True), axis=0, keepdims=True)

        @pl.when(i == 0)
        def _():
            l_ref[...] = part

        @pl.when(i > 0)
        def _():
            l_ref[...] += part

    row = pl.BlockSpec((tm, d), lambda i: (i, 0))
    return pl.pallas_call(
        body, name="loss_grad", grid=(t // tm,), in_specs=[row, row],
        out_specs=[pl.BlockSpec((1, 1), lambda i: (0, 0)), row],
        out_shape=[SDS((1, 1), F32), SDS((t, d), F32)],
        compiler_params=_params(("arbitrary",)),
    )(y, target)


def mm_nn(a, b, out_dtype, tm, tn, name, comm=None):
    m, k = a.shape
    n = b.shape[1]

    def body(a_ref, b_ref, o_ref):
        o_ref[...] = jnp.dot(a_ref[...], b_ref[...], preferred_element_type=F32).astype(out_dtype)

    outs, couts = _call(
        body, (a, b), name=name, grid=(n // tn, m // tm),
        in_specs=[pl.BlockSpec((tm, k), lambda j, i: (i, 0)), pl.BlockSpec((k, tn), lambda j, i: (0, j))],
        out_specs=[pl.BlockSpec((tm, tn), lambda j, i: (i, j))],
        out_shape=[SDS((m, n), out_dtype)], sem=("arbitrary", "arbitrary"), comm=comm)
    return outs[0], couts


def mm_nt(a, b, out_dtype, tm, tn, tk, name, comm=None, extra=None):
    m, k = a.shape
    n = b.shape[0]
    nk = k // tk

    def body(a_ref, b_ref, *rest):
        o_ref, acc_ref = rest[-2:]
        kk = pl.program_id(2)
        part = lax.dot_general(a_ref[...], b_ref[...], NT, preferred_element_type=F32)
        if nk == 1:
            if extra is not None:
                part = part + lax.dot_general(rest[0][...], rest[1][...], NT, preferred_element_type=F32)
            o_ref[...] = part.astype(out_dtype)
        else:
            @pl.when(kk == 0)
            def _():
                if extra is None:
                    acc_ref[...] = part
                else:
                    acc_ref[...] = part + lax.dot_general(rest[0][...], rest[1][...], NT, preferred_element_type=F32)

            @pl.when(kk > 0)
            def _():
                acc_ref[...] += part

            @pl.when(kk == nk - 1)
            def _():
                o_ref[...] = acc_ref[...].astype(out_dtype)

    more_specs = [] if extra is None else [pl.BlockSpec((tm, extra[0].shape[1]), lambda i, j, kk: (i, 0)),
                                           pl.BlockSpec((tn, extra[1].shape[1]), lambda i, j, kk: (j, 0))]
    outs, couts = _call(
        body, (a, b) + tuple(extra or ()), name=name, grid=(m // tm, n // tn, nk),
        in_specs=[pl.BlockSpec((tm, tk), lambda i, j, kk: (i, kk)), pl.BlockSpec((tn, tk), lambda i, j, kk: (j, kk))] + more_specs,
        out_specs=[pl.BlockSpec((tm, tn), lambda i, j, kk: (i, j))],
        out_shape=[SDS((m, n), out_dtype)],
        scratch_shapes=[pltpu.VMEM((tm, tn) if nk > 1 else (8, LANES), F32)],
        sem=("arbitrary", "arbitrary", "arbitrary"), comm=comm)
    return outs[0], couts


def mm_tn(a, b, tm, tn, name, comm=None):
    t, m = a.shape
    n = b.shape[1]

    def body(a_ref, b_ref, o_ref):
        o_ref[...] = lax.dot_general(a_ref[...], b_ref[...], TN, preferred_element_type=F32)

    outs, couts = _call(
        body, (a, b), name=name, grid=(m // tm, n // tn),
        in_specs=[pl.BlockSpec((t, tm), lambda i, j: (0, i)), pl.BlockSpec((t, tn), lambda i, j: (0, j))],
        out_specs=[pl.BlockSpec((tm, tn), lambda i, j: (i, j))],
        out_shape=[SDS((m, n), F32)], sem=("arbitrary", "arbitrary"), comm=comm)
    return outs[0], couts


def _window_sums(ext, n_rows, lookahead):
    def sh(v, k):
        return pltpu.roll(v, (n_rows - k) if lookahead else k, 0)
    s2 = ext + sh(ext, 1)
    s4 = s2 + sh(s2, 2)
    s8 = s4 + sh(s4, 4)
    s16 = s8 + sh(s8, 8)
    return (s2, s4, s8, s16)


def _pool_counts(i, tm, w):
    tpos = i * tm + lax.broadcasted_iota(jnp.int32, (tm, 1), 0)
    return jnp.minimum(tpos + 1, w).astype(F32)


def _pooled(uc_ref, up_ref, i, tm):
    cur = uc_ref[...]
    prev = jnp.where(i > 0, up_ref[...], 0.0)
    ext = jnp.concatenate([prev, cur], axis=0)
    return cur, _window_sums(ext, tm + POOL_HALO, False)


def pool_fwd(proj, mixw, scale, tm):
    t = proj.shape[0]
    pw = scale.shape[1]
    gw = pw // len(POOL_WINDOWS)
    nh = tm // POOL_HALO

    def body(uc_ref, up_ref, g_ref, w_ref, s_ref, o_ref):
        i = pl.program_id(0)
        cur, sums = _pooled(uc_ref, up_ref, i, tm)
        for g, w in enumerate(POOL_WINDOWS):
            cols = slice(g * gw, (g + 1) * gw)
            pooled = sums[g][POOL_HALO:, cols] / _pool_counts(i, tm, w) - cur[:, cols]
            mixed = jnp.dot(pooled.astype(BF16), w_ref[g], preferred_element_type=F32)
            o_ref[:, cols] = (mixed * s_ref[:, cols] * _silu(g_ref[:, cols])).astype(BF16)

    return pl.pallas_call(
        body, name="pool_fwd", grid=(t // tm,),
        in_specs=[pl.BlockSpec((tm, pw), lambda i: (i, 0)),
                  pl.BlockSpec((POOL_HALO, pw), lambda i: (jnp.maximum(i * nh - 1, 0), 0)),
                  pl.BlockSpec((tm, pw), lambda i: (i, 1)),
                  pl.BlockSpec(mixw.shape, lambda i: (0, 0, 0)),
                  pl.BlockSpec((1, pw), lambda i: (0, 0))],
        out_specs=pl.BlockSpec((tm, pw), lambda i: (i, 0)),
        out_shape=SDS((t, 2 * pw), BF16),
        compiler_params=_params(("arbitrary",)),
    )(proj, proj, proj, mixw, scale)


def pool_bwd_a(dmixed, proj, mixw, scale, tm):
    t, e = proj.shape
    pw = scale.shape[1]
    ng = len(POOL_WINDOWS)
    gw = pw // ng
    nh = tm // POOL_HALO

    def body(dy_ref, uc_ref, up_ref, g_ref, w_ref, s_ref, dg_ref, dq_ref, ds_ref, dw_ref):
        i = pl.program_id(0)

        @pl.when(i == 0)
        def _():
            ds_ref[...] = jnp.zeros_like(ds_ref)
            dw_ref[...] = jnp.zeros_like(dw_ref)

        cur, sums = _pooled(uc_ref, up_ref, i, tm)
        for g, w in enumerate(POOL_WINDOWS):
            cols = slice(g * gw, (g + 1) * gw)
            cnt = _pool_counts(i, tm, w)
            pooled = (sums[g][POOL_HALO:, cols] / cnt - cur[:, cols]).astype(BF16)
            mixed = jnp.dot(pooled, w_ref[g], preferred_element_type=F32)
            gate = g_ref[:, cols]
            dy = dy_ref[:, cols]
            sc = s_ref[:, cols]
            dg_ref[:, cols] = (dy * mixed * sc * _dsilu(gate)).astype(BF16)
            ds = dy * _silu(gate)
            ds_ref[:, cols] += jnp.sum(ds * mixed, axis=0, keepdims=True)
            dmix = (ds * sc).astype(BF16)
            dw_ref[g] += lax.dot_general(pooled, dmix, TN, preferred_element_type=F32)
            dq_ref[:, cols] = lax.dot_general(dmix, w_ref[g], NT, preferred_element_type=F32) / cnt

    return pl.pallas_call(
        body, name="pool_bwd_a", grid=(t // tm,),
        in_specs=[pl.BlockSpec((tm, pw), lambda i: (i, 0)),
                  pl.BlockSpec((tm, pw), lambda i: (i, 0)),
                  pl.BlockSpec((POOL_HALO, pw), lambda i: (jnp.maximum(i * nh - 1, 0), 0)),
                  pl.BlockSpec((tm, pw), lambda i: (i, 1)),
                  pl.BlockSpec(mixw.shape, lambda i: (0, 0, 0)),
                  pl.BlockSpec((1, pw), lambda i: (0, 0))],
        out_specs=[pl.BlockSpec((tm, pw), lambda i: (i, 1)),
                   pl.BlockSpec((tm, pw), lambda i: (i, 0)),
                   pl.BlockSpec((1, pw), lambda i: (0, 0)),
                   pl.BlockSpec((ng, gw, gw), lambda i: (0, 0, 0))],
        out_shape=[SDS((t, e), BF16), SDS((t, pw), F32), SDS((1, pw), F32), SDS((ng, gw, gw), F32)],
        compiler_params=_params(("arbitrary",)),
    )(dmixed, proj, proj, proj, mixw, scale)


def pool_bwd_b(dq, dproj, tm):
    t, pw = dq.shape
    gw = pw // len(POOL_WINDOWS)
    nh = tm // POOL_HALO
    nt = t // tm

    def body(c_ref, n_ref, alias_ref, o_ref):
        i = pl.program_id(0)
        cur = c_ref[...]
        nxt = jnp.where(i < nt - 1, n_ref[...], 0.0)
        sums = _window_sums(jnp.concatenate([cur, nxt], axis=0), tm + POOL_HALO, True)
        for g, w in enumerate(POOL_WINDOWS):
            cols = slice(g * gw, (g + 1) * gw)
            o_ref[:, cols] = (sums[g][:tm, cols] - cur[:, cols] * _pool_counts(i, tm, w)).astype(BF16)

    return pl.pallas_call(
        body, name="pool_bwd_b", grid=(nt,),
        in_specs=[pl.BlockSpec((tm, pw), lambda i: (i, 0)),
                  pl.BlockSpec((POOL_HALO, pw), lambda i: (jnp.minimum((i + 1) * nh, t // POOL_HALO - 1), 0)),
                  pl.BlockSpec(memory_space=pl.ANY)],
        out_specs=pl.BlockSpec((tm, pw), lambda i: (i, 0)),
        out_shape=SDS(dproj.shape, dproj.dtype),
        input_output_aliases={2: 0},
        compiler_params=_params(("arbitrary",)),
    )(dq, dq, dproj)


ELEMENTWISE_LANE_CHUNK = 256


def _lane_chunks(width):
    return [slice(c, c + ELEMENTWISE_LANE_CHUNK) for c in range(0, width, ELEMENTWISE_LANE_CHUNK)]


def _conv_pre(xc_ref, xp_ref, w_ref, b_ref, i, cols):
    cur = xc_ref[:, cols]
    prev = jnp.where(i > 0, xp_ref[:, cols], 0.0)
    ext = jnp.concatenate([prev, cur], axis=0)
    taps = [pltpu.roll(ext, CONV_WIDTH - 1 - k, 0)[CONV_HALO:] for k in range(CONV_WIDTH - 1)] + [cur]
    pre = b_ref[:, cols]
    for k in range(CONV_WIDTH):
        pre = pre + w_ref[k:k + 1, cols] * taps[k]
    return pre, taps


def conv_fwd(proj, conv_w, conv_b, col_block, tm):
    t = proj.shape[0]
    cd = conv_b.shape[1]
    nh = tm // CONV_HALO

    def body(xc_ref, xp_ref, w_ref, b_ref, o_ref):
        i = pl.program_id(0)
        for cols in _lane_chunks(cd):
            pre, _ = _conv_pre(xc_ref, xp_ref, w_ref, b_ref, i, cols)
            o_ref[:, cols] = _silu(pre)

    return pl.pallas_call(
        body, name="conv_fwd", grid=(t // tm,),
        in_specs=[pl.BlockSpec((tm, cd), lambda i: (i, col_block)),
                  pl.BlockSpec((CONV_HALO, cd), lambda i: (jnp.maximum(i * nh - 1, 0), col_block)),
                  pl.BlockSpec((CONV_WIDTH, cd), lambda i: (0, 0)),
                  pl.BlockSpec((1, cd), lambda i: (0, 0))],
        out_specs=pl.BlockSpec((tm, cd), lambda i: (i, 0)),
        out_shape=SDS((t, cd), F32),
        compiler_params=_params(("arbitrary",)),
    )(proj, proj, conv_w, conv_b)


def conv_bwd_a(dxs, db, dc, proj, conv_w, conv_b, col_block, tm, comm=None):
    t = proj.shape[0]
    cd = conv_b.shape[1]
    sw = dxs.shape[1]
    gn = db.shape[1]
    nh = tm // CONV_HALO

    def body(dx_ref, db_ref, dc_ref, xc_ref, xp_ref, w_ref, b_ref, dp_ref, dw_ref, dbias_ref):
        i = pl.program_id(0)

        @pl.when(i == 0)
        def _():
            dw_ref[...] = jnp.zeros_like(dw_ref)
            dbias_ref[...] = jnp.zeros_like(dbias_ref)

        for cols in _lane_chunks(cd):
            pre, taps = _conv_pre(xc_ref, xp_ref, w_ref, b_ref, i, cols)
            if cols.start < sw:
                dact = dx_ref[:, cols]
            elif cols.start < sw + gn:
                dact = db_ref[:, cols.start - sw:cols.stop - sw]
            else:
                dact = dc_ref[:, cols.start - sw - gn:cols.stop - sw - gn]
            dpre = dact * _dsilu(pre)
            dp_ref[:, cols] = dpre
            dbias_ref[:, cols] += jnp.sum(dpre, axis=0, keepdims=True)
            for k in range(CONV_WIDTH):
                dw_ref[k:k + 1, cols] += jnp.sum(dpre * taps[k], axis=0, keepdims=True)

    return _call(
        body, (dxs, db, dc, proj, proj, conv_w, conv_b), name="conv_bwd_a", grid=(t // tm,),
        in_specs=[pl.BlockSpec((tm, sw), lambda i: (i, 0)), pl.BlockSpec((tm, gn), lambda i: (i, 0)),
                  pl.BlockSpec((tm, gn), lambda i: (i, 0)),
                  pl.BlockSpec((tm, cd), lambda i: (i, col_block)),
                  pl.BlockSpec((CONV_HALO, cd), lambda i: (jnp.maximum(i * nh - 1, 0), col_block)),
                  pl.BlockSpec((CONV_WIDTH, cd), lambda i: (0, 0)),
                  pl.BlockSpec((1, cd), lambda i: (0, 0))],
        out_specs=[pl.BlockSpec((tm, cd), lambda i: (i, 0)),
                   pl.BlockSpec((CONV_WIDTH, cd), lambda i: (0, 0)),
                   pl.BlockSpec((1, cd), lambda i: (0, 0))],
        out_shape=[SDS((t, cd), F32), SDS((CONV_WIDTH, cd), F32), SDS((1, cd), F32)],
        sem=("arbitrary",), comm=comm)


def conv_bwd_b(dpre, conv_w, dproj, col_block, tm):
    t, cd = dpre.shape
    nh = tm // CONV_HALO
    nt = t // tm

    def body(c_ref, n_ref, w_ref, alias_ref, o_ref):
        i = pl.program_id(0)
        n = tm + CONV_HALO
        for cols in _lane_chunks(cd):
            cur = c_ref[:, cols]
            nxt = jnp.where(i < nt - 1, n_ref[:, cols], 0.0)
            ext = jnp.concatenate([cur, nxt], axis=0)
            acc = w_ref[CONV_WIDTH - 1:CONV_WIDTH, cols] * cur
            for k in range(CONV_WIDTH - 1):
                acc = acc + w_ref[k:k + 1, cols] * pltpu.roll(ext, n - (CONV_WIDTH - 1 - k), 0)[:tm]
            o_ref[:, cols] = acc.astype(BF16)

    return pl.pallas_call(
        body, name="conv_bwd_b", grid=(nt,),
        in_specs=[pl.BlockSpec((tm, cd), lambda i: (i, 0)),
                  pl.BlockSpec((CONV_HALO, cd), lambda i: (jnp.minimum((i + 1) * nh, t // CONV_HALO - 1), 0)),
                  pl.BlockSpec((CONV_WIDTH, cd), lambda i: (0, 0)),
                  pl.BlockSpec(memory_space=pl.ANY)],
        out_specs=pl.BlockSpec((tm, cd), lambda i: (i, col_block)),
        out_shape=SDS(dproj.shape, dproj.dtype),
        input_output_aliases={3: 0},
        compiler_params=_params(("arbitrary",)),
    )(dpre, dpre, conv_w, dproj)


def _softplus(v):
    return jnp.maximum(v, 0.0) + jnp.log(1.0 + jnp.exp(-jnp.abs(v)))


def _ssd_chunk_terms(dtr_ref, bias_ref, a_ref, n_heads):
    q = SSD_CHUNK
    lane = lax.broadcasted_iota(jnp.int32, (1, LANES), 1)
    pre = dtr_ref[...] + bias_ref[...]
    dt = jnp.where(lane < n_heads, _softplus(pre), 0.0)
    a = jnp.where(lane < n_heads, -jnp.exp(a_ref[...]), 0.0)
    row = lax.broadcasted_iota(jnp.int32, (q, q), 0)
    col = lax.broadcasted_iota(jnp.int32, (q, q), 1)
    causal = row >= col
    acs = jnp.dot(causal.astype(F32), dt * a, precision=HIGHEST, preferred_element_type=F32)
    last = acs[q - 1:q, :]
    return dict(pre=pre, dt=dt, a=a, acs=acs, acs_t=acs.T, eacs=jnp.exp(acs), dstate=jnp.exp(last - acs),
                cdec=jnp.exp(last), causal=causal, diag=row == col, lane=lane)


_TERM_FIELDS = ("pre", "dt", "acs", "acs_t", "eacs", "dstate", "cdec")


def _prefetched_terms(step, dtr_ref, dtn_ref, bias_ref, a_ref, n_heads, terms_ref):
    q = SSD_CHUNK

    def store(slot, tm_):
        for f, name in enumerate(_TERM_FIELDS):
            terms_ref[slot, f] = jnp.broadcast_to(tm_[name], (q, LANES))

    @pl.when(step == 0)
    def _():
        store(0, _ssd_chunk_terms(dtr_ref, bias_ref, a_ref, n_heads))

    slot = lax.rem(step, 2)
    nxt = _ssd_chunk_terms(dtn_ref, bias_ref, a_ref, n_heads)
    tm_ = dict(nxt, **{name: terms_ref[slot, f] for f, name in enumerate(_TERM_FIELDS)})
    tm_["cdec"] = tm_["cdec"][0:1]
    return tm_, lambda: store(1 - slot, nxt)


def _pair_cols(lo, v, h):
    return jnp.where(lo, v[:, h:h + 1], v[:, h + 1:h + 2])


def _pair_decay(tm_, cb, h):
    l0 = jnp.exp(jnp.where(tm_["causal"], tm_["acs"][:, h:h + 1] - tm_["acs_t"][h:h + 1, :], -jnp.inf))
    l1 = jnp.exp(jnp.where(tm_["causal"], tm_["acs"][:, h + 1:h + 2] - tm_["acs_t"][h + 1:h + 2, :], -jnp.inf))
    return l0, l1, jnp.concatenate([cb * l0, cb * l1], axis=1)


def _pair_decay_t(tm_, cbt, h):
    upper = jnp.logical_not(tm_["causal"]) | tm_["diag"]
    t0 = jnp.exp(jnp.where(upper, tm_["acs_t"][h:h + 1, :] - tm_["acs"][:, h:h + 1], -jnp.inf))
    t1 = jnp.exp(jnp.where(upper, tm_["acs_t"][h + 1:h + 2, :] - tm_["acs"][:, h + 1:h + 2], -jnp.inf))
    return jnp.concatenate([cbt * t0, cbt * t1], axis=0).astype(BF16)


def _block_diag(lo, xdt):
    return jnp.concatenate([jnp.where(lo, xdt, 0.0), jnp.where(lo, 0.0, xdt)], axis=0).astype(BF16)


def ssd_fwd(xbc, dt_raw, dt_bias, a_log, n_heads, comm=None):
    t = xbc.shape[0]
    q = SSD_CHUNK
    gn = SSD_GROUPS * SSD_STATE
    sw = n_heads * SSD_HEAD_DIM
    n_pairs = n_heads // 2
    pairs_per_group = n_pairs // SSD_GROUPS
    nc = t // q
    bblk = sw // gn

    def body(xs_ref, b_ref, c_ref, dtr_ref, dtn_ref, bias_ref, a_ref, y_ref, sin_ref, state, terms_ref):
        @pl.when(pl.program_id(0) == 0)
        def _():
            state[...] = jnp.zeros_like(state)

        tm_, keep_next = _prefetched_terms(pl.program_id(0), dtr_ref, dtn_ref, bias_ref, a_ref, n_heads, terms_ref)
        lo = tm_["lane"] < SSD_HEAD_DIM
        for g in range(SSD_GROUPS):
            gcols = slice(g * SSD_STATE, (g + 1) * SSD_STATE)
            bg = b_ref[:, gcols].astype(BF16)
            bg_t = b_ref[:, gcols].T.astype(BF16)
            cg = c_ref[:, gcols].astype(BF16)
            cb = lax.dot_general(cg, bg, NT, preferred_element_type=F32)
            for j in range(pairs_per_group):
                p = g * pairs_per_group + j
                h = 2 * p
                pcols = slice(p * LANES, (p + 1) * LANES)
                _, _, mcat = _pair_decay(tm_, cb, h)
                xdt = xs_ref[:, pcols] * _pair_cols(lo, tm_["dt"], h)
                ydiag = jnp.dot(mcat.astype(BF16), _block_diag(lo, xdt), preferred_element_type=F32)
                st = state[p]
                sin_ref[0, p] = st
                yoff = jnp.dot(cg, st.astype(BF16), preferred_element_type=F32) * _pair_cols(lo, tm_["eacs"], h)
                y_ref[:, pcols] = ydiag + yoff
                xw = (xdt * _pair_cols(lo, tm_["dstate"], h)).astype(BF16)
                state[p] = st * _pair_cols(lo, tm_["cdec"], h) + jnp.dot(bg_t, xw, preferred_element_type=F32)
        keep_next()

    vec = pl.BlockSpec((1, LANES), lambda c: (0, 0))
    return _call(
        body, (xbc, xbc, xbc, dt_raw, dt_raw, dt_bias, a_log), name="ssd_fwd", grid=(nc,),
        in_specs=[pl.BlockSpec((q, sw), lambda c: (c, 0)),
                  pl.BlockSpec((q, gn), lambda c: (c, bblk)),
                  pl.BlockSpec((q, gn), lambda c: (c, bblk + 1)),
                  pl.BlockSpec((q, LANES), lambda c: (c, 0)),
                  pl.BlockSpec((q, LANES), lambda c: (jnp.minimum(c + 1, nc - 1), 0)), vec, vec],
        out_specs=[pl.BlockSpec((q, sw), lambda c: (c, 0)),
                   pl.BlockSpec((1, n_pairs, SSD_STATE, LANES), lambda c: (c, 0, 0, 0))],
        out_shape=[SDS((t, sw), F32), SDS((nc, n_pairs, SSD_STATE, LANES), F32)],
        scratch_shapes=[pltpu.VMEM((n_pairs, SSD_STATE, LANES), F32), pltpu.VMEM((2, len(_TERM_FIELDS), q, LANES), F32)],
        sem=("arbitrary",), comm=comm)


def ssd_bwd(dy, xbc, dt_raw, dt_bias, a_log, d_full, s_in, n_heads, comm=None):
    t = xbc.shape[0]
    q = SSD_CHUNK
    gn = SSD_GROUPS * SSD_STATE
    sw = n_heads * SSD_HEAD_DIM
    n_pairs = n_heads // 2
    pairs_per_group = n_pairs // SSD_GROUPS
    nc = t // q
    bblk = sw // gn

    def body(dy_ref, xs_ref, b_ref, c_ref, dtr_ref, dtn_ref, bias_ref, a_ref, dsk_ref, sin_ref,
             dxs_ref, db_ref, dc_ref, ddtr_ref, dbias_ref, dalog_ref,
             dstate, tbuf, xbuf, rbuf, acc_a, acc_b, sel_ref, terms_ref):
        i = pl.program_id(0)

        @pl.when(i == 0)
        def _():
            dstate[...] = jnp.zeros_like(dstate)
            rbuf[...] = jnp.zeros_like(rbuf)
            acc_a[...] = jnp.zeros_like(acc_a)
            acc_b[...] = jnp.zeros_like(acc_b)
            sel_ref[...] = _head_selector(sw, SSD_HEAD_DIM)

        tm_, keep_next = _prefetched_terms(i, dtr_ref, dtn_ref, bias_ref, a_ref, n_heads, terms_ref)
        lane = tm_["lane"]
        lo = lane < SSD_HEAD_DIM
        head_row = lax.broadcasted_iota(jnp.int32, (LANES, 1), 0)
        rows = jnp.zeros((q, LANES), F32)
        cols_t = jnp.zeros((LANES, q), F32)
        for g in range(SSD_GROUPS):
            gcols = slice(g * SSD_STATE, (g + 1) * SSD_STATE)
            bg = b_ref[:, gcols].astype(BF16)
            cg = c_ref[:, gcols].astype(BF16)
            cg_t = c_ref[:, gcols].T.astype(BF16)
            cb = lax.dot_general(cg, bg, NT, preferred_element_type=F32)
            cbt = lax.dot_general(bg, cg, NT, preferred_element_type=F32)
            dcb = jnp.zeros((q, q), F32)
            db_acc = jnp.zeros((q, SSD_STATE), F32)
            dc_acc = jnp.zeros((q, SSD_STATE), F32)
            for j in range(pairs_per_group):
                p = g * pairs_per_group + j
                h = 2 * p
                pcols = slice(p * LANES, (p + 1) * LANES)
                l0, l1, mcat = _pair_decay(tm_, cb, h)
                xp = xs_ref[:, pcols]
                dtp = _pair_cols(lo, tm_["dt"], h)
                xdt = xp * dtp
                xbd = _block_diag(lo, xdt)
                dyp = dy_ref[:, pcols]
                dyb = dyp.astype(BF16)
                dsb = _pair_cols(lo, tm_["dstate"], h)
                cdr = _pair_cols(lo, tm_["cdec"], h)
                eb = _pair_cols(lo, tm_["eacs"], h)
                st = sin_ref[0, p]
                stb = st.astype(BF16)
                dst = dstate[p]
                dstb = dst.astype(BF16)
                dye = (dyp * eb).astype(BF16)
                both = jnp.dot(_pair_decay_t(tm_, cbt, h), dyb, preferred_element_type=F32)
                dx_state = jnp.dot(bg, dstb, preferred_element_type=F32) * dsb
                dxdt = jnp.where(lo, both[:q], both[q:]) + dx_state
                dmcat = lax.dot_general(dyb, xbd, NT, preferred_element_type=F32)
                dcb = dcb + dmcat[:, :q] * l0 + dmcat[:, q:] * l1
                dseg = dmcat * mcat
                csum = jnp.sum(dseg, axis=0, keepdims=True)
                rows = (rows + jnp.where(lane == h, jnp.sum(dseg[:, :q], axis=1, keepdims=True), 0.0)
                        + jnp.where(lane == h + 1, jnp.sum(dseg[:, q:], axis=1, keepdims=True), 0.0))
                cols_t = (cols_t + jnp.where(head_row == h, csum[:, :q], 0.0)
                          + jnp.where(head_row == h + 1, csum[:, q:], 0.0))
                dc_acc = dc_acc + lax.dot_general(dye, stb, NT, preferred_element_type=F32)
                db_acc = db_acc + lax.dot_general((xdt * dsb).astype(BF16), dstb, NT, preferred_element_type=F32)
                yoff = jnp.dot(cg, stb, preferred_element_type=F32) * eb
                tbuf[:, pcols] = dyp * yoff - xdt * dx_state
                xbuf[:, pcols] = dxdt * xp
                rbuf[0:1, pcols] = (jnp.sum(xdt * dx_state, axis=0, keepdims=True)
                                    + cdr * jnp.sum(dst * st, axis=0, keepdims=True))
                dxs_ref[:, pcols] = dxdt * dtp + dyp * dsk_ref[:, pcols]
                dstate[p] = dst * cdr + jnp.dot(cg_t, dye, preferred_element_type=F32)
            dcbb = dcb.astype(BF16)
            dc_ref[:, gcols] = dc_acc + jnp.dot(dcbb, bg, preferred_element_type=F32)
            db_ref[:, gcols] = db_acc + lax.dot_general(dcbb, cg, TN, preferred_element_type=F32)

        sel = sel_ref[...]
        dacs = rows - cols_t.T + _split_dot(tbuf[...], sel)
        carry = _split_dot(rbuf[...], sel)[0:1]
        anti = jnp.logical_not(tm_["causal"]) | tm_["diag"]
        da = jnp.dot(anti.astype(F32), dacs, precision=HIGHEST, preferred_element_type=F32) + carry
        ddt = da * tm_["a"] + _split_dot(xbuf[...], sel)
        ddtr = jnp.where(tm_["lane"] < n_heads, ddt * jax.nn.sigmoid(tm_["pre"]), 0.0)
        ddtr_ref[...] = ddtr.astype(BF16)
        acc_b[...] += jnp.sum(ddtr, axis=0, keepdims=True)
        acc_a[...] += jnp.sum(da * tm_["dt"], axis=0, keepdims=True)
        keep_next()

        @pl.when(i == nc - 1)
        def _():
            dbias_ref[...] = acc_b[...]
            dalog_ref[...] = acc_a[...] * tm_["a"]

    vec = pl.BlockSpec((1, LANES), lambda i: (0, 0))
    wide = pl.BlockSpec((q, sw), lambda i: (nc - 1 - i, 0))
    return _call(
        body, (dy, xbc, xbc, xbc, dt_raw, dt_raw, dt_bias, a_log, d_full, s_in), name="ssd_bwd", grid=(nc,),
        in_specs=[wide, wide,
                  pl.BlockSpec((q, gn), lambda i: (nc - 1 - i, bblk)),
                  pl.BlockSpec((q, gn), lambda i: (nc - 1 - i, bblk + 1)),
                  pl.BlockSpec((q, LANES), lambda i: (nc - 1 - i, 0)),
                  pl.BlockSpec((q, LANES), lambda i: (jnp.maximum(nc - 2 - i, 0), 0)), vec, vec,
                  pl.BlockSpec((1, sw), lambda i: (0, 0)),
                  pl.BlockSpec((1, n_pairs, SSD_STATE, LANES), lambda i: (nc - 1 - i, 0, 0, 0))],
        out_specs=[wide, pl.BlockSpec((q, gn), lambda i: (nc - 1 - i, 0)), pl.BlockSpec((q, gn), lambda i: (nc - 1 - i, 0)),
                   pl.BlockSpec((q, LANES), lambda i: (nc - 1 - i, 0)), vec, vec],
        out_shape=[SDS((t, sw), F32), SDS((t, gn), F32), SDS((t, gn), F32), SDS((t, LANES), BF16),
                   SDS((1, LANES), F32), SDS((1, LANES), F32)],
        scratch_shapes=[pltpu.VMEM((n_pairs, SSD_STATE, LANES), F32), pltpu.VMEM((q, sw), F32), pltpu.VMEM((q, sw), F32),
                        pltpu.VMEM((8, sw), F32), pltpu.VMEM((1, LANES), F32), pltpu.VMEM((1, LANES), F32),
                        pltpu.VMEM((sw, LANES), BF16), pltpu.VMEM((2, len(_TERM_FIELDS), q, LANES), F32)],
        sem=("arbitrary",), comm=comm)


def _gated(y_ref, xs_ref, z_ref, dsk_ref):
    y1 = y_ref[...] + dsk_ref[...] * xs_ref[...]
    return y1, y1 * _silu(z_ref[...])


def gate_norm_fwd(y, xbc, proj, d_full, norm_w, mixed, z_block, tm):
    t, sw = y.shape
    gw = sw // SSD_GROUPS

    def body(y_ref, xs_ref, z_ref, dsk_ref, nw_ref, alias_ref, o_ref):
        _, y2 = _gated(y_ref, xs_ref, z_ref, dsk_ref)
        for g in range(SSD_GROUPS):
            cols = slice(g * gw, (g + 1) * gw)
            blk = y2[:, cols]
            r = lax.rsqrt(jnp.mean(blk * blk, axis=-1, keepdims=True) + NORM_EPS)
            o_ref[:, cols] = (blk * r * nw_ref[:, cols]).astype(BF16)

    row = pl.BlockSpec((tm, sw), lambda i: (i, 0))
    vec = pl.BlockSpec((1, sw), lambda i: (0, 0))
    return pl.pallas_call(
        body, name="gate_norm_fwd", grid=(t // tm,),
        in_specs=[row, row, pl.BlockSpec((tm, sw), lambda i: (i, z_block)), vec, vec, pl.BlockSpec(memory_space=pl.ANY)],
        out_specs=pl.BlockSpec((tm, sw), lambda i: (i, 1)),
        out_shape=SDS(mixed.shape, mixed.dtype),
        input_output_aliases={5: 0},
        compiler_params=_params(("arbitrary",)),
    )(y, xbc, proj, d_full, norm_w, mixed)


def gate_norm_bwd(dmixed, y, xbc, proj, d_full, norm_w, dproj, z_block, tm):
    t, sw = y.shape
    gw = sw // SSD_GROUPS
    nt = t // tm

    def body(d_ref, y_ref, xs_ref, z_ref, dsk_ref, nw_ref, alias_ref, dy_ref, dz_ref, dnw_ref, dd_ref, acc_d):
        i = pl.program_id(0)

        @pl.when(i == 0)
        def _():
            dnw_ref[...] = jnp.zeros_like(dnw_ref)
            acc_d[...] = jnp.zeros_like(acc_d)

        y1, y2 = _gated(y_ref, xs_ref, z_ref, dsk_ref)
        d3 = d_ref[...]
        parts = []
        for g in range(SSD_GROUPS):
            cols = slice(g * gw, (g + 1) * gw)
            blk = y2[:, cols]
            r = lax.rsqrt(jnp.mean(blk * blk, axis=-1, keepdims=True) + NORM_EPS)
            n = blk * r
            dg = d3[:, cols]
            dnw_ref[:, cols] += jnp.sum(dg * n, axis=0, keepdims=True)
            parts.append(_norm_bwd(dg * nw_ref[:, cols], n, r))
        dy2 = jnp.concatenate(parts, axis=1)
        zv = z_ref[...]
        dz_ref[...] = (dy2 * y1 * _dsilu(zv)).astype(BF16)
        dy1 = dy2 * _silu(zv)
        dy_ref[...] = dy1
        acc_d[0:1, :] += jnp.sum(dy1 * xs_ref[...], axis=0, keepdims=True)

        @pl.when(i == nt - 1)
        def _():
            dd_ref[...] = _split_dot(acc_d[...], _head_selector(sw, SSD_HEAD_DIM))[0:1]

    row = pl.BlockSpec((tm, sw), lambda i: (i, 0))
    vec = pl.BlockSpec((1, sw), lambda i: (0, 0))
    return pl.pallas_call(
        body, name="gate_norm_bwd", grid=(nt,),
        in_specs=[pl.BlockSpec((tm, sw), lambda i: (i, 1)), row, row, pl.BlockSpec((tm, sw), lambda i: (i, z_block)),
                  vec, vec, pl.BlockSpec(memory_space=pl.ANY)],
        out_specs=[row, pl.BlockSpec((tm, sw), lambda i: (i, z_block)), vec, pl.BlockSpec((1, LANES), lambda i: (0, 0))],
        out_shape=[SDS((t, sw), F32), SDS(dproj.shape, dproj.dtype), SDS((1, sw), F32), SDS((1, LANES), F32)],
        scratch_shapes=[pltpu.VMEM((8, sw), F32)],
        input_output_aliases={6: 1},
        compiler_params=_params(("arbitrary",)),
    )(dmixed, y, xbc, proj, d_full, norm_w, dproj)


GATE_BLOCK = 1
Z_BLOCK = 2
CONV_BLOCK = 2


def _tiles(t):
    mm = dict(in_proj=(min(1024, t), 1024), dt_proj=(min(512, t), LANES), out_proj=(min(512, t), 1024),
              d_mixed=(min(512, t), 2048), dh=(min(512, t), 3072), dw_out=(512, 1024), dw_main=(1024, 1024),
              dw_dt=(512, LANES))
    return min(256, t), mm


def _place():
    x, y, c = lax.axis_index("x"), lax.axis_index("y"), lax.axis_index("c")
    return x, y, c, [(1 - x, y), (x, 1 - y), (1 - x, 1 - y)]


def gather_spread(shards, layer, rows=None, carry=None):
    n = len(shards)

    def make(ins, outs, ss, rs):
        x, y, c, chips = _place()
        mine = 4 * x + 2 * y + c
        peers = [(x, y, 1 - c)] + [(px, py, c) for px, py in chips]
        sends, locals_, arrivals = [], [], []
        for a in range(n):
            def place(ref, idx):
                return ref.at[idx] if rows is None else ref.at[idx, pl.ds(rows[0], rows[1])]

            src = place(ins[a], layer)
            locals_.append(pltpu.make_async_copy(src, place(outs[a], mine), ss.at[5 * a + 4]))
            for j, (px, py, pc) in enumerate(peers):
                sends.append(_remote(src, place(outs[a], mine), ss, rs, 5 * a + j, (px, py, pc)))
                arrivals.append(_remote(src, place(outs[a], 4 * px + 2 * py + pc), ss, rs, 5 * a + j, (px, py, pc)))
        return sends, locals_, arrivals

    return Comm(list(shards) + list(carry or []), [SDS((N_DEV,) + s.shape[1:], s.dtype) for s in shards],
                {n + a: a for a in range(n)} if carry else {}, 5 * n, make)


def gather_pass_on(gathered):
    def make(ins, outs, ss, rs):
        x, y, c, chips = _place()
        sends, arrivals = [], []
        for a in range(len(outs)):
            for j, (px, py) in enumerate(chips):
                blk, other = 4 * px + 2 * py + c, 4 * px + 2 * py + (1 - c)
                sends.append(_remote(outs[a].at[blk], outs[a].at[blk], ss, rs, 3 * a + j, (x, y, 1 - c)))
                arrivals.append(_remote(outs[a].at[other], outs[a].at[other], ss, rs, 3 * a + j, (x, y, 1 - c)))
        return sends, [], arrivals

    return Comm(gathered, [SDS(g.shape, g.dtype) for g in gathered], {a: a for a in range(len(gathered))},
                3 * len(gathered), make)


def sibling_swap(sends_):
    def make(ins, outs, ss, rs):
        x, y, c, _ = _place()
        cps = [_remote(ins[a], outs[a], ss, rs, a, (x, y, 1 - c)) for a in range(len(ins))]
        return cps, [], cps

    return Comm(sends_, [SDS(s.shape, s.dtype) for s in sends_], {}, len(sends_), make)


def chips_scatter(slabs):
    def make(ins, outs, ss, rs):
        x, y, c, chips = _place()
        mychip = 2 * x + y
        sends, arrivals = [], []
        for a in range(len(ins)):
            for j, (px, py) in enumerate(chips):
                to_there = lax.rem(2 * px + py - mychip + 4, 4) - 1
                from_here = lax.rem(mychip - 2 * px - py + 4, 4) - 1
                sends.append(_remote(ins[a].at[to_there], outs[a].at[from_here], ss, rs, 3 * a + j, (px, py, c)))
                arrivals.append(_remote(ins[a].at[to_there], outs[a].at[to_there], ss, rs, 3 * a + j, (px, py, c)))
        return sends, [], arrivals

    return Comm(slabs, [SDS(s.shape, s.dtype) for s in slabs], {}, 3 * len(slabs), make)


def comm_only(comm, name):
    def body():
        pass

    return _call(body, (), name=name, grid=(), in_specs=[], out_specs=[], out_shape=[], comm=comm)[1]


def layer_fwd(x, p, nxt=None, late=None):
    t = x.shape[0]
    tm, mm = _tiles(t)
    n_heads = p["d_full"].shape[1] // SSD_HEAD_DIM
    rows_a = nxt[0][0].shape[1] if nxt else 0
    half = rows_a * 5 // 8
    h, r_pre = rms_fwd(x, p["pre_w"], tm)
    proj, got = mm_nn(h, p["w_main"], F32, *mm["in_proj"], "in_proj", merge_comms([
        gather_spread(late[0], late[1]) if late else None,
        gather_spread(nxt[0], nxt[2], rows=(0, half)) if nxt else None]))
    n_late = len(late[0]) if late else 0
    dt_raw, got_late = mm_nn(h, p["w_dt"], F32, *mm["dt_proj"], "dt_proj", gather_pass_on(got[:n_late]) if late else None)
    if late:
        p = dict(p, **late[2](got_late))
    mixed = pool_fwd(proj, p["mixw"], p["pscale"], tm)
    xbc = conv_fwd(proj, p["conv_w"], p["conv_b"], CONV_BLOCK, tm)
    (y, s_in), got_b = ssd_fwd(xbc, dt_raw, p["dt_bias"], p["a_log"], n_heads, gather_spread(nxt[1], nxt[2]) if nxt else None)
    mixed = gate_norm_fwd(y, xbc, proj, p["d_full"], p["norm_w"], mixed, Z_BLOCK, tm)
    out, got_a = mm_nn(mixed, p["w_out"], F32, *mm["out_proj"], "out_proj",
                       gather_spread(nxt[0], nxt[2], rows=(half, rows_a - half), carry=got[n_late:]) if nxt else None)
    (x_next, r_post), gathered = post_fwd(out, x, p["post_w"], tm, gather_pass_on(got_a + got_b) if nxt else None)
    return x_next, dict(x=x, h=h, r_pre=r_pre, proj=proj, dt_raw=dt_raw, xbc=xbc, y=y, s_in=s_in, mixed=mixed,
                        out=out, r_post=r_post), gathered, p


def _pair_sums(own, got):
    return [pair_sum(o, r, min(256, o.shape[1]), "pair_sum") for o, r in zip(own, got)]


def layer_bwd(g, s, p, split_in, split_rest, pending=None, last=False):
    t = g.shape[0]
    tm, mm = _tiles(t)
    d = g.shape[1]
    n_heads = p["d_full"].shape[1] // SSD_HEAD_DIM
    d_out, d_post = post_bwd(g, s["out"], s["r_post"], p["post_w"], tm)
    dmixed, got_sib = mm_nt(d_out, p["w_out"], F32, *mm["d_mixed"], d, "d_mixed", sibling_swap(pending[1]) if pending else None)
    chip_sums = _pair_sums(pending[0], got_sib) if pending else []
    dw_out, _ = mm_tn(s["mixed"], d_out, *mm["dw_out"], "dw_out")
    dproj, dq, d_pscale, d_mixw = pool_bwd_a(dmixed, s["proj"], p["mixw"], p["pscale"], tm)
    dproj = pool_bwd_b(dq, dproj, tm)
    own_rest, send_rest = split_rest(dw_out, d_mixw)
    dy, dproj, d_norm, d_dskip = gate_norm_bwd(dmixed, s["y"], s["xbc"], s["proj"], p["d_full"], p["norm_w"], dproj,
                                               Z_BLOCK, tm)
    (dxs, db, dc, ddtr, d_dtb, d_alog), got = ssd_bwd(
        dy, s["xbc"], s["dt_raw"], p["dt_bias"], p["a_log"], p["d_full"], s["s_in"], n_heads,
        merge_comms([chips_scatter(chip_sums[:1]) if pending else None, sibling_swap(send_rest) if last else None]))
    got_first, my_sib_rest = (got[:1], got[1:]) if pending else ([], got)
    rest_comm = chips_scatter(chip_sums[1:]) if pending else None
    (dpre, d_convw, d_convb), got_rest = conv_bwd_a(dxs, db, dc, s["proj"], p["conv_w"], p["conv_b"], CONV_BLOCK, tm,
                                                    rest_comm if last else None)
    dproj = conv_bwd_b(dpre, p["conv_w"], dproj, CONV_BLOCK, tm)
    dw_main, got = mm_tn(s["h"], dproj, *mm["dw_main"], "dw_main",
                         chips_scatter(_pair_sums(own_rest, my_sib_rest)) if last else rest_comm)
    got_rest, my_chips_rest = (got_rest, got) if last else (got, [])
    dw_dt, _ = mm_tn(s["h"], ddtr, *mm["dw_dt"], "dw_dt")
    own_in, send_in = split_in(dw_main, dw_dt)
    my_sib_in = comm_only(sibling_swap(send_in), "grads_to_sibling") if last else []
    dh, my_chips_in = mm_nt(dproj, p["w_main"], F32, mm["dh"][0], d, mm["dh"][1], "dh_main",
                            chips_scatter(_pair_sums(own_in, my_sib_in)) if last else None, extra=(ddtr, p["w_dt"]))
    gx, d_pre = rms_bwd(dh, s["x"], s["r_pre"], p["pre_w"], g, tm)
    small = dict(pre_w=d_pre, pscale=d_pscale, conv_w=d_convw, conv_b=d_convb, dt_bias=d_dtb, a_log=d_alog,
                 d_skip=d_dskip, norm_w=d_norm, post_w=d_post)
    done = [(got_sib, got_first + got_rest)] if pending else [None]
    if last:
        done.append((my_sib_in + my_sib_rest, my_chips_in + my_chips_rest))
    return gx, small, (own_in + own_rest, send_in + send_rest), done


def _two_level_gather(x_refs, out_slots, send_sems, recv_sems, local_sems):
    x, y, c, chips = _place()
    me, sibling = (x, y, c), (x, y, 1 - c)
    n = len(x_refs)

    def copy(a, k, block, to, src=None):
        return pltpu.make_async_remote_copy(
            src_ref=out_slots[a](*block) if src is None else src, dst_ref=out_slots[a](*block),
            send_sem=send_sems.at[7 * a + k], recv_sem=recv_sems.at[7 * a + k], device_id=to, device_id_type=MESH)

    mine = [pltpu.make_async_copy(x_refs[a], out_slots[a](*me), local_sems.at[a]) for a in range(n)]
    for cp in mine:
        cp.start()
    first = []
    for a in range(n):
        first.append(copy(a, 0, me, sibling, src=x_refs[a]))
        first += [copy(a, 1 + j, me, (*chip, c), src=x_refs[a]) for j, chip in enumerate(chips)]
    for cp in first:
        cp.start()
    passed = []
    for j, chip in enumerate(chips):
        for a in range(n):
            copy(a, 1 + j, (*chip, c), me).wait_recv()
            fwd = copy(a, 4 + j, (*chip, c), sibling)
            fwd.start()
            passed.append(fwd)
    for a in range(n):
        copy(a, 0, sibling, me).wait_recv()
        for j, chip in enumerate(chips):
            copy(a, 4 + j, (*chip, 1 - c), me).wait_recv()
    for cp in first + passed:
        cp.wait_send()
    for cp in mine:
        cp.wait()


def all_gather_hbm(shards, name):
    n = len(shards)

    def body(*refs):
        x_refs, out_refs = refs[:n], refs[n:2 * n]
        send_sems, recv_sems, local_sems = refs[2 * n:]
        slots = [lambda px, py, pc, o=o: o.at[:, 4 * px + 2 * py + pc] for o in out_refs]
        _two_level_gather(x_refs, slots, send_sems, recv_sems, local_sems)

    hbm = pl.BlockSpec(memory_space=pl.ANY)
    return pl.pallas_call(
        body, name=name,
        out_shape=[SDS((s.shape[0], N_DEV) + s.shape[1:], s.dtype) for s in shards],
        in_specs=[hbm] * n, out_specs=[hbm] * n,
        scratch_shapes=[pltpu.SemaphoreType.DMA((7 * n,)), pltpu.SemaphoreType.DMA((7 * n,)), pltpu.SemaphoreType.DMA((n,))],
    )(*shards)


def all_gather_vmem(block, name):
    r, c_ = block.shape

    def body(x_ref, out_ref, send_sems, recv_sems, local_sems):
        _two_level_gather([x_ref], [lambda px, py, pc: out_ref.at[4 * px + 2 * py + pc]], send_sems, recv_sems, local_sems)

    return pl.pallas_call(
        body, name=name, out_shape=SDS((N_DEV, r, c_), block.dtype),
        in_specs=[pl.BlockSpec(memory_space=pltpu.VMEM)], out_specs=pl.BlockSpec(memory_space=pltpu.VMEM),
        scratch_shapes=[pltpu.SemaphoreType.DMA((7,)), pltpu.SemaphoreType.DMA((7,)), pltpu.SemaphoreType.DMA((1,))],
        compiler_params=_params(),
    )(block)


def _block_tiles(cols):
    base = [(cols * i) // LANES for i in range(N_DEV)]
    ends = [-((-cols * (i + 1)) // LANES) for i in range(N_DEV)]
    return base, ends, max(e - b for b, e in zip(base, ends))


def _my_lane_offset(cols):
    me = 4 * lax.axis_index("x") + 2 * lax.axis_index("y") + lax.axis_index("c")
    return lax.rem(cols * me, LANES)


def shift_cast(w, tr):
    nl, r, cols = w.shape
    width = _block_tiles(cols)[2] * LANES

    def body(x_ref, o_ref, pad):
        pad[:, width - LANES:] = jnp.zeros((tr, LANES), F32)
        pad[:, :cols] = x_ref[...]
        o_ref[...] = pltpu.roll(pad[...], _my_lane_offset(cols), 1).astype(BF16)

    assert width - LANES <= cols
    return pl.pallas_call(
        body, name="shift_cast", grid=(nl, r // tr),
        in_specs=[pl.BlockSpec((pl.Squeezed(), tr, cols), lambda l, i: (l, i, 0))],
        out_specs=pl.BlockSpec((pl.Squeezed(), tr, width), lambda l, i: (l, i, 0)),
        out_shape=SDS((nl, r, width), BF16), scratch_shapes=[pltpu.VMEM((tr, width), F32)],
        compiler_params=_params(("arbitrary", "arbitrary")))(w)


def assemble_w_in(blocks, cols, n_tail, tr):
    _, r, width = blocks.shape
    base, ends, _ = _block_tiles(cols)
    total = ends[-1]
    main_tiles = (N_DEV * cols - n_tail) // LANES
    assert main_tiles == total - 1 and (N_DEV * cols - n_tail) % LANES == 0

    def body(b_ref, main_ref, tail_ref):
        for tile in range(total):
            parts = [b_ref[i, :, (tile - base[i]) * LANES:(tile - base[i] + 1) * LANES]
                     for i in range(N_DEV) if base[i] <= tile < ends[i]]
            val = parts[0] if len(parts) == 1 else parts[0] + parts[1]
            if tile < main_tiles:
                main_ref[:, tile * LANES:(tile + 1) * LANES] = val
            else:
                tail_ref[...] = val

    return pl.pallas_call(
        body, name="assemble_w_in", grid=(r // tr,),
        in_specs=[pl.BlockSpec((N_DEV, tr, width), lambda i: (0, i, 0))],
        out_specs=[pl.BlockSpec((tr, main_tiles * LANES), lambda i: (i, 0)), pl.BlockSpec((tr, LANES), lambda i: (i, 0))],
        out_shape=[SDS((r, main_tiles * LANES), blocks.dtype), SDS((r, LANES), blocks.dtype)],
        compiler_params=_params(("arbitrary",)),
    )(blocks)


def grad_blocks(dw_main, dw_tail, cols, tr):
    r = dw_main.shape[0]
    base, _, tpb = _block_tiles(cols)
    width = tpb * LANES

    def body(m_ref, t_ref, own_ref, send_ref):
        cat = jnp.concatenate([m_ref[...], t_ref[...]], axis=1)
        south = lax.axis_index("c") == 0
        for k in range(N_DEV // 2):
            a = cat[:, base[2 * k] * LANES:base[2 * k] * LANES + width]
            b = cat[:, base[2 * k + 1] * LANES:base[2 * k + 1] * LANES + width]
            own_ref[k] = jnp.where(south, a, b)
            send_ref[k] = jnp.where(south, b, a).astype(BF16)

    return pl.pallas_call(
        body, name="grad_blocks", grid=(r // tr,),
        in_specs=[pl.BlockSpec((tr, dw_main.shape[1]), lambda i: (i, 0)), pl.BlockSpec((tr, LANES), lambda i: (i, 0))],
        out_specs=[pl.BlockSpec((N_DEV // 2, tr, width), lambda i: (0, i, 0))] * 2,
        out_shape=[SDS((N_DEV // 2, r, width), F32), SDS((N_DEV // 2, r, width), BF16)],
        compiler_params=_params(("arbitrary",)),
    )(dw_main, dw_tail)


def _adamw(w, g, m, v):
    m = ADAM_B1 * m + (1.0 - ADAM_B1) * g
    v = ADAM_B2 * v + (1.0 - ADAM_B2) * jnp.square(g)
    m_hat = m / (1.0 - ADAM_B1 ** ADAM_STEP)
    v_hat = v / (1.0 - ADAM_B2 ** ADAM_STEP)
    delta = -ADAM_LR * (m_hat / (jnp.sqrt(v_hat) + ADAM_EPS) + ADAM_WD * w)
    return delta, m, v


def _my_chip():
    return 2 * lax.axis_index("x") + lax.axis_index("y")


def pair_sum(own, got, tr, name):
    k, r, c_ = own.shape
    others = lax.rem(_my_chip() + 1 + jnp.arange(k - 1, dtype=jnp.int32), k)

    def body(others_ref, a_ref, b_ref, o_ref):
        o_ref[...] = (a_ref[...] + b_ref[...].astype(F32)).astype(BF16)

    src = pl.BlockSpec((pl.Squeezed(), tr, c_), lambda s, i, oth: (oth[s], i, 0))
    return pl.pallas_call(
        body, name=name, out_shape=SDS((k - 1, r, c_), BF16),
        grid_spec=pltpu.PrefetchScalarGridSpec(
            num_scalar_prefetch=1, grid=(k - 1, r // tr), in_specs=[src, src],
            out_specs=pl.BlockSpec((pl.Squeezed(), tr, c_), lambda s, i, oth: (s, i, 0))),
        compiler_params=_params(("arbitrary", "arbitrary")),
    )(others, own, got)


def reduce_adam(own, got_sibling, got_chips, w, m, v, prev, layer, tr, name, shifted=False):
    nl, r, cols = w.shape
    c_ = own.shape[-1]
    n_scratch = 1 if shifted else 0
    chip = jnp.reshape(_my_chip(), (1,)).astype(jnp.int32)

    def body(chip_ref, own_ref, sib_ref, c0_ref, c1_ref, c2_ref, w_ref, m_ref, v_ref, *rest):
        g_ref, d_ref, nm_ref, nv_ref = rest[len(rest) - n_scratch - 4:len(rest) - n_scratch]
        g = (own_ref[...] + sib_ref[...].astype(F32) + c0_ref[...].astype(F32) + c1_ref[...].astype(F32)
             + c2_ref[...].astype(F32))
        if shifted:
            rest[-1][...] = pltpu.roll(g, c_ - _my_lane_offset(cols), 1)
            g = rest[-1][:, :cols]
        delta, nm, nv = _adamw(w_ref[...], g, m_ref[...], v_ref[...])
        g_ref[...] = g
        d_ref[...] = delta
        nm_ref[...] = nm
        nv_ref[...] = nv

    mine = pl.BlockSpec((pl.Squeezed(), tr, c_), lambda i, ch: (ch[0], i, 0))
    lay = pl.BlockSpec((pl.Squeezed(), tr, cols), lambda i, ch: (layer, i, 0))
    chips = [pl.BlockSpec((pl.Squeezed(), tr, c_), lambda i, ch, s=s: (s, i, 0)) for s in range(3)]
    in_specs = [mine, mine] + chips + [lay, lay, lay]
    args = [chip, own, got_sibling, got_chips, got_chips, got_chips, w, m, v]
    aliases = {}
    if prev is not None:
        in_specs += [pl.BlockSpec(memory_space=pl.ANY)] * 4
        aliases = {len(args) + k: k for k in range(4)}
        args += list(prev)
    return pl.pallas_call(
        body, name=name, out_shape=[SDS((nl, r, cols), F32)] * 4, input_output_aliases=aliases,
        grid_spec=pltpu.PrefetchScalarGridSpec(
            num_scalar_prefetch=1, grid=(r // tr,), in_specs=in_specs, out_specs=[lay] * 4,
            scratch_shapes=[pltpu.VMEM((tr, c_), F32)] * n_scratch),
        compiler_params=_params(("arbitrary",)),
    )(*args)


def sum_devices(packs):
    n, r, c_ = packs.shape

    def body(p_ref, o_ref):
        acc = p_ref[0]
        for k in range(1, n):
            acc = acc + p_ref[k]
        o_ref[...] = acc

    return pl.pallas_call(body, name="sum_devices", out_shape=SDS((r, c_), F32), compiler_params=_params())(packs)


def adam_small(w, g, m, v):
    def body(w_ref, g_ref, m_ref, v_ref, d_ref, nm_ref, nv_ref):
        delta, nm, nv = _adamw(w_ref[...], g_ref[...], m_ref[...], v_ref[...])
        d_ref[...] = delta
        nm_ref[...] = nm
        nv_ref[...] = nv

    return pl.pallas_call(body, name="adam_small", out_shape=[SDS(w.shape, F32)] * 3, compiler_params=_params())(w, g, m, v)


SMALL = ("pre_norm_w", "pool_scale", "conv_b", "dt_bias", "a_log", "d_skip", "_pad", "ssd_norm_w", "post_norm_w", "conv_w")


def _pack(parts):
    flat = jnp.concatenate([parts[k] for k in SMALL], axis=1).reshape(-1, LANES)
    return jnp.pad(flat, ((0, (-flat.shape[0]) % 8), (0, 0)))


def _unpack(pack, sizes, nl):
    total = sum(sizes[k] for k in SMALL)
    flat = pack[: nl * total // LANES].reshape(nl, total)
    out, o = {}, 0
    for k in SMALL:
        out[k] = flat[:, o:o + sizes[k]]
        o += sizes[k]
    return out


def kernel(x, pre_norm_w, w_in, pool_mix_w, pool_scale, conv_w, conv_b, dt_bias, a_log, d_skip, ssd_norm_w, w_out, post_norm_w, loss_target, m_pre_norm_w, m_w_in, m_pool_mix_w, m_pool_scale, m_conv_w, m_conv_b, m_dt_bias, m_a_log, m_d_skip, m_ssd_norm_w, m_w_out, m_post_norm_w, v_pre_norm_w, v_w_in, v_pool_mix_w, v_pool_scale, v_conv_w, v_conv_b, v_dt_bias, v_a_log, v_d_skip, v_ssd_norm_w, v_w_out, v_post_norm_w):
    cx, cy, cc = lax.axis_index("x"), lax.axis_index("y"), lax.axis_index("c")
    me = 4 * cx + 2 * cy + cc
    mychip = 2 * cx + cy
    nl, d, cols = w_in.shape
    t = x.shape[1]
    n_heads = a_log.shape[1]
    sw = n_heads * SSD_HEAD_DIM
    pw = pool_scale.shape[1]
    cd = conv_b.shape[1]
    ng, gsh, gw = pool_mix_w.shape[1:]
    e_main = N_DEV * cols - n_heads
    assert x.shape[0] == 1 and pw == sw and cd == sw + 2 * SSD_GROUPS * SSD_STATE and e_main == 2 * pw + sw + cd
    assert 2 * pw + sw == CONV_BLOCK * cd and n_heads <= LANES and t % SSD_CHUNK == 0 and gsh * N_DEV == gw
    tm, _ = _tiles(t)

    shards_a, shards_b = [shift_cast(w_in, tm)], [w_out.astype(BF16), pool_mix_w.astype(BF16), conv_w]
    pad_h = ((0, 0), (0, LANES - n_heads))

    def params_a(l, g_in):
        w_main, w_dt = assemble_w_in(g_in, cols, n_heads, tm)
        return dict(pre_w=pre_norm_w[l:l + 1], w_main=w_main, w_dt=w_dt, pscale=pool_scale[l:l + 1], conv_b=conv_b[l:l + 1],
                    dt_bias=jnp.pad(dt_bias[l:l + 1], pad_h), a_log=jnp.pad(a_log[l:l + 1], pad_h),
                    d_full=jnp.repeat(d_skip[l:l + 1], SSD_HEAD_DIM, axis=1), norm_w=ssd_norm_w[l:l + 1],
                    post_w=post_norm_w[l:l + 1])

    def params_b(g_out, g_mix, g_conv):
        return dict(mixw=g_mix.transpose(1, 0, 2, 3).reshape(ng, gw, gw), conv_w=g_conv.transpose(1, 0, 2).reshape(CONV_WIDTH, cd),
                    w_out=g_out.reshape(N_DEV * w_out.shape[1], d))

    xs = x[0]
    saved, params = [], []
    p = params_a(0, all_gather_hbm([shards_a[0][:1]], "gather_w_in")[0][0])
    for l in range(nl):
        xs, s, gathered, p = layer_fwd(xs, p, (shards_a, shards_b, l + 1) if l + 1 < nl else None,
                                       (shards_b, 0, lambda got: params_b(*got)) if l == 0 else None)
        saved.append(s)
        params.append(p)
        if l + 1 < nl:
            p = dict(params_a(l + 1, gathered[0]), **params_b(*gathered[1:]))
    loss_part, g = loss_grad(xs, loss_target[0], tm)
    loss = lax.psum(loss_part[0, 0], ("x", "y", "c"))

    big = {"w_in": (w_in, m_w_in, v_w_in), "w_out": (w_out, m_w_out, v_w_out),
           "pool_mix_w": tuple(a.reshape(nl, ng * gsh, gw) for a in (pool_mix_w, m_pool_mix_w, v_pool_mix_w))}
    names = list(big)
    big_out = {k: None for k in big}
    small_g = [None] * nl

    def apply(layer, own, got_sib, got_chips):
        for k, o, gs_, gc in zip(names, own, got_sib, got_chips):
            wk, mk, vk = big[k]
            big_out[k] = reduce_adam(o, gs_, gc, wk, mk, vk, big_out[k], layer, min(256, wk.shape[1]), "reduce_adam_" + k,
                                     shifted=(k == "w_in"))

    def split_in(dw_main, dw_dt):
        own, send = grad_blocks(dw_main, dw_dt, cols, min(128, d))
        return [own], [send]

    def split_rest(dw_out, d_mixw):
        halves = [lambda ci: lax.dynamic_index_in_dim(dw_out.reshape(4, 2, -1, d), ci, 1, keepdims=False),
                  lambda ci: lax.dynamic_index_in_dim(
                      d_mixw.reshape(ng, 4, 2, gsh, gw), ci, 2, keepdims=False).transpose(1, 0, 2, 3).reshape(4, ng * gsh, gw)]
        return [h(cc) for h in halves], [h(1 - cc).astype(BF16) for h in halves]

    pending = None
    for l in reversed(range(nl)):
        g, gr, mine, done = layer_bwd(g, saved[l], params[l], split_in, split_rest, pending, last=(l == 0))
        if pending is not None:
            apply(l + 1, pending[0], *done[0])
        if l == 0:
            apply(0, mine[0], *done[1])
        pending = mine
        small_g[l] = dict(pre_norm_w=gr["pre_w"], pool_scale=gr["pscale"], conv_b=gr["conv_b"], dt_bias=gr["dt_bias"][:, :n_heads],
                          a_log=gr["a_log"][:, :n_heads], d_skip=gr["d_skip"][:, :n_heads], _pad=jnp.zeros((1, LANES - 3 * n_heads), F32),
                          ssd_norm_w=gr["norm_w"], post_norm_w=gr["post_w"], conv_w=gr["conv_w"].reshape(1, CONV_WIDTH * cd))

    sizes = {k: small_g[0][k].shape[1] for k in SMALL}
    gsum = sum_devices(all_gather_vmem(_pack({k: jnp.concatenate([sg[k] for sg in small_g], axis=0) for k in SMALL}),
                                       "gather_small_grads"))
    gs = _unpack(gsum, sizes, nl)
    csh = conv_w.shape[2]
    gs["conv_w"] = lax.dynamic_slice_in_dim(gs["conv_w"].reshape(nl, CONV_WIDTH, cd), me * csh, csh, axis=2).reshape(nl, -1)
    lsizes = dict(sizes, conv_w=CONV_WIDTH * csh)
    zpad = jnp.zeros((nl, sizes["_pad"]), F32)

    def local(pre, scale, cb, dtb, al, dsk, nw, post, cw):
        return _pack(dict(pre_norm_w=pre, pool_scale=scale, conv_b=cb, dt_bias=dtb, a_log=al, d_skip=dsk, _pad=zpad,
                          ssd_norm_w=nw, post_norm_w=post, conv_w=cw.reshape(nl, -1)))

    wp = local(pre_norm_w, pool_scale, conv_b, dt_bias, a_log, d_skip, ssd_norm_w, post_norm_w, conv_w)
    mp = local(m_pre_norm_w, m_pool_scale, m_conv_b, m_dt_bias, m_a_log, m_d_skip, m_ssd_norm_w, m_post_norm_w, m_conv_w)
    vp = local(v_pre_norm_w, v_pool_scale, v_conv_b, v_dt_bias, v_a_log, v_d_skip, v_ssd_norm_w, v_post_norm_w, v_conv_w)
    small_out = [gs] + [_unpack(o, lsizes, nl) for o in adam_small(wp, _pack(gs), mp, vp)]

    def leaf(kind, name):
        if name in big:
            return big_out[name][kind].reshape(big[name][0].shape if name != "pool_mix_w" else pool_mix_w.shape)
        val = small_out[kind][name]
        return val.reshape(conv_w.shape) if name == "conv_w" else val

    order = ("pre_norm_w", "w_in", "pool_mix_w", "pool_scale", "conv_w", "conv_b", "dt_bias", "a_log", "d_skip",
             "ssd_norm_w", "w_out", "post_norm_w")
    return (loss, g[None]) + tuple(leaf(kind, name) for kind in range(4) for name in order)
```

```python
import jax
import jax.numpy as jnp
from jax import lax
from jax.experimental import pallas as pl
from jax.experimental.pallas import tpu as pltpu

F32 = jnp.float32
BF16 = jnp.bfloat16
SDS = jax.ShapeDtypeStruct
MESH = pl.DeviceIdType.MESH
HIGHEST = lax.Precision.HIGHEST

NORM_EPS = 1e-6
POOL_WINDOWS = (2, 4, 8, 16)
POOL_HALO = 16
CONV_WIDTH = 4
CONV_HALO = 8
SSD_CHUNK = 128
SSD_HEAD_DIM = 64
SSD_STATE = 128
SSD_GROUPS = 4
LANES = 128
N_DEV = 8

ADAM_LR = 0.001
ADAM_B1 = 0.9
ADAM_B2 = 0.999
ADAM_EPS = 1e-08
ADAM_WD = 0.01
ADAM_STEP = 10

VMEM_LIMIT = 56 * 1024 * 1024

NT = (((1,), (1,)), ((), ()))
TN = (((0,), (0,)), ((), ()))


def _params(sem=None):
    kw = dict(vmem_limit_bytes=VMEM_LIMIT)
    if sem is not None:
        kw["dimension_semantics"] = sem
    return pltpu.CompilerParams(**kw)


def _silu(v):
    return v * jax.nn.sigmoid(v)


def _dsilu(v):
    s = jax.nn.sigmoid(v)
    return s * (1.0 + v * (1.0 - s))


def _split_dot(v, sel):
    hi = v.astype(BF16)
    lo = (v - hi.astype(F32)).astype(BF16)
    return (jnp.dot(hi, sel, preferred_element_type=F32) + jnp.dot(lo, sel, preferred_element_type=F32))


def _head_selector(width, per):
    ch = lax.broadcasted_iota(jnp.int32, (width, LANES), 0)
    hd = lax.broadcasted_iota(jnp.int32, (width, LANES), 1)
    return jnp.where((ch >= hd * per) & (ch < (hd + 1) * per), 1.0, 0.0).astype(BF16)


class Comm:
    def __init__(self, inputs, out_shapes, aliases, n_sems, make):
        self.inputs, self.out_shapes, self.aliases, self.n_sems, self.make = list(inputs), list(out_shapes), dict(aliases), n_sems, make


def _remote(src, dst, send_sems, recv_sems, k, peer):
    return pltpu.make_async_remote_copy(src_ref=src, dst_ref=dst, send_sem=send_sems.at[k], recv_sem=recv_sems.at[k],
                                        device_id=peer, device_id_type=MESH)


class _SemRange:
    def __init__(self, sems, start):
        self.sems, self.start = sems, start

    @property
    def at(self):
        return self

    def __getitem__(self, k):
        return self.sems.at[self.start + k]


def merge_comms(comms):
    comms = [c for c in comms if c is not None]
    if len(comms) <= 1:
        return comms[0] if comms else None
    aliases, i_off, o_off = {}, 0, 0
    for c in comms:
        aliases.update({i_off + k: o_off + v for k, v in c.aliases.items()})
        i_off, o_off = i_off + len(c.inputs), o_off + len(c.out_shapes)

    def make(ins, outs, ss, rs):
        sends, locals_, arrivals, i0, o0, s0 = [], [], [], 0, 0, 0
        for c in comms:
            s, l, a = c.make(ins[i0:i0 + len(c.inputs)], outs[o0:o0 + len(c.out_shapes)], _SemRange(ss, s0), _SemRange(rs, s0))
            sends, locals_, arrivals = sends + s, locals_ + l, arrivals + a
            i0, o0, s0 = i0 + len(c.inputs), o0 + len(c.out_shapes), s0 + c.n_sems
        return sends, locals_, arrivals

    return Comm(sum((c.inputs for c in comms), []), sum((c.out_shapes for c in comms), []), aliases,
                sum(c.n_sems for c in comms), make)


def _call(body, args, *, name, grid, in_specs, out_specs, out_shape, scratch_shapes=(), sem=None, comm=None):
    in_specs, out_specs, out_shape = list(in_specs), list(out_specs), list(out_shape)
    if comm is None:
        outs = pl.pallas_call(body, name=name, grid=grid, in_specs=in_specs, out_specs=out_specs, out_shape=out_shape,
                              scratch_shapes=list(scratch_shapes), compiler_params=_params(sem))(*args)
        return list(outs), []
    ni, no, nci, nco, ns = len(in_specs), len(out_specs), len(comm.inputs), len(comm.out_shapes), len(scratch_shapes)
    hbm = pl.BlockSpec(memory_space=pl.ANY)

    def hosted(*refs):
        ins, cins = refs[:ni], refs[ni:ni + nci]
        outs, couts = refs[ni + nci:ni + nci + no], refs[ni + nci + no:ni + nci + no + nco]
        scratch = refs[ni + nci + no + nco:]
        sends, locals_, arrivals = comm.make(cins, couts, scratch[ns], scratch[ns + 1])
        first = last = None if grid else True
        for axis, extent in enumerate(grid):
            pid = pl.program_id(axis)
            first = (pid == 0) if first is None else first & (pid == 0)
            last = (pid == extent - 1) if last is None else last & (pid == extent - 1)

        @pl.when(first)
        def _():
            for cp in locals_ + sends:
                cp.start()

        body(*ins, *outs, *scratch[:ns])

        @pl.when(last)
        def _():
            for cp in arrivals:
                cp.wait_recv()
            for cp in sends:
                cp.wait_send()
            for cp in locals_:
                cp.wait()

    outs = pl.pallas_call(
        hosted, name=name, grid=grid, in_specs=in_specs + [hbm] * nci, out_specs=out_specs + [hbm] * nco,
        out_shape=out_shape + comm.out_shapes,
        scratch_shapes=list(scratch_shapes) + [pltpu.SemaphoreType.DMA((comm.n_sems,)), pltpu.SemaphoreType.DMA((comm.n_sems,))],
        input_output_aliases={ni + k: no + v for k, v in comm.aliases.items()},
        compiler_params=_params(sem),
    )(*args, *comm.inputs)
    return list(outs[:no]), list(outs[no:])


def rms_fwd(x, w, tm):
    t, d = x.shape

    def body(x_ref, w_ref, h_ref, r_ref):
        xv = x_ref[...]
        r = lax.rsqrt(jnp.mean(xv * xv, axis=-1, keepdims=True) + NORM_EPS)
        h_ref[...] = (xv * r * w_ref[...]).astype(BF16)
        r_ref[...] = r

    return pl.pallas_call(
        body, name="rms_fwd", grid=(t // tm,),
        in_specs=[pl.BlockSpec((tm, d), lambda i: (i, 0)), pl.BlockSpec((1, d), lambda i: (0, 0))],
        out_specs=[pl.BlockSpec((tm, d), lambda i: (i, 0)), pl.BlockSpec((tm, 1), lambda i: (i, 0))],
        out_shape=[SDS((t, d), BF16), SDS((t, 1), F32)],
        compiler_params=_params(("arbitrary",)),
    )(x, w)


def post_fwd(out, x, w, tm, comm=None):
    t, d = x.shape

    def body(o_ref, x_ref, w_ref, y_ref, r_ref):
        ov = o_ref[...]
        r = lax.rsqrt(jnp.mean(ov * ov, axis=-1, keepdims=True) + NORM_EPS)
        y_ref[...] = x_ref[...] + ov * r * w_ref[...]
        r_ref[...] = r

    return _call(
        body, (out, x, w), name="post_fwd", grid=(t // tm,),
        in_specs=[pl.BlockSpec((tm, d), lambda i: (i, 0)), pl.BlockSpec((tm, d), lambda i: (i, 0)),
                  pl.BlockSpec((1, d), lambda i: (0, 0))],
        out_specs=[pl.BlockSpec((tm, d), lambda i: (i, 0)), pl.BlockSpec((tm, 1), lambda i: (i, 0))],
        out_shape=[SDS((t, d), F32), SDS((t, 1), F32)], sem=("arbitrary",), comm=comm)


def _norm_bwd(g_n, n, r):
    return r * (g_n - n * jnp.mean(g_n * n, axis=-1, keepdims=True))


def post_bwd(g, out, r, w, tm):
    t, d = g.shape

    def body(g_ref, o_ref, r_ref, w_ref, do_ref, dw_ref):
        i = pl.program_id(0)
        gv = g_ref[...]
        rv = r_ref[...]
        n = o_ref[...] * rv
        part = jnp.sum(gv * n, axis=0, keepdims=True)

        @pl.when(i == 0)
        def _():
            dw_ref[...] = part

        @pl.when(i > 0)
        def _():
            dw_ref[...] += part

        do_ref[...] = _norm_bwd(gv * w_ref[...], n, rv).astype(BF16)

    return pl.pallas_call(
        body, name="post_bwd", grid=(t // tm,),
        in_specs=[pl.BlockSpec((tm, d), lambda i: (i, 0)), pl.BlockSpec((tm, d), lambda i: (i, 0)),
                  pl.BlockSpec((tm, 1), lambda i: (i, 0)), pl.BlockSpec((1, d), lambda i: (0, 0))],
        out_specs=[pl.BlockSpec((tm, d), lambda i: (i, 0)), pl.BlockSpec((1, d), lambda i: (0, 0))],
        out_shape=[SDS((t, d), BF16), SDS((1, d), F32)],
        compiler_params=_params(("arbitrary",)),
    )(g, out, r, w)


def rms_bwd(dh, x, r, w, g, tm):
    t, d = x.shape

    def body(a_ref, x_ref, r_ref, w_ref, g_ref, gx_ref, dw_ref):
        i = pl.program_id(0)
        dh = a_ref[...]
        rv = r_ref[...]
        n = x_ref[...] * rv
        part = jnp.sum(dh * n, axis=0, keepdims=True)

        @pl.when(i == 0)
        def _():
            dw_ref[...] = part

        @pl.when(i > 0)
        def _():
            dw_ref[...] += part

        gx_ref[...] = g_ref[...] + _norm_bwd(dh * w_ref[...], n, rv)

    row = pl.BlockSpec((tm, d), lambda i: (i, 0))
    return pl.pallas_call(
        body, name="rms_bwd", grid=(t // tm,),
        in_specs=[row, row, pl.BlockSpec((tm, 1), lambda i: (i, 0)), pl.BlockSpec((1, d), lambda i: (0, 0)), row],
        out_specs=[row, pl.BlockSpec((1, d), lambda i: (0, 0))],
        out_shape=[SDS((t, d), F32), SDS((1, d), F32)],
        compiler_params=_params(("arbitrary",)),
    )(dh, x, r, w, g)


def loss_grad(y, target, tm):
    t, d = y.shape

    def body(y_ref, t_ref, l_ref, g_ref):
        i = pl.program_id(0)
        err = y_ref[...] - t_ref[...]
        g_ref[...] = err / d
        part = 0.5 * jnp.sum(jnp.mean(err * err, axis=-1, keepdims=True), axis=0, keepdims=True)

        @pl.when(i == 0)
        def _():
            l_ref[...] = part

        @pl.when(i > 0)
        def _():
            l_ref[...] += part

    row = pl.BlockSpec((tm, d), lambda i: (i, 0))
    return pl.pallas_call(
        body, name="loss_grad", grid=(t // tm,), in_specs=[row, row],
        out_specs=[pl.BlockSpec((1, 1), lambda i: (0, 0)), row],
        out_shape=[SDS((1, 1), F32), SDS((t, d), F32)],
        compiler_params=_params(("arbitrary",)),
    )(y, target)


def mm_nn(a, b, out_dtype, tm, tn, name, comm=None):
    m, k = a.shape
    n = b.shape[1]

    def body(a_ref, b_ref, o_ref):
        o_ref[...] = jnp.dot(a_ref[...], b_ref[...], preferred_element_type=F32).astype(out_dtype)

    outs, couts = _call(
        body, (a, b), name=name, grid=(n // tn, m // tm),
        in_specs=[pl.BlockSpec((tm, k), lambda j, i: (i, 0)), pl.BlockSpec((k, tn), lambda j, i: (0, j))],
        out_specs=[pl.BlockSpec((tm, tn), lambda j, i: (i, j))],
        out_shape=[SDS((m, n), out_dtype)], sem=("arbitrary", "arbitrary"), comm=comm)
    return outs[0], couts


def mm_nt(a, b, out_dtype, tm, tn, tk, name, comm=None, extra=None):
    m, k = a.shape
    n = b.shape[0]
    nk = k // tk

    def body(a_ref, b_ref, *rest):
        o_ref, acc_ref = rest[-2:]
        kk = pl.program_id(2)
        part = lax.dot_general(a_ref[...], b_ref[...], NT, preferred_element_type=F32)
        if nk == 1:
            if extra is not None:
                part = part + lax.dot_general(rest[0][...], rest[1][...], NT, preferred_element_type=F32)
            o_ref[...] = part.astype(out_dtype)
        else:
            @pl.when(kk == 0)
            def _():
                if extra is None:
                    acc_ref[...] = part
                else:
                    acc_ref[...] = part + lax.dot_general(rest[0][...], rest[1][...], NT, preferred_element_type=F32)

            @pl.when(kk > 0)
            def _():
                acc_ref[...] += part

            @pl.when(kk == nk - 1)
            def _():
                o_ref[...] = acc_ref[...].astype(out_dtype)

    more_specs = [] if extra is None else [pl.BlockSpec((tm, extra[0].shape[1]), lambda i, j, kk: (i, 0)),
                                           pl.BlockSpec((tn, extra[1].shape[1]), lambda i, j, kk: (j, 0))]
    outs, couts = _call(
        body, (a, b) + tuple(extra or ()), name=name, grid=(m // tm, n // tn, nk),
        in_specs=[pl.BlockSpec((tm, tk), lambda i, j, kk: (i, kk)), pl.BlockSpec((tn, tk), lambda i, j, kk: (j, kk))] + more_specs,
        out_specs=[pl.BlockSpec((tm, tn), lambda i, j, kk: (i, j))],
        out_shape=[SDS((m, n), out_dtype)],
        scratch_shapes=[pltpu.VMEM((tm, tn) if nk > 1 else (8, LANES), F32)],
        sem=("arbitrary", "arbitrary", "arbitrary"), comm=comm)
    return outs[0], couts


def mm_tn(a, b, tm, tn, name, comm=None):
    t, m = a.shape
    n = b.shape[1]

    def body(a_ref, b_ref, o_ref):
        o_ref[...] = lax.dot_general(a_ref[...], b_ref[...], TN, preferred_element_type=F32)

    outs, couts = _call(
        body, (a, b), name=name, grid=(m // tm, n // tn),
        in_specs=[pl.BlockSpec((t, tm), lambda i, j: (0, i)), pl.BlockSpec((t, tn), lambda i, j: (0, j))],
        out_specs=[pl.BlockSpec((tm, tn), lambda i, j: (i, j))],
        out_shape=[SDS((m, n), F32)], sem=("arbitrary", "arbitrary"), comm=comm)
    return outs[0], couts


def _window_sums(ext, n_rows, lookahead):
    def sh(v, k):
        return pltpu.roll(v, (n_rows - k) if lookahead else k, 0)
    s2 = ext + sh(ext, 1)
    s4 = s2 + sh(s2, 2)
    s8 = s4 + sh(s4, 4)
    s16 = s8 + sh(s8, 8)
    return (s2, s4, s8, s16)


def _pool_counts(i, tm, w):
    tpos = i * tm + lax.broadcasted_iota(jnp.int32, (tm, 1), 0)
    return jnp.minimum(tpos + 1, w).astype(F32)


def _pooled(uc_ref, up_ref, i, tm):
    cur = uc_ref[...]
    prev = jnp.where(i > 0, up_ref[...], 0.0)
    ext = jnp.concatenate([prev, cur], axis=0)
    return cur, _window_sums(ext, tm + POOL_HALO, False)


def pool_fwd(proj, mixw, scale, tm):
    t = proj.shape[0]
    pw = scale.shape[1]
    gw = pw // len(POOL_WINDOWS)
    nh = tm // POOL_HALO

    def body(uc_ref, up_ref, g_ref, w_ref, s_ref, o_ref):
        i = pl.program_id(0)
        cur, sums = _pooled(uc_ref, up_ref, i, tm)
        for g, w in enumerate(POOL_WINDOWS):
            cols = slice(g * gw, (g + 1) * gw)
            pooled = sums[g][POOL_HALO:, cols] / _pool_counts(i, tm, w) - cur[:, cols]
            mixed = jnp.dot(pooled.astype(BF16), w_ref[g], preferred_element_type=F32)
            o_ref[:, cols] = (mixed * s_ref[:, cols] * _silu(g_ref[:, cols])).astype(BF16)

    return pl.pallas_call(
        body, name="pool_fwd", grid=(t // tm,),
        in_specs=[pl.BlockSpec((tm, pw), lambda i: (i, 0)),
                  pl.BlockSpec((POOL_HALO, pw), lambda i: (jnp.maximum(i * nh - 1, 0), 0)),
                  pl.BlockSpec((tm, pw), lambda i: (i, 1)),
                  pl.BlockSpec(mixw.shape, lambda i: (0, 0, 0)),
                  pl.BlockSpec((1, pw), lambda i: (0, 0))],
        out_specs=pl.BlockSpec((tm, pw), lambda i: (i, 0)),
        out_shape=SDS((t, 2 * pw), BF16),
        compiler_params=_params(("arbitrary",)),
    )(proj, proj, proj, mixw, scale)


def pool_bwd_a(dmixed, proj, mixw, scale, tm):
    t, e = proj.shape
    pw = scale.shape[1]
    ng = len(POOL_WINDOWS)
    gw = pw // ng
    nh = tm // POOL_HALO

    def body(dy_ref, uc_ref, up_ref, g_ref, w_ref, s_ref, dg_ref, dq_ref, ds_ref, dw_ref):
        i = pl.program_id(0)

        @pl.when(i == 0)
        def _():
            ds_ref[...] = jnp.zeros_like(ds_ref)
            dw_ref[...] = jnp.zeros_like(dw_ref)

        cur, sums = _pooled(uc_ref, up_ref, i, tm)
        for g, w in enumerate(POOL_WINDOWS):
            cols = slice(g * gw, (g + 1) * gw)
            cnt = _pool_counts(i, tm, w)
            pooled = (sums[g][POOL_HALO:, cols] / cnt - cur[:, cols]).astype(BF16)
            mixed = jnp.dot(pooled, w_ref[g], preferred_element_type=F32)
            gate = g_ref[:, cols]
            dy = dy_ref[:, cols]
            sc = s_ref[:, cols]
            dg_ref[:, cols] = (dy * mixed * sc * _dsilu(gate)).astype(BF16)
            ds = dy * _silu(gate)
            ds_ref[:, cols] += jnp.sum(ds * mixed, axis=0, keepdims=True)
            dmix = (ds * sc).astype(BF16)
            dw_ref[g] += lax.dot_general(pooled, dmix, TN, preferred_element_type=F32)
            dq_ref[:, cols] = lax.dot_general(dmix, w_ref[g], NT, preferred_element_type=F32) / cnt

    return pl.pallas_call(
        body, name="pool_bwd_a", grid=(t // tm,),
        in_specs=[pl.BlockSpec((tm, pw), lambda i: (i, 0)),
                  pl.BlockSpec((tm, pw), lambda i: (i, 0)),
                  pl.BlockSpec((POOL_HALO, pw), lambda i: (jnp.maximum(i * nh - 1, 0), 0)),
                  pl.BlockSpec((tm, pw), lambda i: (i, 1)),
                  pl.BlockSpec(mixw.shape, lambda i: (0, 0, 0)),
                  pl.BlockSpec((1, pw), lambda i: (0, 0))],
        out_specs=[pl.BlockSpec((tm, pw), lambda i: (i, 1)),
                   pl.BlockSpec((tm, pw), lambda i: (i, 0)),
                   pl.BlockSpec((1, pw), lambda i: (0, 0)),
                   pl.BlockSpec((ng, gw, gw), lambda i: (0, 0, 0))],
        out_shape=[SDS((t, e), BF16), SDS((t, pw), F32), SDS((1, pw), F32), SDS((ng, gw, gw), F32)],
        compiler_params=_params(("arbitrary",)),
    )(dmixed, proj, proj, proj, mixw, scale)


def pool_bwd_b(dq, dproj, tm):
    t, pw = dq.shape
    gw = pw // len(POOL_WINDOWS)
    nh = tm // POOL_HALO
    nt = t // tm

    def body(c_ref, n_ref, alias_ref, o_ref):
        i = pl.program_id(0)
        cur = c_ref[...]
        nxt = jnp.where(i < nt - 1, n_ref[...], 0.0)
        sums = _window_sums(jnp.concatenate([cur, nxt], axis=0), tm + POOL_HALO, True)
        for g, w in enumerate(POOL_WINDOWS):
            cols = slice(g * gw, (g + 1) * gw)
            o_ref[:, cols] = (sums[g][:tm, cols] - cur[:, cols] * _pool_counts(i, tm, w)).astype(BF16)

    return pl.pallas_call(
        body, name="pool_bwd_b", grid=(nt,),
        in_specs=[pl.BlockSpec((tm, pw), lambda i: (i, 0)),
                  pl.BlockSpec((POOL_HALO, pw), lambda i: (jnp.minimum((i + 1) * nh, t // POOL_HALO - 1), 0)),
                  pl.BlockSpec(memory_space=pl.ANY)],
        out_specs=pl.BlockSpec((tm, pw), lambda i: (i, 0)),
        out_shape=SDS(dproj.shape, dproj.dtype),
        input_output_aliases={2: 0},
        compiler_params=_params(("arbitrary",)),
    )(dq, dq, dproj)


ELEMENTWISE_LANE_CHUNK = 256


def _lane_chunks(width):
    return [slice(c, c + ELEMENTWISE_LANE_CHUNK) for c in range(0, width, ELEMENTWISE_LANE_CHUNK)]


def _conv_pre(xc_ref, xp_ref, w_ref, b_ref, i, cols):
    cur = xc_ref[:, cols]
    prev = jnp.where(i > 0, xp_ref[:, cols], 0.0)
    ext = jnp.concatenate([prev, cur], axis=0)
    taps = [pltpu.roll(ext, CONV_WIDTH - 1 - k, 0)[CONV_HALO:] for k in range(CONV_WIDTH - 1)] + [cur]
    pre = b_ref[:, cols]
    for k in range(CONV_WIDTH):
        pre = pre + w_ref[k:k + 1, cols] * taps[k]
    return pre, taps


def conv_fwd(proj, conv_w, conv_b, col_block, tm, comm=None):
    t = proj.shape[0]
    cd = conv_b.shape[1]
    nh = tm // CONV_HALO

    def body(xc_ref, xp_ref, w_ref, b_ref, o_ref):
        i = pl.program_id(0)
        for cols in _lane_chunks(cd):
            pre, _ = _conv_pre(xc_ref, xp_ref, w_ref, b_ref, i, cols)
            o_ref[:, cols] = _silu(pre)

    outs, couts = _call(
        body, (proj, proj, conv_w, conv_b), name="conv_fwd", grid=(t // tm,),
        in_specs=[pl.BlockSpec((tm, cd), lambda i: (i, col_block)),
                  pl.BlockSpec((CONV_HALO, cd), lambda i: (jnp.maximum(i * nh - 1, 0), col_block)),
                  pl.BlockSpec((CONV_WIDTH, cd), lambda i: (0, 0)),
                  pl.BlockSpec((1, cd), lambda i: (0, 0))],
        out_specs=[pl.BlockSpec((tm, cd), lambda i: (i, 0))],
        out_shape=[SDS((t, cd), F32)], sem=("arbitrary",), comm=comm)
    return outs[0], couts


def conv_bwd_a(dxs, db, dc, proj, conv_w, conv_b, col_block, tm, comm=None):
    t = proj.shape[0]
    cd = conv_b.shape[1]
    sw = dxs.shape[1]
    gn = db.shape[1]
    nh = tm // CONV_HALO

    def body(dx_ref, db_ref, dc_ref, xc_ref, xp_ref, w_ref, b_ref, dp_ref, dw_ref, dbias_ref):
        i = pl.program_id(0)

        @pl.when(i == 0)
        def _():
            dw_ref[...] = jnp.zeros_like(dw_ref)
            dbias_ref[...] = jnp.zeros_like(dbias_ref)

        for cols in _lane_chunks(cd):
            pre, taps = _conv_pre(xc_ref, xp_ref, w_ref, b_ref, i, cols)
            if cols.start < sw:
                dact = dx_ref[:, cols]
            elif cols.start < sw + gn:
                dact = db_ref[:, cols.start - sw:cols.stop - sw]
            else:
                dact = dc_ref[:, cols.start - sw - gn:cols.stop - sw - gn]
            dpre = dact * _dsilu(pre)
            dp_ref[:, cols] = dpre
            dbias_ref[:, cols] += jnp.sum(dpre, axis=0, keepdims=True)
            for k in range(CONV_WIDTH):
                dw_ref[k:k + 1, cols] += jnp.sum(dpre * taps[k], axis=0, keepdims=True)

    return _call(
        body, (dxs, db, dc, proj, proj, conv_w, conv_b), name="conv_bwd_a", grid=(t // tm,),
        in_specs=[pl.BlockSpec((tm, sw), lambda i: (i, 0)), pl.BlockSpec((tm, gn), lambda i: (i, 0)),
                  pl.BlockSpec((tm, gn), lambda i: (i, 0)),
                  pl.BlockSpec((tm, cd), lambda i: (i, col_block)),
                  pl.BlockSpec((CONV_HALO, cd), lambda i: (jnp.maximum(i * nh - 1, 0), col_block)),
                  pl.BlockSpec((CONV_WIDTH, cd), lambda i: (0, 0)),
                  pl.BlockSpec((1, cd), lambda i: (0, 0))],
        out_specs=[pl.BlockSpec((tm, cd), lambda i: (i, 0)),
                   pl.BlockSpec((CONV_WIDTH, cd), lambda i: (0, 0)),
                   pl.BlockSpec((1, cd), lambda i: (0, 0))],
        out_shape=[SDS((t, cd), F32), SDS((CONV_WIDTH, cd), F32), SDS((1, cd), F32)],
        sem=("arbitrary",), comm=comm)


def conv_bwd_b(dpre, conv_w, dproj, col_block, tm):
    t, cd = dpre.shape
    nh = tm // CONV_HALO
    nt = t // tm

    def body(c_ref, n_ref, w_ref, alias_ref, o_ref):
        i = pl.program_id(0)
        n = tm + CONV_HALO
        for cols in _lane_chunks(cd):
            cur = c_ref[:, cols]
            nxt = jnp.where(i < nt - 1, n_ref[:, cols], 0.0)
            ext = jnp.concatenate([cur, nxt], axis=0)
            acc = w_ref[CONV_WIDTH - 1:CONV_WIDTH, cols] * cur
            for k in range(CONV_WIDTH - 1):
                acc = acc + w_ref[k:k + 1, cols] * pltpu.roll(ext, n - (CONV_WIDTH - 1 - k), 0)[:tm]
            o_ref[:, cols] = acc.astype(BF16)

    return pl.pallas_call(
        body, name="conv_bwd_b", grid=(nt,),
        in_specs=[pl.BlockSpec((tm, cd), lambda i: (i, 0)),
                  pl.BlockSpec((CONV_HALO, cd), lambda i: (jnp.minimum((i + 1) * nh, t // CONV_HALO - 1), 0)),
                  pl.BlockSpec((CONV_WIDTH, cd), lambda i: (0, 0)),
                  pl.BlockSpec(memory_space=pl.ANY)],
        out_specs=pl.BlockSpec((tm, cd), lambda i: (i, col_block)),
        out_shape=SDS(dproj.shape, dproj.dtype),
        input_output_aliases={3: 0},
        compiler_params=_params(("arbitrary",)),
    )(dpre, dpre, conv_w, dproj)


def _softplus(v):
    return jnp.maximum(v, 0.0) + jnp.log(1.0 + jnp.exp(-jnp.abs(v)))


def _ssd_chunk_terms(dtr_ref, bias_ref, a_ref, n_heads):
    q = SSD_CHUNK
    lane = lax.broadcasted_iota(jnp.int32, (1, LANES), 1)
    pre = dtr_ref[...] + bias_ref[...]
    dt = jnp.where(lane < n_heads, _softplus(pre), 0.0)
    a = jnp.where(lane < n_heads, -jnp.exp(a_ref[...]), 0.0)
    row = lax.broadcasted_iota(jnp.int32, (q, q), 0)
    col = lax.broadcasted_iota(jnp.int32, (q, q), 1)
    causal = row >= col
    acs = jnp.dot(causal.astype(F32), dt * a, precision=HIGHEST, preferred_element_type=F32)
    last = acs[q - 1:q, :]
    return dict(pre=pre, dt=dt, a=a, acs=acs, acs_t=acs.T, eacs=jnp.exp(acs), dstate=jnp.exp(last - acs),
                cdec=jnp.exp(last), causal=causal, diag=row == col, lane=lane)


_TERM_FIELDS = ("pre", "dt", "acs", "acs_t", "eacs", "dstate", "cdec")


def _prefetched_terms(step, dtr_ref, dtn_ref, bias_ref, a_ref, n_heads, terms_ref):
    q = SSD_CHUNK

    def store(slot, tm_):
        for f, name in enumerate(_TERM_FIELDS):
            terms_ref[slot, f] = jnp.broadcast_to(tm_[name], (q, LANES))

    @pl.when(step == 0)
    def _():
        store(0, _ssd_chunk_terms(dtr_ref, bias_ref, a_ref, n_heads))

    slot = lax.rem(step, 2)
    nxt = _ssd_chunk_terms(dtn_ref, bias_ref, a_ref, n_heads)
    tm_ = dict(nxt, **{name: terms_ref[slot, f] for f, name in enumerate(_TERM_FIELDS)})
    tm_["cdec"] = tm_["cdec"][0:1]
    return tm_, lambda: store(1 - slot, nxt)


def _pair_cols(lo, v, h):
    if v.shape[0] < 8:
        return jnp.where(lo, v[:, h:h + 1], v[:, h + 1:h + 2])
    idx = jnp.broadcast_to(jnp.where(lo, h, h + 1).astype(jnp.int32), v.shape)
    return jnp.take_along_axis(v, idx, axis=1, mode="promise_in_bounds")


def _pair_decay(tm_, cb, h):
    l0 = jnp.exp(jnp.where(tm_["causal"], tm_["acs"][:, h:h + 1] - tm_["acs_t"][h:h + 1, :], -jnp.inf))
    l1 = jnp.exp(jnp.where(tm_["causal"], tm_["acs"][:, h + 1:h + 2] - tm_["acs_t"][h + 1:h + 2, :], -jnp.inf))
    return l0, l1, jnp.concatenate([cb * l0, cb * l1], axis=1)


def _pair_decay_t(tm_, cbt, h):
    upper = jnp.logical_not(tm_["causal"]) | tm_["diag"]
    t0 = jnp.exp(jnp.where(upper, tm_["acs_t"][h:h + 1, :] - tm_["acs"][:, h:h + 1], -jnp.inf))
    t1 = jnp.exp(jnp.where(upper, tm_["acs_t"][h + 1:h + 2, :] - tm_["acs"][:, h + 1:h + 2], -jnp.inf))
    return jnp.concatenate([cbt * t0, cbt * t1], axis=0).astype(BF16)


def _block_diag(lo, xdt):
    return jnp.concatenate([jnp.where(lo, xdt, 0.0), jnp.where(lo, 0.0, xdt)], axis=0).astype(BF16)


def ssd_fwd(xbc, dt_raw, dt_bias, a_log, n_heads, comm=None):
    t = xbc.shape[0]
    q = SSD_CHUNK
    gn = SSD_GROUPS * SSD_STATE
    sw = n_heads * SSD_HEAD_DIM
    n_pairs = n_heads // 2
    pairs_per_group = n_pairs // SSD_GROUPS
    nc = t // q
    bblk = sw // gn

    def body(xs_ref, b_ref, c_ref, dtr_ref, dtn_ref, bias_ref, a_ref, y_ref, sin_ref, state, terms_ref):
        @pl.when(pl.program_id(0) == 0)
        def _():
            state[...] = jnp.zeros_like(state)

        tm_, keep_next = _prefetched_terms(pl.program_id(0), dtr_ref, dtn_ref, bias_ref, a_ref, n_heads, terms_ref)
        lo = tm_["lane"] < SSD_HEAD_DIM
        for g in range(SSD_GROUPS):
            gcols = slice(g * SSD_STATE, (g + 1) * SSD_STATE)
            bg = b_ref[:, gcols].astype(BF16)
            bg_t = b_ref[:, gcols].T.astype(BF16)
            cg = c_ref[:, gcols].astype(BF16)
            cb = lax.dot_general(cg, bg, NT, preferred_element_type=F32)
            for j in range(pairs_per_group):
                p = g * pairs_per_group + j
                h = 2 * p
                pcols = slice(p * LANES, (p + 1) * LANES)
                _, _, mcat = _pair_decay(tm_, cb, h)
                xdt = xs_ref[:, pcols] * _pair_cols(lo, tm_["dt"], h)
                ydiag = jnp.dot(mcat.astype(BF16), _block_diag(lo, xdt), preferred_element_type=F32)
                st = state[p]
                sin_ref[0, p] = st
                yoff = jnp.dot(cg, st.astype(BF16), preferred_element_type=F32) * _pair_cols(lo, tm_["eacs"], h)
                y_ref[:, pcols] = ydiag + yoff
                xw = (xdt * _pair_cols(lo, tm_["dstate"], h)).astype(BF16)
                state[p] = st * _pair_cols(lo, tm_["cdec"], h) + jnp.dot(bg_t, xw, preferred_element_type=F32)
        keep_next()

    vec = pl.BlockSpec((1, LANES), lambda c: (0, 0))
    return _call(
        body, (xbc, xbc, xbc, dt_raw, dt_raw, dt_bias, a_log), name="ssd_fwd", grid=(nc,),
        in_specs=[pl.BlockSpec((q, sw), lambda c: (c, 0)),
                  pl.BlockSpec((q, gn), lambda c: (c, bblk)),
                  pl.BlockSpec((q, gn), lambda c: (c, bblk + 1)),
                  pl.BlockSpec((q, LANES), lambda c: (c, 0)),
                  pl.BlockSpec((q, LANES), lambda c: (jnp.minimum(c + 1, nc - 1), 0)), vec, vec],
        out_specs=[pl.BlockSpec((q, sw), lambda c: (c, 0)),
                   pl.BlockSpec((1, n_pairs, SSD_STATE, LANES), lambda c: (c, 0, 0, 0))],
        out_shape=[SDS((t, sw), F32), SDS((nc, n_pairs, SSD_STATE, LANES), F32)],
        scratch_shapes=[pltpu.VMEM((n_pairs, SSD_STATE, LANES), F32), pltpu.VMEM((2, len(_TERM_FIELDS), q, LANES), F32)],
        sem=("arbitrary",), comm=comm)


def ssd_bwd(dy, xbc, dt_raw, dt_bias, a_log, d_full, s_in, n_heads, comm=None):
    t = xbc.shape[0]
    q = SSD_CHUNK
    gn = SSD_GROUPS * SSD_STATE
    sw = n_heads * SSD_HEAD_DIM
    n_pairs = n_heads // 2
    pairs_per_group = n_pairs // SSD_GROUPS
    nc = t // q
    bblk = sw // gn

    def body(dy_ref, xs_ref, b_ref, c_ref, dtr_ref, dtn_ref, bias_ref, a_ref, dsk_ref, sin_ref,
             dxs_ref, db_ref, dc_ref, ddtr_ref, dbias_ref, dalog_ref,
             dstate, tbuf, xbuf, rbuf, acc_a, acc_b, sel_ref, terms_ref):
        i = pl.program_id(0)

        @pl.when(i == 0)
        def _():
            dstate[...] = jnp.zeros_like(dstate)
            rbuf[...] = jnp.zeros_like(rbuf)
            acc_a[...] = jnp.zeros_like(acc_a)
            acc_b[...] = jnp.zeros_like(acc_b)
            sel_ref[...] = _head_selector(sw, SSD_HEAD_DIM)

        tm_, keep_next = _prefetched_terms(i, dtr_ref, dtn_ref, bias_ref, a_ref, n_heads, terms_ref)
        lane = tm_["lane"]
        lo = lane < SSD_HEAD_DIM
        head_row = lax.broadcasted_iota(jnp.int32, (LANES, 1), 0)
        rows = jnp.zeros((q, LANES), F32)
        cols_t = jnp.zeros((LANES, q), F32)
        for g in range(SSD_GROUPS):
            gcols = slice(g * SSD_STATE, (g + 1) * SSD_STATE)
            bg = b_ref[:, gcols].astype(BF16)
            cg = c_ref[:, gcols].astype(BF16)
            cg_t = c_ref[:, gcols].T.astype(BF16)
            cb = lax.dot_general(cg, bg, NT, preferred_element_type=F32)
            cbt = lax.dot_general(bg, cg, NT, preferred_element_type=F32)
            dcb = jnp.zeros((q, q), F32)
            db_acc = jnp.zeros((q, SSD_STATE), F32)
            dc_acc = jnp.zeros((q, SSD_STATE), F32)
            for j in range(pairs_per_group):
                p = g * pairs_per_group + j
                h = 2 * p
                pcols = slice(p * LANES, (p + 1) * LANES)
                l0, l1, mcat = _pair_decay(tm_, cb, h)
                xp = xs_ref[:, pcols]
                dtp = _pair_cols(lo, tm_["dt"], h)
                xdt = xp * dtp
                xbd = _block_diag(lo, xdt)
                dyp = dy_ref[:, pcols]
                dyb = dyp.astype(BF16)
                dsb = _pair_cols(lo, tm_["dstate"], h)
                cdr = _pair_cols(lo, tm_["cdec"], h)
                eb = _pair_cols(lo, tm_["eacs"], h)
                st = sin_ref[0, p]
                stb = st.astype(BF16)
                dst = dstate[p]
                dstb = dst.astype(BF16)
                dye = (dyp * eb).astype(BF16)
                both = jnp.dot(_pair_decay_t(tm_, cbt, h), dyb, preferred_element_type=F32)
                dx_state = jnp.dot(bg, dstb, preferred_element_type=F32) * dsb
                dxdt = jnp.where(lo, both[:q], both[q:]) + dx_state
                dmcat = lax.dot_general(dyb, xbd, NT, preferred_element_type=F32)
                dcb = dcb + dmcat[:, :q] * l0 + dmcat[:, q:] * l1
                dseg = dmcat * mcat
                csum = jnp.sum(dseg, axis=0, keepdims=True)
                rows = (rows + jnp.where(lane == h, jnp.sum(dseg[:, :q], axis=1, keepdims=True), 0.0)
                        + jnp.where(lane == h + 1, jnp.sum(dseg[:, q:], axis=1, keepdims=True), 0.0))
                cols_t = (cols_t + jnp.where(head_row == h, csum[:, :q], 0.0)
                          + jnp.where(head_row == h + 1, csum[:, q:], 0.0))
                dc_acc = dc_acc + lax.dot_general(dye, stb, NT, preferred_element_type=F32)
                db_acc = db_acc + lax.dot_general((xdt * dsb).astype(BF16), dstb, NT, preferred_element_type=F32)
                yoff = jnp.dot(cg, stb, preferred_element_type=F32) * eb
                tbuf[:, pcols] = dyp * yoff - xdt * dx_state
                xbuf[:, pcols] = dxdt * xp
                rbuf[0:1, pcols] = (jnp.sum(xdt * dx_state, axis=0, keepdims=True)
                                    + cdr * jnp.sum(dst * st, axis=0, keepdims=True))
                dxs_ref[:, pcols] = dxdt * dtp + dyp * dsk_ref[:, pcols]
                dstate[p] = dst * cdr + jnp.dot(cg_t, dye, preferred_element_type=F32)
            dcbb = dcb.astype(BF16)
            dc_ref[:, gcols] = dc_acc + jnp.dot(dcbb, bg, preferred_element_type=F32)
            db_ref[:, gcols] = db_acc + lax.dot_general(dcbb, cg, TN, preferred_element_type=F32)

        sel = sel_ref[...]
        dacs = rows - cols_t.T + _split_dot(tbuf[...], sel)
        carry = _split_dot(rbuf[...], sel)[0:1]
        anti = jnp.logical_not(tm_["causal"]) | tm_["diag"]
        da = jnp.dot(anti.astype(F32), dacs, precision=HIGHEST, preferred_element_type=F32) + carry
        ddt = da * tm_["a"] + _split_dot(xbuf[...], sel)
        ddtr = jnp.where(tm_["lane"] < n_heads, ddt * jax.nn.sigmoid(tm_["pre"]), 0.0)
        ddtr_ref[...] = ddtr.astype(BF16)
        acc_b[...] += jnp.sum(ddtr, axis=0, keepdims=True)
        acc_a[...] += jnp.sum(da * tm_["dt"], axis=0, keepdims=True)
        keep_next()

        @pl.when(i == nc - 1)
        def _():
            dbias_ref[...] = acc_b[...]
            dalog_ref[...] = acc_a[...] * tm_["a"]

    vec = pl.BlockSpec((1, LANES), lambda i: (0, 0))
    wide = pl.BlockSpec((q, sw), lambda i: (nc - 1 - i, 0))
    return _call(
        body, (dy, xbc, xbc, xbc, dt_raw, dt_raw, dt_bias, a_log, d_full, s_in), name="ssd_bwd", grid=(nc,),
        in_specs=[wide, wide,
                  pl.BlockSpec((q, gn), lambda i: (nc - 1 - i, bblk)),
                  pl.BlockSpec((q, gn), lambda i: (nc - 1 - i, bblk + 1)),
                  pl.BlockSpec((q, LANES), lambda i: (nc - 1 - i, 0)),
                  pl.BlockSpec((q, LANES), lambda i: (jnp.maximum(nc - 2 - i, 0), 0)), vec, vec,
                  pl.BlockSpec((1, sw), lambda i: (0, 0)),
                  pl.BlockSpec((1, n_pairs, SSD_STATE, LANES), lambda i: (nc - 1 - i, 0, 0, 0))],
        out_specs=[wide, pl.BlockSpec((q, gn), lambda i: (nc - 1 - i, 0)), pl.BlockSpec((q, gn), lambda i: (nc - 1 - i, 0)),
                   pl.BlockSpec((q, LANES), lambda i: (nc - 1 - i, 0)), vec, vec],
        out_shape=[SDS((t, sw), F32), SDS((t, gn), F32), SDS((t, gn), F32), SDS((t, LANES), BF16),
                   SDS((1, LANES), F32), SDS((1, LANES), F32)],
        scratch_shapes=[pltpu.VMEM((n_pairs, SSD_STATE, LANES), F32), pltpu.VMEM((q, sw), F32), pltpu.VMEM((q, sw), F32),
                        pltpu.VMEM((8, sw), F32), pltpu.VMEM((1, LANES), F32), pltpu.VMEM((1, LANES), F32),
                        pltpu.VMEM((sw, LANES), BF16), pltpu.VMEM((2, len(_TERM_FIELDS), q, LANES), F32)],
        sem=("arbitrary",), comm=comm)


def _gated(y_ref, xs_ref, z_ref, dsk_ref):
    y1 = y_ref[...] + dsk_ref[...] * xs_ref[...]
    return y1, y1 * _silu(z_ref[...])


def gate_norm_fwd(y, xbc, proj, d_full, norm_w, mixed, z_block, tm):
    t, sw = y.shape
    gw = sw // SSD_GROUPS

    def body(y_ref, xs_ref, z_ref, dsk_ref, nw_ref, alias_ref, o_ref):
        _, y2 = _gated(y_ref, xs_ref, z_ref, dsk_ref)
        for g in range(SSD_GROUPS):
            cols = slice(g * gw, (g + 1) * gw)
            blk = y2[:, cols]
            r = lax.rsqrt(jnp.mean(blk * blk, axis=-1, keepdims=True) + NORM_EPS)
            o_ref[:, cols] = (blk * r * nw_ref[:, cols]).astype(BF16)

    row = pl.BlockSpec((tm, sw), lambda i: (i, 0))
    vec = pl.BlockSpec((1, sw), lambda i: (0, 0))
    return pl.pallas_call(
        body, name="gate_norm_fwd", grid=(t // tm,),
        in_specs=[row, row, pl.BlockSpec((tm, sw), lambda i: (i, z_block)), vec, vec, pl.BlockSpec(memory_space=pl.ANY)],
        out_specs=pl.BlockSpec((tm, sw), lambda i: (i, 1)),
        out_shape=SDS(mixed.shape, mixed.dtype),
        input_output_aliases={5: 0},
        compiler_params=_params(("arbitrary",)),
    )(y, xbc, proj, d_full, norm_w, mixed)


def gate_norm_bwd(dmixed, y, xbc, proj, d_full, norm_w, dproj, z_block, tm):
    t, sw = y.shape
    gw = sw // SSD_GROUPS
    nt = t // tm

    def body(d_ref, y_ref, xs_ref, z_ref, dsk_ref, nw_ref, alias_ref, dy_ref, dz_ref, dnw_ref, dd_ref, acc_d):
        i = pl.program_id(0)

        @pl.when(i == 0)
        def _():
            dnw_ref[...] = jnp.zeros_like(dnw_ref)
            acc_d[...] = jnp.zeros_like(acc_d)

        y1, y2 = _gated(y_ref, xs_ref, z_ref, dsk_ref)
        d3 = d_ref[...]
        parts = []
        for g in range(SSD_GROUPS):
            cols = slice(g * gw, (g + 1) * gw)
            blk = y2[:, cols]
            r = lax.rsqrt(jnp.mean(blk * blk, axis=-1, keepdims=True) + NORM_EPS)
            n = blk * r
            dg = d3[:, cols]
            dnw_ref[:, cols] += jnp.sum(dg * n, axis=0, keepdims=True)
            parts.append(_norm_bwd(dg * nw_ref[:, cols], n, r))
        dy2 = jnp.concatenate(parts, axis=1)
        zv = z_ref[...]
        dz_ref[...] = (dy2 * y1 * _dsilu(zv)).astype(BF16)
        dy1 = dy2 * _silu(zv)
        dy_ref[...] = dy1
        acc_d[0:1, :] += jnp.sum(dy1 * xs_ref[...], axis=0, keepdims=True)

        @pl.when(i == nt - 1)
        def _():
            dd_ref[...] = _split_dot(acc_d[...], _head_selector(sw, SSD_HEAD_DIM))[0:1]

    row = pl.BlockSpec((tm, sw), lambda i: (i, 0))
    vec = pl.BlockSpec((1, sw), lambda i: (0, 0))
    return pl.pallas_call(
        body, name="gate_norm_bwd", grid=(nt,),
        in_specs=[pl.BlockSpec((tm, sw), lambda i: (i, 1)), row, row, pl.BlockSpec((tm, sw), lambda i: (i, z_block)),
                  vec, vec, pl.BlockSpec(memory_space=pl.ANY)],
        out_specs=[row, pl.BlockSpec((tm, sw), lambda i: (i, z_block)), vec, pl.BlockSpec((1, LANES), lambda i: (0, 0))],
        out_shape=[SDS((t, sw), F32), SDS(dproj.shape, dproj.dtype), SDS((1, sw), F32), SDS((1, LANES), F32)],
        scratch_shapes=[pltpu.VMEM((8, sw), F32)],
        input_output_aliases={6: 1},
        compiler_params=_params(("arbitrary",)),
    )(dmixed, y, xbc, proj, d_full, norm_w, dproj)


GATE_BLOCK = 1
Z_BLOCK = 2
CONV_BLOCK = 2


def _tiles(t):
    mm = dict(in_proj=(min(1024, t), 1024), dt_proj=(min(512, t), LANES), out_proj=(min(512, t), 1024),
              d_mixed=(min(512, t), 2048), dh=(min(512, t), 3072), dw_out=(512, 1024), dw_main=(1024, 1024),
              dw_dt=(512, LANES))
    return min(256, t), mm


def _place():
    x, y, c = lax.axis_index("x"), lax.axis_index("y"), lax.axis_index("c")
    return x, y, c, [(1 - x, y), (x, 1 - y), (1 - x, 1 - y)]


def gather_spread(shards, layer, rows=None, carry=None):
    n = len(shards)

    def make(ins, outs, ss, rs):
        x, y, c, chips = _place()
        mine = 4 * x + 2 * y + c
        peers = [(x, y, 1 - c)] + [(px, py, c) for px, py in chips]
        sends, locals_, arrivals = [], [], []
        for a in range(n):
            def place(ref, idx):
                return ref.at[idx] if rows is None else ref.at[idx, pl.ds(rows[0], rows[1])]

            src = place(ins[a], layer)
            locals_.append(pltpu.make_async_copy(src, place(outs[a], mine), ss.at[5 * a + 4]))
            for j, (px, py, pc) in enumerate(peers):
                sends.append(_remote(src, place(outs[a], mine), ss, rs, 5 * a + j, (px, py, pc)))
                arrivals.append(_remote(src, place(outs[a], 4 * px + 2 * py + pc), ss, rs, 5 * a + j, (px, py, pc)))
        return sends, locals_, arrivals

    return Comm(list(shards) + list(carry or []), [SDS((N_DEV,) + s.shape[1:], s.dtype) for s in shards],
                {n + a: a for a in range(n)} if carry else {}, 5 * n, make)


def gather_pass_on(gathered):
    def make(ins, outs, ss, rs):
        x, y, c, chips = _place()
        sends, arrivals = [], []
        for a in range(len(outs)):
            for j, (px, py) in enumerate(chips):
                blk, other = 4 * px + 2 * py + c, 4 * px + 2 * py + (1 - c)
                sends.append(_remote(outs[a].at[blk], outs[a].at[blk], ss, rs, 3 * a + j, (x, y, 1 - c)))
                arrivals.append(_remote(outs[a].at[other], outs[a].at[other], ss, rs, 3 * a + j, (x, y, 1 - c)))
        return sends, [], arrivals

    return Comm(gathered, [SDS(g.shape, g.dtype) for g in gathered], {a: a for a in range(len(gathered))},
                3 * len(gathered), make)


def sibling_swap(sends_):
    def make(ins, outs, ss, rs):
        x, y, c, _ = _place()
        cps = [_remote(ins[a], outs[a], ss, rs, a, (x, y, 1 - c)) for a in range(len(ins))]
        return cps, [], cps

    return Comm(sends_, [SDS(s.shape, s.dtype) for s in sends_], {}, len(sends_), make)


def chips_scatter(slabs, rows=None, carry=None):
    n = len(slabs)

    def make(ins, outs, ss, rs):
        x, y, c, chips = _place()
        mychip = 2 * x + y
        sends, arrivals = [], []

        def part(ref, slot):
            return ref.at[slot] if rows is None else ref.at[slot, pl.ds(rows[0], rows[1])]

        for a in range(n):
            for j, (px, py) in enumerate(chips):
                to_there = lax.rem(2 * px + py - mychip + 4, 4) - 1
                from_here = lax.rem(mychip - 2 * px - py + 4, 4) - 1
                sends.append(_remote(part(ins[a], to_there), part(outs[a], from_here), ss, rs, 3 * a + j, (px, py, c)))
                arrivals.append(_remote(part(ins[a], to_there), part(outs[a], to_there), ss, rs, 3 * a + j, (px, py, c)))
        return sends, [], arrivals

    return Comm(list(slabs) + list(carry or []), [SDS(s.shape, s.dtype) for s in slabs],
                {n + a: a for a in range(n)} if carry else {}, 3 * n, make)


def comm_only(comm, name):
    def body():
        pass

    return _call(body, (), name=name, grid=(), in_specs=[], out_specs=[], out_shape=[], comm=comm)[1]


def layer_fwd(x, p, nxt=None, late=None):
    t = x.shape[0]
    tm, mm = _tiles(t)
    n_heads = p["d_full"].shape[1] // SSD_HEAD_DIM
    rows_a = nxt[0][0].shape[1] if nxt else 0
    rows_o = nxt[1][0].shape[1] if nxt else 0
    cut_a = rows_a * (3 if late else 6) // 8
    cut_o = rows_o * 3 // 8
    h, r_pre = rms_fwd(x, p["pre_w"], tm)
    proj, got = mm_nn(h, p["w_main"], F32, *mm["in_proj"], "in_proj", merge_comms([
        gather_spread(late[0], late[1]) if late else None,
        gather_spread(nxt[0], nxt[2], rows=(0, cut_a)) if nxt else None]))
    n_late = len(late[0]) if late else 0
    dt_raw, got_late = mm_nn(h, p["w_dt"], F32, *mm["dt_proj"], "dt_proj", gather_pass_on(got[:n_late]) if late else None)
    if late:
        p = dict(p, **late[2](got_late))
    mixed = pool_fwd(proj, p["mixw"], p["pscale"], tm)
    xbc, got_o = conv_fwd(proj, p["conv_w"], p["conv_b"], CONV_BLOCK, tm,
                          gather_spread(nxt[1][:1], nxt[2], rows=(0, cut_o)) if nxt else None)
    (y, s_in), got_o = ssd_fwd(xbc, dt_raw, p["dt_bias"], p["a_log"], n_heads,
                               gather_spread(nxt[1][:1], nxt[2], rows=(cut_o, rows_o - cut_o), carry=got_o) if nxt else None)
    mixed = gate_norm_fwd(y, xbc, proj, p["d_full"], p["norm_w"], mixed, Z_BLOCK, tm)
    out, got_ab = mm_nn(mixed, p["w_out"], F32, *mm["out_proj"], "out_proj", merge_comms([
        gather_spread(nxt[0], nxt[2], rows=(cut_a, rows_a - cut_a), carry=got[n_late:]),
        gather_spread(nxt[1][1:], nxt[2])]) if nxt else None)
    (x_next, r_post), gathered = post_fwd(out, x, p["post_w"], tm,
                                          gather_pass_on(got_ab[:1] + got_o + got_ab[1:]) if nxt else None)
    return x_next, dict(x=x, h=h, r_pre=r_pre, proj=proj, dt_raw=dt_raw, xbc=xbc, y=y, s_in=s_in, mixed=mixed,
                        out=out, r_post=r_post), gathered, p


def _pair_sums(own, got):
    return [pair_sum(o, r, min(256, o.shape[1]), "pair_sum") for o, r in zip(own, got)]


def layer_bwd(g, s, p, split_in, split_rest, pending=None, last=False):
    t = g.shape[0]
    tm, mm = _tiles(t)
    d = g.shape[1]
    n_heads = p["d_full"].shape[1] // SSD_HEAD_DIM
    d_out, d_post = post_bwd(g, s["out"], s["r_post"], p["post_w"], tm)
    dmixed, got_sib = mm_nt(d_out, p["w_out"], F32, *mm["d_mixed"], d, "d_mixed", sibling_swap(pending[1]) if pending else None)
    chip_sums = _pair_sums(pending[0], got_sib) if pending else []
    dw_out, _ = mm_tn(s["mixed"], d_out, *mm["dw_out"], "dw_out")
    dproj, dq, d_pscale, d_mixw = pool_bwd_a(dmixed, s["proj"], p["mixw"], p["pscale"], tm)
    dproj = pool_bwd_b(dq, dproj, tm)
    own_rest, send_rest = split_rest(dw_out, d_mixw)
    dy, dproj, d_norm, d_dskip = gate_norm_bwd(dmixed, s["y"], s["xbc"], s["proj"], p["d_full"], p["norm_w"], dproj,
                                               Z_BLOCK, tm)
    rows_in = chip_sums[0].shape[1] if pending else 0
    cut_in = rows_in if last else rows_in // 2
    (dxs, db, dc, ddtr, d_dtb, d_alog), got = ssd_bwd(
        dy, s["xbc"], s["dt_raw"], p["dt_bias"], p["a_log"], p["d_full"], s["s_in"], n_heads,
        merge_comms([chips_scatter(chip_sums[:1], rows=(0, cut_in)) if pending else None,
                     sibling_swap(send_rest) if last else None]))
    got_first, my_sib_rest = (got[:1], got[1:]) if pending else ([], got)
    rest_comm = chips_scatter(chip_sums[1:]) if pending else None
    (dpre, d_convw, d_convb), got_rest = conv_bwd_a(dxs, db, dc, s["proj"], p["conv_w"], p["conv_b"], CONV_BLOCK, tm,
                                                    rest_comm if last else None)
    dproj = conv_bwd_b(dpre, p["conv_w"], dproj, CONV_BLOCK, tm)
    dw_main, got = mm_tn(s["h"], dproj, *mm["dw_main"], "dw_main",
                         chips_scatter(_pair_sums(own_rest, my_sib_rest)) if last else rest_comm)
    got_rest, my_chips_rest = (got_rest, got) if last else (got, [])
    dw_dt, _ = mm_tn(s["h"], ddtr, *mm["dw_dt"], "dw_dt")
    own_in, send_in = split_in(dw_main, dw_dt)
    my_sib_in = comm_only(sibling_swap(send_in), "grads_to_sibling") if last else []
    if last:
        dh_comm = chips_scatter(_pair_sums(own_in, my_sib_in))
    else:
        dh_comm = chips_scatter(chip_sums[:1], rows=(cut_in, rows_in - cut_in), carry=got_first) if pending else None
    dh, got = mm_nt(dproj, p["w_main"], F32, mm["dh"][0], d, mm["dh"][1], "dh_main", dh_comm, extra=(ddtr, p["w_dt"]))
    my_chips_in, got_first = (got, got_first) if last else ([], got if pending else got_first)
    gx, d_pre = rms_bwd(dh, s["x"], s["r_pre"], p["pre_w"], g, tm)
    small = dict(pre_w=d_pre, pscale=d_pscale, conv_w=d_convw, conv_b=d_convb, dt_bias=d_dtb, a_log=d_alog,
                 d_skip=d_dskip, norm_w=d_norm, post_w=d_post)
    done = [(got_sib, got_first + got_rest)] if pending else [None]
    if last:
        done.append((my_sib_in + my_sib_rest, my_chips_in + my_chips_rest))
    return gx, small, (own_in + own_rest, send_in + send_rest), done


def _two_level_gather(x_refs, out_slots, send_sems, recv_sems, local_sems):
    x, y, c, chips = _place()
    me, sibling = (x, y, c), (x, y, 1 - c)
    n = len(x_refs)

    def copy(a, k, block, to, src=None):
        return pltpu.make_async_remote_copy(
            src_ref=out_slots[a](*block) if src is None else src, dst_ref=out_slots[a](*block),
            send_sem=send_sems.at[7 * a + k], recv_sem=recv_sems.at[7 * a + k], device_id=to, device_id_type=MESH)

    mine = [pltpu.make_async_copy(x_refs[a], out_slots[a](*me), local_sems.at[a]) for a in range(n)]
    for cp in mine:
        cp.start()
    first = []
    for a in range(n):
        first.append(copy(a, 0, me, sibling, src=x_refs[a]))
        first += [copy(a, 1 + j, me, (*chip, c), src=x_refs[a]) for j, chip in enumerate(chips)]
    for cp in first:
        cp.start()
    passed = []
    for j, chip in enumerate(chips):
        for a in range(n):
            copy(a, 1 + j, (*chip, c), me).wait_recv()
            fwd = copy(a, 4 + j, (*chip, c), sibling)
            fwd.start()
            passed.append(fwd)
    for a in range(n):
        copy(a, 0, sibling, me).wait_recv()
        for j, chip in enumerate(chips):
            copy(a, 4 + j, (*chip, 1 - c), me).wait_recv()
    for cp in first + passed:
        cp.wait_send()
    for cp in mine:
        cp.wait()


def all_gather_hbm(shards, name):
    n = len(shards)

    def body(*refs):
        x_refs, out_refs = refs[:n], refs[n:2 * n]
        send_sems, recv_sems, local_sems = refs[2 * n:]
        slots = [lambda px, py, pc, o=o: o.at[:, 4 * px + 2 * py + pc] for o in out_refs]
        _two_level_gather(x_refs, slots, send_sems, recv_sems, local_sems)

    hbm = pl.BlockSpec(memory_space=pl.ANY)
    return pl.pallas_call(
        body, name=name,
        out_shape=[SDS((s.shape[0], N_DEV) + s.shape[1:], s.dtype) for s in shards],
        in_specs=[hbm] * n, out_specs=[hbm] * n,
        scratch_shapes=[pltpu.SemaphoreType.DMA((7 * n,)), pltpu.SemaphoreType.DMA((7 * n,)), pltpu.SemaphoreType.DMA((n,))],
    )(*shards)


def all_gather_vmem(block, name):
    r, c_ = block.shape

    def body(x_ref, out_ref, send_sems, recv_sems, local_sems):
        _two_level_gather([x_ref], [lambda px, py, pc: out_ref.at[4 * px + 2 * py + pc]], send_sems, recv_sems, local_sems)

    return pl.pallas_call(
        body, name=name, out_shape=SDS((N_DEV, r, c_), block.dtype),
        in_specs=[pl.BlockSpec(memory_space=pltpu.VMEM)], out_specs=pl.BlockSpec(memory_space=pltpu.VMEM),
        scratch_shapes=[pltpu.SemaphoreType.DMA((7,)), pltpu.SemaphoreType.DMA((7,)), pltpu.SemaphoreType.DMA((1,))],
        compiler_params=_params(),
    )(block)


def _block_tiles(cols):
    base = [(cols * i) // LANES for i in range(N_DEV)]
    ends = [-((-cols * (i + 1)) // LANES) for i in range(N_DEV)]
    return base, ends, max(e - b for b, e in zip(base, ends))


def _my_lane_offset(cols):
    me = 4 * lax.axis_index("x") + 2 * lax.axis_index("y") + lax.axis_index("c")
    return lax.rem(cols * me, LANES)


def shift_cast(w, tr):
    nl, r, cols = w.shape
    width = _block_tiles(cols)[2] * LANES

    def body(x_ref, o_ref, pad):
        pad[:, width - LANES:] = jnp.zeros((tr, LANES), F32)
        pad[:, :cols] = x_ref[...]
        o_ref[...] = pltpu.roll(pad[...], _my_lane_offset(cols), 1).astype(BF16)

    assert width - LANES <= cols
    return pl.pallas_call(
        body, name="shift_cast", grid=(nl, r // tr),
        in_specs=[pl.BlockSpec((pl.Squeezed(), tr, cols), lambda l, i: (l, i, 0))],
        out_specs=pl.BlockSpec((pl.Squeezed(), tr, width), lambda l, i: (l, i, 0)),
        out_shape=SDS((nl, r, width), BF16), scratch_shapes=[pltpu.VMEM((tr, width), F32)],
        compiler_params=_params(("arbitrary", "arbitrary")))(w)


def assemble_w_in(blocks, cols, n_tail, tr):
    _, r, width = blocks.shape
    base, ends, _ = _block_tiles(cols)
    total = ends[-1]
    main_tiles = (N_DEV * cols - n_tail) // LANES
    assert main_tiles == total - 1 and (N_DEV * cols - n_tail) % LANES == 0

    def body(b_ref, main_ref, tail_ref):
        for tile in range(total):
            parts = [b_ref[i, :, (tile - base[i]) * LANES:(tile - base[i] + 1) * LANES]
                     for i in range(N_DEV) if base[i] <= tile < ends[i]]
            val = parts[0] if len(parts) == 1 else parts[0] + parts[1]
            if tile < main_tiles:
                main_ref[:, tile * LANES:(tile + 1) * LANES] = val
            else:
                tail_ref[...] = val

    return pl.pallas_call(
        body, name="assemble_w_in", grid=(r // tr,),
        in_specs=[pl.BlockSpec((N_DEV, tr, width), lambda i: (0, i, 0))],
        out_specs=[pl.BlockSpec((tr, main_tiles * LANES), lambda i: (i, 0)), pl.BlockSpec((tr, LANES), lambda i: (i, 0))],
        out_shape=[SDS((r, main_tiles * LANES), blocks.dtype), SDS((r, LANES), blocks.dtype)],
        compiler_params=_params(("arbitrary",)),
    )(blocks)


def grad_blocks(dw_main, dw_tail, cols, tr):
    r = dw_main.shape[0]
    base, _, tpb = _block_tiles(cols)
    width = tpb * LANES

    def body(m_ref, t_ref, own_ref, send_ref):
        cat = jnp.concatenate([m_ref[...], t_ref[...]], axis=1)
        south = lax.axis_index("c") == 0
        for k in range(N_DEV // 2):
            a = cat[:, base[2 * k] * LANES:base[2 * k] * LANES + width]
            b = cat[:, base[2 * k + 1] * LANES:base[2 * k + 1] * LANES + width]
            own_ref[k] = jnp.where(south, a, b)
            send_ref[k] = jnp.where(south, b, a).astype(BF16)

    return pl.pallas_call(
        body, name="grad_blocks", grid=(r // tr,),
        in_specs=[pl.BlockSpec((tr, dw_main.shape[1]), lambda i: (i, 0)), pl.BlockSpec((tr, LANES), lambda i: (i, 0))],
        out_specs=[pl.BlockSpec((N_DEV // 2, tr, width), lambda i: (0, i, 0))] * 2,
        out_shape=[SDS((N_DEV // 2, r, width), F32), SDS((N_DEV // 2, r, width), BF16)],
        compiler_params=_params(("arbitrary",)),
    )(dw_main, dw_tail)


def _adamw(w, g, m, v):
    m = ADAM_B1 * m + (1.0 - ADAM_B1) * g
    v = ADAM_B2 * v + (1.0 - ADAM_B2) * jnp.square(g)
    m_hat = m / (1.0 - ADAM_B1 ** ADAM_STEP)
    v_hat = v / (1.0 - ADAM_B2 ** ADAM_STEP)
    delta = -ADAM_LR * (m_hat / (jnp.sqrt(v_hat) + ADAM_EPS) + ADAM_WD * w)
    return delta, m, v


def _my_chip():
    return 2 * lax.axis_index("x") + lax.axis_index("y")


def pair_sum(own, got, tr, name):
    k, r, c_ = own.shape
    others = lax.rem(_my_chip() + 1 + jnp.arange(k - 1, dtype=jnp.int32), k)

    def body(others_ref, a_ref, b_ref, o_ref):
        o_ref[...] = (a_ref[...] + b_ref[...].astype(F32)).astype(BF16)

    src = pl.BlockSpec((pl.Squeezed(), tr, c_), lambda s, i, oth: (oth[s], i, 0))
    return pl.pallas_call(
        body, name=name, out_shape=SDS((k - 1, r, c_), BF16),
        grid_spec=pltpu.PrefetchScalarGridSpec(
            num_scalar_prefetch=1, grid=(k - 1, r // tr), in_specs=[src, src],
            out_specs=pl.BlockSpec((pl.Squeezed(), tr, c_), lambda s, i, oth: (s, i, 0))),
        compiler_params=_params(("arbitrary", "arbitrary")),
    )(others, own, got)


def reduce_adam(own, got_sibling, got_chips, w, m, v, prev, layer, tr, name, shifted=False):
    nl, r, cols = w.shape
    c_ = own.shape[-1]
    n_scratch = 1 if shifted else 0
    chip = jnp.reshape(_my_chip(), (1,)).astype(jnp.int32)

    def body(chip_ref, own_ref, sib_ref, c0_ref, c1_ref, c2_ref, w_ref, m_ref, v_ref, *rest):
        g_ref, d_ref, nm_ref, nv_ref = rest[len(rest) - n_scratch - 4:len(rest) - n_scratch]
        g = (own_ref[...] + sib_ref[...].astype(F32) + c0_ref[...].astype(F32) + c1_ref[...].astype(F32)
             + c2_ref[...].astype(F32))
        if shifted:
            rest[-1][...] = pltpu.roll(g, c_ - _my_lane_offset(cols), 1)
            g = rest[-1][:, :cols]
        delta, nm, nv = _adamw(w_ref[...], g, m_ref[...], v_ref[...])
        g_ref[...] = g
        d_ref[...] = delta
        nm_ref[...] = nm
        nv_ref[...] = nv

    mine = pl.BlockSpec((pl.Squeezed(), tr, c_), lambda i, ch: (ch[0], i, 0))
    lay = pl.BlockSpec((pl.Squeezed(), tr, cols), lambda i, ch: (layer, i, 0))
    chips = [pl.BlockSpec((pl.Squeezed(), tr, c_), lambda i, ch, s=s: (s, i, 0)) for s in range(3)]
    in_specs = [mine, mine] + chips + [lay, lay, lay]
    args = [chip, own, got_sibling, got_chips, got_chips, got_chips, w, m, v]
    aliases = {}
    if prev is not None:
        in_specs += [pl.BlockSpec(memory_space=pl.ANY)] * 4
        aliases = {len(args) + k: k for k in range(4)}
        args += list(prev)
    return pl.pallas_call(
        body, name=name, out_shape=[SDS((nl, r, cols), F32)] * 4, input_output_aliases=aliases,
        grid_spec=pltpu.PrefetchScalarGridSpec(
            num_scalar_prefetch=1, grid=(r // tr,), in_specs=in_specs, out_specs=[lay] * 4,
            scratch_shapes=[pltpu.VMEM((tr, c_), F32)] * n_scratch),
        compiler_params=_params(("arbitrary",)),
    )(*args)


def sum_devices(packs):
    n, r, c_ = packs.shape

    def body(p_ref, o_ref):
        acc = p_ref[0]
        for k in range(1, n):
            acc = acc + p_ref[k]
        o_ref[...] = acc

    return pl.pallas_call(body, name="sum_devices", out_shape=SDS((r, c_), F32), compiler_params=_params())(packs)


def adam_small(w, g, m, v):
    def body(w_ref, g_ref, m_ref, v_ref, d_ref, nm_ref, nv_ref):
        delta, nm, nv = _adamw(w_ref[...], g_ref[...], m_ref[...], v_ref[...])
        d_ref[...] = delta
        nm_ref[...] = nm
        nv_ref[...] = nv

    return pl.pallas_call(body, name="adam_small", out_shape=[SDS(w.shape, F32)] * 3, compiler_params=_params())(w, g, m, v)


SMALL = ("pre_norm_w", "pool_scale", "conv_b", "dt_bias", "a_log", "d_skip", "_pad", "ssd_norm_w", "post_norm_w", "conv_w")


def _pack(parts):
    flat = jnp.concatenate([parts[k] for k in SMALL], axis=1).reshape(-1, LANES)
    return jnp.pad(flat, ((0, (-flat.shape[0]) % 8), (0, 0)))


def _unpack(pack, sizes, nl):
    total = sum(sizes[k] for k in SMALL)
    flat = pack[: nl * total // LANES].reshape(nl, total)
    out, o = {}, 0
    for k in SMALL:
        out[k] = flat[:, o:o + sizes[k]]
        o += sizes[k]
    return out


def kernel(x, pre_norm_w, w_in, pool_mix_w, pool_scale, conv_w, conv_b, dt_bias, a_log, d_skip, ssd_norm_w, w_out, post_norm_w, loss_target, m_pre_norm_w, m_w_in, m_pool_mix_w, m_pool_scale, m_conv_w, m_conv_b, m_dt_bias, m_a_log, m_d_skip, m_ssd_norm_w, m_w_out, m_post_norm_w, v_pre_norm_w, v_w_in, v_pool_mix_w, v_pool_scale, v_conv_w, v_conv_b, v_dt_bias, v_a_log, v_d_skip, v_ssd_norm_w, v_w_out, v_post_norm_w):
    cx, cy, cc = lax.axis_index("x"), lax.axis_index("y"), lax.axis_index("c")
    me = 4 * cx + 2 * cy + cc
    mychip = 2 * cx + cy
    nl, d, cols = w_in.shape
    t = x.shape[1]
    n_heads = a_log.shape[1]
    sw = n_heads * SSD_HEAD_DIM
    pw = pool_scale.shape[1]
    cd = conv_b.shape[1]
    ng, gsh, gw = pool_mix_w.shape[1:]
    e_main = N_DEV * cols - n_heads
    assert x.shape[0] == 1 and pw == sw and cd == sw + 2 * SSD_GROUPS * SSD_STATE and e_main == 2 * pw + sw + cd
    assert 2 * pw + sw == CONV_BLOCK * cd and n_heads <= LANES and t % SSD_CHUNK == 0 and gsh * N_DEV == gw
    tm, _ = _tiles(t)

    shards_a, shards_b = [shift_cast(w_in, tm)], [w_out.astype(BF16), pool_mix_w.astype(BF16), conv_w]
    pad_h = ((0, 0), (0, LANES - n_heads))

    def params_a(l, g_in):
        w_main, w_dt = assemble_w_in(g_in, cols, n_heads, tm)
        return dict(pre_w=pre_norm_w[l:l + 1], w_main=w_main, w_dt=w_dt, pscale=pool_scale[l:l + 1], conv_b=conv_b[l:l + 1],
                    dt_bias=jnp.pad(dt_bias[l:l + 1], pad_h), a_log=jnp.pad(a_log[l:l + 1], pad_h),
                    d_full=jnp.repeat(d_skip[l:l + 1], SSD_HEAD_DIM, axis=1), norm_w=ssd_norm_w[l:l + 1],
                    post_w=post_norm_w[l:l + 1])

    def params_b(g_out, g_mix, g_conv):
        return dict(mixw=g_mix.transpose(1, 0, 2, 3).reshape(ng, gw, gw), conv_w=g_conv.transpose(1, 0, 2).reshape(CONV_WIDTH, cd),
                    w_out=g_out.reshape(N_DEV * w_out.shape[1], d))

    xs = x[0]
    saved, params = [], []
    p = params_a(0, all_gather_hbm([shards_a[0][:1]], "gather_w_in")[0][0])
    for l in range(nl):
        xs, s, gathered, p = layer_fwd(xs, p, (shards_a, shards_b, l + 1) if l + 1 < nl else None,
                                       (shards_b, 0, lambda got: params_b(*got)) if l == 0 else None)
        saved.append(s)
        params.append(p)
        if l + 1 < nl:
            p = dict(params_a(l + 1, gathered[0]), **params_b(*gathered[1:]))
    loss_part, g = loss_grad(xs, loss_target[0], tm)
    loss = lax.psum(loss_part[0, 0], ("x", "y", "c"))

    big = {"w_in": (w_in, m_w_in, v_w_in), "w_out": (w_out, m_w_out, v_w_out),
           "pool_mix_w": tuple(a.reshape(nl, ng * gsh, gw) for a in (pool_mix_w, m_pool_mix_w, v_pool_mix_w))}
    names = list(big)
    big_out = {k: None for k in big}
    small_g = [None] * nl

    def apply(layer, own, got_sib, got_chips):
        for k, o, gs_, gc in zip(names, own, got_sib, got_chips):
            wk, mk, vk = big[k]
            big_out[k] = reduce_adam(o, gs_, gc, wk, mk, vk, big_out[k], layer, min(256, wk.shape[1]), "reduce_adam_" + k,
                                     shifted=(k == "w_in"))

    def split_in(dw_main, dw_dt):
        own, send = grad_blocks(dw_main, dw_dt, cols, min(128, d))
        return [own], [send]

    def split_rest(dw_out, d_mixw):
        halves = [lambda ci: lax.dynamic_index_in_dim(dw_out.reshape(4, 2, -1, d), ci, 1, keepdims=False),
                  lambda ci: lax.dynamic_index_in_dim(
                      d_mixw.reshape(ng, 4, 2, gsh, gw), ci, 2, keepdims=False).transpose(1, 0, 2, 3).reshape(4, ng * gsh, gw)]
        return [h(cc) for h in halves], [h(1 - cc).astype(BF16) for h in halves]

    pending = None
    for l in reversed(range(nl)):
        g, gr, mine, done = layer_bwd(g, saved[l], params[l], split_in, split_rest, pending, last=(l == 0))
        if pending is not None:
            apply(l + 1, pending[0], *done[0])
        if l == 0:
            apply(0, mine[0], *done[1])
        pending = mine
        small_g[l] = dict(pre_norm_w=gr["pre_w"], pool_scale=gr["pscale"], conv_b=gr["conv_b"], dt_bias=gr["dt_bias"][:, :n_heads],
                          a_log=gr["a_log"][:, :n_heads], d_skip=gr["d_skip"][:, :n_heads], _pad=jnp.zeros((1, LANES - 3 * n_heads), F32),
                          ssd_norm_w=gr["norm_w"], post_norm_w=gr["post_w"], conv_w=gr["conv_w"].reshape(1, CONV_WIDTH * cd))

    sizes = {k: small_g[0][k].shape[1] for k in SMALL}
    gsum = sum_devices(all_gather_vmem(_pack({k: jnp.concatenate([sg[k] for sg in small_g], axis=0) for k in SMALL}),
                                       "gather_small_grads"))
    gs = _unpack(gsum, sizes, nl)
    csh = conv_w.shape[2]
    gs["conv_w"] = lax.dynamic_slice_in_dim(gs["conv_w"].reshape(nl, CONV_WIDTH, cd), me * csh, csh, axis=2).reshape(nl, -1)
    lsizes = dict(sizes, conv_w=CONV_WIDTH * csh)
    zpad = jnp.zeros((nl, sizes["_pad"]), F32)

    def local(pre, scale, cb, dtb, al, dsk, nw, post, cw):
        return _pack(dict(pre_norm_w=pre, pool_scale=scale, conv_b=cb, dt_bias=dtb, a_log=al, d_skip=dsk, _pad=zpad,
                          ssd_norm_w=nw, post_norm_w=post, conv_w=cw.reshape(nl, -1)))

    wp = local(pre_norm_w, pool_scale, conv_b, dt_bias, a_log, d_skip, ssd_norm_w, post_norm_w, conv_w)
    mp = local(m_pre_norm_w, m_pool_scale, m_conv_b, m_dt_bias, m_a_log, m_d_skip, m_ssd_norm_w, m_post_norm_w, m_conv_w)
    vp = local(v_pre_norm_w, v_pool_scale, v_conv_b, v_dt_bias, v_a_log, v_d_skip, v_ssd_norm_w, v_post_norm_w, v_conv_w)
    small_out = [gs] + [_unpack(o, lsizes, nl) for o in adam_small(wp, _pack(gs), mp, vp)]

    def leaf(kind, name):
        if name in big:
            return big_out[name][kind].reshape(big[name][0].shape if name != "pool_mix_w" else pool_mix_w.shape)
        val = small_out[kind][name]
        return val.reshape(conv_w.shape) if name == "conv_w" else val

    order = ("pre_norm_w", "w_in", "pool_mix_w", "pool_scale", "conv_w", "conv_b", "dt_bias", "a_log", "d_skip",
             "ssd_norm_w", "w_out", "post_norm_w")
    return (loss, g[None]) + tuple(leaf(kind, name) for kind in range(4) for name in order)
```

```python
import jax
import jax.numpy as jnp
from jax import lax
from jax.experimental import pallas as pl
from jax.experimental.pallas import tpu as pltpu

F32 = jnp.float32
BF16 = jnp.bfloat16
SDS = jax.ShapeDtypeStruct
MESH = pl.DeviceIdType.MESH
HIGHEST = lax.Precision.HIGHEST

NORM_EPS = 1e-6
POOL_WINDOWS = (2, 4, 8, 16)
POOL_HALO = 16
CONV_WIDTH = 4
CONV_HALO = 8
SSD_CHUNK = 128
SSD_HEAD_DIM = 64
SSD_STATE = 128
SSD_GROUPS = 4
LANES = 128
N_DEV = 8

ADAM_LR = 0.001
ADAM_B1 = 0.9
ADAM_B2 = 0.999
ADAM_EPS = 1e-08
ADAM_WD = 0.01
ADAM_STEP = 10

VMEM_LIMIT = 56 * 1024 * 1024

NT = (((1,), (1,)), ((), ()))
TN = (((0,), (0,)), ((), ()))


def _params(sem=None):
    kw = dict(vmem_limit_bytes=VMEM_LIMIT)
    if sem is not None:
        kw["dimension_semantics"] = sem
    return pltpu.CompilerParams(**kw)


def _silu(v):
    return v * jax.nn.sigmoid(v)


def _dsilu(v):
    s = jax.nn.sigmoid(v)
    return s * (1.0 + v * (1.0 - s))


def _split_dot(v, sel):
    hi = v.astype(BF16)
    lo = (v - hi.astype(F32)).astype(BF16)
    return (jnp.dot(hi, sel, preferred_element_type=F32) + jnp.dot(lo, sel, preferred_element_type=F32))


def _head_selector(width, per):
    ch = lax.broadcasted_iota(jnp.int32, (width, LANES), 0)
    hd = lax.broadcasted_iota(jnp.int32, (width, LANES), 1)
    return jnp.where((ch >= hd * per) & (ch < (hd + 1) * per), 1.0, 0.0).astype(BF16)


class Comm:
    def __init__(self, inputs, out_shapes, aliases, n_sems, make):
        self.inputs, self.out_shapes, self.aliases, self.n_sems, self.make = list(inputs), list(out_shapes), dict(aliases), n_sems, make


def _remote(src, dst, send_sems, recv_sems, k, peer):
    return pltpu.make_async_remote_copy(src_ref=src, dst_ref=dst, send_sem=send_sems.at[k], recv_sem=recv_sems.at[k],
                                        device_id=peer, device_id_type=MESH)


class _SemRange:
    def __init__(self, sems, start):
        self.sems, self.start = sems, start

    @property
    def at(self):
        return self

    def __getitem__(self, k):
        return self.sems.at[self.start + k]


def merge_comms(comms):
    comms = [c for c in comms if c is not None]
    if len(comms) <= 1:
        return comms[0] if comms else None
    aliases, i_off, o_off = {}, 0, 0
    for c in comms:
        aliases.update({i_off + k: o_off + v for k, v in c.aliases.items()})
        i_off, o_off = i_off + len(c.inputs), o_off + len(c.out_shapes)

    def make(ins, outs, ss, rs):
        sends, locals_, arrivals, i0, o0, s0 = [], [], [], 0, 0, 0
        for c in comms:
            s, l, a = c.make(ins[i0:i0 + len(c.inputs)], outs[o0:o0 + len(c.out_shapes)], _SemRange(ss, s0), _SemRange(rs, s0))
            sends, locals_, arrivals = sends + s, locals_ + l, arrivals + a
            i0, o0, s0 = i0 + len(c.inputs), o0 + len(c.out_shapes), s0 + c.n_sems
        return sends, locals_, arrivals

    return Comm(sum((c.inputs for c in comms), []), sum((c.out_shapes for c in comms), []), aliases,
                sum(c.n_sems for c in comms), make)


def _call(body, args, *, name, grid, in_specs, out_specs, out_shape, scratch_shapes=(), sem=None, comm=None):
    in_specs, out_specs, out_shape = list(in_specs), list(out_specs), list(out_shape)
    if comm is None:
        outs = pl.pallas_call(body, name=name, grid=grid, in_specs=in_specs, out_specs=out_specs, out_shape=out_shape,
                              scratch_shapes=list(scratch_shapes), compiler_params=_params(sem))(*args)
        return list(outs), []
    ni, no, nci, nco, ns = len(in_specs), len(out_specs), len(comm.inputs), len(comm.out_shapes), len(scratch_shapes)
    hbm = pl.BlockSpec(memory_space=pl.ANY)

    def hosted(*refs):
        ins, cins = refs[:ni], refs[ni:ni + nci]
        outs, couts = refs[ni + nci:ni + nci + no], refs[ni + nci + no:ni + nci + no + nco]
        scratch = refs[ni + nci + no + nco:]
        sends, locals_, arrivals = comm.make(cins, couts, scratch[ns], scratch[ns + 1])
        first = last = None if grid else True
        for axis, extent in enumerate(grid):
            pid = pl.program_id(axis)
            first = (pid == 0) if first is None else first & (pid == 0)
            last = (pid == extent - 1) if last is None else last & (pid == extent - 1)

        @pl.when(first)
        def _():
            for cp in locals_ + sends:
                cp.start()

        body(*ins, *outs, *scratch[:ns])

        @pl.when(last)
        def _():
            for cp in arrivals:
                cp.wait_recv()
            for cp in sends:
                cp.wait_send()
            for cp in locals_:
                cp.wait()

    outs = pl.pallas_call(
        hosted, name=name, grid=grid, in_specs=in_specs + [hbm] * nci, out_specs=out_specs + [hbm] * nco,
        out_shape=out_shape + comm.out_shapes,
        scratch_shapes=list(scratch_shapes) + [pltpu.SemaphoreType.DMA((comm.n_sems,)), pltpu.SemaphoreType.DMA((comm.n_sems,))],
        input_output_aliases={ni + k: no + v for k, v in comm.aliases.items()},
        compiler_params=_params(sem),
    )(*args, *comm.inputs)
    return list(outs[:no]), list(outs[no:])


def rms_fwd(x, w, tm):
    t, d = x.shape

    def body(x_ref, w_ref, h_ref, r_ref):
        xv = x_ref[...]
        r = lax.rsqrt(jnp.mean(xv * xv, axis=-1, keepdims=True) + NORM_EPS)
        h_ref[...] = (xv * r * w_ref[...]).astype(BF16)
        r_ref[...] = r

    return pl.pallas_call(
        body, name="rms_fwd", grid=(t // tm,),
        in_specs=[pl.BlockSpec((tm, d), lambda i: (i, 0)), pl.BlockSpec((1, d), lambda i: (0, 0))],
        out_specs=[pl.BlockSpec((tm, d), lambda i: (i, 0)), pl.BlockSpec((tm, 1), lambda i: (i, 0))],
        out_shape=[SDS((t, d), BF16), SDS((t, 1), F32)],
        compiler_params=_params(("arbitrary",)),
    )(x, w)


def post_fwd(out, x, w, tm, comm=None):
    t, d = x.shape

    def body(o_ref, x_ref, w_ref, y_ref, r_ref):
        ov = o_ref[...]
        r = lax.rsqrt(jnp.mean(ov * ov, axis=-1, keepdims=True) + NORM_EPS)
        y_ref[...] = x_ref[...] + ov * r * w_ref[...]
        r_ref[...] = r

    return _call(
        body, (out, x, w), name="post_fwd", grid=(t // tm,),
        in_specs=[pl.BlockSpec((tm, d), lambda i: (i, 0)), pl.BlockSpec((tm, d), lambda i: (i, 0)),
                  pl.BlockSpec((1, d), lambda i: (0, 0))],
        out_specs=[pl.BlockSpec((tm, d), lambda i: (i, 0)), pl.BlockSpec((tm, 1), lambda i: (i, 0))],
        out_shape=[SDS((t, d), F32), SDS((t, 1), F32)], sem=("arbitrary",), comm=comm)


def _norm_bwd(g_n, n, r):
    return r * (g_n - n * jnp.mean(g_n * n, axis=-1, keepdims=True))


def post_bwd(g, out, r, w, tm):
    t, d = g.shape

    def body(g_ref, o_ref, r_ref, w_ref, do_ref, dw_ref):
        i = pl.program_id(0)
        gv = g_ref[...]
        rv = r_ref[...]
        n = o_ref[...] * rv
        part = jnp.sum(gv * n, axis=0, keepdims=True)

        @pl.when(i == 0)
        def _():
            dw_ref[...] = part

        @pl.when(i > 0)
        def _():
            dw_ref[...] += part

        do_ref[...] = _norm_bwd(gv * w_ref[...], n, rv).astype(BF16)

    return pl.pallas_call(
        body, name="post_bwd", grid=(t // tm,),
        in_specs=[pl.BlockSpec((tm, d), lambda i: (i, 0)), pl.BlockSpec((tm, d), lambda i: (i, 0)),
                  pl.BlockSpec((tm, 1), lambda i: (i, 0)), pl.BlockSpec((1, d), lambda i: (0, 0))],
        out_specs=[pl.BlockSpec((tm, d), lambda i: (i, 0)), pl.BlockSpec((1, d), lambda i: (0, 0))],
        out_shape=[SDS((t, d), BF16), SDS((1, d), F32)],
        compiler_params=_params(("arbitrary",)),
    )(g, out, r, w)


def rms_bwd(dh, x, r, w, g, tm, comm=None):
    t, d = x.shape

    def body(a_ref, x_ref, r_ref, w_ref, g_ref, gx_ref, dw_ref):
        i = pl.program_id(0)
        dh = a_ref[...]
        rv = r_ref[...]
        n = x_ref[...] * rv
        part = jnp.sum(dh * n, axis=0, keepdims=True)

        @pl.when(i == 0)
        def _():
            dw_ref[...] = part

        @pl.when(i > 0)
        def _():
            dw_ref[...] += part

        gx_ref[...] = g_ref[...] + _norm_bwd(dh * w_ref[...], n, rv)

    row = pl.BlockSpec((tm, d), lambda i: (i, 0))
    return _call(
        body, (dh, x, r, w, g), name="rms_bwd", grid=(t // tm,),
        in_specs=[row, row, pl.BlockSpec((tm, 1), lambda i: (i, 0)), pl.BlockSpec((1, d), lambda i: (0, 0)), row],
        out_specs=[row, pl.BlockSpec((1, d), lambda i: (0, 0))],
        out_shape=[SDS((t, d), F32), SDS((1, d), F32)], sem=("arbitrary",), comm=comm)


def loss_grad(y, target, tm):
    t, d = y.shape

    def body(y_ref, t_ref, l_ref, g_ref):
        i = pl.program_id(0)
        err = y_ref[...] - t_ref[...]
        g_ref[...] = err / d
        part = 0.5 * jnp.sum(jnp.mean(err * err, axis=-1, keepdims=True), axis=0, keepdims=True)

        @pl.when(i == 0)
        def _():
            l_ref[...] = part

        @pl.when(i > 0)
        def _():
            l_ref[...] += part

    row = pl.BlockSpec((tm, d), lambda i: (i, 0))
    return pl.pallas_call(
        body, name="loss_grad", grid=(t // tm,), in_specs=[row, row],
        out_specs=[pl.BlockSpec((1, 1), lambda i: (0, 0)), row],
        out_shape=[SDS((1, 1), F32), SDS((t, d), F32)],
        compiler_params=_params(("arbitrary",)),
    )(y, target)


def mm_nn(a, b, out_dtype, tm, tn, name, comm=None):
    m, k = a.shape
    n = b.shape[1]

    def body(a_ref, b_ref, o_ref):
        o_ref[...] = jnp.dot(a_ref[...], b_ref[...], preferred_element_type=F32).astype(out_dtype)

    outs, couts = _call(
        body, (a, b), name=name, grid=(n // tn, m // tm),
        in_specs=[pl.BlockSpec((tm, k), lambda j, i: (i, 0)), pl.BlockSpec((k, tn), lambda j, i: (0, j))],
        out_specs=[pl.BlockSpec((tm, tn), lambda j, i: (i, j))],
        out_shape=[SDS((m, n), out_dtype)], sem=("arbitrary", "arbitrary"), comm=comm)
    return outs[0], couts


def mm_nt(a, b, out_dtype, tm, tn, tk, name, comm=None, extra=None):
    m, k = a.shape
    n = b.shape[0]
    nk = k // tk

    def body(a_ref, b_ref, *rest):
        o_ref, acc_ref = rest[-2:]
        kk = pl.program_id(2)
        part = lax.dot_general(a_ref[...], b_ref[...], NT, preferred_element_type=F32)
        if nk == 1:
            if extra is not None:
                part = part + lax.dot_general(rest[0][...], rest[1][...], NT, preferred_element_type=F32)
            o_ref[...] = part.astype(out_dtype)
        else:
            @pl.when(kk == 0)
            def _():
                if extra is None:
                    acc_ref[...] = part
                else:
                    acc_ref[...] = part + lax.dot_general(rest[0][...], rest[1][...], NT, preferred_element_type=F32)

            @pl.when(kk > 0)
            def _():
                acc_ref[...] += part

            @pl.when(kk == nk - 1)
            def _():
                o_ref[...] = acc_ref[...].astype(out_dtype)

    more_specs = [] if extra is None else [pl.BlockSpec((tm, extra[0].shape[1]), lambda i, j, kk: (i, 0)),
                                           pl.BlockSpec((tn, extra[1].shape[1]), lambda i, j, kk: (j, 0))]
    outs, couts = _call(
        body, (a, b) + tuple(extra or ()), name=name, grid=(m // tm, n // tn, nk),
        in_specs=[pl.BlockSpec((tm, tk), lambda i, j, kk: (i, kk)), pl.BlockSpec((tn, tk), lambda i, j, kk: (j, kk))] + more_specs,
        out_specs=[pl.BlockSpec((tm, tn), lambda i, j, kk: (i, j))],
        out_shape=[SDS((m, n), out_dtype)],
        scratch_shapes=[pltpu.VMEM((tm, tn) if nk > 1 else (8, LANES), F32)],
        sem=("arbitrary", "arbitrary", "arbitrary"), comm=comm)
    return outs[0], couts


def mm_tn(a, b, tm, tn, name, comm=None):
    t, m = a.shape
    n = b.shape[1]

    def body(a_ref, b_ref, o_ref):
        o_ref[...] = lax.dot_general(a_ref[...], b_ref[...], TN, preferred_element_type=F32)

    outs, couts = _call(
        body, (a, b), name=name, grid=(m // tm, n // tn),
        in_specs=[pl.BlockSpec((t, tm), lambda i, j: (0, i)), pl.BlockSpec((t, tn), lambda i, j: (0, j))],
        out_specs=[pl.BlockSpec((tm, tn), lambda i, j: (i, j))],
        out_shape=[SDS((m, n), F32)], sem=("arbitrary", "arbitrary"), comm=comm)
    return outs[0], couts


def _window_sums(ext, n_rows, lookahead):
    def sh(v, k):
        return pltpu.roll(v, (n_rows - k) if lookahead else k, 0)
    s2 = ext + sh(ext, 1)
    s4 = s2 + sh(s2, 2)
    s8 = s4 + sh(s4, 4)
    s16 = s8 + sh(s8, 8)
    return (s2, s4, s8, s16)


def _pool_counts(i, tm, w):
    tpos = i * tm + lax.broadcasted_iota(jnp.int32, (tm, 1), 0)
    return jnp.minimum(tpos + 1, w).astype(F32)


def _pooled(uc_ref, up_ref, i, tm):
    cur = uc_ref[...]
    prev = jnp.where(i > 0, up_ref[...], 0.0)
    ext = jnp.concatenate([prev, cur], axis=0)
    return cur, _window_sums(ext, tm + POOL_HALO, False)


def pool_fwd(proj, mixw, scale, tm):
    t = proj.shape[0]
    pw = scale.shape[1]
    gw = pw // len(POOL_WINDOWS)
    nh = tm // POOL_HALO

    def body(uc_ref, up_ref, g_ref, w_ref, s_ref, o_ref):
        i = pl.program_id(0)
        cur, sums = _pooled(uc_ref, up_ref, i, tm)
        for g, w in enumerate(POOL_WINDOWS):
            cols = slice(g * gw, (g + 1) * gw)
            pooled = sums[g][POOL_HALO:, cols] / _pool_counts(i, tm, w) - cur[:, cols]
            mixed = jnp.dot(pooled.astype(BF16), w_ref[g], preferred_element_type=F32)
            o_ref[:, cols] = (mixed * s_ref[:, cols] * _silu(g_ref[:, cols])).astype(BF16)

    return pl.pallas_call(
        body, name="pool_fwd", grid=(t // tm,),
        in_specs=[pl.BlockSpec((tm, pw), lambda i: (i, 0)),
                  pl.BlockSpec((POOL_HALO, pw), lambda i: (jnp.maximum(i * nh - 1, 0), 0)),
                  pl.BlockSpec((tm, pw), lambda i: (i, 1)),
                  pl.BlockSpec(mixw.shape, lambda i: (0, 0, 0)),
                  pl.BlockSpec((1, pw), lambda i: (0, 0))],
        out_specs=pl.BlockSpec((tm, pw), lambda i: (i, 0)),
        out_shape=SDS((t, 2 * pw), BF16),
        compiler_params=_params(("arbitrary",)),
    )(proj, proj, proj, mixw, scale)


def pool_bwd_a(dmixed, proj, mixw, scale, tm):
    t, e = proj.shape
    pw = scale.shape[1]
    ng = len(POOL_WINDOWS)
    gw = pw // ng
    nh = tm // POOL_HALO

    def body(dy_ref, uc_ref, up_ref, g_ref, w_ref, s_ref, dg_ref, dq_ref, ds_ref, dw_ref):
        i = pl.program_id(0)

        @pl.when(i == 0)
        def _():
            ds_ref[...] = jnp.zeros_like(ds_ref)
            dw_ref[...] = jnp.zeros_like(dw_ref)

        cur, sums = _pooled(uc_ref, up_ref, i, tm)
        for g, w in enumerate(POOL_WINDOWS):
            cols = slice(g * gw, (g + 1) * gw)
            cnt = _pool_counts(i, tm, w)
            pooled = (sums[g][POOL_HALO:, cols] / cnt - cur[:, cols]).astype(BF16)
            mixed = jnp.dot(pooled, w_ref[g], preferred_element_type=F32)
            gate = g_ref[:, cols]
            dy = dy_ref[:, cols]
            sc = s_ref[:, cols]
            dg_ref[:, cols] = (dy * mixed * sc * _dsilu(gate)).astype(BF16)
            ds = dy * _silu(gate)
            ds_ref[:, cols] += jnp.sum(ds * mixed, axis=0, keepdims=True)
            dmix = (ds * sc).astype(BF16)
            dw_ref[g] += lax.dot_general(pooled, dmix, TN, preferred_element_type=F32)
            dq_ref[:, cols] = lax.dot_general(dmix, w_ref[g], NT, preferred_element_type=F32) / cnt

    return pl.pallas_call(
        body, name="pool_bwd_a", grid=(t // tm,),
        in_specs=[pl.BlockSpec((tm, pw), lambda i: (i, 0)),
                  pl.BlockSpec((tm, pw), lambda i: (i, 0)),
                  pl.BlockSpec((POOL_HALO, pw), lambda i: (jnp.maximum(i * nh - 1, 0), 0)),
                  pl.BlockSpec((tm, pw), lambda i: (i, 1)),
                  pl.BlockSpec(mixw.shape, lambda i: (0, 0, 0)),
                  pl.BlockSpec((1, pw), lambda i: (0, 0))],
        out_specs=[pl.BlockSpec((tm, pw), lambda i: (i, 1)),
                   pl.BlockSpec((tm, pw), lambda i: (i, 0)),
                   pl.BlockSpec((1, pw), lambda i: (0, 0)),
                   pl.BlockSpec((ng, gw, gw), lambda i: (0, 0, 0))],
        out_shape=[SDS((t, e), BF16), SDS((t, pw), F32), SDS((1, pw), F32), SDS((ng, gw, gw), F32)],
        compiler_params=_params(("arbitrary",)),
    )(dmixed, proj, proj, proj, mixw, scale)


def pool_bwd_b(dq, dproj, tm):
    t, pw = dq.shape
    gw = pw // len(POOL_WINDOWS)
    nh = tm // POOL_HALO
    nt = t // tm

    def body(c_ref, n_ref, alias_ref, o_ref):
        i = pl.program_id(0)
        cur = c_ref[...]
        nxt = jnp.where(i < nt - 1, n_ref[...], 0.0)
        sums = _window_sums(jnp.concatenate([cur, nxt], axis=0), tm + POOL_HALO, True)
        for g, w in enumerate(POOL_WINDOWS):
            cols = slice(g * gw, (g + 1) * gw)
            o_ref[:, cols] = (sums[g][:tm, cols] - cur[:, cols] * _pool_counts(i, tm, w)).astype(BF16)

    return pl.pallas_call(
        body, name="pool_bwd_b", grid=(nt,),
        in_specs=[pl.BlockSpec((tm, pw), lambda i: (i, 0)),
                  pl.BlockSpec((POOL_HALO, pw), lambda i: (jnp.minimum((i + 1) * nh, t // POOL_HALO - 1), 0)),
                  pl.BlockSpec(memory_space=pl.ANY)],
        out_specs=pl.BlockSpec((tm, pw), lambda i: (i, 0)),
        out_shape=SDS(dproj.shape, dproj.dtype),
        input_output_aliases={2: 0},
        compiler_params=_params(("arbitrary",)),
    )(dq, dq, dproj)


ELEMENTWISE_LANE_CHUNK = 256


def _lane_chunks(width):
    return [slice(c, c + ELEMENTWISE_LANE_CHUNK) for c in range(0, width, ELEMENTWISE_LANE_CHUNK)]


def _conv_pre(xc_ref, xp_ref, w_ref, b_ref, i, cols):
    cur = xc_ref[:, cols]
    prev = jnp.where(i > 0, xp_ref[:, cols], 0.0)
    ext = jnp.concatenate([prev, cur], axis=0)
    taps = [pltpu.roll(ext, CONV_WIDTH - 1 - k, 0)[CONV_HALO:] for k in range(CONV_WIDTH - 1)] + [cur]
    pre = b_ref[:, cols]
    for k in range(CONV_WIDTH):
        pre = pre + w_ref[k:k + 1, cols] * taps[k]
    return pre, taps


def conv_fwd(proj, conv_w, conv_b, col_block, tm, comm=None):
    t = proj.shape[0]
    cd = conv_b.shape[1]
    nh = tm // CONV_HALO

    def body(xc_ref, xp_ref, w_ref, b_ref, o_ref):
        i = pl.program_id(0)
        for cols in _lane_chunks(cd):
            pre, _ = _conv_pre(xc_ref, xp_ref, w_ref, b_ref, i, cols)
            o_ref[:, cols] = _silu(pre)

    outs, couts = _call(
        body, (proj, proj, conv_w, conv_b), name="conv_fwd", grid=(t // tm,),
        in_specs=[pl.BlockSpec((tm, cd), lambda i: (i, col_block)),
                  pl.BlockSpec((CONV_HALO, cd), lambda i: (jnp.maximum(i * nh - 1, 0), col_block)),
                  pl.BlockSpec((CONV_WIDTH, cd), lambda i: (0, 0)),
                  pl.BlockSpec((1, cd), lambda i: (0, 0))],
        out_specs=[pl.BlockSpec((tm, cd), lambda i: (i, 0))],
        out_shape=[SDS((t, cd), F32)], sem=("arbitrary",), comm=comm)
    return outs[0], couts


def conv_bwd_a(dxs, db, dc, proj, conv_w, conv_b, col_block, tm, comm=None):
    t = proj.shape[0]
    cd = conv_b.shape[1]
    sw = dxs.shape[1]
    gn = db.shape[1]
    nh = tm // CONV_HALO

    def body(dx_ref, db_ref, dc_ref, xc_ref, xp_ref, w_ref, b_ref, dp_ref, dw_ref, dbias_ref):
        i = pl.program_id(0)

        @pl.when(i == 0)
        def _():
            dw_ref[...] = jnp.zeros_like(dw_ref)
            dbias_ref[...] = jnp.zeros_like(dbias_ref)

        for cols in _lane_chunks(cd):
            pre, taps = _conv_pre(xc_ref, xp_ref, w_ref, b_ref, i, cols)
            if cols.start < sw:
                dact = dx_ref[:, cols]
            elif cols.start < sw + gn:
                dact = db_ref[:, cols.start - sw:cols.stop - sw]
            else:
                dact = dc_ref[:, cols.start - sw - gn:cols.stop - sw - gn]
            dpre = dact * _dsilu(pre)
            dp_ref[:, cols] = dpre
            dbias_ref[:, cols] += jnp.sum(dpre, axis=0, keepdims=True)
            for k in range(CONV_WIDTH):
                dw_ref[k:k + 1, cols] += jnp.sum(dpre * taps[k], axis=0, keepdims=True)

    return _call(
        body, (dxs, db, dc, proj, proj, conv_w, conv_b), name="conv_bwd_a", grid=(t // tm,),
        in_specs=[pl.BlockSpec((tm, sw), lambda i: (i, 0)), pl.BlockSpec((tm, gn), lambda i: (i, 0)),
                  pl.BlockSpec((tm, gn), lambda i: (i, 0)),
                  pl.BlockSpec((tm, cd), lambda i: (i, col_block)),
                  pl.BlockSpec((CONV_HALO, cd), lambda i: (jnp.maximum(i * nh - 1, 0), col_block)),
                  pl.BlockSpec((CONV_WIDTH, cd), lambda i: (0, 0)),
                  pl.BlockSpec((1, cd), lambda i: (0, 0))],
        out_specs=[pl.BlockSpec((tm, cd), lambda i: (i, 0)),
                   pl.BlockSpec((CONV_WIDTH, cd), lambda i: (0, 0)),
                   pl.BlockSpec((1, cd), lambda i: (0, 0))],
        out_shape=[SDS((t, cd), F32), SDS((CONV_WIDTH, cd), F32), SDS((1, cd), F32)],
        sem=("arbitrary",), comm=comm)


def conv_bwd_b(dpre, conv_w, dproj, col_block, tm):
    t, cd = dpre.shape
    nh = tm // CONV_HALO
    nt = t // tm

    def body(c_ref, n_ref, w_ref, alias_ref, o_ref):
        i = pl.program_id(0)
        n = tm + CONV_HALO
        for cols in _lane_chunks(cd):
            cur = c_ref[:, cols]
            nxt = jnp.where(i < nt - 1, n_ref[:, cols], 0.0)
            ext = jnp.concatenate([cur, nxt], axis=0)
            acc = w_ref[CONV_WIDTH - 1:CONV_WIDTH, cols] * cur
            for k in range(CONV_WIDTH - 1):
                acc = acc + w_ref[k:k + 1, cols] * pltpu.roll(ext, n - (CONV_WIDTH - 1 - k), 0)[:tm]
            o_ref[:, cols] = acc.astype(BF16)

    return pl.pallas_call(
        body, name="conv_bwd_b", grid=(nt,),
        in_specs=[pl.BlockSpec((tm, cd), lambda i: (i, 0)),
                  pl.BlockSpec((CONV_HALO, cd), lambda i: (jnp.minimum((i + 1) * nh, t // CONV_HALO - 1), 0)),
                  pl.BlockSpec((CONV_WIDTH, cd), lambda i: (0, 0)),
                  pl.BlockSpec(memory_space=pl.ANY)],
        out_specs=pl.BlockSpec((tm, cd), lambda i: (i, col_block)),
        out_shape=SDS(dproj.shape, dproj.dtype),
        input_output_aliases={3: 0},
        compiler_params=_params(("arbitrary",)),
    )(dpre, dpre, conv_w, dproj)


def _softplus(v):
    return jnp.maximum(v, 0.0) + jnp.log(1.0 + jnp.exp(-jnp.abs(v)))


def _ssd_chunk_terms(dtr_ref, bias_ref, a_ref, n_heads):
    q = SSD_CHUNK
    lane = lax.broadcasted_iota(jnp.int32, (1, LANES), 1)
    pre = dtr_ref[...] + bias_ref[...]
    dt = jnp.where(lane < n_heads, _softplus(pre), 0.0)
    a = jnp.where(lane < n_heads, -jnp.exp(a_ref[...]), 0.0)
    row = lax.broadcasted_iota(jnp.int32, (q, q), 0)
    col = lax.broadcasted_iota(jnp.int32, (q, q), 1)
    causal = row >= col
    acs = jnp.dot(causal.astype(F32), dt * a, precision=HIGHEST, preferred_element_type=F32)
    last = acs[q - 1:q, :]
    return dict(pre=pre, dt=dt, a=a, acs=acs, acs_t=acs.T, eacs=jnp.exp(acs), dstate=jnp.exp(last - acs),
                cdec=jnp.exp(last), causal=causal, diag=row == col, lane=lane)


_TERM_FIELDS = ("pre", "dt", "acs", "acs_t", "eacs", "dstate", "cdec")


def _prefetched_terms(step, dtr_ref, dtn_ref, bias_ref, a_ref, n_heads, terms_ref):
    q = SSD_CHUNK

    def store(slot, tm_):
        for f, name in enumerate(_TERM_FIELDS):
            terms_ref[slot, f] = jnp.broadcast_to(tm_[name], (q, LANES))

    @pl.when(step == 0)
    def _():
        store(0, _ssd_chunk_terms(dtr_ref, bias_ref, a_ref, n_heads))

    slot = lax.rem(step, 2)
    nxt = _ssd_chunk_terms(dtn_ref, bias_ref, a_ref, n_heads)
    tm_ = dict(nxt, **{name: terms_ref[slot, f] for f, name in enumerate(_TERM_FIELDS)})
    tm_["cdec"] = tm_["cdec"][0:1]
    return tm_, lambda: store(1 - slot, nxt)


def _pair_cols(lo, v, h):
    if v.shape[0] < 8:
        return jnp.where(lo, v[:, h:h + 1], v[:, h + 1:h + 2])
    idx = jnp.broadcast_to(jnp.where(lo, h, h + 1).astype(jnp.int32), v.shape)
    return jnp.take_along_axis(v, idx, axis=1, mode="promise_in_bounds")


def _pair_decay(tm_, cb, h):
    l0 = jnp.exp(jnp.where(tm_["causal"], tm_["acs"][:, h:h + 1] - tm_["acs_t"][h:h + 1, :], -jnp.inf))
    l1 = jnp.exp(jnp.where(tm_["causal"], tm_["acs"][:, h + 1:h + 2] - tm_["acs_t"][h + 1:h + 2, :], -jnp.inf))
    return l0, l1, jnp.concatenate([cb * l0, cb * l1], axis=1)


def _pair_decay_t(tm_, cbt, h):
    upper = jnp.logical_not(tm_["causal"]) | tm_["diag"]
    t0 = jnp.exp(jnp.where(upper, tm_["acs_t"][h:h + 1, :] - tm_["acs"][:, h:h + 1], -jnp.inf))
    t1 = jnp.exp(jnp.where(upper, tm_["acs_t"][h + 1:h + 2, :] - tm_["acs"][:, h + 1:h + 2], -jnp.inf))
    return jnp.concatenate([cbt * t0, cbt * t1], axis=0).astype(BF16)


def _block_diag(lo, xdt):
    return jnp.concatenate([jnp.where(lo, xdt, 0.0), jnp.where(lo, 0.0, xdt)], axis=0).astype(BF16)


def ssd_fwd(xbc, dt_raw, dt_bias, a_log, n_heads, comm=None):
    t = xbc.shape[0]
    q = SSD_CHUNK
    gn = SSD_GROUPS * SSD_STATE
    sw = n_heads * SSD_HEAD_DIM
    n_pairs = n_heads // 2
    pairs_per_group = n_pairs // SSD_GROUPS
    nc = t // q
    bblk = sw // gn

    def body(xs_ref, b_ref, c_ref, dtr_ref, dtn_ref, bias_ref, a_ref, y_ref, sin_ref, state, terms_ref):
        @pl.when(pl.program_id(0) == 0)
        def _():
            state[...] = jnp.zeros_like(state)

        tm_, keep_next = _prefetched_terms(pl.program_id(0), dtr_ref, dtn_ref, bias_ref, a_ref, n_heads, terms_ref)
        lo = tm_["lane"] < SSD_HEAD_DIM
        for g in range(SSD_GROUPS):
            gcols = slice(g * SSD_STATE, (g + 1) * SSD_STATE)
            bg = b_ref[:, gcols].astype(BF16)
            bg_t = b_ref[:, gcols].T.astype(BF16)
            cg = c_ref[:, gcols].astype(BF16)
            cb = lax.dot_general(cg, bg, NT, preferred_element_type=F32)
            for j in range(pairs_per_group):
                p = g * pairs_per_group + j
                h = 2 * p
                pcols = slice(p * LANES, (p + 1) * LANES)
                _, _, mcat = _pair_decay(tm_, cb, h)
                xdt = xs_ref[:, pcols] * _pair_cols(lo, tm_["dt"], h)
                ydiag = jnp.dot(mcat.astype(BF16), _block_diag(lo, xdt), preferred_element_type=F32)
                st = state[p]
                sin_ref[0, p] = st
                yoff = jnp.dot(cg, st.astype(BF16), preferred_element_type=F32) * _pair_cols(lo, tm_["eacs"], h)
                y_ref[:, pcols] = ydiag + yoff
                xw = (xdt * _pair_cols(lo, tm_["dstate"], h)).astype(BF16)
                state[p] = st * _pair_cols(lo, tm_["cdec"], h) + jnp.dot(bg_t, xw, preferred_element_type=F32)
        keep_next()

    vec = pl.BlockSpec((1, LANES), lambda c: (0, 0))
    return _call(
        body, (xbc, xbc, xbc, dt_raw, dt_raw, dt_bias, a_log), name="ssd_fwd", grid=(nc,),
        in_specs=[pl.BlockSpec((q, sw), lambda c: (c, 0)),
                  pl.BlockSpec((q, gn), lambda c: (c, bblk)),
                  pl.BlockSpec((q, gn), lambda c: (c, bblk + 1)),
                  pl.BlockSpec((q, LANES), lambda c: (c, 0)),
                  pl.BlockSpec((q, LANES), lambda c: (jnp.minimum(c + 1, nc - 1), 0)), vec, vec],
        out_specs=[pl.BlockSpec((q, sw), lambda c: (c, 0)),
                   pl.BlockSpec((1, n_pairs, SSD_STATE, LANES), lambda c: (c, 0, 0, 0))],
        out_shape=[SDS((t, sw), F32), SDS((nc, n_pairs, SSD_STATE, LANES), F32)],
        scratch_shapes=[pltpu.VMEM((n_pairs, SSD_STATE, LANES), F32), pltpu.VMEM((2, len(_TERM_FIELDS), q, LANES), F32)],
        sem=("arbitrary",), comm=comm)


def ssd_bwd(dy, xbc, dt_raw, dt_bias, a_log, d_full, s_in, n_heads, comm=None):
    t = xbc.shape[0]
    q = SSD_CHUNK
    gn = SSD_GROUPS * SSD_STATE
    sw = n_heads * SSD_HEAD_DIM
    n_pairs = n_heads // 2
    pairs_per_group = n_pairs // SSD_GROUPS
    nc = t // q
    bblk = sw // gn

    def body(dy_ref, xs_ref, b_ref, c_ref, dtr_ref, dtn_ref, bias_ref, a_ref, dsk_ref, sin_ref,
             dxs_ref, db_ref, dc_ref, ddtr_ref, dbias_ref, dalog_ref,
             dstate, tbuf, xbuf, rbuf, acc_a, acc_b, sel_ref, terms_ref):
        i = pl.program_id(0)

        @pl.when(i == 0)
        def _():
            dstate[...] = jnp.zeros_like(dstate)
            rbuf[...] = jnp.zeros_like(rbuf)
            acc_a[...] = jnp.zeros_like(acc_a)
            acc_b[...] = jnp.zeros_like(acc_b)
            sel_ref[...] = _head_selector(sw, SSD_HEAD_DIM)

        tm_, keep_next = _prefetched_terms(i, dtr_ref, dtn_ref, bias_ref, a_ref, n_heads, terms_ref)
        lane = tm_["lane"]
        lo = lane < SSD_HEAD_DIM
        head_row = lax.broadcasted_iota(jnp.int32, (LANES, 1), 0)
        rows = jnp.zeros((q, LANES), F32)
        cols_t = jnp.zeros((LANES, q), F32)
        for g in range(SSD_GROUPS):
            gcols = slice(g * SSD_STATE, (g + 1) * SSD_STATE)
            bg = b_ref[:, gcols].astype(BF16)
            cg = c_ref[:, gcols].astype(BF16)
            cg_t = c_ref[:, gcols].T.astype(BF16)
            cb = lax.dot_general(cg, bg, NT, preferred_element_type=F32)
            cbt = lax.dot_general(bg, cg, NT, preferred_element_type=F32)
            dcb = jnp.zeros((q, q), F32)
            db_acc = jnp.zeros((q, SSD_STATE), F32)
            dc_acc = jnp.zeros((q, SSD_STATE), F32)
            for j in range(pairs_per_group):
                p = g * pairs_per_group + j
                h = 2 * p
                pcols = slice(p * LANES, (p + 1) * LANES)
                l0, l1, mcat = _pair_decay(tm_, cb, h)
                xp = xs_ref[:, pcols]
                dtp = _pair_cols(lo, tm_["dt"], h)
                xdt = xp * dtp
                xbd = _block_diag(lo, xdt)
                dyp = dy_ref[:, pcols]
                dyb = dyp.astype(BF16)
                dsb = _pair_cols(lo, tm_["dstate"], h)
                cdr = _pair_cols(lo, tm_["cdec"], h)
                eb = _pair_cols(lo, tm_["eacs"], h)
                st = sin_ref[0, p]
                stb = st.astype(BF16)
                dst = dstate[p]
                dstb = dst.astype(BF16)
                dye = (dyp * eb).astype(BF16)
                both = jnp.dot(_pair_decay_t(tm_, cbt, h), dyb, preferred_element_type=F32)
                dx_state = jnp.dot(bg, dstb, preferred_element_type=F32) * dsb
                dxdt = jnp.where(lo, both[:q], both[q:]) + dx_state
                dmcat = lax.dot_general(dyb, xbd, NT, preferred_element_type=F32)
                dcb = dcb + dmcat[:, :q] * l0 + dmcat[:, q:] * l1
                dseg = dmcat * mcat
                csum = jnp.sum(dseg, axis=0, keepdims=True)
                rows = (rows + jnp.where(lane == h, jnp.sum(dseg[:, :q], axis=1, keepdims=True), 0.0)
                        + jnp.where(lane == h + 1, jnp.sum(dseg[:, q:], axis=1, keepdims=True), 0.0))
                cols_t = (cols_t + jnp.where(head_row == h, csum[:, :q], 0.0)
                          + jnp.where(head_row == h + 1, csum[:, q:], 0.0))
                dc_acc = dc_acc + lax.dot_general(dye, stb, NT, preferred_element_type=F32)
                db_acc = db_acc + lax.dot_general((xdt * dsb).astype(BF16), dstb, NT, preferred_element_type=F32)
                yoff = jnp.dot(cg, stb, preferred_element_type=F32) * eb
                tbuf[:, pcols] = dyp * yoff - xdt * dx_state
                xbuf[:, pcols] = dxdt * xp
                rbuf[0:1, pcols] = (jnp.sum(xdt * dx_state, axis=0, keepdims=True)
                                    + cdr * jnp.sum(dst * st, axis=0, keepdims=True))
                dxs_ref[:, pcols] = dxdt * dtp + dyp * dsk_ref[:, pcols]
                dstate[p] = dst * cdr + jnp.dot(cg_t, dye, preferred_element_type=F32)
            dcbb = dcb.astype(BF16)
            dc_ref[:, gcols] = dc_acc + jnp.dot(dcbb, bg, preferred_element_type=F32)
            db_ref[:, gcols] = db_acc + lax.dot_general(dcbb, cg, TN, preferred_element_type=F32)

        sel = sel_ref[...]
        dacs = rows - cols_t.T + _split_dot(tbuf[...], sel)
        carry = _split_dot(rbuf[...], sel)[0:1]
        anti = jnp.logical_not(tm_["causal"]) | tm_["diag"]
        da = jnp.dot(anti.astype(F32), dacs, precision=HIGHEST, preferred_element_type=F32) + carry
        ddt = da * tm_["a"] + _split_dot(xbuf[...], sel)
        ddtr = jnp.where(tm_["lane"] < n_heads, ddt * jax.nn.sigmoid(tm_["pre"]), 0.0)
        ddtr_ref[...] = ddtr.astype(BF16)
        acc_b[...] += jnp.sum(ddtr, axis=0, keepdims=True)
        acc_a[...] += jnp.sum(da * tm_["dt"], axis=0, keepdims=True)
        keep_next()

        @pl.when(i == nc - 1)
        def _():
            dbias_ref[...] = acc_b[...]
            dalog_ref[...] = acc_a[...] * tm_["a"]

    vec = pl.BlockSpec((1, LANES), lambda i: (0, 0))
    wide = pl.BlockSpec((q, sw), lambda i: (nc - 1 - i, 0))
    return _call(
        body, (dy, xbc, xbc, xbc, dt_raw, dt_raw, dt_bias, a_log, d_full, s_in), name="ssd_bwd", grid=(nc,),
        in_specs=[wide, wide,
                  pl.BlockSpec((q, gn), lambda i: (nc - 1 - i, bblk)),
                  pl.BlockSpec((q, gn), lambda i: (nc - 1 - i, bblk + 1)),
                  pl.BlockSpec((q, LANES), lambda i: (nc - 1 - i, 0)),
                  pl.BlockSpec((q, LANES), lambda i: (jnp.maximum(nc - 2 - i, 0), 0)), vec, vec,
                  pl.BlockSpec((1, sw), lambda i: (0, 0)),
                  pl.BlockSpec((1, n_pairs, SSD_STATE, LANES), lambda i: (nc - 1 - i, 0, 0, 0))],
        out_specs=[wide, pl.BlockSpec((q, gn), lambda i: (nc - 1 - i, 0)), pl.BlockSpec((q, gn), lambda i: (nc - 1 - i, 0)),
                   pl.BlockSpec((q, LANES), lambda i: (nc - 1 - i, 0)), vec, vec],
        out_shape=[SDS((t, sw), F32), SDS((t, gn), F32), SDS((t, gn), F32), SDS((t, LANES), BF16),
                   SDS((1, LANES), F32), SDS((1, LANES), F32)],
        scratch_shapes=[pltpu.VMEM((n_pairs, SSD_STATE, LANES), F32), pltpu.VMEM((q, sw), F32), pltpu.VMEM((q, sw), F32),
                        pltpu.VMEM((8, sw), F32), pltpu.VMEM((1, LANES), F32), pltpu.VMEM((1, LANES), F32),
                        pltpu.VMEM((sw, LANES), BF16), pltpu.VMEM((2, len(_TERM_FIELDS), q, LANES), F32)],
        sem=("arbitrary",), comm=comm)


def _gated(y_ref, xs_ref, z_ref, dsk_ref):
    y1 = y_ref[...] + dsk_ref[...] * xs_ref[...]
    return y1, y1 * _silu(z_ref[...])


def gate_norm_fwd(y, xbc, proj, d_full, norm_w, mixed, z_block, tm):
    t, sw = y.shape
    gw = sw // SSD_GROUPS

    def body(y_ref, xs_ref, z_ref, dsk_ref, nw_ref, alias_ref, o_ref):
        _, y2 = _gated(y_ref, xs_ref, z_ref, dsk_ref)
        for g in range(SSD_GROUPS):
            cols = slice(g * gw, (g + 1) * gw)
            blk = y2[:, cols]
            r = lax.rsqrt(jnp.mean(blk * blk, axis=-1, keepdims=True) + NORM_EPS)
            o_ref[:, cols] = (blk * r * nw_ref[:, cols]).astype(BF16)

    row = pl.BlockSpec((tm, sw), lambda i: (i, 0))
    vec = pl.BlockSpec((1, sw), lambda i: (0, 0))
    return pl.pallas_call(
        body, name="gate_norm_fwd", grid=(t // tm,),
        in_specs=[row, row, pl.BlockSpec((tm, sw), lambda i: (i, z_block)), vec, vec, pl.BlockSpec(memory_space=pl.ANY)],
        out_specs=pl.BlockSpec((tm, sw), lambda i: (i, 1)),
        out_shape=SDS(mixed.shape, mixed.dtype),
        input_output_aliases={5: 0},
        compiler_params=_params(("arbitrary",)),
    )(y, xbc, proj, d_full, norm_w, mixed)


def gate_norm_bwd(dmixed, y, xbc, proj, d_full, norm_w, dproj, z_block, tm):
    t, sw = y.shape
    gw = sw // SSD_GROUPS
    nt = t // tm

    def body(d_ref, y_ref, xs_ref, z_ref, dsk_ref, nw_ref, alias_ref, dy_ref, dz_ref, dnw_ref, dd_ref, acc_d):
        i = pl.program_id(0)

        @pl.when(i == 0)
        def _():
            dnw_ref[...] = jnp.zeros_like(dnw_ref)
            acc_d[...] = jnp.zeros_like(acc_d)

        y1, y2 = _gated(y_ref, xs_ref, z_ref, dsk_ref)
        d3 = d_ref[...]
        parts = []
        for g in range(SSD_GROUPS):
            cols = slice(g * gw, (g + 1) * gw)
            blk = y2[:, cols]
            r = lax.rsqrt(jnp.mean(blk * blk, axis=-1, keepdims=True) + NORM_EPS)
            n = blk * r
            dg = d3[:, cols]
            dnw_ref[:, cols] += jnp.sum(dg * n, axis=0, keepdims=True)
            parts.append(_norm_bwd(dg * nw_ref[:, cols], n, r))
        dy2 = jnp.concatenate(parts, axis=1)
        zv = z_ref[...]
        dz_ref[...] = (dy2 * y1 * _dsilu(zv)).astype(BF16)
        dy1 = dy2 * _silu(zv)
        dy_ref[...] = dy1
        acc_d[0:1, :] += jnp.sum(dy1 * xs_ref[...], axis=0, keepdims=True)

        @pl.when(i == nt - 1)
        def _():
            dd_ref[...] = _split_dot(acc_d[...], _head_selector(sw, SSD_HEAD_DIM))[0:1]

    row = pl.BlockSpec((tm, sw), lambda i: (i, 0))
    vec = pl.BlockSpec((1, sw), lambda i: (0, 0))
    return pl.pallas_call(
        body, name="gate_norm_bwd", grid=(nt,),
        in_specs=[pl.BlockSpec((tm, sw), lambda i: (i, 1)), row, row, pl.BlockSpec((tm, sw), lambda i: (i, z_block)),
                  vec, vec, pl.BlockSpec(memory_space=pl.ANY)],
        out_specs=[row, pl.BlockSpec((tm, sw), lambda i: (i, z_block)), vec, pl.BlockSpec((1, LANES), lambda i: (0, 0))],
        out_shape=[SDS((t, sw), F32), SDS(dproj.shape, dproj.dtype), SDS((1, sw), F32), SDS((1, LANES), F32)],
        scratch_shapes=[pltpu.VMEM((8, sw), F32)],
        input_output_aliases={6: 1},
        compiler_params=_params(("arbitrary",)),
    )(dmixed, y, xbc, proj, d_full, norm_w, dproj)


GATE_BLOCK = 1
Z_BLOCK = 2
CONV_BLOCK = 2


def _tiles(t):
    mm = dict(in_proj=(min(1024, t), 1024), dt_proj=(min(512, t), LANES), out_proj=(min(512, t), 1024),
              d_mixed=(min(512, t), 2048), dh=(min(512, t), 3072), dw_out=(512, 1024), dw_main=(1024, 1024),
              dw_dt=(512, LANES))
    return min(256, t), mm


def _place():
    x, y, c = lax.axis_index("x"), lax.axis_index("y"), lax.axis_index("c")
    return x, y, c, [(1 - x, y), (x, 1 - y), (1 - x, 1 - y)]


def gather_spread(shards, layer, rows=None, carry=None):
    n = len(shards)

    def make(ins, outs, ss, rs):
        x, y, c, chips = _place()
        mine = 4 * x + 2 * y + c
        peers = [(x, y, 1 - c)] + [(px, py, c) for px, py in chips]
        sends, locals_, arrivals = [], [], []
        for a in range(n):
            def place(ref, idx):
                return ref.at[idx] if rows is None else ref.at[idx, pl.ds(rows[0], rows[1])]

            src = place(ins[a], layer)
            locals_.append(pltpu.make_async_copy(src, place(outs[a], mine), ss.at[5 * a + 4]))
            for j, (px, py, pc) in enumerate(peers):
                sends.append(_remote(src, place(outs[a], mine), ss, rs, 5 * a + j, (px, py, pc)))
                arrivals.append(_remote(src, place(outs[a], 4 * px + 2 * py + pc), ss, rs, 5 * a + j, (px, py, pc)))
        return sends, locals_, arrivals

    return Comm(list(shards) + list(carry or []), [SDS((N_DEV,) + s.shape[1:], s.dtype) for s in shards],
                {n + a: a for a in range(n)} if carry else {}, 5 * n, make)


def gather_pass_on(gathered):
    def make(ins, outs, ss, rs):
        x, y, c, chips = _place()
        sends, arrivals = [], []
        for a in range(len(outs)):
            for j, (px, py) in enumerate(chips):
                blk, other = 4 * px + 2 * py + c, 4 * px + 2 * py + (1 - c)
                sends.append(_remote(outs[a].at[blk], outs[a].at[blk], ss, rs, 3 * a + j, (x, y, 1 - c)))
                arrivals.append(_remote(outs[a].at[other], outs[a].at[other], ss, rs, 3 * a + j, (x, y, 1 - c)))
        return sends, [], arrivals

    return Comm(gathered, [SDS(g.shape, g.dtype) for g in gathered], {a: a for a in range(len(gathered))},
                3 * len(gathered), make)


def sibling_swap(sends_):
    def make(ins, outs, ss, rs):
        x, y, c, _ = _place()
        cps = [_remote(ins[a], outs[a], ss, rs, a, (x, y, 1 - c)) for a in range(len(ins))]
        return cps, [], cps

    return Comm(sends_, [SDS(s.shape, s.dtype) for s in sends_], {}, len(sends_), make)


def chips_scatter(slabs, rows=None, carry=None):
    n = len(slabs)

    def make(ins, outs, ss, rs):
        x, y, c, chips = _place()
        mychip = 2 * x + y
        sends, arrivals = [], []

        def part(ref, slot):
            return ref.at[slot] if rows is None else ref.at[slot, pl.ds(rows[0], rows[1])]

        for a in range(n):
            for j, (px, py) in enumerate(chips):
                to_there = lax.rem(2 * px + py - mychip + 4, 4) - 1
                from_here = lax.rem(mychip - 2 * px - py + 4, 4) - 1
                sends.append(_remote(part(ins[a], to_there), part(outs[a], from_here), ss, rs, 3 * a + j, (px, py, c)))
                arrivals.append(_remote(part(ins[a], to_there), part(outs[a], to_there), ss, rs, 3 * a + j, (px, py, c)))
        return sends, [], arrivals

    return Comm(list(slabs) + list(carry or []), [SDS(s.shape, s.dtype) for s in slabs],
                {n + a: a for a in range(n)} if carry else {}, 3 * n, make)


def comm_only(comm, name):
    def body():
        pass

    return _call(body, (), name=name, grid=(), in_specs=[], out_specs=[], out_shape=[], comm=comm)[1]


W_IN_GATHER_EIGHTHS = (3, 1, 2, 2)


def layer_fwd(x, p, layer=0, shards=None, finish=None, nxt=False, first=False):
    t = x.shape[0]
    tm, mm = _tiles(t)
    n_heads = p["d_full"].shape[1] // SSD_HEAD_DIM
    travel = shards is not None
    nxt, first = nxt and travel, first and travel
    rows = shards["in"][0].shape[1] if travel else 0
    cuts = [0]
    for eighths in W_IN_GATHER_EIGHTHS:
        cuts.append(cuts[-1] + rows * eighths // 8)
    assert cuts[-1] == rows

    def next_w_in(part, carry):
        return gather_spread(shards["in"], layer + 1, rows=(cuts[part], cuts[part + 1] - cuts[part]), carry=carry) if nxt else None

    h, r_pre = rms_fwd(x, p["pre_w"], tm)
    proj, got = mm_nn(h, p["w_main"], F32, *mm["in_proj"], "in_proj", merge_comms([
        gather_spread(shards["small"], layer) if first else None,
        gather_spread(shards["out"], layer) if travel else None, next_w_in(0, None)]))
    n_small = len(shards["small"]) if first else 0
    got_small, got_out, got_in = got[:n_small], got[n_small:n_small + 1], got[n_small + 1:]
    dt_raw, got_small = mm_nn(h, p["w_dt"], F32, *mm["dt_proj"], "dt_proj", gather_pass_on(got_small) if first else None)
    if first:
        p = dict(p, **finish["small"](got_small))
    mixed = pool_fwd(proj, p["mixw"], p["pscale"], tm)
    xbc, got_in = conv_fwd(proj, p["conv_w"], p["conv_b"], CONV_BLOCK, tm, next_w_in(1, got_in))
    (y, s_in), got = ssd_fwd(xbc, dt_raw, p["dt_bias"], p["a_log"], n_heads,
                             merge_comms([gather_pass_on(got_out) if travel else None, next_w_in(2, got_in)]))
    if travel:
        p = dict(p, **finish["out"](got[:1]))
    mixed = gate_norm_fwd(y, xbc, proj, p["d_full"], p["norm_w"], mixed, Z_BLOCK, tm)
    out, got = mm_nn(mixed, p["w_out"], F32, *mm["out_proj"], "out_proj",
                     merge_comms([next_w_in(3, got[1:]), gather_spread(shards["small"], layer + 1)]) if nxt else None)
    (x_next, r_post), gathered = post_fwd(out, x, p["post_w"], tm, gather_pass_on(got) if nxt else None)
    return x_next, dict(x=x, h=h, r_pre=r_pre, proj=proj, dt_raw=dt_raw, xbc=xbc, y=y, s_in=s_in, mixed=mixed,
                        out=out, r_post=r_post), gathered, p


def _pair_sums(own, got):
    return [pair_sum(o, r, min(256, o.shape[1]), "pair_sum") for o, r in zip(own, got)]


def layer_bwd(g, s, p, split_in, split_rest, pending=None, last=False):
    t = g.shape[0]
    tm, mm = _tiles(t)
    d = g.shape[1]
    n_heads = p["d_full"].shape[1] // SSD_HEAD_DIM
    d_out, d_post = post_bwd(g, s["out"], s["r_post"], p["post_w"], tm)
    dmixed, got_sib = mm_nt(d_out, p["w_out"], F32, *mm["d_mixed"], d, "d_mixed", sibling_swap(pending[1]) if pending else None)
    chip_sums = _pair_sums(pending[0], got_sib) if pending else []
    dw_out, _ = mm_tn(s["mixed"], d_out, *mm["dw_out"], "dw_out")
    dproj, dq, d_pscale, d_mixw = pool_bwd_a(dmixed, s["proj"], p["mixw"], p["pscale"], tm)
    dproj = pool_bwd_b(dq, dproj, tm)
    own_rest, send_rest = split_rest(dw_out, d_mixw)
    dy, dproj, d_norm, d_dskip = gate_norm_bwd(dmixed, s["y"], s["xbc"], s["proj"], p["d_full"], p["norm_w"], dproj,
                                               Z_BLOCK, tm)
    rows_in = chip_sums[0].shape[1] if pending else 0
    cut_in = rows_in if last else rows_in // 2
    (dxs, db, dc, ddtr, d_dtb, d_alog), got = ssd_bwd(
        dy, s["xbc"], s["dt_raw"], p["dt_bias"], p["a_log"], p["d_full"], s["s_in"], n_heads,
        merge_comms([chips_scatter(chip_sums[:1], rows=(0, cut_in)) if pending else None,
                     sibling_swap(send_rest) if last else None]))
    got_first, my_sib_rest = (got[:1], got[1:]) if pending else ([], got)
    rest_comm = chips_scatter(chip_sums[1:]) if pending else None
    (dpre, d_convw, d_convb), got_rest = conv_bwd_a(dxs, db, dc, s["proj"], p["conv_w"], p["conv_b"], CONV_BLOCK, tm,
                                                    rest_comm if last else None)
    dproj = conv_bwd_b(dpre, p["conv_w"], dproj, CONV_BLOCK, tm)
    dw_main, got = mm_tn(s["h"], dproj, *mm["dw_main"], "dw_main",
                         chips_scatter(_pair_sums(own_rest, my_sib_rest)) if last else rest_comm)
    got_rest, my_chips_rest = (got_rest, got) if last else (got, [])
    dw_dt, _ = mm_tn(s["h"], ddtr, *mm["dw_dt"], "dw_dt")
    own_in, send_in = split_in(dw_main, dw_dt)
    my_sib_in = comm_only(sibling_swap(send_in), "grads_to_sibling") if last else []
    if last:
        my_sums = _pair_sums(own_in, my_sib_in)
        rows_own = my_sums[0].shape[1]
        cut_own = rows_own * 13 // 16
        dh_comm = chips_scatter(my_sums, rows=(0, cut_own))
    else:
        dh_comm = chips_scatter(chip_sums[:1], rows=(cut_in, rows_in - cut_in), carry=got_first) if pending else None
    dh, got = mm_nt(dproj, p["w_main"], F32, mm["dh"][0], d, mm["dh"][1], "dh_main", dh_comm, extra=(ddtr, p["w_dt"]))
    my_chips_in, got_first = (got, got_first) if last else ([], got if pending else got_first)
    (gx, d_pre), got = rms_bwd(dh, s["x"], s["r_pre"], p["pre_w"], g, tm,
                               chips_scatter(my_sums, rows=(cut_own, rows_own - cut_own), carry=my_chips_in) if last else None)
    my_chips_in = got if last else my_chips_in
    small = dict(pre_w=d_pre, pscale=d_pscale, conv_w=d_convw, conv_b=d_convb, dt_bias=d_dtb, a_log=d_alog,
                 d_skip=d_dskip, norm_w=d_norm, post_w=d_post)
    done = [(got_sib, got_first + got_rest)] if pending else [None]
    if last:
        done.append((my_sib_in + my_sib_rest, my_chips_in + my_chips_rest))
    return gx, small, (own_in + own_rest, send_in + send_rest), done


def _two_level_gather(x_refs, out_slots, send_sems, recv_sems, local_sems):
    x, y, c, chips = _place()
    me, sibling = (x, y, c), (x, y, 1 - c)
    n = len(x_refs)

    def copy(a, k, block, to, src=None):
        return pltpu.make_async_remote_copy(
            src_ref=out_slots[a](*block) if src is None else src, dst_ref=out_slots[a](*block),
            send_sem=send_sems.at[7 * a + k], recv_sem=recv_sems.at[7 * a + k], device_id=to, device_id_type=MESH)

    mine = [pltpu.make_async_copy(x_refs[a], out_slots[a](*me), local_sems.at[a]) for a in range(n)]
    for cp in mine:
        cp.start()
    first = []
    for a in range(n):
        first.append(copy(a, 0, me, sibling, src=x_refs[a]))
        first += [copy(a, 1 + j, me, (*chip, c), src=x_refs[a]) for j, chip in enumerate(chips)]
    for cp in first:
        cp.start()
    passed = []
    for j, chip in enumerate(chips):
        for a in range(n):
            copy(a, 1 + j, (*chip, c), me).wait_recv()
            fwd = copy(a, 4 + j, (*chip, c), sibling)
            fwd.start()
            passed.append(fwd)
    for a in range(n):
        copy(a, 0, sibling, me).wait_recv()
        for j, chip in enumerate(chips):
            copy(a, 4 + j, (*chip, 1 - c), me).wait_recv()
    for cp in first + passed:
        cp.wait_send()
    for cp in mine:
        cp.wait()


def all_gather_hbm(shards, name):
    n = len(shards)

    def body(*refs):
        x_refs, out_refs = refs[:n], refs[n:2 * n]
        send_sems, recv_sems, local_sems = refs[2 * n:]
        slots = [lambda px, py, pc, o=o: o.at[:, 4 * px + 2 * py + pc] for o in out_refs]
        _two_level_gather(x_refs, slots, send_sems, recv_sems, local_sems)

    hbm = pl.BlockSpec(memory_space=pl.ANY)
    return pl.pallas_call(
        body, name=name,
        out_shape=[SDS((s.shape[0], N_DEV) + s.shape[1:], s.dtype) for s in shards],
        in_specs=[hbm] * n, out_specs=[hbm] * n,
        scratch_shapes=[pltpu.SemaphoreType.DMA((7 * n,)), pltpu.SemaphoreType.DMA((7 * n,)), pltpu.SemaphoreType.DMA((n,))],
    )(*shards)


def all_gather_vmem(block, name):
    r, c_ = block.shape

    def body(x_ref, out_ref, send_sems, recv_sems, local_sems):
        _two_level_gather([x_ref], [lambda px, py, pc: out_ref.at[4 * px + 2 * py + pc]], send_sems, recv_sems, local_sems)

    return pl.pallas_call(
        body, name=name, out_shape=SDS((N_DEV, r, c_), block.dtype),
        in_specs=[pl.BlockSpec(memory_space=pltpu.VMEM)], out_specs=pl.BlockSpec(memory_space=pltpu.VMEM),
        scratch_shapes=[pltpu.SemaphoreType.DMA((7,)), pltpu.SemaphoreType.DMA((7,)), pltpu.SemaphoreType.DMA((1,))],
        compiler_params=_params(),
    )(block)


def _block_tiles(cols):
    base = [(cols * i) // LANES for i in range(N_DEV)]
    ends = [-((-cols * (i + 1)) // LANES) for i in range(N_DEV)]
    return base, ends, max(e - b for b, e in zip(base, ends))


def _my_lane_offset(cols):
    me = 4 * lax.axis_index("x") + 2 * lax.axis_index("y") + lax.axis_index("c")
    return lax.rem(cols * me, LANES)


def shift_cast(w, tr):
    nl, r, cols = w.shape
    width = _block_tiles(cols)[2] * LANES

    def body(x_ref, o_ref, pad):
        pad[:, width - LANES:] = jnp.zeros((tr, LANES), F32)
        pad[:, :cols] = x_ref[...]
        o_ref[...] = pltpu.roll(pad[...], _my_lane_offset(cols), 1).astype(BF16)

    assert width - LANES <= cols
    return pl.pallas_call(
        body, name="shift_cast", grid=(nl, r // tr),
        in_specs=[pl.BlockSpec((pl.Squeezed(), tr, cols), lambda l, i: (l, i, 0))],
        out_specs=pl.BlockSpec((pl.Squeezed(), tr, width), lambda l, i: (l, i, 0)),
        out_shape=SDS((nl, r, width), BF16), scratch_shapes=[pltpu.VMEM((tr, width), F32)],
        compiler_params=_params(("arbitrary", "arbitrary")))(w)


def assemble_w_in(blocks, cols, n_tail, tr):
    _, r, width = blocks.shape
    base, ends, _ = _block_tiles(cols)
    total = ends[-1]
    main_tiles = (N_DEV * cols - n_tail) // LANES
    assert main_tiles == total - 1 and (N_DEV * cols - n_tail) % LANES == 0

    def body(b_ref, main_ref, tail_ref):
        for tile in range(total):
            parts = [b_ref[i, :, (tile - base[i]) * LANES:(tile - base[i] + 1) * LANES]
                     for i in range(N_DEV) if base[i] <= tile < ends[i]]
            val = parts[0] if len(parts) == 1 else parts[0] + parts[1]
            if tile < main_tiles:
                main_ref[:, tile * LANES:(tile + 1) * LANES] = val
            else:
                tail_ref[...] = val

    return pl.pallas_call(
        body, name="assemble_w_in", grid=(r // tr,),
        in_specs=[pl.BlockSpec((N_DEV, tr, width), lambda i: (0, i, 0))],
        out_specs=[pl.BlockSpec((tr, main_tiles * LANES), lambda i: (i, 0)), pl.BlockSpec((tr, LANES), lambda i: (i, 0))],
        out_shape=[SDS((r, main_tiles * LANES), blocks.dtype), SDS((r, LANES), blocks.dtype)],
        compiler_params=_params(("arbitrary",)),
    )(blocks)


def grad_blocks(dw_main, dw_tail, cols, tr):
    r = dw_main.shape[0]
    base, _, tpb = _block_tiles(cols)
    width = tpb * LANES

    def body(m_ref, t_ref, own_ref, send_ref):
        cat = jnp.concatenate([m_ref[...], t_ref[...]], axis=1)
        south = lax.axis_index("c") == 0
        for k in range(N_DEV // 2):
            a = cat[:, base[2 * k] * LANES:base[2 * k] * LANES + width]
            b = cat[:, base[2 * k + 1] * LANES:base[2 * k + 1] * LANES + width]
            own_ref[k] = jnp.where(south, a, b)
            send_ref[k] = jnp.where(south, b, a).astype(BF16)

    return pl.pallas_call(
        body, name="grad_blocks", grid=(r // tr,),
        in_specs=[pl.BlockSpec((tr, dw_main.shape[1]), lambda i: (i, 0)), pl.BlockSpec((tr, LANES), lambda i: (i, 0))],
        out_specs=[pl.BlockSpec((N_DEV // 2, tr, width), lambda i: (0, i, 0))] * 2,
        out_shape=[SDS((N_DEV // 2, r, width), F32), SDS((N_DEV // 2, r, width), BF16)],
        compiler_params=_params(("arbitrary",)),
    )(dw_main, dw_tail)


def _adamw(w, g, m, v):
    m = ADAM_B1 * m + (1.0 - ADAM_B1) * g
    v = ADAM_B2 * v + (1.0 - ADAM_B2) * jnp.square(g)
    m_hat = m / (1.0 - ADAM_B1 ** ADAM_STEP)
    v_hat = v / (1.0 - ADAM_B2 ** ADAM_STEP)
    delta = -ADAM_LR * (m_hat / (jnp.sqrt(v_hat) + ADAM_EPS) + ADAM_WD * w)
    return delta, m, v


def _my_chip():
    return 2 * lax.axis_index("x") + lax.axis_index("y")


def pair_sum(own, got, tr, name):
    k, r, c_ = own.shape
    others = lax.rem(_my_chip() + 1 + jnp.arange(k - 1, dtype=jnp.int32), k)

    def body(others_ref, a_ref, b_ref, o_ref):
        o_ref[...] = (a_ref[...] + b_ref[...].astype(F32)).astype(BF16)

    src = pl.BlockSpec((pl.Squeezed(), tr, c_), lambda s, i, oth: (oth[s], i, 0))
    return pl.pallas_call(
        body, name=name, out_shape=SDS((k - 1, r, c_), BF16),
        grid_spec=pltpu.PrefetchScalarGridSpec(
            num_scalar_prefetch=1, grid=(k - 1, r // tr), in_specs=[src, src],
            out_specs=pl.BlockSpec((pl.Squeezed(), tr, c_), lambda s, i, oth: (s, i, 0))),
        compiler_params=_params(("arbitrary", "arbitrary")),
    )(others, own, got)


def reduce_adam(own, got_sibling, got_chips, w, m, v, prev, layer, tr, name, shifted=False):
    nl, r, cols = w.shape
    c_ = own.shape[-1]
    n_scratch = 1 if shifted else 0
    chip = jnp.reshape(_my_chip(), (1,)).astype(jnp.int32)

    def body(chip_ref, own_ref, sib_ref, c0_ref, c1_ref, c2_ref, w_ref, m_ref, v_ref, *rest):
        g_ref, d_ref, nm_ref, nv_ref = rest[len(rest) - n_scratch - 4:len(rest) - n_scratch]
        g = (own_ref[...] + sib_ref[...].astype(F32) + c0_ref[...].astype(F32) + c1_ref[...].astype(F32)
             + c2_ref[...].astype(F32))
        if shifted:
            rest[-1][...] = pltpu.roll(g, c_ - _my_lane_offset(cols), 1)
            g = rest[-1][:, :cols]
        delta, nm, nv = _adamw(w_ref[...], g, m_ref[...], v_ref[...])
        g_ref[...] = g
        d_ref[...] = delta
        nm_ref[...] = nm
        nv_ref[...] = nv

    mine = pl.BlockSpec((pl.Squeezed(), tr, c_), lambda i, ch: (ch[0], i, 0))
    lay = pl.BlockSpec((pl.Squeezed(), tr, cols), lambda i, ch: (layer, i, 0))
    chips = [pl.BlockSpec((pl.Squeezed(), tr, c_), lambda i, ch, s=s: (s, i, 0)) for s in range(3)]
    in_specs = [mine, mine] + chips + [lay, lay, lay]
    args = [chip, own, got_sibling, got_chips, got_chips, got_chips, w, m, v]
    aliases = {}
    if prev is not None:
        in_specs += [pl.BlockSpec(memory_space=pl.ANY)] * 4
        aliases = {len(args) + k: k for k in range(4)}
        args += list(prev)
    return pl.pallas_call(
        body, name=name, out_shape=[SDS((nl, r, cols), F32)] * 4, input_output_aliases=aliases,
        grid_spec=pltpu.PrefetchScalarGridSpec(
            num_scalar_prefetch=1, grid=(r // tr,), in_specs=in_specs, out_specs=[lay] * 4,
            scratch_shapes=[pltpu.VMEM((tr, c_), F32)] * n_scratch),
        compiler_params=_params(("arbitrary",)),
    )(*args)


def sum_devices(packs):
    n, r, c_ = packs.shape

    def body(p_ref, o_ref):
        acc = p_ref[0]
        for k in range(1, n):
            acc = acc + p_ref[k]
        o_ref[...] = acc

    return pl.pallas_call(body, name="sum_devices", out_shape=SDS((r, c_), F32), compiler_params=_params())(packs)


def adam_small(w, g, m, v):
    def body(w_ref, g_ref, m_ref, v_ref, d_ref, nm_ref, nv_ref):
        delta, nm, nv = _adamw(w_ref[...], g_ref[...], m_ref[...], v_ref[...])
        d_ref[...] = delta
        nm_ref[...] = nm
        nv_ref[...] = nv

    return pl.pallas_call(body, name="adam_small", out_shape=[SDS(w.shape, F32)] * 3, compiler_params=_params())(w, g, m, v)


SMALL = ("pre_norm_w", "pool_scale", "conv_b", "dt_bias", "a_log", "d_skip", "_pad", "ssd_norm_w", "post_norm_w", "conv_w")


def _pack(parts):
    flat = jnp.concatenate([parts[k] for k in SMALL], axis=1).reshape(-1, LANES)
    return jnp.pad(flat, ((0, (-flat.shape[0]) % 8), (0, 0)))


def _unpack(pack, sizes, nl):
    total = sum(sizes[k] for k in SMALL)
    flat = pack[: nl * total // LANES].reshape(nl, total)
    out, o = {}, 0
    for k in SMALL:
        out[k] = flat[:, o:o + sizes[k]]
        o += sizes[k]
    return out


def kernel(x, pre_norm_w, w_in, pool_mix_w, pool_scale, conv_w, conv_b, dt_bias, a_log, d_skip, ssd_norm_w, w_out, post_norm_w, loss_target, m_pre_norm_w, m_w_in, m_pool_mix_w, m_pool_scale, m_conv_w, m_conv_b, m_dt_bias, m_a_log, m_d_skip, m_ssd_norm_w, m_w_out, m_post_norm_w, v_pre_norm_w, v_w_in, v_pool_mix_w, v_pool_scale, v_conv_w, v_conv_b, v_dt_bias, v_a_log, v_d_skip, v_ssd_norm_w, v_w_out, v_post_norm_w):
    cx, cy, cc = lax.axis_index("x"), lax.axis_index("y"), lax.axis_index("c")
    me = 4 * cx + 2 * cy + cc
    mychip = 2 * cx + cy
    nl, d, cols = w_in.shape
    t = x.shape[1]
    n_heads = a_log.shape[1]
    sw = n_heads * SSD_HEAD_DIM
    pw = pool_scale.shape[1]
    cd = conv_b.shape[1]
    ng, gsh, gw = pool_mix_w.shape[1:]
    e_main = N_DEV * cols - n_heads
    assert x.shape[0] == 1 and pw == sw and cd == sw + 2 * SSD_GROUPS * SSD_STATE and e_main == 2 * pw + sw + cd
    assert 2 * pw + sw == CONV_BLOCK * cd and n_heads <= LANES and t % SSD_CHUNK == 0 and gsh * N_DEV == gw
    tm, _ = _tiles(t)

    shards = {"in": [shift_cast(w_in, tm)], "out": [w_out.astype(BF16)], "small": [pool_mix_w.astype(BF16), conv_w]}
    pad_h = ((0, 0), (0, LANES - n_heads))

    def params_a(l, g_in):
        w_main, w_dt = assemble_w_in(g_in, cols, n_heads, tm)
        return dict(pre_w=pre_norm_w[l:l + 1], w_main=w_main, w_dt=w_dt, pscale=pool_scale[l:l + 1], conv_b=conv_b[l:l + 1],
                    dt_bias=jnp.pad(dt_bias[l:l + 1], pad_h), a_log=jnp.pad(a_log[l:l + 1], pad_h),
                    d_full=jnp.repeat(d_skip[l:l + 1], SSD_HEAD_DIM, axis=1), norm_w=ssd_norm_w[l:l + 1],
                    post_w=post_norm_w[l:l + 1])

    finish = {"small": lambda got: dict(mixw=got[0].transpose(1, 0, 2, 3).reshape(ng, gw, gw),
                                        conv_w=got[1].transpose(1, 0, 2).reshape(CONV_WIDTH, cd)),
              "out": lambda got: dict(w_out=got[0].reshape(N_DEV * w_out.shape[1], d))}

    xs = x[0]
    saved, params = [], []
    p = params_a(0, all_gather_hbm([shards["in"][0][:1]], "gather_w_in")[0][0])
    for l in range(nl):
        xs, s, gathered, p = layer_fwd(xs, p, l, shards, finish, nxt=l + 1 < nl, first=l == 0)
        saved.append(s)
        params.append(p)
        if l + 1 < nl:
            p = dict(params_a(l + 1, gathered[0]), **finish["small"](gathered[1:]))
    loss_part, g = loss_grad(xs, loss_target[0], tm)
    loss = lax.psum(loss_part[0, 0], ("x", "y", "c"))

    big = {"w_in": (w_in, m_w_in, v_w_in), "w_out": (w_out, m_w_out, v_w_out),
           "pool_mix_w": tuple(a.reshape(nl, ng * gsh, gw) for a in (pool_mix_w, m_pool_mix_w, v_pool_mix_w))}
    names = list(big)
    big_out = {k: None for k in big}
    small_g = [None] * nl

    def apply(layer, own, got_sib, got_chips):
        for k, o, gs_, gc in zip(names, own, got_sib, got_chips):
            wk, mk, vk = big[k]
            big_out[k] = reduce_adam(o, gs_, gc, wk, mk, vk, big_out[k], layer, min(256, wk.shape[1]), "reduce_adam_" + k,
                                     shifted=(k == "w_in"))

    def split_in(dw_main, dw_dt):
        own, send = grad_blocks(dw_main, dw_dt, cols, min(128, d))
        return [own], [send]

    def split_rest(dw_out, d_mixw):
        halves = [lambda ci: lax.dynamic_index_in_dim(dw_out.reshape(4, 2, -1, d), ci, 1, keepdims=False),
                  lambda ci: lax.dynamic_index_in_dim(
                      d_mixw.reshape(ng, 4, 2, gsh, gw), ci, 2, keepdims=False).transpose(1, 0, 2, 3).reshape(4, ng * gsh, gw)]
        return [h(cc) for h in halves], [h(1 - cc).astype(BF16) for h in halves]

    pending = None
    for l in reversed(range(nl)):
        g, gr, mine, done = layer_bwd(g, saved[l], params[l], split_in, split_rest, pending, last=(l == 0))
        if pending is not None:
            apply(l + 1, pending[0], *done[0])
        if l == 0:
            apply(0, mine[0], *done[1])
        pending = mine
        small_g[l] = dict(pre_norm_w=gr["pre_w"], pool_scale=gr["pscale"], conv_b=gr["conv_b"], dt_bias=gr["dt_bias"][:, :n_heads],
                          a_log=gr["a_log"][:, :n_heads], d_skip=gr["d_skip"][:, :n_heads], _pad=jnp.zeros((1, LANES - 3 * n_heads), F32),
                          ssd_norm_w=gr["norm_w"], post_norm_w=gr["post_w"], conv_w=gr["conv_w"].reshape(1, CONV_WIDTH * cd))

    sizes = {k: small_g[0][k].shape[1] for k in SMALL}
    gsum = sum_devices(all_gather_vmem(_pack({k: jnp.concatenate([sg[k] for sg in small_g], axis=0) for k in SMALL}),
                                       "gather_small_grads"))
    gs = _unpack(gsum, sizes, nl)
    csh = conv_w.shape[2]
    gs["conv_w"] = lax.dynamic_slice_in_dim(gs["conv_w"].reshape(nl, CONV_WIDTH, cd), me * csh, csh, axis=2).reshape(nl, -1)
    lsizes = dict(sizes, conv_w=CONV_WIDTH * csh)
    zpad = jnp.zeros((nl, sizes["_pad"]), F32)

    def local(pre, scale, cb, dtb, al, dsk, nw, post, cw):
        return _pack(dict(pre_norm_w=pre, pool_scale=scale, conv_b=cb, dt_bias=dtb, a_log=al, d_skip=dsk, _pad=zpad,
                          ssd_norm_w=nw, post_norm_w=post, conv_w=cw.reshape(nl, -1)))

    wp = local(pre_norm_w, pool_scale, conv_b, dt_bias, a_log, d_skip, ssd_norm_w, post_norm_w, conv_w)
    mp = local(m_pre_norm_w, m_pool_scale, m_conv_b, m_dt_bias, m_a_log, m_d_skip, m_ssd_norm_w, m_post_norm_w, m_conv_w)
    vp = local(v_pre_norm_w, v_pool_scale, v_conv_b, v_dt_bias, v_a_log, v_d_skip, v_ssd_norm_w, v_post_norm_w, v_conv_w)
    small_out = [gs] + [_unpack(o, lsizes, nl) for o in adam_small(wp, _pack(gs), mp, vp)]

    def leaf(kind, name):
        if name in big:
            return big_out[name][kind].reshape(big[name][0].shape if name != "pool_mix_w" else pool_mix_w.shape)
        val = small_out[kind][name]
        return val.reshape(conv_w.shape) if name == "conv_w" else val

    order = ("pre_norm_w", "w_in", "pool_mix_w", "pool_scale", "conv_w", "conv_b", "dt_bias", "a_log", "d_skip",
             "ssd_norm_w", "w_out", "post_norm_w")
    return (loss, g[None]) + tuple(leaf(kind, name) for kind in range(4) for name in order)
```

```python
import jax
import jax.numpy as jnp
from jax import lax
from jax.experimental import pallas as pl
from jax.experimental.pallas import tpu as pltpu

F32 = jnp.float32
BF16 = jnp.bfloat16
SDS = jax.ShapeDtypeStruct
MESH = pl.DeviceIdType.MESH
HIGHEST = lax.Precision.HIGHEST

NORM_EPS = 1e-6
POOL_WINDOWS = (2, 4, 8, 16)
POOL_HALO = 16
CONV_WIDTH = 4
CONV_HALO = 8
SSD_CHUNK = 128
SSD_HEAD_DIM = 64
SSD_STATE = 128
SSD_GROUPS = 4
LANES = 128
N_DEV = 8

ADAM_LR = 0.001
ADAM_B1 = 0.9
ADAM_B2 = 0.999
ADAM_EPS = 1e-08
ADAM_WD = 0.01
ADAM_STEP = 10

VMEM_LIMIT = 56 * 1024 * 1024

NT = (((1,), (1,)), ((), ()))
TN = (((0,), (0,)), ((), ()))


def _params(sem=None):
    kw = dict(vmem_limit_bytes=VMEM_LIMIT)
    if sem is not None:
        kw["dimension_semantics"] = sem
    return pltpu.CompilerParams(**kw)


def _silu(v):
    return v * jax.nn.sigmoid(v)


def _dsilu(v):
    s = jax.nn.sigmoid(v)
    return s * (1.0 + v * (1.0 - s))


def _split_dot(v, sel):
    hi = v.astype(BF16)
    lo = (v - hi.astype(F32)).astype(BF16)
    return (jnp.dot(hi, sel, preferred_element_type=F32) + jnp.dot(lo, sel, preferred_element_type=F32))


def _head_selector(width, per):
    ch = lax.broadcasted_iota(jnp.int32, (width, LANES), 0)
    hd = lax.broadcasted_iota(jnp.int32, (width, LANES), 1)
    return jnp.where((ch >= hd * per) & (ch < (hd + 1) * per), 1.0, 0.0).astype(BF16)


class Comm:
    def __init__(self, inputs, out_shapes, aliases, n_sems, make):
        self.inputs, self.out_shapes, self.aliases, self.n_sems, self.make = list(inputs), list(out_shapes), dict(aliases), n_sems, make


def _remote(src, dst, send_sems, recv_sems, k, peer):
    return pltpu.make_async_remote_copy(src_ref=src, dst_ref=dst, send_sem=send_sems.at[k], recv_sem=recv_sems.at[k],
                                        device_id=peer, device_id_type=MESH)


class _SemRange:
    def __init__(self, sems, start):
        self.sems, self.start = sems, start

    @property
    def at(self):
        return self

    def __getitem__(self, k):
        return self.sems.at[self.start + k]


def merge_comms(comms):
    comms = [c for c in comms if c is not None]
    if len(comms) <= 1:
        return comms[0] if comms else None
    aliases, i_off, o_off = {}, 0, 0
    for c in comms:
        aliases.update({i_off + k: o_off + v for k, v in c.aliases.items()})
        i_off, o_off = i_off + len(c.inputs), o_off + len(c.out_shapes)

    def make(ins, outs, ss, rs):
        sends, locals_, arrivals, i0, o0, s0 = [], [], [], 0, 0, 0
        for c in comms:
            s, l, a = c.make(ins[i0:i0 + len(c.inputs)], outs[o0:o0 + len(c.out_shapes)], _SemRange(ss, s0), _SemRange(rs, s0))
            sends, locals_, arrivals = sends + s, locals_ + l, arrivals + a
            i0, o0, s0 = i0 + len(c.inputs), o0 + len(c.out_shapes), s0 + c.n_sems
        return sends, locals_, arrivals

    return Comm(sum((c.inputs for c in comms), []), sum((c.out_shapes for c in comms), []), aliases,
                sum(c.n_sems for c in comms), make)


def _call(body, args, *, name, grid, in_specs, out_specs, out_shape, scratch_shapes=(), sem=None, comm=None, aliases=None):
    in_specs, out_specs, out_shape = list(in_specs), list(out_specs), list(out_shape)
    aliases = dict(aliases or {})
    if comm is None:
        outs = pl.pallas_call(body, name=name, grid=grid, in_specs=in_specs, out_specs=out_specs, out_shape=out_shape,
                              scratch_shapes=list(scratch_shapes), input_output_aliases=aliases,
                              compiler_params=_params(sem))(*args)
        return list(outs), []
    ni, no, nci, nco, ns = len(in_specs), len(out_specs), len(comm.inputs), len(comm.out_shapes), len(scratch_shapes)
    hbm = pl.BlockSpec(memory_space=pl.ANY)

    def hosted(*refs):
        ins, cins = refs[:ni], refs[ni:ni + nci]
        outs, couts = refs[ni + nci:ni + nci + no], refs[ni + nci + no:ni + nci + no + nco]
        scratch = refs[ni + nci + no + nco:]
        sends, locals_, arrivals = comm.make(cins, couts, scratch[ns], scratch[ns + 1])
        first = last = None if grid else True
        for axis, extent in enumerate(grid):
            pid = pl.program_id(axis)
            first = (pid == 0) if first is None else first & (pid == 0)
            last = (pid == extent - 1) if last is None else last & (pid == extent - 1)

        @pl.when(first)
        def _():
            for cp in locals_ + sends:
                cp.start()

        body(*ins, *outs, *scratch[:ns])

        @pl.when(last)
        def _():
            for cp in arrivals:
                cp.wait_recv()
            for cp in sends:
                cp.wait_send()
            for cp in locals_:
                cp.wait()

    outs = pl.pallas_call(
        hosted, name=name, grid=grid, in_specs=in_specs + [hbm] * nci, out_specs=out_specs + [hbm] * nco,
        out_shape=out_shape + comm.out_shapes,
        scratch_shapes=list(scratch_shapes) + [pltpu.SemaphoreType.DMA((comm.n_sems,)), pltpu.SemaphoreType.DMA((comm.n_sems,))],
        input_output_aliases={**aliases, **{ni + k: no + v for k, v in comm.aliases.items()}},
        compiler_params=_params(sem),
    )(*args, *comm.inputs)
    return list(outs[:no]), list(outs[no:])


def rms_fwd(x, w, tm):
    t, d = x.shape

    def body(x_ref, w_ref, h_ref, r_ref):
        xv = x_ref[...]
        r = lax.rsqrt(jnp.mean(xv * xv, axis=-1, keepdims=True) + NORM_EPS)
        h_ref[...] = (xv * r * w_ref[...]).astype(BF16)
        r_ref[...] = r

    return pl.pallas_call(
        body, name="rms_fwd", grid=(t // tm,),
        in_specs=[pl.BlockSpec((tm, d), lambda i: (i, 0)), pl.BlockSpec((1, d), lambda i: (0, 0))],
        out_specs=[pl.BlockSpec((tm, d), lambda i: (i, 0)), pl.BlockSpec((tm, 1), lambda i: (i, 0))],
        out_shape=[SDS((t, d), BF16), SDS((t, 1), F32)],
        compiler_params=_params(("arbitrary",)),
    )(x, w)


def post_fwd(out, x, w, tm, comm=None):
    t, d = x.shape

    def body(o_ref, x_ref, w_ref, y_ref, r_ref):
        ov = o_ref[...]
        r = lax.rsqrt(jnp.mean(ov * ov, axis=-1, keepdims=True) + NORM_EPS)
        y_ref[...] = x_ref[...] + ov * r * w_ref[...]
        r_ref[...] = r

    return _call(
        body, (out, x, w), name="post_fwd", grid=(t // tm,),
        in_specs=[pl.BlockSpec((tm, d), lambda i: (i, 0)), pl.BlockSpec((tm, d), lambda i: (i, 0)),
                  pl.BlockSpec((1, d), lambda i: (0, 0))],
        out_specs=[pl.BlockSpec((tm, d), lambda i: (i, 0)), pl.BlockSpec((tm, 1), lambda i: (i, 0))],
        out_shape=[SDS((t, d), F32), SDS((t, 1), F32)], sem=("arbitrary",), comm=comm)


def _norm_bwd(g_n, n, r):
    return r * (g_n - n * jnp.mean(g_n * n, axis=-1, keepdims=True))


def post_bwd(g, out, r, w, tm):
    t, d = g.shape

    def body(g_ref, o_ref, r_ref, w_ref, do_ref, dw_ref):
        i = pl.program_id(0)
        gv = g_ref[...]
        rv = r_ref[...]
        n = o_ref[...] * rv
        part = jnp.sum(gv * n, axis=0, keepdims=True)

        @pl.when(i == 0)
        def _():
            dw_ref[...] = part

        @pl.when(i > 0)
        def _():
            dw_ref[...] += part

        do_ref[...] = _norm_bwd(gv * w_ref[...], n, rv).astype(BF16)

    return pl.pallas_call(
        body, name="post_bwd", grid=(t // tm,),
        in_specs=[pl.BlockSpec((tm, d), lambda i: (i, 0)), pl.BlockSpec((tm, d), lambda i: (i, 0)),
                  pl.BlockSpec((tm, 1), lambda i: (i, 0)), pl.BlockSpec((1, d), lambda i: (0, 0))],
        out_specs=[pl.BlockSpec((tm, d), lambda i: (i, 0)), pl.BlockSpec((1, d), lambda i: (0, 0))],
        out_shape=[SDS((t, d), BF16), SDS((1, d), F32)],
        compiler_params=_params(("arbitrary",)),
    )(g, out, r, w)


def rms_bwd(dh, x, r, w, g, tm, comm=None):
    t, d = x.shape

    def body(a_ref, x_ref, r_ref, w_ref, g_ref, gx_ref, dw_ref):
        i = pl.program_id(0)
        dh = a_ref[...]
        rv = r_ref[...]
        n = x_ref[...] * rv
        part = jnp.sum(dh * n, axis=0, keepdims=True)

        @pl.when(i == 0)
        def _():
            dw_ref[...] = part

        @pl.when(i > 0)
        def _():
            dw_ref[...] += part

        gx_ref[...] = g_ref[...] + _norm_bwd(dh * w_ref[...], n, rv)

    row = pl.BlockSpec((tm, d), lambda i: (i, 0))
    return _call(
        body, (dh, x, r, w, g), name="rms_bwd", grid=(t // tm,),
        in_specs=[row, row, pl.BlockSpec((tm, 1), lambda i: (i, 0)), pl.BlockSpec((1, d), lambda i: (0, 0)), row],
        out_specs=[row, pl.BlockSpec((1, d), lambda i: (0, 0))],
        out_shape=[SDS((t, d), F32), SDS((1, d), F32)], sem=("arbitrary",), comm=comm)


def loss_grad(y, target, tm):
    t, d = y.shape

    def body(y_ref, t_ref, l_ref, g_ref):
        i = pl.program_id(0)
        err = y_ref[...] - t_ref[...]
        g_ref[...] = err / d
        part = 0.5 * jnp.sum(jnp.mean(err * err, axis=-1, keepdims=True), axis=0, keepdims=True)

        @pl.when(i == 0)
        def _():
            l_ref[...] = part

        @pl.when(i > 0)
        def _():
            l_ref[...] += part

    row = pl.BlockSpec((tm, d), lambda i: (i, 0))
    return pl.pallas_call(
        body, name="loss_grad", grid=(t // tm,), in_specs=[row, row],
        out_specs=[pl.BlockSpec((1, 1), lambda i: (0, 0)), row],
        out_shape=[SDS((1, 1), F32), SDS((t, d), F32)],
        compiler_params=_params(("arbitrary",)),
    )(y, target)


def mm_nn(a, b, out_dtype, tm, tn, name, comm=None):
    m, k = a.shape
    n = b.shape[1]

    def body(a_ref, b_ref, o_ref):
        o_ref[...] = jnp.dot(a_ref[...], b_ref[...], preferred_element_type=F32).astype(out_dtype)

    outs, couts = _call(
        body, (a, b), name=name, grid=(n // tn, m // tm),
        in_specs=[pl.BlockSpec((tm, k), lambda j, i: (i, 0)), pl.BlockSpec((k, tn), lambda j, i: (0, j))],
        out_specs=[pl.BlockSpec((tm, tn), lambda j, i: (i, j))],
        out_shape=[SDS((m, n), out_dtype)], sem=("arbitrary", "arbitrary"), comm=comm)
    return outs[0], couts


def mm_nt(a, b, out_dtype, tm, tn, tk, name, comm=None, extra=None):
    m, k = a.shape
    n = b.shape[0]
    nk = k // tk

    def body(a_ref, b_ref, *rest):
        o_ref, acc_ref = rest[-2:]
        kk = pl.program_id(2)
        part = lax.dot_general(a_ref[...], b_ref[...], NT, preferred_element_type=F32)
        if nk == 1:
            if extra is not None:
                part = part + lax.dot_general(rest[0][...], rest[1][...], NT, preferred_element_type=F32)
            o_ref[...] = part.astype(out_dtype)
        else:
            @pl.when(kk == 0)
            def _():
                if extra is None:
                    acc_ref[...] = part
                else:
                    acc_ref[...] = part + lax.dot_general(rest[0][...], rest[1][...], NT, preferred_element_type=F32)

            @pl.when(kk > 0)
            def _():
                acc_ref[...] += part

            @pl.when(kk == nk - 1)
            def _():
                o_ref[...] = acc_ref[...].astype(out_dtype)

    more_specs = [] if extra is None else [pl.BlockSpec((tm, extra[0].shape[1]), lambda i, j, kk: (i, 0)),
                                           pl.BlockSpec((tn, extra[1].shape[1]), lambda i, j, kk: (j, 0))]
    outs, couts = _call(
        body, (a, b) + tuple(extra or ()), name=name, grid=(m // tm, n // tn, nk),
        in_specs=[pl.BlockSpec((tm, tk), lambda i, j, kk: (i, kk)), pl.BlockSpec((tn, tk), lambda i, j, kk: (j, kk))] + more_specs,
        out_specs=[pl.BlockSpec((tm, tn), lambda i, j, kk: (i, j))],
        out_shape=[SDS((m, n), out_dtype)],
        scratch_shapes=[pltpu.VMEM((tm, tn) if nk > 1 else (8, LANES), F32)],
        sem=("arbitrary", "arbitrary", "arbitrary"), comm=comm)
    return outs[0], couts


def mm_tn(a, b, tm, tn, name, comm=None):
    t, m = a.shape
    n = b.shape[1]

    def body(a_ref, b_ref, o_ref):
        o_ref[...] = lax.dot_general(a_ref[...], b_ref[...], TN, preferred_element_type=F32)

    outs, couts = _call(
        body, (a, b), name=name, grid=(m // tm, n // tn),
        in_specs=[pl.BlockSpec((t, tm), lambda i, j: (0, i)), pl.BlockSpec((t, tn), lambda i, j: (0, j))],
        out_specs=[pl.BlockSpec((tm, tn), lambda i, j: (i, j))],
        out_shape=[SDS((m, n), F32)], sem=("arbitrary", "arbitrary"), comm=comm)
    return outs[0], couts


def _window_sums(ext, n_rows, lookahead):
    def sh(v, k):
        return pltpu.roll(v, (n_rows - k) if lookahead else k, 0)
    s2 = ext + sh(ext, 1)
    s4 = s2 + sh(s2, 2)
    s8 = s4 + sh(s4, 4)
    s16 = s8 + sh(s8, 8)
    return (s2, s4, s8, s16)


def _pool_counts(i, tm, w):
    tpos = i * tm + lax.broadcasted_iota(jnp.int32, (tm, 1), 0)
    return jnp.minimum(tpos + 1, w).astype(F32)


def _pooled(uc_ref, up_ref, i, tm):
    cur = uc_ref[...]
    prev = jnp.where(i > 0, up_ref[...], 0.0)
    ext = jnp.concatenate([prev, cur], axis=0)
    return cur, _window_sums(ext, tm + POOL_HALO, False)


def pool_fwd(proj, mixw, scale, tm):
    t = proj.shape[0]
    pw = scale.shape[1]
    gw = pw // len(POOL_WINDOWS)
    nh = tm // POOL_HALO

    def body(uc_ref, up_ref, g_ref, w_ref, s_ref, o_ref):
        i = pl.program_id(0)
        cur, sums = _pooled(uc_ref, up_ref, i, tm)
        for g, w in enumerate(POOL_WINDOWS):
            cols = slice(g * gw, (g + 1) * gw)
            pooled = sums[g][POOL_HALO:, cols] / _pool_counts(i, tm, w) - cur[:, cols]
            mixed = jnp.dot(pooled.astype(BF16), w_ref[g], preferred_element_type=F32)
            o_ref[:, cols] = (mixed * s_ref[:, cols] * _silu(g_ref[:, cols])).astype(BF16)

    return pl.pallas_call(
        body, name="pool_fwd", grid=(t // tm,),
        in_specs=[pl.BlockSpec((tm, pw), lambda i: (i, 0)),
                  pl.BlockSpec((POOL_HALO, pw), lambda i: (jnp.maximum(i * nh - 1, 0), 0)),
                  pl.BlockSpec((tm, pw), lambda i: (i, 1)),
                  pl.BlockSpec(mixw.shape, lambda i: (0, 0, 0)),
                  pl.BlockSpec((1, pw), lambda i: (0, 0))],
        out_specs=pl.BlockSpec((tm, pw), lambda i: (i, 0)),
        out_shape=SDS((t, 2 * pw), BF16),
        compiler_params=_params(("arbitrary",)),
    )(proj, proj, proj, mixw, scale)


def pool_bwd_a(dmixed, proj, mixw, scale, tm):
    t, e = proj.shape
    pw = scale.shape[1]
    ng = len(POOL_WINDOWS)
    gw = pw // ng
    nh = tm // POOL_HALO

    def body(dy_ref, uc_ref, up_ref, g_ref, w_ref, s_ref, dg_ref, dq_ref, ds_ref, dw_ref):
        i = pl.program_id(0)

        @pl.when(i == 0)
        def _():
            ds_ref[...] = jnp.zeros_like(ds_ref)
            dw_ref[...] = jnp.zeros_like(dw_ref)

        cur, sums = _pooled(uc_ref, up_ref, i, tm)
        for g, w in enumerate(POOL_WINDOWS):
            cols = slice(g * gw, (g + 1) * gw)
            cnt = _pool_counts(i, tm, w)
            pooled = (sums[g][POOL_HALO:, cols] / cnt - cur[:, cols]).astype(BF16)
            mixed = jnp.dot(pooled, w_ref[g], preferred_element_type=F32)
            gate = g_ref[:, cols]
            dy = dy_ref[:, cols]
            sc = s_ref[:, cols]
            dg_ref[:, cols] = (dy * mixed * sc * _dsilu(gate)).astype(BF16)
            ds = dy * _silu(gate)
            ds_ref[:, cols] += jnp.sum(ds * mixed, axis=0, keepdims=True)
            dmix = (ds * sc).astype(BF16)
            dw_ref[g] += lax.dot_general(pooled, dmix, TN, preferred_element_type=F32)
            dq_ref[:, cols] = lax.dot_general(dmix, w_ref[g], NT, preferred_element_type=F32) / cnt

    return pl.pallas_call(
        body, name="pool_bwd_a", grid=(t // tm,),
        in_specs=[pl.BlockSpec((tm, pw), lambda i: (i, 0)),
                  pl.BlockSpec((tm, pw), lambda i: (i, 0)),
                  pl.BlockSpec((POOL_HALO, pw), lambda i: (jnp.maximum(i * nh - 1, 0), 0)),
                  pl.BlockSpec((tm, pw), lambda i: (i, 1)),
                  pl.BlockSpec(mixw.shape, lambda i: (0, 0, 0)),
                  pl.BlockSpec((1, pw), lambda i: (0, 0))],
        out_specs=[pl.BlockSpec((tm, pw), lambda i: (i, 1)),
                   pl.BlockSpec((tm, pw), lambda i: (i, 0)),
                   pl.BlockSpec((1, pw), lambda i: (0, 0)),
                   pl.BlockSpec((ng, gw, gw), lambda i: (0, 0, 0))],
        out_shape=[SDS((t, e), BF16), SDS((t, pw), F32), SDS((1, pw), F32), SDS((ng, gw, gw), F32)],
        compiler_params=_params(("arbitrary",)),
    )(dmixed, proj, proj, proj, mixw, scale)


def pool_bwd_b(dq, dproj, tm):
    t, pw = dq.shape
    gw = pw // len(POOL_WINDOWS)
    nh = tm // POOL_HALO
    nt = t // tm

    def body(c_ref, n_ref, alias_ref, o_ref):
        i = pl.program_id(0)
        cur = c_ref[...]
        nxt = jnp.where(i < nt - 1, n_ref[...], 0.0)
        sums = _window_sums(jnp.concatenate([cur, nxt], axis=0), tm + POOL_HALO, True)
        for g, w in enumerate(POOL_WINDOWS):
            cols = slice(g * gw, (g + 1) * gw)
            o_ref[:, cols] = (sums[g][:tm, cols] - cur[:, cols] * _pool_counts(i, tm, w)).astype(BF16)

    return pl.pallas_call(
        body, name="pool_bwd_b", grid=(nt,),
        in_specs=[pl.BlockSpec((tm, pw), lambda i: (i, 0)),
                  pl.BlockSpec((POOL_HALO, pw), lambda i: (jnp.minimum((i + 1) * nh, t // POOL_HALO - 1), 0)),
                  pl.BlockSpec(memory_space=pl.ANY)],
        out_specs=pl.BlockSpec((tm, pw), lambda i: (i, 0)),
        out_shape=SDS(dproj.shape, dproj.dtype),
        input_output_aliases={2: 0},
        compiler_params=_params(("arbitrary",)),
    )(dq, dq, dproj)


ELEMENTWISE_LANE_CHUNK = 256


def _lane_chunks(width):
    return [slice(c, c + ELEMENTWISE_LANE_CHUNK) for c in range(0, width, ELEMENTWISE_LANE_CHUNK)]


def _conv_pre(xc_ref, xp_ref, w_ref, b_ref, i, cols):
    cur = xc_ref[:, cols]
    prev = jnp.where(i > 0, xp_ref[:, cols], 0.0)
    ext = jnp.concatenate([prev, cur], axis=0)
    taps = [pltpu.roll(ext, CONV_WIDTH - 1 - k, 0)[CONV_HALO:] for k in range(CONV_WIDTH - 1)] + [cur]
    pre = b_ref[:, cols]
    for k in range(CONV_WIDTH):
        pre = pre + w_ref[k:k + 1, cols] * taps[k]
    return pre, taps


def conv_fwd(proj, conv_w, conv_b, col_block, tm, comm=None):
    t = proj.shape[0]
    cd = conv_b.shape[1]
    nh = tm // CONV_HALO

    def body(xc_ref, xp_ref, w_ref, b_ref, o_ref):
        i = pl.program_id(0)
        for cols in _lane_chunks(cd):
            pre, _ = _conv_pre(xc_ref, xp_ref, w_ref, b_ref, i, cols)
            o_ref[:, cols] = _silu(pre)

    outs, couts = _call(
        body, (proj, proj, conv_w, conv_b), name="conv_fwd", grid=(t // tm,),
        in_specs=[pl.BlockSpec((tm, cd), lambda i: (i, col_block)),
                  pl.BlockSpec((CONV_HALO, cd), lambda i: (jnp.maximum(i * nh - 1, 0), col_block)),
                  pl.BlockSpec((CONV_WIDTH, cd), lambda i: (0, 0)),
                  pl.BlockSpec((1, cd), lambda i: (0, 0))],
        out_specs=[pl.BlockSpec((tm, cd), lambda i: (i, 0))],
        out_shape=[SDS((t, cd), F32)], sem=("arbitrary",), comm=comm)
    return outs[0], couts


def conv_bwd_a(dxs, db, dc, proj, conv_w, conv_b, col_block, tm, comm=None):
    t = proj.shape[0]
    cd = conv_b.shape[1]
    sw = dxs.shape[1]
    gn = db.shape[1]
    nh = tm // CONV_HALO

    def body(dx_ref, db_ref, dc_ref, xc_ref, xp_ref, w_ref, b_ref, dp_ref, dw_ref, dbias_ref):
        i = pl.program_id(0)

        @pl.when(i == 0)
        def _():
            dw_ref[...] = jnp.zeros_like(dw_ref)
            dbias_ref[...] = jnp.zeros_like(dbias_ref)

        for cols in _lane_chunks(cd):
            pre, taps = _conv_pre(xc_ref, xp_ref, w_ref, b_ref, i, cols)
            if cols.start < sw:
                dact = dx_ref[:, cols]
            elif cols.start < sw + gn:
                dact = db_ref[:, cols.start - sw:cols.stop - sw]
            else:
                dact = dc_ref[:, cols.start - sw - gn:cols.stop - sw - gn]
            dpre = dact * _dsilu(pre)
            dp_ref[:, cols] = dpre
            dbias_ref[:, cols] += jnp.sum(dpre, axis=0, keepdims=True)
            for k in range(CONV_WIDTH):
                dw_ref[k:k + 1, cols] += jnp.sum(dpre * taps[k], axis=0, keepdims=True)

    return _call(
        body, (dxs, db, dc, proj, proj, conv_w, conv_b), name="conv_bwd_a", grid=(t // tm,),
        in_specs=[pl.BlockSpec((tm, sw), lambda i: (i, 0)), pl.BlockSpec((tm, gn), lambda i: (i, 0)),
                  pl.BlockSpec((tm, gn), lambda i: (i, 0)),
                  pl.BlockSpec((tm, cd), lambda i: (i, col_block)),
                  pl.BlockSpec((CONV_HALO, cd), lambda i: (jnp.maximum(i * nh - 1, 0), col_block)),
                  pl.BlockSpec((CONV_WIDTH, cd), lambda i: (0, 0)),
                  pl.BlockSpec((1, cd), lambda i: (0, 0))],
        out_specs=[pl.BlockSpec((tm, cd), lambda i: (i, 0)),
                   pl.BlockSpec((CONV_WIDTH, cd), lambda i: (0, 0)),
                   pl.BlockSpec((1, cd), lambda i: (0, 0))],
        out_shape=[SDS((t, cd), F32), SDS((CONV_WIDTH, cd), F32), SDS((1, cd), F32)],
        sem=("arbitrary",), comm=comm)


def conv_bwd_b(dpre, conv_w, dproj, col_block, tm):
    t, cd = dpre.shape
    nh = tm // CONV_HALO
    nt = t // tm

    def body(c_ref, n_ref, w_ref, alias_ref, o_ref):
        i = pl.program_id(0)
        n = tm + CONV_HALO
        for cols in _lane_chunks(cd):
            cur = c_ref[:, cols]
            nxt = jnp.where(i < nt - 1, n_ref[:, cols], 0.0)
            ext = jnp.concatenate([cur, nxt], axis=0)
            acc = w_ref[CONV_WIDTH - 1:CONV_WIDTH, cols] * cur
            for k in range(CONV_WIDTH - 1):
                acc = acc + w_ref[k:k + 1, cols] * pltpu.roll(ext, n - (CONV_WIDTH - 1 - k), 0)[:tm]
            o_ref[:, cols] = acc.astype(BF16)

    return pl.pallas_call(
        body, name="conv_bwd_b", grid=(nt,),
        in_specs=[pl.BlockSpec((tm, cd), lambda i: (i, 0)),
                  pl.BlockSpec((CONV_HALO, cd), lambda i: (jnp.minimum((i + 1) * nh, t // CONV_HALO - 1), 0)),
                  pl.BlockSpec((CONV_WIDTH, cd), lambda i: (0, 0)),
                  pl.BlockSpec(memory_space=pl.ANY)],
        out_specs=pl.BlockSpec((tm, cd), lambda i: (i, col_block)),
        out_shape=SDS(dproj.shape, dproj.dtype),
        input_output_aliases={3: 0},
        compiler_params=_params(("arbitrary",)),
    )(dpre, dpre, conv_w, dproj)


def _softplus(v):
    return jnp.maximum(v, 0.0) + jnp.log(1.0 + jnp.exp(-jnp.abs(v)))


def _ssd_chunk_terms(dtr_ref, bias_ref, a_ref, n_heads):
    q = SSD_CHUNK
    lane = lax.broadcasted_iota(jnp.int32, (1, LANES), 1)
    pre = dtr_ref[...] + bias_ref[...]
    dt = jnp.where(lane < n_heads, _softplus(pre), 0.0)
    a = jnp.where(lane < n_heads, -jnp.exp(a_ref[...]), 0.0)
    row = lax.broadcasted_iota(jnp.int32, (q, q), 0)
    col = lax.broadcasted_iota(jnp.int32, (q, q), 1)
    causal = row >= col
    acs = jnp.dot(causal.astype(F32), dt * a, precision=HIGHEST, preferred_element_type=F32)
    last = acs[q - 1:q, :]
    return dict(pre=pre, dt=dt, a=a, acs=acs, acs_t=acs.T, eacs=jnp.exp(acs), dstate=jnp.exp(last - acs),
                cdec=jnp.exp(last), causal=causal, diag=row == col, lane=lane)


_TERM_FIELDS = ("pre", "dt", "acs", "acs_t", "eacs", "dstate", "cdec")


def _prefetched_terms(step, dtr_ref, dtn_ref, bias_ref, a_ref, n_heads, terms_ref):
    q = SSD_CHUNK

    def store(slot, tm_):
        for f, name in enumerate(_TERM_FIELDS):
            terms_ref[slot, f] = jnp.broadcast_to(tm_[name], (q, LANES))

    @pl.when(step == 0)
    def _():
        store(0, _ssd_chunk_terms(dtr_ref, bias_ref, a_ref, n_heads))

    slot = lax.rem(step, 2)
    nxt = _ssd_chunk_terms(dtn_ref, bias_ref, a_ref, n_heads)
    tm_ = dict(nxt, **{name: terms_ref[slot, f] for f, name in enumerate(_TERM_FIELDS)})
    tm_["cdec"] = tm_["cdec"][0:1]
    return tm_, lambda: store(1 - slot, nxt)


def _pair_cols(lo, v, h):
    if v.shape[0] < 8:
        return jnp.where(lo, v[:, h:h + 1], v[:, h + 1:h + 2])
    idx = jnp.broadcast_to(jnp.where(lo, h, h + 1).astype(jnp.int32), v.shape)
    return jnp.take_along_axis(v, idx, axis=1, mode="promise_in_bounds")


def _pair_decay(tm_, cb, h):
    l0 = jnp.exp(jnp.where(tm_["causal"], tm_["acs"][:, h:h + 1] - tm_["acs_t"][h:h + 1, :], -jnp.inf))
    l1 = jnp.exp(jnp.where(tm_["causal"], tm_["acs"][:, h + 1:h + 2] - tm_["acs_t"][h + 1:h + 2, :], -jnp.inf))
    return l0, l1, jnp.concatenate([cb * l0, cb * l1], axis=1)


def _pair_decay_t(tm_, cbt, h):
    upper = jnp.logical_not(tm_["causal"]) | tm_["diag"]
    t0 = jnp.exp(jnp.where(upper, tm_["acs_t"][h:h + 1, :] - tm_["acs"][:, h:h + 1], -jnp.inf))
    t1 = jnp.exp(jnp.where(upper, tm_["acs_t"][h + 1:h + 2, :] - tm_["acs"][:, h + 1:h + 2], -jnp.inf))
    return jnp.concatenate([cbt * t0, cbt * t1], axis=0).astype(BF16)


def _block_diag(lo, xdt):
    return jnp.concatenate([jnp.where(lo, xdt, 0.0), jnp.where(lo, 0.0, xdt)], axis=0).astype(BF16)


def ssd_fwd(xbc, proj, dt_raw, dt_bias, a_log, d_full, norm_w, mixed, z_block, n_heads, comm=None):
    t = xbc.shape[0]
    q = SSD_CHUNK
    gn = SSD_GROUPS * SSD_STATE
    sw = n_heads * SSD_HEAD_DIM
    gw = sw // SSD_GROUPS
    n_pairs = n_heads // 2
    pairs_per_group = n_pairs // SSD_GROUPS
    nc = t // q
    bblk = sw // gn

    def body(xs_ref, b_ref, c_ref, dtr_ref, dtn_ref, bias_ref, a_ref, z_ref, dsk_ref, nw_ref, alias_ref,
             y_ref, sin_ref, m_ref, state, terms_ref):
        @pl.when(pl.program_id(0) == 0)
        def _():
            state[...] = jnp.zeros_like(state)

        tm_, keep_next = _prefetched_terms(pl.program_id(0), dtr_ref, dtn_ref, bias_ref, a_ref, n_heads, terms_ref)
        lo = tm_["lane"] < SSD_HEAD_DIM
        for g in range(SSD_GROUPS):
            gcols = slice(g * SSD_STATE, (g + 1) * SSD_STATE)
            bg = b_ref[:, gcols].astype(BF16)
            bg_t = b_ref[:, gcols].T.astype(BF16)
            cg = c_ref[:, gcols].astype(BF16)
            cb = lax.dot_general(cg, bg, NT, preferred_element_type=F32)
            for j in range(pairs_per_group):
                p = g * pairs_per_group + j
                h = 2 * p
                pcols = slice(p * LANES, (p + 1) * LANES)
                _, _, mcat = _pair_decay(tm_, cb, h)
                xdt = xs_ref[:, pcols] * _pair_cols(lo, tm_["dt"], h)
                ydiag = jnp.dot(mcat.astype(BF16), _block_diag(lo, xdt), preferred_element_type=F32)
                st = state[p]
                sin_ref[0, p] = st
                yoff = jnp.dot(cg, st.astype(BF16), preferred_element_type=F32) * _pair_cols(lo, tm_["eacs"], h)
                y_ref[:, pcols] = ydiag + yoff
                xw = (xdt * _pair_cols(lo, tm_["dstate"], h)).astype(BF16)
                state[p] = st * _pair_cols(lo, tm_["cdec"], h) + jnp.dot(bg_t, xw, preferred_element_type=F32)
        keep_next()
        for g in range(SSD_GROUPS):
            cols = slice(g * gw, (g + 1) * gw)
            blk = (y_ref[:, cols] + dsk_ref[:, cols] * xs_ref[:, cols]) * _silu(z_ref[:, cols])
            r = lax.rsqrt(jnp.mean(blk * blk, axis=-1, keepdims=True) + NORM_EPS)
            m_ref[:, cols] = (blk * r * nw_ref[:, cols]).astype(BF16)

    vec = pl.BlockSpec((1, LANES), lambda c: (0, 0))
    wide = pl.BlockSpec((1, sw), lambda c: (0, 0))
    return _call(
        body, (xbc, xbc, xbc, dt_raw, dt_raw, dt_bias, a_log, proj, d_full, norm_w, mixed), name="ssd_fwd", grid=(nc,),
        in_specs=[pl.BlockSpec((q, sw), lambda c: (c, 0)),
                  pl.BlockSpec((q, gn), lambda c: (c, bblk)),
                  pl.BlockSpec((q, gn), lambda c: (c, bblk + 1)),
                  pl.BlockSpec((q, LANES), lambda c: (c, 0)),
                  pl.BlockSpec((q, LANES), lambda c: (jnp.minimum(c + 1, nc - 1), 0)), vec, vec,
                  pl.BlockSpec((q, sw), lambda c: (c, z_block)), wide, wide, pl.BlockSpec(memory_space=pl.ANY)],
        out_specs=[pl.BlockSpec((q, sw), lambda c: (c, 0)),
                   pl.BlockSpec((1, n_pairs, SSD_STATE, LANES), lambda c: (c, 0, 0, 0)),
                   pl.BlockSpec((q, sw), lambda c: (c, 1))],
        out_shape=[SDS((t, sw), F32), SDS((nc, n_pairs, SSD_STATE, LANES), F32), SDS(mixed.shape, mixed.dtype)],
        scratch_shapes=[pltpu.VMEM((n_pairs, SSD_STATE, LANES), F32), pltpu.VMEM((2, len(_TERM_FIELDS), q, LANES), F32)],
        sem=("arbitrary",), comm=comm, aliases={10: 2})


def ssd_bwd(dy, xbc, dt_raw, dt_bias, a_log, d_full, s_in, n_heads, comm=None):
    t = xbc.shape[0]
    q = SSD_CHUNK
    gn = SSD_GROUPS * SSD_STATE
    sw = n_heads * SSD_HEAD_DIM
    n_pairs = n_heads // 2
    pairs_per_group = n_pairs // SSD_GROUPS
    nc = t // q
    bblk = sw // gn

    def body(dy_ref, xs_ref, b_ref, c_ref, dtr_ref, dtn_ref, bias_ref, a_ref, dsk_ref, sin_ref,
             dxs_ref, db_ref, dc_ref, ddtr_ref, dbias_ref, dalog_ref,
             dstate, tbuf, xbuf, rbuf, acc_a, acc_b, sel_ref, terms_ref):
        i = pl.program_id(0)

        @pl.when(i == 0)
        def _():
            dstate[...] = jnp.zeros_like(dstate)
            rbuf[...] = jnp.zeros_like(rbuf)
            acc_a[...] = jnp.zeros_like(acc_a)
            acc_b[...] = jnp.zeros_like(acc_b)
            sel_ref[...] = _head_selector(sw, SSD_HEAD_DIM)

        tm_, keep_next = _prefetched_terms(i, dtr_ref, dtn_ref, bias_ref, a_ref, n_heads, terms_ref)
        lane = tm_["lane"]
        lo = lane < SSD_HEAD_DIM
        head_row = lax.broadcasted_iota(jnp.int32, (LANES, 1), 0)
        rows = jnp.zeros((q, LANES), F32)
        cols_t = jnp.zeros((LANES, q), F32)
        for g in range(SSD_GROUPS):
            gcols = slice(g * SSD_STATE, (g + 1) * SSD_STATE)
            bg = b_ref[:, gcols].astype(BF16)
            cg = c_ref[:, gcols].astype(BF16)
            cg_t = c_ref[:, gcols].T.astype(BF16)
            cb = lax.dot_general(cg, bg, NT, preferred_element_type=F32)
            cbt = lax.dot_general(bg, cg, NT, preferred_element_type=F32)
            dcb = jnp.zeros((q, q), F32)
            db_acc = jnp.zeros((q, SSD_STATE), F32)
            dc_acc = jnp.zeros((q, SSD_STATE), F32)
            for j in range(pairs_per_group):
                p = g * pairs_per_group + j
                h = 2 * p
                pcols = slice(p * LANES, (p + 1) * LANES)
                l0, l1, mcat = _pair_decay(tm_, cb, h)
                xp = xs_ref[:, pcols]
                dtp = _pair_cols(lo, tm_["dt"], h)
                xdt = xp * dtp
                xbd = _block_diag(lo, xdt)
                dyp = dy_ref[:, pcols]
                dyb = dyp.astype(BF16)
                dsb = _pair_cols(lo, tm_["dstate"], h)
                cdr = _pair_cols(lo, tm_["cdec"], h)
                eb = _pair_cols(lo, tm_["eacs"], h)
                st = sin_ref[0, p]
                stb = st.astype(BF16)
                dst = dstate[p]
                dstb = dst.astype(BF16)
                dye = (dyp * eb).astype(BF16)
                both = jnp.dot(_pair_decay_t(tm_, cbt, h), dyb, preferred_element_type=F32)
                dx_state = jnp.dot(bg, dstb, preferred_element_type=F32) * dsb
                dxdt = jnp.where(lo, both[:q], both[q:]) + dx_state
                dmcat = lax.dot_general(dyb, xbd, NT, preferred_element_type=F32)
                dcb = dcb + dmcat[:, :q] * l0 + dmcat[:, q:] * l1
                dseg = dmcat * mcat
                csum = jnp.sum(dseg, axis=0, keepdims=True)
                rows = (rows + jnp.where(lane == h, jnp.sum(dseg[:, :q], axis=1, keepdims=True), 0.0)
                        + jnp.where(lane == h + 1, jnp.sum(dseg[:, q:], axis=1, keepdims=True), 0.0))
                cols_t = (cols_t + jnp.where(head_row == h, csum[:, :q], 0.0)
                          + jnp.where(head_row == h + 1, csum[:, q:], 0.0))
                dc_acc = dc_acc + lax.dot_general(dye, stb, NT, preferred_element_type=F32)
                db_acc = db_acc + lax.dot_general((xdt * dsb).astype(BF16), dstb, NT, preferred_element_type=F32)
                yoff = jnp.dot(cg, stb, preferred_element_type=F32) * eb
                tbuf[:, pcols] = dyp * yoff - xdt * dx_state
                xbuf[:, pcols] = dxdt * xp
                rbuf[0:1, pcols] = (jnp.sum(xdt * dx_state, axis=0, keepdims=True)
                                    + cdr * jnp.sum(dst * st, axis=0, keepdims=True))
                dxs_ref[:, pcols] = dxdt * dtp + dyp * dsk_ref[:, pcols]
                dstate[p] = dst * cdr + jnp.dot(cg_t, dye, preferred_element_type=F32)
            dcbb = dcb.astype(BF16)
            dc_ref[:, gcols] = dc_acc + jnp.dot(dcbb, bg, preferred_element_type=F32)
            db_ref[:, gcols] = db_acc + lax.dot_general(dcbb, cg, TN, preferred_element_type=F32)

        sel = sel_ref[...]
        dacs = rows - cols_t.T + _split_dot(tbuf[...], sel)
        carry = _split_dot(rbuf[...], sel)[0:1]
        anti = jnp.logical_not(tm_["causal"]) | tm_["diag"]
        da = jnp.dot(anti.astype(F32), dacs, precision=HIGHEST, preferred_element_type=F32) + carry
        ddt = da * tm_["a"] + _split_dot(xbuf[...], sel)
        ddtr = jnp.where(tm_["lane"] < n_heads, ddt * jax.nn.sigmoid(tm_["pre"]), 0.0)
        ddtr_ref[...] = ddtr.astype(BF16)
        acc_b[...] += jnp.sum(ddtr, axis=0, keepdims=True)
        acc_a[...] += jnp.sum(da * tm_["dt"], axis=0, keepdims=True)
        keep_next()

        @pl.when(i == nc - 1)
        def _():
            dbias_ref[...] = acc_b[...]
            dalog_ref[...] = acc_a[...] * tm_["a"]

    vec = pl.BlockSpec((1, LANES), lambda i: (0, 0))
    wide = pl.BlockSpec((q, sw), lambda i: (nc - 1 - i, 0))
    return _call(
        body, (dy, xbc, xbc, xbc, dt_raw, dt_raw, dt_bias, a_log, d_full, s_in), name="ssd_bwd", grid=(nc,),
        in_specs=[wide, wide,
                  pl.BlockSpec((q, gn), lambda i: (nc - 1 - i, bblk)),
                  pl.BlockSpec((q, gn), lambda i: (nc - 1 - i, bblk + 1)),
                  pl.BlockSpec((q, LANES), lambda i: (nc - 1 - i, 0)),
                  pl.BlockSpec((q, LANES), lambda i: (jnp.maximum(nc - 2 - i, 0), 0)), vec, vec,
                  pl.BlockSpec((1, sw), lambda i: (0, 0)),
                  pl.BlockSpec((1, n_pairs, SSD_STATE, LANES), lambda i: (nc - 1 - i, 0, 0, 0))],
        out_specs=[wide, pl.BlockSpec((q, gn), lambda i: (nc - 1 - i, 0)), pl.BlockSpec((q, gn), lambda i: (nc - 1 - i, 0)),
                   pl.BlockSpec((q, LANES), lambda i: (nc - 1 - i, 0)), vec, vec],
        out_shape=[SDS((t, sw), F32), SDS((t, gn), F32), SDS((t, gn), F32), SDS((t, LANES), BF16),
                   SDS((1, LANES), F32), SDS((1, LANES), F32)],
        scratch_shapes=[pltpu.VMEM((n_pairs, SSD_STATE, LANES), F32), pltpu.VMEM((q, sw), F32), pltpu.VMEM((q, sw), F32),
                        pltpu.VMEM((8, sw), F32), pltpu.VMEM((1, LANES), F32), pltpu.VMEM((1, LANES), F32),
                        pltpu.VMEM((sw, LANES), BF16), pltpu.VMEM((2, len(_TERM_FIELDS), q, LANES), F32)],
        sem=("arbitrary",), comm=comm)


def _gated(y_ref, xs_ref, z_ref, dsk_ref):
    y1 = y_ref[...] + dsk_ref[...] * xs_ref[...]
    return y1, y1 * _silu(z_ref[...])


def gate_norm_bwd(dmixed, y, xbc, proj, d_full, norm_w, dproj, z_block, tm):
    t, sw = y.shape
    gw = sw // SSD_GROUPS
    nt = t // tm

    def body(d_ref, y_ref, xs_ref, z_ref, dsk_ref, nw_ref, alias_ref, dy_ref, dz_ref, dnw_ref, dd_ref, acc_d):
        i = pl.program_id(0)

        @pl.when(i == 0)
        def _():
            dnw_ref[...] = jnp.zeros_like(dnw_ref)
            acc_d[...] = jnp.zeros_like(acc_d)

        y1, y2 = _gated(y_ref, xs_ref, z_ref, dsk_ref)
        d3 = d_ref[...]
        parts = []
        for g in range(SSD_GROUPS):
            cols = slice(g * gw, (g + 1) * gw)
            blk = y2[:, cols]
            r = lax.rsqrt(jnp.mean(blk * blk, axis=-1, keepdims=True) + NORM_EPS)
            n = blk * r
            dg = d3[:, cols]
            dnw_ref[:, cols] += jnp.sum(dg * n, axis=0, keepdims=True)
            parts.append(_norm_bwd(dg * nw_ref[:, cols], n, r))
        dy2 = jnp.concatenate(parts, axis=1)
        zv = z_ref[...]
        dz_ref[...] = (dy2 * y1 * _dsilu(zv)).astype(BF16)
        dy1 = dy2 * _silu(zv)
        dy_ref[...] = dy1
        acc_d[0:1, :] += jnp.sum(dy1 * xs_ref[...], axis=0, keepdims=True)

        @pl.when(i == nt - 1)
        def _():
            dd_ref[...] = _split_dot(acc_d[...], _head_selector(sw, SSD_HEAD_DIM))[0:1]

    row = pl.BlockSpec((tm, sw), lambda i: (i, 0))
    vec = pl.BlockSpec((1, sw), lambda i: (0, 0))
    return pl.pallas_call(
        body, name="gate_norm_bwd", grid=(nt,),
        in_specs=[pl.BlockSpec((tm, sw), lambda i: (i, 1)), row, row, pl.BlockSpec((tm, sw), lambda i: (i, z_block)),
                  vec, vec, pl.BlockSpec(memory_space=pl.ANY)],
        out_specs=[row, pl.BlockSpec((tm, sw), lambda i: (i, z_block)), vec, pl.BlockSpec((1, LANES), lambda i: (0, 0))],
        out_shape=[SDS((t, sw), F32), SDS(dproj.shape, dproj.dtype), SDS((1, sw), F32), SDS((1, LANES), F32)],
        scratch_shapes=[pltpu.VMEM((8, sw), F32)],
        input_output_aliases={6: 1},
        compiler_params=_params(("arbitrary",)),
    )(dmixed, y, xbc, proj, d_full, norm_w, dproj)


GATE_BLOCK = 1
Z_BLOCK = 2
CONV_BLOCK = 2


def _tiles(t):
    mm = dict(in_proj=(min(1024, t), 1024), dt_proj=(min(512, t), LANES), out_proj=(min(512, t), 1024),
              d_mixed=(min(512, t), 2048), dh=(min(512, t), 3072), dw_out=(512, 1024), dw_main=(1024, 1024),
              dw_dt=(512, LANES))
    return min(256, t), mm


def _place():
    x, y, c = lax.axis_index("x"), lax.axis_index("y"), lax.axis_index("c")
    return x, y, c, [(1 - x, y), (x, 1 - y), (1 - x, 1 - y)]


def gather_spread(shards, layer, rows=None, carry=None):
    n = len(shards)

    def make(ins, outs, ss, rs):
        x, y, c, chips = _place()
        mine = 4 * x + 2 * y + c
        peers = [(x, y, 1 - c)] + [(px, py, c) for px, py in chips]
        sends, locals_, arrivals = [], [], []
        for a in range(n):
            def place(ref, idx):
                return ref.at[idx] if rows is None else ref.at[idx, pl.ds(rows[0], rows[1])]

            src = place(ins[a], layer)
            locals_.append(pltpu.make_async_copy(src, place(outs[a], mine), ss.at[5 * a + 4]))
            for j, (px, py, pc) in enumerate(peers):
                sends.append(_remote(src, place(outs[a], mine), ss, rs, 5 * a + j, (px, py, pc)))
                arrivals.append(_remote(src, place(outs[a], 4 * px + 2 * py + pc), ss, rs, 5 * a + j, (px, py, pc)))
        return sends, locals_, arrivals

    return Comm(list(shards) + list(carry or []), [SDS((N_DEV,) + s.shape[1:], s.dtype) for s in shards],
                {n + a: a for a in range(n)} if carry else {}, 5 * n, make)


def gather_pass_on(gathered):
    def make(ins, outs, ss, rs):
        x, y, c, chips = _place()
        sends, arrivals = [], []
        for a in range(len(outs)):
            for j, (px, py) in enumerate(chips):
                blk, other = 4 * px + 2 * py + c, 4 * px + 2 * py + (1 - c)
                sends.append(_remote(outs[a].at[blk], outs[a].at[blk], ss, rs, 3 * a + j, (x, y, 1 - c)))
                arrivals.append(_remote(outs[a].at[other], outs[a].at[other], ss, rs, 3 * a + j, (x, y, 1 - c)))
        return sends, [], arrivals

    return Comm(gathered, [SDS(g.shape, g.dtype) for g in gathered], {a: a for a in range(len(gathered))},
                3 * len(gathered), make)


def sibling_swap(sends_):
    def make(ins, outs, ss, rs):
        x, y, c, _ = _place()
        cps = [_remote(ins[a], outs[a], ss, rs, a, (x, y, 1 - c)) for a in range(len(ins))]
        return cps, [], cps

    return Comm(sends_, [SDS(s.shape, s.dtype) for s in sends_], {}, len(sends_), make)


def chips_scatter(slabs, rows=None, carry=None):
    n = len(slabs)

    def make(ins, outs, ss, rs):
        x, y, c, chips = _place()
        mychip = 2 * x + y
        sends, arrivals = [], []

        def part(ref, slot):
            return ref.at[slot] if rows is None else ref.at[slot, pl.ds(rows[0], rows[1])]

        for a in range(n):
            for j, (px, py) in enumerate(chips):
                to_there = lax.rem(2 * px + py - mychip + 4, 4) - 1
                from_here = lax.rem(mychip - 2 * px - py + 4, 4) - 1
                sends.append(_remote(part(ins[a], to_there), part(outs[a], from_here), ss, rs, 3 * a + j, (px, py, c)))
                arrivals.append(_remote(part(ins[a], to_there), part(outs[a], to_there), ss, rs, 3 * a + j, (px, py, c)))
        return sends, [], arrivals

    return Comm(list(slabs) + list(carry or []), [SDS(s.shape, s.dtype) for s in slabs],
                {n + a: a for a in range(n)} if carry else {}, 3 * n, make)


def comm_only(comm, name):
    def body():
        pass

    return _call(body, (), name=name, grid=(), in_specs=[], out_specs=[], out_shape=[], comm=comm)[1]


W_IN_GATHER_EIGHTHS = (3, 1, 2, 2)


def layer_fwd(x, p, layer=0, shards=None, finish=None, nxt=False, first=False):
    t = x.shape[0]
    tm, mm = _tiles(t)
    n_heads = p["d_full"].shape[1] // SSD_HEAD_DIM
    travel = shards is not None
    nxt, first = nxt and travel, first and travel
    rows = shards["in"][0].shape[1] if travel else 0
    cuts = [0]
    for eighths in W_IN_GATHER_EIGHTHS:
        cuts.append(cuts[-1] + rows * eighths // 8)
    assert cuts[-1] == rows

    def next_w_in(part, carry):
        return gather_spread(shards["in"], layer + 1, rows=(cuts[part], cuts[part + 1] - cuts[part]), carry=carry) if nxt else None

    h, r_pre = rms_fwd(x, p["pre_w"], tm)
    proj, got = mm_nn(h, p["w_main"], F32, *mm["in_proj"], "in_proj", merge_comms([
        gather_spread(shards["small"], layer) if first else None,
        gather_spread(shards["out"], layer) if travel else None, next_w_in(0, None)]))
    n_small = len(shards["small"]) if first else 0
    got_small, got_out, got_in = got[:n_small], got[n_small:n_small + 1], got[n_small + 1:]
    dt_raw, got_small = mm_nn(h, p["w_dt"], F32, *mm["dt_proj"], "dt_proj", gather_pass_on(got_small) if first else None)
    if first:
        p = dict(p, **finish["small"](got_small))
    mixed = pool_fwd(proj, p["mixw"], p["pscale"], tm)
    xbc, got_in = conv_fwd(proj, p["conv_w"], p["conv_b"], CONV_BLOCK, tm, next_w_in(1, got_in))
    (y, s_in, mixed), got = ssd_fwd(xbc, proj, dt_raw, p["dt_bias"], p["a_log"], p["d_full"], p["norm_w"], mixed, Z_BLOCK,
                                    n_heads, merge_comms([gather_pass_on(got_out) if travel else None, next_w_in(2, got_in)]))
    if travel:
        p = dict(p, **finish["out"](got[:1]))
    out, got = mm_nn(mixed, p["w_out"], F32, *mm["out_proj"], "out_proj",
                     merge_comms([next_w_in(3, got[1:]), gather_spread(shards["small"], layer + 1)]) if nxt else None)
    (x_next, r_post), gathered = post_fwd(out, x, p["post_w"], tm, gather_pass_on(got) if nxt else None)
    return x_next, dict(x=x, h=h, r_pre=r_pre, proj=proj, dt_raw=dt_raw, xbc=xbc, y=y, s_in=s_in, mixed=mixed,
                        out=out, r_post=r_post), gathered, p


def _pair_sums(own, got):
    return [pair_sum(o, r, min(256, o.shape[1]), "pair_sum") for o, r in zip(own, got)]


def layer_bwd(g, s, p, split_in, split_rest, pending=None, last=False):
    t = g.shape[0]
    tm, mm = _tiles(t)
    d = g.shape[1]
    n_heads = p["d_full"].shape[1] // SSD_HEAD_DIM
    d_out, d_post = post_bwd(g, s["out"], s["r_post"], p["post_w"], tm)
    dmixed, got_sib = mm_nt(d_out, p["w_out"], F32, *mm["d_mixed"], d, "d_mixed", sibling_swap(pending[1]) if pending else None)
    chip_sums = _pair_sums(pending[0], got_sib) if pending else []
    rows_in = chip_sums[0].shape[1] if pending else 0
    early = rows_in * 3 // 8 if last else 0
    dw_out, got_early = mm_tn(s["mixed"], d_out, *mm["dw_out"], "dw_out",
                              chips_scatter(chip_sums[:1], rows=(0, early)) if pending and last else None)
    dproj, dq, d_pscale, d_mixw = pool_bwd_a(dmixed, s["proj"], p["mixw"], p["pscale"], tm)
    dproj = pool_bwd_b(dq, dproj, tm)
    own_rest, send_rest = split_rest(dw_out, d_mixw)
    dy, dproj, d_norm, d_dskip = gate_norm_bwd(dmixed, s["y"], s["xbc"], s["proj"], p["d_full"], p["norm_w"], dproj,
                                               Z_BLOCK, tm)
    cut_in = rows_in if last else rows_in // 2
    (dxs, db, dc, ddtr, d_dtb, d_alog), got = ssd_bwd(
        dy, s["xbc"], s["dt_raw"], p["dt_bias"], p["a_log"], p["d_full"], s["s_in"], n_heads,
        merge_comms([chips_scatter(chip_sums[:1], rows=(early, cut_in - early), carry=got_early or None) if pending else None,
                     sibling_swap(send_rest) if last else None]))
    got_first, my_sib_rest = (got[:1], got[1:]) if pending else ([], got)
    rest_comm = chips_scatter(chip_sums[1:]) if pending else None
    (dpre, d_convw, d_convb), got_rest = conv_bwd_a(dxs, db, dc, s["proj"], p["conv_w"], p["conv_b"], CONV_BLOCK, tm,
                                                    rest_comm if last else None)
    dproj = conv_bwd_b(dpre, p["conv_w"], dproj, CONV_BLOCK, tm)
    dw_main, got = mm_tn(s["h"], dproj, *mm["dw_main"], "dw_main",
                         chips_scatter(_pair_sums(own_rest, my_sib_rest)) if last else rest_comm)
    got_rest, my_chips_rest = (got_rest, got) if last else (got, [])
    dw_dt, _ = mm_tn(s["h"], ddtr, *mm["dw_dt"], "dw_dt")
    own_in, send_in = split_in(dw_main, dw_dt)
    my_sib_in = comm_only(sibling_swap(send_in), "grads_to_sibling") if last else []
    if last:
        my_sums = _pair_sums(own_in, my_sib_in)
        rows_own = my_sums[0].shape[1]
        cut_own = rows_own * 13 // 16
        dh_comm = chips_scatter(my_sums, rows=(0, cut_own))
    else:
        dh_comm = chips_scatter(chip_sums[:1], rows=(cut_in, rows_in - cut_in), carry=got_first) if pending else None
    dh, got = mm_nt(dproj, p["w_main"], F32, mm["dh"][0], d, mm["dh"][1], "dh_main", dh_comm, extra=(ddtr, p["w_dt"]))
    my_chips_in, got_first = (got, got_first) if last else ([], got if pending else got_first)
    (gx, d_pre), got = rms_bwd(dh, s["x"], s["r_pre"], p["pre_w"], g, tm,
                               chips_scatter(my_sums, rows=(cut_own, rows_own - cut_own), carry=my_chips_in) if last else None)
    my_chips_in = got if last else my_chips_in
    small = dict(pre_w=d_pre, pscale=d_pscale, conv_w=d_convw, conv_b=d_convb, dt_bias=d_dtb, a_log=d_alog,
                 d_skip=d_dskip, norm_w=d_norm, post_w=d_post)
    done = [(got_sib, got_first + got_rest)] if pending else [None]
    if last:
        done.append((my_sib_in + my_sib_rest, my_chips_in + my_chips_rest))
    return gx, small, (own_in + own_rest, send_in + send_rest), done


def _two_level_gather(x_refs, out_slots, send_sems, recv_sems, local_sems):
    x, y, c, chips = _place()
    me, sibling = (x, y, c), (x, y, 1 - c)
    n = len(x_refs)

    def copy(a, k, block, to, src=None):
        return pltpu.make_async_remote_copy(
            src_ref=out_slots[a](*block) if src is None else src, dst_ref=out_slots[a](*block),
            send_sem=send_sems.at[7 * a + k], recv_sem=recv_sems.at[7 * a + k], device_id=to, device_id_type=MESH)

    mine = [pltpu.make_async_copy(x_refs[a], out_slots[a](*me), local_sems.at[a]) for a in range(n)]
    for cp in mine:
        cp.start()
    first = []
    for a in range(n):
        first.append(copy(a, 0, me, sibling, src=x_refs[a]))
        first += [copy(a, 1 + j, me, (*chip, c), src=x_refs[a]) for j, chip in enumerate(chips)]
    for cp in first:
        cp.start()
    passed = []
    for j, chip in enumerate(chips):
        for a in range(n):
            copy(a, 1 + j, (*chip, c), me).wait_recv()
            fwd = copy(a, 4 + j, (*chip, c), sibling)
            fwd.start()
            passed.append(fwd)
    for a in range(n):
        copy(a, 0, sibling, me).wait_recv()
        for j, chip in enumerate(chips):
            copy(a, 4 + j, (*chip, 1 - c), me).wait_recv()
    for cp in first + passed:
        cp.wait_send()
    for cp in mine:
        cp.wait()


def all_gather_hbm(shards, name):
    n = len(shards)

    def body(*refs):
        x_refs, out_refs = refs[:n], refs[n:2 * n]
        send_sems, recv_sems, local_sems = refs[2 * n:]
        slots = [lambda px, py, pc, o=o: o.at[:, 4 * px + 2 * py + pc] for o in out_refs]
        _two_level_gather(x_refs, slots, send_sems, recv_sems, local_sems)

    hbm = pl.BlockSpec(memory_space=pl.ANY)
    return pl.pallas_call(
        body, name=name,
        out_shape=[SDS((s.shape[0], N_DEV) + s.shape[1:], s.dtype) for s in shards],
        in_specs=[hbm] * n, out_specs=[hbm] * n,
        scratch_shapes=[pltpu.SemaphoreType.DMA((7 * n,)), pltpu.SemaphoreType.DMA((7 * n,)), pltpu.SemaphoreType.DMA((n,))],
    )(*shards)


def all_gather_vmem(block, name):
    r, c_ = block.shape

    def body(x_ref, out_ref, send_sems, recv_sems, local_sems):
        _two_level_gather([x_ref], [lambda px, py, pc: out_ref.at[4 * px + 2 * py + pc]], send_sems, recv_sems, local_sems)

    return pl.pallas_call(
        body, name=name, out_shape=SDS((N_DEV, r, c_), block.dtype),
        in_specs=[pl.BlockSpec(memory_space=pltpu.VMEM)], out_specs=pl.BlockSpec(memory_space=pltpu.VMEM),
        scratch_shapes=[pltpu.SemaphoreType.DMA((7,)), pltpu.SemaphoreType.DMA((7,)), pltpu.SemaphoreType.DMA((1,))],
        compiler_params=_params(),
    )(block)


def _block_tiles(cols):
    base = [(cols * i) // LANES for i in range(N_DEV)]
    ends = [-((-cols * (i + 1)) // LANES) for i in range(N_DEV)]
    return base, ends, max(e - b for b, e in zip(base, ends))


def _my_lane_offset(cols):
    me = 4 * lax.axis_index("x") + 2 * lax.axis_index("y") + lax.axis_index("c")
    return lax.rem(cols * me, LANES)


def shift_cast(w, tr):
    nl, r, cols = w.shape
    width = _block_tiles(cols)[2] * LANES

    def body(x_ref, o_ref, pad):
        pad[:, width - LANES:] = jnp.zeros((tr, LANES), F32)
        pad[:, :cols] = x_ref[...]
        o_ref[...] = pltpu.roll(pad[...], _my_lane_offset(cols), 1).astype(BF16)

    assert width - LANES <= cols
    return pl.pallas_call(
        body, name="shift_cast", grid=(nl, r // tr),
        in_specs=[pl.BlockSpec((pl.Squeezed(), tr, cols), lambda l, i: (l, i, 0))],
        out_specs=pl.BlockSpec((pl.Squeezed(), tr, width), lambda l, i: (l, i, 0)),
        out_shape=SDS((nl, r, width), BF16), scratch_shapes=[pltpu.VMEM((tr, width), F32)],
        compiler_params=_params(("arbitrary", "arbitrary")))(w)


def assemble_w_in(blocks, cols, n_tail, tr):
    _, r, width = blocks.shape
    base, ends, _ = _block_tiles(cols)
    total = ends[-1]
    main_tiles = (N_DEV * cols - n_tail) // LANES
    assert main_tiles == total - 1 and (N_DEV * cols - n_tail) % LANES == 0

    def body(b_ref, main_ref, tail_ref):
        for tile in range(total):
            parts = [b_ref[i, :, (tile - base[i]) * LANES:(tile - base[i] + 1) * LANES]
                     for i in range(N_DEV) if base[i] <= tile < ends[i]]
            val = parts[0] if len(parts) == 1 else parts[0] + parts[1]
            if tile < main_tiles:
                main_ref[:, tile * LANES:(tile + 1) * LANES] = val
            else:
                tail_ref[...] = val

    return pl.pallas_call(
        body, name="assemble_w_in", grid=(r // tr,),
        in_specs=[pl.BlockSpec((N_DEV, tr, width), lambda i: (0, i, 0))],
        out_specs=[pl.BlockSpec((tr, main_tiles * LANES), lambda i: (i, 0)), pl.BlockSpec((tr, LANES), lambda i: (i, 0))],
        out_shape=[SDS((r, main_tiles * LANES), blocks.dtype), SDS((r, LANES), blocks.dtype)],
        compiler_params=_params(("arbitrary",)),
    )(blocks)


def grad_blocks(dw_main, dw_tail, cols, tr):
    r = dw_main.shape[0]
    base, _, tpb = _block_tiles(cols)
    width = tpb * LANES

    def body(m_ref, t_ref, own_ref, send_ref):
        cat = jnp.concatenate([m_ref[...], t_ref[...]], axis=1)
        south = lax.axis_index("c") == 0
        for k in range(N_DEV // 2):
            a = cat[:, base[2 * k] * LANES:base[2 * k] * LANES + width]
            b = cat[:, base[2 * k + 1] * LANES:base[2 * k + 1] * LANES + width]
            own_ref[k] = jnp.where(south, a, b)
            send_ref[k] = jnp.where(south, b, a).astype(BF16)

    return pl.pallas_call(
        body, name="grad_blocks", grid=(r // tr,),
        in_specs=[pl.BlockSpec((tr, dw_main.shape[1]), lambda i: (i, 0)), pl.BlockSpec((tr, LANES), lambda i: (i, 0))],
        out_specs=[pl.BlockSpec((N_DEV // 2, tr, width), lambda i: (0, i, 0))] * 2,
        out_shape=[SDS((N_DEV // 2, r, width), F32), SDS((N_DEV // 2, r, width), BF16)],
        compiler_params=_params(("arbitrary",)),
    )(dw_main, dw_tail)


def _adamw(w, g, m, v):
    m = ADAM_B1 * m + (1.0 - ADAM_B1) * g
    v = ADAM_B2 * v + (1.0 - ADAM_B2) * jnp.square(g)
    m_hat = m / (1.0 - ADAM_B1 ** ADAM_STEP)
    v_hat = v / (1.0 - ADAM_B2 ** ADAM_STEP)
    delta = -ADAM_LR * (m_hat / (jnp.sqrt(v_hat) + ADAM_EPS) + ADAM_WD * w)
    return delta, m, v


def _my_chip():
    return 2 * lax.axis_index("x") + lax.axis_index("y")


def pair_sum(own, got, tr, name):
    k, r, c_ = own.shape
    others = lax.rem(_my_chip() + 1 + jnp.arange(k - 1, dtype=jnp.int32), k)

    def body(others_ref, a_ref, b_ref, o_ref):
        o_ref[...] = (a_ref[...] + b_ref[...].astype(F32)).astype(BF16)

    src = pl.BlockSpec((pl.Squeezed(), tr, c_), lambda s, i, oth: (oth[s], i, 0))
    return pl.pallas_call(
        body, name=name, out_shape=SDS((k - 1, r, c_), BF16),
        grid_spec=pltpu.PrefetchScalarGridSpec(
            num_scalar_prefetch=1, grid=(k - 1, r // tr), in_specs=[src, src],
            out_specs=pl.BlockSpec((pl.Squeezed(), tr, c_), lambda s, i, oth: (s, i, 0))),
        compiler_params=_params(("arbitrary", "arbitrary")),
    )(others, own, got)


def reduce_adam(own, got_sibling, got_chips, w, m, v, prev, layer, tr, name, shifted=False):
    nl, r, cols = w.shape
    c_ = own.shape[-1]
    n_scratch = 1 if shifted else 0
    chip = jnp.reshape(_my_chip(), (1,)).astype(jnp.int32)

    def body(chip_ref, own_ref, sib_ref, c0_ref, c1_ref, c2_ref, w_ref, m_ref, v_ref, *rest):
        g_ref, d_ref, nm_ref, nv_ref = rest[len(rest) - n_scratch - 4:len(rest) - n_scratch]
        g = (own_ref[...] + sib_ref[...].astype(F32) + c0_ref[...].astype(F32) + c1_ref[...].astype(F32)
             + c2_ref[...].astype(F32))
        if shifted:
            rest[-1][...] = pltpu.roll(g, c_ - _my_lane_offset(cols), 1)
            g = rest[-1][:, :cols]
        delta, nm, nv = _adamw(w_ref[...], g, m_ref[...], v_ref[...])
        g_ref[...] = g
        d_ref[...] = delta
        nm_ref[...] = nm
        nv_ref[...] = nv

    mine = pl.BlockSpec((pl.Squeezed(), tr, c_), lambda i, ch: (ch[0], i, 0))
    lay = pl.BlockSpec((pl.Squeezed(), tr, cols), lambda i, ch: (layer, i, 0))
    chips = [pl.BlockSpec((pl.Squeezed(), tr, c_), lambda i, ch, s=s: (s, i, 0)) for s in range(3)]
    in_specs = [mine, mine] + chips + [lay, lay, lay]
    args = [chip, own, got_sibling, got_chips, got_chips, got_chips, w, m, v]
    aliases = {}
    if prev is not None:
        in_specs += [pl.BlockSpec(memory_space=pl.ANY)] * 4
        aliases = {len(args) + k: k for k in range(4)}
        args += list(prev)
    return pl.pallas_call(
        body, name=name, out_shape=[SDS((nl, r, cols), F32)] * 4, input_output_aliases=aliases,
        grid_spec=pltpu.PrefetchScalarGridSpec(
            num_scalar_prefetch=1, grid=(r // tr,), in_specs=in_specs, out_specs=[lay] * 4,
            scratch_shapes=[pltpu.VMEM((tr, c_), F32)] * n_scratch),
        compiler_params=_params(("arbitrary",)),
    )(*args)


def sum_devices(packs):
    n, r, c_ = packs.shape

    def body(p_ref, o_ref):
        acc = p_ref[0]
        for k in range(1, n):
            acc = acc + p_ref[k]
        o_ref[...] = acc

    return pl.pallas_call(body, name="sum_devices", out_shape=SDS((r, c_), F32), compiler_params=_params())(packs)


def adam_small(w, g, m, v):
    def body(w_ref, g_ref, m_ref, v_ref, d_ref, nm_ref, nv_ref):
        delta, nm, nv = _adamw(w_ref[...], g_ref[...], m_ref[...], v_ref[...])
        d_ref[...] = delta
        nm_ref[...] = nm
        nv_ref[...] = nv

    return pl.pallas_call(body, name="adam_small", out_shape=[SDS(w.shape, F32)] * 3, compiler_params=_params())(w, g, m, v)


SMALL = ("pre_norm_w", "pool_scale", "conv_b", "dt_bias", "a_log", "d_skip", "_pad", "ssd_norm_w", "post_norm_w", "conv_w")


def _pack(parts):
    flat = jnp.concatenate([parts[k] for k in SMALL], axis=1).reshape(-1, LANES)
    return jnp.pad(flat, ((0, (-flat.shape[0]) % 8), (0, 0)))


def _unpack(pack, sizes, nl):
    total = sum(sizes[k] for k in SMALL)
    flat = pack[: nl * total // LANES].reshape(nl, total)
    out, o = {}, 0
    for k in SMALL:
        out[k] = flat[:, o:o + sizes[k]]
        o += sizes[k]
    return out


def kernel(x, pre_norm_w, w_in, pool_mix_w, pool_scale, conv_w, conv_b, dt_bias, a_log, d_skip, ssd_norm_w, w_out, post_norm_w, loss_target, m_pre_norm_w, m_w_in, m_pool_mix_w, m_pool_scale, m_conv_w, m_conv_b, m_dt_bias, m_a_log, m_d_skip, m_ssd_norm_w, m_w_out, m_post_norm_w, v_pre_norm_w, v_w_in, v_pool_mix_w, v_pool_scale, v_conv_w, v_conv_b, v_dt_bias, v_a_log, v_d_skip, v_ssd_norm_w, v_w_out, v_post_norm_w):
    cx, cy, cc = lax.axis_index("x"), lax.axis_index("y"), lax.axis_index("c")
    me = 4 * cx + 2 * cy + cc
    mychip = 2 * cx + cy
    nl, d, cols = w_in.shape
    t = x.shape[1]
    n_heads = a_log.shape[1]
    sw = n_heads * SSD_HEAD_DIM
    pw = pool_scale.shape[1]
    cd = conv_b.shape[1]
    ng, gsh, gw = pool_mix_w.shape[1:]
    e_main = N_DEV * cols - n_heads
    assert x.shape[0] == 1 and pw == sw and cd == sw + 2 * SSD_GROUPS * SSD_STATE and e_main == 2 * pw + sw + cd
    assert 2 * pw + sw == CONV_BLOCK * cd and n_heads <= LANES and t % SSD_CHUNK == 0 and gsh * N_DEV == gw
    tm, _ = _tiles(t)

    shards = {"in": [shift_cast(w_in, tm)], "out": [w_out.astype(BF16)], "small": [pool_mix_w.astype(BF16), conv_w]}
    pad_h = ((0, 0), (0, LANES - n_heads))

    def params_a(l, g_in):
        w_main, w_dt = assemble_w_in(g_in, cols, n_heads, tm)
        return dict(pre_w=pre_norm_w[l:l + 1], w_main=w_main, w_dt=w_dt, pscale=pool_scale[l:l + 1], conv_b=conv_b[l:l + 1],
                    dt_bias=jnp.pad(dt_bias[l:l + 1], pad_h), a_log=jnp.pad(a_log[l:l + 1], pad_h),
                    d_full=jnp.repeat(d_skip[l:l + 1], SSD_HEAD_DIM, axis=1), norm_w=ssd_norm_w[l:l + 1],
                    post_w=post_norm_w[l:l + 1])

    finish = {"small": lambda got: dict(mixw=got[0].transpose(1, 0, 2, 3).reshape(ng, gw, gw),
                                        conv_w=got[1].transpose(1, 0, 2).reshape(CONV_WIDTH, cd)),
              "out": lambda got: dict(w_out=got[0].reshape(N_DEV * w_out.shape[1], d))}

    xs = x[0]
    saved, params = [], []
    p = params_a(0, all_gather_hbm([shards["in"][0][:1]], "gather_w_in")[0][0])
    for l in range(nl):
        xs, s, gathered, p = layer_fwd(xs, p, l, shards, finish, nxt=l + 1 < nl, first=l == 0)
        saved.append(s)
        params.append(p)
        if l + 1 < nl:
            p = dict(params_a(l + 1, gathered[0]), **finish["small"](gathered[1:]))
    loss_part, g = loss_grad(xs, loss_target[0], tm)
    loss = lax.psum(loss_part[0, 0], ("x", "y", "c"))

    big = {"w_in": (w_in, m_w_in, v_w_in), "w_out": (w_out, m_w_out, v_w_out),
           "pool_mix_w": tuple(a.reshape(nl, ng * gsh, gw) for a in (pool_mix_w, m_pool_mix_w, v_pool_mix_w))}
    names = list(big)
    big_out = {k: None for k in big}
    small_g = [None] * nl

    def apply(layer, own, got_sib, got_chips):
        for k, o, gs_, gc in zip(names, own, got_sib, got_chips):
            wk, mk, vk = big[k]
            big_out[k] = reduce_adam(o, gs_, gc, wk, mk, vk, big_out[k], layer, min(256, wk.shape[1]), "reduce_adam_" + k,
                                     shifted=(k == "w_in"))

    def split_in(dw_main, dw_dt):
        own, send = grad_blocks(dw_main, dw_dt, cols, min(128, d))
        return [own], [send]

    def split_rest(dw_out, d_mixw):
        halves = [lambda ci: lax.dynamic_index_in_dim(dw_out.reshape(4, 2, -1, d), ci, 1, keepdims=False),
                  lambda ci: lax.dynamic_index_in_dim(
                      d_mixw.reshape(ng, 4, 2, gsh, gw), ci, 2, keepdims=False).transpose(1, 0, 2, 3).reshape(4, ng * gsh, gw)]
        return [h(cc) for h in halves], [h(1 - cc).astype(BF16) for h in halves]

    pending = None
    for l in reversed(range(nl)):
        g, gr, mine, done = layer_bwd(g, saved[l], params[l], split_in, split_rest, pending, last=(l == 0))
        if pending is not None:
            apply(l + 1, pending[0], *done[0])
        if l == 0:
            apply(0, mine[0], *done[1])
        pending = mine
        small_g[l] = dict(pre_norm_w=gr["pre_w"], pool_scale=gr["pscale"], conv_b=gr["conv_b"], dt_bias=gr["dt_bias"][:, :n_heads],
                          a_log=gr["a_log"][:, :n_heads], d_skip=gr["d_skip"][:, :n_heads], _pad=jnp.zeros((1, LANES - 3 * n_heads), F32),
                          ssd_norm_w=gr["norm_w"], post_norm_w=gr["post_w"], conv_w=gr["conv_w"].reshape(1, CONV_WIDTH * cd))

    sizes = {k: small_g[0][k].shape[1] for k in SMALL}
    gsum = sum_devices(all_gather_vmem(_pack({k: jnp.concatenate([sg[k] for sg in small_g], axis=0) for k in SMALL}),
                                       "gather_small_grads"))
    gs = _unpack(gsum, sizes, nl)
    csh = conv_w.shape[2]
    gs["conv_w"] = lax.dynamic_slice_in_dim(gs["conv_w"].reshape(nl, CONV_WIDTH, cd), me * csh, csh, axis=2).reshape(nl, -1)
    lsizes = dict(sizes, conv_w=CONV_WIDTH * csh)
    zpad = jnp.zeros((nl, sizes["_pad"]), F32)

    def local(pre, scale, cb, dtb, al, dsk, nw, post, cw):
        return _pack(dict(pre_norm_w=pre, pool_scale=scale, conv_b=cb, dt_bias=dtb, a_log=al, d_skip=dsk, _pad=zpad,
                          ssd_norm_w=nw, post_norm_w=post, conv_w=cw.reshape(nl, -1)))

    wp = local(pre_norm_w, pool_scale, conv_b, dt_bias, a_log, d_skip, ssd_norm_w, post_norm_w, conv_w)
    mp = local(m_pre_norm_w, m_pool_scale, m_conv_b, m_dt_bias, m_a_log, m_d_skip, m_ssd_norm_w, m_post_norm_w, m_conv_w)
    vp = local(v_pre_norm_w, v_pool_scale, v_conv_b, v_dt_bias, v_a_log, v_d_skip, v_ssd_norm_w, v_post_norm_w, v_conv_w)
    small_out = [gs] + [_unpack(o, lsizes, nl) for o in adam_small(wp, _pack(gs), mp, vp)]

    def leaf(kind, name):
        if name in big:
            return big_out[name][kind].reshape(big[name][0].shape if name != "pool_mix_w" else pool_mix_w.shape)
        val = small_out[kind][name]
        return val.reshape(conv_w.shape) if name == "conv_w" else val

    order = ("pre_norm_w", "w_in", "pool_mix_w", "pool_scale", "conv_w", "conv_b", "dt_bias", "a_log", "d_skip",
             "ssd_norm_w", "w_out", "post_norm_w")
    return (loss, g[None]) + tuple(leaf(kind, name) for kind in range(4) for name in order)
```

```python
import jax
import jax.numpy as jnp
from jax import lax
from jax.experimental import pallas as pl
from jax.experimental.pallas import tpu as pltpu

F32 = jnp.float32
BF16 = jnp.bfloat16
SDS = jax.ShapeDtypeStruct
MESH = pl.DeviceIdType.MESH
HIGHEST = lax.Precision.HIGHEST

NORM_EPS = 1e-6
POOL_WINDOWS = (2, 4, 8, 16)
POOL_HALO = 16
CONV_WIDTH = 4
CONV_HALO = 8
SSD_CHUNK = 128
SSD_HEAD_DIM = 64
SSD_STATE = 128
SSD_GROUPS = 4
LANES = 128
N_DEV = 8

ADAM_LR = 0.001
ADAM_B1 = 0.9
ADAM_B2 = 0.999
ADAM_EPS = 1e-08
ADAM_WD = 0.01
ADAM_STEP = 10

VMEM_LIMIT = 56 * 1024 * 1024

NT = (((1,), (1,)), ((), ()))
TN = (((0,), (0,)), ((), ()))


def _params(sem=None):
    kw = dict(vmem_limit_bytes=VMEM_LIMIT)
    if sem is not None:
        kw["dimension_semantics"] = sem
    return pltpu.CompilerParams(**kw)


def _silu(v):
    return v * jax.nn.sigmoid(v)


def _dsilu(v):
    s = jax.nn.sigmoid(v)
    return s * (1.0 + v * (1.0 - s))


def _split_dot(v, sel):
    hi = v.astype(BF16)
    lo = (v - hi.astype(F32)).astype(BF16)
    return (jnp.dot(hi, sel, preferred_element_type=F32) + jnp.dot(lo, sel, preferred_element_type=F32))


def _head_selector(width, per):
    ch = lax.broadcasted_iota(jnp.int32, (width, LANES), 0)
    hd = lax.broadcasted_iota(jnp.int32, (width, LANES), 1)
    return jnp.where((ch >= hd * per) & (ch < (hd + 1) * per), 1.0, 0.0).astype(BF16)


class Comm:
    def __init__(self, inputs, out_shapes, aliases, n_sems, make):
        self.inputs, self.out_shapes, self.aliases, self.n_sems, self.make = list(inputs), list(out_shapes), dict(aliases), n_sems, make


def _remote(src, dst, send_sems, recv_sems, k, peer):
    return pltpu.make_async_remote_copy(src_ref=src, dst_ref=dst, send_sem=send_sems.at[k], recv_sem=recv_sems.at[k],
                                        device_id=peer, device_id_type=MESH)


class _SemRange:
    def __init__(self, sems, start):
        self.sems, self.start = sems, start

    @property
    def at(self):
        return self

    def __getitem__(self, k):
        return self.sems.at[self.start + k]


def merge_comms(comms):
    comms = [c for c in comms if c is not None]
    if len(comms) <= 1:
        return comms[0] if comms else None
    aliases, i_off, o_off = {}, 0, 0
    for c in comms:
        aliases.update({i_off + k: o_off + v for k, v in c.aliases.items()})
        i_off, o_off = i_off + len(c.inputs), o_off + len(c.out_shapes)

    def make(ins, outs, ss, rs):
        sends, locals_, arrivals, i0, o0, s0 = [], [], [], 0, 0, 0
        for c in comms:
            s, l, a = c.make(ins[i0:i0 + len(c.inputs)], outs[o0:o0 + len(c.out_shapes)], _SemRange(ss, s0), _SemRange(rs, s0))
            sends, locals_, arrivals = sends + s, locals_ + l, arrivals + a
            i0, o0, s0 = i0 + len(c.inputs), o0 + len(c.out_shapes), s0 + c.n_sems
        return sends, locals_, arrivals

    return Comm(sum((c.inputs for c in comms), []), sum((c.out_shapes for c in comms), []), aliases,
                sum(c.n_sems for c in comms), make)


def _call(body, args, *, name, grid, in_specs, out_specs, out_shape, scratch_shapes=(), sem=None, comm=None, aliases=None):
    in_specs, out_specs, out_shape = list(in_specs), list(out_specs), list(out_shape)
    aliases = dict(aliases or {})
    if comm is None:
        outs = pl.pallas_call(body, name=name, grid=grid, in_specs=in_specs, out_specs=out_specs, out_shape=out_shape,
                              scratch_shapes=list(scratch_shapes), input_output_aliases=aliases,
                              compiler_params=_params(sem))(*args)
        return list(outs), []
    ni, no, nci, nco, ns = len(in_specs), len(out_specs), len(comm.inputs), len(comm.out_shapes), len(scratch_shapes)
    hbm = pl.BlockSpec(memory_space=pl.ANY)

    def hosted(*refs):
        ins, cins = refs[:ni], refs[ni:ni + nci]
        outs, couts = refs[ni + nci:ni + nci + no], refs[ni + nci + no:ni + nci + no + nco]
        scratch = refs[ni + nci + no + nco:]
        sends, locals_, arrivals = comm.make(cins, couts, scratch[ns], scratch[ns + 1])
        first = last = None if grid else True
        for axis, extent in enumerate(grid):
            pid = pl.program_id(axis)
            first = (pid == 0) if first is None else first & (pid == 0)
            last = (pid == extent - 1) if last is None else last & (pid == extent - 1)

        @pl.when(first)
        def _():
            for cp in locals_ + sends:
                cp.start()

        body(*ins, *outs, *scratch[:ns])

        @pl.when(last)
        def _():
            for cp in arrivals:
                cp.wait_recv()
            for cp in sends:
                cp.wait_send()
            for cp in locals_:
                cp.wait()

    outs = pl.pallas_call(
        hosted, name=name, grid=grid, in_specs=in_specs + [hbm] * nci, out_specs=out_specs + [hbm] * nco,
        out_shape=out_shape + comm.out_shapes,
        scratch_shapes=list(scratch_shapes) + [pltpu.SemaphoreType.DMA((comm.n_sems,)), pltpu.SemaphoreType.DMA((comm.n_sems,))],
        input_output_aliases={**aliases, **{ni + k: no + v for k, v in comm.aliases.items()}},
        compiler_params=_params(sem),
    )(*args, *comm.inputs)
    return list(outs[:no]), list(outs[no:])


def rms_fwd(x, w, tm):
    t, d = x.shape

    def body(x_ref, w_ref, h_ref, r_ref):
        xv = x_ref[...]
        r = lax.rsqrt(jnp.mean(xv * xv, axis=-1, keepdims=True) + NORM_EPS)
        h_ref[...] = (xv * r * w_ref[...]).astype(BF16)
        r_ref[...] = r

    return pl.pallas_call(
        body, name="rms_fwd", grid=(t // tm,),
        in_specs=[pl.BlockSpec((tm, d), lambda i: (i, 0)), pl.BlockSpec((1, d), lambda i: (0, 0))],
        out_specs=[pl.BlockSpec((tm, d), lambda i: (i, 0)), pl.BlockSpec((tm, 1), lambda i: (i, 0))],
        out_shape=[SDS((t, d), BF16), SDS((t, 1), F32)],
        compiler_params=_params(("arbitrary",)),
    )(x, w)


def post_fwd(out, x, w, tm, comm=None):
    t, d = x.shape

    def body(o_ref, x_ref, w_ref, y_ref, r_ref):
        ov = o_ref[...]
        r = lax.rsqrt(jnp.mean(ov * ov, axis=-1, keepdims=True) + NORM_EPS)
        y_ref[...] = x_ref[...] + ov * r * w_ref[...]
        r_ref[...] = r

    return _call(
        body, (out, x, w), name="post_fwd", grid=(t // tm,),
        in_specs=[pl.BlockSpec((tm, d), lambda i: (i, 0)), pl.BlockSpec((tm, d), lambda i: (i, 0)),
                  pl.BlockSpec((1, d), lambda i: (0, 0))],
        out_specs=[pl.BlockSpec((tm, d), lambda i: (i, 0)), pl.BlockSpec((tm, 1), lambda i: (i, 0))],
        out_shape=[SDS((t, d), F32), SDS((t, 1), F32)], sem=("arbitrary",), comm=comm)


def _norm_bwd(g_n, n, r):
    return r * (g_n - n * jnp.mean(g_n * n, axis=-1, keepdims=True))


def post_bwd(g, out, r, w, tm):
    t, d = g.shape

    def body(g_ref, o_ref, r_ref, w_ref, do_ref, dw_ref):
        i = pl.program_id(0)
        gv = g_ref[...]
        rv = r_ref[...]
        n = o_ref[...] * rv
        part = jnp.sum(gv * n, axis=0, keepdims=True)

        @pl.when(i == 0)
        def _():
            dw_ref[...] = part

        @pl.when(i > 0)
        def _():
            dw_ref[...] += part

        do_ref[...] = _norm_bwd(gv * w_ref[...], n, rv).astype(BF16)

    return pl.pallas_call(
        body, name="post_bwd", grid=(t // tm,),
        in_specs=[pl.BlockSpec((tm, d), lambda i: (i, 0)), pl.BlockSpec((tm, d), lambda i: (i, 0)),
                  pl.BlockSpec((tm, 1), lambda i: (i, 0)), pl.BlockSpec((1, d), lambda i: (0, 0))],
        out_specs=[pl.BlockSpec((tm, d), lambda i: (i, 0)), pl.BlockSpec((1, d), lambda i: (0, 0))],
        out_shape=[SDS((t, d), BF16), SDS((1, d), F32)],
        compiler_params=_params(("arbitrary",)),
    )(g, out, r, w)


def rms_bwd(dh, x, r, w, g, tm, comm=None):
    t, d = x.shape

    def body(a_ref, x_ref, r_ref, w_ref, g_ref, gx_ref, dw_ref):
        i = pl.program_id(0)
        dh = a_ref[...]
        rv = r_ref[...]
        n = x_ref[...] * rv
        part = jnp.sum(dh * n, axis=0, keepdims=True)

        @pl.when(i == 0)
        def _():
            dw_ref[...] = part

        @pl.when(i > 0)
        def _():
            dw_ref[...] += part

        gx_ref[...] = g_ref[...] + _norm_bwd(dh * w_ref[...], n, rv)

    row = pl.BlockSpec((tm, d), lambda i: (i, 0))
    return _call(
        body, (dh, x, r, w, g), name="rms_bwd", grid=(t // tm,),
        in_specs=[row, row, pl.BlockSpec((tm, 1), lambda i: (i, 0)), pl.BlockSpec((1, d), lambda i: (0, 0)), row],
        out_specs=[row, pl.BlockSpec((1, d), lambda i: (0, 0))],
        out_shape=[SDS((t, d), F32), SDS((1, d), F32)], sem=("arbitrary",), comm=comm)


def loss_grad(y, target, tm):
    t, d = y.shape

    def body(y_ref, t_ref, l_ref, g_ref):
        i = pl.program_id(0)
        err = y_ref[...] - t_ref[...]
        g_ref[...] = err / d
        part = 0.5 * jnp.sum(jnp.mean(err * err, axis=-1, keepdims=True), axis=0, keepdims=True)

        @pl.when(i == 0)
        def _():
            l_ref[...] = part

        @pl.when(i > 0)
        def _():
            l_ref[...] += part

    row = pl.BlockSpec((tm, d), lambda i: (i, 0))
    return pl.pallas_call(
        body, name="loss_grad", grid=(t // tm,), in_specs=[row, row],
        out_specs=[pl.BlockSpec((1, 1), lambda i: (0, 0)), row],
        out_shape=[SDS((1, 1), F32), SDS((t, d), F32)],
        compiler_params=_params(("arbitrary",)),
    )(y, target)


def mm_nn(a, b, out_dtype, tm, tn, name, comm=None):
    m, k = a.shape
    n = b.shape[1]

    def body(a_ref, b_ref, o_ref):
        o_ref[...] = jnp.dot(a_ref[...], b_ref[...], preferred_element_type=F32).astype(out_dtype)

    outs, couts = _call(
        body, (a, b), name=name, grid=(n // tn, m // tm),
        in_specs=[pl.BlockSpec((tm, k), lambda j, i: (i, 0)), pl.BlockSpec((k, tn), lambda j, i: (0, j))],
        out_specs=[pl.BlockSpec((tm, tn), lambda j, i: (i, j))],
        out_shape=[SDS((m, n), out_dtype)], sem=("arbitrary", "arbitrary"), comm=comm)
    return outs[0], couts


def mm_nt(a, b, out_dtype, tm, tn, tk, name, comm=None, extra=None):
    m, k = a.shape
    n = b.shape[0]
    nk = k // tk

    def body(a_ref, b_ref, *rest):
        o_ref, acc_ref = rest[-2:]
        kk = pl.program_id(2)
        part = lax.dot_general(a_ref[...], b_ref[...], NT, preferred_element_type=F32)
        if nk == 1:
            if extra is not None:
                part = part + lax.dot_general(rest[0][...], rest[1][...], NT, preferred_element_type=F32)
            o_ref[...] = part.astype(out_dtype)
        else:
            @pl.when(kk == 0)
            def _():
                if extra is None:
                    acc_ref[...] = part
                else:
                    acc_ref[...] = part + lax.dot_general(rest[0][...], rest[1][...], NT, preferred_element_type=F32)

            @pl.when(kk > 0)
            def _():
                acc_ref[...] += part

            @pl.when(kk == nk - 1)
            def _():
                o_ref[...] = acc_ref[...].astype(out_dtype)

    more_specs = [] if extra is None else [pl.BlockSpec((tm, extra[0].shape[1]), lambda i, j, kk: (i, 0)),
                                           pl.BlockSpec((tn, extra[1].shape[1]), lambda i, j, kk: (j, 0))]
    outs, couts = _call(
        body, (a, b) + tuple(extra or ()), name=name, grid=(m // tm, n // tn, nk),
        in_specs=[pl.BlockSpec((tm, tk), lambda i, j, kk: (i, kk)), pl.BlockSpec((tn, tk), lambda i, j, kk: (j, kk))] + more_specs,
        out_specs=[pl.BlockSpec((tm, tn), lambda i, j, kk: (i, j))],
        out_shape=[SDS((m, n), out_dtype)],
        scratch_shapes=[pltpu.VMEM((tm, tn) if nk > 1 else (8, LANES), F32)],
        sem=("arbitrary", "arbitrary", "arbitrary"), comm=comm)
    return outs[0], couts


def mm_tn(a, b, tm, tn, name, comm=None):
    t, m = a.shape
    n = b.shape[1]

    def body(a_ref, b_ref, o_ref):
        o_ref[...] = lax.dot_general(a_ref[...], b_ref[...], TN, preferred_element_type=F32)

    outs, couts = _call(
        body, (a, b), name=name, grid=(m // tm, n // tn),
        in_specs=[pl.BlockSpec((t, tm), lambda i, j: (0, i)), pl.BlockSpec((t, tn), lambda i, j: (0, j))],
        out_specs=[pl.BlockSpec((tm, tn), lambda i, j: (i, j))],
        out_shape=[SDS((m, n), F32)], sem=("arbitrary", "arbitrary"), comm=comm)
    return outs[0], couts


def _window_sums(ext, n_rows, lookahead):
    def sh(v, k):
        return pltpu.roll(v, (n_rows - k) if lookahead else k, 0)
    s2 = ext + sh(ext, 1)
    s4 = s2 + sh(s2, 2)
    s8 = s4 + sh(s4, 4)
    s16 = s8 + sh(s8, 8)
    return (s2, s4, s8, s16)


def _pool_counts(i, tm, w):
    tpos = i * tm + lax.broadcasted_iota(jnp.int32, (tm, 1), 0)
    return jnp.minimum(tpos + 1, w).astype(F32)


def _pooled(uc_ref, up_ref, i, tm):
    cur = uc_ref[...]
    prev = jnp.where(i > 0, up_ref[...], 0.0)
    ext = jnp.concatenate([prev, cur], axis=0)
    return cur, _window_sums(ext, tm + POOL_HALO, False)


def pool_fwd(proj, mixw, scale, tm):
    t = proj.shape[0]
    pw = scale.shape[1]
    gw = pw // len(POOL_WINDOWS)
    nh = tm // POOL_HALO

    def body(uc_ref, up_ref, g_ref, w_ref, s_ref, o_ref):
        i = pl.program_id(0)
        cur, sums = _pooled(uc_ref, up_ref, i, tm)
        for g, w in enumerate(POOL_WINDOWS):
            cols = slice(g * gw, (g + 1) * gw)
            pooled = sums[g][POOL_HALO:, cols] / _pool_counts(i, tm, w) - cur[:, cols]
            mixed = jnp.dot(pooled.astype(BF16), w_ref[g], preferred_element_type=F32)
            o_ref[:, cols] = (mixed * s_ref[:, cols] * _silu(g_ref[:, cols])).astype(BF16)

    return pl.pallas_call(
        body, name="pool_fwd", grid=(t // tm,),
        in_specs=[pl.BlockSpec((tm, pw), lambda i: (i, 0)),
                  pl.BlockSpec((POOL_HALO, pw), lambda i: (jnp.maximum(i * nh - 1, 0), 0)),
                  pl.BlockSpec((tm, pw), lambda i: (i, 1)),
                  pl.BlockSpec(mixw.shape, lambda i: (0, 0, 0)),
                  pl.BlockSpec((1, pw), lambda i: (0, 0))],
        out_specs=pl.BlockSpec((tm, pw), lambda i: (i, 0)),
        out_shape=SDS((t, 2 * pw), BF16),
        compiler_params=_params(("arbitrary",)),
    )(proj, proj, proj, mixw, scale)


def pool_bwd_a(dmixed, proj, mixw, scale, tm):
    t, e = proj.shape
    pw = scale.shape[1]
    ng = len(POOL_WINDOWS)
    gw = pw // ng
    nh = tm // POOL_HALO

    def body(dy_ref, uc_ref, up_ref, g_ref, w_ref, s_ref, dg_ref, dq_ref, ds_ref, dw_ref):
        i = pl.program_id(0)

        @pl.when(i == 0)
        def _():
            ds_ref[...] = jnp.zeros_like(ds_ref)
            dw_ref[...] = jnp.zeros_like(dw_ref)

        cur, sums = _pooled(uc_ref, up_ref, i, tm)
        for g, w in enumerate(POOL_WINDOWS):
            cols = slice(g * gw, (g + 1) * gw)
            cnt = _pool_counts(i, tm, w)
            pooled = (sums[g][POOL_HALO:, cols] / cnt - cur[:, cols]).astype(BF16)
            mixed = jnp.dot(pooled, w_ref[g], preferred_element_type=F32)
            gate = g_ref[:, cols]
            dy = dy_ref[:, cols]
            sc = s_ref[:, cols]
            dg_ref[:, cols] = (dy * mixed * sc * _dsilu(gate)).astype(BF16)
            ds = dy * _silu(gate)
            ds_ref[:, cols] += jnp.sum(ds * mixed, axis=0, keepdims=True)
            dmix = (ds * sc).astype(BF16)
            dw_ref[g] += lax.dot_general(pooled, dmix, TN, preferred_element_type=F32)
            dq_ref[:, cols] = lax.dot_general(dmix, w_ref[g], NT, preferred_element_type=F32) / cnt

    return pl.pallas_call(
        body, name="pool_bwd_a", grid=(t // tm,),
        in_specs=[pl.BlockSpec((tm, pw), lambda i: (i, 0)),
                  pl.BlockSpec((tm, pw), lambda i: (i, 0)),
                  pl.BlockSpec((POOL_HALO, pw), lambda i: (jnp.maximum(i * nh - 1, 0), 0)),
                  pl.BlockSpec((tm, pw), lambda i: (i, 1)),
                  pl.BlockSpec(mixw.shape, lambda i: (0, 0, 0)),
                  pl.BlockSpec((1, pw), lambda i: (0, 0))],
        out_specs=[pl.BlockSpec((tm, pw), lambda i: (i, 1)),
                   pl.BlockSpec((tm, pw), lambda i: (i, 0)),
                   pl.BlockSpec((1, pw), lambda i: (0, 0)),
                   pl.BlockSpec((ng, gw, gw), lambda i: (0, 0, 0))],
        out_shape=[SDS((t, e), BF16), SDS((t, pw), F32), SDS((1, pw), F32), SDS((ng, gw, gw), F32)],
        compiler_params=_params(("arbitrary",)),
    )(dmixed, proj, proj, proj, mixw, scale)


def pool_bwd_b(dq, dproj, tm):
    t, pw = dq.shape
    gw = pw // len(POOL_WINDOWS)
    nh = tm // POOL_HALO
    nt = t // tm

    def body(c_ref, n_ref, alias_ref, o_ref):
        i = pl.program_id(0)
        cur = c_ref[...]
        nxt = jnp.where(i < nt - 1, n_ref[...], 0.0)
        sums = _window_sums(jnp.concatenate([cur, nxt], axis=0), tm + POOL_HALO, True)
        for g, w in enumerate(POOL_WINDOWS):
            cols = slice(g * gw, (g + 1) * gw)
            o_ref[:, cols] = (sums[g][:tm, cols] - cur[:, cols] * _pool_counts(i, tm, w)).astype(BF16)

    return pl.pallas_call(
        body, name="pool_bwd_b", grid=(nt,),
        in_specs=[pl.BlockSpec((tm, pw), lambda i: (i, 0)),
                  pl.BlockSpec((POOL_HALO, pw), lambda i: (jnp.minimum((i + 1) * nh, t // POOL_HALO - 1), 0)),
                  pl.BlockSpec(memory_space=pl.ANY)],
        out_specs=pl.BlockSpec((tm, pw), lambda i: (i, 0)),
        out_shape=SDS(dproj.shape, dproj.dtype),
        input_output_aliases={2: 0},
        compiler_params=_params(("arbitrary",)),
    )(dq, dq, dproj)


ELEMENTWISE_LANE_CHUNK = 256


def _lane_chunks(width):
    return [slice(c, c + ELEMENTWISE_LANE_CHUNK) for c in range(0, width, ELEMENTWISE_LANE_CHUNK)]


def _conv_pre(xc_ref, xp_ref, w_ref, b_ref, i, cols):
    cur = xc_ref[:, cols]
    prev = jnp.where(i > 0, xp_ref[:, cols], 0.0)
    ext = jnp.concatenate([prev, cur], axis=0)
    taps = [pltpu.roll(ext, CONV_WIDTH - 1 - k, 0)[CONV_HALO:] for k in range(CONV_WIDTH - 1)] + [cur]
    pre = b_ref[:, cols]
    for k in range(CONV_WIDTH):
        pre = pre + w_ref[k:k + 1, cols] * taps[k]
    return pre, taps


def conv_fwd(proj, conv_w, conv_b, col_block, tm, comm=None):
    t = proj.shape[0]
    cd = conv_b.shape[1]
    nh = tm // CONV_HALO

    def body(xc_ref, xp_ref, w_ref, b_ref, o_ref):
        i = pl.program_id(0)
        for cols in _lane_chunks(cd):
            pre, _ = _conv_pre(xc_ref, xp_ref, w_ref, b_ref, i, cols)
            o_ref[:, cols] = _silu(pre)

    outs, couts = _call(
        body, (proj, proj, conv_w, conv_b), name="conv_fwd", grid=(t // tm,),
        in_specs=[pl.BlockSpec((tm, cd), lambda i: (i, col_block)),
                  pl.BlockSpec((CONV_HALO, cd), lambda i: (jnp.maximum(i * nh - 1, 0), col_block)),
                  pl.BlockSpec((CONV_WIDTH, cd), lambda i: (0, 0)),
                  pl.BlockSpec((1, cd), lambda i: (0, 0))],
        out_specs=[pl.BlockSpec((tm, cd), lambda i: (i, 0))],
        out_shape=[SDS((t, cd), F32)], sem=("arbitrary",), comm=comm)
    return outs[0], couts


def conv_bwd_a(dxs, db, dc, proj, conv_w, conv_b, col_block, tm, comm=None):
    t = proj.shape[0]
    cd = conv_b.shape[1]
    sw = dxs.shape[1]
    gn = db.shape[1]
    nh = tm // CONV_HALO

    def body(dx_ref, db_ref, dc_ref, xc_ref, xp_ref, w_ref, b_ref, dp_ref, dw_ref, dbias_ref):
        i = pl.program_id(0)

        @pl.when(i == 0)
        def _():
            dw_ref[...] = jnp.zeros_like(dw_ref)
            dbias_ref[...] = jnp.zeros_like(dbias_ref)

        for cols in _lane_chunks(cd):
            pre, taps = _conv_pre(xc_ref, xp_ref, w_ref, b_ref, i, cols)
            if cols.start < sw:
                dact = dx_ref[:, cols]
            elif cols.start < sw + gn:
                dact = db_ref[:, cols.start - sw:cols.stop - sw]
            else:
                dact = dc_ref[:, cols.start - sw - gn:cols.stop - sw - gn]
            dpre = dact * _dsilu(pre)
            dp_ref[:, cols] = dpre
            dbias_ref[:, cols] += jnp.sum(dpre, axis=0, keepdims=True)
            for k in range(CONV_WIDTH):
                dw_ref[k:k + 1, cols] += jnp.sum(dpre * taps[k], axis=0, keepdims=True)

    return _call(
        body, (dxs, db, dc, proj, proj, conv_w, conv_b), name="conv_bwd_a", grid=(t // tm,),
        in_specs=[pl.BlockSpec((tm, sw), lambda i: (i, 0)), pl.BlockSpec((tm, gn), lambda i: (i, 0)),
                  pl.BlockSpec((tm, gn), lambda i: (i, 0)),
                  pl.BlockSpec((tm, cd), lambda i: (i, col_block)),
                  pl.BlockSpec((CONV_HALO, cd), lambda i: (jnp.maximum(i * nh - 1, 0), col_block)),
                  pl.BlockSpec((CONV_WIDTH, cd), lambda i: (0, 0)),
                  pl.BlockSpec((1, cd), lambda i: (0, 0))],
        out_specs=[pl.BlockSpec((tm, cd), lambda i: (i, 0)),
                   pl.BlockSpec((CONV_WIDTH, cd), lambda i: (0, 0)),
                   pl.BlockSpec((1, cd), lambda i: (0, 0))],
        out_shape=[SDS((t, cd), F32), SDS((CONV_WIDTH, cd), F32), SDS((1, cd), F32)],
        sem=("arbitrary",), comm=comm)


def conv_bwd_b(dpre, conv_w, dproj, col_block, tm):
    t, cd = dpre.shape
    nh = tm // CONV_HALO
    nt = t // tm

    def body(c_ref, n_ref, w_ref, alias_ref, o_ref):
        i = pl.program_id(0)
        n = tm + CONV_HALO
        for cols in _lane_chunks(cd):
            cur = c_ref[:, cols]
            nxt = jnp.where(i < nt - 1, n_ref[:, cols], 0.0)
            ext = jnp.concatenate([cur, nxt], axis=0)
            acc = w_ref[CONV_WIDTH - 1:CONV_WIDTH, cols] * cur
            for k in range(CONV_WIDTH - 1):
                acc = acc + w_ref[k:k + 1, cols] * pltpu.roll(ext, n - (CONV_WIDTH - 1 - k), 0)[:tm]
            o_ref[:, cols] = acc.astype(BF16)

    return pl.pallas_call(
        body, name="conv_bwd_b", grid=(nt,),
        in_specs=[pl.BlockSpec((tm, cd), lambda i: (i, 0)),
                  pl.BlockSpec((CONV_HALO, cd), lambda i: (jnp.minimum((i + 1) * nh, t // CONV_HALO - 1), 0)),
                  pl.BlockSpec((CONV_WIDTH, cd), lambda i: (0, 0)),
                  pl.BlockSpec(memory_space=pl.ANY)],
        out_specs=pl.BlockSpec((tm, cd), lambda i: (i, col_block)),
        out_shape=SDS(dproj.shape, dproj.dtype),
        input_output_aliases={3: 0},
        compiler_params=_params(("arbitrary",)),
    )(dpre, dpre, conv_w, dproj)


def _softplus(v):
    return jnp.maximum(v, 0.0) + jnp.log(1.0 + jnp.exp(-jnp.abs(v)))


def _ssd_chunk_terms(dtr_ref, bias_ref, a_ref, n_heads):
    q = SSD_CHUNK
    lane = lax.broadcasted_iota(jnp.int32, (1, LANES), 1)
    pre = dtr_ref[...] + bias_ref[...]
    dt = jnp.where(lane < n_heads, _softplus(pre), 0.0)
    a = jnp.where(lane < n_heads, -jnp.exp(a_ref[...]), 0.0)
    row = lax.broadcasted_iota(jnp.int32, (q, q), 0)
    col = lax.broadcasted_iota(jnp.int32, (q, q), 1)
    causal = row >= col
    acs = jnp.dot(causal.astype(F32), dt * a, precision=HIGHEST, preferred_element_type=F32)
    last = acs[q - 1:q, :]
    return dict(pre=pre, dt=dt, a=a, acs=acs, acs_t=acs.T, eacs=jnp.exp(acs), dstate=jnp.exp(last - acs),
                cdec=jnp.exp(last), causal=causal, diag=row == col, lane=lane)


_TERM_FIELDS = ("pre", "dt", "acs", "acs_t", "eacs", "dstate", "cdec")


def _prefetched_terms(step, dtr_ref, dtn_ref, bias_ref, a_ref, n_heads, terms_ref):
    q = SSD_CHUNK

    def store(slot, tm_):
        for f, name in enumerate(_TERM_FIELDS):
            terms_ref[slot, f] = jnp.broadcast_to(tm_[name], (q, LANES))

    @pl.when(step == 0)
    def _():
        store(0, _ssd_chunk_terms(dtr_ref, bias_ref, a_ref, n_heads))

    slot = lax.rem(step, 2)
    nxt = _ssd_chunk_terms(dtn_ref, bias_ref, a_ref, n_heads)
    tm_ = dict(nxt, **{name: terms_ref[slot, f] for f, name in enumerate(_TERM_FIELDS)})
    tm_["cdec"] = tm_["cdec"][0:1]
    return tm_, lambda: store(1 - slot, nxt)


def _pair_cols(lo, v, h):
    if v.shape[0] < 8:
        return jnp.where(lo, v[:, h:h + 1], v[:, h + 1:h + 2])
    idx = jnp.broadcast_to(jnp.where(lo, h, h + 1).astype(jnp.int32), v.shape)
    return jnp.take_along_axis(v, idx, axis=1, mode="promise_in_bounds")


def _pair_decay(tm_, cb, h):
    l0 = jnp.exp(jnp.where(tm_["causal"], tm_["acs"][:, h:h + 1] - tm_["acs_t"][h:h + 1, :], -jnp.inf))
    l1 = jnp.exp(jnp.where(tm_["causal"], tm_["acs"][:, h + 1:h + 2] - tm_["acs_t"][h + 1:h + 2, :], -jnp.inf))
    return l0, l1, jnp.concatenate([cb * l0, cb * l1], axis=1)


def _pair_decay_t(tm_, cbt, h):
    upper = jnp.logical_not(tm_["causal"]) | tm_["diag"]
    t0 = jnp.exp(jnp.where(upper, tm_["acs_t"][h:h + 1, :] - tm_["acs"][:, h:h + 1], -jnp.inf))
    t1 = jnp.exp(jnp.where(upper, tm_["acs_t"][h + 1:h + 2, :] - tm_["acs"][:, h + 1:h + 2], -jnp.inf))
    return jnp.concatenate([cbt * t0, cbt * t1], axis=0).astype(BF16)


def _block_diag(lo, xdt):
    return jnp.concatenate([jnp.where(lo, xdt, 0.0), jnp.where(lo, 0.0, xdt)], axis=0).astype(BF16)


def ssd_fwd(xbc, proj, dt_raw, dt_bias, a_log, d_full, norm_w, mixed, z_block, n_heads, comm=None):
    t = xbc.shape[0]
    q = SSD_CHUNK
    gn = SSD_GROUPS * SSD_STATE
    sw = n_heads * SSD_HEAD_DIM
    gw = sw // SSD_GROUPS
    n_pairs = n_heads // 2
    pairs_per_group = n_pairs // SSD_GROUPS
    nc = t // q
    bblk = sw // gn

    def body(xs_ref, b_ref, c_ref, dtr_ref, dtn_ref, bias_ref, a_ref, z_ref, dsk_ref, nw_ref, alias_ref,
             y_ref, sin_ref, m_ref, state, terms_ref):
        @pl.when(pl.program_id(0) == 0)
        def _():
            state[...] = jnp.zeros_like(state)

        tm_, keep_next = _prefetched_terms(pl.program_id(0), dtr_ref, dtn_ref, bias_ref, a_ref, n_heads, terms_ref)
        lo = tm_["lane"] < SSD_HEAD_DIM
        for g in range(SSD_GROUPS):
            gcols = slice(g * SSD_STATE, (g + 1) * SSD_STATE)
            bg = b_ref[:, gcols].astype(BF16)
            bg_t = b_ref[:, gcols].T.astype(BF16)
            cg = c_ref[:, gcols].astype(BF16)
            cb = lax.dot_general(cg, bg, NT, preferred_element_type=F32)
            for j in range(pairs_per_group):
                p = g * pairs_per_group + j
                h = 2 * p
                pcols = slice(p * LANES, (p + 1) * LANES)
                _, _, mcat = _pair_decay(tm_, cb, h)
                xdt = xs_ref[:, pcols] * _pair_cols(lo, tm_["dt"], h)
                ydiag = jnp.dot(mcat.astype(BF16), _block_diag(lo, xdt), preferred_element_type=F32)
                st = state[p]
                sin_ref[0, p] = st
                yoff = jnp.dot(cg, st.astype(BF16), preferred_element_type=F32) * _pair_cols(lo, tm_["eacs"], h)
                y_ref[:, pcols] = ydiag + yoff
                xw = (xdt * _pair_cols(lo, tm_["dstate"], h)).astype(BF16)
                state[p] = st * _pair_cols(lo, tm_["cdec"], h) + jnp.dot(bg_t, xw, preferred_element_type=F32)
        keep_next()
        for g in range(SSD_GROUPS):
            cols = slice(g * gw, (g + 1) * gw)
            blk = (y_ref[:, cols] + dsk_ref[:, cols] * xs_ref[:, cols]) * _silu(z_ref[:, cols])
            r = lax.rsqrt(jnp.mean(blk * blk, axis=-1, keepdims=True) + NORM_EPS)
            m_ref[:, cols] = (blk * r * nw_ref[:, cols]).astype(BF16)

    vec = pl.BlockSpec((1, LANES), lambda c: (0, 0))
    wide = pl.BlockSpec((1, sw), lambda c: (0, 0))
    return _call(
        body, (xbc, xbc, xbc, dt_raw, dt_raw, dt_bias, a_log, proj, d_full, norm_w, mixed), name="ssd_fwd", grid=(nc,),
        in_specs=[pl.BlockSpec((q, sw), lambda c: (c, 0)),
                  pl.BlockSpec((q, gn), lambda c: (c, bblk)),
                  pl.BlockSpec((q, gn), lambda c: (c, bblk + 1)),
                  pl.BlockSpec((q, LANES), lambda c: (c, 0)),
                  pl.BlockSpec((q, LANES), lambda c: (jnp.minimum(c + 1, nc - 1), 0)), vec, vec,
                  pl.BlockSpec((q, sw), lambda c: (c, z_block)), wide, wide, pl.BlockSpec(memory_space=pl.ANY)],
        out_specs=[pl.BlockSpec((q, sw), lambda c: (c, 0)),
                   pl.BlockSpec((1, n_pairs, SSD_STATE, LANES), lambda c: (c, 0, 0, 0)),
                   pl.BlockSpec((q, sw), lambda c: (c, 1))],
        out_shape=[SDS((t, sw), F32), SDS((nc, n_pairs, SSD_STATE, LANES), F32), SDS(mixed.shape, mixed.dtype)],
        scratch_shapes=[pltpu.VMEM((n_pairs, SSD_STATE, LANES), F32), pltpu.VMEM((2, len(_TERM_FIELDS), q, LANES), F32)],
        sem=("arbitrary",), comm=comm, aliases={10: 2})


def ssd_bwd(dmixed, y, xbc, proj, dt_raw, dt_bias, a_log, d_full, norm_w, s_in, dproj, z_block, n_heads, comm=None):
    t = xbc.shape[0]
    q = SSD_CHUNK
    gn = SSD_GROUPS * SSD_STATE
    sw = n_heads * SSD_HEAD_DIM
    gw = sw // SSD_GROUPS
    n_pairs = n_heads // 2
    pairs_per_group = n_pairs // SSD_GROUPS
    nc = t // q
    bblk = sw // gn

    def body(d3_ref, y_ref, xs_ref, b_ref, c_ref, dtr_ref, dtn_ref, bias_ref, a_ref, dsk_ref, sin_ref, z_ref, nw_ref, alias_ref,
             dxs_ref, db_ref, dc_ref, ddtr_ref, dbias_ref, dalog_ref, dz_ref, dnw_ref, dd_ref,
             dstate, tbuf, xbuf, rbuf, acc_a, acc_b, sel_ref, terms_ref, dy_ref, acc_d):
        i = pl.program_id(0)

        @pl.when(i == 0)
        def _():
            dstate[...] = jnp.zeros_like(dstate)
            rbuf[...] = jnp.zeros_like(rbuf)
            acc_a[...] = jnp.zeros_like(acc_a)
            acc_b[...] = jnp.zeros_like(acc_b)
            acc_d[...] = jnp.zeros_like(acc_d)
            dnw_ref[...] = jnp.zeros_like(dnw_ref)
            sel_ref[...] = _head_selector(sw, SSD_HEAD_DIM)

        for g in range(SSD_GROUPS):
            cols = slice(g * gw, (g + 1) * gw)
            xs = xs_ref[:, cols]
            zv = z_ref[:, cols]
            y1 = y_ref[:, cols] + dsk_ref[:, cols] * xs
            sz = _silu(zv)
            blk = y1 * sz
            r = lax.rsqrt(jnp.mean(blk * blk, axis=-1, keepdims=True) + NORM_EPS)
            n = blk * r
            dg = d3_ref[:, cols]
            dnw_ref[:, cols] += jnp.sum(dg * n, axis=0, keepdims=True)
            dy2 = _norm_bwd(dg * nw_ref[:, cols], n, r)
            dz_ref[:, cols] = (dy2 * y1 * _dsilu(zv)).astype(BF16)
            dy1 = dy2 * sz
            dy_ref[:, cols] = dy1
            acc_d[0:1, cols] += jnp.sum(dy1 * xs, axis=0, keepdims=True)

        tm_, keep_next = _prefetched_terms(i, dtr_ref, dtn_ref, bias_ref, a_ref, n_heads, terms_ref)
        lane = tm_["lane"]
        lo = lane < SSD_HEAD_DIM
        head_row = lax.broadcasted_iota(jnp.int32, (LANES, 1), 0)
        rows = jnp.zeros((q, LANES), F32)
        cols_t = jnp.zeros((LANES, q), F32)
        for g in range(SSD_GROUPS):
            gcols = slice(g * SSD_STATE, (g + 1) * SSD_STATE)
            bg = b_ref[:, gcols].astype(BF16)
            cg = c_ref[:, gcols].astype(BF16)
            cg_t = c_ref[:, gcols].T.astype(BF16)
            cb = lax.dot_general(cg, bg, NT, preferred_element_type=F32)
            cbt = lax.dot_general(bg, cg, NT, preferred_element_type=F32)
            dcb = jnp.zeros((q, q), F32)
            db_acc = jnp.zeros((q, SSD_STATE), F32)
            dc_acc = jnp.zeros((q, SSD_STATE), F32)
            for j in range(pairs_per_group):
                p = g * pairs_per_group + j
                h = 2 * p
                pcols = slice(p * LANES, (p + 1) * LANES)
                l0, l1, mcat = _pair_decay(tm_, cb, h)
                xp = xs_ref[:, pcols]
                dtp = _pair_cols(lo, tm_["dt"], h)
                xdt = xp * dtp
                xbd = _block_diag(lo, xdt)
                dyp = dy_ref[:, pcols]
                dyb = dyp.astype(BF16)
                dsb = _pair_cols(lo, tm_["dstate"], h)
                cdr = _pair_cols(lo, tm_["cdec"], h)
                eb = _pair_cols(lo, tm_["eacs"], h)
                st = sin_ref[0, p]
                stb = st.astype(BF16)
                dst = dstate[p]
                dstb = dst.astype(BF16)
                dye = (dyp * eb).astype(BF16)
                both = jnp.dot(_pair_decay_t(tm_, cbt, h), dyb, preferred_element_type=F32)
                dx_state = jnp.dot(bg, dstb, preferred_element_type=F32) * dsb
                dxdt = jnp.where(lo, both[:q], both[q:]) + dx_state
                dmcat = lax.dot_general(dyb, xbd, NT, preferred_element_type=F32)
                dcb = dcb + dmcat[:, :q] * l0 + dmcat[:, q:] * l1
                dseg = dmcat * mcat
                csum = jnp.sum(dseg, axis=0, keepdims=True)
                rows = (rows + jnp.where(lane == h, jnp.sum(dseg[:, :q], axis=1, keepdims=True), 0.0)
                        + jnp.where(lane == h + 1, jnp.sum(dseg[:, q:], axis=1, keepdims=True), 0.0))
                cols_t = (cols_t + jnp.where(head_row == h, csum[:, :q], 0.0)
                          + jnp.where(head_row == h + 1, csum[:, q:], 0.0))
                dc_acc = dc_acc + lax.dot_general(dye, stb, NT, preferred_element_type=F32)
                db_acc = db_acc + lax.dot_general((xdt * dsb).astype(BF16), dstb, NT, preferred_element_type=F32)
                yoff = jnp.dot(cg, stb, preferred_element_type=F32) * eb
                tbuf[:, pcols] = dyp * yoff - xdt * dx_state
                xbuf[:, pcols] = dxdt * xp
                rbuf[0:1, pcols] = (jnp.sum(xdt * dx_state, axis=0, keepdims=True)
                                    + cdr * jnp.sum(dst * st, axis=0, keepdims=True))
                dxs_ref[:, pcols] = dxdt * dtp + dyp * dsk_ref[:, pcols]
                dstate[p] = dst * cdr + jnp.dot(cg_t, dye, preferred_element_type=F32)
            dcbb = dcb.astype(BF16)
            dc_ref[:, gcols] = dc_acc + jnp.dot(dcbb, bg, preferred_element_type=F32)
            db_ref[:, gcols] = db_acc + lax.dot_general(dcbb, cg, TN, preferred_element_type=F32)

        sel = sel_ref[...]
        dacs = rows - cols_t.T + _split_dot(tbuf[...], sel)
        carry = _split_dot(rbuf[...], sel)[0:1]
        anti = jnp.logical_not(tm_["causal"]) | tm_["diag"]
        da = jnp.dot(anti.astype(F32), dacs, precision=HIGHEST, preferred_element_type=F32) + carry
        ddt = da * tm_["a"] + _split_dot(xbuf[...], sel)
        ddtr = jnp.where(tm_["lane"] < n_heads, ddt * jax.nn.sigmoid(tm_["pre"]), 0.0)
        ddtr_ref[...] = ddtr.astype(BF16)
        acc_b[...] += jnp.sum(ddtr, axis=0, keepdims=True)
        acc_a[...] += jnp.sum(da * tm_["dt"], axis=0, keepdims=True)
        keep_next()

        @pl.when(i == nc - 1)
        def _():
            dbias_ref[...] = acc_b[...]
            dalog_ref[...] = acc_a[...] * tm_["a"]
            dd_ref[...] = _split_dot(acc_d[...], sel)[0:1]

    vec = pl.BlockSpec((1, LANES), lambda i: (0, 0))
    full = pl.BlockSpec((1, sw), lambda i: (0, 0))
    wide = pl.BlockSpec((q, sw), lambda i: (nc - 1 - i, 0))
    return _call(
        body, (dmixed, y, xbc, xbc, xbc, dt_raw, dt_raw, dt_bias, a_log, d_full, s_in, proj, norm_w, dproj),
        name="ssd_bwd", grid=(nc,),
        in_specs=[pl.BlockSpec((q, sw), lambda i: (nc - 1 - i, 1)), wide, wide,
                  pl.BlockSpec((q, gn), lambda i: (nc - 1 - i, bblk)),
                  pl.BlockSpec((q, gn), lambda i: (nc - 1 - i, bblk + 1)),
                  pl.BlockSpec((q, LANES), lambda i: (nc - 1 - i, 0)),
                  pl.BlockSpec((q, LANES), lambda i: (jnp.maximum(nc - 2 - i, 0), 0)), vec, vec, full,
                  pl.BlockSpec((1, n_pairs, SSD_STATE, LANES), lambda i: (nc - 1 - i, 0, 0, 0)),
                  pl.BlockSpec((q, sw), lambda i: (nc - 1 - i, z_block)), full, pl.BlockSpec(memory_space=pl.ANY)],
        out_specs=[wide, pl.BlockSpec((q, gn), lambda i: (nc - 1 - i, 0)), pl.BlockSpec((q, gn), lambda i: (nc - 1 - i, 0)),
                   pl.BlockSpec((q, LANES), lambda i: (nc - 1 - i, 0)), vec, vec,
                   pl.BlockSpec((q, sw), lambda i: (nc - 1 - i, z_block)), full, vec],
        out_shape=[SDS((t, sw), F32), SDS((t, gn), F32), SDS((t, gn), F32), SDS((t, LANES), BF16),
                   SDS((1, LANES), F32), SDS((1, LANES), F32), SDS(dproj.shape, dproj.dtype), SDS((1, sw), F32),
                   SDS((1, LANES), F32)],
        scratch_shapes=[pltpu.VMEM((n_pairs, SSD_STATE, LANES), F32), pltpu.VMEM((q, sw), F32), pltpu.VMEM((q, sw), F32),
                        pltpu.VMEM((8, sw), F32), pltpu.VMEM((1, LANES), F32), pltpu.VMEM((1, LANES), F32),
                        pltpu.VMEM((sw, LANES), BF16), pltpu.VMEM((2, len(_TERM_FIELDS), q, LANES), F32),
                        pltpu.VMEM((q, sw), F32), pltpu.VMEM((8, sw), F32)],
        sem=("arbitrary",), comm=comm, aliases={13: 6})


GATE_BLOCK = 1
Z_BLOCK = 2
CONV_BLOCK = 2


def _tiles(t):
    mm = dict(in_proj=(min(1024, t), 1024), dt_proj=(min(512, t), LANES), out_proj=(min(512, t), 1024),
              d_mixed=(min(512, t), 2048), dh=(min(512, t), 3072), dw_out=(512, 1024), dw_main=(1024, 1024),
              dw_dt=(512, LANES))
    return min(256, t), mm


def _place():
    x, y, c = lax.axis_index("x"), lax.axis_index("y"), lax.axis_index("c")
    return x, y, c, [(1 - x, y), (x, 1 - y), (1 - x, 1 - y)]


def gather_spread(shards, layer, rows=None, carry=None):
    n = len(shards)

    def make(ins, outs, ss, rs):
        x, y, c, chips = _place()
        mine = 4 * x + 2 * y + c
        peers = [(x, y, 1 - c)] + [(px, py, c) for px, py in chips]
        sends, locals_, arrivals = [], [], []
        for a in range(n):
            def place(ref, idx):
                return ref.at[idx] if rows is None else ref.at[idx, pl.ds(rows[0], rows[1])]

            src = place(ins[a], layer)
            locals_.append(pltpu.make_async_copy(src, place(outs[a], mine), ss.at[5 * a + 4]))
            for j, (px, py, pc) in enumerate(peers):
                sends.append(_remote(src, place(outs[a], mine), ss, rs, 5 * a + j, (px, py, pc)))
                arrivals.append(_remote(src, place(outs[a], 4 * px + 2 * py + pc), ss, rs, 5 * a + j, (px, py, pc)))
        return sends, locals_, arrivals

    return Comm(list(shards) + list(carry or []), [SDS((N_DEV,) + s.shape[1:], s.dtype) for s in shards],
                {n + a: a for a in range(n)} if carry else {}, 5 * n, make)


def gather_pass_on(gathered):
    def make(ins, outs, ss, rs):
        x, y, c, chips = _place()
        sends, arrivals = [], []
        for a in range(len(outs)):
            for j, (px, py) in enumerate(chips):
                blk, other = 4 * px + 2 * py + c, 4 * px + 2 * py + (1 - c)
                sends.append(_remote(outs[a].at[blk], outs[a].at[blk], ss, rs, 3 * a + j, (x, y, 1 - c)))
                arrivals.append(_remote(outs[a].at[other], outs[a].at[other], ss, rs, 3 * a + j, (x, y, 1 - c)))
        return sends, [], arrivals

    return Comm(gathered, [SDS(g.shape, g.dtype) for g in gathered], {a: a for a in range(len(gathered))},
                3 * len(gathered), make)


def sibling_swap(sends_):
    def make(ins, outs, ss, rs):
        x, y, c, _ = _place()
        cps = [_remote(ins[a], outs[a], ss, rs, a, (x, y, 1 - c)) for a in range(len(ins))]
        return cps, [], cps

    return Comm(sends_, [SDS(s.shape, s.dtype) for s in sends_], {}, len(sends_), make)


def chips_scatter(slabs, rows=None, carry=None):
    n = len(slabs)

    def make(ins, outs, ss, rs):
        x, y, c, chips = _place()
        mychip = 2 * x + y
        sends, arrivals = [], []

        def part(ref, slot):
            return ref.at[slot] if rows is None else ref.at[slot, pl.ds(rows[0], rows[1])]

        for a in range(n):
            for j, (px, py) in enumerate(chips):
                to_there = lax.rem(2 * px + py - mychip + 4, 4) - 1
                from_here = lax.rem(mychip - 2 * px - py + 4, 4) - 1
                sends.append(_remote(part(ins[a], to_there), part(outs[a], from_here), ss, rs, 3 * a + j, (px, py, c)))
                arrivals.append(_remote(part(ins[a], to_there), part(outs[a], to_there), ss, rs, 3 * a + j, (px, py, c)))
        return sends, [], arrivals

    return Comm(list(slabs) + list(carry or []), [SDS(s.shape, s.dtype) for s in slabs],
                {n + a: a for a in range(n)} if carry else {}, 3 * n, make)


def comm_only(comm, name):
    def body():
        pass

    return _call(body, (), name=name, grid=(), in_specs=[], out_specs=[], out_shape=[], comm=comm)[1]


W_IN_GATHER_EIGHTHS = (3, 1, 2, 2)


def layer_fwd(x, p, layer=0, shards=None, finish=None, nxt=False, first=False):
    t = x.shape[0]
    tm, mm = _tiles(t)
    n_heads = p["d_full"].shape[1] // SSD_HEAD_DIM
    travel = shards is not None
    nxt, first = nxt and travel, first and travel
    rows = shards["in"][0].shape[1] if travel else 0
    cuts = [0]
    for eighths in W_IN_GATHER_EIGHTHS:
        cuts.append(cuts[-1] + rows * eighths // 8)
    assert cuts[-1] == rows

    def next_w_in(part, carry):
        return gather_spread(shards["in"], layer + 1, rows=(cuts[part], cuts[part + 1] - cuts[part]), carry=carry) if nxt else None

    h, r_pre = rms_fwd(x, p["pre_w"], tm)
    proj, got = mm_nn(h, p["w_main"], F32, *mm["in_proj"], "in_proj", merge_comms([
        gather_spread(shards["small"], layer) if first else None,
        gather_spread(shards["out"], layer) if travel else None, next_w_in(0, None)]))
    n_small = len(shards["small"]) if first else 0
    got_small, got_out, got_in = got[:n_small], got[n_small:n_small + 1], got[n_small + 1:]
    dt_raw, got_small = mm_nn(h, p["w_dt"], F32, *mm["dt_proj"], "dt_proj", gather_pass_on(got_small) if first else None)
    if first:
        p = dict(p, **finish["small"](got_small))
    mixed = pool_fwd(proj, p["mixw"], p["pscale"], tm)
    xbc, got_in = conv_fwd(proj, p["conv_w"], p["conv_b"], CONV_BLOCK, tm, next_w_in(1, got_in))
    (y, s_in, mixed), got = ssd_fwd(xbc, proj, dt_raw, p["dt_bias"], p["a_log"], p["d_full"], p["norm_w"], mixed, Z_BLOCK,
                                    n_heads, merge_comms([gather_pass_on(got_out) if travel else None, next_w_in(2, got_in)]))
    if travel:
        p = dict(p, **finish["out"](got[:1]))
    out, got = mm_nn(mixed, p["w_out"], F32, *mm["out_proj"], "out_proj",
                     merge_comms([next_w_in(3, got[1:]), gather_spread(shards["small"], layer + 1)]) if nxt else None)
    (x_next, r_post), gathered = post_fwd(out, x, p["post_w"], tm, gather_pass_on(got) if nxt else None)
    return x_next, dict(x=x, h=h, r_pre=r_pre, proj=proj, dt_raw=dt_raw, xbc=xbc, y=y, s_in=s_in, mixed=mixed,
                        out=out, r_post=r_post), gathered, p


def _pair_sums(own, got):
    return [pair_sum(o, r, min(256, o.shape[1]), "pair_sum") for o, r in zip(own, got)]


def layer_bwd(g, s, p, split_in, split_rest, pending=None, last=False):
    t = g.shape[0]
    tm, mm = _tiles(t)
    d = g.shape[1]
    n_heads = p["d_full"].shape[1] // SSD_HEAD_DIM
    d_out, d_post = post_bwd(g, s["out"], s["r_post"], p["post_w"], tm)
    dmixed, got_sib = mm_nt(d_out, p["w_out"], F32, *mm["d_mixed"], d, "d_mixed", sibling_swap(pending[1]) if pending else None)
    chip_sums = _pair_sums(pending[0], got_sib) if pending else []
    rows_in = chip_sums[0].shape[1] if pending else 0
    early = rows_in * 3 // 8 if last else 0
    dw_out, got_early = mm_tn(s["mixed"], d_out, *mm["dw_out"], "dw_out",
                              chips_scatter(chip_sums[:1], rows=(0, early)) if pending and last else None)
    dproj, dq, d_pscale, d_mixw = pool_bwd_a(dmixed, s["proj"], p["mixw"], p["pscale"], tm)
    dproj = pool_bwd_b(dq, dproj, tm)
    own_rest, send_rest = split_rest(dw_out, d_mixw)
    cut_in = rows_in if last else rows_in // 2
    (dxs, db, dc, ddtr, d_dtb, d_alog, dproj, d_norm, d_dskip), got = ssd_bwd(
        dmixed, s["y"], s["xbc"], s["proj"], s["dt_raw"], p["dt_bias"], p["a_log"], p["d_full"], p["norm_w"], s["s_in"],
        dproj, Z_BLOCK, n_heads,
        merge_comms([chips_scatter(chip_sums[:1], rows=(early, cut_in - early), carry=got_early or None) if pending else None,
                     sibling_swap(send_rest) if last else None]))
    got_first, my_sib_rest = (got[:1], got[1:]) if pending else ([], got)
    rest_comm = chips_scatter(chip_sums[1:]) if pending else None
    (dpre, d_convw, d_convb), got_rest = conv_bwd_a(dxs, db, dc, s["proj"], p["conv_w"], p["conv_b"], CONV_BLOCK, tm,
                                                    rest_comm if last else None)
    dproj = conv_bwd_b(dpre, p["conv_w"], dproj, CONV_BLOCK, tm)
    dw_main, got = mm_tn(s["h"], dproj, *mm["dw_main"], "dw_main",
                         chips_scatter(_pair_sums(own_rest, my_sib_rest)) if last else rest_comm)
    got_rest, my_chips_rest = (got_rest, got) if last else (got, [])
    dw_dt, _ = mm_tn(s["h"], ddtr, *mm["dw_dt"], "dw_dt")
    own_in, send_in = split_in(dw_main, dw_dt)
    my_sib_in = comm_only(sibling_swap(send_in), "grads_to_sibling") if last else []
    if last:
        my_sums = _pair_sums(own_in, my_sib_in)
        rows_own = my_sums[0].shape[1]
        cut_own = rows_own * 13 // 16
        dh_comm = chips_scatter(my_sums, rows=(0, cut_own))
    else:
        dh_comm = chips_scatter(chip_sums[:1], rows=(cut_in, rows_in - cut_in), carry=got_first) if pending else None
    dh, got = mm_nt(dproj, p["w_main"], F32, mm["dh"][0], d, mm["dh"][1], "dh_main", dh_comm, extra=(ddtr, p["w_dt"]))
    my_chips_in, got_first = (got, got_first) if last else ([], got if pending else got_first)
    (gx, d_pre), got = rms_bwd(dh, s["x"], s["r_pre"], p["pre_w"], g, tm,
                               chips_scatter(my_sums, rows=(cut_own, rows_own - cut_own), carry=my_chips_in) if last else None)
    my_chips_in = got if last else my_chips_in
    small = dict(pre_w=d_pre, pscale=d_pscale, conv_w=d_convw, conv_b=d_convb, dt_bias=d_dtb, a_log=d_alog,
                 d_skip=d_dskip, norm_w=d_norm, post_w=d_post)
    done = [(got_sib, got_first + got_rest)] if pending else [None]
    if last:
        done.append((my_sib_in + my_sib_rest, my_chips_in + my_chips_rest))
    return gx, small, (own_in + own_rest, send_in + send_rest), done


def _two_level_gather(x_refs, out_slots, send_sems, recv_sems, local_sems):
    x, y, c, chips = _place()
    me, sibling = (x, y, c), (x, y, 1 - c)
    n = len(x_refs)

    def copy(a, k, block, to, src=None):
        return pltpu.make_async_remote_copy(
            src_ref=out_slots[a](*block) if src is None else src, dst_ref=out_slots[a](*block),
            send_sem=send_sems.at[7 * a + k], recv_sem=recv_sems.at[7 * a + k], device_id=to, device_id_type=MESH)

    mine = [pltpu.make_async_copy(x_refs[a], out_slots[a](*me), local_sems.at[a]) for a in range(n)]
    for cp in mine:
        cp.start()
    first = []
    for a in range(n):
        first.append(copy(a, 0, me, sibling, src=x_refs[a]))
        first += [copy(a, 1 + j, me, (*chip, c), src=x_refs[a]) for j, chip in enumerate(chips)]
    for cp in first:
        cp.start()
    passed = []
    for j, chip in enumerate(chips):
        for a in range(n):
            copy(a, 1 + j, (*chip, c), me).wait_recv()
            fwd = copy(a, 4 + j, (*chip, c), sibling)
            fwd.start()
            passed.append(fwd)
    for a in range(n):
        copy(a, 0, sibling, me).wait_recv()
        for j, chip in enumerate(chips):
            copy(a, 4 + j, (*chip, 1 - c), me).wait_recv()
    for cp in first + passed:
        cp.wait_send()
    for cp in mine:
        cp.wait()


def all_gather_hbm(shards, name):
    n = len(shards)

    def body(*refs):
        x_refs, out_refs = refs[:n], refs[n:2 * n]
        send_sems, recv_sems, local_sems = refs[2 * n:]
        slots = [lambda px, py, pc, o=o: o.at[:, 4 * px + 2 * py + pc] for o in out_refs]
        _two_level_gather(x_refs, slots, send_sems, recv_sems, local_sems)

    hbm = pl.BlockSpec(memory_space=pl.ANY)
    return pl.pallas_call(
        body, name=name,
        out_shape=[SDS((s.shape[0], N_DEV) + s.shape[1:], s.dtype) for s in shards],
        in_specs=[hbm] * n, out_specs=[hbm] * n,
        scratch_shapes=[pltpu.SemaphoreType.DMA((7 * n,)), pltpu.SemaphoreType.DMA((7 * n,)), pltpu.SemaphoreType.DMA((n,))],
    )(*shards)


def all_gather_vmem(block, name):
    r, c_ = block.shape

    def body(x_ref, out_ref, send_sems, recv_sems, local_sems):
        _two_level_gather([x_ref], [lambda px, py, pc: out_ref.at[4 * px + 2 * py + pc]], send_sems, recv_sems, local_sems)

    return pl.pallas_call(
        body, name=name, out_shape=SDS((N_DEV, r, c_), block.dtype),
        in_specs=[pl.BlockSpec(memory_space=pltpu.VMEM)], out_specs=pl.BlockSpec(memory_space=pltpu.VMEM),
        scratch_shapes=[pltpu.SemaphoreType.DMA((7,)), pltpu.SemaphoreType.DMA((7,)), pltpu.SemaphoreType.DMA((1,))],
        compiler_params=_params(),
    )(block)


def _block_tiles(cols):
    base = [(cols * i) // LANES for i in range(N_DEV)]
    ends = [-((-cols * (i + 1)) // LANES) for i in range(N_DEV)]
    return base, ends, max(e - b for b, e in zip(base, ends))


def _my_lane_offset(cols):
    me = 4 * lax.axis_index("x") + 2 * lax.axis_index("y") + lax.axis_index("c")
    return lax.rem(cols * me, LANES)


def shift_cast(w, tr):
    nl, r, cols = w.shape
    width = _block_tiles(cols)[2] * LANES

    def body(x_ref, o_ref, pad):
        pad[:, width - LANES:] = jnp.zeros((tr, LANES), F32)
        pad[:, :cols] = x_ref[...]
        o_ref[...] = pltpu.roll(pad[...], _my_lane_offset(cols), 1).astype(BF16)

    assert width - LANES <= cols
    return pl.pallas_call(
        body, name="shift_cast", grid=(nl, r // tr),
        in_specs=[pl.BlockSpec((pl.Squeezed(), tr, cols), lambda l, i: (l, i, 0))],
        out_specs=pl.BlockSpec((pl.Squeezed(), tr, width), lambda l, i: (l, i, 0)),
        out_shape=SDS((nl, r, width), BF16), scratch_shapes=[pltpu.VMEM((tr, width), F32)],
        compiler_params=_params(("arbitrary", "arbitrary")))(w)


def assemble_w_in(blocks, cols, n_tail, tr):
    _, r, width = blocks.shape
    base, ends, _ = _block_tiles(cols)
    total = ends[-1]
    main_tiles = (N_DEV * cols - n_tail) // LANES
    assert main_tiles == total - 1 and (N_DEV * cols - n_tail) % LANES == 0

    def body(b_ref, main_ref, tail_ref):
        for tile in range(total):
            parts = [b_ref[i, :, (tile - base[i]) * LANES:(tile - base[i] + 1) * LANES]
                     for i in range(N_DEV) if base[i] <= tile < ends[i]]
            val = parts[0] if len(parts) == 1 else parts[0] + parts[1]
            if tile < main_tiles:
                main_ref[:, tile * LANES:(tile + 1) * LANES] = val
            else:
                tail_ref[...] = val

    return pl.pallas_call(
        body, name="assemble_w_in", grid=(r // tr,),
        in_specs=[pl.BlockSpec((N_DEV, tr, width), lambda i: (0, i, 0))],
        out_specs=[pl.BlockSpec((tr, main_tiles * LANES), lambda i: (i, 0)), pl.BlockSpec((tr, LANES), lambda i: (i, 0))],
        out_shape=[SDS((r, main_tiles * LANES), blocks.dtype), SDS((r, LANES), blocks.dtype)],
        compiler_params=_params(("arbitrary",)),
    )(blocks)


def grad_blocks(dw_main, dw_tail, cols, tr):
    r = dw_main.shape[0]
    base, _, tpb = _block_tiles(cols)
    width = tpb * LANES

    def body(m_ref, t_ref, own_ref, send_ref):
        cat = jnp.concatenate([m_ref[...], t_ref[...]], axis=1)
        south = lax.axis_index("c") == 0
        for k in range(N_DEV // 2):
            a = cat[:, base[2 * k] * LANES:base[2 * k] * LANES + width]
            b = cat[:, base[2 * k + 1] * LANES:base[2 * k + 1] * LANES + width]
            own_ref[k] = jnp.where(south, a, b)
            send_ref[k] = jnp.where(south, b, a).astype(BF16)

    return pl.pallas_call(
        body, name="grad_blocks", grid=(r // tr,),
        in_specs=[pl.BlockSpec((tr, dw_main.shape[1]), lambda i: (i, 0)), pl.BlockSpec((tr, LANES), lambda i: (i, 0))],
        out_specs=[pl.BlockSpec((N_DEV // 2, tr, width), lambda i: (0, i, 0))] * 2,
        out_shape=[SDS((N_DEV // 2, r, width), F32), SDS((N_DEV // 2, r, width), BF16)],
        compiler_params=_params(("arbitrary",)),
    )(dw_main, dw_tail)


def _adamw(w, g, m, v):
    m = ADAM_B1 * m + (1.0 - ADAM_B1) * g
    v = ADAM_B2 * v + (1.0 - ADAM_B2) * jnp.square(g)
    m_hat = m / (1.0 - ADAM_B1 ** ADAM_STEP)
    v_hat = v / (1.0 - ADAM_B2 ** ADAM_STEP)
    delta = -ADAM_LR * (m_hat / (jnp.sqrt(v_hat) + ADAM_EPS) + ADAM_WD * w)
    return delta, m, v


def _my_chip():
    return 2 * lax.axis_index("x") + lax.axis_index("y")


def pair_sum(own, got, tr, name):
    k, r, c_ = own.shape
    others = lax.rem(_my_chip() + 1 + jnp.arange(k - 1, dtype=jnp.int32), k)

    def body(others_ref, a_ref, b_ref, o_ref):
        o_ref[...] = (a_ref[...] + b_ref[...].astype(F32)).astype(BF16)

    src = pl.BlockSpec((pl.Squeezed(), tr, c_), lambda s, i, oth: (oth[s], i, 0))
    return pl.pallas_call(
        body, name=name, out_shape=SDS((k - 1, r, c_), BF16),
        grid_spec=pltpu.PrefetchScalarGridSpec(
            num_scalar_prefetch=1, grid=(k - 1, r // tr), in_specs=[src, src],
            out_specs=pl.BlockSpec((pl.Squeezed(), tr, c_), lambda s, i, oth: (s, i, 0))),
        compiler_params=_params(("arbitrary", "arbitrary")),
    )(others, own, got)


def reduce_adam(own, got_sibling, got_chips, w, m, v, prev, layer, tr, name, shifted=False):
    nl, r, cols = w.shape
    c_ = own.shape[-1]
    n_scratch = 1 if shifted else 0
    chip = jnp.reshape(_my_chip(), (1,)).astype(jnp.int32)

    def body(chip_ref, own_ref, sib_ref, c0_ref, c1_ref, c2_ref, w_ref, m_ref, v_ref, *rest):
        g_ref, d_ref, nm_ref, nv_ref = rest[len(rest) - n_scratch - 4:len(rest) - n_scratch]
        g = (own_ref[...] + sib_ref[...].astype(F32) + c0_ref[...].astype(F32) + c1_ref[...].astype(F32)
             + c2_ref[...].astype(F32))
        if shifted:
            rest[-1][...] = pltpu.roll(g, c_ - _my_lane_offset(cols), 1)
            g = rest[-1][:, :cols]
        delta, nm, nv = _adamw(w_ref[...], g, m_ref[...], v_ref[...])
        g_ref[...] = g
        d_ref[...] = delta
        nm_ref[...] = nm
        nv_ref[...] = nv

    mine = pl.BlockSpec((pl.Squeezed(), tr, c_), lambda i, ch: (ch[0], i, 0))
    lay = pl.BlockSpec((pl.Squeezed(), tr, cols), lambda i, ch: (layer, i, 0))
    chips = [pl.BlockSpec((pl.Squeezed(), tr, c_), lambda i, ch, s=s: (s, i, 0)) for s in range(3)]
    in_specs = [mine, mine] + chips + [lay, lay, lay]
    args = [chip, own, got_sibling, got_chips, got_chips, got_chips, w, m, v]
    aliases = {}
    if prev is not None:
        in_specs += [pl.BlockSpec(memory_space=pl.ANY)] * 4
        aliases = {len(args) + k: k for k in range(4)}
        args += list(prev)
    return pl.pallas_call(
        body, name=name, out_shape=[SDS((nl, r, cols), F32)] * 4, input_output_aliases=aliases,
        grid_spec=pltpu.PrefetchScalarGridSpec(
            num_scalar_prefetch=1, grid=(r // tr,), in_specs=in_specs, out_specs=[lay] * 4,
            scratch_shapes=[pltpu.VMEM((tr, c_), F32)] * n_scratch),
        compiler_params=_params(("arbitrary",)),
    )(*args)


def sum_devices(packs):
    n, r, c_ = packs.shape

    def body(p_ref, o_ref):
        acc = p_ref[0]
        for k in range(1, n):
            acc = acc + p_ref[k]
        o_ref[...] = acc

    return pl.pallas_call(body, name="sum_devices", out_shape=SDS((r, c_), F32), compiler_params=_params())(packs)


def adam_small(w, g, m, v):
    def body(w_ref, g_ref, m_ref, v_ref, d_ref, nm_ref, nv_ref):
        delta, nm, nv = _adamw(w_ref[...], g_ref[...], m_ref[...], v_ref[...])
        d_ref[...] = delta
        nm_ref[...] = nm
        nv_ref[...] = nv

    return pl.pallas_call(body, name="adam_small", out_shape=[SDS(w.shape, F32)] * 3, compiler_params=_params())(w, g, m, v)


SMALL = ("pre_norm_w", "pool_scale", "conv_b", "dt_bias", "a_log", "d_skip", "_pad", "ssd_norm_w", "post_norm_w", "conv_w")


def _pack(parts):
    flat = jnp.concatenate([parts[k] for k in SMALL], axis=1).reshape(-1, LANES)
    return jnp.pad(flat, ((0, (-flat.shape[0]) % 8), (0, 0)))


def _unpack(pack, sizes, nl):
    total = sum(sizes[k] for k in SMALL)
    flat = pack[: nl * total // LANES].reshape(nl, total)
    out, o = {}, 0
    for k in SMALL:
        out[k] = flat[:, o:o + sizes[k]]
        o += sizes[k]
    return out


def kernel(x, pre_norm_w, w_in, pool_mix_w, pool_scale, conv_w, conv_b, dt_bias, a_log, d_skip, ssd_norm_w, w_out, post_norm_w, loss_target, m_pre_norm_w, m_w_in, m_pool_mix_w, m_pool_scale, m_conv_w, m_conv_b, m_dt_bias, m_a_log, m_d_skip, m_ssd_norm_w, m_w_out, m_post_norm_w, v_pre_norm_w, v_w_in, v_pool_mix_w, v_pool_scale, v_conv_w, v_conv_b, v_dt_bias, v_a_log, v_d_skip, v_ssd_norm_w, v_w_out, v_post_norm_w):
    cx, cy, cc = lax.axis_index("x"), lax.axis_index("y"), lax.axis_index("c")
    me = 4 * cx + 2 * cy + cc
    mychip = 2 * cx + cy
    nl, d, cols = w_in.shape
    t = x.shape[1]
    n_heads = a_log.shape[1]
    sw = n_heads * SSD_HEAD_DIM
    pw = pool_scale.shape[1]
    cd = conv_b.shape[1]
    ng, gsh, gw = pool_mix_w.shape[1:]
    e_main = N_DEV * cols - n_heads
    assert x.shape[0] == 1 and pw == sw and cd == sw + 2 * SSD_GROUPS * SSD_STATE and e_main == 2 * pw + sw + cd
    assert 2 * pw + sw == CONV_BLOCK * cd and n_heads <= LANES and t % SSD_CHUNK == 0 and gsh * N_DEV == gw
    tm, _ = _tiles(t)

    shards = {"in": [shift_cast(w_in, tm)], "out": [w_out.astype(BF16)], "small": [pool_mix_w.astype(BF16), conv_w]}
    pad_h = ((0, 0), (0, LANES - n_heads))

    def params_a(l, g_in):
        w_main, w_dt = assemble_w_in(g_in, cols, n_heads, tm)
        return dict(pre_w=pre_norm_w[l:l + 1], w_main=w_main, w_dt=w_dt, pscale=pool_scale[l:l + 1], conv_b=conv_b[l:l + 1],
                    dt_bias=jnp.pad(dt_bias[l:l + 1], pad_h), a_log=jnp.pad(a_log[l:l + 1], pad_h),
                    d_full=jnp.repeat(d_skip[l:l + 1], SSD_HEAD_DIM, axis=1), norm_w=ssd_norm_w[l:l + 1],
                    post_w=post_norm_w[l:l + 1])

    finish = {"small": lambda got: dict(mixw=got[0].transpose(1, 0, 2, 3).reshape(ng, gw, gw),
                                        conv_w=got[1].transpose(1, 0, 2).reshape(CONV_WIDTH, cd)),
              "out": lambda got: dict(w_out=got[0].reshape(N_DEV * w_out.shape[1], d))}

    xs = x[0]
    saved, params = [], []
    p = params_a(0, all_gather_hbm([shards["in"][0][:1]], "gather_w_in")[0][0])
    for l in range(nl):
        xs, s, gathered, p = layer_fwd(xs, p, l, shards, finish, nxt=l + 1 < nl, first=l == 0)
        saved.append(s)
        params.append(p)
        if l + 1 < nl:
            p = dict(params_a(l + 1, gathered[0]), **finish["small"](gathered[1:]))
    loss_part, g = loss_grad(xs, loss_target[0], tm)
    loss = lax.psum(loss_part[0, 0], ("x", "y", "c"))

    big = {"w_in": (w_in, m_w_in, v_w_in), "w_out": (w_out, m_w_out, v_w_out),
           "pool_mix_w": tuple(a.reshape(nl, ng * gsh, gw) for a in (pool_mix_w, m_pool_mix_w, v_pool_mix_w))}
    names = list(big)
    big_out = {k: None for k in big}
    small_g = [None] * nl

    def apply(layer, own, got_sib, got_chips):
        for k, o, gs_, gc in zip(names, own, got_sib, got_chips):
            wk, mk, vk = big[k]
            big_out[k] = reduce_adam(o, gs_, gc, wk, mk, vk, big_out[k], layer, min(256, wk.shape[1]), "reduce_adam_" + k,
                                     shifted=(k == "w_in"))

    def split_in(dw_main, dw_dt):
        own, send = grad_blocks(dw_main, dw_dt, cols, min(128, d))
        return [own], [send]

    def split_rest(dw_out, d_mixw):
        halves = [lambda ci: lax.dynamic_index_in_dim(dw_out.reshape(4, 2, -1, d), ci, 1, keepdims=False),
                  lambda ci: lax.dynamic_index_in_dim(
                      d_mixw.reshape(ng, 4, 2, gsh, gw), ci, 2, keepdims=False).transpose(1, 0, 2, 3).reshape(4, ng * gsh, gw)]
        return [h(cc) for h in halves], [h(1 - cc).astype(BF16) for h in halves]

    pending = None
    for l in reversed(range(nl)):
        g, gr, mine, done = layer_bwd(g, saved[l], params[l], split_in, split_rest, pending, last=(l == 0))
        if pending is not None:
            apply(l + 1, pending[0], *done[0])
        if l == 0:
            apply(0, mine[0], *done[1])
        pending = mine
        small_g[l] = dict(pre_norm_w=gr["pre_w"], pool_scale=gr["pscale"], conv_b=gr["conv_b"], dt_bias=gr["dt_bias"][:, :n_heads],
                          a_log=gr["a_log"][:, :n_heads], d_skip=gr["d_skip"][:, :n_heads], _pad=jnp.zeros((1, LANES - 3 * n_heads), F32),
                          ssd_norm_w=gr["norm_w"], post_norm_w=gr["post_w"], conv_w=gr["conv_w"].reshape(1, CONV_WIDTH * cd))

    sizes = {k: small_g[0][k].shape[1] for k in SMALL}
    gsum = sum_devices(all_gather_vmem(_pack({k: jnp.concatenate([sg[k] for sg in small_g], axis=0) for k in SMALL}),
                                       "gather_small_grads"))
    gs = _unpack(gsum, sizes, nl)
    csh = conv_w.shape[2]
    gs["conv_w"] = lax.dynamic_slice_in_dim(gs["conv_w"].reshape(nl, CONV_WIDTH, cd), me * csh, csh, axis=2).reshape(nl, -1)
    lsizes = dict(sizes, conv_w=CONV_WIDTH * csh)
    zpad = jnp.zeros((nl, sizes["_pad"]), F32)

    def local(pre, scale, cb, dtb, al, dsk, nw, post, cw):
        return _pack(dict(pre_norm_w=pre, pool_scale=scale, conv_b=cb, dt_bias=dtb, a_log=al, d_skip=dsk, _pad=zpad,
                          ssd_norm_w=nw, post_norm_w=post, conv_w=cw.reshape(nl, -1)))

    wp = local(pre_norm_w, pool_scale, conv_b, dt_bias, a_log, d_skip, ssd_norm_w, post_norm_w, conv_w)
    mp = local(m_pre_norm_w, m_pool_scale, m_conv_b, m_dt_bias, m_a_log, m_d_skip, m_ssd_norm_w, m_post_norm_w, m_conv_w)
    vp = local(v_pre_norm_w, v_pool_scale, v_conv_b, v_dt_bias, v_a_log, v_d_skip, v_ssd_norm_w, v_post_norm_w, v_conv_w)
    small_out = [gs] + [_unpack(o, lsizes, nl) for o in adam_small(wp, _pack(gs), mp, vp)]

    def leaf(kind, name):
        if name in big:
            return big_out[name][kind].reshape(big[name][0].shape if name != "pool_mix_w" else pool_mix_w.shape)
        val = small_out[kind][name]
        return val.reshape(conv_w.shape) if name == "conv_w" else val

    order = ("pre_norm_w", "w_in", "pool_mix_w", "pool_scale", "conv_w", "conv_b", "dt_bias", "a_log", "d_skip",
             "ssd_norm_w", "w_out", "post_norm_w")
    return (loss, g[None]) + tuple(leaf(kind, name) for kind in range(4) for name in order)
```

```python
import jax
import jax.numpy as jnp
from jax import lax
from jax.experimental import pallas as pl
from jax.experimental.pallas import tpu as pltpu

F32 = jnp.float32
BF16 = jnp.bfloat16
SDS = jax.ShapeDtypeStruct
MESH = pl.DeviceIdType.MESH
HIGHEST = lax.Precision.HIGHEST

NORM_EPS = 1e-6
POOL_WINDOWS = (2, 4, 8, 16)
POOL_HALO = 16
CONV_WIDTH = 4
CONV_HALO = 8
SSD_CHUNK = 128
SSD_HEAD_DIM = 64
SSD_STATE = 128
SSD_GROUPS = 4
LANES = 128
N_DEV = 8

ADAM_LR = 0.001
ADAM_B1 = 0.9
ADAM_B2 = 0.999
ADAM_EPS = 1e-08
ADAM_WD = 0.01
ADAM_STEP = 10

VMEM_LIMIT = 56 * 1024 * 1024

NT = (((1,), (1,)), ((), ()))
TN = (((0,), (0,)), ((), ()))


def _params(sem=None):
    kw = dict(vmem_limit_bytes=VMEM_LIMIT)
    if sem is not None:
        kw["dimension_semantics"] = sem
    return pltpu.CompilerParams(**kw)


def _silu(v):
    return v * jax.nn.sigmoid(v)


def _dsilu(v):
    s = jax.nn.sigmoid(v)
    return s * (1.0 + v * (1.0 - s))


def _split_dot(v, sel):
    hi = v.astype(BF16)
    lo = (v - hi.astype(F32)).astype(BF16)
    return (jnp.dot(hi, sel, preferred_element_type=F32) + jnp.dot(lo, sel, preferred_element_type=F32))


def _head_selector(width, per):
    ch = lax.broadcasted_iota(jnp.int32, (width, LANES), 0)
    hd = lax.broadcasted_iota(jnp.int32, (width, LANES), 1)
    return jnp.where((ch >= hd * per) & (ch < (hd + 1) * per), 1.0, 0.0).astype(BF16)


class Comm:
    def __init__(self, inputs, out_shapes, aliases, n_sems, make):
        self.inputs, self.out_shapes, self.aliases, self.n_sems, self.make = list(inputs), list(out_shapes), dict(aliases), n_sems, make


def _remote(src, dst, send_sems, recv_sems, k, peer):
    return pltpu.make_async_remote_copy(src_ref=src, dst_ref=dst, send_sem=send_sems.at[k], recv_sem=recv_sems.at[k],
                                        device_id=peer, device_id_type=MESH)


class _SemRange:
    def __init__(self, sems, start):
        self.sems, self.start = sems, start

    @property
    def at(self):
        return self

    def __getitem__(self, k):
        return self.sems.at[self.start + k]


def merge_comms(comms):
    comms = [c for c in comms if c is not None]
    if len(comms) <= 1:
        return comms[0] if comms else None
    aliases, i_off, o_off = {}, 0, 0
    for c in comms:
        aliases.update({i_off + k: o_off + v for k, v in c.aliases.items()})
        i_off, o_off = i_off + len(c.inputs), o_off + len(c.out_shapes)

    def make(ins, outs, ss, rs):
        sends, locals_, arrivals, i0, o0, s0 = [], [], [], 0, 0, 0
        for c in comms:
            s, l, a = c.make(ins[i0:i0 + len(c.inputs)], outs[o0:o0 + len(c.out_shapes)], _SemRange(ss, s0), _SemRange(rs, s0))
            sends, locals_, arrivals = sends + s, locals_ + l, arrivals + a
            i0, o0, s0 = i0 + len(c.inputs), o0 + len(c.out_shapes), s0 + c.n_sems
        return sends, locals_, arrivals

    return Comm(sum((c.inputs for c in comms), []), sum((c.out_shapes for c in comms), []), aliases,
                sum(c.n_sems for c in comms), make)


def _call(body, args, *, name, grid, in_specs, out_specs, out_shape, scratch_shapes=(), sem=None, comm=None, aliases=None):
    in_specs, out_specs, out_shape = list(in_specs), list(out_specs), list(out_shape)
    aliases = dict(aliases or {})
    if comm is None:
        outs = pl.pallas_call(body, name=name, grid=grid, in_specs=in_specs, out_specs=out_specs, out_shape=out_shape,
                              scratch_shapes=list(scratch_shapes), input_output_aliases=aliases,
                              compiler_params=_params(sem))(*args)
        return list(outs), []
    ni, no, nci, nco, ns = len(in_specs), len(out_specs), len(comm.inputs), len(comm.out_shapes), len(scratch_shapes)
    hbm = pl.BlockSpec(memory_space=pl.ANY)

    def hosted(*refs):
        ins, cins = refs[:ni], refs[ni:ni + nci]
        outs, couts = refs[ni + nci:ni + nci + no], refs[ni + nci + no:ni + nci + no + nco]
        scratch = refs[ni + nci + no + nco:]
        sends, locals_, arrivals = comm.make(cins, couts, scratch[ns], scratch[ns + 1])
        first = last = None if grid else True
        for axis, extent in enumerate(grid):
            pid = pl.program_id(axis)
            first = (pid == 0) if first is None else first & (pid == 0)
            last = (pid == extent - 1) if last is None else last & (pid == extent - 1)

        @pl.when(first)
        def _():
            for cp in locals_ + sends:
                cp.start()

        body(*ins, *outs, *scratch[:ns])

        @pl.when(last)
        def _():
            for cp in arrivals:
                cp.wait_recv()
            for cp in sends:
                cp.wait_send()
            for cp in locals_:
                cp.wait()

    outs = pl.pallas_call(
        hosted, name=name, grid=grid, in_specs=in_specs + [hbm] * nci, out_specs=out_specs + [hbm] * nco,
        out_shape=out_shape + comm.out_shapes,
        scratch_shapes=list(scratch_shapes) + [pltpu.SemaphoreType.DMA((comm.n_sems,)), pltpu.SemaphoreType.DMA((comm.n_sems,))],
        input_output_aliases={**aliases, **{ni + k: no + v for k, v in comm.aliases.items()}},
        compiler_params=_params(sem),
    )(*args, *comm.inputs)
    return list(outs[:no]), list(outs[no:])


def rms_fwd(x, w, tm):
    t, d = x.shape

    def body(x_ref, w_ref, h_ref, r_ref):
        xv = x_ref[...]
        r = lax.rsqrt(jnp.mean(xv * xv, axis=-1, keepdims=True) + NORM_EPS)
        h_ref[...] = (xv * r * w_ref[...]).astype(BF16)
        r_ref[...] = r

    return pl.pallas_call(
        body, name="rms_fwd", grid=(t // tm,),
        in_specs=[pl.BlockSpec((tm, d), lambda i: (i, 0)), pl.BlockSpec((1, d), lambda i: (0, 0))],
        out_specs=[pl.BlockSpec((tm, d), lambda i: (i, 0)), pl.BlockSpec((tm, 1), lambda i: (i, 0))],
        out_shape=[SDS((t, d), BF16), SDS((t, 1), F32)],
        compiler_params=_params(("arbitrary",)),
    )(x, w)


def post_fwd(out, x, w, tm, comm=None, next_w=None):
    t, d = x.shape
    fused = next_w is not None

    def body(o_ref, x_ref, w_ref, *rest):
        y_ref, r_ref = rest[fused:fused + 2]
        ov = o_ref[...]
        r = lax.rsqrt(jnp.mean(ov * ov, axis=-1, keepdims=True) + NORM_EPS)
        y = x_ref[...] + ov * r * w_ref[...]
        y_ref[...] = y
        r_ref[...] = r
        if fused:
            rn = lax.rsqrt(jnp.mean(y * y, axis=-1, keepdims=True) + NORM_EPS)
            rest[3][...] = (y * rn * rest[0][...]).astype(BF16)
            rest[4][...] = rn

    row = pl.BlockSpec((tm, d), lambda i: (i, 0))
    vec = pl.BlockSpec((1, d), lambda i: (0, 0))
    col = pl.BlockSpec((tm, 1), lambda i: (i, 0))
    return _call(
        body, (out, x, w) + ((next_w,) if fused else ()), name="post_fwd", grid=(t // tm,),
        in_specs=[row, row, vec] + [vec] * fused, out_specs=[row, col] + [row, col] * fused,
        out_shape=[SDS((t, d), F32), SDS((t, 1), F32)] + [SDS((t, d), BF16), SDS((t, 1), F32)] * fused,
        sem=("arbitrary",), comm=comm)


def _norm_bwd(g_n, n, r):
    return r * (g_n - n * jnp.mean(g_n * n, axis=-1, keepdims=True))


def post_bwd(g, out, r, w, tm):
    t, d = g.shape

    def body(g_ref, o_ref, r_ref, w_ref, do_ref, dw_ref):
        i = pl.program_id(0)
        gv = g_ref[...]
        rv = r_ref[...]
        n = o_ref[...] * rv
        part = jnp.sum(gv * n, axis=0, keepdims=True)

        @pl.when(i == 0)
        def _():
            dw_ref[...] = part

        @pl.when(i > 0)
        def _():
            dw_ref[...] += part

        do_ref[...] = _norm_bwd(gv * w_ref[...], n, rv).astype(BF16)

    return pl.pallas_call(
        body, name="post_bwd", grid=(t // tm,),
        in_specs=[pl.BlockSpec((tm, d), lambda i: (i, 0)), pl.BlockSpec((tm, d), lambda i: (i, 0)),
                  pl.BlockSpec((tm, 1), lambda i: (i, 0)), pl.BlockSpec((1, d), lambda i: (0, 0))],
        out_specs=[pl.BlockSpec((tm, d), lambda i: (i, 0)), pl.BlockSpec((1, d), lambda i: (0, 0))],
        out_shape=[SDS((t, d), BF16), SDS((1, d), F32)],
        compiler_params=_params(("arbitrary",)),
    )(g, out, r, w)


def rms_bwd(dh, x, r, w, g, tm, comm=None, below=None):
    t, d = x.shape
    fused = below is not None

    def body(a_ref, x_ref, r_ref, w_ref, g_ref, *rest):
        gx_ref, dw_ref = rest[3 * fused:3 * fused + 2]
        i = pl.program_id(0)
        dh = a_ref[...]
        rv = r_ref[...]
        n = x_ref[...] * rv
        gx = g_ref[...] + _norm_bwd(dh * w_ref[...], n, rv)
        gx_ref[...] = gx
        parts = [(dw_ref, jnp.sum(dh * n, axis=0, keepdims=True))]
        if fused:
            o_ref, rp_ref, pw_ref = rest[:3]
            do_ref, dpw_ref = rest[5:7]
            rp = rp_ref[...]
            nb = o_ref[...] * rp
            do_ref[...] = _norm_bwd(gx * pw_ref[...], nb, rp).astype(BF16)
            parts.append((dpw_ref, jnp.sum(gx * nb, axis=0, keepdims=True)))

        @pl.when(i == 0)
        def _():
            for ref, part in parts:
                ref[...] = part

        @pl.when(i > 0)
        def _():
            for ref, part in parts:
                ref[...] += part

    row = pl.BlockSpec((tm, d), lambda i: (i, 0))
    vec = pl.BlockSpec((1, d), lambda i: (0, 0))
    col = pl.BlockSpec((tm, 1), lambda i: (i, 0))
    return _call(
        body, (dh, x, r, w, g) + tuple(below or ()), name="rms_bwd", grid=(t // tm,),
        in_specs=[row, row, col, vec, row] + [row, col, vec] * fused, out_specs=[row, vec] + [row, vec] * fused,
        out_shape=[SDS((t, d), F32), SDS((1, d), F32)] + [SDS((t, d), BF16), SDS((1, d), F32)] * fused,
        sem=("arbitrary",), comm=comm)


def loss_grad(y, target, tm):
    t, d = y.shape

    def body(y_ref, t_ref, l_ref, g_ref):
        i = pl.program_id(0)
        err = y_ref[...] - t_ref[...]
        g_ref[...] = err / d
        part = 0.5 * jnp.sum(jnp.mean(err * err, axis=-1, keepdims=True), axis=0, keepdims=True)

        @pl.when(i == 0)
        def _():
            l_ref[...] = part

        @pl.when(i > 0)
        def _():
            l_ref[...] += part

    row = pl.BlockSpec((tm, d), lambda i: (i, 0))
    return pl.pallas_call(
        body, name="loss_grad", grid=(t // tm,), in_specs=[row, row],
        out_specs=[pl.BlockSpec((1, 1), lambda i: (0, 0)), row],
        out_shape=[SDS((1, 1), F32), SDS((t, d), F32)],
        compiler_params=_params(("arbitrary",)),
    )(y, target)


def mm_nn(a, b, out_dtype, tm, tn, name, comm=None):
    m, k = a.shape
    n = b.shape[1]

    def body(a_ref, b_ref, o_ref):
        o_ref[...] = jnp.dot(a_ref[...], b_ref[...], preferred_element_type=F32).astype(out_dtype)

    outs, couts = _call(
        body, (a, b), name=name, grid=(n // tn, m // tm),
        in_specs=[pl.BlockSpec((tm, k), lambda j, i: (i, 0)), pl.BlockSpec((k, tn), lambda j, i: (0, j))],
        out_specs=[pl.BlockSpec((tm, tn), lambda j, i: (i, j))],
        out_shape=[SDS((m, n), out_dtype)], sem=("arbitrary", "arbitrary"), comm=comm)
    return outs[0], couts


def mm_nt(a, b, out_dtype, tm, tn, tk, name, comm=None, extra=None):
    m, k = a.shape
    n = b.shape[0]
    nk = k // tk

    def body(a_ref, b_ref, *rest):
        o_ref, acc_ref = rest[-2:]
        kk = pl.program_id(2)
        part = lax.dot_general(a_ref[...], b_ref[...], NT, preferred_element_type=F32)
        if nk == 1:
            if extra is not None:
                part = part + lax.dot_general(rest[0][...], rest[1][...], NT, preferred_element_type=F32)
            o_ref[...] = part.astype(out_dtype)
        else:
            @pl.when(kk == 0)
            def _():
                if extra is None:
                    acc_ref[...] = part
                else:
                    acc_ref[...] = part + lax.dot_general(rest[0][...], rest[1][...], NT, preferred_element_type=F32)

            @pl.when(kk > 0)
            def _():
                acc_ref[...] += part

            @pl.when(kk == nk - 1)
            def _():
                o_ref[...] = acc_ref[...].astype(out_dtype)

    more_specs = [] if extra is None else [pl.BlockSpec((tm, extra[0].shape[1]), lambda i, j, kk: (i, 0)),
                                           pl.BlockSpec((tn, extra[1].shape[1]), lambda i, j, kk: (j, 0))]
    outs, couts = _call(
        body, (a, b) + tuple(extra or ()), name=name, grid=(m // tm, n // tn, nk),
        in_specs=[pl.BlockSpec((tm, tk), lambda i, j, kk: (i, kk)), pl.BlockSpec((tn, tk), lambda i, j, kk: (j, kk))] + more_specs,
        out_specs=[pl.BlockSpec((tm, tn), lambda i, j, kk: (i, j))],
        out_shape=[SDS((m, n), out_dtype)],
        scratch_shapes=[pltpu.VMEM((tm, tn) if nk > 1 else (8, LANES), F32)],
        sem=("arbitrary", "arbitrary", "arbitrary"), comm=comm)
    return outs[0], couts


def mm_tn(a, b, tm, tn, name, comm=None):
    t, m = a.shape
    n = b.shape[1]

    def body(a_ref, b_ref, o_ref):
        o_ref[...] = lax.dot_general(a_ref[...], b_ref[...], TN, preferred_element_type=F32)

    outs, couts = _call(
        body, (a, b), name=name, grid=(m // tm, n // tn),
        in_specs=[pl.BlockSpec((t, tm), lambda i, j: (0, i)), pl.BlockSpec((t, tn), lambda i, j: (0, j))],
        out_specs=[pl.BlockSpec((tm, tn), lambda i, j: (i, j))],
        out_shape=[SDS((m, n), F32)], sem=("arbitrary", "arbitrary"), comm=comm)
    return outs[0], couts


def _window_sums(ext, n_rows, lookahead):
    def sh(v, k):
        return pltpu.roll(v, (n_rows - k) if lookahead else k, 0)
    s2 = ext + sh(ext, 1)
    s4 = s2 + sh(s2, 2)
    s8 = s4 + sh(s4, 4)
    s16 = s8 + sh(s8, 8)
    return (s2, s4, s8, s16)


def _pool_counts(i, tm, w):
    tpos = i * tm + lax.broadcasted_iota(jnp.int32, (tm, 1), 0)
    return jnp.minimum(tpos + 1, w).astype(F32)


def _pooled(uc_ref, up_ref, i, tm):
    cur = uc_ref[...]
    prev = jnp.where(i > 0, up_ref[...], 0.0)
    ext = jnp.concatenate([prev, cur], axis=0)
    return cur, _window_sums(ext, tm + POOL_HALO, False)


def pool_fwd(proj, mixw, scale, tm):
    t = proj.shape[0]
    pw = scale.shape[1]
    gw = pw // len(POOL_WINDOWS)
    nh = tm // POOL_HALO

    def body(uc_ref, up_ref, g_ref, w_ref, s_ref, o_ref):
        i = pl.program_id(0)
        cur, sums = _pooled(uc_ref, up_ref, i, tm)
        for g, w in enumerate(POOL_WINDOWS):
            cols = slice(g * gw, (g + 1) * gw)
            pooled = sums[g][POOL_HALO:, cols] / _pool_counts(i, tm, w) - cur[:, cols]
            mixed = jnp.dot(pooled.astype(BF16), w_ref[g], preferred_element_type=F32)
            o_ref[:, cols] = (mixed * s_ref[:, cols] * _silu(g_ref[:, cols])).astype(BF16)

    return pl.pallas_call(
        body, name="pool_fwd", grid=(t // tm,),
        in_specs=[pl.BlockSpec((tm, pw), lambda i: (i, 0)),
                  pl.BlockSpec((POOL_HALO, pw), lambda i: (jnp.maximum(i * nh - 1, 0), 0)),
                  pl.BlockSpec((tm, pw), lambda i: (i, 1)),
                  pl.BlockSpec(mixw.shape, lambda i: (0, 0, 0)),
                  pl.BlockSpec((1, pw), lambda i: (0, 0))],
        out_specs=pl.BlockSpec((tm, pw), lambda i: (i, 0)),
        out_shape=SDS((t, 2 * pw), BF16),
        compiler_params=_params(("arbitrary",)),
    )(proj, proj, proj, mixw, scale)


def pool_bwd_a(dmixed, proj, mixw, scale, tm):
    t, e = proj.shape
    pw = scale.shape[1]
    ng = len(POOL_WINDOWS)
    gw = pw // ng
    nh = tm // POOL_HALO

    def body(dy_ref, uc_ref, up_ref, g_ref, w_ref, s_ref, dg_ref, dq_ref, ds_ref, dw_ref):
        i = pl.program_id(0)

        @pl.when(i == 0)
        def _():
            ds_ref[...] = jnp.zeros_like(ds_ref)
            dw_ref[...] = jnp.zeros_like(dw_ref)

        cur, sums = _pooled(uc_ref, up_ref, i, tm)
        for g, w in enumerate(POOL_WINDOWS):
            cols = slice(g * gw, (g + 1) * gw)
            cnt = _pool_counts(i, tm, w)
            pooled = (sums[g][POOL_HALO:, cols] / cnt - cur[:, cols]).astype(BF16)
            mixed = jnp.dot(pooled, w_ref[g], preferred_element_type=F32)
            gate = g_ref[:, cols]
            dy = dy_ref[:, cols]
            sc = s_ref[:, cols]
            dg_ref[:, cols] = (dy * mixed * sc * _dsilu(gate)).astype(BF16)
            ds = dy * _silu(gate)
            ds_ref[:, cols] += jnp.sum(ds * mixed, axis=0, keepdims=True)
            dmix = (ds * sc).astype(BF16)
            dw_ref[g] += lax.dot_general(pooled, dmix, TN, preferred_element_type=F32)
            dq_ref[:, cols] = lax.dot_general(dmix, w_ref[g], NT, preferred_element_type=F32) / cnt

    return pl.pallas_call(
        body, name="pool_bwd_a", grid=(t // tm,),
        in_specs=[pl.BlockSpec((tm, pw), lambda i: (i, 0)),
                  pl.BlockSpec((tm, pw), lambda i: (i, 0)),
                  pl.BlockSpec((POOL_HALO, pw), lambda i: (jnp.maximum(i * nh - 1, 0), 0)),
                  pl.BlockSpec((tm, pw), lambda i: (i, 1)),
                  pl.BlockSpec(mixw.shape, lambda i: (0, 0, 0)),
                  pl.BlockSpec((1, pw), lambda i: (0, 0))],
        out_specs=[pl.BlockSpec((tm, pw), lambda i: (i, 1)),
                   pl.BlockSpec((tm, pw), lambda i: (i, 0)),
                   pl.BlockSpec((1, pw), lambda i: (0, 0)),
                   pl.BlockSpec((ng, gw, gw), lambda i: (0, 0, 0))],
        out_shape=[SDS((t, e), BF16), SDS((t, pw), F32), SDS((1, pw), F32), SDS((ng, gw, gw), F32)],
        compiler_params=_params(("arbitrary",)),
    )(dmixed, proj, proj, proj, mixw, scale)


def pool_bwd_b(dq, dproj, tm):
    t, pw = dq.shape
    gw = pw // len(POOL_WINDOWS)
    nh = tm // POOL_HALO
    nt = t // tm

    def body(c_ref, n_ref, alias_ref, o_ref):
        i = pl.program_id(0)
        cur = c_ref[...]
        nxt = jnp.where(i < nt - 1, n_ref[...], 0.0)
        sums = _window_sums(jnp.concatenate([cur, nxt], axis=0), tm + POOL_HALO, True)
        for g, w in enumerate(POOL_WINDOWS):
            cols = slice(g * gw, (g + 1) * gw)
            o_ref[:, cols] = (sums[g][:tm, cols] - cur[:, cols] * _pool_counts(i, tm, w)).astype(BF16)

    return pl.pallas_call(
        body, name="pool_bwd_b", grid=(nt,),
        in_specs=[pl.BlockSpec((tm, pw), lambda i: (i, 0)),
                  pl.BlockSpec((POOL_HALO, pw), lambda i: (jnp.minimum((i + 1) * nh, t // POOL_HALO - 1), 0)),
                  pl.BlockSpec(memory_space=pl.ANY)],
        out_specs=pl.BlockSpec((tm, pw), lambda i: (i, 0)),
        out_shape=SDS(dproj.shape, dproj.dtype),
        input_output_aliases={2: 0},
        compiler_params=_params(("arbitrary",)),
    )(dq, dq, dproj)


ELEMENTWISE_LANE_CHUNK = 256


def _lane_chunks(width):
    return [slice(c, c + ELEMENTWISE_LANE_CHUNK) for c in range(0, width, ELEMENTWISE_LANE_CHUNK)]


def _conv_pre(xc_ref, xp_ref, w_ref, b_ref, i, cols):
    cur = xc_ref[:, cols]
    prev = jnp.where(i > 0, xp_ref[:, cols], 0.0)
    ext = jnp.concatenate([prev, cur], axis=0)
    taps = [pltpu.roll(ext, CONV_WIDTH - 1 - k, 0)[CONV_HALO:] for k in range(CONV_WIDTH - 1)] + [cur]
    pre = b_ref[:, cols]
    for k in range(CONV_WIDTH):
        pre = pre + w_ref[k:k + 1, cols] * taps[k]
    return pre, taps


def conv_fwd(proj, conv_w, conv_b, col_block, tm, comm=None):
    t = proj.shape[0]
    cd = conv_b.shape[1]
    nh = tm // CONV_HALO

    def body(xc_ref, xp_ref, w_ref, b_ref, o_ref):
        i = pl.program_id(0)
        for cols in _lane_chunks(cd):
            pre, _ = _conv_pre(xc_ref, xp_ref, w_ref, b_ref, i, cols)
            o_ref[:, cols] = _silu(pre)

    outs, couts = _call(
        body, (proj, proj, conv_w, conv_b), name="conv_fwd", grid=(t // tm,),
        in_specs=[pl.BlockSpec((tm, cd), lambda i: (i, col_block)),
                  pl.BlockSpec((CONV_HALO, cd), lambda i: (jnp.maximum(i * nh - 1, 0), col_block)),
                  pl.BlockSpec((CONV_WIDTH, cd), lambda i: (0, 0)),
                  pl.BlockSpec((1, cd), lambda i: (0, 0))],
        out_specs=[pl.BlockSpec((tm, cd), lambda i: (i, 0))],
        out_shape=[SDS((t, cd), F32)], sem=("arbitrary",), comm=comm)
    return outs[0], couts


def conv_bwd_a(dxs, db, dc, proj, conv_w, conv_b, col_block, tm, comm=None):
    t = proj.shape[0]
    cd = conv_b.shape[1]
    sw = dxs.shape[1]
    gn = db.shape[1]
    nh = tm // CONV_HALO

    def body(dx_ref, db_ref, dc_ref, xc_ref, xp_ref, w_ref, b_ref, dp_ref, dw_ref, dbias_ref):
        i = pl.program_id(0)

        @pl.when(i == 0)
        def _():
            dw_ref[...] = jnp.zeros_like(dw_ref)
            dbias_ref[...] = jnp.zeros_like(dbias_ref)

        for cols in _lane_chunks(cd):
            pre, taps = _conv_pre(xc_ref, xp_ref, w_ref, b_ref, i, cols)
            if cols.start < sw:
                dact = dx_ref[:, cols]
            elif cols.start < sw + gn:
                dact = db_ref[:, cols.start - sw:cols.stop - sw]
            else:
                dact = dc_ref[:, cols.start - sw - gn:cols.stop - sw - gn]
            dpre = dact * _dsilu(pre)
            dp_ref[:, cols] = dpre
            dbias_ref[:, cols] += jnp.sum(dpre, axis=0, keepdims=True)
            for k in range(CONV_WIDTH):
                dw_ref[k:k + 1, cols] += jnp.sum(dpre * taps[k], axis=0, keepdims=True)

    return _call(
        body, (dxs, db, dc, proj, proj, conv_w, conv_b), name="conv_bwd_a", grid=(t // tm,),
        in_specs=[pl.BlockSpec((tm, sw), lambda i: (i, 0)), pl.BlockSpec((tm, gn), lambda i: (i, 0)),
                  pl.BlockSpec((tm, gn), lambda i: (i, 0)),
                  pl.BlockSpec((tm, cd), lambda i: (i, col_block)),
                  pl.BlockSpec((CONV_HALO, cd), lambda i: (jnp.maximum(i * nh - 1, 0), col_block)),
                  pl.BlockSpec((CONV_WIDTH, cd), lambda i: (0, 0)),
                  pl.BlockSpec((1, cd), lambda i: (0, 0))],
        out_specs=[pl.BlockSpec((tm, cd), lambda i: (i, 0)),
                   pl.BlockSpec((CONV_WIDTH, cd), lambda i: (0, 0)),
                   pl.BlockSpec((1, cd), lambda i: (0, 0))],
        out_shape=[SDS((t, cd), F32), SDS((CONV_WIDTH, cd), F32), SDS((1, cd), F32)],
        sem=("arbitrary",), comm=comm)


def conv_bwd_b(dpre, conv_w, dproj, col_block, tm):
    t, cd = dpre.shape
    nh = tm // CONV_HALO
    nt = t // tm

    def body(c_ref, n_ref, w_ref, alias_ref, o_ref):
        i = pl.program_id(0)
        n = tm + CONV_HALO
        for cols in _lane_chunks(cd):
            cur = c_ref[:, cols]
            nxt = jnp.where(i < nt - 1, n_ref[:, cols], 0.0)
            ext = jnp.concatenate([cur, nxt], axis=0)
            acc = w_ref[CONV_WIDTH - 1:CONV_WIDTH, cols] * cur
            for k in range(CONV_WIDTH - 1):
                acc = acc + w_ref[k:k + 1, cols] * pltpu.roll(ext, n - (CONV_WIDTH - 1 - k), 0)[:tm]
            o_ref[:, cols] = acc.astype(BF16)

    return pl.pallas_call(
        body, name="conv_bwd_b", grid=(nt,),
        in_specs=[pl.BlockSpec((tm, cd), lambda i: (i, 0)),
                  pl.BlockSpec((CONV_HALO, cd), lambda i: (jnp.minimum((i + 1) * nh, t // CONV_HALO - 1), 0)),
                  pl.BlockSpec((CONV_WIDTH, cd), lambda i: (0, 0)),
                  pl.BlockSpec(memory_space=pl.ANY)],
        out_specs=pl.BlockSpec((tm, cd), lambda i: (i, col_block)),
        out_shape=SDS(dproj.shape, dproj.dtype),
        input_output_aliases={3: 0},
        compiler_params=_params(("arbitrary",)),
    )(dpre, dpre, conv_w, dproj)


def _softplus(v):
    return jnp.maximum(v, 0.0) + jnp.log(1.0 + jnp.exp(-jnp.abs(v)))


def _ssd_chunk_terms(dtr_ref, bias_ref, a_ref, n_heads):
    q = SSD_CHUNK
    lane = lax.broadcasted_iota(jnp.int32, (1, LANES), 1)
    pre = dtr_ref[...] + bias_ref[...]
    dt = jnp.where(lane < n_heads, _softplus(pre), 0.0)
    a = jnp.where(lane < n_heads, -jnp.exp(a_ref[...]), 0.0)
    row = lax.broadcasted_iota(jnp.int32, (q, q), 0)
    col = lax.broadcasted_iota(jnp.int32, (q, q), 1)
    causal = row >= col
    acs = jnp.dot(causal.astype(F32), dt * a, precision=HIGHEST, preferred_element_type=F32)
    last = acs[q - 1:q, :]
    return dict(pre=pre, dt=dt, a=a, acs=acs, acs_t=acs.T, eacs=jnp.exp(acs), dstate=jnp.exp(last - acs),
                cdec=jnp.exp(last), causal=causal, diag=row == col, lane=lane)


_TERM_FIELDS = ("pre", "dt", "acs", "acs_t", "eacs", "dstate", "cdec")


def _prefetched_terms(step, dtr_ref, dtn_ref, bias_ref, a_ref, n_heads, terms_ref):
    q = SSD_CHUNK

    def store(slot, tm_):
        for f, name in enumerate(_TERM_FIELDS):
            terms_ref[slot, f] = jnp.broadcast_to(tm_[name], (q, LANES))

    @pl.when(step == 0)
    def _():
        store(0, _ssd_chunk_terms(dtr_ref, bias_ref, a_ref, n_heads))

    slot = lax.rem(step, 2)
    nxt = _ssd_chunk_terms(dtn_ref, bias_ref, a_ref, n_heads)
    tm_ = dict(nxt, **{name: terms_ref[slot, f] for f, name in enumerate(_TERM_FIELDS)})
    tm_["cdec"] = tm_["cdec"][0:1]
    return tm_, lambda: store(1 - slot, nxt)


def _pair_cols(lo, v, h):
    if v.shape[0] < 8:
        return jnp.where(lo, v[:, h:h + 1], v[:, h + 1:h + 2])
    idx = jnp.broadcast_to(jnp.where(lo, h, h + 1).astype(jnp.int32), v.shape)
    return jnp.take_along_axis(v, idx, axis=1, mode="promise_in_bounds")


def _pair_decay(tm_, cb, h):
    l0 = jnp.exp(jnp.where(tm_["causal"], tm_["acs"][:, h:h + 1] - tm_["acs_t"][h:h + 1, :], -jnp.inf))
    l1 = jnp.exp(jnp.where(tm_["causal"], tm_["acs"][:, h + 1:h + 2] - tm_["acs_t"][h + 1:h + 2, :], -jnp.inf))
    return l0, l1, jnp.concatenate([cb * l0, cb * l1], axis=1)


def _pair_decay_t(tm_, cbt, h):
    upper = jnp.logical_not(tm_["causal"]) | tm_["diag"]
    t0 = jnp.exp(jnp.where(upper, tm_["acs_t"][h:h + 1, :] - tm_["acs"][:, h:h + 1], -jnp.inf))
    t1 = jnp.exp(jnp.where(upper, tm_["acs_t"][h + 1:h + 2, :] - tm_["acs"][:, h + 1:h + 2], -jnp.inf))
    return jnp.concatenate([cbt * t0, cbt * t1], axis=0).astype(BF16)


def _block_diag(lo, xdt):
    return jnp.concatenate([jnp.where(lo, xdt, 0.0), jnp.where(lo, 0.0, xdt)], axis=0).astype(BF16)


def ssd_fwd(xbc, proj, dt_raw, dt_bias, a_log, d_full, norm_w, mixed, z_block, n_heads, comm=None):
    t = xbc.shape[0]
    q = SSD_CHUNK
    gn = SSD_GROUPS * SSD_STATE
    sw = n_heads * SSD_HEAD_DIM
    gw = sw // SSD_GROUPS
    n_pairs = n_heads // 2
    pairs_per_group = n_pairs // SSD_GROUPS
    nc = t // q
    bblk = sw // gn

    def body(xs_ref, b_ref, c_ref, dtr_ref, dtn_ref, bias_ref, a_ref, z_ref, dsk_ref, nw_ref, alias_ref,
             y_ref, sin_ref, m_ref, state, terms_ref):
        @pl.when(pl.program_id(0) == 0)
        def _():
            state[...] = jnp.zeros_like(state)

        tm_, keep_next = _prefetched_terms(pl.program_id(0), dtr_ref, dtn_ref, bias_ref, a_ref, n_heads, terms_ref)
        lo = tm_["lane"] < SSD_HEAD_DIM
        for g in range(SSD_GROUPS):
            gcols = slice(g * SSD_STATE, (g + 1) * SSD_STATE)
            bg = b_ref[:, gcols].astype(BF16)
            bg_t = b_ref[:, gcols].T.astype(BF16)
            cg = c_ref[:, gcols].astype(BF16)
            cb = lax.dot_general(cg, bg, NT, preferred_element_type=F32)
            for j in range(pairs_per_group):
                p = g * pairs_per_group + j
                h = 2 * p
                pcols = slice(p * LANES, (p + 1) * LANES)
                _, _, mcat = _pair_decay(tm_, cb, h)
                xdt = xs_ref[:, pcols] * _pair_cols(lo, tm_["dt"], h)
                ydiag = jnp.dot(mcat.astype(BF16), _block_diag(lo, xdt), preferred_element_type=F32)
                st = state[p]
                sin_ref[0, p] = st
                yoff = jnp.dot(cg, st.astype(BF16), preferred_element_type=F32) * _pair_cols(lo, tm_["eacs"], h)
                y_ref[:, pcols] = ydiag + yoff
                xw = (xdt * _pair_cols(lo, tm_["dstate"], h)).astype(BF16)
                state[p] = st * _pair_cols(lo, tm_["cdec"], h) + jnp.dot(bg_t, xw, preferred_element_type=F32)
        keep_next()
        for g in range(SSD_GROUPS):
            cols = slice(g * gw, (g + 1) * gw)
            blk = (y_ref[:, cols] + dsk_ref[:, cols] * xs_ref[:, cols]) * _silu(z_ref[:, cols])
            r = lax.rsqrt(jnp.mean(blk * blk, axis=-1, keepdims=True) + NORM_EPS)
            m_ref[:, cols] = (blk * r * nw_ref[:, cols]).astype(BF16)

    vec = pl.BlockSpec((1, LANES), lambda c: (0, 0))
    wide = pl.BlockSpec((1, sw), lambda c: (0, 0))
    return _call(
        body, (xbc, xbc, xbc, dt_raw, dt_raw, dt_bias, a_log, proj, d_full, norm_w, mixed), name="ssd_fwd", grid=(nc,),
        in_specs=[pl.BlockSpec((q, sw), lambda c: (c, 0)),
                  pl.BlockSpec((q, gn), lambda c: (c, bblk)),
                  pl.BlockSpec((q, gn), lambda c: (c, bblk + 1)),
                  pl.BlockSpec((q, LANES), lambda c: (c, 0)),
                  pl.BlockSpec((q, LANES), lambda c: (jnp.minimum(c + 1, nc - 1), 0)), vec, vec,
                  pl.BlockSpec((q, sw), lambda c: (c, z_block)), wide, wide, pl.BlockSpec(memory_space=pl.ANY)],
        out_specs=[pl.BlockSpec((q, sw), lambda c: (c, 0)),
                   pl.BlockSpec((1, n_pairs, SSD_STATE, LANES), lambda c: (c, 0, 0, 0)),
                   pl.BlockSpec((q, sw), lambda c: (c, 1))],
        out_shape=[SDS((t, sw), F32), SDS((nc, n_pairs, SSD_STATE, LANES), F32), SDS(mixed.shape, mixed.dtype)],
        scratch_shapes=[pltpu.VMEM((n_pairs, SSD_STATE, LANES), F32), pltpu.VMEM((2, len(_TERM_FIELDS), q, LANES), F32)],
        sem=("arbitrary",), comm=comm, aliases={10: 2})


def ssd_bwd(dmixed, y, xbc, proj, dt_raw, dt_bias, a_log, d_full, norm_w, s_in, dproj, z_block, n_heads, comm=None):
    t = xbc.shape[0]
    q = SSD_CHUNK
    gn = SSD_GROUPS * SSD_STATE
    sw = n_heads * SSD_HEAD_DIM
    gw = sw // SSD_GROUPS
    n_pairs = n_heads // 2
    pairs_per_group = n_pairs // SSD_GROUPS
    nc = t // q
    bblk = sw // gn

    def body(d3_ref, y_ref, xs_ref, b_ref, c_ref, dtr_ref, dtn_ref, bias_ref, a_ref, dsk_ref, sin_ref, z_ref, nw_ref, alias_ref,
             dxs_ref, db_ref, dc_ref, ddtr_ref, dbias_ref, dalog_ref, dz_ref, dnw_ref, dd_ref,
             dstate, tbuf, xbuf, rbuf, acc_a, acc_b, sel_ref, terms_ref, dy_ref, acc_d):
        i = pl.program_id(0)

        @pl.when(i == 0)
        def _():
            dstate[...] = jnp.zeros_like(dstate)
            rbuf[...] = jnp.zeros_like(rbuf)
            acc_a[...] = jnp.zeros_like(acc_a)
            acc_b[...] = jnp.zeros_like(acc_b)
            acc_d[...] = jnp.zeros_like(acc_d)
            dnw_ref[...] = jnp.zeros_like(dnw_ref)
            sel_ref[...] = _head_selector(sw, SSD_HEAD_DIM)

        for g in range(SSD_GROUPS):
            cols = slice(g * gw, (g + 1) * gw)
            xs = xs_ref[:, cols]
            zv = z_ref[:, cols]
            y1 = y_ref[:, cols] + dsk_ref[:, cols] * xs
            sz = _silu(zv)
            blk = y1 * sz
            r = lax.rsqrt(jnp.mean(blk * blk, axis=-1, keepdims=True) + NORM_EPS)
            n = blk * r
            dg = d3_ref[:, cols]
            dnw_ref[:, cols] += jnp.sum(dg * n, axis=0, keepdims=True)
            dy2 = _norm_bwd(dg * nw_ref[:, cols], n, r)
            dz_ref[:, cols] = (dy2 * y1 * _dsilu(zv)).astype(BF16)
            dy1 = dy2 * sz
            dy_ref[:, cols] = dy1
            acc_d[0:1, cols] += jnp.sum(dy1 * xs, axis=0, keepdims=True)

        tm_, keep_next = _prefetched_terms(i, dtr_ref, dtn_ref, bias_ref, a_ref, n_heads, terms_ref)
        lane = tm_["lane"]
        lo = lane < SSD_HEAD_DIM
        head_row = lax.broadcasted_iota(jnp.int32, (LANES, 1), 0)
        rows = jnp.zeros((q, LANES), F32)
        cols_t = jnp.zeros((LANES, q), F32)
        for g in range(SSD_GROUPS):
            gcols = slice(g * SSD_STATE, (g + 1) * SSD_STATE)
            bg = b_ref[:, gcols].astype(BF16)
            cg = c_ref[:, gcols].astype(BF16)
            cg_t = c_ref[:, gcols].T.astype(BF16)
            cb = lax.dot_general(cg, bg, NT, preferred_element_type=F32)
            cbt = lax.dot_general(bg, cg, NT, preferred_element_type=F32)
            dcb = jnp.zeros((q, q), F32)
            db_acc = jnp.zeros((q, SSD_STATE), F32)
            dc_acc = jnp.zeros((q, SSD_STATE), F32)
            for j in range(pairs_per_group):
                p = g * pairs_per_group + j
                h = 2 * p
                pcols = slice(p * LANES, (p + 1) * LANES)
                l0, l1, mcat = _pair_decay(tm_, cb, h)
                xp = xs_ref[:, pcols]
                dtp = _pair_cols(lo, tm_["dt"], h)
                xdt = xp * dtp
                xbd = _block_diag(lo, xdt)
                dyp = dy_ref[:, pcols]
                dyb = dyp.astype(BF16)
                dsb = _pair_cols(lo, tm_["dstate"], h)
                cdr = _pair_cols(lo, tm_["cdec"], h)
                eb = _pair_cols(lo, tm_["eacs"], h)
                st = sin_ref[0, p]
                stb = st.astype(BF16)
                dst = dstate[p]
                dstb = dst.astype(BF16)
                dye = (dyp * eb).astype(BF16)
                both = jnp.dot(_pair_decay_t(tm_, cbt, h), dyb, preferred_element_type=F32)
                dx_state = jnp.dot(bg, dstb, preferred_element_type=F32) * dsb
                dxdt = jnp.where(lo, both[:q], both[q:]) + dx_state
                dmcat = lax.dot_general(dyb, xbd, NT, preferred_element_type=F32)
                dcb = dcb + dmcat[:, :q] * l0 + dmcat[:, q:] * l1
                dseg = dmcat * mcat
                csum = jnp.sum(dseg, axis=0, keepdims=True)
                rows = (rows + jnp.where(lane == h, jnp.sum(dseg[:, :q], axis=1, keepdims=True), 0.0)
                        + jnp.where(lane == h + 1, jnp.sum(dseg[:, q:], axis=1, keepdims=True), 0.0))
                cols_t = (cols_t + jnp.where(head_row == h, csum[:, :q], 0.0)
                          + jnp.where(head_row == h + 1, csum[:, q:], 0.0))
                dc_acc = dc_acc + lax.dot_general(dye, stb, NT, preferred_element_type=F32)
                db_acc = db_acc + lax.dot_general((xdt * dsb).astype(BF16), dstb, NT, preferred_element_type=F32)
                yoff = jnp.dot(cg, stb, preferred_element_type=F32) * eb
                tbuf[:, pcols] = dyp * yoff - xdt * dx_state
                xbuf[:, pcols] = dxdt * xp
                rbuf[0:1, pcols] = (jnp.sum(xdt * dx_state, axis=0, keepdims=True)
                                    + cdr * jnp.sum(dst * st, axis=0, keepdims=True))
                dxs_ref[:, pcols] = dxdt * dtp + dyp * dsk_ref[:, pcols]
                dstate[p] = dst * cdr + jnp.dot(cg_t, dye, preferred_element_type=F32)
            dcbb = dcb.astype(BF16)
            dc_ref[:, gcols] = dc_acc + jnp.dot(dcbb, bg, preferred_element_type=F32)
            db_ref[:, gcols] = db_acc + lax.dot_general(dcbb, cg, TN, preferred_element_type=F32)

        sel = sel_ref[...]
        dacs = rows - cols_t.T + _split_dot(tbuf[...], sel)
        carry = _split_dot(rbuf[...], sel)[0:1]
        anti = jnp.logical_not(tm_["causal"]) | tm_["diag"]
        da = jnp.dot(anti.astype(F32), dacs, precision=HIGHEST, preferred_element_type=F32) + carry
        ddt = da * tm_["a"] + _split_dot(xbuf[...], sel)
        ddtr = jnp.where(tm_["lane"] < n_heads, ddt * jax.nn.sigmoid(tm_["pre"]), 0.0)
        ddtr_ref[...] = ddtr.astype(BF16)
        acc_b[...] += jnp.sum(ddtr, axis=0, keepdims=True)
        acc_a[...] += jnp.sum(da * tm_["dt"], axis=0, keepdims=True)
        keep_next()

        @pl.when(i == nc - 1)
        def _():
            dbias_ref[...] = acc_b[...]
            dalog_ref[...] = acc_a[...] * tm_["a"]
            dd_ref[...] = _split_dot(acc_d[...], sel)[0:1]

    vec = pl.BlockSpec((1, LANES), lambda i: (0, 0))
    full = pl.BlockSpec((1, sw), lambda i: (0, 0))
    wide = pl.BlockSpec((q, sw), lambda i: (nc - 1 - i, 0))
    return _call(
        body, (dmixed, y, xbc, xbc, xbc, dt_raw, dt_raw, dt_bias, a_log, d_full, s_in, proj, norm_w, dproj),
        name="ssd_bwd", grid=(nc,),
        in_specs=[pl.BlockSpec((q, sw), lambda i: (nc - 1 - i, 1)), wide, wide,
                  pl.BlockSpec((q, gn), lambda i: (nc - 1 - i, bblk)),
                  pl.BlockSpec((q, gn), lambda i: (nc - 1 - i, bblk + 1)),
                  pl.BlockSpec((q, LANES), lambda i: (nc - 1 - i, 0)),
                  pl.BlockSpec((q, LANES), lambda i: (jnp.maximum(nc - 2 - i, 0), 0)), vec, vec, full,
                  pl.BlockSpec((1, n_pairs, SSD_STATE, LANES), lambda i: (nc - 1 - i, 0, 0, 0)),
                  pl.BlockSpec((q, sw), lambda i: (nc - 1 - i, z_block)), full, pl.BlockSpec(memory_space=pl.ANY)],
        out_specs=[wide, pl.BlockSpec((q, gn), lambda i: (nc - 1 - i, 0)), pl.BlockSpec((q, gn), lambda i: (nc - 1 - i, 0)),
                   pl.BlockSpec((q, LANES), lambda i: (nc - 1 - i, 0)), vec, vec,
                   pl.BlockSpec((q, sw), lambda i: (nc - 1 - i, z_block)), full, vec],
        out_shape=[SDS((t, sw), F32), SDS((t, gn), F32), SDS((t, gn), F32), SDS((t, LANES), BF16),
                   SDS((1, LANES), F32), SDS((1, LANES), F32), SDS(dproj.shape, dproj.dtype), SDS((1, sw), F32),
                   SDS((1, LANES), F32)],
        scratch_shapes=[pltpu.VMEM((n_pairs, SSD_STATE, LANES), F32), pltpu.VMEM((q, sw), F32), pltpu.VMEM((q, sw), F32),
                        pltpu.VMEM((8, sw), F32), pltpu.VMEM((1, LANES), F32), pltpu.VMEM((1, LANES), F32),
                        pltpu.VMEM((sw, LANES), BF16), pltpu.VMEM((2, len(_TERM_FIELDS), q, LANES), F32),
                        pltpu.VMEM((q, sw), F32), pltpu.VMEM((8, sw), F32)],
        sem=("arbitrary",), comm=comm, aliases={13: 6})


GATE_BLOCK = 1
Z_BLOCK = 2
CONV_BLOCK = 2


def _tiles(t):
    mm = dict(in_proj=(min(1024, t), 1024), dt_proj=(min(512, t), LANES), out_proj=(min(512, t), 1024),
              d_mixed=(min(512, t), 2048), dh=(min(512, t), 3072), dw_out=(512, 1024), dw_main=(1024, 1024),
              dw_dt=(512, LANES))
    return min(256, t), mm


def _place():
    x, y, c = lax.axis_index("x"), lax.axis_index("y"), lax.axis_index("c")
    return x, y, c, [(1 - x, y), (x, 1 - y), (1 - x, 1 - y)]


def gather_spread(shards, layer, rows=None, carry=None):
    n = len(shards)

    def make(ins, outs, ss, rs):
        x, y, c, chips = _place()
        mine = 4 * x + 2 * y + c
        peers = [(x, y, 1 - c)] + [(px, py, c) for px, py in chips]
        sends, locals_, arrivals = [], [], []
        for a in range(n):
            def place(ref, idx):
                return ref.at[idx] if rows is None else ref.at[idx, pl.ds(rows[0], rows[1])]

            src = place(ins[a], layer)
            locals_.append(pltpu.make_async_copy(src, place(outs[a], mine), ss.at[5 * a + 4]))
            for j, (px, py, pc) in enumerate(peers):
                sends.append(_remote(src, place(outs[a], mine), ss, rs, 5 * a + j, (px, py, pc)))
                arrivals.append(_remote(src, place(outs[a], 4 * px + 2 * py + pc), ss, rs, 5 * a + j, (px, py, pc)))
        return sends, locals_, arrivals

    return Comm(list(shards) + list(carry or []), [SDS((N_DEV,) + s.shape[1:], s.dtype) for s in shards],
                {n + a: a for a in range(n)} if carry else {}, 5 * n, make)


def gather_pass_on(gathered):
    def make(ins, outs, ss, rs):
        x, y, c, chips = _place()
        sends, arrivals = [], []
        for a in range(len(outs)):
            for j, (px, py) in enumerate(chips):
                blk, other = 4 * px + 2 * py + c, 4 * px + 2 * py + (1 - c)
                sends.append(_remote(outs[a].at[blk], outs[a].at[blk], ss, rs, 3 * a + j, (x, y, 1 - c)))
                arrivals.append(_remote(outs[a].at[other], outs[a].at[other], ss, rs, 3 * a + j, (x, y, 1 - c)))
        return sends, [], arrivals

    return Comm(gathered, [SDS(g.shape, g.dtype) for g in gathered], {a: a for a in range(len(gathered))},
                3 * len(gathered), make)


def sibling_swap(sends_):
    def make(ins, outs, ss, rs):
        x, y, c, _ = _place()
        cps = [_remote(ins[a], outs[a], ss, rs, a, (x, y, 1 - c)) for a in range(len(ins))]
        return cps, [], cps

    return Comm(sends_, [SDS(s.shape, s.dtype) for s in sends_], {}, len(sends_), make)


def chips_scatter(slabs, rows=None, carry=None):
    n = len(slabs)

    def make(ins, outs, ss, rs):
        x, y, c, chips = _place()
        mychip = 2 * x + y
        sends, arrivals = [], []

        def part(ref, slot):
            return ref.at[slot] if rows is None else ref.at[slot, pl.ds(rows[0], rows[1])]

        for a in range(n):
            for j, (px, py) in enumerate(chips):
                to_there = lax.rem(2 * px + py - mychip + 4, 4) - 1
                from_here = lax.rem(mychip - 2 * px - py + 4, 4) - 1
                sends.append(_remote(part(ins[a], to_there), part(outs[a], from_here), ss, rs, 3 * a + j, (px, py, c)))
                arrivals.append(_remote(part(ins[a], to_there), part(outs[a], to_there), ss, rs, 3 * a + j, (px, py, c)))
        return sends, [], arrivals

    return Comm(list(slabs) + list(carry or []), [SDS(s.shape, s.dtype) for s in slabs],
                {n + a: a for a in range(n)} if carry else {}, 3 * n, make)


def comm_only(comm, name):
    def body():
        pass

    return _call(body, (), name=name, grid=(), in_specs=[], out_specs=[], out_shape=[], comm=comm)[1]


W_IN_GATHER_EIGHTHS = (3, 1, 2, 2)


def layer_fwd(x, p, layer=0, shards=None, finish=None, nxt=False, first=False, normed=None, next_pre_w=None):
    t = x.shape[0]
    tm, mm = _tiles(t)
    n_heads = p["d_full"].shape[1] // SSD_HEAD_DIM
    travel = shards is not None
    nxt, first = nxt and travel, first and travel
    rows = shards["in"][0].shape[1] if travel else 0
    cuts = [0]
    for eighths in W_IN_GATHER_EIGHTHS:
        cuts.append(cuts[-1] + rows * eighths // 8)
    assert cuts[-1] == rows

    def next_w_in(part, carry):
        return gather_spread(shards["in"], layer + 1, rows=(cuts[part], cuts[part + 1] - cuts[part]), carry=carry) if nxt else None

    h, r_pre = normed if normed is not None else rms_fwd(x, p["pre_w"], tm)
    proj, got = mm_nn(h, p["w_main"], F32, *mm["in_proj"], "in_proj", merge_comms([
        gather_spread(shards["small"], layer) if first else None,
        gather_spread(shards["out"], layer) if travel else None, next_w_in(0, None)]))
    n_small = len(shards["small"]) if first else 0
    got_small, got_out, got_in = got[:n_small], got[n_small:n_small + 1], got[n_small + 1:]
    dt_raw, got_small = mm_nn(h, p["w_dt"], F32, *mm["dt_proj"], "dt_proj", gather_pass_on(got_small) if first else None)
    if first:
        p = dict(p, **finish["small"](got_small))
    mixed = pool_fwd(proj, p["mixw"], p["pscale"], tm)
    xbc, got_in = conv_fwd(proj, p["conv_w"], p["conv_b"], CONV_BLOCK, tm, next_w_in(1, got_in))
    (y, s_in, mixed), got = ssd_fwd(xbc, proj, dt_raw, p["dt_bias"], p["a_log"], p["d_full"], p["norm_w"], mixed, Z_BLOCK,
                                    n_heads, merge_comms([gather_pass_on(got_out) if travel else None, next_w_in(2, got_in)]))
    if travel:
        p = dict(p, **finish["out"](got[:1]))
    out, got = mm_nn(mixed, p["w_out"], F32, *mm["out_proj"], "out_proj",
                     merge_comms([next_w_in(3, got[1:]), gather_spread(shards["small"], layer + 1)]) if nxt else None)
    (x_next, r_post, *next_normed), gathered = post_fwd(out, x, p["post_w"], tm, gather_pass_on(got) if nxt else None,
                                                        next_pre_w)
    return (x_next, next_normed or None), dict(x=x, h=h, r_pre=r_pre, proj=proj, dt_raw=dt_raw, xbc=xbc, y=y, s_in=s_in,
                                               mixed=mixed, out=out, r_post=r_post), gathered, p


def _pair_sums(own, got):
    return [pair_sum(o, r, min(256, o.shape[1]), "pair_sum") for o, r in zip(own, got)]


def layer_bwd(g, s, p, split_in, split_rest, pending=None, last=False, post=None, below=None):
    t = g.shape[0]
    tm, mm = _tiles(t)
    d = g.shape[1]
    n_heads = p["d_full"].shape[1] // SSD_HEAD_DIM
    d_out, d_post = post if post is not None else post_bwd(g, s["out"], s["r_post"], p["post_w"], tm)
    dmixed, got_sib = mm_nt(d_out, p["w_out"], F32, *mm["d_mixed"], d, "d_mixed", sibling_swap(pending[1]) if pending else None)
    chip_sums = _pair_sums(pending[0], got_sib) if pending else []
    rows_in = chip_sums[0].shape[1] if pending else 0
    early = rows_in * 3 // 8 if last else 0
    dw_out, got_early = mm_tn(s["mixed"], d_out, *mm["dw_out"], "dw_out",
                              chips_scatter(chip_sums[:1], rows=(0, early)) if pending and last else None)
    dproj, dq, d_pscale, d_mixw = pool_bwd_a(dmixed, s["proj"], p["mixw"], p["pscale"], tm)
    dproj = pool_bwd_b(dq, dproj, tm)
    own_rest, send_rest = split_rest(dw_out, d_mixw)
    cut_in = rows_in if last else rows_in // 2
    (dxs, db, dc, ddtr, d_dtb, d_alog, dproj, d_norm, d_dskip), got = ssd_bwd(
        dmixed, s["y"], s["xbc"], s["proj"], s["dt_raw"], p["dt_bias"], p["a_log"], p["d_full"], p["norm_w"], s["s_in"],
        dproj, Z_BLOCK, n_heads,
        merge_comms([chips_scatter(chip_sums[:1], rows=(early, cut_in - early), carry=got_early or None) if pending else None,
                     sibling_swap(send_rest) if last else None]))
    got_first, my_sib_rest = (got[:1], got[1:]) if pending else ([], got)
    rest_comm = chips_scatter(chip_sums[1:]) if pending else None
    (dpre, d_convw, d_convb), got_rest = conv_bwd_a(dxs, db, dc, s["proj"], p["conv_w"], p["conv_b"], CONV_BLOCK, tm,
                                                    rest_comm if last else None)
    dproj = conv_bwd_b(dpre, p["conv_w"], dproj, CONV_BLOCK, tm)
    dw_main, got = mm_tn(s["h"], dproj, *mm["dw_main"], "dw_main",
                         chips_scatter(_pair_sums(own_rest, my_sib_rest)) if last else rest_comm)
    got_rest, my_chips_rest = (got_rest, got) if last else (got, [])
    dw_dt, _ = mm_tn(s["h"], ddtr, *mm["dw_dt"], "dw_dt")
    own_in, send_in = split_in(dw_main, dw_dt)
    my_sib_in = comm_only(sibling_swap(send_in), "grads_to_sibling") if last else []
    if last:
        my_sums = _pair_sums(own_in, my_sib_in)
        rows_own = my_sums[0].shape[1]
        cut_own = rows_own * 13 // 16
        dh_comm = chips_scatter(my_sums, rows=(0, cut_own))
    else:
        dh_comm = chips_scatter(chip_sums[:1], rows=(cut_in, rows_in - cut_in), carry=got_first) if pending else None
    dh, got = mm_nt(dproj, p["w_main"], F32, mm["dh"][0], d, mm["dh"][1], "dh_main", dh_comm, extra=(ddtr, p["w_dt"]))
    my_chips_in, got_first = (got, got_first) if last else ([], got if pending else got_first)
    (gx, d_pre, *post_below), got = rms_bwd(
        dh, s["x"], s["r_pre"], p["pre_w"], g, tm,
        chips_scatter(my_sums, rows=(cut_own, rows_own - cut_own), carry=my_chips_in) if last else None, below)
    my_chips_in = got if last else my_chips_in
    small = dict(pre_w=d_pre, pscale=d_pscale, conv_w=d_convw, conv_b=d_convb, dt_bias=d_dtb, a_log=d_alog,
                 d_skip=d_dskip, norm_w=d_norm, post_w=d_post)
    done = [(got_sib, got_first + got_rest)] if pending else [None]
    if last:
        done.append((my_sib_in + my_sib_rest, my_chips_in + my_chips_rest))
    return (gx, post_below or None), small, (own_in + own_rest, send_in + send_rest), done


def _two_level_gather(x_refs, out_slots, send_sems, recv_sems, local_sems):
    x, y, c, chips = _place()
    me, sibling = (x, y, c), (x, y, 1 - c)
    n = len(x_refs)

    def copy(a, k, block, to, src=None):
        return pltpu.make_async_remote_copy(
            src_ref=out_slots[a](*block) if src is None else src, dst_ref=out_slots[a](*block),
            send_sem=send_sems.at[7 * a + k], recv_sem=recv_sems.at[7 * a + k], device_id=to, device_id_type=MESH)

    mine = [pltpu.make_async_copy(x_refs[a], out_slots[a](*me), local_sems.at[a]) for a in range(n)]
    for cp in mine:
        cp.start()
    first = []
    for a in range(n):
        first.append(copy(a, 0, me, sibling, src=x_refs[a]))
        first += [copy(a, 1 + j, me, (*chip, c), src=x_refs[a]) for j, chip in enumerate(chips)]
    for cp in first:
        cp.start()
    passed = []
    for j, chip in enumerate(chips):
        for a in range(n):
            copy(a, 1 + j, (*chip, c), me).wait_recv()
            fwd = copy(a, 4 + j, (*chip, c), sibling)
            fwd.start()
            passed.append(fwd)
    for a in range(n):
        copy(a, 0, sibling, me).wait_recv()
        for j, chip in enumerate(chips):
            copy(a, 4 + j, (*chip, 1 - c), me).wait_recv()
    for cp in first + passed:
        cp.wait_send()
    for cp in mine:
        cp.wait()


def all_gather_hbm(shards, name):
    n = len(shards)

    def body(*refs):
        x_refs, out_refs = refs[:n], refs[n:2 * n]
        send_sems, recv_sems, local_sems = refs[2 * n:]
        slots = [lambda px, py, pc, o=o: o.at[:, 4 * px + 2 * py + pc] for o in out_refs]
        _two_level_gather(x_refs, slots, send_sems, recv_sems, local_sems)

    hbm = pl.BlockSpec(memory_space=pl.ANY)
    return pl.pallas_call(
        body, name=name,
        out_shape=[SDS((s.shape[0], N_DEV) + s.shape[1:], s.dtype) for s in shards],
        in_specs=[hbm] * n, out_specs=[hbm] * n,
        scratch_shapes=[pltpu.SemaphoreType.DMA((7 * n,)), pltpu.SemaphoreType.DMA((7 * n,)), pltpu.SemaphoreType.DMA((n,))],
    )(*shards)


def all_gather_vmem(block, name):
    r, c_ = block.shape

    def body(x_ref, out_ref, send_sems, recv_sems, local_sems):
        _two_level_gather([x_ref], [lambda px, py, pc: out_ref.at[4 * px + 2 * py + pc]], send_sems, recv_sems, local_sems)

    return pl.pallas_call(
        body, name=name, out_shape=SDS((N_DEV, r, c_), block.dtype),
        in_specs=[pl.BlockSpec(memory_space=pltpu.VMEM)], out_specs=pl.BlockSpec(memory_space=pltpu.VMEM),
        scratch_shapes=[pltpu.SemaphoreType.DMA((7,)), pltpu.SemaphoreType.DMA((7,)), pltpu.SemaphoreType.DMA((1,))],
        compiler_params=_params(),
    )(block)


def _block_tiles(cols):
    base = [(cols * i) // LANES for i in range(N_DEV)]
    ends = [-((-cols * (i + 1)) // LANES) for i in range(N_DEV)]
    return base, ends, max(e - b for b, e in zip(base, ends))


def _my_lane_offset(cols):
    me = 4 * lax.axis_index("x") + 2 * lax.axis_index("y") + lax.axis_index("c")
    return lax.rem(cols * me, LANES)


def shift_cast(w, tr):
    nl, r, cols = w.shape
    width = _block_tiles(cols)[2] * LANES

    def body(x_ref, o_ref, pad):
        pad[:, width - LANES:] = jnp.zeros((tr, LANES), F32)
        pad[:, :cols] = x_ref[...]
        o_ref[...] = pltpu.roll(pad[...], _my_lane_offset(cols), 1).astype(BF16)

    assert width - LANES <= cols
    return pl.pallas_call(
        body, name="shift_cast", grid=(nl, r // tr),
        in_specs=[pl.BlockSpec((pl.Squeezed(), tr, cols), lambda l, i: (l, i, 0))],
        out_specs=pl.BlockSpec((pl.Squeezed(), tr, width), lambda l, i: (l, i, 0)),
        out_shape=SDS((nl, r, width), BF16), scratch_shapes=[pltpu.VMEM((tr, width), F32)],
        compiler_params=_params(("arbitrary", "arbitrary")))(w)


def assemble_w_in(blocks, cols, n_tail, tr):
    _, r, width = blocks.shape
    base, ends, _ = _block_tiles(cols)
    total = ends[-1]
    main_tiles = (N_DEV * cols - n_tail) // LANES
    assert main_tiles == total - 1 and (N_DEV * cols - n_tail) % LANES == 0

    def body(b_ref, main_ref, tail_ref):
        for tile in range(total):
            parts = [b_ref[i, :, (tile - base[i]) * LANES:(tile - base[i] + 1) * LANES]
                     for i in range(N_DEV) if base[i] <= tile < ends[i]]
            val = parts[0] if len(parts) == 1 else parts[0] + parts[1]
            if tile < main_tiles:
                main_ref[:, tile * LANES:(tile + 1) * LANES] = val
            else:
                tail_ref[...] = val

    return pl.pallas_call(
        body, name="assemble_w_in", grid=(r // tr,),
        in_specs=[pl.BlockSpec((N_DEV, tr, width), lambda i: (0, i, 0))],
        out_specs=[pl.BlockSpec((tr, main_tiles * LANES), lambda i: (i, 0)), pl.BlockSpec((tr, LANES), lambda i: (i, 0))],
        out_shape=[SDS((r, main_tiles * LANES), blocks.dtype), SDS((r, LANES), blocks.dtype)],
        compiler_params=_params(("arbitrary",)),
    )(blocks)


def grad_blocks(dw_main, dw_tail, cols, tr):
    r = dw_main.shape[0]
    base, _, tpb = _block_tiles(cols)
    width = tpb * LANES

    def body(m_ref, t_ref, own_ref, send_ref):
        cat = jnp.concatenate([m_ref[...], t_ref[...]], axis=1)
        south = lax.axis_index("c") == 0
        for k in range(N_DEV // 2):
            a = cat[:, base[2 * k] * LANES:base[2 * k] * LANES + width]
            b = cat[:, base[2 * k + 1] * LANES:base[2 * k + 1] * LANES + width]
            own_ref[k] = jnp.where(south, a, b)
            send_ref[k] = jnp.where(south, b, a).astype(BF16)

    return pl.pallas_call(
        body, name="grad_blocks", grid=(r // tr,),
        in_specs=[pl.BlockSpec((tr, dw_main.shape[1]), lambda i: (i, 0)), pl.BlockSpec((tr, LANES), lambda i: (i, 0))],
        out_specs=[pl.BlockSpec((N_DEV // 2, tr, width), lambda i: (0, i, 0))] * 2,
        out_shape=[SDS((N_DEV // 2, r, width), F32), SDS((N_DEV // 2, r, width), BF16)],
        compiler_params=_params(("arbitrary",)),
    )(dw_main, dw_tail)


def _adamw(w, g, m, v):
    m = ADAM_B1 * m + (1.0 - ADAM_B1) * g
    v = ADAM_B2 * v + (1.0 - ADAM_B2) * jnp.square(g)
    m_hat = m / (1.0 - ADAM_B1 ** ADAM_STEP)
    v_hat = v / (1.0 - ADAM_B2 ** ADAM_STEP)
    delta = -ADAM_LR * (m_hat / (jnp.sqrt(v_hat) + ADAM_EPS) + ADAM_WD * w)
    return delta, m, v


def _my_chip():
    return 2 * lax.axis_index("x") + lax.axis_index("y")


def pair_sum(own, got, tr, name):
    k, r, c_ = own.shape
    others = lax.rem(_my_chip() + 1 + jnp.arange(k - 1, dtype=jnp.int32), k)

    def body(others_ref, a_ref, b_ref, o_ref):
        o_ref[...] = (a_ref[...] + b_ref[...].astype(F32)).astype(BF16)

    src = pl.BlockSpec((pl.Squeezed(), tr, c_), lambda s, i, oth: (oth[s], i, 0))
    return pl.pallas_call(
        body, name=name, out_shape=SDS((k - 1, r, c_), BF16),
        grid_spec=pltpu.PrefetchScalarGridSpec(
            num_scalar_prefetch=1, grid=(k - 1, r // tr), in_specs=[src, src],
            out_specs=pl.BlockSpec((pl.Squeezed(), tr, c_), lambda s, i, oth: (s, i, 0))),
        compiler_params=_params(("arbitrary", "arbitrary")),
    )(others, own, got)


def reduce_adam(own, got_sibling, got_chips, w, m, v, prev, layer, tr, name, shifted=False):
    nl, r, cols = w.shape
    c_ = own.shape[-1]
    n_scratch = 1 if shifted else 0
    chip = jnp.reshape(_my_chip(), (1,)).astype(jnp.int32)

    def body(chip_ref, own_ref, sib_ref, c0_ref, c1_ref, c2_ref, w_ref, m_ref, v_ref, *rest):
        g_ref, d_ref, nm_ref, nv_ref = rest[len(rest) - n_scratch - 4:len(rest) - n_scratch]
        g = (own_ref[...] + sib_ref[...].astype(F32) + c0_ref[...].astype(F32) + c1_ref[...].astype(F32)
             + c2_ref[...].astype(F32))
        if shifted:
            rest[-1][...] = pltpu.roll(g, c_ - _my_lane_offset(cols), 1)
            g = rest[-1][:, :cols]
        delta, nm, nv = _adamw(w_ref[...], g, m_ref[...], v_ref[...])
        g_ref[...] = g
        d_ref[...] = delta
        nm_ref[...] = nm
        nv_ref[...] = nv

    mine = pl.BlockSpec((pl.Squeezed(), tr, c_), lambda i, ch: (ch[0], i, 0))
    lay = pl.BlockSpec((pl.Squeezed(), tr, cols), lambda i, ch: (layer, i, 0))
    chips = [pl.BlockSpec((pl.Squeezed(), tr, c_), lambda i, ch, s=s: (s, i, 0)) for s in range(3)]
    in_specs = [mine, mine] + chips + [lay, lay, lay]
    args = [chip, own, got_sibling, got_chips, got_chips, got_chips, w, m, v]
    aliases = {}
    if prev is not None:
        in_specs += [pl.BlockSpec(memory_space=pl.ANY)] * 4
        aliases = {len(args) + k: k for k in range(4)}
        args += list(prev)
    return pl.pallas_call(
        body, name=name, out_shape=[SDS((nl, r, cols), F32)] * 4, input_output_aliases=aliases,
        grid_spec=pltpu.PrefetchScalarGridSpec(
            num_scalar_prefetch=1, grid=(r // tr,), in_specs=in_specs, out_specs=[lay] * 4,
            scratch_shapes=[pltpu.VMEM((tr, c_), F32)] * n_scratch),
        compiler_params=_params(("arbitrary",)),
    )(*args)


def sum_devices(packs):
    n, r, c_ = packs.shape

    def body(p_ref, o_ref):
        acc = p_ref[0]
        for k in range(1, n):
            acc = acc + p_ref[k]
        o_ref[...] = acc

    return pl.pallas_call(body, name="sum_devices", out_shape=SDS((r, c_), F32), compiler_params=_params())(packs)


def adam_small(w, g, m, v):
    def body(w_ref, g_ref, m_ref, v_ref, d_ref, nm_ref, nv_ref):
        delta, nm, nv = _adamw(w_ref[...], g_ref[...], m_ref[...], v_ref[...])
        d_ref[...] = delta
        nm_ref[...] = nm
        nv_ref[...] = nv

    return pl.pallas_call(body, name="adam_small", out_shape=[SDS(w.shape, F32)] * 3, compiler_params=_params())(w, g, m, v)


SMALL = ("pre_norm_w", "pool_scale", "conv_b", "dt_bias", "a_log", "d_skip", "_pad", "ssd_norm_w", "post_norm_w", "conv_w")


def _pack(parts):
    flat = jnp.concatenate([parts[k] for k in SMALL], axis=1).reshape(-1, LANES)
    return jnp.pad(flat, ((0, (-flat.shape[0]) % 8), (0, 0)))


def _unpack(pack, sizes, nl):
    total = sum(sizes[k] for k in SMALL)
    flat = pack[: nl * total // LANES].reshape(nl, total)
    out, o = {}, 0
    for k in SMALL:
        out[k] = flat[:, o:o + sizes[k]]
        o += sizes[k]
    return out


def kernel(x, pre_norm_w, w_in, pool_mix_w, pool_scale, conv_w, conv_b, dt_bias, a_log, d_skip, ssd_norm_w, w_out, post_norm_w, loss_target, m_pre_norm_w, m_w_in, m_pool_mix_w, m_pool_scale, m_conv_w, m_conv_b, m_dt_bias, m_a_log, m_d_skip, m_ssd_norm_w, m_w_out, m_post_norm_w, v_pre_norm_w, v_w_in, v_pool_mix_w, v_pool_scale, v_conv_w, v_conv_b, v_dt_bias, v_a_log, v_d_skip, v_ssd_norm_w, v_w_out, v_post_norm_w):
    cx, cy, cc = lax.axis_index("x"), lax.axis_index("y"), lax.axis_index("c")
    me = 4 * cx + 2 * cy + cc
    mychip = 2 * cx + cy
    nl, d, cols = w_in.shape
    t = x.shape[1]
    n_heads = a_log.shape[1]
    sw = n_heads * SSD_HEAD_DIM
    pw = pool_scale.shape[1]
    cd = conv_b.shape[1]
    ng, gsh, gw = pool_mix_w.shape[1:]
    e_main = N_DEV * cols - n_heads
    assert x.shape[0] == 1 and pw == sw and cd == sw + 2 * SSD_GROUPS * SSD_STATE and e_main == 2 * pw + sw + cd
    assert 2 * pw + sw == CONV_BLOCK * cd and n_heads <= LANES and t % SSD_CHUNK == 0 and gsh * N_DEV == gw
    tm, _ = _tiles(t)

    shards = {"in": [shift_cast(w_in, tm)], "out": [w_out.astype(BF16)], "small": [pool_mix_w.astype(BF16), conv_w]}
    pad_h = ((0, 0), (0, LANES - n_heads))

    def params_a(l, g_in):
        w_main, w_dt = assemble_w_in(g_in, cols, n_heads, tm)
        return dict(pre_w=pre_norm_w[l:l + 1], w_main=w_main, w_dt=w_dt, pscale=pool_scale[l:l + 1], conv_b=conv_b[l:l + 1],
                    dt_bias=jnp.pad(dt_bias[l:l + 1], pad_h), a_log=jnp.pad(a_log[l:l + 1], pad_h),
                    d_full=jnp.repeat(d_skip[l:l + 1], SSD_HEAD_DIM, axis=1), norm_w=ssd_norm_w[l:l + 1],
                    post_w=post_norm_w[l:l + 1])

    finish = {"small": lambda got: dict(mixw=got[0].transpose(1, 0, 2, 3).reshape(ng, gw, gw),
                                        conv_w=got[1].transpose(1, 0, 2).reshape(CONV_WIDTH, cd)),
              "out": lambda got: dict(w_out=got[0].reshape(N_DEV * w_out.shape[1], d))}

    xs = x[0]
    saved, params = [], []
    p = params_a(0, all_gather_hbm([shards["in"][0][:1]], "gather_w_in")[0][0])
    normed = None
    for l in range(nl):
        (xs, normed), s, gathered, p = layer_fwd(xs, p, l, shards, finish, nxt=l + 1 < nl, first=l == 0, normed=normed,
                                                 next_pre_w=pre_norm_w[l + 1:l + 2] if l + 1 < nl else None)
        saved.append(s)
        params.append(p)
        if l + 1 < nl:
            p = dict(params_a(l + 1, gathered[0]), **finish["small"](gathered[1:]))
    loss_part, g = loss_grad(xs, loss_target[0], tm)
    loss = lax.psum(loss_part[0, 0], ("x", "y", "c"))

    big = {"w_in": (w_in, m_w_in, v_w_in), "w_out": (w_out, m_w_out, v_w_out),
           "pool_mix_w": tuple(a.reshape(nl, ng * gsh, gw) for a in (pool_mix_w, m_pool_mix_w, v_pool_mix_w))}
    names = list(big)
    big_out = {k: None for k in big}
    small_g = [None] * nl

    def apply(layer, own, got_sib, got_chips):
        for k, o, gs_, gc in zip(names, own, got_sib, got_chips):
            wk, mk, vk = big[k]
            big_out[k] = reduce_adam(o, gs_, gc, wk, mk, vk, big_out[k], layer, min(256, wk.shape[1]), "reduce_adam_" + k,
                                     shifted=(k == "w_in"))

    def split_in(dw_main, dw_dt):
        own, send = grad_blocks(dw_main, dw_dt, cols, min(128, d))
        return [own], [send]

    def split_rest(dw_out, d_mixw):
        halves = [lambda ci: lax.dynamic_index_in_dim(dw_out.reshape(4, 2, -1, d), ci, 1, keepdims=False),
                  lambda ci: lax.dynamic_index_in_dim(
                      d_mixw.reshape(ng, 4, 2, gsh, gw), ci, 2, keepdims=False).transpose(1, 0, 2, 3).reshape(4, ng * gsh, gw)]
        return [h(cc) for h in halves], [h(1 - cc).astype(BF16) for h in halves]

    pending = post = None
    for l in reversed(range(nl)):
        below = (saved[l - 1]["out"], saved[l - 1]["r_post"], params[l - 1]["post_w"]) if l > 0 else None
        (g, post), gr, mine, done = layer_bwd(g, saved[l], params[l], split_in, split_rest, pending, last=(l == 0),
                                              post=post, below=below)
        if pending is not None:
            apply(l + 1, pending[0], *done[0])
        if l == 0:
            apply(0, mine[0], *done[1])
        pending = mine
        small_g[l] = dict(pre_norm_w=gr["pre_w"], pool_scale=gr["pscale"], conv_b=gr["conv_b"], dt_bias=gr["dt_bias"][:, :n_heads],
                          a_log=gr["a_log"][:, :n_heads], d_skip=gr["d_skip"][:, :n_heads], _pad=jnp.zeros((1, LANES - 3 * n_heads), F32),
                          ssd_norm_w=gr["norm_w"], post_norm_w=gr["post_w"], conv_w=gr["conv_w"].reshape(1, CONV_WIDTH * cd))

    sizes = {k: small_g[0][k].shape[1] for k in SMALL}
    gsum = sum_devices(all_gather_vmem(_pack({k: jnp.concatenate([sg[k] for sg in small_g], axis=0) for k in SMALL}),
                                       "gather_small_grads"))
    gs = _unpack(gsum, sizes, nl)
    csh = conv_w.shape[2]
    gs["conv_w"] = lax.dynamic_slice_in_dim(gs["conv_w"].reshape(nl, CONV_WIDTH, cd), me * csh, csh, axis=2).reshape(nl, -1)
    lsizes = dict(sizes, conv_w=CONV_WIDTH * csh)
    zpad = jnp.zeros((nl, sizes["_pad"]), F32)

    def local(pre, scale, cb, dtb, al, dsk, nw, post, cw):
        return _pack(dict(pre_norm_w=pre, pool_scale=scale, conv_b=cb, dt_bias=dtb, a_log=al, d_skip=dsk, _pad=zpad,
                          ssd_norm_w=nw, post_norm_w=post, conv_w=cw.reshape(nl, -1)))

    wp = local(pre_norm_w, pool_scale, conv_b, dt_bias, a_log, d_skip, ssd_norm_w, post_norm_w, conv_w)
    mp = local(m_pre_norm_w, m_pool_scale, m_conv_b, m_dt_bias, m_a_log, m_d_skip, m_ssd_norm_w, m_post_norm_w, m_conv_w)
    vp = local(v_pre_norm_w, v_pool_scale, v_conv_b, v_dt_bias, v_a_log, v_d_skip, v_ssd_norm_w, v_post_norm_w, v_conv_w)
    small_out = [gs] + [_unpack(o, lsizes, nl) for o in adam_small(wp, _pack(gs), mp, vp)]

    def leaf(kind, name):
        if name in big:
            return big_out[name][kind].reshape(big[name][0].shape if name != "pool_mix_w" else pool_mix_w.shape)
        val = small_out[kind][name]
        return val.reshape(conv_w.shape) if name == "conv_w" else val

    order = ("pre_norm_w", "w_in", "pool_mix_w", "pool_scale", "conv_w", "conv_b", "dt_bias", "a_log", "d_skip",
             "ssd_norm_w", "w_out", "post_norm_w")
    return (loss, g[None]) + tuple(leaf(kind, name) for kind in range(4) for name in order)
```

```python
import jax
import jax.numpy as jnp
from jax import lax
from jax.experimental import pallas as pl
from jax.experimental.pallas import tpu as pltpu

F32 = jnp.float32
BF16 = jnp.bfloat16
SDS = jax.ShapeDtypeStruct
MESH = pl.DeviceIdType.MESH
HIGHEST = lax.Precision.HIGHEST

NORM_EPS = 1e-6
POOL_WINDOWS = (2, 4, 8, 16)
POOL_HALO = 16
CONV_WIDTH = 4
CONV_HALO = 8
SSD_CHUNK = 128
SSD_HEAD_DIM = 64
SSD_STATE = 128
SSD_GROUPS = 4
LANES = 128
N_DEV = 8

ADAM_LR = 0.001
ADAM_B1 = 0.9
ADAM_B2 = 0.999
ADAM_EPS = 1e-08
ADAM_WD = 0.01
ADAM_STEP = 10

VMEM_LIMIT = 56 * 1024 * 1024

NT = (((1,), (1,)), ((), ()))
TN = (((0,), (0,)), ((), ()))


def _params(sem=None):
    kw = dict(vmem_limit_bytes=VMEM_LIMIT)
    if sem is not None:
        kw["dimension_semantics"] = sem
    return pltpu.CompilerParams(**kw)


def _silu(v):
    return v * jax.nn.sigmoid(v)


def _dsilu(v):
    s = jax.nn.sigmoid(v)
    return s * (1.0 + v * (1.0 - s))


def _split_dot(v, sel):
    hi = v.astype(BF16)
    lo = (v - hi.astype(F32)).astype(BF16)
    return (jnp.dot(hi, sel, preferred_element_type=F32) + jnp.dot(lo, sel, preferred_element_type=F32))


def _head_selector(width, per):
    ch = lax.broadcasted_iota(jnp.int32, (width, LANES), 0)
    hd = lax.broadcasted_iota(jnp.int32, (width, LANES), 1)
    return jnp.where((ch >= hd * per) & (ch < (hd + 1) * per), 1.0, 0.0).astype(BF16)


class Comm:
    def __init__(self, inputs, out_shapes, aliases, n_sems, make):
        self.inputs, self.out_shapes, self.aliases, self.n_sems, self.make = list(inputs), list(out_shapes), dict(aliases), n_sems, make


def _remote(src, dst, send_sems, recv_sems, k, peer):
    return pltpu.make_async_remote_copy(src_ref=src, dst_ref=dst, send_sem=send_sems.at[k], recv_sem=recv_sems.at[k],
                                        device_id=peer, device_id_type=MESH)


class _SemRange:
    def __init__(self, sems, start):
        self.sems, self.start = sems, start

    @property
    def at(self):
        return self

    def __getitem__(self, k):
        return self.sems.at[self.start + k]


def merge_comms(comms):
    comms = [c for c in comms if c is not None]
    if len(comms) <= 1:
        return comms[0] if comms else None
    aliases, i_off, o_off = {}, 0, 0
    for c in comms:
        aliases.update({i_off + k: o_off + v for k, v in c.aliases.items()})
        i_off, o_off = i_off + len(c.inputs), o_off + len(c.out_shapes)

    def make(ins, outs, ss, rs):
        sends, locals_, arrivals, i0, o0, s0 = [], [], [], 0, 0, 0
        for c in comms:
            s, l, a = c.make(ins[i0:i0 + len(c.inputs)], outs[o0:o0 + len(c.out_shapes)], _SemRange(ss, s0), _SemRange(rs, s0))
            sends, locals_, arrivals = sends + s, locals_ + l, arrivals + a
            i0, o0, s0 = i0 + len(c.inputs), o0 + len(c.out_shapes), s0 + c.n_sems
        return sends, locals_, arrivals

    return Comm(sum((c.inputs for c in comms), []), sum((c.out_shapes for c in comms), []), aliases,
                sum(c.n_sems for c in comms), make)


def _call(body, args, *, name, grid, in_specs, out_specs, out_shape, scratch_shapes=(), sem=None, comm=None, aliases=None):
    in_specs, out_specs, out_shape = list(in_specs), list(out_specs), list(out_shape)
    aliases = dict(aliases or {})
    if comm is None:
        outs = pl.pallas_call(body, name=name, grid=grid, in_specs=in_specs, out_specs=out_specs, out_shape=out_shape,
                              scratch_shapes=list(scratch_shapes), input_output_aliases=aliases,
                              compiler_params=_params(sem))(*args)
        return list(outs), []
    ni, no, nci, nco, ns = len(in_specs), len(out_specs), len(comm.inputs), len(comm.out_shapes), len(scratch_shapes)
    hbm = pl.BlockSpec(memory_space=pl.ANY)

    def hosted(*refs):
        ins, cins = refs[:ni], refs[ni:ni + nci]
        outs, couts = refs[ni + nci:ni + nci + no], refs[ni + nci + no:ni + nci + no + nco]
        scratch = refs[ni + nci + no + nco:]
        sends, locals_, arrivals = comm.make(cins, couts, scratch[ns], scratch[ns + 1])
        first = last = None if grid else True
        for axis, extent in enumerate(grid):
            pid = pl.program_id(axis)
            first = (pid == 0) if first is None else first & (pid == 0)
            last = (pid == extent - 1) if last is None else last & (pid == extent - 1)

        @pl.when(first)
        def _():
            for cp in locals_ + sends:
                cp.start()

        body(*ins, *outs, *scratch[:ns])

        @pl.when(last)
        def _():
            for cp in arrivals:
                cp.wait_recv()
            for cp in sends:
                cp.wait_send()
            for cp in locals_:
                cp.wait()

    outs = pl.pallas_call(
        hosted, name=name, grid=grid, in_specs=in_specs + [hbm] * nci, out_specs=out_specs + [hbm] * nco,
        out_shape=out_shape + comm.out_shapes,
        scratch_shapes=list(scratch_shapes) + [pltpu.SemaphoreType.DMA((comm.n_sems,)), pltpu.SemaphoreType.DMA((comm.n_sems,))],
        input_output_aliases={**aliases, **{ni + k: no + v for k, v in comm.aliases.items()}},
        compiler_params=_params(sem),
    )(*args, *comm.inputs)
    return list(outs[:no]), list(outs[no:])


def rms_fwd(x, w, tm):
    t, d = x.shape

    def body(x_ref, w_ref, h_ref, r_ref):
        xv = x_ref[...]
        r = lax.rsqrt(jnp.mean(xv * xv, axis=-1, keepdims=True) + NORM_EPS)
        h_ref[...] = (xv * r * w_ref[...]).astype(BF16)
        r_ref[...] = r

    return pl.pallas_call(
        body, name="rms_fwd", grid=(t // tm,),
        in_specs=[pl.BlockSpec((tm, d), lambda i: (i, 0)), pl.BlockSpec((1, d), lambda i: (0, 0))],
        out_specs=[pl.BlockSpec((tm, d), lambda i: (i, 0)), pl.BlockSpec((tm, 1), lambda i: (i, 0))],
        out_shape=[SDS((t, d), BF16), SDS((t, 1), F32)],
        compiler_params=_params(("arbitrary",)),
    )(x, w)


def post_fwd(out, x, w, tm, comm=None, next_w=None):
    t, d = x.shape
    fused = next_w is not None

    def body(o_ref, x_ref, w_ref, *rest):
        y_ref, r_ref = rest[fused:fused + 2]
        ov = o_ref[...]
        r = lax.rsqrt(jnp.mean(ov * ov, axis=-1, keepdims=True) + NORM_EPS)
        y = x_ref[...] + ov * r * w_ref[...]
        y_ref[...] = y
        r_ref[...] = r
        if fused:
            rn = lax.rsqrt(jnp.mean(y * y, axis=-1, keepdims=True) + NORM_EPS)
            rest[3][...] = (y * rn * rest[0][...]).astype(BF16)
            rest[4][...] = rn

    row = pl.BlockSpec((tm, d), lambda i: (i, 0))
    vec = pl.BlockSpec((1, d), lambda i: (0, 0))
    col = pl.BlockSpec((tm, 1), lambda i: (i, 0))
    return _call(
        body, (out, x, w) + ((next_w,) if fused else ()), name="post_fwd", grid=(t // tm,),
        in_specs=[row, row, vec] + [vec] * fused, out_specs=[row, col] + [row, col] * fused,
        out_shape=[SDS((t, d), F32), SDS((t, 1), F32)] + [SDS((t, d), BF16), SDS((t, 1), F32)] * fused,
        sem=("arbitrary",), comm=comm)


def _norm_bwd(g_n, n, r):
    return r * (g_n - n * jnp.mean(g_n * n, axis=-1, keepdims=True))


def post_bwd(g, out, r, w, tm):
    t, d = g.shape

    def body(g_ref, o_ref, r_ref, w_ref, do_ref, dw_ref):
        i = pl.program_id(0)
        gv = g_ref[...]
        rv = r_ref[...]
        n = o_ref[...] * rv
        part = jnp.sum(gv * n, axis=0, keepdims=True)

        @pl.when(i == 0)
        def _():
            dw_ref[...] = part

        @pl.when(i > 0)
        def _():
            dw_ref[...] += part

        do_ref[...] = _norm_bwd(gv * w_ref[...], n, rv).astype(BF16)

    return pl.pallas_call(
        body, name="post_bwd", grid=(t // tm,),
        in_specs=[pl.BlockSpec((tm, d), lambda i: (i, 0)), pl.BlockSpec((tm, d), lambda i: (i, 0)),
                  pl.BlockSpec((tm, 1), lambda i: (i, 0)), pl.BlockSpec((1, d), lambda i: (0, 0))],
        out_specs=[pl.BlockSpec((tm, d), lambda i: (i, 0)), pl.BlockSpec((1, d), lambda i: (0, 0))],
        out_shape=[SDS((t, d), BF16), SDS((1, d), F32)],
        compiler_params=_params(("arbitrary",)),
    )(g, out, r, w)


def rms_bwd(dh, x, r, w, g, tm, comm=None, below=None):
    t, d = x.shape
    fused = below is not None

    def body(a_ref, x_ref, r_ref, w_ref, g_ref, *rest):
        gx_ref, dw_ref = rest[3 * fused:3 * fused + 2]
        i = pl.program_id(0)
        dh = a_ref[...]
        rv = r_ref[...]
        n = x_ref[...] * rv
        gx = g_ref[...] + _norm_bwd(dh * w_ref[...], n, rv)
        gx_ref[...] = gx
        parts = [(dw_ref, jnp.sum(dh * n, axis=0, keepdims=True))]
        if fused:
            o_ref, rp_ref, pw_ref = rest[:3]
            do_ref, dpw_ref = rest[5:7]
            rp = rp_ref[...]
            nb = o_ref[...] * rp
            do_ref[...] = _norm_bwd(gx * pw_ref[...], nb, rp).astype(BF16)
            parts.append((dpw_ref, jnp.sum(gx * nb, axis=0, keepdims=True)))

        @pl.when(i == 0)
        def _():
            for ref, part in parts:
                ref[...] = part

        @pl.when(i > 0)
        def _():
            for ref, part in parts:
                ref[...] += part

    row = pl.BlockSpec((tm, d), lambda i: (i, 0))
    vec = pl.BlockSpec((1, d), lambda i: (0, 0))
    col = pl.BlockSpec((tm, 1), lambda i: (i, 0))
    return _call(
        body, (dh, x, r, w, g) + tuple(below or ()), name="rms_bwd", grid=(t // tm,),
        in_specs=[row, row, col, vec, row] + [row, col, vec] * fused, out_specs=[row, vec] + [row, vec] * fused,
        out_shape=[SDS((t, d), F32), SDS((1, d), F32)] + [SDS((t, d), BF16), SDS((1, d), F32)] * fused,
        sem=("arbitrary",), comm=comm)


def loss_grad(y, target, tm):
    t, d = y.shape

    def body(y_ref, t_ref, l_ref, g_ref):
        i = pl.program_id(0)
        err = y_ref[...] - t_ref[...]
        g_ref[...] = err / d
        part = 0.5 * jnp.sum(jnp.mean(err * err, axis=-1, keepdims=True), axis=0, keepdims=True)

        @pl.when(i == 0)
        def _():
            l_ref[...] = part

        @pl.when(i > 0)
        def _():
            l_ref[...] += part

    row = pl.BlockSpec((tm, d), lambda i: (i, 0))
    return pl.pallas_call(
        body, name="loss_grad", grid=(t // tm,), in_specs=[row, row],
        out_specs=[pl.BlockSpec((1, 1), lambda i: (0, 0)), row],
        out_shape=[SDS((1, 1), F32), SDS((t, d), F32)],
        compiler_params=_params(("arbitrary",)),
    )(y, target)


def mm_nn(a, b, out_dtype, tm, tn, name, comm=None):
    m, k = a.shape
    n = b.shape[1]

    def body(a_ref, b_ref, o_ref):
        o_ref[...] = jnp.dot(a_ref[...], b_ref[...], preferred_element_type=F32).astype(out_dtype)

    outs, couts = _call(
        body, (a, b), name=name, grid=(n // tn, m // tm),
        in_specs=[pl.BlockSpec((tm, k), lambda j, i: (i, 0)), pl.BlockSpec((k, tn), lambda j, i: (0, j))],
        out_specs=[pl.BlockSpec((tm, tn), lambda j, i: (i, j))],
        out_shape=[SDS((m, n), out_dtype)], sem=("arbitrary", "arbitrary"), comm=comm)
    return outs[0], couts


def mm_nt(a, b, out_dtype, tm, tn, tk, name, comm=None, extra=None):
    m, k = a.shape
    n = b.shape[0]
    nk = k // tk

    def body(a_ref, b_ref, *rest):
        o_ref, acc_ref = rest[-2:]
        kk = pl.program_id(2)
        part = lax.dot_general(a_ref[...], b_ref[...], NT, preferred_element_type=F32)
        if nk == 1:
            if extra is not None:
                part = part + lax.dot_general(rest[0][...], rest[1][...], NT, preferred_element_type=F32)
            o_ref[...] = part.astype(out_dtype)
        else:
            @pl.when(kk == 0)
            def _():
                if extra is None:
                    acc_ref[...] = part
                else:
                    acc_ref[...] = part + lax.dot_general(rest[0][...], rest[1][...], NT, preferred_element_type=F32)

            @pl.when(kk > 0)
            def _():
                acc_ref[...] += part

            @pl.when(kk == nk - 1)
            def _():
                o_ref[...] = acc_ref[...].astype(out_dtype)

    more_specs = [] if extra is None else [pl.BlockSpec((tm, extra[0].shape[1]), lambda i, j, kk: (i, 0)),
                                           pl.BlockSpec((tn, extra[1].shape[1]), lambda i, j, kk: (j, 0))]
    outs, couts = _call(
        body, (a, b) + tuple(extra or ()), name=name, grid=(m // tm, n // tn, nk),
        in_specs=[pl.BlockSpec((tm, tk), lambda i, j, kk: (i, kk)), pl.BlockSpec((tn, tk), lambda i, j, kk: (j, kk))] + more_specs,
        out_specs=[pl.BlockSpec((tm, tn), lambda i, j, kk: (i, j))],
        out_shape=[SDS((m, n), out_dtype)],
        scratch_shapes=[pltpu.VMEM((tm, tn) if nk > 1 else (8, LANES), F32)],
        sem=("arbitrary", "arbitrary", "arbitrary"), comm=comm)
    return outs[0], couts


def mm_tn(a, b, tm, tn, name, comm=None):
    t, m = a.shape
    n = b.shape[1]

    def body(a_ref, b_ref, o_ref):
        o_ref[...] = lax.dot_general(a_ref[...], b_ref[...], TN, preferred_element_type=F32)

    outs, couts = _call(
        body, (a, b), name=name, grid=(m // tm, n // tn),
        in_specs=[pl.BlockSpec((t, tm), lambda i, j: (0, i)), pl.BlockSpec((t, tn), lambda i, j: (0, j))],
        out_specs=[pl.BlockSpec((tm, tn), lambda i, j: (i, j))],
        out_shape=[SDS((m, n), F32)], sem=("arbitrary", "arbitrary"), comm=comm)
    return outs[0], couts


def _window_sums(ext, n_rows, lookahead):
    def sh(v, k):
        return pltpu.roll(v, (n_rows - k) if lookahead else k, 0)
    s2 = ext + sh(ext, 1)
    s4 = s2 + sh(s2, 2)
    s8 = s4 + sh(s4, 4)
    s16 = s8 + sh(s8, 8)
    return (s2, s4, s8, s16)


def _pool_counts(i, tm, w):
    tpos = i * tm + lax.broadcasted_iota(jnp.int32, (tm, 1), 0)
    return jnp.minimum(tpos + 1, w).astype(F32)


def _pooled(uc_ref, up_ref, i, tm):
    cur = uc_ref[...]
    prev = jnp.where(i > 0, up_ref[...], 0.0)
    ext = jnp.concatenate([prev, cur], axis=0)
    return cur, _window_sums(ext, tm + POOL_HALO, False)


def pool_fwd(proj, mixw, scale, tm):
    t = proj.shape[0]
    pw = scale.shape[1]
    gw = pw // len(POOL_WINDOWS)
    nh = tm // POOL_HALO

    def body(uc_ref, up_ref, g_ref, w_ref, s_ref, o_ref):
        i = pl.program_id(0)
        cur, sums = _pooled(uc_ref, up_ref, i, tm)
        for g, w in enumerate(POOL_WINDOWS):
            cols = slice(g * gw, (g + 1) * gw)
            pooled = sums[g][POOL_HALO:, cols] / _pool_counts(i, tm, w) - cur[:, cols]
            mixed = jnp.dot(pooled.astype(BF16), w_ref[g], preferred_element_type=F32)
            o_ref[:, cols] = (mixed * s_ref[:, cols] * _silu(g_ref[:, cols])).astype(BF16)

    return pl.pallas_call(
        body, name="pool_fwd", grid=(t // tm,),
        in_specs=[pl.BlockSpec((tm, pw), lambda i: (i, 0)),
                  pl.BlockSpec((POOL_HALO, pw), lambda i: (jnp.maximum(i * nh - 1, 0), 0)),
                  pl.BlockSpec((tm, pw), lambda i: (i, 1)),
                  pl.BlockSpec(mixw.shape, lambda i: (0, 0, 0)),
                  pl.BlockSpec((1, pw), lambda i: (0, 0))],
        out_specs=pl.BlockSpec((tm, pw), lambda i: (i, 0)),
        out_shape=SDS((t, 2 * pw), BF16),
        compiler_params=_params(("arbitrary",)),
    )(proj, proj, proj, mixw, scale)


def pool_bwd_a(dmixed, proj, mixw, scale, tm):
    t, e = proj.shape
    pw = scale.shape[1]
    ng = len(POOL_WINDOWS)
    gw = pw // ng
    nh = tm // POOL_HALO

    def body(dy_ref, uc_ref, up_ref, g_ref, w_ref, s_ref, dg_ref, dq_ref, ds_ref, dw_ref):
        i = pl.program_id(0)

        @pl.when(i == 0)
        def _():
            ds_ref[...] = jnp.zeros_like(ds_ref)
            dw_ref[...] = jnp.zeros_like(dw_ref)

        cur, sums = _pooled(uc_ref, up_ref, i, tm)
        for g, w in enumerate(POOL_WINDOWS):
            cols = slice(g * gw, (g + 1) * gw)
            cnt = _pool_counts(i, tm, w)
            pooled = (sums[g][POOL_HALO:, cols] / cnt - cur[:, cols]).astype(BF16)
            mixed = jnp.dot(pooled, w_ref[g], preferred_element_type=F32)
            gate = g_ref[:, cols]
            dy = dy_ref[:, cols]
            sc = s_ref[:, cols]
            dg_ref[:, cols] = (dy * mixed * sc * _dsilu(gate)).astype(BF16)
            ds = dy * _silu(gate)
            ds_ref[:, cols] += jnp.sum(ds * mixed, axis=0, keepdims=True)
            dmix = (ds * sc).astype(BF16)
            dw_ref[g] += lax.dot_general(pooled, dmix, TN, preferred_element_type=F32)
            dq_ref[:, cols] = lax.dot_general(dmix, w_ref[g], NT, preferred_element_type=F32) / cnt

    return pl.pallas_call(
        body, name="pool_bwd_a", grid=(t // tm,),
        in_specs=[pl.BlockSpec((tm, pw), lambda i: (i, 0)),
                  pl.BlockSpec((tm, pw), lambda i: (i, 0)),
                  pl.BlockSpec((POOL_HALO, pw), lambda i: (jnp.maximum(i * nh - 1, 0), 0)),
                  pl.BlockSpec((tm, pw), lambda i: (i, 1)),
                  pl.BlockSpec(mixw.shape, lambda i: (0, 0, 0)),
                  pl.BlockSpec((1, pw), lambda i: (0, 0))],
        out_specs=[pl.BlockSpec((tm, pw), lambda i: (i, 1)),
                   pl.BlockSpec((tm, pw), lambda i: (i, 0)),
                   pl.BlockSpec((1, pw), lambda i: (0, 0)),
                   pl.BlockSpec((ng, gw, gw), lambda i: (0, 0, 0))],
        out_shape=[SDS((t, e), BF16), SDS((t, pw), F32), SDS((1, pw), F32), SDS((ng, gw, gw), F32)],
        compiler_params=_params(("arbitrary",)),
    )(dmixed, proj, proj, proj, mixw, scale)


def pool_bwd_b(dq, dproj, tm):
    t, pw = dq.shape
    gw = pw // len(POOL_WINDOWS)
    nh = tm // POOL_HALO
    nt = t // tm

    def body(c_ref, n_ref, alias_ref, o_ref):
        i = pl.program_id(0)
        cur = c_ref[...]
        nxt = jnp.where(i < nt - 1, n_ref[...], 0.0)
        sums = _window_sums(jnp.concatenate([cur, nxt], axis=0), tm + POOL_HALO, True)
        for g, w in enumerate(POOL_WINDOWS):
            cols = slice(g * gw, (g + 1) * gw)
            o_ref[:, cols] = (sums[g][:tm, cols] - cur[:, cols] * _pool_counts(i, tm, w)).astype(BF16)

    return pl.pallas_call(
        body, name="pool_bwd_b", grid=(nt,),
        in_specs=[pl.BlockSpec((tm, pw), lambda i: (i, 0)),
                  pl.BlockSpec((POOL_HALO, pw), lambda i: (jnp.minimum((i + 1) * nh, t // POOL_HALO - 1), 0)),
                  pl.BlockSpec(memory_space=pl.ANY)],
        out_specs=pl.BlockSpec((tm, pw), lambda i: (i, 0)),
        out_shape=SDS(dproj.shape, dproj.dtype),
        input_output_aliases={2: 0},
        compiler_params=_params(("arbitrary",)),
    )(dq, dq, dproj)


ELEMENTWISE_LANE_CHUNK = 256


def _lane_chunks(width):
    return [slice(c, c + ELEMENTWISE_LANE_CHUNK) for c in range(0, width, ELEMENTWISE_LANE_CHUNK)]


def _conv_pre(xc_ref, xp_ref, w_ref, b_ref, i, cols):
    cur = xc_ref[:, cols]
    prev = jnp.where(i > 0, xp_ref[:, cols], 0.0)
    ext = jnp.concatenate([prev, cur], axis=0)
    taps = [pltpu.roll(ext, CONV_WIDTH - 1 - k, 0)[CONV_HALO:] for k in range(CONV_WIDTH - 1)] + [cur]
    pre = b_ref[:, cols]
    for k in range(CONV_WIDTH):
        pre = pre + w_ref[k:k + 1, cols] * taps[k]
    return pre, taps


def conv_fwd(proj, conv_w, conv_b, col_block, tm, comm=None):
    t = proj.shape[0]
    cd = conv_b.shape[1]
    nh = tm // CONV_HALO

    def body(xc_ref, xp_ref, w_ref, b_ref, o_ref):
        i = pl.program_id(0)
        for cols in _lane_chunks(cd):
            pre, _ = _conv_pre(xc_ref, xp_ref, w_ref, b_ref, i, cols)
            o_ref[:, cols] = _silu(pre)

    outs, couts = _call(
        body, (proj, proj, conv_w, conv_b), name="conv_fwd", grid=(t // tm,),
        in_specs=[pl.BlockSpec((tm, cd), lambda i: (i, col_block)),
                  pl.BlockSpec((CONV_HALO, cd), lambda i: (jnp.maximum(i * nh - 1, 0), col_block)),
                  pl.BlockSpec((CONV_WIDTH, cd), lambda i: (0, 0)),
                  pl.BlockSpec((1, cd), lambda i: (0, 0))],
        out_specs=[pl.BlockSpec((tm, cd), lambda i: (i, 0))],
        out_shape=[SDS((t, cd), F32)], sem=("arbitrary",), comm=comm)
    return outs[0], couts


def conv_bwd_a(dxs, db, dc, proj, conv_w, conv_b, col_block, tm, comm=None):
    t = proj.shape[0]
    cd = conv_b.shape[1]
    sw = dxs.shape[1]
    gn = db.shape[1]
    nh = tm // CONV_HALO

    def body(dx_ref, db_ref, dc_ref, xc_ref, xp_ref, w_ref, b_ref, dp_ref, dw_ref, dbias_ref):
        i = pl.program_id(0)

        @pl.when(i == 0)
        def _():
            dw_ref[...] = jnp.zeros_like(dw_ref)
            dbias_ref[...] = jnp.zeros_like(dbias_ref)

        for cols in _lane_chunks(cd):
            pre, taps = _conv_pre(xc_ref, xp_ref, w_ref, b_ref, i, cols)
            if cols.start < sw:
                dact = dx_ref[:, cols]
            elif cols.start < sw + gn:
                dact = db_ref[:, cols.start - sw:cols.stop - sw]
            else:
                dact = dc_ref[:, cols.start - sw - gn:cols.stop - sw - gn]
            dpre = dact * _dsilu(pre)
            dp_ref[:, cols] = dpre
            dbias_ref[:, cols] += jnp.sum(dpre, axis=0, keepdims=True)
            for k in range(CONV_WIDTH):
                dw_ref[k:k + 1, cols] += jnp.sum(dpre * taps[k], axis=0, keepdims=True)

    return _call(
        body, (dxs, db, dc, proj, proj, conv_w, conv_b), name="conv_bwd_a", grid=(t // tm,),
        in_specs=[pl.BlockSpec((tm, sw), lambda i: (i, 0)), pl.BlockSpec((tm, gn), lambda i: (i, 0)),
                  pl.BlockSpec((tm, gn), lambda i: (i, 0)),
                  pl.BlockSpec((tm, cd), lambda i: (i, col_block)),
                  pl.BlockSpec((CONV_HALO, cd), lambda i: (jnp.maximum(i * nh - 1, 0), col_block)),
                  pl.BlockSpec((CONV_WIDTH, cd), lambda i: (0, 0)),
                  pl.BlockSpec((1, cd), lambda i: (0, 0))],
        out_specs=[pl.BlockSpec((tm, cd), lambda i: (i, 0)),
                   pl.BlockSpec((CONV_WIDTH, cd), lambda i: (0, 0)),
                   pl.BlockSpec((1, cd), lambda i: (0, 0))],
        out_shape=[SDS((t, cd), F32), SDS((CONV_WIDTH, cd), F32), SDS((1, cd), F32)],
        sem=("arbitrary",), comm=comm)


def conv_bwd_b(dpre, conv_w, dproj, col_block, tm):
    t, cd = dpre.shape
    nh = tm // CONV_HALO
    nt = t // tm

    def body(c_ref, n_ref, w_ref, alias_ref, o_ref):
        i = pl.program_id(0)
        n = tm + CONV_HALO
        for cols in _lane_chunks(cd):
            cur = c_ref[:, cols]
            nxt = jnp.where(i < nt - 1, n_ref[:, cols], 0.0)
            ext = jnp.concatenate([cur, nxt], axis=0)
            acc = w_ref[CONV_WIDTH - 1:CONV_WIDTH, cols] * cur
            for k in range(CONV_WIDTH - 1):
                acc = acc + w_ref[k:k + 1, cols] * pltpu.roll(ext, n - (CONV_WIDTH - 1 - k), 0)[:tm]
            o_ref[:, cols] = acc.astype(BF16)

    return pl.pallas_call(
        body, name="conv_bwd_b", grid=(nt,),
        in_specs=[pl.BlockSpec((tm, cd), lambda i: (i, 0)),
                  pl.BlockSpec((CONV_HALO, cd), lambda i: (jnp.minimum((i + 1) * nh, t // CONV_HALO - 1), 0)),
                  pl.BlockSpec((CONV_WIDTH, cd), lambda i: (0, 0)),
                  pl.BlockSpec(memory_space=pl.ANY)],
        out_specs=pl.BlockSpec((tm, cd), lambda i: (i, col_block)),
        out_shape=SDS(dproj.shape, dproj.dtype),
        input_output_aliases={3: 0},
        compiler_params=_params(("arbitrary",)),
    )(dpre, dpre, conv_w, dproj)


def _softplus(v):
    return jnp.maximum(v, 0.0) + jnp.log(1.0 + jnp.exp(-jnp.abs(v)))


def _ssd_chunk_terms(dtr_ref, bias_ref, a_ref, n_heads):
    q = SSD_CHUNK
    lane = lax.broadcasted_iota(jnp.int32, (1, LANES), 1)
    pre = dtr_ref[...] + bias_ref[...]
    dt = jnp.where(lane < n_heads, _softplus(pre), 0.0)
    a = jnp.where(lane < n_heads, -jnp.exp(a_ref[...]), 0.0)
    row = lax.broadcasted_iota(jnp.int32, (q, q), 0)
    col = lax.broadcasted_iota(jnp.int32, (q, q), 1)
    causal = row >= col
    acs = jnp.dot(causal.astype(F32), dt * a, precision=HIGHEST, preferred_element_type=F32)
    last = acs[q - 1:q, :]
    return dict(pre=pre, dt=dt, a=a, acs=acs, acs_t=acs.T, eacs=jnp.exp(acs), dstate=jnp.exp(last - acs),
                cdec=jnp.exp(last), causal=causal, diag=row == col, lane=lane)


_TERM_FIELDS = ("pre", "dt", "acs", "acs_t", "eacs", "dstate", "cdec")


def _prefetched_terms(step, dtr_ref, dtn_ref, bias_ref, a_ref, n_heads, terms_ref):
    q = SSD_CHUNK

    def store(slot, tm_):
        for f, name in enumerate(_TERM_FIELDS):
            terms_ref[slot, f] = jnp.broadcast_to(tm_[name], (q, LANES))

    @pl.when(step == 0)
    def _():
        store(0, _ssd_chunk_terms(dtr_ref, bias_ref, a_ref, n_heads))

    slot = lax.rem(step, 2)
    nxt = _ssd_chunk_terms(dtn_ref, bias_ref, a_ref, n_heads)
    tm_ = dict(nxt, **{name: terms_ref[slot, f] for f, name in enumerate(_TERM_FIELDS)})
    tm_["cdec"] = tm_["cdec"][0:1]
    return tm_, lambda: store(1 - slot, nxt)


def _pair_cols(lo, v, h):
    if v.shape[0] < 8:
        return jnp.where(lo, v[:, h:h + 1], v[:, h + 1:h + 2])
    idx = jnp.broadcast_to(jnp.where(lo, h, h + 1).astype(jnp.int32), v.shape)
    return jnp.take_along_axis(v, idx, axis=1, mode="promise_in_bounds")


def _pair_decay(tm_, cb, h):
    l0 = jnp.exp(jnp.where(tm_["causal"], tm_["acs"][:, h:h + 1] - tm_["acs_t"][h:h + 1, :], -jnp.inf))
    l1 = jnp.exp(jnp.where(tm_["causal"], tm_["acs"][:, h + 1:h + 2] - tm_["acs_t"][h + 1:h + 2, :], -jnp.inf))
    return l0, l1, jnp.concatenate([cb * l0, cb * l1], axis=1)


def _pair_decay_t(tm_, cbt, h):
    upper = jnp.logical_not(tm_["causal"]) | tm_["diag"]
    t0 = jnp.exp(jnp.where(upper, tm_["acs_t"][h:h + 1, :] - tm_["acs"][:, h:h + 1], -jnp.inf))
    t1 = jnp.exp(jnp.where(upper, tm_["acs_t"][h + 1:h + 2, :] - tm_["acs"][:, h + 1:h + 2], -jnp.inf))
    return jnp.concatenate([cbt * t0, cbt * t1], axis=0).astype(BF16)


def _block_diag(lo, xdt):
    return jnp.concatenate([jnp.where(lo, xdt, 0.0), jnp.where(lo, 0.0, xdt)], axis=0).astype(BF16)


def ssd_fwd(xbc, proj, dt_raw, dt_bias, a_log, d_full, norm_w, mixed, z_block, n_heads, comm=None):
    t = xbc.shape[0]
    q = SSD_CHUNK
    gn = SSD_GROUPS * SSD_STATE
    sw = n_heads * SSD_HEAD_DIM
    gw = sw // SSD_GROUPS
    n_pairs = n_heads // 2
    pairs_per_group = n_pairs // SSD_GROUPS
    nc = t // q
    bblk = sw // gn

    def body(xs_ref, b_ref, c_ref, dtr_ref, dtn_ref, bias_ref, a_ref, z_ref, dsk_ref, nw_ref, alias_ref,
             y_ref, sin_ref, m_ref, state, terms_ref):
        @pl.when(pl.program_id(0) == 0)
        def _():
            state[...] = jnp.zeros_like(state)

        tm_, keep_next = _prefetched_terms(pl.program_id(0), dtr_ref, dtn_ref, bias_ref, a_ref, n_heads, terms_ref)
        lo = tm_["lane"] < SSD_HEAD_DIM
        for g in range(SSD_GROUPS):
            gcols = slice(g * SSD_STATE, (g + 1) * SSD_STATE)
            bg = b_ref[:, gcols].astype(BF16)
            bg_t = b_ref[:, gcols].T.astype(BF16)
            cg = c_ref[:, gcols].astype(BF16)
            cb = lax.dot_general(cg, bg, NT, preferred_element_type=F32)
            for j in range(pairs_per_group):
                p = g * pairs_per_group + j
                h = 2 * p
                pcols = slice(p * LANES, (p + 1) * LANES)
                _, _, mcat = _pair_decay(tm_, cb, h)
                xdt = xs_ref[:, pcols] * _pair_cols(lo, tm_["dt"], h)
                ydiag = jnp.dot(mcat.astype(BF16), _block_diag(lo, xdt), preferred_element_type=F32)
                st = state[p]
                sin_ref[0, p] = st
                yoff = jnp.dot(cg, st.astype(BF16), preferred_element_type=F32) * _pair_cols(lo, tm_["eacs"], h)
                y_ref[:, pcols] = ydiag + yoff
                xw = (xdt * _pair_cols(lo, tm_["dstate"], h)).astype(BF16)
                state[p] = st * _pair_cols(lo, tm_["cdec"], h) + jnp.dot(bg_t, xw, preferred_element_type=F32)
        keep_next()
        for g in range(SSD_GROUPS):
            cols = slice(g * gw, (g + 1) * gw)
            blk = (y_ref[:, cols] + dsk_ref[:, cols] * xs_ref[:, cols]) * _silu(z_ref[:, cols])
            r = lax.rsqrt(jnp.mean(blk * blk, axis=-1, keepdims=True) + NORM_EPS)
            m_ref[:, cols] = (blk * r * nw_ref[:, cols]).astype(BF16)

    vec = pl.BlockSpec((1, LANES), lambda c: (0, 0))
    wide = pl.BlockSpec((1, sw), lambda c: (0, 0))
    return _call(
        body, (xbc, xbc, xbc, dt_raw, dt_raw, dt_bias, a_log, proj, d_full, norm_w, mixed), name="ssd_fwd", grid=(nc,),
        in_specs=[pl.BlockSpec((q, sw), lambda c: (c, 0)),
                  pl.BlockSpec((q, gn), lambda c: (c, bblk)),
                  pl.BlockSpec((q, gn), lambda c: (c, bblk + 1)),
                  pl.BlockSpec((q, LANES), lambda c: (c, 0)),
                  pl.BlockSpec((q, LANES), lambda c: (jnp.minimum(c + 1, nc - 1), 0)), vec, vec,
                  pl.BlockSpec((q, sw), lambda c: (c, z_block)), wide, wide, pl.BlockSpec(memory_space=pl.ANY)],
        out_specs=[pl.BlockSpec((q, sw), lambda c: (c, 0)),
                   pl.BlockSpec((1, n_pairs, SSD_STATE, LANES), lambda c: (c, 0, 0, 0)),
                   pl.BlockSpec((q, sw), lambda c: (c, 1))],
        out_shape=[SDS((t, sw), F32), SDS((nc, n_pairs, SSD_STATE, LANES), F32), SDS(mixed.shape, mixed.dtype)],
        scratch_shapes=[pltpu.VMEM((n_pairs, SSD_STATE, LANES), F32), pltpu.VMEM((2, len(_TERM_FIELDS), q, LANES), F32)],
        sem=("arbitrary",), comm=comm, aliases={10: 2})


def ssd_bwd(dmixed, y, xbc, proj, dt_raw, dt_bias, a_log, d_full, norm_w, s_in, dproj, z_block, n_heads, comm=None):
    t = xbc.shape[0]
    q = SSD_CHUNK
    gn = SSD_GROUPS * SSD_STATE
    sw = n_heads * SSD_HEAD_DIM
    gw = sw // SSD_GROUPS
    n_pairs = n_heads // 2
    pairs_per_group = n_pairs // SSD_GROUPS
    nc = t // q
    bblk = sw // gn

    def body(d3_ref, y_ref, xs_ref, b_ref, c_ref, dtr_ref, dtn_ref, bias_ref, a_ref, dsk_ref, sin_ref, z_ref, nw_ref, alias_ref,
             dxs_ref, db_ref, dc_ref, ddtr_ref, dbias_ref, dalog_ref, dz_ref, dnw_ref, dd_ref,
             dstate, tbuf, xbuf, rbuf, acc_a, acc_b, sel_ref, terms_ref, dy_ref, acc_d):
        i = pl.program_id(0)

        @pl.when(i == 0)
        def _():
            dstate[...] = jnp.zeros_like(dstate)
            rbuf[...] = jnp.zeros_like(rbuf)
            acc_a[...] = jnp.zeros_like(acc_a)
            acc_b[...] = jnp.zeros_like(acc_b)
            acc_d[...] = jnp.zeros_like(acc_d)
            dnw_ref[...] = jnp.zeros_like(dnw_ref)
            sel_ref[...] = _head_selector(sw, SSD_HEAD_DIM)

        for g in range(SSD_GROUPS):
            cols = slice(g * gw, (g + 1) * gw)
            xs = xs_ref[:, cols]
            zv = z_ref[:, cols]
            y1 = y_ref[:, cols] + dsk_ref[:, cols] * xs
            sz = _silu(zv)
            blk = y1 * sz
            r = lax.rsqrt(jnp.mean(blk * blk, axis=-1, keepdims=True) + NORM_EPS)
            n = blk * r
            dg = d3_ref[:, cols]
            dnw_ref[:, cols] += jnp.sum(dg * n, axis=0, keepdims=True)
            dy2 = _norm_bwd(dg * nw_ref[:, cols], n, r)
            dz_ref[:, cols] = (dy2 * y1 * _dsilu(zv)).astype(BF16)
            dy1 = dy2 * sz
            dy_ref[:, cols] = dy1
            acc_d[0:1, cols] += jnp.sum(dy1 * xs, axis=0, keepdims=True)

        tm_, keep_next = _prefetched_terms(i, dtr_ref, dtn_ref, bias_ref, a_ref, n_heads, terms_ref)
        lane = tm_["lane"]
        lo = lane < SSD_HEAD_DIM
        head_row = lax.broadcasted_iota(jnp.int32, (LANES, 1), 0)
        rows = jnp.zeros((q, LANES), F32)
        cols_t = jnp.zeros((LANES, q), F32)
        for g in range(SSD_GROUPS):
            gcols = slice(g * SSD_STATE, (g + 1) * SSD_STATE)
            bg = b_ref[:, gcols].astype(BF16)
            cg = c_ref[:, gcols].astype(BF16)
            cg_t = c_ref[:, gcols].T.astype(BF16)
            cb = lax.dot_general(cg, bg, NT, preferred_element_type=F32)
            cbt = lax.dot_general(bg, cg, NT, preferred_element_type=F32)
            dcb = jnp.zeros((q, q), F32)
            db_acc = jnp.zeros((q, SSD_STATE), F32)
            dc_acc = jnp.zeros((q, SSD_STATE), F32)
            for j in range(pairs_per_group):
                p = g * pairs_per_group + j
                h = 2 * p
                pcols = slice(p * LANES, (p + 1) * LANES)
                l0, l1, mcat = _pair_decay(tm_, cb, h)
                xp = xs_ref[:, pcols]
                dtp = _pair_cols(lo, tm_["dt"], h)
                xdt = xp * dtp
                xbd = _block_diag(lo, xdt)
                dyp = dy_ref[:, pcols]
                dyb = dyp.astype(BF16)
                dsb = _pair_cols(lo, tm_["dstate"], h)
                cdr = _pair_cols(lo, tm_["cdec"], h)
                eb = _pair_cols(lo, tm_["eacs"], h)
                st = sin_ref[0, p]
                stb = st.astype(BF16)
                dst = dstate[p]
                dstb = dst.astype(BF16)
                dye = (dyp * eb).astype(BF16)
                both = jnp.dot(_pair_decay_t(tm_, cbt, h), dyb, preferred_element_type=F32)
                dx_state = jnp.dot(bg, dstb, preferred_element_type=F32) * dsb
                dxdt = jnp.where(lo, both[:q], both[q:]) + dx_state
                dmcat = lax.dot_general(dyb, xbd, NT, preferred_element_type=F32)
                dcb = dcb + dmcat[:, :q] * l0 + dmcat[:, q:] * l1
                dseg = dmcat * mcat
                csum = jnp.sum(dseg, axis=0, keepdims=True)
                rows = (rows + jnp.where(lane == h, jnp.sum(dseg[:, :q], axis=1, keepdims=True), 0.0)
                        + jnp.where(lane == h + 1, jnp.sum(dseg[:, q:], axis=1, keepdims=True), 0.0))
                cols_t = (cols_t + jnp.where(head_row == h, csum[:, :q], 0.0)
                          + jnp.where(head_row == h + 1, csum[:, q:], 0.0))
                dc_acc = dc_acc + lax.dot_general(dye, stb, NT, preferred_element_type=F32)
                db_acc = db_acc + lax.dot_general((xdt * dsb).astype(BF16), dstb, NT, preferred_element_type=F32)
                yoff = jnp.dot(cg, stb, preferred_element_type=F32) * eb
                tbuf[:, pcols] = dyp * yoff - xdt * dx_state
                xbuf[:, pcols] = dxdt * xp
                rbuf[0:1, pcols] = (jnp.sum(xdt * dx_state, axis=0, keepdims=True)
                                    + cdr * jnp.sum(dst * st, axis=0, keepdims=True))
                dxs_ref[:, pcols] = dxdt * dtp + dyp * dsk_ref[:, pcols]
                dstate[p] = dst * cdr + jnp.dot(cg_t, dye, preferred_element_type=F32)
            dcbb = dcb.astype(BF16)
            dc_ref[:, gcols] = dc_acc + jnp.dot(dcbb, bg, preferred_element_type=F32)
            db_ref[:, gcols] = db_acc + lax.dot_general(dcbb, cg, TN, preferred_element_type=F32)

        sel = sel_ref[...]
        dacs = rows - cols_t.T + _split_dot(tbuf[...], sel)
        carry = _split_dot(rbuf[...], sel)[0:1]
        anti = jnp.logical_not(tm_["causal"]) | tm_["diag"]
        da = jnp.dot(anti.astype(F32), dacs, precision=HIGHEST, preferred_element_type=F32) + carry
        ddt = da * tm_["a"] + _split_dot(xbuf[...], sel)
        ddtr = jnp.where(tm_["lane"] < n_heads, ddt * jax.nn.sigmoid(tm_["pre"]), 0.0)
        ddtr_ref[...] = ddtr.astype(BF16)
        acc_b[...] += jnp.sum(ddtr, axis=0, keepdims=True)
        acc_a[...] += jnp.sum(da * tm_["dt"], axis=0, keepdims=True)
        keep_next()

        @pl.when(i == nc - 1)
        def _():
            dbias_ref[...] = acc_b[...]
            dalog_ref[...] = acc_a[...] * tm_["a"]
            dd_ref[...] = _split_dot(acc_d[...], sel)[0:1]

    vec = pl.BlockSpec((1, LANES), lambda i: (0, 0))
    full = pl.BlockSpec((1, sw), lambda i: (0, 0))
    wide = pl.BlockSpec((q, sw), lambda i: (nc - 1 - i, 0))
    return _call(
        body, (dmixed, y, xbc, xbc, xbc, dt_raw, dt_raw, dt_bias, a_log, d_full, s_in, proj, norm_w, dproj),
        name="ssd_bwd", grid=(nc,),
        in_specs=[pl.BlockSpec((q, sw), lambda i: (nc - 1 - i, 1)), wide, wide,
                  pl.BlockSpec((q, gn), lambda i: (nc - 1 - i, bblk)),
                  pl.BlockSpec((q, gn), lambda i: (nc - 1 - i, bblk + 1)),
                  pl.BlockSpec((q, LANES), lambda i: (nc - 1 - i, 0)),
                  pl.BlockSpec((q, LANES), lambda i: (jnp.maximum(nc - 2 - i, 0), 0)), vec, vec, full,
                  pl.BlockSpec((1, n_pairs, SSD_STATE, LANES), lambda i: (nc - 1 - i, 0, 0, 0)),
                  pl.BlockSpec((q, sw), lambda i: (nc - 1 - i, z_block)), full, pl.BlockSpec(memory_space=pl.ANY)],
        out_specs=[wide, pl.BlockSpec((q, gn), lambda i: (nc - 1 - i, 0)), pl.BlockSpec((q, gn), lambda i: (nc - 1 - i, 0)),
                   pl.BlockSpec((q, LANES), lambda i: (nc - 1 - i, 0)), vec, vec,
                   pl.BlockSpec((q, sw), lambda i: (nc - 1 - i, z_block)), full, vec],
        out_shape=[SDS((t, sw), F32), SDS((t, gn), F32), SDS((t, gn), F32), SDS((t, LANES), BF16),
                   SDS((1, LANES), F32), SDS((1, LANES), F32), SDS(dproj.shape, dproj.dtype), SDS((1, sw), F32),
                   SDS((1, LANES), F32)],
        scratch_shapes=[pltpu.VMEM((n_pairs, SSD_STATE, LANES), F32), pltpu.VMEM((q, sw), F32), pltpu.VMEM((q, sw), F32),
                        pltpu.VMEM((8, sw), F32), pltpu.VMEM((1, LANES), F32), pltpu.VMEM((1, LANES), F32),
                        pltpu.VMEM((sw, LANES), BF16), pltpu.VMEM((2, len(_TERM_FIELDS), q, LANES), F32),
                        pltpu.VMEM((q, sw), F32), pltpu.VMEM((8, sw), F32)],
        sem=("arbitrary",), comm=comm, aliases={13: 6})


GATE_BLOCK = 1
Z_BLOCK = 2
CONV_BLOCK = 2


def _tiles(t):
    mm = dict(in_proj=(min(1024, t), 1024), dt_proj=(min(512, t), LANES), out_proj=(min(512, t), 1024),
              d_mixed=(min(512, t), 2048), dh=(min(512, t), 3072), dw_out=(512, 1024), dw_main=(1024, 1024),
              dw_dt=(512, LANES))
    return min(512, t), mm


def _place():
    x, y, c = lax.axis_index("x"), lax.axis_index("y"), lax.axis_index("c")
    return x, y, c, [(1 - x, y), (x, 1 - y), (1 - x, 1 - y)]


def gather_spread(shards, layer, rows=None, carry=None):
    n = len(shards)

    def make(ins, outs, ss, rs):
        x, y, c, chips = _place()
        mine = 4 * x + 2 * y + c
        peers = [(x, y, 1 - c)] + [(px, py, c) for px, py in chips]
        sends, locals_, arrivals = [], [], []
        for a in range(n):
            def place(ref, idx):
                return ref.at[idx] if rows is None else ref.at[idx, pl.ds(rows[0], rows[1])]

            src = place(ins[a], layer)
            locals_.append(pltpu.make_async_copy(src, place(outs[a], mine), ss.at[5 * a + 4]))
            for j, (px, py, pc) in enumerate(peers):
                sends.append(_remote(src, place(outs[a], mine), ss, rs, 5 * a + j, (px, py, pc)))
                arrivals.append(_remote(src, place(outs[a], 4 * px + 2 * py + pc), ss, rs, 5 * a + j, (px, py, pc)))
        return sends, locals_, arrivals

    return Comm(list(shards) + list(carry or []), [SDS((N_DEV,) + s.shape[1:], s.dtype) for s in shards],
                {n + a: a for a in range(n)} if carry else {}, 5 * n, make)


def gather_pass_on(gathered):
    def make(ins, outs, ss, rs):
        x, y, c, chips = _place()
        sends, arrivals = [], []
        for a in range(len(outs)):
            for j, (px, py) in enumerate(chips):
                blk, other = 4 * px + 2 * py + c, 4 * px + 2 * py + (1 - c)
                sends.append(_remote(outs[a].at[blk], outs[a].at[blk], ss, rs, 3 * a + j, (x, y, 1 - c)))
                arrivals.append(_remote(outs[a].at[other], outs[a].at[other], ss, rs, 3 * a + j, (x, y, 1 - c)))
        return sends, [], arrivals

    return Comm(gathered, [SDS(g.shape, g.dtype) for g in gathered], {a: a for a in range(len(gathered))},
                3 * len(gathered), make)


def sibling_swap(sends_):
    def make(ins, outs, ss, rs):
        x, y, c, _ = _place()
        cps = [_remote(ins[a], outs[a], ss, rs, a, (x, y, 1 - c)) for a in range(len(ins))]
        return cps, [], cps

    return Comm(sends_, [SDS(s.shape, s.dtype) for s in sends_], {}, len(sends_), make)


def chips_scatter(slabs, rows=None, carry=None):
    n = len(slabs)

    def make(ins, outs, ss, rs):
        x, y, c, chips = _place()
        mychip = 2 * x + y
        sends, arrivals = [], []

        def part(ref, slot):
            return ref.at[slot] if rows is None else ref.at[slot, pl.ds(rows[0], rows[1])]

        for a in range(n):
            for j, (px, py) in enumerate(chips):
                to_there = lax.rem(2 * px + py - mychip + 4, 4) - 1
                from_here = lax.rem(mychip - 2 * px - py + 4, 4) - 1
                sends.append(_remote(part(ins[a], to_there), part(outs[a], from_here), ss, rs, 3 * a + j, (px, py, c)))
                arrivals.append(_remote(part(ins[a], to_there), part(outs[a], to_there), ss, rs, 3 * a + j, (px, py, c)))
        return sends, [], arrivals

    return Comm(list(slabs) + list(carry or []), [SDS(s.shape, s.dtype) for s in slabs],
                {n + a: a for a in range(n)} if carry else {}, 3 * n, make)


def comm_only(comm, name):
    def body():
        pass

    return _call(body, (), name=name, grid=(), in_specs=[], out_specs=[], out_shape=[], comm=comm)[1]


W_IN_GATHER_EIGHTHS = (3, 1, 2, 2)


def layer_fwd(x, p, layer=0, shards=None, finish=None, nxt=False, first=False, normed=None, next_pre_w=None):
    t = x.shape[0]
    tm, mm = _tiles(t)
    n_heads = p["d_full"].shape[1] // SSD_HEAD_DIM
    travel = shards is not None
    nxt, first = nxt and travel, first and travel
    rows = shards["in"][0].shape[1] if travel else 0
    cuts = [0]
    for eighths in W_IN_GATHER_EIGHTHS:
        cuts.append(cuts[-1] + rows * eighths // 8)
    assert cuts[-1] == rows

    def next_w_in(part, carry):
        return gather_spread(shards["in"], layer + 1, rows=(cuts[part], cuts[part + 1] - cuts[part]), carry=carry) if nxt else None

    h, r_pre = normed if normed is not None else rms_fwd(x, p["pre_w"], tm)
    proj, got = mm_nn(h, p["w_main"], F32, *mm["in_proj"], "in_proj", merge_comms([
        gather_spread(shards["small"], layer) if first else None,
        gather_spread(shards["out"], layer) if travel else None, next_w_in(0, None)]))
    n_small = len(shards["small"]) if first else 0
    got_small, got_out, got_in = got[:n_small], got[n_small:n_small + 1], got[n_small + 1:]
    dt_raw, got_small = mm_nn(h, p["w_dt"], F32, *mm["dt_proj"], "dt_proj", gather_pass_on(got_small) if first else None)
    if first:
        p = dict(p, **finish["small"](got_small))
    mixed = pool_fwd(proj, p["mixw"], p["pscale"], tm)
    xbc, got_in = conv_fwd(proj, p["conv_w"], p["conv_b"], CONV_BLOCK, tm, next_w_in(1, got_in))
    (y, s_in, mixed), got = ssd_fwd(xbc, proj, dt_raw, p["dt_bias"], p["a_log"], p["d_full"], p["norm_w"], mixed, Z_BLOCK,
                                    n_heads, merge_comms([gather_pass_on(got_out) if travel else None, next_w_in(2, got_in)]))
    if travel:
        p = dict(p, **finish["out"](got[:1]))
    out, got = mm_nn(mixed, p["w_out"], F32, *mm["out_proj"], "out_proj",
                     merge_comms([next_w_in(3, got[1:]), gather_spread(shards["small"], layer + 1)]) if nxt else None)
    (x_next, r_post, *next_normed), gathered = post_fwd(out, x, p["post_w"], tm, gather_pass_on(got) if nxt else None,
                                                        next_pre_w)
    return (x_next, next_normed or None), dict(x=x, h=h, r_pre=r_pre, proj=proj, dt_raw=dt_raw, xbc=xbc, y=y, s_in=s_in,
                                               mixed=mixed, out=out, r_post=r_post), gathered, p


def _pair_sums(own, got):
    return [pair_sum(o, r, min(256, o.shape[1]), "pair_sum") for o, r in zip(own, got)]


def layer_bwd(g, s, p, split_in, split_rest, pending=None, last=False, post=None, below=None):
    t = g.shape[0]
    tm, mm = _tiles(t)
    d = g.shape[1]
    n_heads = p["d_full"].shape[1] // SSD_HEAD_DIM
    d_out, d_post = post if post is not None else post_bwd(g, s["out"], s["r_post"], p["post_w"], tm)
    dmixed, got_sib = mm_nt(d_out, p["w_out"], F32, *mm["d_mixed"], d, "d_mixed", sibling_swap(pending[1]) if pending else None)
    chip_sums = _pair_sums(pending[0], got_sib) if pending else []
    rows_in = chip_sums[0].shape[1] if pending else 0
    early = rows_in * 3 // 8 if last else 0
    dw_out, got_early = mm_tn(s["mixed"], d_out, *mm["dw_out"], "dw_out",
                              chips_scatter(chip_sums[:1], rows=(0, early)) if pending and last else None)
    dproj, dq, d_pscale, d_mixw = pool_bwd_a(dmixed, s["proj"], p["mixw"], p["pscale"], tm)
    dproj = pool_bwd_b(dq, dproj, tm)
    own_rest, send_rest = split_rest(dw_out, d_mixw)
    cut_in = rows_in if last else rows_in // 2
    (dxs, db, dc, ddtr, d_dtb, d_alog, dproj, d_norm, d_dskip), got = ssd_bwd(
        dmixed, s["y"], s["xbc"], s["proj"], s["dt_raw"], p["dt_bias"], p["a_log"], p["d_full"], p["norm_w"], s["s_in"],
        dproj, Z_BLOCK, n_heads,
        merge_comms([chips_scatter(chip_sums[:1], rows=(early, cut_in - early), carry=got_early or None) if pending else None,
                     sibling_swap(send_rest) if last else None]))
    got_first, my_sib_rest = (got[:1], got[1:]) if pending else ([], got)
    rest_comm = chips_scatter(chip_sums[1:]) if pending else None
    (dpre, d_convw, d_convb), got_rest = conv_bwd_a(dxs, db, dc, s["proj"], p["conv_w"], p["conv_b"], CONV_BLOCK, tm,
                                                    rest_comm if last else None)
    dproj = conv_bwd_b(dpre, p["conv_w"], dproj, CONV_BLOCK, tm)
    dw_main, got = mm_tn(s["h"], dproj, *mm["dw_main"], "dw_main",
                         chips_scatter(_pair_sums(own_rest, my_sib_rest)) if last else rest_comm)
    got_rest, my_chips_rest = (got_rest, got) if last else (got, [])
    dw_dt, _ = mm_tn(s["h"], ddtr, *mm["dw_dt"], "dw_dt")
    own_in, send_in = split_in(dw_main, dw_dt)
    my_sib_in = comm_only(sibling_swap(send_in), "grads_to_sibling") if last else []
    if last:
        my_sums = _pair_sums(own_in, my_sib_in)
        rows_own = my_sums[0].shape[1]
        cut_own = rows_own * 13 // 16
        dh_comm = chips_scatter(my_sums, rows=(0, cut_own))
    else:
        dh_comm = chips_scatter(chip_sums[:1], rows=(cut_in, rows_in - cut_in), carry=got_first) if pending else None
    dh, got = mm_nt(dproj, p["w_main"], F32, mm["dh"][0], d, mm["dh"][1], "dh_main", dh_comm, extra=(ddtr, p["w_dt"]))
    my_chips_in, got_first = (got, got_first) if last else ([], got if pending else got_first)
    (gx, d_pre, *post_below), got = rms_bwd(
        dh, s["x"], s["r_pre"], p["pre_w"], g, tm,
        chips_scatter(my_sums, rows=(cut_own, rows_own - cut_own), carry=my_chips_in) if last else None, below)
    my_chips_in = got if last else my_chips_in
    small = dict(pre_w=d_pre, pscale=d_pscale, conv_w=d_convw, conv_b=d_convb, dt_bias=d_dtb, a_log=d_alog,
                 d_skip=d_dskip, norm_w=d_norm, post_w=d_post)
    done = [(got_sib, got_first + got_rest)] if pending else [None]
    if last:
        done.append((my_sib_in + my_sib_rest, my_chips_in + my_chips_rest))
    return (gx, post_below or None), small, (own_in + own_rest, send_in + send_rest), done


def _two_level_gather(x_refs, out_slots, send_sems, recv_sems, local_sems):
    x, y, c, chips = _place()
    me, sibling = (x, y, c), (x, y, 1 - c)
    n = len(x_refs)

    def copy(a, k, block, to, src=None):
        return pltpu.make_async_remote_copy(
            src_ref=out_slots[a](*block) if src is None else src, dst_ref=out_slots[a](*block),
            send_sem=send_sems.at[7 * a + k], recv_sem=recv_sems.at[7 * a + k], device_id=to, device_id_type=MESH)

    mine = [pltpu.make_async_copy(x_refs[a], out_slots[a](*me), local_sems.at[a]) for a in range(n)]
    for cp in mine:
        cp.start()
    first = []
    for a in range(n):
        first.append(copy(a, 0, me, sibling, src=x_refs[a]))
        first += [copy(a, 1 + j, me, (*chip, c), src=x_refs[a]) for j, chip in enumerate(chips)]
    for cp in first:
        cp.start()
    passed = []
    for j, chip in enumerate(chips):
        for a in range(n):
            copy(a, 1 + j, (*chip, c), me).wait_recv()
            fwd = copy(a, 4 + j, (*chip, c), sibling)
            fwd.start()
            passed.append(fwd)
    for a in range(n):
        copy(a, 0, sibling, me).wait_recv()
        for j, chip in enumerate(chips):
            copy(a, 4 + j, (*chip, 1 - c), me).wait_recv()
    for cp in first + passed:
        cp.wait_send()
    for cp in mine:
        cp.wait()


def all_gather_hbm(shards, name):
    n = len(shards)

    def body(*refs):
        x_refs, out_refs = refs[:n], refs[n:2 * n]
        send_sems, recv_sems, local_sems = refs[2 * n:]
        slots = [lambda px, py, pc, o=o: o.at[:, 4 * px + 2 * py + pc] for o in out_refs]
        _two_level_gather(x_refs, slots, send_sems, recv_sems, local_sems)

    hbm = pl.BlockSpec(memory_space=pl.ANY)
    return pl.pallas_call(
        body, name=name,
        out_shape=[SDS((s.shape[0], N_DEV) + s.shape[1:], s.dtype) for s in shards],
        in_specs=[hbm] * n, out_specs=[hbm] * n,
        scratch_shapes=[pltpu.SemaphoreType.DMA((7 * n,)), pltpu.SemaphoreType.DMA((7 * n,)), pltpu.SemaphoreType.DMA((n,))],
    )(*shards)


def all_gather_vmem(block, name):
    r, c_ = block.shape

    def body(x_ref, out_ref, send_sems, recv_sems, local_sems):
        _two_level_gather([x_ref], [lambda px, py, pc: out_ref.at[4 * px + 2 * py + pc]], send_sems, recv_sems, local_sems)

    return pl.pallas_call(
        body, name=name, out_shape=SDS((N_DEV, r, c_), block.dtype),
        in_specs=[pl.BlockSpec(memory_space=pltpu.VMEM)], out_specs=pl.BlockSpec(memory_space=pltpu.VMEM),
        scratch_shapes=[pltpu.SemaphoreType.DMA((7,)), pltpu.SemaphoreType.DMA((7,)), pltpu.SemaphoreType.DMA((1,))],
        compiler_params=_params(),
    )(block)


def _block_tiles(cols):
    base = [(cols * i) // LANES for i in range(N_DEV)]
    ends = [-((-cols * (i + 1)) // LANES) for i in range(N_DEV)]
    return base, ends, max(e - b for b, e in zip(base, ends))


def _my_lane_offset(cols):
    me = 4 * lax.axis_index("x") + 2 * lax.axis_index("y") + lax.axis_index("c")
    return lax.rem(cols * me, LANES)


def shift_cast(w, tr):
    nl, r, cols = w.shape
    width = _block_tiles(cols)[2] * LANES

    def body(x_ref, o_ref, pad):
        pad[:, width - LANES:] = jnp.zeros((tr, LANES), F32)
        pad[:, :cols] = x_ref[...]
        o_ref[...] = pltpu.roll(pad[...], _my_lane_offset(cols), 1).astype(BF16)

    assert width - LANES <= cols
    return pl.pallas_call(
        body, name="shift_cast", grid=(nl, r // tr),
        in_specs=[pl.BlockSpec((pl.Squeezed(), tr, cols), lambda l, i: (l, i, 0))],
        out_specs=pl.BlockSpec((pl.Squeezed(), tr, width), lambda l, i: (l, i, 0)),
        out_shape=SDS((nl, r, width), BF16), scratch_shapes=[pltpu.VMEM((tr, width), F32)],
        compiler_params=_params(("arbitrary", "arbitrary")))(w)


def assemble_w_in(blocks, cols, n_tail, tr):
    _, r, width = blocks.shape
    base, ends, _ = _block_tiles(cols)
    total = ends[-1]
    main_tiles = (N_DEV * cols - n_tail) // LANES
    assert main_tiles == total - 1 and (N_DEV * cols - n_tail) % LANES == 0

    def body(b_ref, main_ref, tail_ref):
        for tile in range(total):
            parts = [b_ref[i, :, (tile - base[i]) * LANES:(tile - base[i] + 1) * LANES]
                     for i in range(N_DEV) if base[i] <= tile < ends[i]]
            val = parts[0] if len(parts) == 1 else parts[0] + parts[1]
            if tile < main_tiles:
                main_ref[:, tile * LANES:(tile + 1) * LANES] = val
            else:
                tail_ref[...] = val

    return pl.pallas_call(
        body, name="assemble_w_in", grid=(r // tr,),
        in_specs=[pl.BlockSpec((N_DEV, tr, width), lambda i: (0, i, 0))],
        out_specs=[pl.BlockSpec((tr, main_tiles * LANES), lambda i: (i, 0)), pl.BlockSpec((tr, LANES), lambda i: (i, 0))],
        out_shape=[SDS((r, main_tiles * LANES), blocks.dtype), SDS((r, LANES), blocks.dtype)],
        compiler_params=_params(("arbitrary",)),
    )(blocks)


def grad_blocks(dw_main, dw_tail, cols, tr):
    r = dw_main.shape[0]
    base, _, tpb = _block_tiles(cols)
    width = tpb * LANES

    def body(m_ref, t_ref, own_ref, send_ref):
        cat = jnp.concatenate([m_ref[...], t_ref[...]], axis=1)
        south = lax.axis_index("c") == 0
        for k in range(N_DEV // 2):
            a = cat[:, base[2 * k] * LANES:base[2 * k] * LANES + width]
            b = cat[:, base[2 * k + 1] * LANES:base[2 * k + 1] * LANES + width]
            own_ref[k] = jnp.where(south, a, b)
            send_ref[k] = jnp.where(south, b, a).astype(BF16)

    return pl.pallas_call(
        body, name="grad_blocks", grid=(r // tr,),
        in_specs=[pl.BlockSpec((tr, dw_main.shape[1]), lambda i: (i, 0)), pl.BlockSpec((tr, LANES), lambda i: (i, 0))],
        out_specs=[pl.BlockSpec((N_DEV // 2, tr, width), lambda i: (0, i, 0))] * 2,
        out_shape=[SDS((N_DEV // 2, r, width), F32), SDS((N_DEV // 2, r, width), BF16)],
        compiler_params=_params(("arbitrary",)),
    )(dw_main, dw_tail)


def _adamw(w, g, m, v):
    m = ADAM_B1 * m + (1.0 - ADAM_B1) * g
    v = ADAM_B2 * v + (1.0 - ADAM_B2) * jnp.square(g)
    m_hat = m / (1.0 - ADAM_B1 ** ADAM_STEP)
    v_hat = v / (1.0 - ADAM_B2 ** ADAM_STEP)
    delta = -ADAM_LR * (m_hat / (jnp.sqrt(v_hat) + ADAM_EPS) + ADAM_WD * w)
    return delta, m, v


def _my_chip():
    return 2 * lax.axis_index("x") + lax.axis_index("y")


def pair_sum(own, got, tr, name):
    k, r, c_ = own.shape
    others = lax.rem(_my_chip() + 1 + jnp.arange(k - 1, dtype=jnp.int32), k)

    def body(others_ref, a_ref, b_ref, o_ref):
        o_ref[...] = (a_ref[...] + b_ref[...].astype(F32)).astype(BF16)

    src = pl.BlockSpec((pl.Squeezed(), tr, c_), lambda s, i, oth: (oth[s], i, 0))
    return pl.pallas_call(
        body, name=name, out_shape=SDS((k - 1, r, c_), BF16),
        grid_spec=pltpu.PrefetchScalarGridSpec(
            num_scalar_prefetch=1, grid=(k - 1, r // tr), in_specs=[src, src],
            out_specs=pl.BlockSpec((pl.Squeezed(), tr, c_), lambda s, i, oth: (s, i, 0))),
        compiler_params=_params(("arbitrary", "arbitrary")),
    )(others, own, got)


def reduce_adam(own, got_sibling, got_chips, w, m, v, prev, layer, tr, name, shifted=False):
    nl, r, cols = w.shape
    c_ = own.shape[-1]
    n_scratch = 1 if shifted else 0
    chip = jnp.reshape(_my_chip(), (1,)).astype(jnp.int32)

    def body(chip_ref, own_ref, sib_ref, c0_ref, c1_ref, c2_ref, w_ref, m_ref, v_ref, *rest):
        g_ref, d_ref, nm_ref, nv_ref = rest[len(rest) - n_scratch - 4:len(rest) - n_scratch]
        g = (own_ref[...] + sib_ref[...].astype(F32) + c0_ref[...].astype(F32) + c1_ref[...].astype(F32)
             + c2_ref[...].astype(F32))
        if shifted:
            rest[-1][...] = pltpu.roll(g, c_ - _my_lane_offset(cols), 1)
            g = rest[-1][:, :cols]
        delta, nm, nv = _adamw(w_ref[...], g, m_ref[...], v_ref[...])
        g_ref[...] = g
        d_ref[...] = delta
        nm_ref[...] = nm
        nv_ref[...] = nv

    mine = pl.BlockSpec((pl.Squeezed(), tr, c_), lambda i, ch: (ch[0], i, 0))
    lay = pl.BlockSpec((pl.Squeezed(), tr, cols), lambda i, ch: (layer, i, 0))
    chips = [pl.BlockSpec((pl.Squeezed(), tr, c_), lambda i, ch, s=s: (s, i, 0)) for s in range(3)]
    in_specs = [mine, mine] + chips + [lay, lay, lay]
    args = [chip, own, got_sibling, got_chips, got_chips, got_chips, w, m, v]
    aliases = {}
    if prev is not None:
        in_specs += [pl.BlockSpec(memory_space=pl.ANY)] * 4
        aliases = {len(args) + k: k for k in range(4)}
        args += list(prev)
    return pl.pallas_call(
        body, name=name, out_shape=[SDS((nl, r, cols), F32)] * 4, input_output_aliases=aliases,
        grid_spec=pltpu.PrefetchScalarGridSpec(
            num_scalar_prefetch=1, grid=(r // tr,), in_specs=in_specs, out_specs=[lay] * 4,
            scratch_shapes=[pltpu.VMEM((tr, c_), F32)] * n_scratch),
        compiler_params=_params(("arbitrary",)),
    )(*args)


def sum_devices(packs):
    n, r, c_ = packs.shape

    def body(p_ref, o_ref):
        acc = p_ref[0]
        for k in range(1, n):
            acc = acc + p_ref[k]
        o_ref[...] = acc

    return pl.pallas_call(body, name="sum_devices", out_shape=SDS((r, c_), F32), compiler_params=_params())(packs)


def adam_small(w, g, m, v):
    def body(w_ref, g_ref, m_ref, v_ref, d_ref, nm_ref, nv_ref):
        delta, nm, nv = _adamw(w_ref[...], g_ref[...], m_ref[...], v_ref[...])
        d_ref[...] = delta
        nm_ref[...] = nm
        nv_ref[...] = nv

    return pl.pallas_call(body, name="adam_small", out_shape=[SDS(w.shape, F32)] * 3, compiler_params=_params())(w, g, m, v)


SMALL = ("pre_norm_w", "pool_scale", "conv_b", "dt_bias", "a_log", "d_skip", "_pad", "ssd_norm_w", "post_norm_w", "conv_w")


def _pack(parts):
    flat = jnp.concatenate([parts[k] for k in SMALL], axis=1).reshape(-1, LANES)
    return jnp.pad(flat, ((0, (-flat.shape[0]) % 8), (0, 0)))


def _unpack(pack, sizes, nl):
    total = sum(sizes[k] for k in SMALL)
    flat = pack[: nl * total // LANES].reshape(nl, total)
    out, o = {}, 0
    for k in SMALL:
        out[k] = flat[:, o:o + sizes[k]]
        o += sizes[k]
    return out


def kernel(x, pre_norm_w, w_in, pool_mix_w, pool_scale, conv_w, conv_b, dt_bias, a_log, d_skip, ssd_norm_w, w_out, post_norm_w, loss_target, m_pre_norm_w, m_w_in, m_pool_mix_w, m_pool_scale, m_conv_w, m_conv_b, m_dt_bias, m_a_log, m_d_skip, m_ssd_norm_w, m_w_out, m_post_norm_w, v_pre_norm_w, v_w_in, v_pool_mix_w, v_pool_scale, v_conv_w, v_conv_b, v_dt_bias, v_a_log, v_d_skip, v_ssd_norm_w, v_w_out, v_post_norm_w):
    cx, cy, cc = lax.axis_index("x"), lax.axis_index("y"), lax.axis_index("c")
    me = 4 * cx + 2 * cy + cc
    mychip = 2 * cx + cy
    nl, d, cols = w_in.shape
    t = x.shape[1]
    n_heads = a_log.shape[1]
    sw = n_heads * SSD_HEAD_DIM
    pw = pool_scale.shape[1]
    cd = conv_b.shape[1]
    ng, gsh, gw = pool_mix_w.shape[1:]
    e_main = N_DEV * cols - n_heads
    assert x.shape[0] == 1 and pw == sw and cd == sw + 2 * SSD_GROUPS * SSD_STATE and e_main == 2 * pw + sw + cd
    assert 2 * pw + sw == CONV_BLOCK * cd and n_heads <= LANES and t % SSD_CHUNK == 0 and gsh * N_DEV == gw
    tm, _ = _tiles(t)

    shards = {"in": [shift_cast(w_in, tm)], "out": [w_out.astype(BF16)], "small": [pool_mix_w.astype(BF16), conv_w]}
    pad_h = ((0, 0), (0, LANES - n_heads))

    def params_a(l, g_in):
        w_main, w_dt = assemble_w_in(g_in, cols, n_heads, tm)
        return dict(pre_w=pre_norm_w[l:l + 1], w_main=w_main, w_dt=w_dt, pscale=pool_scale[l:l + 1], conv_b=conv_b[l:l + 1],
                    dt_bias=jnp.pad(dt_bias[l:l + 1], pad_h), a_log=jnp.pad(a_log[l:l + 1], pad_h),
                    d_full=jnp.repeat(d_skip[l:l + 1], SSD_HEAD_DIM, axis=1), norm_w=ssd_norm_w[l:l + 1],
                    post_w=post_norm_w[l:l + 1])

    finish = {"small": lambda got: dict(mixw=got[0].transpose(1, 0, 2, 3).reshape(ng, gw, gw),
                                        conv_w=got[1].transpose(1, 0, 2).reshape(CONV_WIDTH, cd)),
              "out": lambda got: dict(w_out=got[0].reshape(N_DEV * w_out.shape[1], d))}

    xs = x[0]
    saved, params = [], []
    p = params_a(0, all_gather_hbm([shards["in"][0][:1]], "gather_w_in")[0][0])
    normed = None
    for l in range(nl):
        (xs, normed), s, gathered, p = layer_fwd(xs, p, l, shards, finish, nxt=l + 1 < nl, first=l == 0, normed=normed,
                                                 next_pre_w=pre_norm_w[l + 1:l + 2] if l + 1 < nl else None)
        saved.append(s)
        params.append(p)
        if l + 1 < nl:
            p = dict(params_a(l + 1, gathered[0]), **finish["small"](gathered[1:]))
    loss_part, g = loss_grad(xs, loss_target[0], tm)
    loss = lax.psum(loss_part[0, 0], ("x", "y", "c"))

    big = {"w_in": (w_in, m_w_in, v_w_in), "w_out": (w_out, m_w_out, v_w_out),
           "pool_mix_w": tuple(a.reshape(nl, ng * gsh, gw) for a in (pool_mix_w, m_pool_mix_w, v_pool_mix_w))}
    names = list(big)
    big_out = {k: None for k in big}
    small_g = [None] * nl

    def apply(layer, own, got_sib, got_chips):
        for k, o, gs_, gc in zip(names, own, got_sib, got_chips):
            wk, mk, vk = big[k]
            big_out[k] = reduce_adam(o, gs_, gc, wk, mk, vk, big_out[k], layer, min(256, wk.shape[1]), "reduce_adam_" + k,
                                     shifted=(k == "w_in"))

    def split_in(dw_main, dw_dt):
        own, send = grad_blocks(dw_main, dw_dt, cols, min(128, d))
        return [own], [send]

    def split_rest(dw_out, d_mixw):
        halves = [lambda ci: lax.dynamic_index_in_dim(dw_out.reshape(4, 2, -1, d), ci, 1, keepdims=False),
                  lambda ci: lax.dynamic_index_in_dim(
                      d_mixw.reshape(ng, 4, 2, gsh, gw), ci, 2, keepdims=False).transpose(1, 0, 2, 3).reshape(4, ng * gsh, gw)]
        return [h(cc) for h in halves], [h(1 - cc).astype(BF16) for h in halves]

    pending = post = None
    for l in reversed(range(nl)):
        below = (saved[l - 1]["out"], saved[l - 1]["r_post"], params[l - 1]["post_w"]) if l > 0 else None
        (g, post), gr, mine, done = layer_bwd(g, saved[l], params[l], split_in, split_rest, pending, last=(l == 0),
                                              post=post, below=below)
        if pending is not None:
            apply(l + 1, pending[0], *done[0])
        if l == 0:
            apply(0, mine[0], *done[1])
        pending = mine
        small_g[l] = dict(pre_norm_w=gr["pre_w"], pool_scale=gr["pscale"], conv_b=gr["conv_b"], dt_bias=gr["dt_bias"][:, :n_heads],
                          a_log=gr["a_log"][:, :n_heads], d_skip=gr["d_skip"][:, :n_heads], _pad=jnp.zeros((1, LANES - 3 * n_heads), F32),
                          ssd_norm_w=gr["norm_w"], post_norm_w=gr["post_w"], conv_w=gr["conv_w"].reshape(1, CONV_WIDTH * cd))

    sizes = {k: small_g[0][k].shape[1] for k in SMALL}
    gsum = sum_devices(all_gather_vmem(_pack({k: jnp.concatenate([sg[k] for sg in small_g], axis=0) for k in SMALL}),
                                       "gather_small_grads"))
    gs = _unpack(gsum, sizes, nl)
    csh = conv_w.shape[2]
    gs["conv_w"] = lax.dynamic_slice_in_dim(gs["conv_w"].reshape(nl, CONV_WIDTH, cd), me * csh, csh, axis=2).reshape(nl, -1)
    lsizes = dict(sizes, conv_w=CONV_WIDTH * csh)
    zpad = jnp.zeros((nl, sizes["_pad"]), F32)

    def local(pre, scale, cb, dtb, al, dsk, nw, post, cw):
        return _pack(dict(pre_norm_w=pre, pool_scale=scale, conv_b=cb, dt_bias=dtb, a_log=al, d_skip=dsk, _pad=zpad,
                          ssd_norm_w=nw, post_norm_w=post, conv_w=cw.reshape(nl, -1)))

    wp = local(pre_norm_w, pool_scale, conv_b, dt_bias, a_log, d_skip, ssd_norm_w, post_norm_w, conv_w)
    mp = local(m_pre_norm_w, m_pool_scale, m_conv_b, m_dt_bias, m_a_log, m_d_skip, m_ssd_norm_w, m_post_norm_w, m_conv_w)
    vp = local(v_pre_norm_w, v_pool_scale, v_conv_b, v_dt_bias, v_a_log, v_d_skip, v_ssd_norm_w, v_post_norm_w, v_conv_w)
    small_out = [gs] + [_unpack(o, lsizes, nl) for o in adam_small(wp, _pack(gs), mp, vp)]

    def leaf(kind, name):
        if name in big:
            return big_out[name][kind].reshape(big[name][0].shape if name != "pool_mix_w" else pool_mix_w.shape)
        val = small_out[kind][name]
        return val.reshape(conv_w.shape) if name == "conv_w" else val

    order = ("pre_norm_w", "w_in", "pool_mix_w", "pool_scale", "conv_w", "conv_b", "dt_bias", "a_log", "d_skip",
             "ssd_norm_w", "w_out", "post_norm_w")
    return (loss, g[None]) + tuple(leaf(kind, name) for kind in range(4) for name in order)
```

```python
import jax
import jax.numpy as jnp
from jax import lax
from jax.experimental import pallas as pl
from jax.experimental.pallas import tpu as pltpu

F32 = jnp.float32
BF16 = jnp.bfloat16
SDS = jax.ShapeDtypeStruct
MESH = pl.DeviceIdType.MESH
HIGHEST = lax.Precision.HIGHEST

NORM_EPS = 1e-6
POOL_WINDOWS = (2, 4, 8, 16)
POOL_HALO = 16
CONV_WIDTH = 4
CONV_HALO = 8
SSD_CHUNK = 128
SSD_HEAD_DIM = 64
SSD_STATE = 128
SSD_GROUPS = 4
LANES = 128
N_DEV = 8

ADAM_LR = 0.001
ADAM_B1 = 0.9
ADAM_B2 = 0.999
ADAM_EPS = 1e-08
ADAM_WD = 0.01
ADAM_STEP = 10

VMEM_LIMIT = 56 * 1024 * 1024

NT = (((1,), (1,)), ((), ()))
TN = (((0,), (0,)), ((), ()))


def _params(sem=None):
    kw = dict(vmem_limit_bytes=VMEM_LIMIT)
    if sem is not None:
        kw["dimension_semantics"] = sem
    return pltpu.CompilerParams(**kw)


def _silu(v):
    return v * jax.nn.sigmoid(v)


def _dsilu(v):
    s = jax.nn.sigmoid(v)
    return s * (1.0 + v * (1.0 - s))


def _split_dot(v, sel):
    hi = v.astype(BF16)
    lo = (v - hi.astype(F32)).astype(BF16)
    return (jnp.dot(hi, sel, preferred_element_type=F32) + jnp.dot(lo, sel, preferred_element_type=F32))


def _head_selector(width, per):
    ch = lax.broadcasted_iota(jnp.int32, (width, LANES), 0)
    hd = lax.broadcasted_iota(jnp.int32, (width, LANES), 1)
    return jnp.where((ch >= hd * per) & (ch < (hd + 1) * per), 1.0, 0.0).astype(BF16)


class Comm:
    def __init__(self, inputs, out_shapes, aliases, n_sems, make):
        self.inputs, self.out_shapes, self.aliases, self.n_sems, self.make = list(inputs), list(out_shapes), dict(aliases), n_sems, make


def _remote(src, dst, send_sems, recv_sems, k, peer):
    return pltpu.make_async_remote_copy(src_ref=src, dst_ref=dst, send_sem=send_sems.at[k], recv_sem=recv_sems.at[k],
                                        device_id=peer, device_id_type=MESH)


class _SemRange:
    def __init__(self, sems, start):
        self.sems, self.start = sems, start

    @property
    def at(self):
        return self

    def __getitem__(self, k):
        return self.sems.at[self.start + k]


def merge_comms(comms):
    comms = [c for c in comms if c is not None]
    if len(comms) <= 1:
        return comms[0] if comms else None
    aliases, i_off, o_off = {}, 0, 0
    for c in comms:
        aliases.update({i_off + k: o_off + v for k, v in c.aliases.items()})
        i_off, o_off = i_off + len(c.inputs), o_off + len(c.out_shapes)

    def make(ins, outs, ss, rs):
        sends, locals_, arrivals, i0, o0, s0 = [], [], [], 0, 0, 0
        for c in comms:
            s, l, a = c.make(ins[i0:i0 + len(c.inputs)], outs[o0:o0 + len(c.out_shapes)], _SemRange(ss, s0), _SemRange(rs, s0))
            sends, locals_, arrivals = sends + s, locals_ + l, arrivals + a
            i0, o0, s0 = i0 + len(c.inputs), o0 + len(c.out_shapes), s0 + c.n_sems
        return sends, locals_, arrivals

    return Comm(sum((c.inputs for c in comms), []), sum((c.out_shapes for c in comms), []), aliases,
                sum(c.n_sems for c in comms), make)


def _call(body, args, *, name, grid, in_specs, out_specs, out_shape, scratch_shapes=(), sem=None, comm=None, aliases=None):
    in_specs, out_specs, out_shape = list(in_specs), list(out_specs), list(out_shape)
    aliases = dict(aliases or {})
    if comm is None:
        outs = pl.pallas_call(body, name=name, grid=grid, in_specs=in_specs, out_specs=out_specs, out_shape=out_shape,
                              scratch_shapes=list(scratch_shapes), input_output_aliases=aliases,
                              compiler_params=_params(sem))(*args)
        return list(outs), []
    ni, no, nci, nco, ns = len(in_specs), len(out_specs), len(comm.inputs), len(comm.out_shapes), len(scratch_shapes)
    hbm = pl.BlockSpec(memory_space=pl.ANY)

    def hosted(*refs):
        ins, cins = refs[:ni], refs[ni:ni + nci]
        outs, couts = refs[ni + nci:ni + nci + no], refs[ni + nci + no:ni + nci + no + nco]
        scratch = refs[ni + nci + no + nco:]
        sends, locals_, arrivals = comm.make(cins, couts, scratch[ns], scratch[ns + 1])
        first = last = None if grid else True
        for axis, extent in enumerate(grid):
            pid = pl.program_id(axis)
            first = (pid == 0) if first is None else first & (pid == 0)
            last = (pid == extent - 1) if last is None else last & (pid == extent - 1)

        @pl.when(first)
        def _():
            for cp in locals_ + sends:
                cp.start()

        body(*ins, *outs, *scratch[:ns])

        @pl.when(last)
        def _():
            for cp in arrivals:
                cp.wait_recv()
            for cp in sends:
                cp.wait_send()
            for cp in locals_:
                cp.wait()

    outs = pl.pallas_call(
        hosted, name=name, grid=grid, in_specs=in_specs + [hbm] * nci, out_specs=out_specs + [hbm] * nco,
        out_shape=out_shape + comm.out_shapes,
        scratch_shapes=list(scratch_shapes) + [pltpu.SemaphoreType.DMA((comm.n_sems,)), pltpu.SemaphoreType.DMA((comm.n_sems,))],
        input_output_aliases={**aliases, **{ni + k: no + v for k, v in comm.aliases.items()}},
        compiler_params=_params(sem),
    )(*args, *comm.inputs)
    return list(outs[:no]), list(outs[no:])


def rms_fwd(x, w, tm):
    t, d = x.shape

    def body(x_ref, w_ref, h_ref, r_ref):
        xv = x_ref[...]
        r = lax.rsqrt(jnp.mean(xv * xv, axis=-1, keepdims=True) + NORM_EPS)
        h_ref[...] = (xv * r * w_ref[...]).astype(BF16)
        r_ref[...] = r

    return pl.pallas_call(
        body, name="rms_fwd", grid=(t // tm,),
        in_specs=[pl.BlockSpec((tm, d), lambda i: (i, 0)), pl.BlockSpec((1, d), lambda i: (0, 0))],
        out_specs=[pl.BlockSpec((tm, d), lambda i: (i, 0)), pl.BlockSpec((tm, 1), lambda i: (i, 0))],
        out_shape=[SDS((t, d), BF16), SDS((t, 1), F32)],
        compiler_params=_params(("arbitrary",)),
    )(x, w)


def post_fwd(out, x, w, tm, comm=None, next_w=None):
    t, d = x.shape
    fused = next_w is not None

    def body(o_ref, x_ref, w_ref, *rest):
        y_ref, r_ref = rest[fused:fused + 2]
        ov = o_ref[...]
        r = lax.rsqrt(jnp.mean(ov * ov, axis=-1, keepdims=True) + NORM_EPS)
        y = x_ref[...] + ov * r * w_ref[...]
        y_ref[...] = y
        r_ref[...] = r
        if fused:
            rn = lax.rsqrt(jnp.mean(y * y, axis=-1, keepdims=True) + NORM_EPS)
            rest[3][...] = (y * rn * rest[0][...]).astype(BF16)
            rest[4][...] = rn

    row = pl.BlockSpec((tm, d), lambda i: (i, 0))
    vec = pl.BlockSpec((1, d), lambda i: (0, 0))
    col = pl.BlockSpec((tm, 1), lambda i: (i, 0))
    return _call(
        body, (out, x, w) + ((next_w,) if fused else ()), name="post_fwd", grid=(t // tm,),
        in_specs=[row, row, vec] + [vec] * fused, out_specs=[row, col] + [row, col] * fused,
        out_shape=[SDS((t, d), F32), SDS((t, 1), F32)] + [SDS((t, d), BF16), SDS((t, 1), F32)] * fused,
        sem=("arbitrary",), comm=comm)


def _norm_bwd(g_n, n, r):
    return r * (g_n - n * jnp.mean(g_n * n, axis=-1, keepdims=True))


def post_bwd(g, out, r, w, tm):
    t, d = g.shape

    def body(g_ref, o_ref, r_ref, w_ref, do_ref, dw_ref):
        i = pl.program_id(0)
        gv = g_ref[...]
        rv = r_ref[...]
        n = o_ref[...] * rv
        part = jnp.sum(gv * n, axis=0, keepdims=True)

        @pl.when(i == 0)
        def _():
            dw_ref[...] = part

        @pl.when(i > 0)
        def _():
            dw_ref[...] += part

        do_ref[...] = _norm_bwd(gv * w_ref[...], n, rv).astype(BF16)

    return pl.pallas_call(
        body, name="post_bwd", grid=(t // tm,),
        in_specs=[pl.BlockSpec((tm, d), lambda i: (i, 0)), pl.BlockSpec((tm, d), lambda i: (i, 0)),
                  pl.BlockSpec((tm, 1), lambda i: (i, 0)), pl.BlockSpec((1, d), lambda i: (0, 0))],
        out_specs=[pl.BlockSpec((tm, d), lambda i: (i, 0)), pl.BlockSpec((1, d), lambda i: (0, 0))],
        out_shape=[SDS((t, d), BF16), SDS((1, d), F32)],
        compiler_params=_params(("arbitrary",)),
    )(g, out, r, w)


def rms_bwd(dh, x, r, w, g, tm, comm=None, below=None):
    t, d = x.shape
    fused = below is not None

    def body(a_ref, x_ref, r_ref, w_ref, g_ref, *rest):
        gx_ref, dw_ref = rest[3 * fused:3 * fused + 2]
        i = pl.program_id(0)
        dh = a_ref[...]
        rv = r_ref[...]
        n = x_ref[...] * rv
        gx = g_ref[...] + _norm_bwd(dh * w_ref[...], n, rv)
        gx_ref[...] = gx
        parts = [(dw_ref, jnp.sum(dh * n, axis=0, keepdims=True))]
        if fused:
            o_ref, rp_ref, pw_ref = rest[:3]
            do_ref, dpw_ref = rest[5:7]
            rp = rp_ref[...]
            nb = o_ref[...] * rp
            do_ref[...] = _norm_bwd(gx * pw_ref[...], nb, rp).astype(BF16)
            parts.append((dpw_ref, jnp.sum(gx * nb, axis=0, keepdims=True)))

        @pl.when(i == 0)
        def _():
            for ref, part in parts:
                ref[...] = part

        @pl.when(i > 0)
        def _():
            for ref, part in parts:
                ref[...] += part

    row = pl.BlockSpec((tm, d), lambda i: (i, 0))
    vec = pl.BlockSpec((1, d), lambda i: (0, 0))
    col = pl.BlockSpec((tm, 1), lambda i: (i, 0))
    return _call(
        body, (dh, x, r, w, g) + tuple(below or ()), name="rms_bwd", grid=(t // tm,),
        in_specs=[row, row, col, vec, row] + [row, col, vec] * fused, out_specs=[row, vec] + [row, vec] * fused,
        out_shape=[SDS((t, d), F32), SDS((1, d), F32)] + [SDS((t, d), BF16), SDS((1, d), F32)] * fused,
        sem=("arbitrary",), comm=comm)


def loss_grad(y, target, tm):
    t, d = y.shape

    def body(y_ref, t_ref, l_ref, g_ref):
        i = pl.program_id(0)
        err = y_ref[...] - t_ref[...]
        g_ref[...] = err / d
        part = 0.5 * jnp.sum(jnp.mean(err * err, axis=-1, keepdims=True), axis=0, keepdims=True)

        @pl.when(i == 0)
        def _():
            l_ref[...] = part

        @pl.when(i > 0)
        def _():
            l_ref[...] += part

    row = pl.BlockSpec((tm, d), lambda i: (i, 0))
    return pl.pallas_call(
        body, name="loss_grad", grid=(t // tm,), in_specs=[row, row],
        out_specs=[pl.BlockSpec((1, 1), lambda i: (0, 0)), row],
        out_shape=[SDS((1, 1), F32), SDS((t, d), F32)],
        compiler_params=_params(("arbitrary",)),
    )(y, target)


def mm_nn(a, b, out_dtype, tm, tn, name, comm=None):
    m, k = a.shape
    n = b.shape[1]

    def body(a_ref, b_ref, o_ref):
        o_ref[...] = jnp.dot(a_ref[...], b_ref[...], preferred_element_type=F32).astype(out_dtype)

    outs, couts = _call(
        body, (a, b), name=name, grid=(n // tn, m // tm),
        in_specs=[pl.BlockSpec((tm, k), lambda j, i: (i, 0)), pl.BlockSpec((k, tn), lambda j, i: (0, j))],
        out_specs=[pl.BlockSpec((tm, tn), lambda j, i: (i, j))],
        out_shape=[SDS((m, n), out_dtype)], sem=("arbitrary", "arbitrary"), comm=comm)
    return outs[0], couts


def mm_nt(a, b, out_dtype, tm, tn, tk, name, comm=None, extra=None):
    m, k = a.shape
    n = b.shape[0]
    nk = k // tk

    def body(a_ref, b_ref, *rest):
        o_ref, acc_ref = rest[-2:]
        kk = pl.program_id(2)
        part = lax.dot_general(a_ref[...], b_ref[...], NT, preferred_element_type=F32)
        if nk == 1:
            if extra is not None:
                part = part + lax.dot_general(rest[0][...], rest[1][...], NT, preferred_element_type=F32)
            o_ref[...] = part.astype(out_dtype)
        else:
            @pl.when(kk == 0)
            def _():
                if extra is None:
                    acc_ref[...] = part
                else:
                    acc_ref[...] = part + lax.dot_general(rest[0][...], rest[1][...], NT, preferred_element_type=F32)

            @pl.when(kk > 0)
            def _():
                acc_ref[...] += part

            @pl.when(kk == nk - 1)
            def _():
                o_ref[...] = acc_ref[...].astype(out_dtype)

    more_specs = [] if extra is None else [pl.BlockSpec((tm, extra[0].shape[1]), lambda i, j, kk: (i, 0)),
                                           pl.BlockSpec((tn, extra[1].shape[1]), lambda i, j, kk: (j, 0))]
    outs, couts = _call(
        body, (a, b) + tuple(extra or ()), name=name, grid=(m // tm, n // tn, nk),
        in_specs=[pl.BlockSpec((tm, tk), lambda i, j, kk: (i, kk)), pl.BlockSpec((tn, tk), lambda i, j, kk: (j, kk))] + more_specs,
        out_specs=[pl.BlockSpec((tm, tn), lambda i, j, kk: (i, j))],
        out_shape=[SDS((m, n), out_dtype)],
        scratch_shapes=[pltpu.VMEM((tm, tn) if nk > 1 else (8, LANES), F32)],
        sem=("arbitrary", "arbitrary", "arbitrary"), comm=comm)
    return outs[0], couts


def mm_tn(a, b, tm, tn, name, comm=None):
    t, m = a.shape
    n = b.shape[1]

    def body(a_ref, b_ref, o_ref):
        o_ref[...] = lax.dot_general(a_ref[...], b_ref[...], TN, preferred_element_type=F32)

    outs, couts = _call(
        body, (a, b), name=name, grid=(m // tm, n // tn),
        in_specs=[pl.BlockSpec((t, tm), lambda i, j: (0, i)), pl.BlockSpec((t, tn), lambda i, j: (0, j))],
        out_specs=[pl.BlockSpec((tm, tn), lambda i, j: (i, j))],
        out_shape=[SDS((m, n), F32)], sem=("arbitrary", "arbitrary"), comm=comm)
    return outs[0], couts


def _window_sums(ext, n_rows, lookahead):
    def sh(v, k):
        return pltpu.roll(v, (n_rows - k) if lookahead else k, 0)
    s2 = ext + sh(ext, 1)
    s4 = s2 + sh(s2, 2)
    s8 = s4 + sh(s4, 4)
    s16 = s8 + sh(s8, 8)
    return (s2, s4, s8, s16)


def _pool_counts(i, tm, w):
    tpos = i * tm + lax.broadcasted_iota(jnp.int32, (tm, 1), 0)
    return jnp.minimum(tpos + 1, w).astype(F32)


def _pooled(uc_ref, up_ref, i, tm):
    cur = uc_ref[...]
    prev = jnp.where(i > 0, up_ref[...], 0.0)
    ext = jnp.concatenate([prev, cur], axis=0)
    return cur, _window_sums(ext, tm + POOL_HALO, False)


def pool_fwd(proj, mixw, scale, tm):
    t = proj.shape[0]
    pw = scale.shape[1]
    gw = pw // len(POOL_WINDOWS)
    nh = tm // POOL_HALO

    def body(uc_ref, up_ref, g_ref, w_ref, s_ref, o_ref):
        i = pl.program_id(0)
        cur, sums = _pooled(uc_ref, up_ref, i, tm)
        for g, w in enumerate(POOL_WINDOWS):
            cols = slice(g * gw, (g + 1) * gw)
            pooled = sums[g][POOL_HALO:, cols] / _pool_counts(i, tm, w) - cur[:, cols]
            mixed = jnp.dot(pooled.astype(BF16), w_ref[g], preferred_element_type=F32)
            o_ref[:, cols] = (mixed * s_ref[:, cols] * _silu(g_ref[:, cols])).astype(BF16)

    return pl.pallas_call(
        body, name="pool_fwd", grid=(t // tm,),
        in_specs=[pl.BlockSpec((tm, pw), lambda i: (i, 0)),
                  pl.BlockSpec((POOL_HALO, pw), lambda i: (jnp.maximum(i * nh - 1, 0), 0)),
                  pl.BlockSpec((tm, pw), lambda i: (i, 1)),
                  pl.BlockSpec(mixw.shape, lambda i: (0, 0, 0)),
                  pl.BlockSpec((1, pw), lambda i: (0, 0))],
        out_specs=pl.BlockSpec((tm, pw), lambda i: (i, 0)),
        out_shape=SDS((t, 2 * pw), BF16),
        compiler_params=_params(("arbitrary",)),
    )(proj, proj, proj, mixw, scale)


def pool_bwd_a(dmixed, proj, mixw, scale, tm):
    t, e = proj.shape
    pw = scale.shape[1]
    ng = len(POOL_WINDOWS)
    gw = pw // ng
    nh = tm // POOL_HALO

    def body(dy_ref, uc_ref, up_ref, g_ref, w_ref, s_ref, dg_ref, dq_ref, ds_ref, dw_ref):
        i = pl.program_id(0)

        @pl.when(i == 0)
        def _():
            ds_ref[...] = jnp.zeros_like(ds_ref)
            dw_ref[...] = jnp.zeros_like(dw_ref)

        cur, sums = _pooled(uc_ref, up_ref, i, tm)
        for g, w in enumerate(POOL_WINDOWS):
            cols = slice(g * gw, (g + 1) * gw)
            cnt = _pool_counts(i, tm, w)
            pooled = (sums[g][POOL_HALO:, cols] / cnt - cur[:, cols]).astype(BF16)
            mixed = jnp.dot(pooled, w_ref[g], preferred_element_type=F32)
            gate = g_ref[:, cols]
            dy = dy_ref[:, cols]
            sc = s_ref[:, cols]
            dg_ref[:, cols] = (dy * mixed * sc * _dsilu(gate)).astype(BF16)
            ds = dy * _silu(gate)
            ds_ref[:, cols] += jnp.sum(ds * mixed, axis=0, keepdims=True)
            dmix = (ds * sc).astype(BF16)
            dw_ref[g] += lax.dot_general(pooled, dmix, TN, preferred_element_type=F32)
            dq_ref[:, cols] = lax.dot_general(dmix, w_ref[g], NT, preferred_element_type=F32) / cnt

    return pl.pallas_call(
        body, name="pool_bwd_a", grid=(t // tm,),
        in_specs=[pl.BlockSpec((tm, pw), lambda i: (i, 0)),
                  pl.BlockSpec((tm, pw), lambda i: (i, 0)),
                  pl.BlockSpec((POOL_HALO, pw), lambda i: (jnp.maximum(i * nh - 1, 0), 0)),
                  pl.BlockSpec((tm, pw), lambda i: (i, 1)),
                  pl.BlockSpec(mixw.shape, lambda i: (0, 0, 0)),
                  pl.BlockSpec((1, pw), lambda i: (0, 0))],
        out_specs=[pl.BlockSpec((tm, pw), lambda i: (i, 1)),
                   pl.BlockSpec((tm, pw), lambda i: (i, 0)),
                   pl.BlockSpec((1, pw), lambda i: (0, 0)),
                   pl.BlockSpec((ng, gw, gw), lambda i: (0, 0, 0))],
        out_shape=[SDS((t, e), BF16), SDS((t, pw), F32), SDS((1, pw), F32), SDS((ng, gw, gw), F32)],
        compiler_params=_params(("arbitrary",)),
    )(dmixed, proj, proj, proj, mixw, scale)


def pool_bwd_b(dq, dproj, tm):
    t, pw = dq.shape
    gw = pw // len(POOL_WINDOWS)
    nh = tm // POOL_HALO
    nt = t // tm

    def body(c_ref, n_ref, alias_ref, o_ref):
        i = pl.program_id(0)
        cur = c_ref[...]
        nxt = jnp.where(i < nt - 1, n_ref[...], 0.0)
        sums = _window_sums(jnp.concatenate([cur, nxt], axis=0), tm + POOL_HALO, True)
        for g, w in enumerate(POOL_WINDOWS):
            cols = slice(g * gw, (g + 1) * gw)
            o_ref[:, cols] = (sums[g][:tm, cols] - cur[:, cols] * _pool_counts(i, tm, w)).astype(BF16)

    return pl.pallas_call(
        body, name="pool_bwd_b", grid=(nt,),
        in_specs=[pl.BlockSpec((tm, pw), lambda i: (i, 0)),
                  pl.BlockSpec((POOL_HALO, pw), lambda i: (jnp.minimum((i + 1) * nh, t // POOL_HALO - 1), 0)),
                  pl.BlockSpec(memory_space=pl.ANY)],
        out_specs=pl.BlockSpec((tm, pw), lambda i: (i, 0)),
        out_shape=SDS(dproj.shape, dproj.dtype),
        input_output_aliases={2: 0},
        compiler_params=_params(("arbitrary",)),
    )(dq, dq, dproj)


ELEMENTWISE_LANE_CHUNK = 256
HEAVY_ROW_TILE = 256


def _lane_chunks(width):
    return [slice(c, c + ELEMENTWISE_LANE_CHUNK) for c in range(0, width, ELEMENTWISE_LANE_CHUNK)]


def _conv_pre(xc_ref, xp_ref, w_ref, b_ref, i, cols):
    cur = xc_ref[:, cols]
    prev = jnp.where(i > 0, xp_ref[:, cols], 0.0)
    ext = jnp.concatenate([prev, cur], axis=0)
    taps = [pltpu.roll(ext, CONV_WIDTH - 1 - k, 0)[CONV_HALO:] for k in range(CONV_WIDTH - 1)] + [cur]
    pre = b_ref[:, cols]
    for k in range(CONV_WIDTH):
        pre = pre + w_ref[k:k + 1, cols] * taps[k]
    return pre, taps


def conv_fwd(proj, conv_w, conv_b, col_block, tm, comm=None):
    t = proj.shape[0]
    cd = conv_b.shape[1]
    nh = tm // CONV_HALO

    def body(xc_ref, xp_ref, w_ref, b_ref, o_ref):
        i = pl.program_id(0)
        for cols in _lane_chunks(cd):
            pre, _ = _conv_pre(xc_ref, xp_ref, w_ref, b_ref, i, cols)
            o_ref[:, cols] = _silu(pre)

    outs, couts = _call(
        body, (proj, proj, conv_w, conv_b), name="conv_fwd", grid=(t // tm,),
        in_specs=[pl.BlockSpec((tm, cd), lambda i: (i, col_block)),
                  pl.BlockSpec((CONV_HALO, cd), lambda i: (jnp.maximum(i * nh - 1, 0), col_block)),
                  pl.BlockSpec((CONV_WIDTH, cd), lambda i: (0, 0)),
                  pl.BlockSpec((1, cd), lambda i: (0, 0))],
        out_specs=[pl.BlockSpec((tm, cd), lambda i: (i, 0))],
        out_shape=[SDS((t, cd), F32)], sem=("arbitrary",), comm=comm)
    return outs[0], couts


def conv_bwd_a(dxs, db, dc, proj, conv_w, conv_b, col_block, tm, comm=None):
    t = proj.shape[0]
    cd = conv_b.shape[1]
    sw = dxs.shape[1]
    gn = db.shape[1]
    nh = tm // CONV_HALO

    def body(dx_ref, db_ref, dc_ref, xc_ref, xp_ref, w_ref, b_ref, dp_ref, dw_ref, dbias_ref):
        i = pl.program_id(0)

        @pl.when(i == 0)
        def _():
            dw_ref[...] = jnp.zeros_like(dw_ref)
            dbias_ref[...] = jnp.zeros_like(dbias_ref)

        for cols in _lane_chunks(cd):
            pre, taps = _conv_pre(xc_ref, xp_ref, w_ref, b_ref, i, cols)
            if cols.start < sw:
                dact = dx_ref[:, cols]
            elif cols.start < sw + gn:
                dact = db_ref[:, cols.start - sw:cols.stop - sw]
            else:
                dact = dc_ref[:, cols.start - sw - gn:cols.stop - sw - gn]
            dpre = dact * _dsilu(pre)
            dp_ref[:, cols] = dpre
            dbias_ref[:, cols] += jnp.sum(dpre, axis=0, keepdims=True)
            for k in range(CONV_WIDTH):
                dw_ref[k:k + 1, cols] += jnp.sum(dpre * taps[k], axis=0, keepdims=True)

    return _call(
        body, (dxs, db, dc, proj, proj, conv_w, conv_b), name="conv_bwd_a", grid=(t // tm,),
        in_specs=[pl.BlockSpec((tm, sw), lambda i: (i, 0)), pl.BlockSpec((tm, gn), lambda i: (i, 0)),
                  pl.BlockSpec((tm, gn), lambda i: (i, 0)),
                  pl.BlockSpec((tm, cd), lambda i: (i, col_block)),
                  pl.BlockSpec((CONV_HALO, cd), lambda i: (jnp.maximum(i * nh - 1, 0), col_block)),
                  pl.BlockSpec((CONV_WIDTH, cd), lambda i: (0, 0)),
                  pl.BlockSpec((1, cd), lambda i: (0, 0))],
        out_specs=[pl.BlockSpec((tm, cd), lambda i: (i, 0)),
                   pl.BlockSpec((CONV_WIDTH, cd), lambda i: (0, 0)),
                   pl.BlockSpec((1, cd), lambda i: (0, 0))],
        out_shape=[SDS((t, cd), F32), SDS((CONV_WIDTH, cd), F32), SDS((1, cd), F32)],
        sem=("arbitrary",), comm=comm)


def conv_bwd_b(dpre, conv_w, dproj, col_block, tm):
    t, cd = dpre.shape
    nh = tm // CONV_HALO
    nt = t // tm

    def body(c_ref, n_ref, w_ref, alias_ref, o_ref):
        i = pl.program_id(0)
        n = tm + CONV_HALO
        for cols in _lane_chunks(cd):
            cur = c_ref[:, cols]
            nxt = jnp.where(i < nt - 1, n_ref[:, cols], 0.0)
            ext = jnp.concatenate([cur, nxt], axis=0)
            acc = w_ref[CONV_WIDTH - 1:CONV_WIDTH, cols] * cur
            for k in range(CONV_WIDTH - 1):
                acc = acc + w_ref[k:k + 1, cols] * pltpu.roll(ext, n - (CONV_WIDTH - 1 - k), 0)[:tm]
            o_ref[:, cols] = acc.astype(BF16)

    return pl.pallas_call(
        body, name="conv_bwd_b", grid=(nt,),
        in_specs=[pl.BlockSpec((tm, cd), lambda i: (i, 0)),
                  pl.BlockSpec((CONV_HALO, cd), lambda i: (jnp.minimum((i + 1) * nh, t // CONV_HALO - 1), 0)),
                  pl.BlockSpec((CONV_WIDTH, cd), lambda i: (0, 0)),
                  pl.BlockSpec(memory_space=pl.ANY)],
        out_specs=pl.BlockSpec((tm, cd), lambda i: (i, col_block)),
        out_shape=SDS(dproj.shape, dproj.dtype),
        input_output_aliases={3: 0},
        compiler_params=_params(("arbitrary",)),
    )(dpre, dpre, conv_w, dproj)


def _softplus(v):
    return jnp.maximum(v, 0.0) + jnp.log(1.0 + jnp.exp(-jnp.abs(v)))


def _ssd_chunk_terms(dtr_ref, bias_ref, a_ref, n_heads):
    q = SSD_CHUNK
    lane = lax.broadcasted_iota(jnp.int32, (1, LANES), 1)
    pre = dtr_ref[...] + bias_ref[...]
    dt = jnp.where(lane < n_heads, _softplus(pre), 0.0)
    a = jnp.where(lane < n_heads, -jnp.exp(a_ref[...]), 0.0)
    row = lax.broadcasted_iota(jnp.int32, (q, q), 0)
    col = lax.broadcasted_iota(jnp.int32, (q, q), 1)
    causal = row >= col
    acs = jnp.dot(causal.astype(F32), dt * a, precision=HIGHEST, preferred_element_type=F32)
    last = acs[q - 1:q, :]
    return dict(pre=pre, dt=dt, a=a, acs=acs, acs_t=acs.T, eacs=jnp.exp(acs), dstate=jnp.exp(last - acs),
                cdec=jnp.exp(last), causal=causal, diag=row == col, lane=lane)


_TERM_FIELDS = ("pre", "dt", "acs", "acs_t", "eacs", "dstate", "cdec")


def _prefetched_terms(step, dtr_ref, dtn_ref, bias_ref, a_ref, n_heads, terms_ref):
    q = SSD_CHUNK

    def store(slot, tm_):
        for f, name in enumerate(_TERM_FIELDS):
            terms_ref[slot, f] = jnp.broadcast_to(tm_[name], (q, LANES))

    @pl.when(step == 0)
    def _():
        store(0, _ssd_chunk_terms(dtr_ref, bias_ref, a_ref, n_heads))

    slot = lax.rem(step, 2)
    nxt = _ssd_chunk_terms(dtn_ref, bias_ref, a_ref, n_heads)
    tm_ = dict(nxt, **{name: terms_ref[slot, f] for f, name in enumerate(_TERM_FIELDS)})
    tm_["cdec"] = tm_["cdec"][0:1]
    return tm_, lambda: store(1 - slot, nxt)


def _pair_cols(lo, v, h):
    if v.shape[0] < 8:
        return jnp.where(lo, v[:, h:h + 1], v[:, h + 1:h + 2])
    idx = jnp.broadcast_to(jnp.where(lo, h, h + 1).astype(jnp.int32), v.shape)
    return jnp.take_along_axis(v, idx, axis=1, mode="promise_in_bounds")


def _pair_decay(tm_, cb, h):
    l0 = jnp.exp(jnp.where(tm_["causal"], tm_["acs"][:, h:h + 1] - tm_["acs_t"][h:h + 1, :], -jnp.inf))
    l1 = jnp.exp(jnp.where(tm_["causal"], tm_["acs"][:, h + 1:h + 2] - tm_["acs_t"][h + 1:h + 2, :], -jnp.inf))
    return l0, l1, jnp.concatenate([cb * l0, cb * l1], axis=1)


def _pair_decay_t(tm_, cbt, h):
    upper = jnp.logical_not(tm_["causal"]) | tm_["diag"]
    t0 = jnp.exp(jnp.where(upper, tm_["acs_t"][h:h + 1, :] - tm_["acs"][:, h:h + 1], -jnp.inf))
    t1 = jnp.exp(jnp.where(upper, tm_["acs_t"][h + 1:h + 2, :] - tm_["acs"][:, h + 1:h + 2], -jnp.inf))
    return jnp.concatenate([cbt * t0, cbt * t1], axis=0).astype(BF16)


def _block_diag(lo, xdt):
    return jnp.concatenate([jnp.where(lo, xdt, 0.0), jnp.where(lo, 0.0, xdt)], axis=0).astype(BF16)


def ssd_fwd(xbc, proj, dt_raw, dt_bias, a_log, d_full, norm_w, mixed, z_block, n_heads, comm=None):
    t = xbc.shape[0]
    q = SSD_CHUNK
    gn = SSD_GROUPS * SSD_STATE
    sw = n_heads * SSD_HEAD_DIM
    gw = sw // SSD_GROUPS
    n_pairs = n_heads // 2
    pairs_per_group = n_pairs // SSD_GROUPS
    nc = t // q
    bblk = sw // gn

    def body(xs_ref, b_ref, c_ref, dtr_ref, dtn_ref, bias_ref, a_ref, z_ref, dsk_ref, nw_ref, alias_ref,
             y_ref, sin_ref, m_ref, state, terms_ref):
        @pl.when(pl.program_id(0) == 0)
        def _():
            state[...] = jnp.zeros_like(state)

        tm_, keep_next = _prefetched_terms(pl.program_id(0), dtr_ref, dtn_ref, bias_ref, a_ref, n_heads, terms_ref)
        lo = tm_["lane"] < SSD_HEAD_DIM
        for g in range(SSD_GROUPS):
            gcols = slice(g * SSD_STATE, (g + 1) * SSD_STATE)
            bg = b_ref[:, gcols].astype(BF16)
            bg_t = b_ref[:, gcols].T.astype(BF16)
            cg = c_ref[:, gcols].astype(BF16)
            cb = lax.dot_general(cg, bg, NT, preferred_element_type=F32)
            for j in range(pairs_per_group):
                p = g * pairs_per_group + j
                h = 2 * p
                pcols = slice(p * LANES, (p + 1) * LANES)
                _, _, mcat = _pair_decay(tm_, cb, h)
                xdt = xs_ref[:, pcols] * _pair_cols(lo, tm_["dt"], h)
                ydiag = jnp.dot(mcat.astype(BF16), _block_diag(lo, xdt), preferred_element_type=F32)
                st = state[p]
                sin_ref[0, p] = st
                yoff = jnp.dot(cg, st.astype(BF16), preferred_element_type=F32) * _pair_cols(lo, tm_["eacs"], h)
                y_ref[:, pcols] = ydiag + yoff
                xw = (xdt * _pair_cols(lo, tm_["dstate"], h)).astype(BF16)
                state[p] = st * _pair_cols(lo, tm_["cdec"], h) + jnp.dot(bg_t, xw, preferred_element_type=F32)
        keep_next()
        for g in range(SSD_GROUPS):
            cols = slice(g * gw, (g + 1) * gw)
            blk = (y_ref[:, cols] + dsk_ref[:, cols] * xs_ref[:, cols]) * _silu(z_ref[:, cols])
            r = lax.rsqrt(jnp.mean(blk * blk, axis=-1, keepdims=True) + NORM_EPS)
            m_ref[:, cols] = (blk * r * nw_ref[:, cols]).astype(BF16)

    vec = pl.BlockSpec((1, LANES), lambda c: (0, 0))
    wide = pl.BlockSpec((1, sw), lambda c: (0, 0))
    return _call(
        body, (xbc, xbc, xbc, dt_raw, dt_raw, dt_bias, a_log, proj, d_full, norm_w, mixed), name="ssd_fwd", grid=(nc,),
        in_specs=[pl.BlockSpec((q, sw), lambda c: (c, 0)),
                  pl.BlockSpec((q, gn), lambda c: (c, bblk)),
                  pl.BlockSpec((q, gn), lambda c: (c, bblk + 1)),
                  pl.BlockSpec((q, LANES), lambda c: (c, 0)),
                  pl.BlockSpec((q, LANES), lambda c: (jnp.minimum(c + 1, nc - 1), 0)), vec, vec,
                  pl.BlockSpec((q, sw), lambda c: (c, z_block)), wide, wide, pl.BlockSpec(memory_space=pl.ANY)],
        out_specs=[pl.BlockSpec((q, sw), lambda c: (c, 0)),
                   pl.BlockSpec((1, n_pairs, SSD_STATE, LANES), lambda c: (c, 0, 0, 0)),
                   pl.BlockSpec((q, sw), lambda c: (c, 1))],
        out_shape=[SDS((t, sw), F32), SDS((nc, n_pairs, SSD_STATE, LANES), F32), SDS(mixed.shape, mixed.dtype)],
        scratch_shapes=[pltpu.VMEM((n_pairs, SSD_STATE, LANES), F32), pltpu.VMEM((2, len(_TERM_FIELDS), q, LANES), F32)],
        sem=("arbitrary",), comm=comm, aliases={10: 2})


def ssd_bwd(dmixed, y, xbc, proj, dt_raw, dt_bias, a_log, d_full, norm_w, s_in, dproj, z_block, n_heads, comm=None):
    t = xbc.shape[0]
    q = SSD_CHUNK
    gn = SSD_GROUPS * SSD_STATE
    sw = n_heads * SSD_HEAD_DIM
    gw = sw // SSD_GROUPS
    n_pairs = n_heads // 2
    pairs_per_group = n_pairs // SSD_GROUPS
    nc = t // q
    bblk = sw // gn

    def body(d3_ref, y_ref, xs_ref, b_ref, c_ref, dtr_ref, dtn_ref, bias_ref, a_ref, dsk_ref, sin_ref, z_ref, nw_ref, alias_ref,
             dxs_ref, db_ref, dc_ref, ddtr_ref, dbias_ref, dalog_ref, dz_ref, dnw_ref, dd_ref,
             dstate, tbuf, xbuf, rbuf, acc_a, acc_b, sel_ref, terms_ref, dy_ref, acc_d):
        i = pl.program_id(0)

        @pl.when(i == 0)
        def _():
            dstate[...] = jnp.zeros_like(dstate)
            rbuf[...] = jnp.zeros_like(rbuf)
            acc_a[...] = jnp.zeros_like(acc_a)
            acc_b[...] = jnp.zeros_like(acc_b)
            acc_d[...] = jnp.zeros_like(acc_d)
            dnw_ref[...] = jnp.zeros_like(dnw_ref)
            sel_ref[...] = _head_selector(sw, SSD_HEAD_DIM)

        for g in range(SSD_GROUPS):
            cols = slice(g * gw, (g + 1) * gw)
            xs = xs_ref[:, cols]
            zv = z_ref[:, cols]
            y1 = y_ref[:, cols] + dsk_ref[:, cols] * xs
            sz = _silu(zv)
            blk = y1 * sz
            r = lax.rsqrt(jnp.mean(blk * blk, axis=-1, keepdims=True) + NORM_EPS)
            n = blk * r
            dg = d3_ref[:, cols]
            dnw_ref[:, cols] += jnp.sum(dg * n, axis=0, keepdims=True)
            dy2 = _norm_bwd(dg * nw_ref[:, cols], n, r)
            dz_ref[:, cols] = (dy2 * y1 * _dsilu(zv)).astype(BF16)
            dy1 = dy2 * sz
            dy_ref[:, cols] = dy1
            acc_d[0:1, cols] += jnp.sum(dy1 * xs, axis=0, keepdims=True)

        tm_, keep_next = _prefetched_terms(i, dtr_ref, dtn_ref, bias_ref, a_ref, n_heads, terms_ref)
        lane = tm_["lane"]
        lo = lane < SSD_HEAD_DIM
        head_row = lax.broadcasted_iota(jnp.int32, (LANES, 1), 0)
        rows = jnp.zeros((q, LANES), F32)
        cols_t = jnp.zeros((LANES, q), F32)
        for g in range(SSD_GROUPS):
            gcols = slice(g * SSD_STATE, (g + 1) * SSD_STATE)
            bg = b_ref[:, gcols].astype(BF16)
            cg = c_ref[:, gcols].astype(BF16)
            cg_t = c_ref[:, gcols].T.astype(BF16)
            cb = lax.dot_general(cg, bg, NT, preferred_element_type=F32)
            cbt = lax.dot_general(bg, cg, NT, preferred_element_type=F32)
            dcb = jnp.zeros((q, q), F32)
            db_acc = jnp.zeros((q, SSD_STATE), F32)
            dc_acc = jnp.zeros((q, SSD_STATE), F32)
            for j in range(pairs_per_group):
                p = g * pairs_per_group + j
                h = 2 * p
                pcols = slice(p * LANES, (p + 1) * LANES)
                l0, l1, mcat = _pair_decay(tm_, cb, h)
                xp = xs_ref[:, pcols]
                dtp = _pair_cols(lo, tm_["dt"], h)
                xdt = xp * dtp
                xbd = _block_diag(lo, xdt)
                dyp = dy_ref[:, pcols]
                dyb = dyp.astype(BF16)
                dsb = _pair_cols(lo, tm_["dstate"], h)
                cdr = _pair_cols(lo, tm_["cdec"], h)
                eb = _pair_cols(lo, tm_["eacs"], h)
                st = sin_ref[0, p]
                stb = st.astype(BF16)
                dst = dstate[p]
                dstb = dst.astype(BF16)
                dye = (dyp * eb).astype(BF16)
                both = jnp.dot(_pair_decay_t(tm_, cbt, h), dyb, preferred_element_type=F32)
                dx_state = jnp.dot(bg, dstb, preferred_element_type=F32) * dsb
                dxdt = jnp.where(lo, both[:q], both[q:]) + dx_state
                dmcat = lax.dot_general(dyb, xbd, NT, preferred_element_type=F32)
                dcb = dcb + dmcat[:, :q] * l0 + dmcat[:, q:] * l1
                dseg = dmcat * mcat
                csum = jnp.sum(dseg, axis=0, keepdims=True)
                rows = (rows + jnp.where(lane == h, jnp.sum(dseg[:, :q], axis=1, keepdims=True), 0.0)
                        + jnp.where(lane == h + 1, jnp.sum(dseg[:, q:], axis=1, keepdims=True), 0.0))
                cols_t = (cols_t + jnp.where(head_row == h, csum[:, :q], 0.0)
                          + jnp.where(head_row == h + 1, csum[:, q:], 0.0))
                dc_acc = dc_acc + lax.dot_general(dye, stb, NT, preferred_element_type=F32)
                db_acc = db_acc + lax.dot_general((xdt * dsb).astype(BF16), dstb, NT, preferred_element_type=F32)
                yoff = jnp.dot(cg, stb, preferred_element_type=F32) * eb
                tbuf[:, pcols] = dyp * yoff - xdt * dx_state
                xbuf[:, pcols] = dxdt * xp
                rbuf[0:1, pcols] = (jnp.sum(xdt * dx_state, axis=0, keepdims=True)
                                    + cdr * jnp.sum(dst * st, axis=0, keepdims=True))
                dxs_ref[:, pcols] = dxdt * dtp + dyp * dsk_ref[:, pcols]
                dstate[p] = dst * cdr + jnp.dot(cg_t, dye, preferred_element_type=F32)
            dcbb = dcb.astype(BF16)
            dc_ref[:, gcols] = dc_acc + jnp.dot(dcbb, bg, preferred_element_type=F32)
            db_ref[:, gcols] = db_acc + lax.dot_general(dcbb, cg, TN, preferred_element_type=F32)

        sel = sel_ref[...]
        dacs = rows - cols_t.T + _split_dot(tbuf[...], sel)
        carry = _split_dot(rbuf[...], sel)[0:1]
        anti = jnp.logical_not(tm_["causal"]) | tm_["diag"]
        da = jnp.dot(anti.astype(F32), dacs, precision=HIGHEST, preferred_element_type=F32) + carry
        ddt = da * tm_["a"] + _split_dot(xbuf[...], sel)
        ddtr = jnp.where(tm_["lane"] < n_heads, ddt * jax.nn.sigmoid(tm_["pre"]), 0.0)
        ddtr_ref[...] = ddtr.astype(BF16)
        acc_b[...] += jnp.sum(ddtr, axis=0, keepdims=True)
        acc_a[...] += jnp.sum(da * tm_["dt"], axis=0, keepdims=True)
        keep_next()

        @pl.when(i == nc - 1)
        def _():
            dbias_ref[...] = acc_b[...]
            dalog_ref[...] = acc_a[...] * tm_["a"]
            dd_ref[...] = _split_dot(acc_d[...], sel)[0:1]

    vec = pl.BlockSpec((1, LANES), lambda i: (0, 0))
    full = pl.BlockSpec((1, sw), lambda i: (0, 0))
    wide = pl.BlockSpec((q, sw), lambda i: (nc - 1 - i, 0))
    return _call(
        body, (dmixed, y, xbc, xbc, xbc, dt_raw, dt_raw, dt_bias, a_log, d_full, s_in, proj, norm_w, dproj),
        name="ssd_bwd", grid=(nc,),
        in_specs=[pl.BlockSpec((q, sw), lambda i: (nc - 1 - i, 1)), wide, wide,
                  pl.BlockSpec((q, gn), lambda i: (nc - 1 - i, bblk)),
                  pl.BlockSpec((q, gn), lambda i: (nc - 1 - i, bblk + 1)),
                  pl.BlockSpec((q, LANES), lambda i: (nc - 1 - i, 0)),
                  pl.BlockSpec((q, LANES), lambda i: (jnp.maximum(nc - 2 - i, 0), 0)), vec, vec, full,
                  pl.BlockSpec((1, n_pairs, SSD_STATE, LANES), lambda i: (nc - 1 - i, 0, 0, 0)),
                  pl.BlockSpec((q, sw), lambda i: (nc - 1 - i, z_block)), full, pl.BlockSpec(memory_space=pl.ANY)],
        out_specs=[wide, pl.BlockSpec((q, gn), lambda i: (nc - 1 - i, 0)), pl.BlockSpec((q, gn), lambda i: (nc - 1 - i, 0)),
                   pl.BlockSpec((q, LANES), lambda i: (nc - 1 - i, 0)), vec, vec,
                   pl.BlockSpec((q, sw), lambda i: (nc - 1 - i, z_block)), full, vec],
        out_shape=[SDS((t, sw), F32), SDS((t, gn), F32), SDS((t, gn), F32), SDS((t, LANES), BF16),
                   SDS((1, LANES), F32), SDS((1, LANES), F32), SDS(dproj.shape, dproj.dtype), SDS((1, sw), F32),
                   SDS((1, LANES), F32)],
        scratch_shapes=[pltpu.VMEM((n_pairs, SSD_STATE, LANES), F32), pltpu.VMEM((q, sw), F32), pltpu.VMEM((q, sw), F32),
                        pltpu.VMEM((8, sw), F32), pltpu.VMEM((1, LANES), F32), pltpu.VMEM((1, LANES), F32),
                        pltpu.VMEM((sw, LANES), BF16), pltpu.VMEM((2, len(_TERM_FIELDS), q, LANES), F32),
                        pltpu.VMEM((q, sw), F32), pltpu.VMEM((8, sw), F32)],
        sem=("arbitrary",), comm=comm, aliases={13: 6})


GATE_BLOCK = 1
Z_BLOCK = 2
CONV_BLOCK = 2


def _tiles(t):
    mm = dict(in_proj=(min(1024, t), 1024), dt_proj=(min(512, t), LANES), out_proj=(min(512, t), 1024),
              d_mixed=(min(512, t), 2048), dh=(min(512, t), 3072), dw_out=(512, 1024), dw_main=(1024, 1024),
              dw_dt=(512, LANES))
    return min(512, t), mm


def _place():
    x, y, c = lax.axis_index("x"), lax.axis_index("y"), lax.axis_index("c")
    return x, y, c, [(1 - x, y), (x, 1 - y), (1 - x, 1 - y)]


def gather_spread(shards, layer, rows=None, carry=None):
    n = len(shards)

    def make(ins, outs, ss, rs):
        x, y, c, chips = _place()
        mine = 4 * x + 2 * y + c
        peers = [(x, y, 1 - c)] + [(px, py, c) for px, py in chips]
        sends, locals_, arrivals = [], [], []
        for a in range(n):
            def place(ref, idx):
                return ref.at[idx] if rows is None else ref.at[idx, pl.ds(rows[0], rows[1])]

            src = place(ins[a], layer)
            locals_.append(pltpu.make_async_copy(src, place(outs[a], mine), ss.at[5 * a + 4]))
            for j, (px, py, pc) in enumerate(peers):
                sends.append(_remote(src, place(outs[a], mine), ss, rs, 5 * a + j, (px, py, pc)))
                arrivals.append(_remote(src, place(outs[a], 4 * px + 2 * py + pc), ss, rs, 5 * a + j, (px, py, pc)))
        return sends, locals_, arrivals

    return Comm(list(shards) + list(carry or []), [SDS((N_DEV,) + s.shape[1:], s.dtype) for s in shards],
                {n + a: a for a in range(n)} if carry else {}, 5 * n, make)


def gather_pass_on(gathered):
    def make(ins, outs, ss, rs):
        x, y, c, chips = _place()
        sends, arrivals = [], []
        for a in range(len(outs)):
            for j, (px, py) in enumerate(chips):
                blk, other = 4 * px + 2 * py + c, 4 * px + 2 * py + (1 - c)
                sends.append(_remote(outs[a].at[blk], outs[a].at[blk], ss, rs, 3 * a + j, (x, y, 1 - c)))
                arrivals.append(_remote(outs[a].at[other], outs[a].at[other], ss, rs, 3 * a + j, (x, y, 1 - c)))
        return sends, [], arrivals

    return Comm(gathered, [SDS(g.shape, g.dtype) for g in gathered], {a: a for a in range(len(gathered))},
                3 * len(gathered), make)


def sibling_swap(sends_):
    def make(ins, outs, ss, rs):
        x, y, c, _ = _place()
        cps = [_remote(ins[a], outs[a], ss, rs, a, (x, y, 1 - c)) for a in range(len(ins))]
        return cps, [], cps

    return Comm(sends_, [SDS(s.shape, s.dtype) for s in sends_], {}, len(sends_), make)


def chips_scatter(slabs, rows=None, carry=None):
    n = len(slabs)

    def make(ins, outs, ss, rs):
        x, y, c, chips = _place()
        mychip = 2 * x + y
        sends, arrivals = [], []

        def part(ref, slot):
            return ref.at[slot] if rows is None else ref.at[slot, pl.ds(rows[0], rows[1])]

        for a in range(n):
            for j, (px, py) in enumerate(chips):
                to_there = lax.rem(2 * px + py - mychip + 4, 4) - 1
                from_here = lax.rem(mychip - 2 * px - py + 4, 4) - 1
                sends.append(_remote(part(ins[a], to_there), part(outs[a], from_here), ss, rs, 3 * a + j, (px, py, c)))
                arrivals.append(_remote(part(ins[a], to_there), part(outs[a], to_there), ss, rs, 3 * a + j, (px, py, c)))
        return sends, [], arrivals

    return Comm(list(slabs) + list(carry or []), [SDS(s.shape, s.dtype) for s in slabs],
                {n + a: a for a in range(n)} if carry else {}, 3 * n, make)


def comm_only(comm, name):
    def body():
        pass

    return _call(body, (), name=name, grid=(), in_specs=[], out_specs=[], out_shape=[], comm=comm)[1]


W_IN_GATHER_EIGHTHS = (3, 1, 2, 2)


def layer_fwd(x, p, layer=0, shards=None, finish=None, nxt=False, first=False, normed=None, next_pre_w=None):
    t = x.shape[0]
    tm, mm = _tiles(t)
    n_heads = p["d_full"].shape[1] // SSD_HEAD_DIM
    travel = shards is not None
    nxt, first = nxt and travel, first and travel
    rows = shards["in"][0].shape[1] if travel else 0
    cuts = [0]
    for eighths in W_IN_GATHER_EIGHTHS:
        cuts.append(cuts[-1] + rows * eighths // 8)
    assert cuts[-1] == rows

    def next_w_in(part, carry):
        return gather_spread(shards["in"], layer + 1, rows=(cuts[part], cuts[part + 1] - cuts[part]), carry=carry) if nxt else None

    h, r_pre = normed if normed is not None else rms_fwd(x, p["pre_w"], tm)
    proj, got = mm_nn(h, p["w_main"], F32, *mm["in_proj"], "in_proj", merge_comms([
        gather_spread(shards["small"], layer) if first else None,
        gather_spread(shards["out"], layer) if travel else None, next_w_in(0, None)]))
    n_small = len(shards["small"]) if first else 0
    got_small, got_out, got_in = got[:n_small], got[n_small:n_small + 1], got[n_small + 1:]
    dt_raw, got_small = mm_nn(h, p["w_dt"], F32, *mm["dt_proj"], "dt_proj", gather_pass_on(got_small) if first else None)
    if first:
        p = dict(p, **finish["small"](got_small))
    mixed = pool_fwd(proj, p["mixw"], p["pscale"], tm)
    xbc, got_in = conv_fwd(proj, p["conv_w"], p["conv_b"], CONV_BLOCK, tm, next_w_in(1, got_in))
    (y, s_in, mixed), got = ssd_fwd(xbc, proj, dt_raw, p["dt_bias"], p["a_log"], p["d_full"], p["norm_w"], mixed, Z_BLOCK,
                                    n_heads, merge_comms([gather_pass_on(got_out) if travel else None, next_w_in(2, got_in)]))
    if travel:
        p = dict(p, **finish["out"](got[:1]))
    out, got = mm_nn(mixed, p["w_out"], F32, *mm["out_proj"], "out_proj",
                     merge_comms([next_w_in(3, got[1:]), gather_spread(shards["small"], layer + 1)]) if nxt else None)
    (x_next, r_post, *next_normed), gathered = post_fwd(out, x, p["post_w"], tm, gather_pass_on(got) if nxt else None,
                                                        next_pre_w)
    return (x_next, next_normed or None), dict(x=x, h=h, r_pre=r_pre, proj=proj, dt_raw=dt_raw, xbc=xbc, y=y, s_in=s_in,
                                               mixed=mixed, out=out, r_post=r_post), gathered, p


def _pair_sums(own, got):
    return [pair_sum(o, r, min(256, o.shape[1]), "pair_sum") for o, r in zip(own, got)]


def layer_bwd(g, s, p, split_in, split_rest, pending=None, last=False, post=None, below=None):
    t = g.shape[0]
    tm, mm = _tiles(t)
    d = g.shape[1]
    n_heads = p["d_full"].shape[1] // SSD_HEAD_DIM
    d_out, d_post = post if post is not None else post_bwd(g, s["out"], s["r_post"], p["post_w"], tm)
    dmixed, got_sib = mm_nt(d_out, p["w_out"], F32, *mm["d_mixed"], d, "d_mixed", sibling_swap(pending[1]) if pending else None)
    chip_sums = _pair_sums(pending[0], got_sib) if pending else []
    rows_in = chip_sums[0].shape[1] if pending else 0
    early = rows_in * 3 // 8 if last else 0
    dw_out, got_early = mm_tn(s["mixed"], d_out, *mm["dw_out"], "dw_out",
                              chips_scatter(chip_sums[:1], rows=(0, early)) if pending and last else None)
    tm_heavy = min(tm, HEAVY_ROW_TILE)
    dproj, dq, d_pscale, d_mixw = pool_bwd_a(dmixed, s["proj"], p["mixw"], p["pscale"], tm_heavy)
    dproj = pool_bwd_b(dq, dproj, tm)
    own_rest, send_rest = split_rest(dw_out, d_mixw)
    cut_in = rows_in if last else rows_in // 2
    (dxs, db, dc, ddtr, d_dtb, d_alog, dproj, d_norm, d_dskip), got = ssd_bwd(
        dmixed, s["y"], s["xbc"], s["proj"], s["dt_raw"], p["dt_bias"], p["a_log"], p["d_full"], p["norm_w"], s["s_in"],
        dproj, Z_BLOCK, n_heads,
        merge_comms([chips_scatter(chip_sums[:1], rows=(early, cut_in - early), carry=got_early or None) if pending else None,
                     sibling_swap(send_rest) if last else None]))
    got_first, my_sib_rest = (got[:1], got[1:]) if pending else ([], got)
    rest_comm = chips_scatter(chip_sums[1:]) if pending else None
    (dpre, d_convw, d_convb), got_rest = conv_bwd_a(dxs, db, dc, s["proj"], p["conv_w"], p["conv_b"], CONV_BLOCK, tm_heavy,
                                                    rest_comm if last else None)
    dproj = conv_bwd_b(dpre, p["conv_w"], dproj, CONV_BLOCK, tm)
    dw_main, got = mm_tn(s["h"], dproj, *mm["dw_main"], "dw_main",
                         chips_scatter(_pair_sums(own_rest, my_sib_rest)) if last else rest_comm)
    got_rest, my_chips_rest = (got_rest, got) if last else (got, [])
    dw_dt, _ = mm_tn(s["h"], ddtr, *mm["dw_dt"], "dw_dt")
    own_in, send_in = split_in(dw_main, dw_dt)
    my_sib_in = comm_only(sibling_swap(send_in), "grads_to_sibling") if last else []
    if last:
        my_sums = _pair_sums(own_in, my_sib_in)
        rows_own = my_sums[0].shape[1]
        cut_own = rows_own * 13 // 16
        dh_comm = chips_scatter(my_sums, rows=(0, cut_own))
    else:
        dh_comm = chips_scatter(chip_sums[:1], rows=(cut_in, rows_in - cut_in), carry=got_first) if pending else None
    dh, got = mm_nt(dproj, p["w_main"], F32, mm["dh"][0], d, mm["dh"][1], "dh_main", dh_comm, extra=(ddtr, p["w_dt"]))
    my_chips_in, got_first = (got, got_first) if last else ([], got if pending else got_first)
    (gx, d_pre, *post_below), got = rms_bwd(
        dh, s["x"], s["r_pre"], p["pre_w"], g, tm,
        chips_scatter(my_sums, rows=(cut_own, rows_own - cut_own), carry=my_chips_in) if last else None, below)
    my_chips_in = got if last else my_chips_in
    small = dict(pre_w=d_pre, pscale=d_pscale, conv_w=d_convw, conv_b=d_convb, dt_bias=d_dtb, a_log=d_alog,
                 d_skip=d_dskip, norm_w=d_norm, post_w=d_post)
    done = [(got_sib, got_first + got_rest)] if pending else [None]
    if last:
        done.append((my_sib_in + my_sib_rest, my_chips_in + my_chips_rest))
    return (gx, post_below or None), small, (own_in + own_rest, send_in + send_rest), done


def _two_level_gather(x_refs, out_slots, send_sems, recv_sems, local_sems):
    x, y, c, chips = _place()
    me, sibling = (x, y, c), (x, y, 1 - c)
    n = len(x_refs)

    def copy(a, k, block, to, src=None):
        return pltpu.make_async_remote_copy(
            src_ref=out_slots[a](*block) if src is None else src, dst_ref=out_slots[a](*block),
            send_sem=send_sems.at[7 * a + k], recv_sem=recv_sems.at[7 * a + k], device_id=to, device_id_type=MESH)

    mine = [pltpu.make_async_copy(x_refs[a], out_slots[a](*me), local_sems.at[a]) for a in range(n)]
    for cp in mine:
        cp.start()
    first = []
    for a in range(n):
        first.append(copy(a, 0, me, sibling, src=x_refs[a]))
        first += [copy(a, 1 + j, me, (*chip, c), src=x_refs[a]) for j, chip in enumerate(chips)]
    for cp in first:
        cp.start()
    passed = []
    for j, chip in enumerate(chips):
        for a in range(n):
            copy(a, 1 + j, (*chip, c), me).wait_recv()
            fwd = copy(a, 4 + j, (*chip, c), sibling)
            fwd.start()
            passed.append(fwd)
    for a in range(n):
        copy(a, 0, sibling, me).wait_recv()
        for j, chip in enumerate(chips):
            copy(a, 4 + j, (*chip, 1 - c), me).wait_recv()
    for cp in first + passed:
        cp.wait_send()
    for cp in mine:
        cp.wait()


def all_gather_hbm(shards, name):
    n = len(shards)

    def body(*refs):
        x_refs, out_refs = refs[:n], refs[n:2 * n]
        send_sems, recv_sems, local_sems = refs[2 * n:]
        slots = [lambda px, py, pc, o=o: o.at[:, 4 * px + 2 * py + pc] for o in out_refs]
        _two_level_gather(x_refs, slots, send_sems, recv_sems, local_sems)

    hbm = pl.BlockSpec(memory_space=pl.ANY)
    return pl.pallas_call(
        body, name=name,
        out_shape=[SDS((s.shape[0], N_DEV) + s.shape[1:], s.dtype) for s in shards],
        in_specs=[hbm] * n, out_specs=[hbm] * n,
        scratch_shapes=[pltpu.SemaphoreType.DMA((7 * n,)), pltpu.SemaphoreType.DMA((7 * n,)), pltpu.SemaphoreType.DMA((n,))],
    )(*shards)


def all_gather_vmem(block, name):
    r, c_ = block.shape

    def body(x_ref, out_ref, send_sems, recv_sems, local_sems):
        _two_level_gather([x_ref], [lambda px, py, pc: out_ref.at[4 * px + 2 * py + pc]], send_sems, recv_sems, local_sems)

    return pl.pallas_call(
        body, name=name, out_shape=SDS((N_DEV, r, c_), block.dtype),
        in_specs=[pl.BlockSpec(memory_space=pltpu.VMEM)], out_specs=pl.BlockSpec(memory_space=pltpu.VMEM),
        scratch_shapes=[pltpu.SemaphoreType.DMA((7,)), pltpu.SemaphoreType.DMA((7,)), pltpu.SemaphoreType.DMA((1,))],
        compiler_params=_params(),
    )(block)


def _block_tiles(cols):
    base = [(cols * i) // LANES for i in range(N_DEV)]
    ends = [-((-cols * (i + 1)) // LANES) for i in range(N_DEV)]
    return base, ends, max(e - b for b, e in zip(base, ends))


def _my_lane_offset(cols):
    me = 4 * lax.axis_index("x") + 2 * lax.axis_index("y") + lax.axis_index("c")
    return lax.rem(cols * me, LANES)


def shift_cast(w, tr):
    nl, r, cols = w.shape
    width = _block_tiles(cols)[2] * LANES

    def body(x_ref, o_ref, pad):
        pad[:, width - LANES:] = jnp.zeros((tr, LANES), F32)
        pad[:, :cols] = x_ref[...]
        o_ref[...] = pltpu.roll(pad[...], _my_lane_offset(cols), 1).astype(BF16)

    assert width - LANES <= cols
    return pl.pallas_call(
        body, name="shift_cast", grid=(nl, r // tr),
        in_specs=[pl.BlockSpec((pl.Squeezed(), tr, cols), lambda l, i: (l, i, 0))],
        out_specs=pl.BlockSpec((pl.Squeezed(), tr, width), lambda l, i: (l, i, 0)),
        out_shape=SDS((nl, r, width), BF16), scratch_shapes=[pltpu.VMEM((tr, width), F32)],
        compiler_params=_params(("arbitrary", "arbitrary")))(w)


def assemble_w_in(blocks, cols, n_tail, tr):
    _, r, width = blocks.shape
    base, ends, _ = _block_tiles(cols)
    total = ends[-1]
    main_tiles = (N_DEV * cols - n_tail) // LANES
    assert main_tiles == total - 1 and (N_DEV * cols - n_tail) % LANES == 0

    def body(b_ref, main_ref, tail_ref):
        for tile in range(total):
            parts = [b_ref[i, :, (tile - base[i]) * LANES:(tile - base[i] + 1) * LANES]
                     for i in range(N_DEV) if base[i] <= tile < ends[i]]
            val = parts[0] if len(parts) == 1 else parts[0] + parts[1]
            if tile < main_tiles:
                main_ref[:, tile * LANES:(tile + 1) * LANES] = val
            else:
                tail_ref[...] = val

    return pl.pallas_call(
        body, name="assemble_w_in", grid=(r // tr,),
        in_specs=[pl.BlockSpec((N_DEV, tr, width), lambda i: (0, i, 0))],
        out_specs=[pl.BlockSpec((tr, main_tiles * LANES), lambda i: (i, 0)), pl.BlockSpec((tr, LANES), lambda i: (i, 0))],
        out_shape=[SDS((r, main_tiles * LANES), blocks.dtype), SDS((r, LANES), blocks.dtype)],
        compiler_params=_params(("arbitrary",)),
    )(blocks)


def grad_blocks(dw_main, dw_tail, cols, tr):
    r = dw_main.shape[0]
    base, _, tpb = _block_tiles(cols)
    width = tpb * LANES

    def body(m_ref, t_ref, own_ref, send_ref):
        cat = jnp.concatenate([m_ref[...], t_ref[...]], axis=1)
        south = lax.axis_index("c") == 0
        for k in range(N_DEV // 2):
            a = cat[:, base[2 * k] * LANES:base[2 * k] * LANES + width]
            b = cat[:, base[2 * k + 1] * LANES:base[2 * k + 1] * LANES + width]
            own_ref[k] = jnp.where(south, a, b)
            send_ref[k] = jnp.where(south, b, a).astype(BF16)

    return pl.pallas_call(
        body, name="grad_blocks", grid=(r // tr,),
        in_specs=[pl.BlockSpec((tr, dw_main.shape[1]), lambda i: (i, 0)), pl.BlockSpec((tr, LANES), lambda i: (i, 0))],
        out_specs=[pl.BlockSpec((N_DEV // 2, tr, width), lambda i: (0, i, 0))] * 2,
        out_shape=[SDS((N_DEV // 2, r, width), F32), SDS((N_DEV // 2, r, width), BF16)],
        compiler_params=_params(("arbitrary",)),
    )(dw_main, dw_tail)


def _adamw(w, g, m, v):
    m = ADAM_B1 * m + (1.0 - ADAM_B1) * g
    v = ADAM_B2 * v + (1.0 - ADAM_B2) * jnp.square(g)
    m_hat = m / (1.0 - ADAM_B1 ** ADAM_STEP)
    v_hat = v / (1.0 - ADAM_B2 ** ADAM_STEP)
    delta = -ADAM_LR * (m_hat / (jnp.sqrt(v_hat) + ADAM_EPS) + ADAM_WD * w)
    return delta, m, v


def _my_chip():
    return 2 * lax.axis_index("x") + lax.axis_index("y")


def pair_sum(own, got, tr, name):
    k, r, c_ = own.shape
    others = lax.rem(_my_chip() + 1 + jnp.arange(k - 1, dtype=jnp.int32), k)

    def body(others_ref, a_ref, b_ref, o_ref):
        o_ref[...] = (a_ref[...] + b_ref[...].astype(F32)).astype(BF16)

    src = pl.BlockSpec((pl.Squeezed(), tr, c_), lambda s, i, oth: (oth[s], i, 0))
    return pl.pallas_call(
        body, name=name, out_shape=SDS((k - 1, r, c_), BF16),
        grid_spec=pltpu.PrefetchScalarGridSpec(
            num_scalar_prefetch=1, grid=(k - 1, r // tr), in_specs=[src, src],
            out_specs=pl.BlockSpec((pl.Squeezed(), tr, c_), lambda s, i, oth: (s, i, 0))),
        compiler_params=_params(("arbitrary", "arbitrary")),
    )(others, own, got)


def reduce_adam(own, got_sibling, got_chips, w, m, v, prev, layer, tr, name, shifted=False):
    nl, r, cols = w.shape
    c_ = own.shape[-1]
    n_scratch = 1 if shifted else 0
    chip = jnp.reshape(_my_chip(), (1,)).astype(jnp.int32)

    def body(chip_ref, own_ref, sib_ref, c0_ref, c1_ref, c2_ref, w_ref, m_ref, v_ref, *rest):
        g_ref, d_ref, nm_ref, nv_ref = rest[len(rest) - n_scratch - 4:len(rest) - n_scratch]
        g = (own_ref[...] + sib_ref[...].astype(F32) + c0_ref[...].astype(F32) + c1_ref[...].astype(F32)
             + c2_ref[...].astype(F32))
        if shifted:
            rest[-1][...] = pltpu.roll(g, c_ - _my_lane_offset(cols), 1)
            g = rest[-1][:, :cols]
        delta, nm, nv = _adamw(w_ref[...], g, m_ref[...], v_ref[...])
        g_ref[...] = g
        d_ref[...] = delta
        nm_ref[...] = nm
        nv_ref[...] = nv

    mine = pl.BlockSpec((pl.Squeezed(), tr, c_), lambda i, ch: (ch[0], i, 0))
    lay = pl.BlockSpec((pl.Squeezed(), tr, cols), lambda i, ch: (layer, i, 0))
    chips = [pl.BlockSpec((pl.Squeezed(), tr, c_), lambda i, ch, s=s: (s, i, 0)) for s in range(3)]
    in_specs = [mine, mine] + chips + [lay, lay, lay]
    args = [chip, own, got_sibling, got_chips, got_chips, got_chips, w, m, v]
    aliases = {}
    if prev is not None:
        in_specs += [pl.BlockSpec(memory_space=pl.ANY)] * 4
        aliases = {len(args) + k: k for k in range(4)}
        args += list(prev)
    return pl.pallas_call(
        body, name=name, out_shape=[SDS((nl, r, cols), F32)] * 4, input_output_aliases=aliases,
        grid_spec=pltpu.PrefetchScalarGridSpec(
            num_scalar_prefetch=1, grid=(r // tr,), in_specs=in_specs, out_specs=[lay] * 4,
            scratch_shapes=[pltpu.VMEM((tr, c_), F32)] * n_scratch),
        compiler_params=_params(("arbitrary",)),
    )(*args)


def sum_devices(packs):
    n, r, c_ = packs.shape

    def body(p_ref, o_ref):
        acc = p_ref[0]
        for k in range(1, n):
            acc = acc + p_ref[k]
        o_ref[...] = acc

    return pl.pallas_call(body, name="sum_devices", out_shape=SDS((r, c_), F32), compiler_params=_params())(packs)


def adam_small(w, g, m, v):
    def body(w_ref, g_ref, m_ref, v_ref, d_ref, nm_ref, nv_ref):
        delta, nm, nv = _adamw(w_ref[...], g_ref[...], m_ref[...], v_ref[...])
        d_ref[...] = delta
        nm_ref[...] = nm
        nv_ref[...] = nv

    return pl.pallas_call(body, name="adam_small", out_shape=[SDS(w.shape, F32)] * 3, compiler_params=_params())(w, g, m, v)


SMALL = ("pre_norm_w", "pool_scale", "conv_b", "dt_bias", "a_log", "d_skip", "_pad", "ssd_norm_w", "post_norm_w", "conv_w")


def _pack(parts):
    flat = jnp.concatenate([parts[k] for k in SMALL], axis=1).reshape(-1, LANES)
    return jnp.pad(flat, ((0, (-flat.shape[0]) % 8), (0, 0)))


def _unpack(pack, sizes, nl):
    total = sum(sizes[k] for k in SMALL)
    flat = pack[: nl * total // LANES].reshape(nl, total)
    out, o = {}, 0
    for k in SMALL:
        out[k] = flat[:, o:o + sizes[k]]
        o += sizes[k]
    return out


def kernel(x, pre_norm_w, w_in, pool_mix_w, pool_scale, conv_w, conv_b, dt_bias, a_log, d_skip, ssd_norm_w, w_out, post_norm_w, loss_target, m_pre_norm_w, m_w_in, m_pool_mix_w, m_pool_scale, m_conv_w, m_conv_b, m_dt_bias, m_a_log, m_d_skip, m_ssd_norm_w, m_w_out, m_post_norm_w, v_pre_norm_w, v_w_in, v_pool_mix_w, v_pool_scale, v_conv_w, v_conv_b, v_dt_bias, v_a_log, v_d_skip, v_ssd_norm_w, v_w_out, v_post_norm_w):
    cx, cy, cc = lax.axis_index("x"), lax.axis_index("y"), lax.axis_index("c")
    me = 4 * cx + 2 * cy + cc
    mychip = 2 * cx + cy
    nl, d, cols = w_in.shape
    t = x.shape[1]
    n_heads = a_log.shape[1]
    sw = n_heads * SSD_HEAD_DIM
    pw = pool_scale.shape[1]
    cd = conv_b.shape[1]
    ng, gsh, gw = pool_mix_w.shape[1:]
    e_main = N_DEV * cols - n_heads
    assert x.shape[0] == 1 and pw == sw and cd == sw + 2 * SSD_GROUPS * SSD_STATE and e_main == 2 * pw + sw + cd
    assert 2 * pw + sw == CONV_BLOCK * cd and n_heads <= LANES and t % SSD_CHUNK == 0 and gsh * N_DEV == gw
    tm, _ = _tiles(t)

    shards = {"in": [shift_cast(w_in, tm)], "out": [w_out.astype(BF16)], "small": [pool_mix_w.astype(BF16), conv_w]}
    pad_h = ((0, 0), (0, LANES - n_heads))

    def params_a(l, g_in):
        w_main, w_dt = assemble_w_in(g_in, cols, n_heads, tm)
        return dict(pre_w=pre_norm_w[l:l + 1], w_main=w_main, w_dt=w_dt, pscale=pool_scale[l:l + 1], conv_b=conv_b[l:l + 1],
                    dt_bias=jnp.pad(dt_bias[l:l + 1], pad_h), a_log=jnp.pad(a_log[l:l + 1], pad_h),
                    d_full=jnp.repeat(d_skip[l:l + 1], SSD_HEAD_DIM, axis=1), norm_w=ssd_norm_w[l:l + 1],
                    post_w=post_norm_w[l:l + 1])

    finish = {"small": lambda got: dict(mixw=got[0].transpose(1, 0, 2, 3).reshape(ng, gw, gw),
                                        conv_w=got[1].transpose(1, 0, 2).reshape(CONV_WIDTH, cd)),
              "out": lambda got: dict(w_out=got[0].reshape(N_DEV * w_out.shape[1], d))}

    xs = x[0]
    saved, params = [], []
    p = params_a(0, all_gather_hbm([shards["in"][0][:1]], "gather_w_in")[0][0])
    normed = None
    for l in range(nl):
        (xs, normed), s, gathered, p = layer_fwd(xs, p, l, shards, finish, nxt=l + 1 < nl, first=l == 0, normed=normed,
                                                 next_pre_w=pre_norm_w[l + 1:l + 2] if l + 1 < nl else None)
        saved.append(s)
        params.append(p)
        if l + 1 < nl:
            p = dict(params_a(l + 1, gathered[0]), **finish["small"](gathered[1:]))
    loss_part, g = loss_grad(xs, loss_target[0], tm)
    loss = lax.psum(loss_part[0, 0], ("x", "y", "c"))

    big = {"w_in": (w_in, m_w_in, v_w_in), "w_out": (w_out, m_w_out, v_w_out),
           "pool_mix_w": tuple(a.reshape(nl, ng * gsh, gw) for a in (pool_mix_w, m_pool_mix_w, v_pool_mix_w))}
    names = list(big)
    big_out = {k: None for k in big}
    small_g = [None] * nl

    def apply(layer, own, got_sib, got_chips):
        for k, o, gs_, gc in zip(names, own, got_sib, got_chips):
            wk, mk, vk = big[k]
            big_out[k] = reduce_adam(o, gs_, gc, wk, mk, vk, big_out[k], layer, min(256, wk.shape[1]), "reduce_adam_" + k,
                                     shifted=(k == "w_in"))

    def split_in(dw_main, dw_dt):
        own, send = grad_blocks(dw_main, dw_dt, cols, min(128, d))
        return [own], [send]

    def split_rest(dw_out, d_mixw):
        halves = [lambda ci: lax.dynamic_index_in_dim(dw_out.reshape(4, 2, -1, d), ci, 1, keepdims=False),
                  lambda ci: lax.dynamic_index_in_dim(
                      d_mixw.reshape(ng, 4, 2, gsh, gw), ci, 2, keepdims=False).transpose(1, 0, 2, 3).reshape(4, ng * gsh, gw)]
        return [h(cc) for h in halves], [h(1 - cc).astype(BF16) for h in halves]

    pending = post = None
    for l in reversed(range(nl)):
        below = (saved[l - 1]["out"], saved[l - 1]["r_post"], params[l - 1]["post_w"]) if l > 0 else None
        (g, post), gr, mine, done = layer_bwd(g, saved[l], params[l], split_in, split_rest, pending, last=(l == 0),
                                              post=post, below=below)
        if pending is not None:
            apply(l + 1, pending[0], *done[0])
        if l == 0:
            apply(0, mine[0], *done[1])
        pending = mine
        small_g[l] = dict(pre_norm_w=gr["pre_w"], pool_scale=gr["pscale"], conv_b=gr["conv_b"], dt_bias=gr["dt_bias"][:, :n_heads],
                          a_log=gr["a_log"][:, :n_heads], d_skip=gr["d_skip"][:, :n_heads], _pad=jnp.zeros((1, LANES - 3 * n_heads), F32),
                          ssd_norm_w=gr["norm_w"], post_norm_w=gr["post_w"], conv_w=gr["conv_w"].reshape(1, CONV_WIDTH * cd))

    sizes = {k: small_g[0][k].shape[1] for k in SMALL}
    gsum = sum_devices(all_gather_vmem(_pack({k: jnp.concatenate([sg[k] for sg in small_g], axis=0) for k in SMALL}),
                                       "gather_small_grads"))
    gs = _unpack(gsum, sizes, nl)
    csh = conv_w.shape[2]
    gs["conv_w"] = lax.dynamic_slice_in_dim(gs["conv_w"].reshape(nl, CONV_WIDTH, cd), me * csh, csh, axis=2).reshape(nl, -1)
    lsizes = dict(sizes, conv_w=CONV_WIDTH * csh)
    zpad = jnp.zeros((nl, sizes["_pad"]), F32)

    def local(pre, scale, cb, dtb, al, dsk, nw, post, cw):
        return _pack(dict(pre_norm_w=pre, pool_scale=scale, conv_b=cb, dt_bias=dtb, a_log=al, d_skip=dsk, _pad=zpad,
                          ssd_norm_w=nw, post_norm_w=post, conv_w=cw.reshape(nl, -1)))

    wp = local(pre_norm_w, pool_scale, conv_b, dt_bias, a_log, d_skip, ssd_norm_w, post_norm_w, conv_w)
    mp = local(m_pre_norm_w, m_pool_scale, m_conv_b, m_dt_bias, m_a_log, m_d_skip, m_ssd_norm_w, m_post_norm_w, m_conv_w)
    vp = local(v_pre_norm_w, v_pool_scale, v_conv_b, v_dt_bias, v_a_log, v_d_skip, v_ssd_norm_w, v_post_norm_w, v_conv_w)
    small_out = [gs] + [_unpack(o, lsizes, nl) for o in adam_small(wp, _pack(gs), mp, vp)]

    def leaf(kind, name):
        if name in big:
            return big_out[name][kind].reshape(big[name][0].shape if name != "pool_mix_w" else pool_mix_w.shape)
        val = small_out[kind][name]
        return val.reshape(conv_w.shape) if name == "conv_w" else val

    order = ("pre_norm_w", "w_in", "pool_mix_w", "pool_scale", "conv_w", "conv_b", "dt_bias", "a_log", "d_skip",
             "ssd_norm_w", "w_out", "post_norm_w")
    return (loss, g[None]) + tuple(leaf(kind, name) for kind in range(4) for name in order)
```
